```python
import jax, jax.numpy as jnp
from jax import lax
import numpy as np

D_MODEL = 1024
BATCH = 2
SEQ = 8192
DEPTH = 2

GRID_W = 64
CTX_LEN = 256
NA_HEADS = 8
HEAD_DIM = 64
NA_WIDTH = NA_HEADS * HEAD_DIM
WIN_ROWS = 8
WIN_COLS = 16
F_GROUPS = 8
F_GROUP_DIM = 64
F_WIDTH = F_GROUPS * F_GROUP_DIM
MIX_WIDTH = NA_WIDTH + F_WIDTH
IN_WIDTH = 3 * NA_WIDTH + F_WIDTH
N_EXPERTS = 16
N_GROUPS = 4
EXPERTS_PER_GROUP = N_EXPERTS // N_GROUPS
TOP_K = 2
GROUP_SCORE_K = 2
D_EXPERT = 256
N_MOD = 6
DEEPNORM_ALPHA = (2.0 * DEPTH) ** 0.25
DEEPNORM_BETA = (8.0 * DEPTH) ** -0.25
LN_EPS = 1e-6

kernel_name = "hybrid_na_fnet_grouped_moe_deepnorm"


def layer_norm(x, g=None, b=None):
    xf = x.astype(jnp.float32)
    mu = jnp.mean(xf, axis=-1, keepdims=True)
    var = jnp.mean(jnp.square(xf - mu), axis=-1, keepdims=True)
    y = (xf - mu) * lax.rsqrt(var + LN_EPS)
    if g is not None:
        y = y * g.astype(jnp.float32) + b.astype(jnp.float32)
    return y.astype(x.dtype)


def modulate(x, shift, scale):
    return layer_norm(x) * (1 + scale) + shift


def post_norm(x, y, g, b):
    return layer_norm(DEEPNORM_ALPHA * x + y, g, b)


def neighborhood_attention(q, k, v, kc, vc, rpb):
    bsz, length = q.shape[0], q.shape[1]
    rows = length // GRID_W
    kh = min(WIN_ROWS, rows)
    n_cb = GRID_W // WIN_COLS
    span = 2 * WIN_COLS
    scale = HEAD_DIM ** -0.5
    grid = lambda t: t.reshape(bsz, rows, GRID_W, NA_HEADS, HEAD_DIM)
    qg, kg, vg = grid(q), grid(k), grid(v)
    row_start = jnp.clip(jnp.arange(rows) - kh // 2, 0, rows - kh)
    cols = jnp.arange(GRID_W)
    col_start = jnp.clip(cols - WIN_COLS // 2, 0, GRID_W - WIN_COLS).reshape(n_cb, WIN_COLS)
    blk_start = jnp.clip(jnp.arange(n_cb) * WIN_COLS - WIN_COLS // 2, 0, GRID_W - span)
    key_cols = blk_start[:, None] + jnp.arange(span)
    q_cols = cols.reshape(n_cb, WIN_COLS)
    kcol = key_cols[:, None, :]
    col_valid = (kcol >= col_start[:, :, None]) & (kcol < col_start[:, :, None] + WIN_COLS)
    dc_idx = jnp.clip(kcol - q_cols[:, :, None] + WIN_COLS - 1, 0, 2 * WIN_COLS - 2)
    bias_cols = rpb[:, :, dc_idx].astype(jnp.float32)
    bias_cols = jnp.where(col_valid, bias_cols, -jnp.inf)

    def row_fn(args):
        r, q_r = args
        rs = row_start[r]
        k_b = lax.dynamic_slice_in_dim(kg, rs, kh, axis=1)[:, :, key_cols]
        v_b = lax.dynamic_slice_in_dim(vg, rs, kh, axis=1)[:, :, key_cols]
        dr_idx = rs + jnp.arange(kh) - r + WIN_ROWS - 1
        bias = jnp.moveaxis(bias_cols[:, dr_idx], 1, 3)
        q_b = q_r.reshape(bsz, n_cb, WIN_COLS, NA_HEADS, HEAD_DIM)
        s_lat = jnp.einsum('bnqhd,bknshd->bhnqks', q_b, k_b).astype(jnp.float32) * scale + bias
        s_ctx = jnp.einsum('bnqhd,bchd->bhnqc', q_b, kc).astype(jnp.float32) * scale
        n_lat = kh * span
        s = jnp.concatenate([s_lat.reshape(bsz, NA_HEADS, n_cb, WIN_COLS, n_lat), s_ctx], axis=-1)
        p = jax.nn.softmax(s, axis=-1).astype(v.dtype)
        p_lat = p[..., :n_lat].reshape(bsz, NA_HEADS, n_cb, WIN_COLS, kh, span)
        p_ctx = p[..., n_lat:]
        o = (jnp.einsum('bhnqks,bknshd->bnqhd', p_lat, v_b)
             + jnp.einsum('bhnqc,bchd->bnqhd', p_ctx, vc))
        return o.reshape(bsz, GRID_W, NA_HEADS, HEAD_DIM)

    o = lax.map(row_fn, (jnp.arange(rows), jnp.moveaxis(qg, 1, 0)))
    return jnp.moveaxis(o, 0, 1).reshape(bsz, length, NA_WIDTH)


def context_attention(qc, kc, vc):
    s = jnp.einsum('bqhd,bkhd->bhqk', qc, kc).astype(jnp.float32) * HEAD_DIM ** -0.5
    p = jax.nn.softmax(s, axis=-1).astype(vc.dtype)
    o = jnp.einsum('bhqk,bkhd->bqhd', p, vc)
    return o.reshape(qc.shape[0], qc.shape[1], NA_WIDTH)


def fourier_mix(f, w_four):
    bsz, n = f.shape[0], f.shape[1]
    fg = f.reshape(bsz, n, F_GROUPS, F_GROUP_DIM).astype(jnp.float32).transpose(0, 2, 1, 3)
    y = jnp.fft.fft2(fg, norm='ortho').real.transpose(0, 2, 1, 3).astype(f.dtype)
    y = jnp.einsum('bngc,gce->bnge', y, w_four)
    return y.reshape(bsz, n, F_WIDTH)


def grouped_moe(h, w_router, router_bias, w_gate, w_up, w_down):
    shp = h.shape
    t = h.reshape(-1, shp[-1])
    s = jax.nn.sigmoid((t @ w_router).astype(jnp.float32))
    sb = s + router_bias.astype(jnp.float32)
    g_score = lax.top_k(sb.reshape(-1, N_GROUPS, EXPERTS_PER_GROUP), GROUP_SCORE_K)[0].sum(-1)
    g_sel = jnp.argmax(g_score, axis=-1)
    in_group = (jnp.arange(N_EXPERTS) // EXPERTS_PER_GROUP)[None, :] == g_sel[:, None]
    _, idx = lax.top_k(jnp.where(in_group, sb, -jnp.inf), TOP_K)
    w = jnp.take_along_axis(s, idx, axis=-1)
    w = w / jnp.sum(w, axis=-1, keepdims=True)
    combine = jnp.sum(jax.nn.one_hot(idx, N_EXPERTS, dtype=jnp.float32) * w[..., None], axis=1)
    hid = jax.nn.silu(jnp.einsum('nd,edf->nef', t, w_gate)) * jnp.einsum('nd,edf->nef', t, w_up)
    hid = hid * combine[:, :, None].astype(hid.dtype)
    y = jnp.einsum('nef,efd->nd', hid, w_down)
    return y.reshape(shp)


def split_heads(t):
    return t.reshape(t.shape[0], t.shape[1], NA_HEADS, HEAD_DIM)


def hybrid_layer(x, xc, c, c_ctx, w_mod, b_mod, w_in, rpb, w_four, w_out, ln1_g, ln1_b, ln2_g, ln2_b,
                 w_router, router_bias, w_gate, w_up, w_down, last):
    bsz = x.shape[0]
    m = (jax.nn.silu(c) @ w_mod + b_mod).reshape(bsz, N_MOD, 1, D_MODEL)
    mc = (jax.nn.silu(c_ctx) @ w_mod + b_mod).reshape(N_MOD, 1, D_MODEL)

    h = modulate(x, m[:, 0], m[:, 1])
    p = h @ w_in
    q = split_heads(p[..., :NA_WIDTH])
    k = split_heads(p[..., NA_WIDTH:2 * NA_WIDTH])
    v = split_heads(p[..., 2 * NA_WIDTH:3 * NA_WIDTH])
    f = p[..., 3 * NA_WIDTH:]
    hc = modulate(xc, mc[0], mc[1])
    if last:
        pc_kv = hc @ w_in[:, NA_WIDTH:3 * NA_WIDTH]
        kc = split_heads(pc_kv[..., :NA_WIDTH])
        vc = split_heads(pc_kv[..., NA_WIDTH:])
    else:
        pc = hc @ w_in
        qc = split_heads(pc[..., :NA_WIDTH])
        kc = split_heads(pc[..., NA_WIDTH:2 * NA_WIDTH])
        vc = split_heads(pc[..., 2 * NA_WIDTH:3 * NA_WIDTH])
        fc = pc[..., 3 * NA_WIDTH:]

    o_mix = jnp.concatenate([neighborhood_attention(q, k, v, kc, vc, rpb), fourier_mix(f, w_four)], axis=-1)
    x = post_norm(x, m[:, 2] * (o_mix @ w_out), ln1_g, ln1_b)

    h2 = modulate(x, m[:, 3], m[:, 4])
    x = post_norm(x, m[:, 5] * grouped_moe(h2, w_router, router_bias, w_gate, w_up, w_down), ln2_g, ln2_b)
    if last:
        return x, xc

    oc_mix = jnp.concatenate([context_attention(qc, kc, vc), fourier_mix(fc, w_four)], axis=-1)
    xc = post_norm(xc, mc[2] * (oc_mix @ w_out), ln1_g, ln1_b)
    h2c = modulate(xc, mc[3], mc[4])
    xc = post_norm(xc, mc[5] * grouped_moe(h2c, w_router, router_bias, w_gate, w_up, w_down), ln2_g, ln2_b)
    return x, xc


def setup_inputs(seed: int = 0) -> dict:
    key = jax.random.key(seed)
    ks = jax.random.split(key, 20)
    nrm = lambda k, shp: jax.random.normal(k, shp, dtype=jnp.float32)
    x = nrm(ks[0], (BATCH, SEQ, D_MODEL))
    c = nrm(ks[1], (BATCH, D_MODEL))
    ctx = nrm(ks[2], (BATCH, CTX_LEN, D_MODEL))
    c_ctx = nrm(ks[3], (D_MODEL,))
    w_mod = nrm(ks[4], (DEPTH, D_MODEL, N_MOD * D_MODEL)) * (0.5 * D_MODEL ** -0.5)
    b_mod = 0.02 * nrm(ks[5], (DEPTH, N_MOD * D_MODEL))
    w_in = nrm(ks[6], (DEPTH, D_MODEL, IN_WIDTH)) * D_MODEL ** -0.5
    w_in = w_in.at[:, :, 2 * NA_WIDTH:3 * NA_WIDTH].multiply(DEEPNORM_BETA)
    rpb = 0.2 * nrm(ks[7], (DEPTH, NA_HEADS, 2 * WIN_ROWS - 1, 2 * WIN_COLS - 1))
    w_four = nrm(ks[8], (DEPTH, F_GROUPS, F_GROUP_DIM, F_GROUP_DIM)) * (F_GROUP_DIM ** -0.5 * DEEPNORM_BETA)
    w_out = nrm(ks[9], (DEPTH, MIX_WIDTH, D_MODEL)) * (MIX_WIDTH ** -0.5 * DEEPNORM_BETA)
    ln1_g = 1.0 + 0.02 * nrm(ks[10], (DEPTH, D_MODEL))
    ln1_b = 0.02 * nrm(ks[11], (DEPTH, D_MODEL))
    ln2_g = 1.0 + 0.02 * nrm(ks[12], (DEPTH, D_MODEL))
    ln2_b = 0.02 * nrm(ks[13], (DEPTH, D_MODEL))
    w_router = nrm(ks[14], (D_MODEL, N_EXPERTS)) * D_MODEL ** -0.5
    router_bias = 0.01 * nrm(ks[15], (N_EXPERTS,))
    w_gate = nrm(ks[16], (DEPTH, N_EXPERTS, D_MODEL, D_EXPERT)) * D_MODEL ** -0.5
    w_up = nrm(ks[17], (DEPTH, N_EXPERTS, D_MODEL, D_EXPERT)) * D_MODEL ** -0.5
    w_down = nrm(ks[18], (DEPTH, N_EXPERTS, D_EXPERT, D_MODEL)) * (D_EXPERT ** -0.5 * DEEPNORM_BETA)
    return {"x": x, "c": c, "ctx": ctx, "c_ctx": c_ctx, "w_mod": w_mod, "b_mod": b_mod, "w_in": w_in,
            "rpb": rpb, "w_four": w_four, "w_out": w_out, "ln1_g": ln1_g, "ln1_b": ln1_b,
            "ln2_g": ln2_g, "ln2_b": ln2_b, "w_router": w_router, "router_bias": router_bias,
            "w_gate": w_gate, "w_up": w_up, "w_down": w_down}


def reference(x, c, ctx, c_ctx, w_mod, b_mod, w_in, rpb, w_four, w_out, ln1_g, ln1_b, ln2_g, ln2_b,
              w_router, router_bias, w_gate, w_up, w_down):
    xc = ctx
    for i in range(DEPTH):
        x, xc = hybrid_layer(x, xc, c, c_ctx, w_mod[i], b_mod[i], w_in[i], rpb[i], w_four[i], w_out[i],
                             ln1_g[i], ln1_b[i], ln2_g[i], ln2_b[i], w_router, router_bias,
                             w_gate[i], w_up[i], w_down[i], i == DEPTH - 1)
    return x
```

```python
import functools
import math

import numpy as np
import jax
import jax.numpy as jnp
from jax import lax
from jax.experimental import pallas as pl
from jax.experimental.pallas import tpu as pltpu

D_MODEL = 1024
DEPTH = 2
GRID_W = 64
NA_HEADS = 8
HEAD_DIM = 64
NA_WIDTH = NA_HEADS * HEAD_DIM
WIN_ROWS = 8
WIN_COLS = 16
F_GROUPS = 8
F_GROUP_DIM = 64
F_WIDTH = F_GROUPS * F_GROUP_DIM
IN_WIDTH = 3 * NA_WIDTH + F_WIDTH
N_EXPERTS = 16
N_GROUPS = 4
EXPERTS_PER_GROUP = N_EXPERTS // N_GROUPS
D_EXPERT = 256
N_MOD = 6
DEEPNORM_ALPHA = (2.0 * DEPTH) ** 0.25
LN_EPS = 1e-6

F32 = jnp.float32
BF16 = jnp.bfloat16

V7X_VMEM_BYTES = 64 * 1024 * 1024
VMEM_LIMIT_BYTES = (V7X_VMEM_BYTES * 3) // 4
LANES = 128
HEADS_PER_STEP = LANES // HEAD_DIM
MASK_VALUE = -1e30

Q_ROWS = 8
K_ROWS = 16
K_CHUNK_ROWS = 4
N_K_CHUNKS = K_ROWS // K_CHUNK_ROWS


def _cparams(n_grid_dims):
    return pltpu.CompilerParams(dimension_semantics=("arbitrary",) * n_grid_dims,
                                vmem_limit_bytes=VMEM_LIMIT_BYTES)


def _layer_norm(x):
    mu = jnp.mean(x, axis=-1, keepdims=True)
    xc = x - mu
    var = jnp.mean(xc * xc, axis=-1, keepdims=True)
    return xc * lax.rsqrt(var + LN_EPS)


def _dot(a, b):
    return jnp.dot(a, b, preferred_element_type=F32)


def _dot_nt(a, b):
    return lax.dot_general(a, b, (((1,), (1,)), ((), ())), preferred_element_type=F32)


def _mod_kernel(c_ref, w_ref, b_ref, o_ref):
    c = c_ref[...]
    a = c * jax.nn.sigmoid(c)
    o_ref[0] = jnp.dot(a, w_ref[0], preferred_element_type=F32, precision=lax.Precision.HIGHEST) + b_ref[0]


def _modulation(cvec, w_mod, b_mod):
    n_col_blocks = 4
    wc = (N_MOD * D_MODEL) // n_col_blocks
    rows = cvec.shape[0]
    return pl.pallas_call(
        _mod_kernel,
        grid=(DEPTH, n_col_blocks),
        in_specs=[pl.BlockSpec((rows, D_MODEL), lambda i, j: (0, 0)),
                  pl.BlockSpec((1, D_MODEL, wc), lambda i, j: (i, 0, j)),
                  pl.BlockSpec((1, 1, wc), lambda i, j: (i, 0, j))],
        out_specs=pl.BlockSpec((1, rows, wc), lambda i, j: (i, 0, j)),
        out_shape=jax.ShapeDtypeStruct((DEPTH, rows, N_MOD * D_MODEL), F32),
        compiler_params=_cparams(2),
        name="modulation",
    )(cvec, w_mod, b_mod.reshape(DEPTH, 1, N_MOD * D_MODEL))


def _proj_kernel(x_ref, m_ref, w_ref, q_ref, k_ref, v_ref, f_ref):
    h = _layer_norm(x_ref[0]) * (1.0 + m_ref[0, 1:2, :]) + m_ref[0, 0:1, :]
    p = _dot(h.astype(BF16), w_ref[...])
    q_ref[0] = (p[:, :NA_WIDTH] * (HEAD_DIM ** -0.5)).astype(BF16)
    k_ref[0] = p[:, NA_WIDTH:2 * NA_WIDTH].astype(BF16)
    v_ref[0] = p[:, 2 * NA_WIDTH:3 * NA_WIDTH].astype(BF16)
    f_ref[0] = p[:, 3 * NA_WIDTH:].astype(BF16)


def _in_projection(x, m, w_in_bf16, tm):
    bsz, length, _ = x.shape
    out = jax.ShapeDtypeStruct((bsz, length, NA_WIDTH), BF16)
    o_spec = pl.BlockSpec((1, tm, NA_WIDTH), lambda b, i: (b, i, 0))
    return pl.pallas_call(
        _proj_kernel,
        grid=(bsz, length // tm),
        in_specs=[pl.BlockSpec((1, tm, D_MODEL), lambda b, i: (b, i, 0)),
                  pl.BlockSpec((1, N_MOD, D_MODEL), lambda b, i: (b, 0, 0)),
                  pl.BlockSpec((D_MODEL, IN_WIDTH), lambda b, i: (0, 0))],
        out_specs=[o_spec, o_spec, o_spec, o_spec],
        out_shape=[out, out, out, out],
        compiler_params=_cparams(2),
        name="ln_mod_in_proj",
    )(x, m, w_in_bf16)


def _softmax_pv(qh, keys, values, bias):
    scores = [_dot_nt(qh, k) for k in keys]
    scores[0] = scores[0] + bias if bias is not None else scores[0]
    m = functools.reduce(jnp.maximum, [jnp.max(s, axis=-1, keepdims=True) for s in scores])
    probs = [jnp.exp(s - m) for s in scores]
    denom = functools.reduce(jnp.add, [jnp.sum(p, axis=-1, keepdims=True) for p in probs])
    o = functools.reduce(jnp.add, [_dot(p.astype(BF16), v) for p, v in zip(probs, values)])
    return o / denom


def _two_head_attention(q, keys, values, bias_ref, q_start, q_len):
    lane = lax.broadcasted_iota(jnp.int32, (1, LANES), 1)
    out = None
    for h in range(HEADS_PER_STEP):
        in_head = (lane >= HEAD_DIM * h) & (lane < HEAD_DIM * (h + 1))
        qh = jnp.where(in_head, q, jnp.zeros_like(q))
        bias = None if bias_ref is None else bias_ref[0, h, q_start:q_start + q_len, :]
        o = _softmax_pv(qh, keys, values, bias)
        out = o if out is None else jnp.where(in_head, o, out)
    return out


def _na_kernel(q_ref, *refs):
    k_refs = refs[:N_K_CHUNKS]
    v_refs = refs[N_K_CHUNKS:2 * N_K_CHUNKS]
    kc_ref, vc_ref, bias_ref, o_ref = refs[2 * N_K_CHUNKS:]
    k = jnp.concatenate([r[0] for r in k_refs], axis=0)
    v = jnp.concatenate([r[0] for r in v_refs], axis=0)
    keys = [k, kc_ref[0]]
    values = [v, vc_ref[0]]
    q_len = 256
    for q_start in range(0, Q_ROWS * GRID_W, q_len):
        q = q_ref[0, q_start:q_start + q_len, :]
        o = _two_head_attention(q, keys, values, bias_ref, q_start, q_len)
        o_ref[0, q_start:q_start + q_len, :] = o.astype(BF16)


def _na_bias_tables(rpb, rows):
    n_rb = rows // Q_ROWS
    kh = min(WIN_ROWS, rows)
    qi, qc = np.divmod(np.arange(Q_ROWS * GRID_W), GRID_W)
    kj, kc = np.divmod(np.arange(K_ROWS * GRID_W), GRID_W)
    cs = np.clip(qc - WIN_COLS // 2, 0, GRID_W - WIN_COLS)
    col_valid = (kc[None, :] >= cs[:, None]) & (kc[None, :] < cs[:, None] + WIN_COLS)
    dc = np.clip(kc[None, :] - qc[:, None] + WIN_COLS - 1, 0, 2 * WIN_COLS - 2)
    tables = []
    for rb in (0, 1, n_rb - 1):
        qr = rb * Q_ROWS + qi
        kr = _key_row_start(rb, rows) + kj
        rs = np.clip(qr - kh // 2, 0, rows - kh)
        row_valid = (kr[None, :] >= rs[:, None]) & (kr[None, :] < rs[:, None] + kh)
        dr = np.clip(kr[None, :] - qr[:, None] + WIN_ROWS - 1, 0, 2 * WIN_ROWS - 2)
        valid = jnp.asarray(row_valid & col_valid)
        tables.append(jnp.where(valid[None], rpb[:, dr, dc].astype(F32), MASK_VALUE))
    return jnp.stack(tables)


def _key_row_start(rb, rows):
    return int(np.clip(rb * Q_ROWS - (K_ROWS - Q_ROWS) // 2, 0, rows - K_ROWS))


def _neighborhood_attention(q, k, v, kc, vc, bias_tables):
    bsz, length, _ = q.shape
    rows = length // GRID_W
    n_rb = rows // Q_ROWS
    assert rows % Q_ROWS == 0 and rows >= K_ROWS + Q_ROWS and n_rb >= 3
    n_ctx = kc.shape[1]
    tq = Q_ROWS * GRID_W
    tk = K_CHUNK_ROWS * GRID_W
    max_chunk = (rows - K_ROWS) // K_CHUNK_ROWS
    half = (K_ROWS - Q_ROWS) // 2 // K_CHUNK_ROWS

    def kv_spec(j):
        def index(b, hp, rb):
            start = jnp.clip(rb * (Q_ROWS // K_CHUNK_ROWS) - half, 0, max_chunk)
            return (b, start + j, hp)
        return pl.BlockSpec((1, tk, LANES), index)

    def bias_index(b, hp, rb):
        variant = jnp.where(rb == 0, 0, jnp.where(rb == n_rb - 1, 2, 1))
        return (variant, hp, 0, 0)

    ctx_spec = pl.BlockSpec((1, n_ctx, LANES), lambda b, hp, rb: (b, 0, hp))
    q_spec = pl.BlockSpec((1, tq, LANES), lambda b, hp, rb: (b, rb, hp))
    return pl.pallas_call(
        _na_kernel,
        grid=(bsz, NA_HEADS // HEADS_PER_STEP, n_rb),
        in_specs=([q_spec] + [kv_spec(j) for j in range(N_K_CHUNKS)] + [kv_spec(j) for j in range(N_K_CHUNKS)]
                  + [ctx_spec, ctx_spec,
                     pl.BlockSpec((1, HEADS_PER_STEP, tq, K_ROWS * GRID_W), bias_index)]),
        out_specs=q_spec,
        out_shape=jax.ShapeDtypeStruct((bsz, length, NA_WIDTH), BF16),
        compiler_params=_cparams(3),
        name="neighborhood_attention",
    )(q, *([k] * N_K_CHUNKS), *([v] * N_K_CHUNKS), kc, vc, bias_tables)


def _ctx_attn_kernel(q_ref, k_ref, v_ref, o_ref):
    n = q_ref.shape[1]
    o = _two_head_attention(q_ref[0], [k_ref[0]], [v_ref[0]], None, 0, n)
    o_ref[0] = o.astype(BF16)


def _context_attention(qc, kc, vc):
    bsz, n_ctx, _ = qc.shape
    spec = pl.BlockSpec((1, n_ctx, LANES), lambda b, hp: (b, 0, hp))
    return pl.pallas_call(
        _ctx_attn_kernel,
        grid=(bsz, NA_HEADS // HEADS_PER_STEP),
        in_specs=[spec, spec, spec],
        out_specs=spec,
        out_shape=jax.ShapeDtypeStruct((bsz, n_ctx, NA_WIDTH), BF16),
        compiler_params=_cparams(2),
        name="context_attention",
    )(qc, kc, vc)


def _dft_cos_sin(n):
    ang = 2.0 * np.pi * np.outer(np.arange(n), np.arange(n)) / n
    return np.cos(ang), np.sin(ang)


def _bf16_table(a):
    return jnp.asarray(a, F32).astype(BF16)


def _channel_dft_matrix():
    c, s = _dft_cos_sin(F_GROUP_DIM)
    scale = F_GROUP_DIM ** -0.5
    eye = np.eye(F_GROUPS)
    return np.concatenate([np.kron(eye, c), np.kron(eye, s)], axis=1) * scale


def _fft_stage1_kernel(f_ref, w1_ref, cs_ref, sc_ref, tc_ref, ts_ref, zr_ref, zi_ref, *, n_slow, nt):
    x = jnp.concatenate([f_ref[0, :, t * F_WIDTH:(t + 1) * F_WIDTH] for t in range(nt)], axis=0)
    ab = _dot(x, w1_ref[...]).astype(BF16)
    for t in range(nt):
        a = ab[t * n_slow:(t + 1) * n_slow, :F_WIDTH]
        b = ab[t * n_slow:(t + 1) * n_slow, F_WIDTH:]
        z = _dot(cs_ref[...], a) + _dot(sc_ref[...], b)
        zr, zi = z[:n_slow], z[n_slow:]
        c, s = tc_ref[t], ts_ref[t]
        zr_ref[0, t] = (zr * c - zi * s).astype(BF16)
        zi_ref[0, t] = (zr * s + zi * c).astype(BF16)


def _fft_stage2_kernel(zr_ref, zi_ref, fc_ref, fs_ref, y_ref):
    y_ref[0] = (_dot(fc_ref[...], zr_ref[0]) + _dot(fs_ref[...], zi_ref[0])).astype(BF16)


def _fourier_positions(f, n_slow, n_fast):
    bsz, n, _ = f.shape
    assert n == n_slow * n_fast
    nt = 8
    w1 = _bf16_table(_channel_dft_matrix())
    c1, s1 = _dft_cos_sin(n_slow)
    sc1 = n_slow ** -0.5
    cs = _bf16_table(np.concatenate([c1, s1], axis=0) * sc1)
    sc = _bf16_table(np.concatenate([-s1, c1], axis=0) * sc1)
    tw = 2.0 * np.pi * np.outer(np.arange(n_fast), np.arange(n_slow)) / n
    tc = jnp.asarray(np.cos(tw)[:, :, None], F32)
    ts = jnp.asarray(np.sin(tw)[:, :, None], F32)
    z_shape = jax.ShapeDtypeStruct((bsz, n_fast, n_slow, F_WIDTH), BF16)
    z_spec = pl.BlockSpec((1, nt, n_slow, F_WIDTH), lambda b, j: (b, j, 0, 0))
    const2 = lambda b, j: (0, 0)
    tw_spec = pl.BlockSpec((nt, n_slow, 1), lambda b, j: (j, 0, 0))
    zr, zi = pl.pallas_call(
        functools.partial(_fft_stage1_kernel, n_slow=n_slow, nt=nt),
        grid=(bsz, n_fast // nt),
        in_specs=[pl.BlockSpec((1, n_slow, nt * F_WIDTH), lambda b, j: (b, 0, j)),
                  pl.BlockSpec(w1.shape, const2), pl.BlockSpec(cs.shape, const2), pl.BlockSpec(sc.shape, const2),
                  tw_spec, tw_spec],
        out_specs=[z_spec, z_spec],
        out_shape=[z_shape, z_shape],
        compiler_params=_cparams(2),
        name="fnet_stage1",
    )(f.reshape(bsz, n_slow, n_fast * F_WIDTH), w1, cs, sc, tc, ts)

    c2, s2 = _dft_cos_sin(n_fast)
    sc2 = n_fast ** -0.5
    fc = _bf16_table(c2 * sc2)
    fs = _bf16_table(-s2 * sc2)
    cols = n_slow * F_WIDTH
    wcol = min(cols, 8192)
    blk = pl.BlockSpec((1, n_fast, wcol), lambda b, j: (b, 0, j))
    y = pl.pallas_call(
        _fft_stage2_kernel,
        grid=(bsz, cols // wcol),
        in_specs=[blk, blk, pl.BlockSpec(fc.shape, const2), pl.BlockSpec(fs.shape, const2)],
        out_specs=blk,
        out_shape=jax.ShapeDtypeStruct((bsz, n_fast, cols), BF16),
        compiler_params=_cparams(2),
        name="fnet_stage2",
    )(zr.reshape(bsz, n_fast, cols), zi.reshape(bsz, n_fast, cols), fc, fs)
    return y.reshape(bsz, n, F_WIDTH)


def _second_largest_sum(a, b, c, d):
    mab, nab = jnp.maximum(a, b), jnp.minimum(a, b)
    mcd, ncd = jnp.maximum(c, d), jnp.minimum(c, d)
    return jnp.maximum(mab, mcd) + jnp.maximum(jnp.minimum(mab, mcd), jnp.maximum(nab, ncd))


def _route(s, sb):
    s_rows = [s[e:e + 1] for e in range(N_EXPERTS)]
    sb_rows = [sb[e:e + 1] for e in range(N_EXPERTS)]
    epg = EXPERTS_PER_GROUP
    g_score = [_second_largest_sum(*sb_rows[g * epg:(g + 1) * epg]) for g in range(N_GROUPS)]
    best = functools.reduce(jnp.maximum, g_score)
    is_g, taken = [], None
    for g in range(N_GROUPS):
        hit = g_score[g] == best
        is_g.append(hit if taken is None else hit & jnp.logical_not(taken))
        taken = hit if taken is None else taken | hit

    def pick(rows, j):
        out = rows[(N_GROUPS - 1) * epg + j]
        for g in range(N_GROUPS - 2, -1, -1):
            out = jnp.where(is_g[g], rows[g * epg + j], out)
        return out

    cand_sb = [pick(sb_rows, j) for j in range(epg)]
    cand_s = [pick(s_rows, j) for j in range(epg)]
    w = []
    for j in range(epg):
        rank = jnp.zeros_like(cand_sb[j])
        for i in range(epg):
            if i == j:
                continue
            ahead = (cand_sb[i] > cand_sb[j]) | ((cand_sb[i] == cand_sb[j]) & (i < j))
            rank = rank + jnp.where(ahead, 1.0, 0.0)
        w.append(jnp.where(rank < 2.0, cand_s[j], 0.0))
    total = functools.reduce(jnp.add, w)
    w = [wj / total for wj in w]
    return [jnp.where(is_g[e // epg], w[e % epg], 0.0) for e in range(N_EXPERTS)]


def _out_kernel(a_ref, y_ref, x_ref, m_ref, wo_ref, bd_ref, g_ref, b_ref, wr_ref, rb_ref,
                x1_ref, h2_ref, comb_ref):
    y2 = _dot(y_ref[0], bd_ref[...]).astype(BF16)
    o = _dot(a_ref[0], wo_ref[:NA_WIDTH, :]) + _dot(y2, wo_ref[NA_WIDTH:, :])
    z = DEEPNORM_ALPHA * x_ref[0] + m_ref[0, 2:3, :] * o
    x1 = _layer_norm(z) * g_ref[...] + b_ref[...]
    x1_ref[0] = x1
    h2 = (_layer_norm(x1) * (1.0 + m_ref[0, 4:5, :]) + m_ref[0, 3:4, :]).astype(BF16)
    h2_ref[0] = h2
    s = jax.nn.sigmoid(_dot_nt(wr_ref[...], h2))
    for e, row in enumerate(_route(s, s + rb_ref[...])):
        comb_ref[0, e:e + 1, :] = row


def _out_projection(attn, yf, x, m, w_out_bf16, w_four_bd, ln_g, ln_b, w_router_t, router_bias, tm):
    bsz, length, _ = x.shape
    row = lambda b, i: (b, i, 0)
    const2 = lambda b, i: (0, 0)
    return pl.pallas_call(
        _out_kernel,
        grid=(bsz, length // tm),
        in_specs=[pl.BlockSpec((1, tm, NA_WIDTH), row), pl.BlockSpec((1, tm, F_WIDTH), row),
                  pl.BlockSpec((1, tm, D_MODEL), row),
                  pl.BlockSpec((1, N_MOD, D_MODEL), lambda b, i: (b, 0, 0)),
                  pl.BlockSpec((NA_WIDTH + F_WIDTH, D_MODEL), const2),
                  pl.BlockSpec((F_WIDTH, F_WIDTH), const2),
                  pl.BlockSpec((1, D_MODEL), const2), pl.BlockSpec((1, D_MODEL), const2),
                  pl.BlockSpec((N_EXPERTS, D_MODEL), const2), pl.BlockSpec((N_EXPERTS, 1), const2)],
        out_specs=[pl.BlockSpec((1, tm, D_MODEL), row), pl.BlockSpec((1, tm, D_MODEL), row),
                   pl.BlockSpec((1, N_EXPERTS, tm), lambda b, i: (b, 0, i))],
        out_shape=[jax.ShapeDtypeStruct((bsz, length, D_MODEL), F32),
                   jax.ShapeDtypeStruct((bsz, length, D_MODEL), BF16),
                   jax.ShapeDtypeStruct((bsz, N_EXPERTS, length), F32)],
        compiler_params=_cparams(2),
        name="out_proj_norm_route",
    )(attn, yf, x, m, w_out_bf16, w_four_bd, ln_g.reshape(1, D_MODEL), ln_b.reshape(1, D_MODEL),
      w_router_t, router_bias.reshape(N_EXPERTS, 1))


def _moe_kernel(h_ref, c_ref, wg_ref, wu_ref, wd_ref, x1_ref, m_ref, g_ref, b_ref, o_ref, acc_ref):
    e = pl.program_id(2)

    @pl.when(e == 0)
    def _():
        acc_ref[...] = jnp.zeros_like(acc_ref)

    h = h_ref[0]
    gate = _dot(h, wg_ref[0])
    up = _dot(h, wu_ref[0])
    comb = c_ref[0]
    lane = lax.broadcasted_iota(jnp.int32, comb.shape, 1)
    cw = jnp.sum(jnp.where(lane == e, comb, 0.0), axis=-1, keepdims=True)
    hid = (gate * jax.nn.sigmoid(gate)) * up * cw
    acc_ref[...] += _dot(hid.astype(BF16), wd_ref[0])

    @pl.when(e == N_EXPERTS - 1)
    def _():
        z = DEEPNORM_ALPHA * x1_ref[0] + m_ref[0, 5:6, :] * acc_ref[...]
        o_ref[0] = _layer_norm(z) * g_ref[...] + b_ref[...]


def _moe(h2, comb, x1, m, w_gate, w_up, w_down, ln_g, ln_b, tm):
    bsz, length, _ = x1.shape
    row = lambda b, i, e: (b, i, 0)
    const2 = lambda b, i, e: (0, 0)
    return pl.pallas_call(
        _moe_kernel,
        grid=(bsz, length // tm, N_EXPERTS),
        in_specs=[pl.BlockSpec((1, tm, D_MODEL), row), pl.BlockSpec((1, tm, N_EXPERTS), row),
                  pl.BlockSpec((1, D_MODEL, D_EXPERT), lambda b, i, e: (e, 0, 0)),
                  pl.BlockSpec((1, D_MODEL, D_EXPERT), lambda b, i, e: (e, 0, 0)),
                  pl.BlockSpec((1, D_EXPERT, D_MODEL), lambda b, i, e: (e, 0, 0)),
                  pl.BlockSpec((1, tm, D_MODEL), row),
                  pl.BlockSpec((1, N_MOD, D_MODEL), lambda b, i, e: (b, 0, 0)),
                  pl.BlockSpec((1, D_MODEL), const2), pl.BlockSpec((1, D_MODEL), const2)],
        out_specs=pl.BlockSpec((1, tm, D_MODEL), row),
        out_shape=jax.ShapeDtypeStruct((bsz, length, D_MODEL), F32),
        scratch_shapes=[pltpu.VMEM((tm, D_MODEL), F32)],
        compiler_params=_cparams(3),
        name="moe_experts_norm",
    )(h2, comb, w_gate, w_up, w_down, x1, m, ln_g.reshape(1, D_MODEL), ln_b.reshape(1, D_MODEL))


def _ctx_fourier_kernel(f_ref, w1_ref, c_ref, s_ref, y_ref):
    ab = _dot(f_ref[0], w1_ref[...]).astype(BF16)
    y = _dot(c_ref[...], ab[:, :F_WIDTH]) + _dot(s_ref[...], ab[:, F_WIDTH:])
    y_ref[0] = y.astype(BF16)


def _context_fourier(fc):
    bsz, n, _ = fc.shape
    w1 = _bf16_table(_channel_dft_matrix())
    c, s = _dft_cos_sin(n)
    cm = _bf16_table(c * n ** -0.5)
    sm = _bf16_table(-s * n ** -0.5)
    const2 = lambda b: (0, 0)
    blk = pl.BlockSpec((1, n, F_WIDTH), lambda b: (b, 0, 0))
    return pl.pallas_call(
        _ctx_fourier_kernel,
        grid=(bsz,),
        in_specs=[blk, pl.BlockSpec(w1.shape, const2), pl.BlockSpec(cm.shape, const2),
                  pl.BlockSpec(sm.shape, const2)],
        out_specs=blk,
        out_shape=jax.ShapeDtypeStruct((bsz, n, F_WIDTH), BF16),
        compiler_params=_cparams(1),
        name="context_fnet",
    )(fc, w1, cm, sm)


def _block_diag(w):
    g, c, _ = w.shape
    eye = jnp.eye(g, dtype=w.dtype)
    return (eye[:, None, :, None] * w[:, :, None, :]).reshape(g * c, g * c)


def kernel(x, c, ctx, c_ctx, w_mod, b_mod, w_in, rpb, w_four, w_out, ln1_g, ln1_b, ln2_g, ln2_b,
           w_router, router_bias, w_gate, w_up, w_down):
    bsz, length, _ = x.shape
    n_ctx = ctx.shape[1]
    rows = length // GRID_W

    cvec = jnp.concatenate([c, c_ctx[None, :], jnp.zeros((8 - bsz - 1, D_MODEL), F32)], axis=0)
    mods = _modulation(cvec, w_mod, b_mod)
    w_router_t = w_router.T.astype(BF16)

    xc = ctx
    for i in range(DEPTH):
        last = i == DEPTH - 1
        m = mods[i, :bsz].reshape(bsz, N_MOD, D_MODEL)
        mc = jnp.broadcast_to(mods[i, bsz].reshape(1, N_MOD, D_MODEL), (bsz, N_MOD, D_MODEL))
        w_in_b = w_in[i].astype(BF16)
        w_out_b = w_out[i].astype(BF16)
        w_four_bd = _block_diag(w_four[i]).astype(BF16)
        wg, wu, wd = w_gate[i].astype(BF16), w_up[i].astype(BF16), w_down[i].astype(BF16)

        q, k, v, f = _in_projection(x, m, w_in_b, tm=512)
        qc, kc, vc, fc = _in_projection(xc, mc, w_in_b, tm=n_ctx)

        attn = _neighborhood_attention(q, k, v, kc, vc, _na_bias_tables(rpb[i], rows))
        yf = _fourier_positions(f, n_slow=rows, n_fast=GRID_W)
        x1, h2, comb = _out_projection(attn, yf, x, m, w_out_b, w_four_bd, ln1_g[i], ln1_b[i],
                                       w_router_t, router_bias, tm=512)
        x = _moe(h2, jnp.swapaxes(comb, 1, 2), x1, m, wg, wu, wd, ln2_g[i], ln2_b[i], tm=512)
        if last:
            break

        attn_c = _context_attention(qc, kc, vc)
        yc = _context_fourier(fc)
        xc1, h2c, comb_c = _out_projection(attn_c, yc, xc, mc, w_out_b, w_four_bd, ln1_g[i], ln1_b[i],
                                           w_router_t, router_bias, tm=n_ctx)
        xc = _moe(h2c, jnp.swapaxes(comb_c, 1, 2), xc1, mc, wg, wu, wd, ln2_g[i], ln2_b[i], tm=n_ctx)
    return x
```

```python
import functools
import math

import numpy as np
import jax
import jax.numpy as jnp
from jax import lax
from jax.experimental import pallas as pl
from jax.experimental.pallas import tpu as pltpu

D_MODEL = 1024
DEPTH = 2
GRID_W = 64
NA_HEADS = 8
HEAD_DIM = 64
NA_WIDTH = NA_HEADS * HEAD_DIM
WIN_ROWS = 8
WIN_COLS = 16
F_GROUPS = 8
F_GROUP_DIM = 64
F_WIDTH = F_GROUPS * F_GROUP_DIM
IN_WIDTH = 3 * NA_WIDTH + F_WIDTH
N_EXPERTS = 16
N_GROUPS = 4
EXPERTS_PER_GROUP = N_EXPERTS // N_GROUPS
D_EXPERT = 256
N_MOD = 6
DEEPNORM_ALPHA = (2.0 * DEPTH) ** 0.25
LN_EPS = 1e-6

F32 = jnp.float32
BF16 = jnp.bfloat16

V7X_VMEM_BYTES = 64 * 1024 * 1024
VMEM_LIMIT_BYTES = (V7X_VMEM_BYTES * 3) // 4
LANES = 128
HEADS_PER_STEP = LANES // HEAD_DIM
MASK_VALUE = -1e30

Q_ROWS = 8
K_ROWS = 16
K_CHUNK_ROWS = 4
N_K_CHUNKS = K_ROWS // K_CHUNK_ROWS


def _cparams(n_grid_dims):
    return pltpu.CompilerParams(dimension_semantics=("arbitrary",) * n_grid_dims,
                                vmem_limit_bytes=VMEM_LIMIT_BYTES)


def _layer_norm(x):
    mu = jnp.mean(x, axis=-1, keepdims=True)
    xc = x - mu
    var = jnp.mean(xc * xc, axis=-1, keepdims=True)
    return xc * lax.rsqrt(var + LN_EPS)


def _dot(a, b):
    return jnp.dot(a, b, preferred_element_type=F32)


def _dot_nt(a, b):
    return lax.dot_general(a, b, (((1,), (1,)), ((), ())), preferred_element_type=F32)


def _mod_kernel(c_ref, w_ref, b_ref, o_ref):
    c = c_ref[...]
    a = c * jax.nn.sigmoid(c)
    o_ref[0] = jnp.dot(a, w_ref[0], preferred_element_type=F32, precision=lax.Precision.HIGHEST) + b_ref[0]


def _modulation(cvec, w_mod, b_mod):
    n_col_blocks = 4
    wc = (N_MOD * D_MODEL) // n_col_blocks
    rows = cvec.shape[0]
    return pl.pallas_call(
        _mod_kernel,
        grid=(DEPTH, n_col_blocks),
        in_specs=[pl.BlockSpec((rows, D_MODEL), lambda i, j: (0, 0)),
                  pl.BlockSpec((1, D_MODEL, wc), lambda i, j: (i, 0, j)),
                  pl.BlockSpec((1, 1, wc), lambda i, j: (i, 0, j))],
        out_specs=pl.BlockSpec((1, rows, wc), lambda i, j: (i, 0, j)),
        out_shape=jax.ShapeDtypeStruct((DEPTH, rows, N_MOD * D_MODEL), F32),
        compiler_params=_cparams(2),
        name="modulation",
    )(cvec, w_mod, b_mod.reshape(DEPTH, 1, N_MOD * D_MODEL))


def _proj_kernel(x_ref, m_ref, w_ref, q_ref, k_ref, v_ref, f_ref):
    h = _layer_norm(x_ref[0]) * (1.0 + m_ref[0, 1:2, :]) + m_ref[0, 0:1, :]
    p = _dot(h.astype(BF16), w_ref[...])
    q_ref[0] = (p[:, :NA_WIDTH] * (HEAD_DIM ** -0.5)).astype(BF16)
    k_ref[0] = p[:, NA_WIDTH:2 * NA_WIDTH].astype(BF16)
    v_ref[0] = p[:, 2 * NA_WIDTH:3 * NA_WIDTH].astype(BF16)
    f_ref[0] = p[:, 3 * NA_WIDTH:].astype(BF16)


def _in_projection(x, m, w_in_bf16, tm):
    bsz, length, _ = x.shape
    out = jax.ShapeDtypeStruct((bsz, length, NA_WIDTH), BF16)
    o_spec = pl.BlockSpec((1, tm, NA_WIDTH), lambda b, i: (b, i, 0))
    return pl.pallas_call(
        _proj_kernel,
        grid=(bsz, length // tm),
        in_specs=[pl.BlockSpec((1, tm, D_MODEL), lambda b, i: (b, i, 0)),
                  pl.BlockSpec((1, N_MOD, D_MODEL), lambda b, i: (b, 0, 0)),
                  pl.BlockSpec((D_MODEL, IN_WIDTH), lambda b, i: (0, 0))],
        out_specs=[o_spec, o_spec, o_spec, o_spec],
        out_shape=[out, out, out, out],
        compiler_params=_cparams(2),
        name="ln_mod_in_proj",
    )(x, m, w_in_bf16)


def _softmax_pv(qh, keys, values, bias):
    scores = [_dot_nt(qh, k) for k in keys]
    scores[0] = scores[0] + bias if bias is not None else scores[0]
    m = functools.reduce(jnp.maximum, [jnp.max(s, axis=-1, keepdims=True) for s in scores])
    probs = [jnp.exp(s - m) for s in scores]
    denom = functools.reduce(jnp.add, [jnp.sum(p, axis=-1, keepdims=True) for p in probs])
    o = functools.reduce(jnp.add, [_dot(p.astype(BF16), v) for p, v in zip(probs, values)])
    return o / denom


def _two_head_attention(q, keys, values, bias_ref, q_start, q_len):
    lane = lax.broadcasted_iota(jnp.int32, (1, LANES), 1)
    out = None
    for h in range(HEADS_PER_STEP):
        in_head = (lane >= HEAD_DIM * h) & (lane < HEAD_DIM * (h + 1))
        qh = jnp.where(in_head, q, jnp.zeros_like(q))
        bias = None if bias_ref is None else bias_ref[0, h, q_start:q_start + q_len, :]
        o = _softmax_pv(qh, keys, values, bias)
        out = o if out is None else jnp.where(in_head, o, out)
    return out


def _na_kernel(q_ref, *refs):
    k_refs = refs[:N_K_CHUNKS]
    v_refs = refs[N_K_CHUNKS:2 * N_K_CHUNKS]
    kc_ref, vc_ref, bias_ref, o_ref = refs[2 * N_K_CHUNKS:]
    k = jnp.concatenate([r[0] for r in k_refs], axis=0)
    v = jnp.concatenate([r[0] for r in v_refs], axis=0)
    keys = [k, kc_ref[0]]
    values = [v, vc_ref[0]]
    q_len = 256
    for q_start in range(0, Q_ROWS * GRID_W, q_len):
        q = q_ref[0, q_start:q_start + q_len, :]
        o = _two_head_attention(q, keys, values, bias_ref, q_start, q_len)
        o_ref[0, q_start:q_start + q_len, :] = o.astype(BF16)


def _na_bias_tables(rpb, rows):
    n_rb = rows // Q_ROWS
    kh = min(WIN_ROWS, rows)
    qc, kc = np.arange(GRID_W)[:, None], np.arange(GRID_W)[None, :]
    cs = np.clip(qc - WIN_COLS // 2, 0, GRID_W - WIN_COLS)
    col_valid = jnp.asarray((kc >= cs) & (kc < cs + WIN_COLS))
    pad = GRID_W - WIN_COLS
    rpb_pad = jnp.pad(rpb.astype(F32), ((0, 0), (0, 0), (pad, pad)))
    toeplitz = jnp.stack([rpb_pad[:, :, GRID_W - 1 - c:2 * GRID_W - 1 - c] for c in range(GRID_W)], axis=2)
    toeplitz = jnp.where(col_valid, toeplitz, MASK_VALUE)
    masked = jnp.full((rpb.shape[0], GRID_W, GRID_W), MASK_VALUE, F32)
    tables = []
    for rb in (0, 1, n_rb - 1):
        block_rows = []
        for qi in range(Q_ROWS):
            qr = rb * Q_ROWS + qi
            rs = int(np.clip(qr - kh // 2, 0, rows - kh))
            blocks = []
            for kj in range(K_ROWS):
                kr = _key_row_start(rb, rows) + kj
                blocks.append(toeplitz[:, kr - qr + WIN_ROWS - 1] if rs <= kr < rs + kh else masked)
            block_rows.append(jnp.concatenate(blocks, axis=-1))
        tables.append(jnp.concatenate(block_rows, axis=1))
    return jnp.stack(tables)


def _key_row_start(rb, rows):
    return int(np.clip(rb * Q_ROWS - (K_ROWS - Q_ROWS) // 2, 0, rows - K_ROWS))


def _neighborhood_attention(q, k, v, kc, vc, bias_tables):
    bsz, length, _ = q.shape
    rows = length // GRID_W
    n_rb = rows // Q_ROWS
    assert rows % Q_ROWS == 0 and rows >= K_ROWS + Q_ROWS and n_rb >= 3
    n_ctx = kc.shape[1]
    tq = Q_ROWS * GRID_W
    tk = K_CHUNK_ROWS * GRID_W
    max_chunk = (rows - K_ROWS) // K_CHUNK_ROWS
    half = (K_ROWS - Q_ROWS) // 2 // K_CHUNK_ROWS

    def kv_spec(j):
        def index(b, hp, rb):
            start = jnp.clip(rb * (Q_ROWS // K_CHUNK_ROWS) - half, 0, max_chunk)
            return (b, start + j, hp)
        return pl.BlockSpec((1, tk, LANES), index)

    def bias_index(b, hp, rb):
        variant = jnp.where(rb == 0, 0, jnp.where(rb == n_rb - 1, 2, 1))
        return (variant, hp, 0, 0)

    ctx_spec = pl.BlockSpec((1, n_ctx, LANES), lambda b, hp, rb: (b, 0, hp))
    q_spec = pl.BlockSpec((1, tq, LANES), lambda b, hp, rb: (b, rb, hp))
    return pl.pallas_call(
        _na_kernel,
        grid=(bsz, NA_HEADS // HEADS_PER_STEP, n_rb),
        in_specs=([q_spec] + [kv_spec(j) for j in range(N_K_CHUNKS)] + [kv_spec(j) for j in range(N_K_CHUNKS)]
                  + [ctx_spec, ctx_spec,
                     pl.BlockSpec((1, HEADS_PER_STEP, tq, K_ROWS * GRID_W), bias_index)]),
        out_specs=q_spec,
        out_shape=jax.ShapeDtypeStruct((bsz, length, NA_WIDTH), BF16),
        compiler_params=_cparams(3),
        name="neighborhood_attention",
    )(q, *([k] * N_K_CHUNKS), *([v] * N_K_CHUNKS), kc, vc, bias_tables)


def _ctx_attn_kernel(q_ref, k_ref, v_ref, o_ref):
    n = q_ref.shape[1]
    o = _two_head_attention(q_ref[0], [k_ref[0]], [v_ref[0]], None, 0, n)
    o_ref[0] = o.astype(BF16)


def _context_attention(qc, kc, vc):
    bsz, n_ctx, _ = qc.shape
    spec = pl.BlockSpec((1, n_ctx, LANES), lambda b, hp: (b, 0, hp))
    return pl.pallas_call(
        _ctx_attn_kernel,
        grid=(bsz, NA_HEADS // HEADS_PER_STEP),
        in_specs=[spec, spec, spec],
        out_specs=spec,
        out_shape=jax.ShapeDtypeStruct((bsz, n_ctx, NA_WIDTH), BF16),
        compiler_params=_cparams(2),
        name="context_attention",
    )(qc, kc, vc)


def _dft_cos_sin(n):
    ang = 2.0 * np.pi * np.outer(np.arange(n), np.arange(n)) / n
    return np.cos(ang), np.sin(ang)


def _bf16_table(a):
    return jnp.asarray(a, F32).astype(BF16)


def _channel_dft_matrix():
    c, s = _dft_cos_sin(F_GROUP_DIM)
    scale = F_GROUP_DIM ** -0.5
    eye = np.eye(F_GROUPS)
    return np.concatenate([np.kron(eye, c), np.kron(eye, s)], axis=1) * scale


def _fft_stage1_kernel(f_ref, w1_ref, cs_ref, sc_ref, tc_ref, ts_ref, zr_ref, zi_ref, *, n_slow, nt):
    x = jnp.concatenate([f_ref[0, :, t * F_WIDTH:(t + 1) * F_WIDTH] for t in range(nt)], axis=0)
    ab = _dot(x, w1_ref[...]).astype(BF16)
    for t in range(nt):
        a = ab[t * n_slow:(t + 1) * n_slow, :F_WIDTH]
        b = ab[t * n_slow:(t + 1) * n_slow, F_WIDTH:]
        z = _dot(cs_ref[...], a) + _dot(sc_ref[...], b)
        zr, zi = z[:n_slow], z[n_slow:]
        c, s = tc_ref[t], ts_ref[t]
        zr_ref[0, t] = (zr * c - zi * s).astype(BF16)
        zi_ref[0, t] = (zr * s + zi * c).astype(BF16)


def _fft_stage2_kernel(zr_ref, zi_ref, fc_ref, fs_ref, y_ref):
    y_ref[0] = (_dot(fc_ref[...], zr_ref[0]) + _dot(fs_ref[...], zi_ref[0])).astype(BF16)


def _fourier_positions(f, n_slow, n_fast):
    bsz, n, _ = f.shape
    assert n == n_slow * n_fast
    nt = 8
    w1 = _bf16_table(_channel_dft_matrix())
    c1, s1 = _dft_cos_sin(n_slow)
    sc1 = n_slow ** -0.5
    cs = _bf16_table(np.concatenate([c1, s1], axis=0) * sc1)
    sc = _bf16_table(np.concatenate([-s1, c1], axis=0) * sc1)
    tw = 2.0 * np.pi * np.outer(np.arange(n_fast), np.arange(n_slow)) / n
    tc = jnp.asarray(np.cos(tw)[:, :, None], F32)
    ts = jnp.asarray(np.sin(tw)[:, :, None], F32)
    z_shape = jax.ShapeDtypeStruct((bsz, n_fast, n_slow, F_WIDTH), BF16)
    z_spec = pl.BlockSpec((1, nt, n_slow, F_WIDTH), lambda b, j: (b, j, 0, 0))
    const2 = lambda b, j: (0, 0)
    tw_spec = pl.BlockSpec((nt, n_slow, 1), lambda b, j: (j, 0, 0))
    zr, zi = pl.pallas_call(
        functools.partial(_fft_stage1_kernel, n_slow=n_slow, nt=nt),
        grid=(bsz, n_fast // nt),
        in_specs=[pl.BlockSpec((1, n_slow, nt * F_WIDTH), lambda b, j: (b, 0, j)),
                  pl.BlockSpec(w1.shape, const2), pl.BlockSpec(cs.shape, const2), pl.BlockSpec(sc.shape, const2),
                  tw_spec, tw_spec],
        out_specs=[z_spec, z_spec],
        out_shape=[z_shape, z_shape],
        compiler_params=_cparams(2),
        name="fnet_stage1",
    )(f.reshape(bsz, n_slow, n_fast * F_WIDTH), w1, cs, sc, tc, ts)

    c2, s2 = _dft_cos_sin(n_fast)
    sc2 = n_fast ** -0.5
    fc = _bf16_table(c2 * sc2)
    fs = _bf16_table(-s2 * sc2)
    cols = n_slow * F_WIDTH
    wcol = min(cols, 8192)
    blk = pl.BlockSpec((1, n_fast, wcol), lambda b, j: (b, 0, j))
    y = pl.pallas_call(
        _fft_stage2_kernel,
        grid=(bsz, cols // wcol),
        in_specs=[blk, blk, pl.BlockSpec(fc.shape, const2), pl.BlockSpec(fs.shape, const2)],
        out_specs=blk,
        out_shape=jax.ShapeDtypeStruct((bsz, n_fast, cols), BF16),
        compiler_params=_cparams(2),
        name="fnet_stage2",
    )(zr.reshape(bsz, n_fast, cols), zi.reshape(bsz, n_fast, cols), fc, fs)
    return y.reshape(bsz, n, F_WIDTH)


def _second_largest_sum(a, b, c, d):
    mab, nab = jnp.maximum(a, b), jnp.minimum(a, b)
    mcd, ncd = jnp.maximum(c, d), jnp.minimum(c, d)
    return jnp.maximum(mab, mcd) + jnp.maximum(jnp.minimum(mab, mcd), jnp.maximum(nab, ncd))


def _route(s, sb):
    s_rows = [s[e:e + 1] for e in range(N_EXPERTS)]
    sb_rows = [sb[e:e + 1] for e in range(N_EXPERTS)]
    epg = EXPERTS_PER_GROUP
    g_score = [_second_largest_sum(*sb_rows[g * epg:(g + 1) * epg]) for g in range(N_GROUPS)]
    best = functools.reduce(jnp.maximum, g_score)
    is_g, taken = [], None
    for g in range(N_GROUPS):
        hit = g_score[g] == best
        is_g.append(hit if taken is None else hit & jnp.logical_not(taken))
        taken = hit if taken is None else taken | hit

    def pick(rows, j):
        out = rows[(N_GROUPS - 1) * epg + j]
        for g in range(N_GROUPS - 2, -1, -1):
            out = jnp.where(is_g[g], rows[g * epg + j], out)
        return out

    cand_sb = [pick(sb_rows, j) for j in range(epg)]
    cand_s = [pick(s_rows, j) for j in range(epg)]
    w = []
    for j in range(epg):
        rank = jnp.zeros_like(cand_sb[j])
        for i in range(epg):
            if i == j:
                continue
            ahead = (cand_sb[i] > cand_sb[j]) | ((cand_sb[i] == cand_sb[j]) & (i < j))
            rank = rank + jnp.where(ahead, 1.0, 0.0)
        w.append(jnp.where(rank < 2.0, cand_s[j], 0.0))
    total = functools.reduce(jnp.add, w)
    w = [wj / total for wj in w]
    return [jnp.where(is_g[e // epg], w[e % epg], 0.0) for e in range(N_EXPERTS)]


def _out_kernel(a_ref, y_ref, x_ref, m_ref, wo_ref, bd_ref, g_ref, b_ref, wr_ref, rb_ref,
                x1_ref, h2_ref, comb_ref):
    y2 = _dot(y_ref[0], bd_ref[...]).astype(BF16)
    o = _dot(a_ref[0], wo_ref[:NA_WIDTH, :]) + _dot(y2, wo_ref[NA_WIDTH:, :])
    z = DEEPNORM_ALPHA * x_ref[0] + m_ref[0, 2:3, :] * o
    x1 = _layer_norm(z) * g_ref[...] + b_ref[...]
    x1_ref[0] = x1
    h2 = (_layer_norm(x1) * (1.0 + m_ref[0, 4:5, :]) + m_ref[0, 3:4, :]).astype(BF16)
    h2_ref[0] = h2
    s = jax.nn.sigmoid(_dot_nt(wr_ref[...], h2))
    for e, row in enumerate(_route(s, s + rb_ref[...])):
        comb_ref[0, e:e + 1, :] = row


def _out_projection(attn, yf, x, m, w_out_bf16, w_four_bd, ln_g, ln_b, w_router_t, router_bias, tm):
    bsz, length, _ = x.shape
    row = lambda b, i: (b, i, 0)
    const2 = lambda b, i: (0, 0)
    return pl.pallas_call(
        _out_kernel,
        grid=(bsz, length // tm),
        in_specs=[pl.BlockSpec((1, tm, NA_WIDTH), row), pl.BlockSpec((1, tm, F_WIDTH), row),
                  pl.BlockSpec((1, tm, D_MODEL), row),
                  pl.BlockSpec((1, N_MOD, D_MODEL), lambda b, i: (b, 0, 0)),
                  pl.BlockSpec((NA_WIDTH + F_WIDTH, D_MODEL), const2),
                  pl.BlockSpec((F_WIDTH, F_WIDTH), const2),
                  pl.BlockSpec((1, D_MODEL), const2), pl.BlockSpec((1, D_MODEL), const2),
                  pl.BlockSpec((N_EXPERTS, D_MODEL), const2), pl.BlockSpec((N_EXPERTS, 1), const2)],
        out_specs=[pl.BlockSpec((1, tm, D_MODEL), row), pl.BlockSpec((1, tm, D_MODEL), row),
                   pl.BlockSpec((1, N_EXPERTS, tm), lambda b, i: (b, 0, i))],
        out_shape=[jax.ShapeDtypeStruct((bsz, length, D_MODEL), F32),
                   jax.ShapeDtypeStruct((bsz, length, D_MODEL), BF16),
                   jax.ShapeDtypeStruct((bsz, N_EXPERTS, length), F32)],
        compiler_params=_cparams(2),
        name="out_proj_norm_route",
    )(attn, yf, x, m, w_out_bf16, w_four_bd, ln_g.reshape(1, D_MODEL), ln_b.reshape(1, D_MODEL),
      w_router_t, router_bias.reshape(N_EXPERTS, 1))


def _moe_kernel(h_ref, c_ref, wg_ref, wu_ref, wd_ref, x1_ref, m_ref, g_ref, b_ref, o_ref, acc_ref):
    e = pl.program_id(2)

    @pl.when(e == 0)
    def _():
        acc_ref[...] = jnp.zeros_like(acc_ref)

    h = h_ref[0]
    gate = _dot(h, wg_ref[0])
    up = _dot(h, wu_ref[0])
    comb = c_ref[0]
    lane = lax.broadcasted_iota(jnp.int32, comb.shape, 1)
    cw = jnp.sum(jnp.where(lane == e, comb, 0.0), axis=-1, keepdims=True)
    hid = (gate * jax.nn.sigmoid(gate)) * up * cw
    acc_ref[...] += _dot(hid.astype(BF16), wd_ref[0])

    @pl.when(e == N_EXPERTS - 1)
    def _():
        z = DEEPNORM_ALPHA * x1_ref[0] + m_ref[0, 5:6, :] * acc_ref[...]
        o_ref[0] = _layer_norm(z) * g_ref[...] + b_ref[...]


def _moe(h2, comb, x1, m, w_gate, w_up, w_down, ln_g, ln_b, tm):
    bsz, length, _ = x1.shape
    row = lambda b, i, e: (b, i, 0)
    const2 = lambda b, i, e: (0, 0)
    return pl.pallas_call(
        _moe_kernel,
        grid=(bsz, length // tm, N_EXPERTS),
        in_specs=[pl.BlockSpec((1, tm, D_MODEL), row), pl.BlockSpec((1, tm, N_EXPERTS), row),
                  pl.BlockSpec((1, D_MODEL, D_EXPERT), lambda b, i, e: (e, 0, 0)),
                  pl.BlockSpec((1, D_MODEL, D_EXPERT), lambda b, i, e: (e, 0, 0)),
                  pl.BlockSpec((1, D_EXPERT, D_MODEL), lambda b, i, e: (e, 0, 0)),
                  pl.BlockSpec((1, tm, D_MODEL), row),
                  pl.BlockSpec((1, N_MOD, D_MODEL), lambda b, i, e: (b, 0, 0)),
                  pl.BlockSpec((1, D_MODEL), const2), pl.BlockSpec((1, D_MODEL), const2)],
        out_specs=pl.BlockSpec((1, tm, D_MODEL), row),
        out_shape=jax.ShapeDtypeStruct((bsz, length, D_MODEL), F32),
        scratch_shapes=[pltpu.VMEM((tm, D_MODEL), F32)],
        compiler_params=_cparams(3),
        name="moe_experts_norm",
    )(h2, comb, w_gate, w_up, w_down, x1, m, ln_g.reshape(1, D_MODEL), ln_b.reshape(1, D_MODEL))


def _ctx_fourier_kernel(f_ref, w1_ref, c_ref, s_ref, y_ref):
    ab = _dot(f_ref[0], w1_ref[...]).astype(BF16)
    y = _dot(c_ref[...], ab[:, :F_WIDTH]) + _dot(s_ref[...], ab[:, F_WIDTH:])
    y_ref[0] = y.astype(BF16)


def _context_fourier(fc):
    bsz, n, _ = fc.shape
    w1 = _bf16_table(_channel_dft_matrix())
    c, s = _dft_cos_sin(n)
    cm = _bf16_table(c * n ** -0.5)
    sm = _bf16_table(-s * n ** -0.5)
    const2 = lambda b: (0, 0)
    blk = pl.BlockSpec((1, n, F_WIDTH), lambda b: (b, 0, 0))
    return pl.pallas_call(
        _ctx_fourier_kernel,
        grid=(bsz,),
        in_specs=[blk, pl.BlockSpec(w1.shape, const2), pl.BlockSpec(cm.shape, const2),
                  pl.BlockSpec(sm.shape, const2)],
        out_specs=blk,
        out_shape=jax.ShapeDtypeStruct((bsz, n, F_WIDTH), BF16),
        compiler_params=_cparams(1),
        name="context_fnet",
    )(fc, w1, cm, sm)


def _block_diag(w):
    g, c, _ = w.shape
    eye = jnp.eye(g, dtype=w.dtype)
    return (eye[:, None, :, None] * w[:, :, None, :]).reshape(g * c, g * c)


def kernel(x, c, ctx, c_ctx, w_mod, b_mod, w_in, rpb, w_four, w_out, ln1_g, ln1_b, ln2_g, ln2_b,
           w_router, router_bias, w_gate, w_up, w_down):
    bsz, length, _ = x.shape
    n_ctx = ctx.shape[1]
    rows = length // GRID_W

    cvec = jnp.concatenate([c, c_ctx[None, :], jnp.zeros((8 - bsz - 1, D_MODEL), F32)], axis=0)
    mods = _modulation(cvec, w_mod, b_mod)
    w_router_t = w_router.T.astype(BF16)

    xc = ctx
    for i in range(DEPTH):
        last = i == DEPTH - 1
        m = mods[i, :bsz].reshape(bsz, N_MOD, D_MODEL)
        mc = jnp.broadcast_to(mods[i, bsz].reshape(1, N_MOD, D_MODEL), (bsz, N_MOD, D_MODEL))
        w_in_b = w_in[i].astype(BF16)
        w_out_b = w_out[i].astype(BF16)
        w_four_bd = _block_diag(w_four[i]).astype(BF16)
        wg, wu, wd = w_gate[i].astype(BF16), w_up[i].astype(BF16), w_down[i].astype(BF16)

        q, k, v, f = _in_projection(x, m, w_in_b, tm=512)
        qc, kc, vc, fc = _in_projection(xc, mc, w_in_b, tm=n_ctx)

        attn = _neighborhood_attention(q, k, v, kc, vc, _na_bias_tables(rpb[i], rows))
        yf = _fourier_positions(f, n_slow=rows, n_fast=GRID_W)
        x1, h2, comb = _out_projection(attn, yf, x, m, w_out_b, w_four_bd, ln1_g[i], ln1_b[i],
                                       w_router_t, router_bias, tm=512)
        x = _moe(h2, jnp.swapaxes(comb, 1, 2), x1, m, wg, wu, wd, ln2_g[i], ln2_b[i], tm=512)
        if last:
            break

        attn_c = _context_attention(qc, kc, vc)
        yc = _context_fourier(fc)
        xc1, h2c, comb_c = _out_projection(attn_c, yc, xc, mc, w_out_b, w_four_bd, ln1_g[i], ln1_b[i],
                                           w_router_t, router_bias, tm=n_ctx)
        xc = _moe(h2c, jnp.swapaxes(comb_c, 1, 2), xc1, mc, wg, wu, wd, ln2_g[i], ln2_b[i], tm=n_ctx)
    return x
```

```python
import functools
import math

import numpy as np
import jax
import jax.numpy as jnp
from jax import lax
from jax.experimental import pallas as pl
from jax.experimental.pallas import tpu as pltpu

D_MODEL = 1024
DEPTH = 2
GRID_W = 64
NA_HEADS = 8
HEAD_DIM = 64
NA_WIDTH = NA_HEADS * HEAD_DIM
WIN_ROWS = 8
WIN_COLS = 16
F_GROUPS = 8
F_GROUP_DIM = 64
F_WIDTH = F_GROUPS * F_GROUP_DIM
IN_WIDTH = 3 * NA_WIDTH + F_WIDTH
N_EXPERTS = 16
N_GROUPS = 4
EXPERTS_PER_GROUP = N_EXPERTS // N_GROUPS
D_EXPERT = 256
N_MOD = 6
DEEPNORM_ALPHA = (2.0 * DEPTH) ** 0.25
LN_EPS = 1e-6

F32 = jnp.float32
BF16 = jnp.bfloat16

V7X_VMEM_BYTES = 64 * 1024 * 1024
VMEM_LIMIT_BYTES = (V7X_VMEM_BYTES * 3) // 4
LANES = 128
SUBLANES = 8
HEADS_PER_STEP = LANES // HEAD_DIM
MASK_VALUE = -1e30

Q_ROWS = 8
K_ROWS = 16
K_CHUNK_ROWS = 4
N_K_CHUNKS = K_ROWS // K_CHUNK_ROWS


def _cparams(n_grid_dims):
    return pltpu.CompilerParams(dimension_semantics=("arbitrary",) * n_grid_dims,
                                vmem_limit_bytes=VMEM_LIMIT_BYTES)


def _layer_norm(x):
    mu = jnp.mean(x, axis=-1, keepdims=True)
    xc = x - mu
    var = jnp.mean(xc * xc, axis=-1, keepdims=True)
    return xc * lax.rsqrt(var + LN_EPS)


def _dot(a, b):
    return jnp.dot(a, b, preferred_element_type=F32)


def _dot_nt(a, b):
    return lax.dot_general(a, b, (((1,), (1,)), ((), ())), preferred_element_type=F32)


def _mod_kernel(c_ref, w_ref, b_ref, o_ref):
    c = c_ref[...]
    a = c * jax.nn.sigmoid(c)
    o_ref[0] = jnp.dot(a, w_ref[0], preferred_element_type=F32, precision=lax.Precision.HIGHEST) + b_ref[0]


def _modulation(cvec, w_mod, b_mod):
    n_col_blocks = 4
    wc = (N_MOD * D_MODEL) // n_col_blocks
    rows = cvec.shape[0]
    return pl.pallas_call(
        _mod_kernel,
        grid=(DEPTH, n_col_blocks),
        in_specs=[pl.BlockSpec((rows, D_MODEL), lambda i, j: (0, 0)),
                  pl.BlockSpec((1, D_MODEL, wc), lambda i, j: (i, 0, j)),
                  pl.BlockSpec((1, 1, wc), lambda i, j: (i, 0, j))],
        out_specs=pl.BlockSpec((1, rows, wc), lambda i, j: (i, 0, j)),
        out_shape=jax.ShapeDtypeStruct((DEPTH, rows, N_MOD * D_MODEL), F32),
        compiler_params=_cparams(2),
        name="modulation",
    )(cvec, w_mod, b_mod.reshape(DEPTH, 1, N_MOD * D_MODEL))


def _proj_kernel(x_ref, m_ref, w_ref, q_ref, k_ref, v_ref, f_ref):
    h = _layer_norm(x_ref[0]) * (1.0 + m_ref[0, 1:2, :]) + m_ref[0, 0:1, :]
    p = _dot(h.astype(BF16), w_ref[...])
    q_ref[0] = (p[:, :NA_WIDTH] * (HEAD_DIM ** -0.5)).astype(BF16)
    k_ref[0] = p[:, NA_WIDTH:2 * NA_WIDTH].astype(BF16)
    v_ref[0] = p[:, 2 * NA_WIDTH:3 * NA_WIDTH].astype(BF16)
    f_ref[0] = p[:, 3 * NA_WIDTH:].astype(BF16)


def _in_projection(x, m, w_in_bf16, tm):
    bsz, length, _ = x.shape
    out = jax.ShapeDtypeStruct((bsz, length, NA_WIDTH), BF16)
    o_spec = pl.BlockSpec((1, tm, NA_WIDTH), lambda b, i: (b, i, 0))
    return pl.pallas_call(
        _proj_kernel,
        grid=(bsz, length // tm),
        in_specs=[pl.BlockSpec((1, tm, D_MODEL), lambda b, i: (b, i, 0)),
                  pl.BlockSpec((1, N_MOD, D_MODEL), lambda b, i: (b, 0, 0)),
                  pl.BlockSpec((D_MODEL, IN_WIDTH), lambda b, i: (0, 0))],
        out_specs=[o_spec, o_spec, o_spec, o_spec],
        out_shape=[out, out, out, out],
        compiler_params=_cparams(2),
        name="ln_mod_in_proj",
    )(x, m, w_in_bf16)


def _softmax_pv(qh, keys, values, bias):
    scores = [_dot_nt(qh, k) for k in keys]
    scores[0] = scores[0] + bias if bias is not None else scores[0]
    m = functools.reduce(jnp.maximum, [jnp.max(s, axis=-1, keepdims=True) for s in scores])
    probs = [jnp.exp(s - m) for s in scores]
    denom = functools.reduce(jnp.add, [jnp.sum(p, axis=-1, keepdims=True) for p in probs])
    o = functools.reduce(jnp.add, [_dot(p.astype(BF16), v) for p, v in zip(probs, values)])
    return o / denom


def _two_head_attention(q, keys, values, bias_ref, q_start, q_len):
    lane = lax.broadcasted_iota(jnp.int32, (1, LANES), 1)
    out = None
    for h in range(HEADS_PER_STEP):
        in_head = (lane >= HEAD_DIM * h) & (lane < HEAD_DIM * (h + 1))
        qh = jnp.where(in_head, q, jnp.zeros_like(q))
        bias = None if bias_ref is None else bias_ref[0, h, q_start:q_start + q_len, :]
        o = _softmax_pv(qh, keys, values, bias)
        out = o if out is None else jnp.where(in_head, o, out)
    return out


def _na_kernel(q_ref, *refs):
    k_refs = refs[:N_K_CHUNKS]
    v_refs = refs[N_K_CHUNKS:2 * N_K_CHUNKS]
    kc_ref, vc_ref, bias_ref, o_ref = refs[2 * N_K_CHUNKS:]
    k = jnp.concatenate([r[0] for r in k_refs], axis=0)
    v = jnp.concatenate([r[0] for r in v_refs], axis=0)
    keys = [k, kc_ref[0]]
    values = [v, vc_ref[0]]
    q_len = 256
    for q_start in range(0, Q_ROWS * GRID_W, q_len):
        q = q_ref[0, q_start:q_start + q_len, :]
        o = _two_head_attention(q, keys, values, bias_ref, q_start, q_len)
        o_ref[0, q_start:q_start + q_len, :] = o.astype(BF16)


def _na_bias_tables(rpb, rows):
    n_rb = rows // Q_ROWS
    kh = min(WIN_ROWS, rows)
    qc, kc = np.arange(GRID_W)[:, None], np.arange(GRID_W)[None, :]
    cs = np.clip(qc - WIN_COLS // 2, 0, GRID_W - WIN_COLS)
    col_valid = jnp.asarray((kc >= cs) & (kc < cs + WIN_COLS))
    pad = GRID_W - WIN_COLS
    rpb_pad = jnp.pad(rpb.astype(F32), ((0, 0), (0, 0), (pad, pad)))
    toeplitz = jnp.stack([rpb_pad[:, :, GRID_W - 1 - c:2 * GRID_W - 1 - c] for c in range(GRID_W)], axis=2)
    toeplitz = jnp.where(col_valid, toeplitz, MASK_VALUE)
    masked = jnp.full((rpb.shape[0], GRID_W, GRID_W), MASK_VALUE, F32)
    tables = []
    for rb in (0, 1, n_rb - 1):
        block_rows = []
        for qi in range(Q_ROWS):
            qr = rb * Q_ROWS + qi
            rs = int(np.clip(qr - kh // 2, 0, rows - kh))
            blocks = []
            for kj in range(K_ROWS):
                kr = _key_row_start(rb, rows) + kj
                blocks.append(toeplitz[:, kr - qr + WIN_ROWS - 1] if rs <= kr < rs + kh else masked)
            block_rows.append(jnp.concatenate(blocks, axis=-1))
        tables.append(jnp.concatenate(block_rows, axis=1))
    return jnp.stack(tables)


def _key_row_start(rb, rows):
    return int(np.clip(rb * Q_ROWS - (K_ROWS - Q_ROWS) // 2, 0, rows - K_ROWS))


def _neighborhood_attention(q, k, v, kc, vc, bias_tables):
    bsz, length, _ = q.shape
    rows = length // GRID_W
    n_rb = rows // Q_ROWS
    assert rows % Q_ROWS == 0 and rows >= K_ROWS + Q_ROWS and n_rb >= 3
    n_ctx = kc.shape[1]
    tq = Q_ROWS * GRID_W
    tk = K_CHUNK_ROWS * GRID_W
    max_chunk = (rows - K_ROWS) // K_CHUNK_ROWS
    half = (K_ROWS - Q_ROWS) // 2 // K_CHUNK_ROWS

    def kv_spec(j):
        def index(b, hp, rb):
            start = jnp.clip(rb * (Q_ROWS // K_CHUNK_ROWS) - half, 0, max_chunk)
            return (b, start + j, hp)
        return pl.BlockSpec((1, tk, LANES), index)

    def bias_index(b, hp, rb):
        variant = jnp.where(rb == 0, 0, jnp.where(rb == n_rb - 1, 2, 1))
        return (variant, hp, 0, 0)

    ctx_spec = pl.BlockSpec((1, n_ctx, LANES), lambda b, hp, rb: (b, 0, hp))
    q_spec = pl.BlockSpec((1, tq, LANES), lambda b, hp, rb: (b, rb, hp))
    return pl.pallas_call(
        _na_kernel,
        grid=(bsz, NA_HEADS // HEADS_PER_STEP, n_rb),
        in_specs=([q_spec] + [kv_spec(j) for j in range(N_K_CHUNKS)] + [kv_spec(j) for j in range(N_K_CHUNKS)]
                  + [ctx_spec, ctx_spec,
                     pl.BlockSpec((1, HEADS_PER_STEP, tq, K_ROWS * GRID_W), bias_index)]),
        out_specs=q_spec,
        out_shape=jax.ShapeDtypeStruct((bsz, length, NA_WIDTH), BF16),
        compiler_params=_cparams(3),
        name="neighborhood_attention",
    )(q, *([k] * N_K_CHUNKS), *([v] * N_K_CHUNKS), kc, vc, bias_tables)


def _ctx_attn_kernel(q_ref, k_ref, v_ref, o_ref):
    n = q_ref.shape[1]
    o = _two_head_attention(q_ref[0], [k_ref[0]], [v_ref[0]], None, 0, n)
    o_ref[0] = o.astype(BF16)


def _context_attention(qc, kc, vc):
    bsz, n_ctx, _ = qc.shape
    spec = pl.BlockSpec((1, n_ctx, LANES), lambda b, hp: (b, 0, hp))
    return pl.pallas_call(
        _ctx_attn_kernel,
        grid=(bsz, NA_HEADS // HEADS_PER_STEP),
        in_specs=[spec, spec, spec],
        out_specs=spec,
        out_shape=jax.ShapeDtypeStruct((bsz, n_ctx, NA_WIDTH), BF16),
        compiler_params=_cparams(2),
        name="context_attention",
    )(qc, kc, vc)


def _dft_cos_sin(n):
    ang = 2.0 * np.pi * np.outer(np.arange(n), np.arange(n)) / n
    return np.cos(ang), np.sin(ang)


def _bf16_table(a):
    return jnp.asarray(a, F32).astype(BF16)


def _channel_dft_matrix():
    c, s = _dft_cos_sin(F_GROUP_DIM)
    scale = F_GROUP_DIM ** -0.5
    eye = np.eye(F_GROUPS)
    return np.concatenate([np.kron(eye, c), np.kron(eye, s)], axis=1) * scale


def _fft_stage1_kernel(f_ref, w1_ref, cs_ref, sc_ref, tc_ref, ts_ref, zr_ref, zi_ref, *, n_slow, nt):
    x = jnp.concatenate([f_ref[0, :, t * F_WIDTH:(t + 1) * F_WIDTH] for t in range(nt)], axis=0)
    ab = _dot(x, w1_ref[...]).astype(BF16)
    for t in range(nt):
        a = ab[t * n_slow:(t + 1) * n_slow, :F_WIDTH]
        b = ab[t * n_slow:(t + 1) * n_slow, F_WIDTH:]
        z = _dot(cs_ref[...], a) + _dot(sc_ref[...], b)
        zr, zi = z[:n_slow], z[n_slow:]
        c, s = tc_ref[t], ts_ref[t]
        zr_ref[0, t] = (zr * c - zi * s).astype(BF16)
        zi_ref[0, t] = (zr * s + zi * c).astype(BF16)


def _fft_stage2_kernel(zr_ref, zi_ref, fc_ref, fs_ref, y_ref):
    y_ref[0] = (_dot(fc_ref[...], zr_ref[0]) + _dot(fs_ref[...], zi_ref[0])).astype(BF16)


def _fourier_positions(f, n_slow, n_fast):
    bsz, n, _ = f.shape
    assert n == n_slow * n_fast
    nt = 8
    w1 = _bf16_table(_channel_dft_matrix())
    c1, s1 = _dft_cos_sin(n_slow)
    sc1 = n_slow ** -0.5
    cs = _bf16_table(np.concatenate([c1, s1], axis=0) * sc1)
    sc = _bf16_table(np.concatenate([-s1, c1], axis=0) * sc1)
    tw = 2.0 * np.pi * np.outer(np.arange(n_fast), np.arange(n_slow)) / n
    tc = jnp.asarray(np.cos(tw)[:, :, None], F32)
    ts = jnp.asarray(np.sin(tw)[:, :, None], F32)
    z_shape = jax.ShapeDtypeStruct((bsz, n_fast, n_slow, F_WIDTH), BF16)
    z_spec = pl.BlockSpec((1, nt, n_slow, F_WIDTH), lambda b, j: (b, j, 0, 0))
    const2 = lambda b, j: (0, 0)
    tw_spec = pl.BlockSpec((nt, n_slow, 1), lambda b, j: (j, 0, 0))
    zr, zi = pl.pallas_call(
        functools.partial(_fft_stage1_kernel, n_slow=n_slow, nt=nt),
        grid=(bsz, n_fast // nt),
        in_specs=[pl.BlockSpec((1, n_slow, nt * F_WIDTH), lambda b, j: (b, 0, j)),
                  pl.BlockSpec(w1.shape, const2), pl.BlockSpec(cs.shape, const2), pl.BlockSpec(sc.shape, const2),
                  tw_spec, tw_spec],
        out_specs=[z_spec, z_spec],
        out_shape=[z_shape, z_shape],
        compiler_params=_cparams(2),
        name="fnet_stage1",
    )(f.reshape(bsz, n_slow, n_fast * F_WIDTH), w1, cs, sc, tc, ts)

    c2, s2 = _dft_cos_sin(n_fast)
    sc2 = n_fast ** -0.5
    fc = _bf16_table(c2 * sc2)
    fs = _bf16_table(-s2 * sc2)
    cols = n_slow * F_WIDTH
    wcol = min(cols, 8192)
    blk = pl.BlockSpec((1, n_fast, wcol), lambda b, j: (b, 0, j))
    y = pl.pallas_call(
        _fft_stage2_kernel,
        grid=(bsz, cols // wcol),
        in_specs=[blk, blk, pl.BlockSpec(fc.shape, const2), pl.BlockSpec(fs.shape, const2)],
        out_specs=blk,
        out_shape=jax.ShapeDtypeStruct((bsz, n_fast, cols), BF16),
        compiler_params=_cparams(2),
        name="fnet_stage2",
    )(zr.reshape(bsz, n_fast, cols), zi.reshape(bsz, n_fast, cols), fc, fs)
    return y.reshape(bsz, n, F_WIDTH)


def _second_largest_sum(a, b, c, d):
    mab, nab = jnp.maximum(a, b), jnp.minimum(a, b)
    mcd, ncd = jnp.maximum(c, d), jnp.minimum(c, d)
    return jnp.maximum(mab, mcd) + jnp.maximum(jnp.minimum(mab, mcd), jnp.maximum(nab, ncd))


def _selected_group(sb_rows):
    epg = EXPERTS_PER_GROUP
    g_score = [_second_largest_sum(*sb_rows[g * epg:(g + 1) * epg]) for g in range(N_GROUPS)]
    best = functools.reduce(jnp.maximum, g_score)
    group = jnp.full_like(best, float(N_GROUPS - 1))
    for g in range(N_GROUPS - 2, -1, -1):
        group = jnp.where(g_score[g] == best, float(g), group)
    return group


def _top2_gates(cand_s, cand_sb):
    n = len(cand_s)
    w = []
    for j in range(n):
        rank = jnp.zeros_like(cand_sb[j])
        for i in range(n):
            if i == j:
                continue
            ahead = (cand_sb[i] > cand_sb[j]) | ((cand_sb[i] == cand_sb[j]) & (i < j))
            rank = rank + jnp.where(ahead, 1.0, 0.0)
        w.append(jnp.where(rank < 2.0, cand_s[j], 0.0))
    total = functools.reduce(jnp.add, w)
    return [wj / total for wj in w]


def _out_kernel(a_ref, y_ref, x_ref, m_ref, wo_ref, bd_ref, g_ref, b_ref, wr_ref, rb_ref,
                x1_ref, h2_ref, grp_ref):
    y2 = _dot(y_ref[0], bd_ref[...]).astype(BF16)
    o = _dot(a_ref[0], wo_ref[:NA_WIDTH, :]) + _dot(y2, wo_ref[NA_WIDTH:, :])
    z = DEEPNORM_ALPHA * x_ref[0] + m_ref[0, 2:3, :] * o
    x1 = _layer_norm(z) * g_ref[...] + b_ref[...]
    x1_ref[0] = x1
    h2 = _layer_norm(x1) * (1.0 + m_ref[0, 4:5, :]) + m_ref[0, 3:4, :]
    h2_ref[0] = h2
    sb = jax.nn.sigmoid(_dot_nt(wr_ref[...], h2.astype(BF16))) + rb_ref[...]
    group = _selected_group([sb[e:e + 1] for e in range(N_EXPERTS)])
    grp_ref[0] = jnp.broadcast_to(group, grp_ref.shape[1:])


def _out_projection(attn, yf, x, m, w_out_bf16, w_four_bd, ln_g, ln_b, w_router_t, router_bias, tm):
    bsz, length, _ = x.shape
    row = lambda b, i: (b, i, 0)
    const2 = lambda b, i: (0, 0)
    return pl.pallas_call(
        _out_kernel,
        grid=(bsz, length // tm),
        in_specs=[pl.BlockSpec((1, tm, NA_WIDTH), row), pl.BlockSpec((1, tm, F_WIDTH), row),
                  pl.BlockSpec((1, tm, D_MODEL), row),
                  pl.BlockSpec((1, N_MOD, D_MODEL), lambda b, i: (b, 0, 0)),
                  pl.BlockSpec((NA_WIDTH + F_WIDTH, D_MODEL), const2),
                  pl.BlockSpec((F_WIDTH, F_WIDTH), const2),
                  pl.BlockSpec((1, D_MODEL), const2), pl.BlockSpec((1, D_MODEL), const2),
                  pl.BlockSpec((N_EXPERTS, D_MODEL), const2), pl.BlockSpec((N_EXPERTS, 1), const2)],
        out_specs=[pl.BlockSpec((1, tm, D_MODEL), row), pl.BlockSpec((1, tm, D_MODEL), row),
                   pl.BlockSpec((1, SUBLANES, tm), lambda b, i: (b, 0, i))],
        out_shape=[jax.ShapeDtypeStruct((bsz, length, D_MODEL), F32),
                   jax.ShapeDtypeStruct((bsz, length, D_MODEL), F32),
                   jax.ShapeDtypeStruct((bsz, SUBLANES, length), F32)],
        compiler_params=_cparams(2),
        name="out_proj_norm_route",
    )(attn, yf, x, m, w_out_bf16, w_four_bd, ln_g.reshape(1, D_MODEL), ln_b.reshape(1, D_MODEL),
      w_router_t, router_bias.reshape(N_EXPERTS, 1))


def _group_plan(group, tm):
    n = group.shape[0]
    n_slots = n + N_GROUPS * tm
    onehot = (group[:, None] == jnp.arange(N_GROUPS, dtype=jnp.int32)[None, :]).astype(jnp.int32)
    csum = jnp.cumsum(onehot, axis=0)
    rank = jnp.sum(onehot * csum, axis=1) - 1
    padded = ((csum[-1] + tm - 1) // tm) * tm
    end = jnp.cumsum(padded)
    start = end - padded
    slot = jnp.sum(onehot * start[None, :], axis=1) + rank
    src = jnp.zeros((n_slots,), jnp.int32).at[slot].set(jnp.arange(n, dtype=jnp.int32))
    tile_start = jnp.arange(n_slots // tm, dtype=jnp.int32) * tm
    tile_group = jnp.sum((end[None, :] <= tile_start[:, None]).astype(jnp.int32), axis=1)
    return slot, src, jnp.minimum(tile_group, N_GROUPS - 1)


def _row_gather_start(idx_ref, base, src_hbm, dst_ref, sem, n_rows):
    def body(r, carry):
        pltpu.make_async_copy(src_hbm.at[pl.ds(idx_ref[base + r], 1), :], dst_ref.at[pl.ds(r, 1), :], sem).start()
        return carry
    lax.fori_loop(0, n_rows, body, 0)


def _row_gather_wait(src_hbm, dst_ref, sem, n_rows):
    pltpu.make_async_copy(src_hbm.at[pl.ds(0, n_rows), :], dst_ref, sem).wait()


def _pipelined_row_gather(idx_ref, src_hbm, buf, sems, step, n_steps, tm):
    slot = step % 2

    @pl.when(step == 0)
    def _():
        _row_gather_start(idx_ref, 0, src_hbm, buf.at[0], sems.at[0], tm)

    @pl.when(step + 1 < n_steps)
    def _():
        _row_gather_start(idx_ref, (step + 1) * tm, src_hbm, buf.at[1 - slot], sems.at[1 - slot], tm)

    _row_gather_wait(src_hbm, buf.at[slot], sems.at[slot], tm)
    return buf.at[slot]


def _moe_group_kernel(src_ref, tile_group_ref, h_hbm, wg_ref, wu_ref, wd_ref, wr_ref, rb_ref, o_ref,
                      hbuf, sems, *, tm, n_tiles):
    del tile_group_ref
    h_rows = _pipelined_row_gather(src_ref, h_hbm, hbuf, sems, pl.program_id(0), n_tiles, tm)
    h = h_rows[...].astype(BF16)
    s = jax.nn.sigmoid(_dot(h, wr_ref[0]))
    sb = s + rb_ref[0]
    epg = EXPERTS_PER_GROUP
    gates = _top2_gates([s[:, j:j + 1] for j in range(epg)], [sb[:, j:j + 1] for j in range(epg)])
    acc = None
    for e in range(epg):
        gate = _dot(h, wg_ref[e])
        up = _dot(h, wu_ref[e])
        hid = (gate * jax.nn.sigmoid(gate)) * up * gates[e]
        y = _dot(hid.astype(BF16), wd_ref[e])
        acc = y if acc is None else acc + y
    o_ref[...] = acc


def _moe_experts(h2, src, tile_group, w_gate, w_up, w_down, w_router_grp, router_bias_grp, tm):
    n_slots = src.shape[0]
    n_tiles = n_slots // tm
    epg = EXPERTS_PER_GROUP
    by_group = lambda i, src_ref, tg_ref: (tg_ref[i], 0, 0)
    grid_spec = pltpu.PrefetchScalarGridSpec(
        num_scalar_prefetch=2,
        grid=(n_tiles,),
        in_specs=[pl.BlockSpec(memory_space=pl.ANY),
                  pl.BlockSpec((epg, D_MODEL, D_EXPERT), by_group),
                  pl.BlockSpec((epg, D_MODEL, D_EXPERT), by_group),
                  pl.BlockSpec((epg, D_EXPERT, D_MODEL), by_group),
                  pl.BlockSpec((1, D_MODEL, LANES), by_group),
                  pl.BlockSpec((1, 1, LANES), by_group)],
        out_specs=pl.BlockSpec((tm, D_MODEL), lambda i, src_ref, tg_ref: (i, 0)),
        scratch_shapes=[pltpu.VMEM((2, tm, D_MODEL), F32), pltpu.SemaphoreType.DMA((2,))],
    )
    return pl.pallas_call(
        functools.partial(_moe_group_kernel, tm=tm, n_tiles=n_tiles),
        grid_spec=grid_spec,
        out_shape=jax.ShapeDtypeStruct((n_slots, D_MODEL), F32),
        compiler_params=_cparams(1),
        name="moe_group_experts",
    )(src, tile_group, h2, w_gate, w_up, w_down, w_router_grp, router_bias_grp)


def _unpermute_norm_kernel(slot_ref, y_hbm, x1_ref, m_ref, g_ref, b_ref, o_ref, ybuf, sems, *, tm, n_tiles):
    y_rows = _pipelined_row_gather(slot_ref, y_hbm, ybuf, sems, pl.program_id(0), n_tiles, tm)
    z = DEEPNORM_ALPHA * x1_ref[...] + m_ref[0, 5:6, :] * y_rows[...]
    o_ref[...] = _layer_norm(z) * g_ref[...] + b_ref[...]


def _unpermute_norm(y_slots, slot, x1, m, ln_g, ln_b, tm):
    bsz, length, _ = x1.shape
    n = bsz * length
    n_tiles = n // tm
    tiles_per_batch = length // tm
    const2 = lambda i, slot_ref: (0, 0)
    grid_spec = pltpu.PrefetchScalarGridSpec(
        num_scalar_prefetch=1,
        grid=(n_tiles,),
        in_specs=[pl.BlockSpec(memory_space=pl.ANY),
                  pl.BlockSpec((tm, D_MODEL), lambda i, slot_ref: (i, 0)),
                  pl.BlockSpec((1, N_MOD, D_MODEL), lambda i, slot_ref: (i // tiles_per_batch, 0, 0)),
                  pl.BlockSpec((1, D_MODEL), const2), pl.BlockSpec((1, D_MODEL), const2)],
        out_specs=pl.BlockSpec((tm, D_MODEL), lambda i, slot_ref: (i, 0)),
        scratch_shapes=[pltpu.VMEM((2, tm, D_MODEL), F32), pltpu.SemaphoreType.DMA((2,))],
    )
    out = pl.pallas_call(
        functools.partial(_unpermute_norm_kernel, tm=tm, n_tiles=n_tiles),
        grid_spec=grid_spec,
        out_shape=jax.ShapeDtypeStruct((n, D_MODEL), F32),
        compiler_params=_cparams(1),
        name="moe_unpermute_norm",
    )(slot, y_slots, x1.reshape(n, D_MODEL), m, ln_g.reshape(1, D_MODEL), ln_b.reshape(1, D_MODEL))
    return out.reshape(bsz, length, D_MODEL)


def _grouped_moe(h2, group_rows, x1, m, w_gate, w_up, w_down, w_router_grp, router_bias_grp, ln_g, ln_b,
                 tm_experts, tm_norm):
    bsz, length, _ = x1.shape
    group = group_rows[:, 0, :].reshape(bsz * length).astype(jnp.int32)
    slot, src, tile_group = _group_plan(group, tm_experts)
    y_slots = _moe_experts(h2.reshape(bsz * length, D_MODEL), src, tile_group, w_gate, w_up, w_down,
                           w_router_grp, router_bias_grp, tm_experts)
    return _unpermute_norm(y_slots, slot, x1, m, ln_g, ln_b, tm_norm)


def _ctx_fourier_kernel(f_ref, w1_ref, c_ref, s_ref, y_ref):
    ab = _dot(f_ref[0], w1_ref[...]).astype(BF16)
    y = _dot(c_ref[...], ab[:, :F_WIDTH]) + _dot(s_ref[...], ab[:, F_WIDTH:])
    y_ref[0] = y.astype(BF16)


def _context_fourier(fc):
    bsz, n, _ = fc.shape
    w1 = _bf16_table(_channel_dft_matrix())
    c, s = _dft_cos_sin(n)
    cm = _bf16_table(c * n ** -0.5)
    sm = _bf16_table(-s * n ** -0.5)
    const2 = lambda b: (0, 0)
    blk = pl.BlockSpec((1, n, F_WIDTH), lambda b: (b, 0, 0))
    return pl.pallas_call(
        _ctx_fourier_kernel,
        grid=(bsz,),
        in_specs=[blk, pl.BlockSpec(w1.shape, const2), pl.BlockSpec(cm.shape, const2),
                  pl.BlockSpec(sm.shape, const2)],
        out_specs=blk,
        out_shape=jax.ShapeDtypeStruct((bsz, n, F_WIDTH), BF16),
        compiler_params=_cparams(1),
        name="context_fnet",
    )(fc, w1, cm, sm)


def _block_diag(w):
    g, c, _ = w.shape
    eye = jnp.eye(g, dtype=w.dtype)
    return (eye[:, None, :, None] * w[:, :, None, :]).reshape(g * c, g * c)


def kernel(x, c, ctx, c_ctx, w_mod, b_mod, w_in, rpb, w_four, w_out, ln1_g, ln1_b, ln2_g, ln2_b,
           w_router, router_bias, w_gate, w_up, w_down):
    bsz, length, _ = x.shape
    n_ctx = ctx.shape[1]
    rows = length // GRID_W

    cvec = jnp.concatenate([c, c_ctx[None, :], jnp.zeros((8 - bsz - 1, D_MODEL), F32)], axis=0)
    mods = _modulation(cvec, w_mod, b_mod)
    w_router_t = w_router.T.astype(BF16)
    lane_pad = LANES - EXPERTS_PER_GROUP
    w_router_grp = jnp.pad(w_router.reshape(D_MODEL, N_GROUPS, EXPERTS_PER_GROUP).transpose(1, 0, 2),
                           ((0, 0), (0, 0), (0, lane_pad))).astype(BF16)
    router_bias_grp = jnp.pad(router_bias.astype(F32).reshape(N_GROUPS, 1, EXPERTS_PER_GROUP),
                              ((0, 0), (0, 0), (0, lane_pad)))

    xc = ctx
    for i in range(DEPTH):
        last = i == DEPTH - 1
        m = mods[i, :bsz].reshape(bsz, N_MOD, D_MODEL)
        mc = jnp.broadcast_to(mods[i, bsz].reshape(1, N_MOD, D_MODEL), (bsz, N_MOD, D_MODEL))
        w_in_b = w_in[i].astype(BF16)
        w_out_b = w_out[i].astype(BF16)
        w_four_bd = _block_diag(w_four[i]).astype(BF16)
        wg, wu, wd = w_gate[i].astype(BF16), w_up[i].astype(BF16), w_down[i].astype(BF16)

        q, k, v, f = _in_projection(x, m, w_in_b, tm=512)
        qc, kc, vc, fc = _in_projection(xc, mc, w_in_b, tm=n_ctx)

        attn = _neighborhood_attention(q, k, v, kc, vc, _na_bias_tables(rpb[i], rows))
        yf = _fourier_positions(f, n_slow=rows, n_fast=GRID_W)
        x1, h2, grp = _out_projection(attn, yf, x, m, w_out_b, w_four_bd, ln1_g[i], ln1_b[i],
                                      w_router_t, router_bias, tm=512)
        x = _grouped_moe(h2, grp, x1, m, wg, wu, wd, w_router_grp, router_bias_grp, ln2_g[i], ln2_b[i],
                         tm_experts=512, tm_norm=512)
        if last:
            break

        attn_c = _context_attention(qc, kc, vc)
        yc = _context_fourier(fc)
        xc1, h2c, grp_c = _out_projection(attn_c, yc, xc, mc, w_out_b, w_four_bd, ln1_g[i], ln1_b[i],
                                          w_router_t, router_bias, tm=n_ctx)
        xc = _grouped_moe(h2c, grp_c, xc1, mc, wg, wu, wd, w_router_grp, router_bias_grp, ln2_g[i], ln2_b[i],
                          tm_experts=128, tm_norm=n_ctx)
    return x
```

```python
import functools
import math

import numpy as np
import jax
import jax.numpy as jnp
from jax import lax
from jax.experimental import pallas as pl
from jax.experimental.pallas import tpu as pltpu

D_MODEL = 1024
DEPTH = 2
GRID_W = 64
NA_HEADS = 8
HEAD_DIM = 64
NA_WIDTH = NA_HEADS * HEAD_DIM
WIN_ROWS = 8
WIN_COLS = 16
F_GROUPS = 8
F_GROUP_DIM = 64
F_WIDTH = F_GROUPS * F_GROUP_DIM
IN_WIDTH = 3 * NA_WIDTH + F_WIDTH
N_EXPERTS = 16
N_GROUPS = 4
EXPERTS_PER_GROUP = N_EXPERTS // N_GROUPS
D_EXPERT = 256
N_MOD = 6
DEEPNORM_ALPHA = (2.0 * DEPTH) ** 0.25
LN_EPS = 1e-6

F32 = jnp.float32
BF16 = jnp.bfloat16

V7X_VMEM_BYTES = 64 * 1024 * 1024
VMEM_LIMIT_BYTES = (V7X_VMEM_BYTES * 3) // 4
LANES = 128
SUBLANES = 8
HEADS_PER_STEP = LANES // HEAD_DIM
MASK_VALUE = -1e30

Q_ROWS = 8
K_ROWS = 16
K_CHUNK_ROWS = 4
N_K_CHUNKS = K_ROWS // K_CHUNK_ROWS


def _cparams(n_grid_dims):
    return pltpu.CompilerParams(dimension_semantics=("arbitrary",) * n_grid_dims,
                                vmem_limit_bytes=VMEM_LIMIT_BYTES)


def _layer_norm(x):
    mu = jnp.mean(x, axis=-1, keepdims=True)
    xc = x - mu
    var = jnp.mean(xc * xc, axis=-1, keepdims=True)
    return xc * lax.rsqrt(var + LN_EPS)


def _dot(a, b):
    return jnp.dot(a, b, preferred_element_type=F32)


def _dot_nt(a, b):
    return lax.dot_general(a, b, (((1,), (1,)), ((), ())), preferred_element_type=F32)


def _mod_kernel(c_ref, w_ref, b_ref, o_ref):
    c = c_ref[...]
    a = c * jax.nn.sigmoid(c)
    o_ref[0] = jnp.dot(a, w_ref[0], preferred_element_type=F32, precision=lax.Precision.HIGHEST) + b_ref[0]


def _modulation(cvec, w_mod, b_mod):
    n_col_blocks = 4
    wc = (N_MOD * D_MODEL) // n_col_blocks
    rows = cvec.shape[0]
    return pl.pallas_call(
        _mod_kernel,
        grid=(DEPTH, n_col_blocks),
        in_specs=[pl.BlockSpec((rows, D_MODEL), lambda i, j: (0, 0)),
                  pl.BlockSpec((1, D_MODEL, wc), lambda i, j: (i, 0, j)),
                  pl.BlockSpec((1, 1, wc), lambda i, j: (i, 0, j))],
        out_specs=pl.BlockSpec((1, rows, wc), lambda i, j: (i, 0, j)),
        out_shape=jax.ShapeDtypeStruct((DEPTH, rows, N_MOD * D_MODEL), F32),
        compiler_params=_cparams(2),
        name="modulation",
    )(cvec, w_mod, b_mod.reshape(DEPTH, 1, N_MOD * D_MODEL))


def _proj_kernel(x_ref, m_ref, w_ref, q_ref, k_ref, v_ref, f_ref):
    h = _layer_norm(x_ref[0]) * (1.0 + m_ref[0, 1:2, :]) + m_ref[0, 0:1, :]
    p = _dot(h.astype(BF16), w_ref[...])
    q_ref[0] = (p[:, :NA_WIDTH] * (HEAD_DIM ** -0.5)).astype(BF16)
    k_ref[0] = p[:, NA_WIDTH:2 * NA_WIDTH].astype(BF16)
    v_ref[0] = p[:, 2 * NA_WIDTH:3 * NA_WIDTH].astype(BF16)
    f_ref[0] = p[:, 3 * NA_WIDTH:].astype(BF16)


def _in_projection(x, m, w_in_bf16, tm):
    bsz, length, _ = x.shape
    out = jax.ShapeDtypeStruct((bsz, length, NA_WIDTH), BF16)
    o_spec = pl.BlockSpec((1, tm, NA_WIDTH), lambda b, i: (b, i, 0))
    return pl.pallas_call(
        _proj_kernel,
        grid=(bsz, length // tm),
        in_specs=[pl.BlockSpec((1, tm, D_MODEL), lambda b, i: (b, i, 0)),
                  pl.BlockSpec((1, N_MOD, D_MODEL), lambda b, i: (b, 0, 0)),
                  pl.BlockSpec((D_MODEL, IN_WIDTH), lambda b, i: (0, 0))],
        out_specs=[o_spec, o_spec, o_spec, o_spec],
        out_shape=[out, out, out, out],
        compiler_params=_cparams(2),
        name="ln_mod_in_proj",
    )(x, m, w_in_bf16)


def _softmax_pv(qh, keys, values, bias):
    scores = [_dot_nt(qh, k) for k in keys]
    scores[0] = scores[0] + bias if bias is not None else scores[0]
    m = functools.reduce(jnp.maximum, [jnp.max(s, axis=-1, keepdims=True) for s in scores])
    probs = [jnp.exp(s - m) for s in scores]
    denom = functools.reduce(jnp.add, [jnp.sum(p, axis=-1, keepdims=True) for p in probs])
    o = functools.reduce(jnp.add, [_dot(p.astype(BF16), v) for p, v in zip(probs, values)])
    return o / denom


def _two_head_attention(q, keys, values, bias_ref, q_start, q_len):
    lane = lax.broadcasted_iota(jnp.int32, (1, LANES), 1)
    out = None
    for h in range(HEADS_PER_STEP):
        in_head = (lane >= HEAD_DIM * h) & (lane < HEAD_DIM * (h + 1))
        qh = jnp.where(in_head, q, jnp.zeros_like(q))
        bias = None if bias_ref is None else bias_ref[0, h, q_start:q_start + q_len, :]
        o = _softmax_pv(qh, keys, values, bias)
        out = o if out is None else jnp.where(in_head, o, out)
    return out


def _na_kernel(q_ref, *refs):
    k_refs = refs[:N_K_CHUNKS]
    v_refs = refs[N_K_CHUNKS:2 * N_K_CHUNKS]
    kc_ref, vc_ref, bias_ref, o_ref = refs[2 * N_K_CHUNKS:]
    k = jnp.concatenate([r[0] for r in k_refs], axis=0)
    v = jnp.concatenate([r[0] for r in v_refs], axis=0)
    keys = [k, kc_ref[0]]
    values = [v, vc_ref[0]]
    q_len = 256
    for q_start in range(0, Q_ROWS * GRID_W, q_len):
        q = q_ref[0, q_start:q_start + q_len, :]
        o = _two_head_attention(q, keys, values, bias_ref, q_start, q_len)
        o_ref[0, q_start:q_start + q_len, :] = o.astype(BF16)


def _na_bias_tables(rpb, rows):
    n_rb = rows // Q_ROWS
    kh = min(WIN_ROWS, rows)
    qc, kc = np.arange(GRID_W)[:, None], np.arange(GRID_W)[None, :]
    cs = np.clip(qc - WIN_COLS // 2, 0, GRID_W - WIN_COLS)
    col_valid = jnp.asarray((kc >= cs) & (kc < cs + WIN_COLS))
    pad = GRID_W - WIN_COLS
    rpb_pad = jnp.pad(rpb.astype(F32), ((0, 0), (0, 0), (pad, pad)))
    toeplitz = jnp.stack([rpb_pad[:, :, GRID_W - 1 - c:2 * GRID_W - 1 - c] for c in range(GRID_W)], axis=2)
    toeplitz = jnp.where(col_valid, toeplitz, MASK_VALUE)
    masked = jnp.full((rpb.shape[0], GRID_W, GRID_W), MASK_VALUE, F32)
    tables = []
    for rb in (0, 1, n_rb - 1):
        block_rows = []
        for qi in range(Q_ROWS):
            qr = rb * Q_ROWS + qi
            rs = int(np.clip(qr - kh // 2, 0, rows - kh))
            blocks = []
            for kj in range(K_ROWS):
                kr = _key_row_start(rb, rows) + kj
                blocks.append(toeplitz[:, kr - qr + WIN_ROWS - 1] if rs <= kr < rs + kh else masked)
            block_rows.append(jnp.concatenate(blocks, axis=-1))
        tables.append(jnp.concatenate(block_rows, axis=1))
    return jnp.stack(tables)


def _key_row_start(rb, rows):
    return int(np.clip(rb * Q_ROWS - (K_ROWS - Q_ROWS) // 2, 0, rows - K_ROWS))


def _neighborhood_attention(q, k, v, kc, vc, bias_tables):
    bsz, length, _ = q.shape
    rows = length // GRID_W
    n_rb = rows // Q_ROWS
    assert rows % Q_ROWS == 0 and rows >= K_ROWS + Q_ROWS and n_rb >= 3
    n_ctx = kc.shape[1]
    tq = Q_ROWS * GRID_W
    tk = K_CHUNK_ROWS * GRID_W
    max_chunk = (rows - K_ROWS) // K_CHUNK_ROWS
    half = (K_ROWS - Q_ROWS) // 2 // K_CHUNK_ROWS

    def kv_spec(j):
        def index(b, hp, rb):
            start = jnp.clip(rb * (Q_ROWS // K_CHUNK_ROWS) - half, 0, max_chunk)
            return (b, start + j, hp)
        return pl.BlockSpec((1, tk, LANES), index)

    def bias_index(b, hp, rb):
        variant = jnp.where(rb == 0, 0, jnp.where(rb == n_rb - 1, 2, 1))
        return (variant, hp, 0, 0)

    ctx_spec = pl.BlockSpec((1, n_ctx, LANES), lambda b, hp, rb: (b, 0, hp))
    q_spec = pl.BlockSpec((1, tq, LANES), lambda b, hp, rb: (b, rb, hp))
    return pl.pallas_call(
        _na_kernel,
        grid=(bsz, NA_HEADS // HEADS_PER_STEP, n_rb),
        in_specs=([q_spec] + [kv_spec(j) for j in range(N_K_CHUNKS)] + [kv_spec(j) for j in range(N_K_CHUNKS)]
                  + [ctx_spec, ctx_spec,
                     pl.BlockSpec((1, HEADS_PER_STEP, tq, K_ROWS * GRID_W), bias_index)]),
        out_specs=q_spec,
        out_shape=jax.ShapeDtypeStruct((bsz, length, NA_WIDTH), BF16),
        compiler_params=_cparams(3),
        name="neighborhood_attention",
    )(q, *([k] * N_K_CHUNKS), *([v] * N_K_CHUNKS), kc, vc, bias_tables)


def _ctx_attn_kernel(q_ref, k_ref, v_ref, o_ref):
    n = q_ref.shape[1]
    o = _two_head_attention(q_ref[0], [k_ref[0]], [v_ref[0]], None, 0, n)
    o_ref[0] = o.astype(BF16)


def _context_attention(qc, kc, vc):
    bsz, n_ctx, _ = qc.shape
    spec = pl.BlockSpec((1, n_ctx, LANES), lambda b, hp: (b, 0, hp))
    return pl.pallas_call(
        _ctx_attn_kernel,
        grid=(bsz, NA_HEADS // HEADS_PER_STEP),
        in_specs=[spec, spec, spec],
        out_specs=spec,
        out_shape=jax.ShapeDtypeStruct((bsz, n_ctx, NA_WIDTH), BF16),
        compiler_params=_cparams(2),
        name="context_attention",
    )(qc, kc, vc)


def _dft_cos_sin(n):
    ang = 2.0 * np.pi * np.outer(np.arange(n), np.arange(n)) / n
    return np.cos(ang), np.sin(ang)


def _bf16_table(a):
    return jnp.asarray(a, F32).astype(BF16)


def _channel_dft_matrix():
    c, s = _dft_cos_sin(F_GROUP_DIM)
    scale = F_GROUP_DIM ** -0.5
    eye = np.eye(F_GROUPS)
    return np.concatenate([np.kron(eye, c), np.kron(eye, s)], axis=1) * scale


def _fft_stage1_kernel(f_ref, w1_ref, cs_ref, sc_ref, tc_ref, ts_ref, zr_ref, zi_ref, *, n_slow, nt):
    x = jnp.concatenate([f_ref[0, :, t * F_WIDTH:(t + 1) * F_WIDTH] for t in range(nt)], axis=0)
    ab = _dot(x, w1_ref[...]).astype(BF16)
    for t in range(nt):
        a = ab[t * n_slow:(t + 1) * n_slow, :F_WIDTH]
        b = ab[t * n_slow:(t + 1) * n_slow, F_WIDTH:]
        z = _dot(cs_ref[...], a) + _dot(sc_ref[...], b)
        zr, zi = z[:n_slow], z[n_slow:]
        c, s = tc_ref[t], ts_ref[t]
        zr_ref[0, t] = (zr * c - zi * s).astype(BF16)
        zi_ref[0, t] = (zr * s + zi * c).astype(BF16)


def _fft_stage2_kernel(zr_ref, zi_ref, fc_ref, fs_ref, y_ref):
    y_ref[0] = (_dot(fc_ref[...], zr_ref[0]) + _dot(fs_ref[...], zi_ref[0])).astype(BF16)


def _fourier_positions(f, n_slow, n_fast):
    bsz, n, _ = f.shape
    assert n == n_slow * n_fast
    nt = 8
    w1 = _bf16_table(_channel_dft_matrix())
    c1, s1 = _dft_cos_sin(n_slow)
    sc1 = n_slow ** -0.5
    cs = _bf16_table(np.concatenate([c1, s1], axis=0) * sc1)
    sc = _bf16_table(np.concatenate([-s1, c1], axis=0) * sc1)
    tw = 2.0 * np.pi * np.outer(np.arange(n_fast), np.arange(n_slow)) / n
    tc = jnp.asarray(np.cos(tw)[:, :, None], F32)
    ts = jnp.asarray(np.sin(tw)[:, :, None], F32)
    z_shape = jax.ShapeDtypeStruct((bsz, n_fast, n_slow, F_WIDTH), BF16)
    z_spec = pl.BlockSpec((1, nt, n_slow, F_WIDTH), lambda b, j: (b, j, 0, 0))
    const2 = lambda b, j: (0, 0)
    tw_spec = pl.BlockSpec((nt, n_slow, 1), lambda b, j: (j, 0, 0))
    zr, zi = pl.pallas_call(
        functools.partial(_fft_stage1_kernel, n_slow=n_slow, nt=nt),
        grid=(bsz, n_fast // nt),
        in_specs=[pl.BlockSpec((1, n_slow, nt * F_WIDTH), lambda b, j: (b, 0, j)),
                  pl.BlockSpec(w1.shape, const2), pl.BlockSpec(cs.shape, const2), pl.BlockSpec(sc.shape, const2),
                  tw_spec, tw_spec],
        out_specs=[z_spec, z_spec],
        out_shape=[z_shape, z_shape],
        compiler_params=_cparams(2),
        name="fnet_stage1",
    )(f.reshape(bsz, n_slow, n_fast * F_WIDTH), w1, cs, sc, tc, ts)

    c2, s2 = _dft_cos_sin(n_fast)
    sc2 = n_fast ** -0.5
    fc = _bf16_table(c2 * sc2)
    fs = _bf16_table(-s2 * sc2)
    cols = n_slow * F_WIDTH
    wcol = min(cols, 8192)
    blk = pl.BlockSpec((1, n_fast, wcol), lambda b, j: (b, 0, j))
    y = pl.pallas_call(
        _fft_stage2_kernel,
        grid=(bsz, cols // wcol),
        in_specs=[blk, blk, pl.BlockSpec(fc.shape, const2), pl.BlockSpec(fs.shape, const2)],
        out_specs=blk,
        out_shape=jax.ShapeDtypeStruct((bsz, n_fast, cols), BF16),
        compiler_params=_cparams(2),
        name="fnet_stage2",
    )(zr.reshape(bsz, n_fast, cols), zi.reshape(bsz, n_fast, cols), fc, fs)
    return y.reshape(bsz, n, F_WIDTH)


def _second_largest_sum(a, b, c, d):
    mab, nab = jnp.maximum(a, b), jnp.minimum(a, b)
    mcd, ncd = jnp.maximum(c, d), jnp.minimum(c, d)
    return jnp.maximum(mab, mcd) + jnp.maximum(jnp.minimum(mab, mcd), jnp.maximum(nab, ncd))


def _selected_group(sb_rows):
    epg = EXPERTS_PER_GROUP
    g_score = [_second_largest_sum(*sb_rows[g * epg:(g + 1) * epg]) for g in range(N_GROUPS)]
    best = functools.reduce(jnp.maximum, g_score)
    group = jnp.full_like(best, float(N_GROUPS - 1))
    for g in range(N_GROUPS - 2, -1, -1):
        group = jnp.where(g_score[g] == best, float(g), group)
    return group


def _top2_gates(cand_s, cand_sb):
    n = len(cand_s)
    w = []
    for j in range(n):
        rank = jnp.zeros_like(cand_sb[j])
        for i in range(n):
            if i == j:
                continue
            ahead = (cand_sb[i] > cand_sb[j]) | ((cand_sb[i] == cand_sb[j]) & (i < j))
            rank = rank + jnp.where(ahead, 1.0, 0.0)
        w.append(jnp.where(rank < 2.0, cand_s[j], 0.0))
    total = functools.reduce(jnp.add, w)
    return [wj / total for wj in w]


def _out_kernel(a_ref, y_ref, x_ref, m_ref, wo_ref, bd_ref, g_ref, b_ref, wr_ref, rb_ref,
                x1_ref, h2_ref, grp_ref):
    y2 = _dot(y_ref[0], bd_ref[...]).astype(BF16)
    o = _dot(a_ref[0], wo_ref[:NA_WIDTH, :]) + _dot(y2, wo_ref[NA_WIDTH:, :])
    z = DEEPNORM_ALPHA * x_ref[0] + m_ref[0, 2:3, :] * o
    x1 = _layer_norm(z) * g_ref[...] + b_ref[...]
    x1_ref[0] = x1
    h2 = _layer_norm(x1) * (1.0 + m_ref[0, 4:5, :]) + m_ref[0, 3:4, :]
    h2_ref[0] = h2
    sb = jax.nn.sigmoid(_dot_nt(wr_ref[...], h2.astype(BF16))) + rb_ref[...]
    group = _selected_group([sb[e:e + 1] for e in range(N_EXPERTS)])
    grp_ref[0] = jnp.broadcast_to(group, grp_ref.shape[1:])


def _out_projection(attn, yf, x, m, w_out_bf16, w_four_bd, ln_g, ln_b, w_router_t, router_bias, tm):
    bsz, length, _ = x.shape
    row = lambda b, i: (b, i, 0)
    const2 = lambda b, i: (0, 0)
    return pl.pallas_call(
        _out_kernel,
        grid=(bsz, length // tm),
        in_specs=[pl.BlockSpec((1, tm, NA_WIDTH), row), pl.BlockSpec((1, tm, F_WIDTH), row),
                  pl.BlockSpec((1, tm, D_MODEL), row),
                  pl.BlockSpec((1, N_MOD, D_MODEL), lambda b, i: (b, 0, 0)),
                  pl.BlockSpec((NA_WIDTH + F_WIDTH, D_MODEL), const2),
                  pl.BlockSpec((F_WIDTH, F_WIDTH), const2),
                  pl.BlockSpec((1, D_MODEL), const2), pl.BlockSpec((1, D_MODEL), const2),
                  pl.BlockSpec((N_EXPERTS, D_MODEL), const2), pl.BlockSpec((N_EXPERTS, 1), const2)],
        out_specs=[pl.BlockSpec((1, tm, D_MODEL), row), pl.BlockSpec((1, tm, D_MODEL), row),
                   pl.BlockSpec((1, SUBLANES, tm), lambda b, i: (b, 0, i))],
        out_shape=[jax.ShapeDtypeStruct((bsz, length, D_MODEL), F32),
                   jax.ShapeDtypeStruct((bsz, length, D_MODEL), F32),
                   jax.ShapeDtypeStruct((bsz, SUBLANES, length), F32)],
        compiler_params=_cparams(2),
        name="out_proj_norm_route",
    )(attn, yf, x, m, w_out_bf16, w_four_bd, ln_g.reshape(1, D_MODEL), ln_b.reshape(1, D_MODEL),
      w_router_t, router_bias.reshape(N_EXPERTS, 1))


def _group_plan(group, tm):
    n = group.shape[0]
    n_slots = n + N_GROUPS * tm
    onehot = (group[:, None] == jnp.arange(N_GROUPS, dtype=jnp.int32)[None, :]).astype(jnp.int32)
    csum = jnp.cumsum(onehot, axis=0)
    rank = jnp.sum(onehot * csum, axis=1) - 1
    count = csum[-1]
    padded = ((count + tm - 1) // tm) * tm
    end = jnp.cumsum(padded)
    start = end - padded
    slot = jnp.sum(onehot * start[None, :], axis=1) + rank
    tile_start = jnp.arange(n_slots // tm, dtype=jnp.int32) * tm
    tile_group = jnp.minimum(jnp.sum((end[None, :] <= tile_start[:, None]).astype(jnp.int32), axis=1),
                             N_GROUPS - 1)
    tile_rows = jnp.clip((start + count)[tile_group] - tile_start, 0, tm)
    return slot, tile_group, tile_rows, start, start + count, end


ROW_DMA_UNROLL = 8


def _row_scatter(idx_ref, base, src_ref, dst_hbm, sem, n_rows, wait):
    def one(r, priority):
        if wait:
            pltpu.make_async_copy(src_ref.at[pl.ds(0, 1), :], dst_hbm.at[pl.ds(0, 1), :], sem).wait()
        else:
            pltpu.make_async_copy(src_ref.at[pl.ds(r, 1), :], dst_hbm.at[pl.ds(idx_ref[base + r], 1), :],
                                  sem).start(priority=priority)

    def chunk(j, carry):
        for u in range(ROW_DMA_UNROLL):
            one(j * ROW_DMA_UNROLL + u, u % 2)
        return carry

    def single(r, carry):
        one(r, 0)
        return carry

    n_chunks = n_rows // ROW_DMA_UNROLL
    lax.fori_loop(0, n_chunks, chunk, 0)
    lax.fori_loop(n_chunks * ROW_DMA_UNROLL, n_rows, single, 0)


def _to_slots_kernel(slot_ref, start_ref, rows_end_ref, end_ref, h_ref, o_hbm, dst_ref, zbuf, sem,
                     *, tm, tm_slots):
    step = pl.program_id(0)
    n_slots = dst_ref.shape[0]

    @pl.when(step == 0)
    def _():
        zbuf[...] = jnp.zeros_like(zbuf)

        def fill(row0):
            return pltpu.make_async_copy(zbuf, o_hbm.at[pl.ds(pl.multiple_of(row0, tm_slots), tm_slots), :], sem)

        for g in range(N_GROUPS):
            @pl.when(end_ref[g] > start_ref[g])
            def _():
                fill(end_ref[g] - tm_slots).start()
        for g in range(N_GROUPS):
            @pl.when(end_ref[g] > start_ref[g])
            def _():
                fill(end_ref[g] - tm_slots).wait()

        def fill_unused(t, carry):
            fill(t * tm_slots).start()
            fill(t * tm_slots).wait()
            return carry
        lax.fori_loop(end_ref[N_GROUPS - 1] // tm_slots, n_slots // tm_slots, fill_unused, 0)

        def clear(p, carry):
            dst_ref[p] = 0
            return carry
        for g in range(N_GROUPS):
            lax.fori_loop(rows_end_ref[g], end_ref[g], clear, 0)
        lax.fori_loop(end_ref[N_GROUPS - 1], n_slots, clear, 0)

    def record(r, carry):
        dst_ref[slot_ref[step * tm + r]] = step * tm + r
        return carry
    lax.fori_loop(0, tm, record, 0)
    _row_scatter(slot_ref, step * tm, h_ref, o_hbm, sem, tm, wait=False)
    _row_scatter(slot_ref, step * tm, h_ref, o_hbm, sem, tm, wait=True)


def _to_slots(h2, slot, start, rows_end, end, tm, tm_slots):
    n = h2.shape[0]
    n_slots = n + N_GROUPS * tm_slots
    grid_spec = pltpu.PrefetchScalarGridSpec(
        num_scalar_prefetch=4,
        grid=(n // tm,),
        in_specs=[pl.BlockSpec((tm, D_MODEL), lambda i, *_: (i, 0))],
        out_specs=[pl.BlockSpec(memory_space=pl.ANY), pl.BlockSpec(memory_space=pltpu.SMEM)],
        scratch_shapes=[pltpu.VMEM((tm_slots, D_MODEL), F32), pltpu.SemaphoreType.DMA(())],
    )
    return pl.pallas_call(
        functools.partial(_to_slots_kernel, tm=tm, tm_slots=tm_slots),
        grid_spec=grid_spec,
        out_shape=[jax.ShapeDtypeStruct((n_slots, D_MODEL), F32), jax.ShapeDtypeStruct((n_slots,), jnp.int32)],
        compiler_params=_cparams(1),
        name="moe_rows_to_slots",
    )(slot, start, rows_end, end, h2)


def _moe_group_kernel(dst_ref, tile_group_ref, tile_rows_ref, n_used_ref, h_ref, wg_ref, wu_ref, wd_ref, wr_ref,
                      rb_ref, y_hbm, ybuf, sems, *, tm, n_tiles):
    del tile_group_ref
    step = pl.program_id(0)
    n_used = n_used_ref[0]
    buf = step % 2

    def scatter(tile, wait):
        _row_scatter(dst_ref, tile * tm, ybuf.at[tile % 2], y_hbm, sems.at[tile % 2], tile_rows_ref[tile], wait)

    @pl.when((step >= 2) & (step < n_used))
    def _():
        scatter(step - 2, wait=True)

    @pl.when(step < n_used)
    def _():
        h = h_ref[...].astype(BF16)
        s = jax.nn.sigmoid(_dot(h, wr_ref[0]))
        sb = s + rb_ref[0]
        epg = EXPERTS_PER_GROUP
        gates = _top2_gates([s[:, j:j + 1] for j in range(epg)], [sb[:, j:j + 1] for j in range(epg)])
        acc = None
        for e in range(epg):
            gate = _dot(h, wg_ref[e])
            up = _dot(h, wu_ref[e])
            hid = (gate * jax.nn.sigmoid(gate)) * up * gates[e]
            y = _dot(hid.astype(BF16), wd_ref[e])
            acc = y if acc is None else acc + y
        ybuf[buf] = acc
        scatter(step, wait=False)

    @pl.when(step == n_tiles - 1)
    def _():
        @pl.when(n_used >= 2)
        def _():
            scatter(n_used - 2, wait=True)

        @pl.when(n_used >= 1)
        def _():
            scatter(n_used - 1, wait=True)


def _moe_experts(h_slots, dst, tile_group, tile_rows, n_used, n_tokens, w_gate, w_up, w_down, w_router_grp,
                 router_bias_grp, tm):
    n_slots = h_slots.shape[0]
    n_tiles = n_slots // tm
    epg = EXPERTS_PER_GROUP
    by_group = lambda i, dst_ref, tg_ref, tr_ref, nu_ref: (tg_ref[i], 0, 0)
    grid_spec = pltpu.PrefetchScalarGridSpec(
        num_scalar_prefetch=4,
        grid=(n_tiles,),
        in_specs=[pl.BlockSpec((tm, D_MODEL),
                               lambda i, dst_ref, tg_ref, tr_ref, nu_ref: (jnp.minimum(i, nu_ref[0] - 1), 0)),
                  pl.BlockSpec((epg, D_MODEL, D_EXPERT), by_group),
                  pl.BlockSpec((epg, D_MODEL, D_EXPERT), by_group),
                  pl.BlockSpec((epg, D_EXPERT, D_MODEL), by_group),
                  pl.BlockSpec((1, D_MODEL, LANES), by_group),
                  pl.BlockSpec((1, 1, LANES), by_group)],
        out_specs=pl.BlockSpec(memory_space=pl.ANY),
        scratch_shapes=[pltpu.VMEM((2, tm, D_MODEL), F32), pltpu.SemaphoreType.DMA((2,))],
    )
    return pl.pallas_call(
        functools.partial(_moe_group_kernel, tm=tm, n_tiles=n_tiles),
        grid_spec=grid_spec,
        out_shape=jax.ShapeDtypeStruct((n_tokens, D_MODEL), F32),
        compiler_params=_cparams(1),
        name="moe_group_experts",
    )(dst, tile_group, tile_rows, n_used, h_slots, w_gate, w_up, w_down, w_router_grp, router_bias_grp)


def _residual_norm_kernel(y_ref, x1_ref, m_ref, g_ref, b_ref, o_ref):
    z = DEEPNORM_ALPHA * x1_ref[...] + m_ref[0, 5:6, :] * y_ref[...]
    o_ref[...] = _layer_norm(z) * g_ref[...] + b_ref[...]


def _residual_norm(y, x1, m, ln_g, ln_b, tm):
    bsz, length, _ = x1.shape
    n = bsz * length
    tiles_per_batch = length // tm
    const2 = lambda i: (0, 0)
    row = pl.BlockSpec((tm, D_MODEL), lambda i: (i, 0))
    out = pl.pallas_call(
        _residual_norm_kernel,
        grid=(n // tm,),
        in_specs=[row, row, pl.BlockSpec((1, N_MOD, D_MODEL), lambda i: (i // tiles_per_batch, 0, 0)),
                  pl.BlockSpec((1, D_MODEL), const2), pl.BlockSpec((1, D_MODEL), const2)],
        out_specs=row,
        out_shape=jax.ShapeDtypeStruct((n, D_MODEL), F32),
        compiler_params=_cparams(1),
        name="moe_residual_norm",
    )(y, x1.reshape(n, D_MODEL), m, ln_g.reshape(1, D_MODEL), ln_b.reshape(1, D_MODEL))
    return out.reshape(bsz, length, D_MODEL)


def _grouped_moe(h2, group_rows, x1, m, w_gate, w_up, w_down, w_router_grp, router_bias_grp, ln_g, ln_b,
                 tm_experts, tm_tokens):
    bsz, length, _ = x1.shape
    n = bsz * length
    group = group_rows[:, 0, :].reshape(n).astype(jnp.int32)
    slot, tile_group, tile_rows, start, rows_end, end = _group_plan(group, tm_experts)
    h_slots, dst = _to_slots(h2.reshape(n, D_MODEL), slot, start, rows_end, end, tm_tokens, tm_experts)
    n_used = (end[N_GROUPS - 1:] // tm_experts).astype(jnp.int32)
    y = _moe_experts(h_slots, dst, tile_group, tile_rows, n_used, n, w_gate, w_up, w_down, w_router_grp,
                     router_bias_grp, tm_experts)
    return _residual_norm(y, x1, m, ln_g, ln_b, tm_tokens)


def _ctx_fourier_kernel(f_ref, w1_ref, c_ref, s_ref, y_ref):
    ab = _dot(f_ref[0], w1_ref[...]).astype(BF16)
    y = _dot(c_ref[...], ab[:, :F_WIDTH]) + _dot(s_ref[...], ab[:, F_WIDTH:])
    y_ref[0] = y.astype(BF16)


def _context_fourier(fc):
    bsz, n, _ = fc.shape
    w1 = _bf16_table(_channel_dft_matrix())
    c, s = _dft_cos_sin(n)
    cm = _bf16_table(c * n ** -0.5)
    sm = _bf16_table(-s * n ** -0.5)
    const2 = lambda b: (0, 0)
    blk = pl.BlockSpec((1, n, F_WIDTH), lambda b: (b, 0, 0))
    return pl.pallas_call(
        _ctx_fourier_kernel,
        grid=(bsz,),
        in_specs=[blk, pl.BlockSpec(w1.shape, const2), pl.BlockSpec(cm.shape, const2),
                  pl.BlockSpec(sm.shape, const2)],
        out_specs=blk,
        out_shape=jax.ShapeDtypeStruct((bsz, n, F_WIDTH), BF16),
        compiler_params=_cparams(1),
        name="context_fnet",
    )(fc, w1, cm, sm)


def _block_diag(w):
    g, c, _ = w.shape
    eye = jnp.eye(g, dtype=w.dtype)
    return (eye[:, None, :, None] * w[:, :, None, :]).reshape(g * c, g * c)


def kernel(x, c, ctx, c_ctx, w_mod, b_mod, w_in, rpb, w_four, w_out, ln1_g, ln1_b, ln2_g, ln2_b,
           w_router, router_bias, w_gate, w_up, w_down):
    bsz, length, _ = x.shape
    n_ctx = ctx.shape[1]
    rows = length // GRID_W

    cvec = jnp.concatenate([c, c_ctx[None, :], jnp.zeros((8 - bsz - 1, D_MODEL), F32)], axis=0)
    mods = _modulation(cvec, w_mod, b_mod)
    w_router_t = w_router.T.astype(BF16)
    lane_pad = LANES - EXPERTS_PER_GROUP
    w_router_grp = jnp.pad(w_router.reshape(D_MODEL, N_GROUPS, EXPERTS_PER_GROUP).transpose(1, 0, 2),
                           ((0, 0), (0, 0), (0, lane_pad))).astype(BF16)
    router_bias_grp = jnp.pad(router_bias.astype(F32).reshape(N_GROUPS, 1, EXPERTS_PER_GROUP),
                              ((0, 0), (0, 0), (0, lane_pad)))

    xc = ctx
    for i in range(DEPTH):
        last = i == DEPTH - 1
        m = mods[i, :bsz].reshape(bsz, N_MOD, D_MODEL)
        mc = jnp.broadcast_to(mods[i, bsz].reshape(1, N_MOD, D_MODEL), (bsz, N_MOD, D_MODEL))
        w_in_b = w_in[i].astype(BF16)
        w_out_b = w_out[i].astype(BF16)
        w_four_bd = _block_diag(w_four[i]).astype(BF16)
        wg, wu, wd = w_gate[i].astype(BF16), w_up[i].astype(BF16), w_down[i].astype(BF16)

        q, k, v, f = _in_projection(x, m, w_in_b, tm=512)
        qc, kc, vc, fc = _in_projection(xc, mc, w_in_b, tm=n_ctx)

        attn = _neighborhood_attention(q, k, v, kc, vc, _na_bias_tables(rpb[i], rows))
        yf = _fourier_positions(f, n_slow=rows, n_fast=GRID_W)
        x1, h2, grp = _out_projection(attn, yf, x, m, w_out_b, w_four_bd, ln1_g[i], ln1_b[i],
                                      w_router_t, router_bias, tm=512)
        x = _grouped_moe(h2, grp, x1, m, wg, wu, wd, w_router_grp, router_bias_grp, ln2_g[i], ln2_b[i],
                         tm_experts=512, tm_tokens=512)
        if last:
            break

        attn_c = _context_attention(qc, kc, vc)
        yc = _context_fourier(fc)
        xc1, h2c, grp_c = _out_projection(attn_c, yc, xc, mc, w_out_b, w_four_bd, ln1_g[i], ln1_b[i],
                                          w_router_t, router_bias, tm=n_ctx)
        xc = _grouped_moe(h2c, grp_c, xc1, mc, wg, wu, wd, w_router_grp, router_bias_grp, ln2_g[i], ln2_b[i],
                          tm_experts=128, tm_tokens=n_ctx)
    return x
```

```python
import functools
import math

import numpy as np
import jax
import jax.numpy as jnp
from jax import lax
from jax.experimental import pallas as pl
from jax.experimental.pallas import tpu as pltpu

D_MODEL = 1024
DEPTH = 2
GRID_W = 64
NA_HEADS = 8
HEAD_DIM = 64
NA_WIDTH = NA_HEADS * HEAD_DIM
WIN_ROWS = 8
WIN_COLS = 16
F_GROUPS = 8
F_GROUP_DIM = 64
F_WIDTH = F_GROUPS * F_GROUP_DIM
IN_WIDTH = 3 * NA_WIDTH + F_WIDTH
N_EXPERTS = 16
N_GROUPS = 4
EXPERTS_PER_GROUP = N_EXPERTS // N_GROUPS
D_EXPERT = 256
N_MOD = 6
DEEPNORM_ALPHA = (2.0 * DEPTH) ** 0.25
LN_EPS = 1e-6

F32 = jnp.float32
BF16 = jnp.bfloat16

V7X_VMEM_BYTES = 64 * 1024 * 1024
VMEM_LIMIT_BYTES = (V7X_VMEM_BYTES * 3) // 4
LANES = 128
SUBLANES = 8
HEADS_PER_STEP = LANES // HEAD_DIM
MASK_VALUE = -1e30

Q_ROWS = 8
K_ROWS = 16
K_CHUNK_ROWS = 4
N_K_CHUNKS = K_ROWS // K_CHUNK_ROWS


def _cparams(n_grid_dims):
    return pltpu.CompilerParams(dimension_semantics=("arbitrary",) * n_grid_dims,
                                vmem_limit_bytes=VMEM_LIMIT_BYTES)


def _layer_norm(x):
    mu = jnp.mean(x, axis=-1, keepdims=True)
    xc = x - mu
    var = jnp.mean(xc * xc, axis=-1, keepdims=True)
    return xc * lax.rsqrt(var + LN_EPS)


SLAB_ROWS = D_MODEL // LANES
assert SLAB_ROWS == SUBLANES


def _to_slabs(ref, x):
    for j in range(SLAB_ROWS):
        ref[pl.ds(j, x.shape[0], stride=SLAB_ROWS), :] = x[:, j * LANES:(j + 1) * LANES]


def _from_slabs(ref, n_tokens):
    return jnp.concatenate([ref[pl.ds(j, n_tokens, stride=SLAB_ROWS), :] for j in range(SLAB_ROWS)], axis=-1)


def _dot(a, b):
    return jnp.dot(a, b, preferred_element_type=F32)


def _dot_nt(a, b):
    return lax.dot_general(a, b, (((1,), (1,)), ((), ())), preferred_element_type=F32)


def _mod_kernel(c_ref, w_ref, b_ref, o_ref):
    c = c_ref[...]
    a = c * jax.nn.sigmoid(c)
    o_ref[0] = jnp.dot(a, w_ref[0], preferred_element_type=F32, precision=lax.Precision.HIGHEST) + b_ref[0]


def _modulation(cvec, w_mod, b_mod):
    n_col_blocks = 4
    wc = (N_MOD * D_MODEL) // n_col_blocks
    rows = cvec.shape[0]
    return pl.pallas_call(
        _mod_kernel,
        grid=(DEPTH, n_col_blocks),
        in_specs=[pl.BlockSpec((rows, D_MODEL), lambda i, j: (0, 0)),
                  pl.BlockSpec((1, D_MODEL, wc), lambda i, j: (i, 0, j)),
                  pl.BlockSpec((1, 1, wc), lambda i, j: (i, 0, j))],
        out_specs=pl.BlockSpec((1, rows, wc), lambda i, j: (i, 0, j)),
        out_shape=jax.ShapeDtypeStruct((DEPTH, rows, N_MOD * D_MODEL), F32),
        compiler_params=_cparams(2),
        name="modulation",
    )(cvec, w_mod, b_mod.reshape(DEPTH, 1, N_MOD * D_MODEL))


def _proj_kernel(x_ref, m_ref, w_ref, q_ref, k_ref, v_ref, f_ref):
    h = _layer_norm(x_ref[0]) * (1.0 + m_ref[0, 1:2, :]) + m_ref[0, 0:1, :]
    p = _dot(h.astype(BF16), w_ref[...])
    q_ref[0] = (p[:, :NA_WIDTH] * (HEAD_DIM ** -0.5)).astype(BF16)
    k_ref[0] = p[:, NA_WIDTH:2 * NA_WIDTH].astype(BF16)
    v_ref[0] = p[:, 2 * NA_WIDTH:3 * NA_WIDTH].astype(BF16)
    f_ref[0] = p[:, 3 * NA_WIDTH:].astype(BF16)


def _in_projection(x, m, w_in_bf16, tm):
    bsz, length, _ = x.shape
    out = jax.ShapeDtypeStruct((bsz, length, NA_WIDTH), BF16)
    o_spec = pl.BlockSpec((1, tm, NA_WIDTH), lambda b, i: (b, i, 0))
    return pl.pallas_call(
        _proj_kernel,
        grid=(bsz, length // tm),
        in_specs=[pl.BlockSpec((1, tm, D_MODEL), lambda b, i: (b, i, 0)),
                  pl.BlockSpec((1, N_MOD, D_MODEL), lambda b, i: (b, 0, 0)),
                  pl.BlockSpec((D_MODEL, IN_WIDTH), lambda b, i: (0, 0))],
        out_specs=[o_spec, o_spec, o_spec, o_spec],
        out_shape=[out, out, out, out],
        compiler_params=_cparams(2),
        name="ln_mod_in_proj",
    )(x, m, w_in_bf16)


def _softmax_pv(qh, keys, values, bias):
    scores = [_dot_nt(qh, k) for k in keys]
    scores[0] = scores[0] + bias if bias is not None else scores[0]
    m = functools.reduce(jnp.maximum, [jnp.max(s, axis=-1, keepdims=True) for s in scores])
    probs = [jnp.exp(s - m) for s in scores]
    denom = functools.reduce(jnp.add, [jnp.sum(p, axis=-1, keepdims=True) for p in probs])
    o = functools.reduce(jnp.add, [_dot(p.astype(BF16), v) for p, v in zip(probs, values)])
    return o / denom


def _two_head_attention(q, keys, values, bias_ref, q_start, q_len):
    lane = lax.broadcasted_iota(jnp.int32, (1, LANES), 1)
    out = None
    for h in range(HEADS_PER_STEP):
        in_head = (lane >= HEAD_DIM * h) & (lane < HEAD_DIM * (h + 1))
        qh = jnp.where(in_head, q, jnp.zeros_like(q))
        bias = None if bias_ref is None else bias_ref[0, h, q_start:q_start + q_len, :]
        o = _softmax_pv(qh, keys, values, bias)
        out = o if out is None else jnp.where(in_head, o, out)
    return out


def _na_kernel(q_ref, *refs):
    k_refs = refs[:N_K_CHUNKS]
    v_refs = refs[N_K_CHUNKS:2 * N_K_CHUNKS]
    kc_ref, vc_ref, bias_ref, o_ref = refs[2 * N_K_CHUNKS:]
    k = jnp.concatenate([r[0] for r in k_refs], axis=0)
    v = jnp.concatenate([r[0] for r in v_refs], axis=0)
    keys = [k, kc_ref[0]]
    values = [v, vc_ref[0]]
    q_len = 256
    for q_start in range(0, Q_ROWS * GRID_W, q_len):
        q = q_ref[0, q_start:q_start + q_len, :]
        o = _two_head_attention(q, keys, values, bias_ref, q_start, q_len)
        o_ref[0, q_start:q_start + q_len, :] = o.astype(BF16)


def _na_bias_tables(rpb, rows):
    n_rb = rows // Q_ROWS
    kh = min(WIN_ROWS, rows)
    qc, kc = np.arange(GRID_W)[:, None], np.arange(GRID_W)[None, :]
    cs = np.clip(qc - WIN_COLS // 2, 0, GRID_W - WIN_COLS)
    col_valid = jnp.asarray((kc >= cs) & (kc < cs + WIN_COLS))
    pad = GRID_W - WIN_COLS
    rpb_pad = jnp.pad(rpb.astype(F32), ((0, 0), (0, 0), (pad, pad)))
    toeplitz = jnp.stack([rpb_pad[:, :, GRID_W - 1 - c:2 * GRID_W - 1 - c] for c in range(GRID_W)], axis=2)
    toeplitz = jnp.where(col_valid, toeplitz, MASK_VALUE)
    masked = jnp.full((rpb.shape[0], GRID_W, GRID_W), MASK_VALUE, F32)
    tables = []
    for rb in (0, 1, n_rb - 1):
        block_rows = []
        for qi in range(Q_ROWS):
            qr = rb * Q_ROWS + qi
            rs = int(np.clip(qr - kh // 2, 0, rows - kh))
            blocks = []
            for kj in range(K_ROWS):
                kr = _key_row_start(rb, rows) + kj
                blocks.append(toeplitz[:, kr - qr + WIN_ROWS - 1] if rs <= kr < rs + kh else masked)
            block_rows.append(jnp.concatenate(blocks, axis=-1))
        tables.append(jnp.concatenate(block_rows, axis=1))
    return jnp.stack(tables)


def _key_row_start(rb, rows):
    return int(np.clip(rb * Q_ROWS - (K_ROWS - Q_ROWS) // 2, 0, rows - K_ROWS))


def _neighborhood_attention(q, k, v, kc, vc, bias_tables):
    bsz, length, _ = q.shape
    rows = length // GRID_W
    n_rb = rows // Q_ROWS
    assert rows % Q_ROWS == 0 and rows >= K_ROWS + Q_ROWS and n_rb >= 3
    n_ctx = kc.shape[1]
    tq = Q_ROWS * GRID_W
    tk = K_CHUNK_ROWS * GRID_W
    max_chunk = (rows - K_ROWS) // K_CHUNK_ROWS
    half = (K_ROWS - Q_ROWS) // 2 // K_CHUNK_ROWS

    def kv_spec(j):
        def index(b, hp, rb):
            start = jnp.clip(rb * (Q_ROWS // K_CHUNK_ROWS) - half, 0, max_chunk)
            return (b, start + j, hp)
        return pl.BlockSpec((1, tk, LANES), index)

    def bias_index(b, hp, rb):
        variant = jnp.where(rb == 0, 0, jnp.where(rb == n_rb - 1, 2, 1))
        return (variant, hp, 0, 0)

    ctx_spec = pl.BlockSpec((1, n_ctx, LANES), lambda b, hp, rb: (b, 0, hp))
    q_spec = pl.BlockSpec((1, tq, LANES), lambda b, hp, rb: (b, rb, hp))
    return pl.pallas_call(
        _na_kernel,
        grid=(bsz, NA_HEADS // HEADS_PER_STEP, n_rb),
        in_specs=([q_spec] + [kv_spec(j) for j in range(N_K_CHUNKS)] + [kv_spec(j) for j in range(N_K_CHUNKS)]
                  + [ctx_spec, ctx_spec,
                     pl.BlockSpec((1, HEADS_PER_STEP, tq, K_ROWS * GRID_W), bias_index)]),
        out_specs=q_spec,
        out_shape=jax.ShapeDtypeStruct((bsz, length, NA_WIDTH), BF16),
        compiler_params=_cparams(3),
        name="neighborhood_attention",
    )(q, *([k] * N_K_CHUNKS), *([v] * N_K_CHUNKS), kc, vc, bias_tables)


def _ctx_attn_kernel(q_ref, k_ref, v_ref, o_ref):
    n = q_ref.shape[1]
    o = _two_head_attention(q_ref[0], [k_ref[0]], [v_ref[0]], None, 0, n)
    o_ref[0] = o.astype(BF16)


def _context_attention(qc, kc, vc):
    bsz, n_ctx, _ = qc.shape
    spec = pl.BlockSpec((1, n_ctx, LANES), lambda b, hp: (b, 0, hp))
    return pl.pallas_call(
        _ctx_attn_kernel,
        grid=(bsz, NA_HEADS // HEADS_PER_STEP),
        in_specs=[spec, spec, spec],
        out_specs=spec,
        out_shape=jax.ShapeDtypeStruct((bsz, n_ctx, NA_WIDTH), BF16),
        compiler_params=_cparams(2),
        name="context_attention",
    )(qc, kc, vc)


def _dft_cos_sin(n):
    ang = 2.0 * np.pi * np.outer(np.arange(n), np.arange(n)) / n
    return np.cos(ang), np.sin(ang)


def _bf16_table(a):
    return jnp.asarray(a, F32).astype(BF16)


def _channel_dft_matrix():
    c, s = _dft_cos_sin(F_GROUP_DIM)
    scale = F_GROUP_DIM ** -0.5
    eye = np.eye(F_GROUPS)
    return np.concatenate([np.kron(eye, c), np.kron(eye, s)], axis=1) * scale


def _fft_stage1_kernel(f_ref, w1_ref, cs_ref, sc_ref, tc_ref, ts_ref, zr_ref, zi_ref, *, n_slow, nt):
    x = jnp.concatenate([f_ref[0, :, t * F_WIDTH:(t + 1) * F_WIDTH] for t in range(nt)], axis=0)
    ab = _dot(x, w1_ref[...]).astype(BF16)
    for t in range(nt):
        a = ab[t * n_slow:(t + 1) * n_slow, :F_WIDTH]
        b = ab[t * n_slow:(t + 1) * n_slow, F_WIDTH:]
        z = _dot(cs_ref[...], a) + _dot(sc_ref[...], b)
        zr, zi = z[:n_slow], z[n_slow:]
        c, s = tc_ref[t], ts_ref[t]
        zr_ref[0, t] = (zr * c - zi * s).astype(BF16)
        zi_ref[0, t] = (zr * s + zi * c).astype(BF16)


def _fft_stage2_kernel(zr_ref, zi_ref, fc_ref, fs_ref, y_ref):
    y_ref[0] = (_dot(fc_ref[...], zr_ref[0]) + _dot(fs_ref[...], zi_ref[0])).astype(BF16)


def _fourier_positions(f, n_slow, n_fast):
    bsz, n, _ = f.shape
    assert n == n_slow * n_fast
    nt = 8
    w1 = _bf16_table(_channel_dft_matrix())
    c1, s1 = _dft_cos_sin(n_slow)
    sc1 = n_slow ** -0.5
    cs = _bf16_table(np.concatenate([c1, s1], axis=0) * sc1)
    sc = _bf16_table(np.concatenate([-s1, c1], axis=0) * sc1)
    tw = 2.0 * np.pi * np.outer(np.arange(n_fast), np.arange(n_slow)) / n
    tc = jnp.asarray(np.cos(tw)[:, :, None], F32)
    ts = jnp.asarray(np.sin(tw)[:, :, None], F32)
    z_shape = jax.ShapeDtypeStruct((bsz, n_fast, n_slow, F_WIDTH), BF16)
    z_spec = pl.BlockSpec((1, nt, n_slow, F_WIDTH), lambda b, j: (b, j, 0, 0))
    const2 = lambda b, j: (0, 0)
    tw_spec = pl.BlockSpec((nt, n_slow, 1), lambda b, j: (j, 0, 0))
    zr, zi = pl.pallas_call(
        functools.partial(_fft_stage1_kernel, n_slow=n_slow, nt=nt),
        grid=(bsz, n_fast // nt),
        in_specs=[pl.BlockSpec((1, n_slow, nt * F_WIDTH), lambda b, j: (b, 0, j)),
                  pl.BlockSpec(w1.shape, const2), pl.BlockSpec(cs.shape, const2), pl.BlockSpec(sc.shape, const2),
                  tw_spec, tw_spec],
        out_specs=[z_spec, z_spec],
        out_shape=[z_shape, z_shape],
        compiler_params=_cparams(2),
        name="fnet_stage1",
    )(f.reshape(bsz, n_slow, n_fast * F_WIDTH), w1, cs, sc, tc, ts)

    c2, s2 = _dft_cos_sin(n_fast)
    sc2 = n_fast ** -0.5
    fc = _bf16_table(c2 * sc2)
    fs = _bf16_table(-s2 * sc2)
    cols = n_slow * F_WIDTH
    wcol = min(cols, 8192)
    blk = pl.BlockSpec((1, n_fast, wcol), lambda b, j: (b, 0, j))
    y = pl.pallas_call(
        _fft_stage2_kernel,
        grid=(bsz, cols // wcol),
        in_specs=[blk, blk, pl.BlockSpec(fc.shape, const2), pl.BlockSpec(fs.shape, const2)],
        out_specs=blk,
        out_shape=jax.ShapeDtypeStruct((bsz, n_fast, cols), BF16),
        compiler_params=_cparams(2),
        name="fnet_stage2",
    )(zr.reshape(bsz, n_fast, cols), zi.reshape(bsz, n_fast, cols), fc, fs)
    return y.reshape(bsz, n, F_WIDTH)


def _second_largest_sum(a, b, c, d):
    mab, nab = jnp.maximum(a, b), jnp.minimum(a, b)
    mcd, ncd = jnp.maximum(c, d), jnp.minimum(c, d)
    return jnp.maximum(mab, mcd) + jnp.maximum(jnp.minimum(mab, mcd), jnp.maximum(nab, ncd))


def _selected_group(sb_rows):
    epg = EXPERTS_PER_GROUP
    g_score = [_second_largest_sum(*sb_rows[g * epg:(g + 1) * epg]) for g in range(N_GROUPS)]
    best = functools.reduce(jnp.maximum, g_score)
    group = jnp.full_like(best, float(N_GROUPS - 1))
    for g in range(N_GROUPS - 2, -1, -1):
        group = jnp.where(g_score[g] == best, float(g), group)
    return group


def _top2_gates(cand_s, cand_sb):
    n = len(cand_s)
    w = []
    for j in range(n):
        rank = jnp.zeros_like(cand_sb[j])
        for i in range(n):
            if i == j:
                continue
            ahead = (cand_sb[i] > cand_sb[j]) | ((cand_sb[i] == cand_sb[j]) & (i < j))
            rank = rank + jnp.where(ahead, 1.0, 0.0)
        w.append(jnp.where(rank < 2.0, cand_s[j], 0.0))
    total = functools.reduce(jnp.add, w)
    return [wj / total for wj in w]


def _out_kernel(a_ref, y_ref, x_ref, m_ref, wo_ref, bd_ref, g_ref, b_ref, wr_ref, rb_ref,
                x1_ref, h2_ref, grp_ref):
    y2 = _dot(y_ref[0], bd_ref[...]).astype(BF16)
    o = _dot(a_ref[0], wo_ref[:NA_WIDTH, :]) + _dot(y2, wo_ref[NA_WIDTH:, :])
    z = DEEPNORM_ALPHA * x_ref[0] + m_ref[0, 2:3, :] * o
    x1 = _layer_norm(z) * g_ref[...] + b_ref[...]
    x1_ref[0] = x1
    h2 = _layer_norm(x1) * (1.0 + m_ref[0, 4:5, :]) + m_ref[0, 3:4, :]
    _to_slabs(h2_ref, h2)
    sb =jax.nn.sigmoid(_dot_nt(wr_ref[...], h2.astype(BF16))) + rb_ref[...]
    group = _selected_group([sb[e:e + 1] for e in range(N_EXPERTS)])
    grp_ref[0] = jnp.broadcast_to(group, grp_ref.shape[1:])


def _out_projection(attn, yf, x, m, w_out_bf16, w_four_bd, ln_g, ln_b, w_router_t, router_bias, tm):
    bsz, length, _ = x.shape
    tiles_per_batch = length // tm
    row = lambda b, i: (b, i, 0)
    const2 = lambda b, i: (0, 0)
    return pl.pallas_call(
        _out_kernel,
        grid=(bsz, length // tm),
        in_specs=[pl.BlockSpec((1, tm, NA_WIDTH), row), pl.BlockSpec((1, tm, F_WIDTH), row),
                  pl.BlockSpec((1, tm, D_MODEL), row),
                  pl.BlockSpec((1, N_MOD, D_MODEL), lambda b, i: (b, 0, 0)),
                  pl.BlockSpec((NA_WIDTH + F_WIDTH, D_MODEL), const2),
                  pl.BlockSpec((F_WIDTH, F_WIDTH), const2),
                  pl.BlockSpec((1, D_MODEL), const2), pl.BlockSpec((1, D_MODEL), const2),
                  pl.BlockSpec((N_EXPERTS, D_MODEL), const2), pl.BlockSpec((N_EXPERTS, 1), const2)],
        out_specs=[pl.BlockSpec((1, tm, D_MODEL), row),
                   pl.BlockSpec((tm * SLAB_ROWS, LANES), lambda b, i: (b * tiles_per_batch + i, 0)),
                   pl.BlockSpec((1, SUBLANES, tm), lambda b, i: (b, 0, i))],
        out_shape=[jax.ShapeDtypeStruct((bsz, length, D_MODEL), F32),
                   jax.ShapeDtypeStruct((bsz * length * SLAB_ROWS, LANES), F32),
                   jax.ShapeDtypeStruct((bsz, SUBLANES, length), F32)],
        compiler_params=_cparams(2),
        name="out_proj_norm_route",
    )(attn, yf, x, m, w_out_bf16, w_four_bd, ln_g.reshape(1, D_MODEL), ln_b.reshape(1, D_MODEL),
      w_router_t, router_bias.reshape(N_EXPERTS, 1))


def _group_plan(group, tm):
    n = group.shape[0]
    n_slots = n + N_GROUPS * tm
    onehot = (group[:, None] == jnp.arange(N_GROUPS, dtype=jnp.int32)[None, :]).astype(jnp.int32)
    csum = jnp.cumsum(onehot, axis=0)
    rank = jnp.sum(onehot * csum, axis=1) - 1
    count = csum[-1]
    padded = ((count + tm - 1) // tm) * tm
    end = jnp.cumsum(padded)
    start = end - padded
    slot = jnp.sum(onehot * start[None, :], axis=1) + rank
    tile_start = jnp.arange(n_slots // tm, dtype=jnp.int32) * tm
    tile_group = jnp.minimum(jnp.sum((end[None, :] <= tile_start[:, None]).astype(jnp.int32), axis=1),
                             N_GROUPS - 1)
    tile_rows = jnp.clip((start + count)[tile_group] - tile_start, 0, tm)
    return slot, tile_group, tile_rows, start, start + count, end


ROW_DMA_UNROLL = 8


def _slab(ref, token):
    return ref.at[pl.ds(pl.multiple_of(token * SLAB_ROWS, SLAB_ROWS), SLAB_ROWS), :]


def _row_scatter(idx_ref, base, src_ref, dst_hbm, sem, n_rows, wait, inverse_ref=None):
    def one(r, priority):
        if wait:
            pltpu.make_async_copy(_slab(src_ref, 0), _slab(dst_hbm, 0), sem).wait()
        else:
            idx = idx_ref[base + r]
            if inverse_ref is not None:
                inverse_ref[idx] = base + r
            pltpu.make_async_copy(_slab(src_ref, r), _slab(dst_hbm, idx), sem).start(priority=priority)

    def chunk(j, carry):
        for u in range(ROW_DMA_UNROLL):
            one(j * ROW_DMA_UNROLL + u, u % 2)
        return carry

    def single(r, carry):
        one(r, 0)
        return carry

    n_chunks = n_rows // ROW_DMA_UNROLL
    lax.fori_loop(0, n_chunks, chunk, 0)
    lax.fori_loop(n_chunks * ROW_DMA_UNROLL, n_rows, single, 0)


def _to_slots_kernel(slot_ref, start_ref, rows_end_ref, end_ref, h_ref, o_hbm, dst_ref, zbuf, sem,
                     *, tm, tm_slots):
    step = pl.program_id(0)
    n_slots = dst_ref.shape[0]

    @pl.when(step == 0)
    def _():
        zbuf[...] = jnp.zeros_like(zbuf)

        def fill(slot0):
            rows = tm_slots * SLAB_ROWS
            return pltpu.make_async_copy(zbuf, o_hbm.at[pl.ds(pl.multiple_of(slot0 * SLAB_ROWS, rows), rows), :], sem)

        for g in range(N_GROUPS):
            @pl.when(end_ref[g] > start_ref[g])
            def _():
                fill(end_ref[g] - tm_slots).start()
        for g in range(N_GROUPS):
            @pl.when(end_ref[g] > start_ref[g])
            def _():
                fill(end_ref[g] - tm_slots).wait()

        def fill_unused(t, carry):
            fill(t * tm_slots).start()
            fill(t * tm_slots).wait()
            return carry
        lax.fori_loop(end_ref[N_GROUPS - 1] // tm_slots, n_slots // tm_slots, fill_unused, 0)

        def clear(p, carry):
            dst_ref[p] = 0
            return carry
        for g in range(N_GROUPS):
            lax.fori_loop(rows_end_ref[g], end_ref[g], clear, 0)
        lax.fori_loop(end_ref[N_GROUPS - 1], n_slots, clear, 0)

    _row_scatter(slot_ref, step * tm, h_ref, o_hbm, sem, tm, wait=False, inverse_ref=dst_ref)
    _row_scatter(slot_ref, step * tm, h_ref, o_hbm, sem, tm, wait=True)


def _to_slots(h2_slabs, slot, start, rows_end, end, tm, tm_slots):
    n = h2_slabs.shape[0] // SLAB_ROWS
    n_slots = n + N_GROUPS * tm_slots
    grid_spec = pltpu.PrefetchScalarGridSpec(
        num_scalar_prefetch=4,
        grid=(n // tm,),
        in_specs=[pl.BlockSpec((tm * SLAB_ROWS, LANES), lambda i, *_: (i, 0))],
        out_specs=[pl.BlockSpec(memory_space=pl.ANY), pl.BlockSpec(memory_space=pltpu.SMEM)],
        scratch_shapes=[pltpu.VMEM((tm_slots * SLAB_ROWS, LANES), F32), pltpu.SemaphoreType.DMA(())],
    )
    return pl.pallas_call(
        functools.partial(_to_slots_kernel, tm=tm, tm_slots=tm_slots),
        grid_spec=grid_spec,
        out_shape=[jax.ShapeDtypeStruct((n_slots * SLAB_ROWS, LANES), F32),
                   jax.ShapeDtypeStruct((n_slots,), jnp.int32)],
        compiler_params=_cparams(1),
        name="moe_rows_to_slots",
    )(slot, start, rows_end, end, h2_slabs)


def _moe_group_kernel(dst_ref, tile_group_ref, tile_rows_ref, n_used_ref, h_ref, wg_ref, wu_ref, wd_ref, wr_ref,
                      rb_ref, y_hbm, ybuf, sems, *, tm, n_tiles):
    del tile_group_ref
    step = pl.program_id(0)
    n_used = n_used_ref[0]
    buf = step % 2

    def scatter(tile, wait):
        _row_scatter(dst_ref, tile * tm, ybuf.at[tile % 2], y_hbm, sems.at[tile % 2], tile_rows_ref[tile], wait)

    @pl.when((step >= 2) & (step < n_used))
    def _():
        scatter(step - 2, wait=True)

    @pl.when(step < n_used)
    def _():
        h = _from_slabs(h_ref, tm).astype(BF16)
        s = jax.nn.sigmoid(_dot(h, wr_ref[0]))
        sb = s + rb_ref[0]
        epg = EXPERTS_PER_GROUP
        gates = _top2_gates([s[:, j:j + 1] for j in range(epg)], [sb[:, j:j + 1] for j in range(epg)])
        acc = None
        for e in range(epg):
            gate = _dot(h, wg_ref[e])
            up = _dot(h, wu_ref[e])
            hid = (gate * jax.nn.sigmoid(gate)) * up * gates[e]
            y = _dot(hid.astype(BF16), wd_ref[e])
            acc = y if acc is None else acc + y
        _to_slabs(ybuf.at[buf], acc)
        scatter(step, wait=False)

    @pl.when(step == n_tiles - 1)
    def _():
        @pl.when(n_used >= 2)
        def _():
            scatter(n_used - 2, wait=True)

        @pl.when(n_used >= 1)
        def _():
            scatter(n_used - 1, wait=True)


def _moe_experts(h_slots, dst, tile_group, tile_rows, n_used, n_tokens, w_gate, w_up, w_down, w_router_grp,
                 router_bias_grp, tm):
    n_slots = h_slots.shape[0] // SLAB_ROWS
    n_tiles = n_slots // tm
    epg = EXPERTS_PER_GROUP
    by_group = lambda i, dst_ref, tg_ref, tr_ref, nu_ref: (tg_ref[i], 0, 0)
    grid_spec = pltpu.PrefetchScalarGridSpec(
        num_scalar_prefetch=4,
        grid=(n_tiles,),
        in_specs=[pl.BlockSpec((tm * SLAB_ROWS, LANES),
                               lambda i, dst_ref, tg_ref, tr_ref, nu_ref: (jnp.minimum(i, nu_ref[0] - 1), 0)),
                  pl.BlockSpec((epg, D_MODEL, D_EXPERT), by_group),
                  pl.BlockSpec((epg, D_MODEL, D_EXPERT), by_group),
                  pl.BlockSpec((epg, D_EXPERT, D_MODEL), by_group),
                  pl.BlockSpec((1, D_MODEL, LANES), by_group),
                  pl.BlockSpec((1, 1, LANES), by_group)],
        out_specs=pl.BlockSpec(memory_space=pl.ANY),
        scratch_shapes=[pltpu.VMEM((2, tm * SLAB_ROWS, LANES), F32), pltpu.SemaphoreType.DMA((2,))],
    )
    return pl.pallas_call(
        functools.partial(_moe_group_kernel, tm=tm, n_tiles=n_tiles),
        grid_spec=grid_spec,
        out_shape=jax.ShapeDtypeStruct((n_tokens * SLAB_ROWS, LANES), F32),
        compiler_params=_cparams(1),
        name="moe_group_experts",
    )(dst, tile_group, tile_rows, n_used, h_slots, w_gate, w_up, w_down, w_router_grp, router_bias_grp)


def _residual_norm_kernel(y_ref, x1_ref, m_ref, g_ref, b_ref, o_ref):
    y = _from_slabs(y_ref, x1_ref.shape[0])
    z = DEEPNORM_ALPHA * x1_ref[...] + m_ref[0, 5:6, :] * y
    o_ref[...] = _layer_norm(z) * g_ref[...] + b_ref[...]


def _residual_norm(y_slabs, x1, m, ln_g, ln_b, tm):
    bsz, length, _ = x1.shape
    n = bsz * length
    tiles_per_batch = length // tm
    const2 = lambda i: (0, 0)
    row = pl.BlockSpec((tm, D_MODEL), lambda i: (i, 0))
    out = pl.pallas_call(
        _residual_norm_kernel,
        grid=(n // tm,),
        in_specs=[pl.BlockSpec((tm * SLAB_ROWS, LANES), lambda i: (i, 0)), row,
                  pl.BlockSpec((1, N_MOD, D_MODEL), lambda i: (i // tiles_per_batch, 0, 0)),
                  pl.BlockSpec((1, D_MODEL), const2), pl.BlockSpec((1, D_MODEL), const2)],
        out_specs=row,
        out_shape=jax.ShapeDtypeStruct((n, D_MODEL), F32),
        compiler_params=_cparams(1),
        name="moe_residual_norm",
    )(y_slabs, x1.reshape(n, D_MODEL), m, ln_g.reshape(1, D_MODEL), ln_b.reshape(1, D_MODEL))
    return out.reshape(bsz, length, D_MODEL)


def _grouped_moe(h2, group_rows, x1, m, w_gate, w_up, w_down, w_router_grp, router_bias_grp, ln_g, ln_b,
                 tm_experts, tm_tokens):
    bsz, length, _ = x1.shape
    n = bsz * length
    group = group_rows[:, 0, :].reshape(n).astype(jnp.int32)
    slot, tile_group, tile_rows, start, rows_end, end = _group_plan(group, tm_experts)
    h_slots, dst = _to_slots(h2, slot, start, rows_end, end, tm_tokens, tm_experts)
    n_used = (end[N_GROUPS - 1:] // tm_experts).astype(jnp.int32)
    y = _moe_experts(h_slots, dst, tile_group, tile_rows, n_used, n, w_gate, w_up, w_down, w_router_grp,
                     router_bias_grp, tm_experts)
    return _residual_norm(y, x1, m, ln_g, ln_b, tm_tokens)


def _ctx_fourier_kernel(f_ref, w1_ref, c_ref, s_ref, y_ref):
    ab = _dot(f_ref[0], w1_ref[...]).astype(BF16)
    y = _dot(c_ref[...], ab[:, :F_WIDTH]) + _dot(s_ref[...], ab[:, F_WIDTH:])
    y_ref[0] = y.astype(BF16)


def _context_fourier(fc):
    bsz, n, _ = fc.shape
    w1 = _bf16_table(_channel_dft_matrix())
    c, s = _dft_cos_sin(n)
    cm = _bf16_table(c * n ** -0.5)
    sm = _bf16_table(-s * n ** -0.5)
    const2 = lambda b: (0, 0)
    blk = pl.BlockSpec((1, n, F_WIDTH), lambda b: (b, 0, 0))
    return pl.pallas_call(
        _ctx_fourier_kernel,
        grid=(bsz,),
        in_specs=[blk, pl.BlockSpec(w1.shape, const2), pl.BlockSpec(cm.shape, const2),
                  pl.BlockSpec(sm.shape, const2)],
        out_specs=blk,
        out_shape=jax.ShapeDtypeStruct((bsz, n, F_WIDTH), BF16),
        compiler_params=_cparams(1),
        name="context_fnet",
    )(fc, w1, cm, sm)


def _block_diag(w):
    g, c, _ = w.shape
    eye = jnp.eye(g, dtype=w.dtype)
    return (eye[:, None, :, None] * w[:, :, None, :]).reshape(g * c, g * c)


def kernel(x, c, ctx, c_ctx, w_mod, b_mod, w_in, rpb, w_four, w_out, ln1_g, ln1_b, ln2_g, ln2_b,
           w_router, router_bias, w_gate, w_up, w_down):
    bsz, length, _ = x.shape
    n_ctx = ctx.shape[1]
    rows = length // GRID_W

    cvec = jnp.concatenate([c, c_ctx[None, :], jnp.zeros((8 - bsz - 1, D_MODEL), F32)], axis=0)
    mods = _modulation(cvec, w_mod, b_mod)
    w_router_t = w_router.T.astype(BF16)
    lane_pad = LANES - EXPERTS_PER_GROUP
    w_router_grp = jnp.pad(w_router.reshape(D_MODEL, N_GROUPS, EXPERTS_PER_GROUP).transpose(1, 0, 2),
                           ((0, 0), (0, 0), (0, lane_pad))).astype(BF16)
    router_bias_grp = jnp.pad(router_bias.astype(F32).reshape(N_GROUPS, 1, EXPERTS_PER_GROUP),
                              ((0, 0), (0, 0), (0, lane_pad)))

    xc = ctx
    for i in range(DEPTH):
        last = i == DEPTH - 1
        m = mods[i, :bsz].reshape(bsz, N_MOD, D_MODEL)
        mc = jnp.broadcast_to(mods[i, bsz].reshape(1, N_MOD, D_MODEL), (bsz, N_MOD, D_MODEL))
        w_in_b = w_in[i].astype(BF16)
        w_out_b = w_out[i].astype(BF16)
        w_four_bd = _block_diag(w_four[i]).astype(BF16)
        wg, wu, wd = w_gate[i].astype(BF16), w_up[i].astype(BF16), w_down[i].astype(BF16)

        q, k, v, f = _in_projection(x, m, w_in_b, tm=512)
        qc, kc, vc, fc = _in_projection(xc, mc, w_in_b, tm=n_ctx)

        attn = _neighborhood_attention(q, k, v, kc, vc, _na_bias_tables(rpb[i], rows))
        yf = _fourier_positions(f, n_slow=rows, n_fast=GRID_W)
        x1, h2, grp = _out_projection(attn, yf, x, m, w_out_b, w_four_bd, ln1_g[i], ln1_b[i],
                                      w_router_t, router_bias, tm=512)
        x = _grouped_moe(h2, grp, x1, m, wg, wu, wd, w_router_grp, router_bias_grp, ln2_g[i], ln2_b[i],
                         tm_experts=512, tm_tokens=512)
        if last:
            break

        attn_c = _context_attention(qc, kc, vc)
        yc = _context_fourier(fc)
        xc1, h2c, grp_c = _out_projection(attn_c, yc, xc, mc, w_out_b, w_four_bd, ln1_g[i], ln1_b[i],
                                          w_router_t, router_bias, tm=n_ctx)
        xc = _grouped_moe(h2c, grp_c, xc1, mc, wg, wu, wd, w_router_grp, router_bias_grp, ln2_g[i], ln2_b[i],
                          tm_experts=128, tm_tokens=n_ctx)
    return x
```

```python
import functools
import math

import numpy as np
import jax
import jax.numpy as jnp
from jax import lax
from jax.experimental import pallas as pl
from jax.experimental.pallas import tpu as pltpu

D_MODEL = 1024
DEPTH = 2
GRID_W = 64
NA_HEADS = 8
HEAD_DIM = 64
NA_WIDTH = NA_HEADS * HEAD_DIM
WIN_ROWS = 8
WIN_COLS = 16
F_GROUPS = 8
F_GROUP_DIM = 64
F_WIDTH = F_GROUPS * F_GROUP_DIM
IN_WIDTH = 3 * NA_WIDTH + F_WIDTH
N_EXPERTS = 16
N_GROUPS = 4
EXPERTS_PER_GROUP = N_EXPERTS // N_GROUPS
D_EXPERT = 256
N_MOD = 6
DEEPNORM_ALPHA = (2.0 * DEPTH) ** 0.25
LN_EPS = 1e-6

F32 = jnp.float32
BF16 = jnp.bfloat16

V7X_VMEM_BYTES = 64 * 1024 * 1024
VMEM_LIMIT_BYTES = (V7X_VMEM_BYTES * 3) // 4
LANES = 128
SUBLANES = 8
HEADS_PER_STEP = LANES // HEAD_DIM
MASK_VALUE = -1e30

Q_ROWS = 8
K_ROWS = 16
K_CHUNK_ROWS = 4
N_K_CHUNKS = K_ROWS // K_CHUNK_ROWS


def _cparams(n_grid_dims):
    return pltpu.CompilerParams(dimension_semantics=("arbitrary",) * n_grid_dims,
                                vmem_limit_bytes=VMEM_LIMIT_BYTES)


def _layer_norm(x):
    mu = jnp.mean(x, axis=-1, keepdims=True)
    xc = x - mu
    var = jnp.mean(xc * xc, axis=-1, keepdims=True)
    return xc * lax.rsqrt(var + LN_EPS)


SLAB_ROWS = D_MODEL // LANES
assert SLAB_ROWS == SUBLANES


def _to_slabs(ref, x):
    for j in range(SLAB_ROWS):
        ref[pl.ds(j, x.shape[0], stride=SLAB_ROWS), :] = x[:, j * LANES:(j + 1) * LANES]


def _from_slabs(ref, n_tokens):
    return jnp.concatenate([ref[pl.ds(j, n_tokens, stride=SLAB_ROWS), :] for j in range(SLAB_ROWS)], axis=-1)


def _dot(a, b):
    return jnp.dot(a, b, preferred_element_type=F32)


def _dot_nt(a, b):
    return lax.dot_general(a, b, (((1,), (1,)), ((), ())), preferred_element_type=F32)


def _mod_kernel(c_ref, w_ref, b_ref, o_ref):
    c = c_ref[...]
    a = c * jax.nn.sigmoid(c)
    o_ref[0] = jnp.dot(a, w_ref[0], preferred_element_type=F32, precision=lax.Precision.HIGHEST) + b_ref[0]


def _modulation(cvec, w_mod, b_mod):
    n_col_blocks = 4
    wc = (N_MOD * D_MODEL) // n_col_blocks
    rows = cvec.shape[0]
    return pl.pallas_call(
        _mod_kernel,
        grid=(DEPTH, n_col_blocks),
        in_specs=[pl.BlockSpec((rows, D_MODEL), lambda i, j: (0, 0)),
                  pl.BlockSpec((1, D_MODEL, wc), lambda i, j: (i, 0, j)),
                  pl.BlockSpec((1, 1, wc), lambda i, j: (i, 0, j))],
        out_specs=pl.BlockSpec((1, rows, wc), lambda i, j: (i, 0, j)),
        out_shape=jax.ShapeDtypeStruct((DEPTH, rows, N_MOD * D_MODEL), F32),
        compiler_params=_cparams(2),
        name="modulation",
    )(cvec, w_mod, b_mod.reshape(DEPTH, 1, N_MOD * D_MODEL))


def _proj_kernel(x_ref, m_ref, w_ref, q_ref, k_ref, v_ref, f_ref):
    h = _layer_norm(x_ref[0]) * (1.0 + m_ref[0, 1:2, :]) + m_ref[0, 0:1, :]
    p = _dot(h.astype(BF16), w_ref[...])
    q_ref[0] = (p[:, :NA_WIDTH] * (HEAD_DIM ** -0.5)).astype(BF16)
    k_ref[0] = p[:, NA_WIDTH:2 * NA_WIDTH].astype(BF16)
    v_ref[0] = p[:, 2 * NA_WIDTH:3 * NA_WIDTH].astype(BF16)
    f_ref[0] = p[:, 3 * NA_WIDTH:].astype(BF16)


def _in_projection(x, m, w_in_bf16, tm):
    bsz, length, _ = x.shape
    out = jax.ShapeDtypeStruct((bsz, length, NA_WIDTH), BF16)
    o_spec = pl.BlockSpec((1, tm, NA_WIDTH), lambda b, i: (b, i, 0))
    return pl.pallas_call(
        _proj_kernel,
        grid=(bsz, length // tm),
        in_specs=[pl.BlockSpec((1, tm, D_MODEL), lambda b, i: (b, i, 0)),
                  pl.BlockSpec((1, N_MOD, D_MODEL), lambda b, i: (b, 0, 0)),
                  pl.BlockSpec((D_MODEL, IN_WIDTH), lambda b, i: (0, 0))],
        out_specs=[o_spec, o_spec, o_spec, o_spec],
        out_shape=[out, out, out, out],
        compiler_params=_cparams(2),
        name="ln_mod_in_proj",
    )(x, m, w_in_bf16)


def _softmax_pv(qh, keys, values, bias):
    scores = [_dot_nt(qh, k) for k in keys]
    scores[0] = scores[0] + bias if bias is not None else scores[0]
    m = functools.reduce(jnp.maximum, [jnp.max(s, axis=-1, keepdims=True) for s in scores])
    probs = [jnp.exp(s - m) for s in scores]
    denom = functools.reduce(jnp.add, [jnp.sum(p, axis=-1, keepdims=True) for p in probs])
    o = functools.reduce(jnp.add, [_dot(p.astype(BF16), v) for p, v in zip(probs, values)])
    return o / denom


def _two_head_attention(q, keys, values, bias_of_head):
    lane = lax.broadcasted_iota(jnp.int32, (1, LANES), 1)
    out = None
    for h in range(HEADS_PER_STEP):
        in_head = (lane >= HEAD_DIM * h) & (lane < HEAD_DIM * (h + 1))
        qh = jnp.where(in_head, q, jnp.zeros_like(q))
        o = _softmax_pv(qh, keys, values, bias_of_head(h))
        out = o if out is None else jnp.where(in_head, o, out)
    return out


DR_PAD = 2 * WIN_ROWS
DC_PAD = 2 * WIN_COLS


def _toeplitz_kernel(r_ref, sel_ref, mask_ref, o_ref):
    n = r_ref.shape[0]
    for qc in range(GRID_W):
        block = jnp.dot(r_ref[...], sel_ref[qc], preferred_element_type=F32, precision=lax.Precision.HIGHEST)
        o_ref[pl.ds(qc, n, stride=GRID_W), :] = block + mask_ref[qc]


def _rpb_toeplitz(rpb):
    depth, heads, n_dr, n_dc = rpb.shape
    qc = np.arange(GRID_W)[:, None]
    kc = (np.arange(LANES) % GRID_W)[None, :]
    cs = np.clip(qc - WIN_COLS // 2, 0, GRID_W - WIN_COLS)
    col_valid = (kc >= cs) & (kc < cs + WIN_COLS)
    dc = kc - qc + WIN_COLS - 1
    select = (np.arange(DC_PAD)[None, :, None] == dc[:, None, :]) & col_valid[:, None, :]
    mask = np.where(col_valid, 0.0, MASK_VALUE)[:, None, :]
    r = jnp.pad(rpb.astype(F32), ((0, 0), (0, 0), (0, DR_PAD - n_dr), (0, DC_PAD - n_dc)))
    n = depth * heads * DR_PAD
    const3 = lambda: (0, 0, 0)
    out = pl.pallas_call(
        _toeplitz_kernel,
        grid=(),
        in_specs=[pl.BlockSpec((n, DC_PAD), lambda: (0, 0)),
                  pl.BlockSpec((GRID_W, DC_PAD, LANES), const3), pl.BlockSpec((GRID_W, 1, LANES), const3)],
        out_specs=pl.BlockSpec((n * GRID_W, LANES), lambda: (0, 0)),
        out_shape=jax.ShapeDtypeStruct((n * GRID_W, LANES), F32),
        compiler_params=pltpu.CompilerParams(vmem_limit_bytes=VMEM_LIMIT_BYTES),
        name="rpb_toeplitz",
    )(r.reshape(n, DC_PAD), jnp.asarray(select, F32), jnp.asarray(mask, F32))
    return out.reshape(depth, heads, DR_PAD, GRID_W, LANES)


Q_CHUNK_ROWS = 4
WINDOW_CHUNKS = 3
N_Q_CHUNKS = Q_ROWS // Q_CHUNK_ROWS
assert Q_CHUNK_ROWS == K_CHUNK_ROWS and WINDOW_CHUNKS * K_CHUNK_ROWS >= Q_CHUNK_ROWS + WIN_ROWS - 1
assert N_Q_CHUNKS == 2 and N_K_CHUNKS == 4


def _key_row_start(rb, rows):
    return int(np.clip(rb * Q_ROWS - (K_ROWS - Q_ROWS) // 2, 0, rows - K_ROWS))


def _window_uses_last_chunk(rb, c, n_rb):
    if c == 0:
        return rb == n_rb - 1
    return rb != 0


def _window_key_rows(rb, c, rows):
    var = N_K_CHUNKS - 1 if _window_uses_last_chunk(rb, c, rows // Q_ROWS) else 0
    k0 = _key_row_start(rb, rows)
    return [k0 + K_CHUNK_ROWS * j + i for j in (var, 1, 2) for i in range(K_CHUNK_ROWS)]


def _row_window(qr, rows):
    kh = min(WIN_ROWS, rows)
    rs = int(np.clip(qr - kh // 2, 0, rows - kh))
    return rs, rs + kh


def _check_windows(rows):
    def relative(rb):
        base = rb * Q_ROWS
        return [([kr - base for kr in _window_key_rows(rb, c, rows)],
                 [tuple(r - base for r in _row_window(base + c * Q_CHUNK_ROWS + qi, rows))
                  for qi in range(Q_CHUNK_ROWS)]) for c in range(N_Q_CHUNKS)]

    n_rb = rows // Q_ROWS
    for rb in range(n_rb):
        assert rb in (0, n_rb - 1) or relative(rb) == relative(1), rb
        for c in range(N_Q_CHUNKS):
            have = set(_window_key_rows(rb, c, rows))
            for qi in range(Q_CHUNK_ROWS):
                lo, hi = _row_window(rb * Q_ROWS + c * Q_CHUNK_ROWS + qi, rows)
                assert set(range(lo, hi)) <= have, (rb, c, qi)


def _build_bias_tables(t_ref, bias_scr, rows):
    n_rb = rows // Q_ROWS
    left = lax.broadcasted_iota(jnp.int32, (GRID_W, LANES), 1) < GRID_W
    masked = jnp.full((GRID_W, LANES), MASK_VALUE, F32)
    for variant, rb in enumerate((0, 1, n_rb - 1)):
        for c in range(N_Q_CHUNKS):
            key_rows = _window_key_rows(rb, c, rows)
            for h in range(HEADS_PER_STEP):
                for qi in range(Q_CHUNK_ROWS):
                    qr = rb * Q_ROWS + c * Q_CHUNK_ROWS + qi
                    lo, hi = _row_window(qr, rows)
                    for p in range(len(key_rows) // 2):
                        pair = [t_ref[h, kr - qr + WIN_ROWS - 1] if lo <= kr < hi else None
                                for kr in key_rows[2 * p:2 * p + 2]]
                        if pair[0] is None and pair[1] is None:
                            block = masked
                        else:
                            block = jnp.where(left, masked if pair[0] is None else pair[0],
                                              masked if pair[1] is None else pair[1])
                        bias_scr[variant, c, h, qi * GRID_W:(qi + 1) * GRID_W, p * LANES:(p + 1) * LANES] = block


def _na_kernel(q_ref, k0, k1, k2, k3, v0, v1, v2, v3, kc_ref, vc_ref, t_ref, o_ref, bias_scr, *, rows):
    n_rb = rows // Q_ROWS
    b, rb = pl.program_id(1), pl.program_id(2)

    @pl.when((b == 0) & (rb == 0))
    def _():
        _build_bias_tables(t_ref, bias_scr, rows)

    variant = jnp.where(rb == 0, 0, jnp.where(rb == n_rb - 1, 2, 1))
    tq = Q_CHUNK_ROWS * GRID_W
    for c in range(N_Q_CHUNKS):
        use_last = (rb == n_rb - 1) if c == 0 else (rb != 0)
        keys = [jnp.concatenate([jnp.where(use_last, k3[0], k0[0]), k1[0], k2[0]], axis=0), kc_ref[0]]
        values = [jnp.concatenate([jnp.where(use_last, v3[0], v0[0]), v1[0], v2[0]], axis=0), vc_ref[0]]
        q = q_ref[0, c * tq:(c + 1) * tq, :]
        o = _two_head_attention(q, keys, values, lambda h: bias_scr[variant, c, h])
        o_ref[0, c * tq:(c + 1) * tq, :] = o.astype(BF16)


def _neighborhood_attention(q, k, v, kc, vc, toeplitz):
    bsz, length, _ = q.shape
    rows = length // GRID_W
    n_rb = rows // Q_ROWS
    assert rows % Q_ROWS == 0 and rows >= K_ROWS + Q_ROWS and n_rb >= 3
    _check_windows(rows)
    n_ctx = kc.shape[1]
    tq = Q_ROWS * GRID_W
    tk = K_CHUNK_ROWS * GRID_W
    max_chunk = (rows - K_ROWS) // K_CHUNK_ROWS
    half = (K_ROWS - Q_ROWS) // 2 // K_CHUNK_ROWS

    def kv_spec(j):
        def index(hp, b, rb):
            start = jnp.clip(rb * (Q_ROWS // K_CHUNK_ROWS) - half, 0, max_chunk)
            return (b, start + j, hp)
        return pl.BlockSpec((1, tk, LANES), index)

    ctx_spec = pl.BlockSpec((1, n_ctx, LANES), lambda hp, b, rb: (b, 0, hp))
    q_spec = pl.BlockSpec((1, tq, LANES), lambda hp, b, rb: (b, rb, hp))
    window = WINDOW_CHUNKS * K_CHUNK_ROWS * GRID_W
    return pl.pallas_call(
        functools.partial(_na_kernel, rows=rows),
        grid=(NA_HEADS // HEADS_PER_STEP, bsz, n_rb),
        in_specs=([q_spec] + [kv_spec(j) for j in range(N_K_CHUNKS)] + [kv_spec(j) for j in range(N_K_CHUNKS)]
                  + [ctx_spec, ctx_spec,
                     pl.BlockSpec((HEADS_PER_STEP, DR_PAD, GRID_W, LANES), lambda hp, b, rb: (hp, 0, 0, 0))]),
        out_specs=q_spec,
        out_shape=jax.ShapeDtypeStruct((bsz, length, NA_WIDTH), BF16),
        scratch_shapes=[pltpu.VMEM((3, N_Q_CHUNKS, HEADS_PER_STEP, Q_CHUNK_ROWS * GRID_W, window), F32)],
        compiler_params=_cparams(3),
        name="neighborhood_attention",
    )(q, *([k] * N_K_CHUNKS), *([v] * N_K_CHUNKS), kc, vc, toeplitz)


def _ctx_attn_kernel(q_ref, k_ref, v_ref, o_ref):
    o = _two_head_attention(q_ref[0], [k_ref[0]], [v_ref[0]], lambda h: None)
    o_ref[0] = o.astype(BF16)


def _context_attention(qc, kc, vc):
    bsz, n_ctx, _ = qc.shape
    spec = pl.BlockSpec((1, n_ctx, LANES), lambda b, hp: (b, 0, hp))
    return pl.pallas_call(
        _ctx_attn_kernel,
        grid=(bsz, NA_HEADS // HEADS_PER_STEP),
        in_specs=[spec, spec, spec],
        out_specs=spec,
        out_shape=jax.ShapeDtypeStruct((bsz, n_ctx, NA_WIDTH), BF16),
        compiler_params=_cparams(2),
        name="context_attention",
    )(qc, kc, vc)


def _dft_cos_sin(n):
    ang = 2.0 * np.pi * np.outer(np.arange(n), np.arange(n)) / n
    return np.cos(ang), np.sin(ang)


def _bf16_table(a):
    return jnp.asarray(a, F32).astype(BF16)


def _channel_dft_matrix():
    c, s = _dft_cos_sin(F_GROUP_DIM)
    scale = F_GROUP_DIM ** -0.5
    eye = np.eye(F_GROUPS)
    return np.concatenate([np.kron(eye, c), np.kron(eye, s)], axis=1) * scale


def _fft_stage1_kernel(f_ref, w1_ref, cs_ref, sc_ref, tc_ref, ts_ref, zr_ref, zi_ref, *, n_slow, nt):
    x = jnp.concatenate([f_ref[0, :, t * F_WIDTH:(t + 1) * F_WIDTH] for t in range(nt)], axis=0)
    ab = _dot(x, w1_ref[...]).astype(BF16)
    for t in range(nt):
        a = ab[t * n_slow:(t + 1) * n_slow, :F_WIDTH]
        b = ab[t * n_slow:(t + 1) * n_slow, F_WIDTH:]
        z = _dot(cs_ref[...], a) + _dot(sc_ref[...], b)
        zr, zi = z[:n_slow], z[n_slow:]
        c, s = tc_ref[t], ts_ref[t]
        zr_ref[0, t] = (zr * c - zi * s).astype(BF16)
        zi_ref[0, t] = (zr * s + zi * c).astype(BF16)


def _fft_stage2_kernel(zr_ref, zi_ref, fc_ref, fs_ref, y_ref):
    y_ref[0] = (_dot(fc_ref[...], zr_ref[0]) + _dot(fs_ref[...], zi_ref[0])).astype(BF16)


def _fourier_positions(f, n_slow, n_fast):
    bsz, n, _ = f.shape
    assert n == n_slow * n_fast
    nt = 8
    w1 = _bf16_table(_channel_dft_matrix())
    c1, s1 = _dft_cos_sin(n_slow)
    sc1 = n_slow ** -0.5
    cs = _bf16_table(np.concatenate([c1, s1], axis=0) * sc1)
    sc = _bf16_table(np.concatenate([-s1, c1], axis=0) * sc1)
    tw = 2.0 * np.pi * np.outer(np.arange(n_fast), np.arange(n_slow)) / n
    tc = jnp.asarray(np.cos(tw)[:, :, None], F32)
    ts = jnp.asarray(np.sin(tw)[:, :, None], F32)
    z_shape = jax.ShapeDtypeStruct((bsz, n_fast, n_slow, F_WIDTH), BF16)
    z_spec = pl.BlockSpec((1, nt, n_slow, F_WIDTH), lambda b, j: (b, j, 0, 0))
    const2 = lambda b, j: (0, 0)
    tw_spec = pl.BlockSpec((nt, n_slow, 1), lambda b, j: (j, 0, 0))
    zr, zi = pl.pallas_call(
        functools.partial(_fft_stage1_kernel, n_slow=n_slow, nt=nt),
        grid=(bsz, n_fast // nt),
        in_specs=[pl.BlockSpec((1, n_slow, nt * F_WIDTH), lambda b, j: (b, 0, j)),
                  pl.BlockSpec(w1.shape, const2), pl.BlockSpec(cs.shape, const2), pl.BlockSpec(sc.shape, const2),
                  tw_spec, tw_spec],
        out_specs=[z_spec, z_spec],
        out_shape=[z_shape, z_shape],
        compiler_params=_cparams(2),
        name="fnet_stage1",
    )(f.reshape(bsz, n_slow, n_fast * F_WIDTH), w1, cs, sc, tc, ts)

    c2, s2 = _dft_cos_sin(n_fast)
    sc2 = n_fast ** -0.5
    fc = _bf16_table(c2 * sc2)
    fs = _bf16_table(-s2 * sc2)
    cols = n_slow * F_WIDTH
    wcol = min(cols, 8192)
    blk = pl.BlockSpec((1, n_fast, wcol), lambda b, j: (b, 0, j))
    y = pl.pallas_call(
        _fft_stage2_kernel,
        grid=(bsz, cols // wcol),
        in_specs=[blk, blk, pl.BlockSpec(fc.shape, const2), pl.BlockSpec(fs.shape, const2)],
        out_specs=blk,
        out_shape=jax.ShapeDtypeStruct((bsz, n_fast, cols), BF16),
        compiler_params=_cparams(2),
        name="fnet_stage2",
    )(zr.reshape(bsz, n_fast, cols), zi.reshape(bsz, n_fast, cols), fc, fs)
    return y.reshape(bsz, n, F_WIDTH)


def _second_largest_sum(a, b, c, d):
    mab, nab = jnp.maximum(a, b), jnp.minimum(a, b)
    mcd, ncd = jnp.maximum(c, d), jnp.minimum(c, d)
    return jnp.maximum(mab, mcd) + jnp.maximum(jnp.minimum(mab, mcd), jnp.maximum(nab, ncd))


def _selected_group(sb_rows):
    epg = EXPERTS_PER_GROUP
    g_score = [_second_largest_sum(*sb_rows[g * epg:(g + 1) * epg]) for g in range(N_GROUPS)]
    best = functools.reduce(jnp.maximum, g_score)
    group = jnp.full_like(best, float(N_GROUPS - 1))
    for g in range(N_GROUPS - 2, -1, -1):
        group = jnp.where(g_score[g] == best, float(g), group)
    return group


def _top2_gates(cand_s, cand_sb):
    n = len(cand_s)
    w = []
    for j in range(n):
        rank = jnp.zeros_like(cand_sb[j])
        for i in range(n):
            if i == j:
                continue
            ahead = (cand_sb[i] > cand_sb[j]) | ((cand_sb[i] == cand_sb[j]) & (i < j))
            rank = rank + jnp.where(ahead, 1.0, 0.0)
        w.append(jnp.where(rank < 2.0, cand_s[j], 0.0))
    total = functools.reduce(jnp.add, w)
    return [wj / total for wj in w]


def _out_kernel(a_ref, y_ref, x_ref, m_ref, wo_ref, bd_ref, g_ref, b_ref, wr_ref, rb_ref,
                x1_ref, h2_ref, grp_ref):
    y2 = _dot(y_ref[0], bd_ref[...]).astype(BF16)
    o = _dot(a_ref[0], wo_ref[:NA_WIDTH, :]) + _dot(y2, wo_ref[NA_WIDTH:, :])
    z = DEEPNORM_ALPHA * x_ref[0] + m_ref[0, 2:3, :] * o
    x1 = _layer_norm(z) * g_ref[...] + b_ref[...]
    x1_ref[0] = x1
    h2 = _layer_norm(x1) * (1.0 + m_ref[0, 4:5, :]) + m_ref[0, 3:4, :]
    _to_slabs(h2_ref, h2)
    sb =jax.nn.sigmoid(_dot_nt(wr_ref[...], h2.astype(BF16))) + rb_ref[...]
    group = _selected_group([sb[e:e + 1] for e in range(N_EXPERTS)])
    grp_ref[0] = jnp.broadcast_to(group, grp_ref.shape[1:])


def _out_projection(attn, yf, x, m, w_out_bf16, w_four_bd, ln_g, ln_b, w_router_t, router_bias, tm):
    bsz, length, _ = x.shape
    tiles_per_batch = length // tm
    row = lambda b, i: (b, i, 0)
    const2 = lambda b, i: (0, 0)
    return pl.pallas_call(
        _out_kernel,
        grid=(bsz, length // tm),
        in_specs=[pl.BlockSpec((1, tm, NA_WIDTH), row), pl.BlockSpec((1, tm, F_WIDTH), row),
                  pl.BlockSpec((1, tm, D_MODEL), row),
                  pl.BlockSpec((1, N_MOD, D_MODEL), lambda b, i: (b, 0, 0)),
                  pl.BlockSpec((NA_WIDTH + F_WIDTH, D_MODEL), const2),
                  pl.BlockSpec((F_WIDTH, F_WIDTH), const2),
                  pl.BlockSpec((1, D_MODEL), const2), pl.BlockSpec((1, D_MODEL), const2),
                  pl.BlockSpec((N_EXPERTS, D_MODEL), const2), pl.BlockSpec((N_EXPERTS, 1), const2)],
        out_specs=[pl.BlockSpec((1, tm, D_MODEL), row),
                   pl.BlockSpec((tm * SLAB_ROWS, LANES), lambda b, i: (b * tiles_per_batch + i, 0)),
                   pl.BlockSpec((1, SUBLANES, tm), lambda b, i: (b, 0, i))],
        out_shape=[jax.ShapeDtypeStruct((bsz, length, D_MODEL), F32),
                   jax.ShapeDtypeStruct((bsz * length * SLAB_ROWS, LANES), F32),
                   jax.ShapeDtypeStruct((bsz, SUBLANES, length), F32)],
        compiler_params=_cparams(2),
        name="out_proj_norm_route",
    )(attn, yf, x, m, w_out_bf16, w_four_bd, ln_g.reshape(1, D_MODEL), ln_b.reshape(1, D_MODEL),
      w_router_t, router_bias.reshape(N_EXPERTS, 1))


def _group_plan(group, tm):
    n = group.shape[0]
    n_slots = n + N_GROUPS * tm
    onehot = (group[:, None] == jnp.arange(N_GROUPS, dtype=jnp.int32)[None, :]).astype(jnp.int32)
    csum = jnp.cumsum(onehot, axis=0)
    rank = jnp.sum(onehot * csum, axis=1) - 1
    count = csum[-1]
    padded = ((count + tm - 1) // tm) * tm
    end = jnp.cumsum(padded)
    start = end - padded
    slot = jnp.sum(onehot * start[None, :], axis=1) + rank
    tile_start = jnp.arange(n_slots // tm, dtype=jnp.int32) * tm
    tile_group = jnp.minimum(jnp.sum((end[None, :] <= tile_start[:, None]).astype(jnp.int32), axis=1),
                             N_GROUPS - 1)
    tile_rows = jnp.clip((start + count)[tile_group] - tile_start, 0, tm)
    return slot, tile_group, tile_rows, start, start + count, end


ROW_DMA_UNROLL = 8


def _slab(ref, token):
    return ref.at[pl.ds(pl.multiple_of(token * SLAB_ROWS, SLAB_ROWS), SLAB_ROWS), :]


def _row_scatter(idx_ref, base, src_ref, dst_hbm, sem, n_rows, wait, inverse_ref=None):
    def one(r, priority):
        if wait:
            pltpu.make_async_copy(_slab(src_ref, 0), _slab(dst_hbm, 0), sem).wait()
        else:
            idx = idx_ref[base + r]
            if inverse_ref is not None:
                inverse_ref[idx] = base + r
            pltpu.make_async_copy(_slab(src_ref, r), _slab(dst_hbm, idx), sem).start(priority=priority)

    def chunk(j, carry):
        for u in range(ROW_DMA_UNROLL):
            one(j * ROW_DMA_UNROLL + u, u % 2)
        return carry

    def single(r, carry):
        one(r, 0)
        return carry

    n_chunks = n_rows // ROW_DMA_UNROLL
    lax.fori_loop(0, n_chunks, chunk, 0)
    lax.fori_loop(n_chunks * ROW_DMA_UNROLL, n_rows, single, 0)


def _to_slots_kernel(slot_ref, start_ref, rows_end_ref, end_ref, h_ref, o_hbm, dst_ref, zbuf, sem,
                     *, tm, tm_slots):
    step = pl.program_id(0)
    n_slots = dst_ref.shape[0]

    @pl.when(step == 0)
    def _():
        zbuf[...] = jnp.zeros_like(zbuf)

        def fill(slot0):
            rows = tm_slots * SLAB_ROWS
            return pltpu.make_async_copy(zbuf, o_hbm.at[pl.ds(pl.multiple_of(slot0 * SLAB_ROWS, rows), rows), :], sem)

        for g in range(N_GROUPS):
            @pl.when(end_ref[g] > start_ref[g])
            def _():
                fill(end_ref[g] - tm_slots).start()
        for g in range(N_GROUPS):
            @pl.when(end_ref[g] > start_ref[g])
            def _():
                fill(end_ref[g] - tm_slots).wait()

        def fill_unused(t, carry):
            fill(t * tm_slots).start()
            fill(t * tm_slots).wait()
            return carry
        lax.fori_loop(end_ref[N_GROUPS - 1] // tm_slots, n_slots // tm_slots, fill_unused, 0)

        def clear(p, carry):
            dst_ref[p] = 0
            return carry
        for g in range(N_GROUPS):
            lax.fori_loop(rows_end_ref[g], end_ref[g], clear, 0)
        lax.fori_loop(end_ref[N_GROUPS - 1], n_slots, clear, 0)

    _row_scatter(slot_ref, step * tm, h_ref, o_hbm, sem, tm, wait=False, inverse_ref=dst_ref)
    _row_scatter(slot_ref, step * tm, h_ref, o_hbm, sem, tm, wait=True)


def _to_slots(h2_slabs, slot, start, rows_end, end, tm, tm_slots):
    n = h2_slabs.shape[0] // SLAB_ROWS
    n_slots = n + N_GROUPS * tm_slots
    grid_spec = pltpu.PrefetchScalarGridSpec(
        num_scalar_prefetch=4,
        grid=(n // tm,),
        in_specs=[pl.BlockSpec((tm * SLAB_ROWS, LANES), lambda i, *_: (i, 0))],
        out_specs=[pl.BlockSpec(memory_space=pl.ANY), pl.BlockSpec(memory_space=pltpu.SMEM)],
        scratch_shapes=[pltpu.VMEM((tm_slots * SLAB_ROWS, LANES), F32), pltpu.SemaphoreType.DMA(())],
    )
    return pl.pallas_call(
        functools.partial(_to_slots_kernel, tm=tm, tm_slots=tm_slots),
        grid_spec=grid_spec,
        out_shape=[jax.ShapeDtypeStruct((n_slots * SLAB_ROWS, LANES), F32),
                   jax.ShapeDtypeStruct((n_slots,), jnp.int32)],
        compiler_params=_cparams(1),
        name="moe_rows_to_slots",
    )(slot, start, rows_end, end, h2_slabs)


def _moe_group_kernel(dst_ref, tile_group_ref, tile_rows_ref, n_used_ref, h_ref, wg_ref, wu_ref, wd_ref, wr_ref,
                      rb_ref, y_hbm, ybuf, sems, *, tm, n_tiles):
    del tile_group_ref
    step = pl.program_id(0)
    n_used = n_used_ref[0]
    buf = step % 2

    def scatter(tile, wait):
        _row_scatter(dst_ref, tile * tm, ybuf.at[tile % 2], y_hbm, sems.at[tile % 2], tile_rows_ref[tile], wait)

    @pl.when((step >= 2) & (step < n_used))
    def _():
        scatter(step - 2, wait=True)

    @pl.when(step < n_used)
    def _():
        h = _from_slabs(h_ref, tm).astype(BF16)
        s = jax.nn.sigmoid(_dot(h, wr_ref[0]))
        sb = s + rb_ref[0]
        epg = EXPERTS_PER_GROUP
        gates = _top2_gates([s[:, j:j + 1] for j in range(epg)], [sb[:, j:j + 1] for j in range(epg)])
        acc = None
        for e in range(epg):
            gate = _dot(h, wg_ref[e])
            up = _dot(h, wu_ref[e])
            hid = (gate * jax.nn.sigmoid(gate)) * up * gates[e]
            y = _dot(hid.astype(BF16), wd_ref[e])
            acc = y if acc is None else acc + y
        _to_slabs(ybuf.at[buf], acc)
        scatter(step, wait=False)

    @pl.when(step == n_tiles - 1)
    def _():
        @pl.when(n_used >= 2)
        def _():
            scatter(n_used - 2, wait=True)

        @pl.when(n_used >= 1)
        def _():
            scatter(n_used - 1, wait=True)


def _moe_experts(h_slots, dst, tile_group, tile_rows, n_used, n_tokens, w_gate, w_up, w_down, w_router_grp,
                 router_bias_grp, tm):
    n_slots = h_slots.shape[0] // SLAB_ROWS
    n_tiles = n_slots // tm
    epg = EXPERTS_PER_GROUP
    by_group = lambda i, dst_ref, tg_ref, tr_ref, nu_ref: (tg_ref[i], 0, 0)
    grid_spec = pltpu.PrefetchScalarGridSpec(
        num_scalar_prefetch=4,
        grid=(n_tiles,),
        in_specs=[pl.BlockSpec((tm * SLAB_ROWS, LANES),
                               lambda i, dst_ref, tg_ref, tr_ref, nu_ref: (jnp.minimum(i, nu_ref[0] - 1), 0)),
                  pl.BlockSpec((epg, D_MODEL, D_EXPERT), by_group),
                  pl.BlockSpec((epg, D_MODEL, D_EXPERT), by_group),
                  pl.BlockSpec((epg, D_EXPERT, D_MODEL), by_group),
                  pl.BlockSpec((1, D_MODEL, LANES), by_group),
                  pl.BlockSpec((1, 1, LANES), by_group)],
        out_specs=pl.BlockSpec(memory_space=pl.ANY),
        scratch_shapes=[pltpu.VMEM((2, tm * SLAB_ROWS, LANES), F32), pltpu.SemaphoreType.DMA((2,))],
    )
    return pl.pallas_call(
        functools.partial(_moe_group_kernel, tm=tm, n_tiles=n_tiles),
        grid_spec=grid_spec,
        out_shape=jax.ShapeDtypeStruct((n_tokens * SLAB_ROWS, LANES), F32),
        compiler_params=_cparams(1),
        name="moe_group_experts",
    )(dst, tile_group, tile_rows, n_used, h_slots, w_gate, w_up, w_down, w_router_grp, router_bias_grp)


def _residual_norm_kernel(y_ref, x1_ref, m_ref, g_ref, b_ref, o_ref):
    y = _from_slabs(y_ref, x1_ref.shape[0])
    z = DEEPNORM_ALPHA * x1_ref[...] + m_ref[0, 5:6, :] * y
    o_ref[...] = _layer_norm(z) * g_ref[...] + b_ref[...]


def _residual_norm(y_slabs, x1, m, ln_g, ln_b, tm):
    bsz, length, _ = x1.shape
    n = bsz * length
    tiles_per_batch = length // tm
    const2 = lambda i: (0, 0)
    row = pl.BlockSpec((tm, D_MODEL), lambda i: (i, 0))
    out = pl.pallas_call(
        _residual_norm_kernel,
        grid=(n // tm,),
        in_specs=[pl.BlockSpec((tm * SLAB_ROWS, LANES), lambda i: (i, 0)), row,
                  pl.BlockSpec((1, N_MOD, D_MODEL), lambda i: (i // tiles_per_batch, 0, 0)),
                  pl.BlockSpec((1, D_MODEL), const2), pl.BlockSpec((1, D_MODEL), const2)],
        out_specs=row,
        out_shape=jax.ShapeDtypeStruct((n, D_MODEL), F32),
        compiler_params=_cparams(1),
        name="moe_residual_norm",
    )(y_slabs, x1.reshape(n, D_MODEL), m, ln_g.reshape(1, D_MODEL), ln_b.reshape(1, D_MODEL))
    return out.reshape(bsz, length, D_MODEL)


def _grouped_moe(h2, group_rows, x1, m, w_gate, w_up, w_down, w_router_grp, router_bias_grp, ln_g, ln_b,
                 tm_experts, tm_tokens):
    bsz, length, _ = x1.shape
    n = bsz * length
    group = group_rows[:, 0, :].reshape(n).astype(jnp.int32)
    slot, tile_group, tile_rows, start, rows_end, end = _group_plan(group, tm_experts)
    h_slots, dst = _to_slots(h2, slot, start, rows_end, end, tm_tokens, tm_experts)
    n_used = (end[N_GROUPS - 1:] // tm_experts).astype(jnp.int32)
    y = _moe_experts(h_slots, dst, tile_group, tile_rows, n_used, n, w_gate, w_up, w_down, w_router_grp,
                     router_bias_grp, tm_experts)
    return _residual_norm(y, x1, m, ln_g, ln_b, tm_tokens)


def _ctx_fourier_kernel(f_ref, w1_ref, c_ref, s_ref, y_ref):
    ab = _dot(f_ref[0], w1_ref[...]).astype(BF16)
    y = _dot(c_ref[...], ab[:, :F_WIDTH]) + _dot(s_ref[...], ab[:, F_WIDTH:])
    y_ref[0] = y.astype(BF16)


def _context_fourier(fc):
    bsz, n, _ = fc.shape
    w1 = _bf16_table(_channel_dft_matrix())
    c, s = _dft_cos_sin(n)
    cm = _bf16_table(c * n ** -0.5)
    sm = _bf16_table(-s * n ** -0.5)
    const2 = lambda b: (0, 0)
    blk = pl.BlockSpec((1, n, F_WIDTH), lambda b: (b, 0, 0))
    return pl.pallas_call(
        _ctx_fourier_kernel,
        grid=(bsz,),
        in_specs=[blk, pl.BlockSpec(w1.shape, const2), pl.BlockSpec(cm.shape, const2),
                  pl.BlockSpec(sm.shape, const2)],
        out_specs=blk,
        out_shape=jax.ShapeDtypeStruct((bsz, n, F_WIDTH), BF16),
        compiler_params=_cparams(1),
        name="context_fnet",
    )(fc, w1, cm, sm)


def _block_diag(w):
    g, c, _ = w.shape
    eye = jnp.eye(g, dtype=w.dtype)
    return (eye[:, None, :, None] * w[:, :, None, :]).reshape(g * c, g * c)


def kernel(x, c, ctx, c_ctx, w_mod, b_mod, w_in, rpb, w_four, w_out, ln1_g, ln1_b, ln2_g, ln2_b,
           w_router, router_bias, w_gate, w_up, w_down):
    bsz, length, _ = x.shape
    n_ctx = ctx.shape[1]
    rows = length // GRID_W

    cvec = jnp.concatenate([c, c_ctx[None, :], jnp.zeros((8 - bsz - 1, D_MODEL), F32)], axis=0)
    mods = _modulation(cvec, w_mod, b_mod)
    w_router_t = w_router.T.astype(BF16)
    toeplitz = _rpb_toeplitz(rpb)
    lane_pad = LANES - EXPERTS_PER_GROUP
    w_router_grp = jnp.pad(w_router.reshape(D_MODEL, N_GROUPS, EXPERTS_PER_GROUP).transpose(1, 0, 2),
                           ((0, 0), (0, 0), (0, lane_pad))).astype(BF16)
    router_bias_grp = jnp.pad(router_bias.astype(F32).reshape(N_GROUPS, 1, EXPERTS_PER_GROUP),
                              ((0, 0), (0, 0), (0, lane_pad)))

    xc = ctx
    for i in range(DEPTH):
        last = i == DEPTH - 1
        m = mods[i, :bsz].reshape(bsz, N_MOD, D_MODEL)
        mc = jnp.broadcast_to(mods[i, bsz].reshape(1, N_MOD, D_MODEL), (bsz, N_MOD, D_MODEL))
        w_in_b = w_in[i].astype(BF16)
        w_out_b = w_out[i].astype(BF16)
        w_four_bd = _block_diag(w_four[i]).astype(BF16)
        wg, wu, wd = w_gate[i].astype(BF16), w_up[i].astype(BF16), w_down[i].astype(BF16)

        q, k, v, f = _in_projection(x, m, w_in_b, tm=512)
        qc, kc, vc, fc = _in_projection(xc, mc, w_in_b, tm=n_ctx)

        attn = _neighborhood_attention(q, k, v, kc, vc, toeplitz[i])
        yf = _fourier_positions(f, n_slow=rows, n_fast=GRID_W)
        x1, h2, grp = _out_projection(attn, yf, x, m, w_out_b, w_four_bd, ln1_g[i], ln1_b[i],
                                      w_router_t, router_bias, tm=512)
        x = _grouped_moe(h2, grp, x1, m, wg, wu, wd, w_router_grp, router_bias_grp, ln2_g[i], ln2_b[i],
                         tm_experts=512, tm_tokens=512)
        if last:
            break

        attn_c = _context_attention(qc, kc, vc)
        yc = _context_fourier(fc)
        xc1, h2c, grp_c = _out_projection(attn_c, yc, xc, mc, w_out_b, w_four_bd, ln1_g[i], ln1_b[i],
                                          w_router_t, router_bias, tm=n_ctx)
        xc = _grouped_moe(h2c, grp_c, xc1, mc, wg, wu, wd, w_router_grp, router_bias_grp, ln2_g[i], ln2_b[i],
                          tm_experts=128, tm_tokens=n_ctx)
    return x
```

```python
import functools
import math

import numpy as np
import jax
import jax.numpy as jnp
from jax import lax
from jax.experimental import pallas as pl
from jax.experimental.pallas import tpu as pltpu

D_MODEL = 1024
DEPTH = 2
GRID_W = 64
NA_HEADS = 8
HEAD_DIM = 64
NA_WIDTH = NA_HEADS * HEAD_DIM
WIN_ROWS = 8
WIN_COLS = 16
F_GROUPS = 8
F_GROUP_DIM = 64
F_WIDTH = F_GROUPS * F_GROUP_DIM
IN_WIDTH = 3 * NA_WIDTH + F_WIDTH
N_EXPERTS = 16
N_GROUPS = 4
EXPERTS_PER_GROUP = N_EXPERTS // N_GROUPS
D_EXPERT = 256
N_MOD = 6
DEEPNORM_ALPHA = (2.0 * DEPTH) ** 0.25
LN_EPS = 1e-6

F32 = jnp.float32
BF16 = jnp.bfloat16

V7X_VMEM_BYTES = 64 * 1024 * 1024
VMEM_LIMIT_BYTES = (V7X_VMEM_BYTES * 3) // 4
LANES = 128
SUBLANES = 8
HEADS_PER_STEP = LANES // HEAD_DIM
MASK_VALUE = -1e30

Q_ROWS = 8
K_ROWS = 16
K_CHUNK_ROWS = 4
N_K_CHUNKS = K_ROWS // K_CHUNK_ROWS


def _cparams(n_grid_dims):
    return pltpu.CompilerParams(dimension_semantics=("arbitrary",) * n_grid_dims,
                                vmem_limit_bytes=VMEM_LIMIT_BYTES)


def _layer_norm(x):
    mu = jnp.mean(x, axis=-1, keepdims=True)
    xc = x - mu
    var = jnp.mean(xc * xc, axis=-1, keepdims=True)
    return xc * lax.rsqrt(var + LN_EPS)


SLAB_ROWS = D_MODEL // LANES
assert SLAB_ROWS == SUBLANES


def _to_slabs(ref, x):
    for j in range(SLAB_ROWS):
        ref[pl.ds(j, x.shape[0], stride=SLAB_ROWS), :] = x[:, j * LANES:(j + 1) * LANES]


def _from_slabs(ref, n_tokens):
    return jnp.concatenate([ref[pl.ds(j, n_tokens, stride=SLAB_ROWS), :] for j in range(SLAB_ROWS)], axis=-1)


def _dot(a, b):
    return jnp.dot(a, b, preferred_element_type=F32)


def _dot_nt(a, b):
    return lax.dot_general(a, b, (((1,), (1,)), ((), ())), preferred_element_type=F32)


def _mod_kernel(c_ref, w_ref, b_ref, o_ref):
    c = c_ref[...]
    a = c * jax.nn.sigmoid(c)
    o_ref[0] = jnp.dot(a, w_ref[0], preferred_element_type=F32, precision=lax.Precision.HIGHEST) + b_ref[0]


def _modulation(cvec, w_mod, b_mod):
    n_col_blocks = 4
    wc = (N_MOD * D_MODEL) // n_col_blocks
    rows = cvec.shape[0]
    return pl.pallas_call(
        _mod_kernel,
        grid=(DEPTH, n_col_blocks),
        in_specs=[pl.BlockSpec((rows, D_MODEL), lambda i, j: (0, 0)),
                  pl.BlockSpec((1, D_MODEL, wc), lambda i, j: (i, 0, j)),
                  pl.BlockSpec((1, 1, wc), lambda i, j: (i, 0, j))],
        out_specs=pl.BlockSpec((1, rows, wc), lambda i, j: (i, 0, j)),
        out_shape=jax.ShapeDtypeStruct((DEPTH, rows, N_MOD * D_MODEL), F32),
        compiler_params=_cparams(2),
        name="modulation",
    )(cvec, w_mod, b_mod.reshape(DEPTH, 1, N_MOD * D_MODEL))


def _proj_kernel(x_ref, m_ref, w_ref, q_ref, k_ref, v_ref, f_ref):
    h = _layer_norm(x_ref[0]) * (1.0 + m_ref[0, 1:2, :]) + m_ref[0, 0:1, :]
    p = _dot(h.astype(BF16), w_ref[...])
    q_ref[0] = (p[:, :NA_WIDTH] * (HEAD_DIM ** -0.5)).astype(BF16)
    k_ref[0] = p[:, NA_WIDTH:2 * NA_WIDTH].astype(BF16)
    v_ref[0] = p[:, 2 * NA_WIDTH:3 * NA_WIDTH].astype(BF16)
    f_ref[0] = p[:, 3 * NA_WIDTH:].astype(BF16)


def _in_projection(x, m, w_in_bf16, tm):
    bsz, length, _ = x.shape
    out = jax.ShapeDtypeStruct((bsz, length, NA_WIDTH), BF16)
    o_spec = pl.BlockSpec((1, tm, NA_WIDTH), lambda b, i: (b, i, 0))
    return pl.pallas_call(
        _proj_kernel,
        grid=(bsz, length // tm),
        in_specs=[pl.BlockSpec((1, tm, D_MODEL), lambda b, i: (b, i, 0)),
                  pl.BlockSpec((1, N_MOD, D_MODEL), lambda b, i: (b, 0, 0)),
                  pl.BlockSpec((D_MODEL, IN_WIDTH), lambda b, i: (0, 0))],
        out_specs=[o_spec, o_spec, o_spec, o_spec],
        out_shape=[out, out, out, out],
        compiler_params=_cparams(2),
        name="ln_mod_in_proj",
    )(x, m, w_in_bf16)


def _head_lanes(h):
    lane = lax.broadcasted_iota(jnp.int32, (1, LANES), 1)
    return (lane >= HEAD_DIM * h) & (lane < HEAD_DIM * (h + 1))


def _scores_pass(q, h, tiles, s_ref):
    qh = jnp.where(_head_lanes(h), q, jnp.zeros_like(q))
    m = None
    t = tiles[0][0].shape[0]
    for j, (k, _, bias) in enumerate(tiles):
        s = _dot_nt(qh, k)
        if bias is not None:
            s = s + bias
        s_ref[:, j * t:(j + 1) * t] = s
        mj = jnp.max(s, axis=-1, keepdims=True)
        m = mj if m is None else jnp.maximum(m, mj)
    return m


def _pv_pass(tiles, s_ref, m):
    denom, o = None, None
    t = tiles[0][0].shape[0]
    for j, (_, v, _) in enumerate(tiles):
        p = jnp.exp(s_ref[:, j * t:(j + 1) * t] - m)
        dj = jnp.sum(p, axis=-1, keepdims=True)
        oj = _dot(p.astype(BF16), v)
        denom = dj if denom is None else denom + dj
        o = oj if o is None else o + oj
    return o / denom


def _attention_units(units):
    outs = []
    maxima = [_scores_pass(*units[0])]
    for u in range(len(units)):
        if u + 1 < len(units):
            maxima.append(_scores_pass(*units[u + 1]))
        _, _, tiles, s_ref = units[u]
        outs.append(_pv_pass(tiles, s_ref, maxima[u]))
    return outs


def _merge_heads(outs):
    merged = outs[0]
    for h in range(1, len(outs)):
        merged = jnp.where(_head_lanes(h), outs[h], merged)
    return merged


DR_PAD = 2 * WIN_ROWS
DC_PAD = 2 * WIN_COLS


def _toeplitz_kernel(r_ref, sel_ref, mask_ref, o_ref):
    n = r_ref.shape[0]
    for qc in range(GRID_W):
        block = jnp.dot(r_ref[...], sel_ref[qc], preferred_element_type=F32, precision=lax.Precision.HIGHEST)
        o_ref[pl.ds(qc, n, stride=GRID_W), :] = block + mask_ref[qc]


def _rpb_toeplitz(rpb):
    depth, heads, n_dr, n_dc = rpb.shape
    qc = np.arange(GRID_W)[:, None]
    kc = (np.arange(LANES) % GRID_W)[None, :]
    cs = np.clip(qc - WIN_COLS // 2, 0, GRID_W - WIN_COLS)
    col_valid = (kc >= cs) & (kc < cs + WIN_COLS)
    dc = kc - qc + WIN_COLS - 1
    select = (np.arange(DC_PAD)[None, :, None] == dc[:, None, :]) & col_valid[:, None, :]
    mask = np.where(col_valid, 0.0, MASK_VALUE)[:, None, :]
    r = jnp.pad(rpb.astype(F32), ((0, 0), (0, 0), (0, DR_PAD - n_dr), (0, DC_PAD - n_dc)))
    n = depth * heads * DR_PAD
    const3 = lambda: (0, 0, 0)
    out = pl.pallas_call(
        _toeplitz_kernel,
        grid=(),
        in_specs=[pl.BlockSpec((n, DC_PAD), lambda: (0, 0)),
                  pl.BlockSpec((GRID_W, DC_PAD, LANES), const3), pl.BlockSpec((GRID_W, 1, LANES), const3)],
        out_specs=pl.BlockSpec((n * GRID_W, LANES), lambda: (0, 0)),
        out_shape=jax.ShapeDtypeStruct((n * GRID_W, LANES), F32),
        compiler_params=pltpu.CompilerParams(vmem_limit_bytes=VMEM_LIMIT_BYTES),
        name="rpb_toeplitz",
    )(r.reshape(n, DC_PAD), jnp.asarray(select, F32), jnp.asarray(mask, F32))
    return out.reshape(depth, heads, DR_PAD, GRID_W, LANES)


Q_CHUNK_ROWS = 4
WINDOW_CHUNKS = 3
N_Q_CHUNKS = Q_ROWS // Q_CHUNK_ROWS
assert Q_CHUNK_ROWS == K_CHUNK_ROWS and WINDOW_CHUNKS * K_CHUNK_ROWS >= Q_CHUNK_ROWS + WIN_ROWS - 1
assert N_Q_CHUNKS == 2 and N_K_CHUNKS == 4


def _key_row_start(rb, rows):
    return int(np.clip(rb * Q_ROWS - (K_ROWS - Q_ROWS) // 2, 0, rows - K_ROWS))


def _window_uses_last_chunk(rb, c, n_rb):
    if c == 0:
        return rb == n_rb - 1
    return rb != 0


def _window_key_rows(rb, c, rows):
    var = N_K_CHUNKS - 1 if _window_uses_last_chunk(rb, c, rows // Q_ROWS) else 0
    k0 = _key_row_start(rb, rows)
    return [k0 + K_CHUNK_ROWS * j + i for j in (var, 1, 2) for i in range(K_CHUNK_ROWS)]


def _row_window(qr, rows):
    kh = min(WIN_ROWS, rows)
    rs = int(np.clip(qr - kh // 2, 0, rows - kh))
    return rs, rs + kh


def _check_windows(rows):
    def relative(rb):
        base = rb * Q_ROWS
        return [([kr - base for kr in _window_key_rows(rb, c, rows)],
                 [tuple(r - base for r in _row_window(base + c * Q_CHUNK_ROWS + qi, rows))
                  for qi in range(Q_CHUNK_ROWS)]) for c in range(N_Q_CHUNKS)]

    n_rb = rows // Q_ROWS
    for rb in range(n_rb):
        assert rb in (0, n_rb - 1) or relative(rb) == relative(1), rb
        for c in range(N_Q_CHUNKS):
            have = set(_window_key_rows(rb, c, rows))
            for qi in range(Q_CHUNK_ROWS):
                lo, hi = _row_window(rb * Q_ROWS + c * Q_CHUNK_ROWS + qi, rows)
                assert set(range(lo, hi)) <= have, (rb, c, qi)


def _build_bias_tables(t_ref, bias_scr, rows):
    n_rb = rows // Q_ROWS
    left = lax.broadcasted_iota(jnp.int32, (GRID_W, LANES), 1) < GRID_W
    masked = jnp.full((GRID_W, LANES), MASK_VALUE, F32)
    for variant, rb in enumerate((0, 1, n_rb - 1)):
        for c in range(N_Q_CHUNKS):
            key_rows = _window_key_rows(rb, c, rows)
            for h in range(HEADS_PER_STEP):
                for qi in range(Q_CHUNK_ROWS):
                    qr = rb * Q_ROWS + c * Q_CHUNK_ROWS + qi
                    lo, hi = _row_window(qr, rows)
                    for p in range(len(key_rows) // 2):
                        pair = [t_ref[h, kr - qr + WIN_ROWS - 1] if lo <= kr < hi else None
                                for kr in key_rows[2 * p:2 * p + 2]]
                        if pair[0] is None and pair[1] is None:
                            block = masked
                        else:
                            block = jnp.where(left, masked if pair[0] is None else pair[0],
                                              masked if pair[1] is None else pair[1])
                        bias_scr[variant, c, h, qi * GRID_W:(qi + 1) * GRID_W, p * LANES:(p + 1) * LANES] = block


def _na_kernel(q_ref, k0, k1, k2, k3, v0, v1, v2, v3, kc_ref, vc_ref, t_ref, o_ref, bias_scr, s_scr, *, rows):
    n_rb = rows // Q_ROWS
    b, rb = pl.program_id(1), pl.program_id(2)

    @pl.when((b == 0) & (rb == 0))
    def _():
        _build_bias_tables(t_ref, bias_scr, rows)

    variant = jnp.where(rb == 0, 0, jnp.where(rb == n_rb - 1, 2, 1))
    tq = Q_CHUNK_ROWS * GRID_W
    tk = K_CHUNK_ROWS * GRID_W
    units = []
    for c in range(N_Q_CHUNKS):
        use_last = (rb == n_rb - 1) if c == 0 else (rb != 0)
        window = [(jnp.where(use_last, k3[0], k0[0]), jnp.where(use_last, v3[0], v0[0])),
                  (k1[0], v1[0]), (k2[0], v2[0])]
        q = q_ref[0, c * tq:(c + 1) * tq, :]
        for h in range(HEADS_PER_STEP):
            lat = [(k, v, bias_scr[variant, c, h, :, j * tk:(j + 1) * tk]) for j, (k, v) in enumerate(window)]
            units.append((q, h, lat + [(kc_ref[0], vc_ref[0], None)], s_scr.at[c, h]))
    outs = _attention_units(units)
    for c in range(N_Q_CHUNKS):
        o = _merge_heads(outs[c * HEADS_PER_STEP:(c + 1) * HEADS_PER_STEP])
        o_ref[0, c * tq:(c + 1) * tq, :] = o.astype(BF16)


def _neighborhood_attention(q, k, v, kc, vc, toeplitz):
    bsz, length, _ = q.shape
    rows = length // GRID_W
    n_rb = rows // Q_ROWS
    assert rows % Q_ROWS == 0 and rows >= K_ROWS + Q_ROWS and n_rb >= 3
    _check_windows(rows)
    n_ctx = kc.shape[1]
    tq = Q_ROWS * GRID_W
    tk = K_CHUNK_ROWS * GRID_W
    assert n_ctx == tk
    max_chunk = (rows - K_ROWS) // K_CHUNK_ROWS
    half = (K_ROWS - Q_ROWS) // 2 // K_CHUNK_ROWS

    def kv_spec(j):
        def index(hp, b, rb):
            start = jnp.clip(rb * (Q_ROWS // K_CHUNK_ROWS) - half, 0, max_chunk)
            return (b, start + j, hp)
        return pl.BlockSpec((1, tk, LANES), index)

    ctx_spec = pl.BlockSpec((1, n_ctx, LANES), lambda hp, b, rb: (b, 0, hp))
    q_spec = pl.BlockSpec((1, tq, LANES), lambda hp, b, rb: (b, rb, hp))
    window = WINDOW_CHUNKS * K_CHUNK_ROWS * GRID_W
    return pl.pallas_call(
        functools.partial(_na_kernel, rows=rows),
        grid=(NA_HEADS // HEADS_PER_STEP, bsz, n_rb),
        in_specs=([q_spec] + [kv_spec(j) for j in range(N_K_CHUNKS)] + [kv_spec(j) for j in range(N_K_CHUNKS)]
                  + [ctx_spec, ctx_spec,
                     pl.BlockSpec((HEADS_PER_STEP, DR_PAD, GRID_W, LANES), lambda hp, b, rb: (hp, 0, 0, 0))]),
        out_specs=q_spec,
        out_shape=jax.ShapeDtypeStruct((bsz, length, NA_WIDTH), BF16),
        scratch_shapes=[pltpu.VMEM((3, N_Q_CHUNKS, HEADS_PER_STEP, Q_CHUNK_ROWS * GRID_W, window), F32),
                        pltpu.VMEM((N_Q_CHUNKS, HEADS_PER_STEP, Q_CHUNK_ROWS * GRID_W, window + n_ctx), F32)],
        compiler_params=_cparams(3),
        name="neighborhood_attention",
    )(q, *([k] * N_K_CHUNKS), *([v] * N_K_CHUNKS), kc, vc, toeplitz)


def _ctx_attn_kernel(q_ref, k_ref, v_ref, o_ref, s_scr):
    tiles = [(k_ref[0], v_ref[0], None)]
    outs = _attention_units([(q_ref[0], h, tiles, s_scr.at[h]) for h in range(HEADS_PER_STEP)])
    o_ref[0] = _merge_heads(outs).astype(BF16)


def _context_attention(qc, kc, vc):
    bsz, n_ctx, _ = qc.shape
    spec = pl.BlockSpec((1, n_ctx, LANES), lambda b, hp: (b, 0, hp))
    return pl.pallas_call(
        _ctx_attn_kernel,
        grid=(bsz, NA_HEADS // HEADS_PER_STEP),
        in_specs=[spec, spec, spec],
        out_specs=spec,
        out_shape=jax.ShapeDtypeStruct((bsz, n_ctx, NA_WIDTH), BF16),
        scratch_shapes=[pltpu.VMEM((HEADS_PER_STEP, n_ctx, n_ctx), F32)],
        compiler_params=_cparams(2),
        name="context_attention",
    )(qc, kc, vc)


def _dft_cos_sin(n):
    ang = 2.0 * np.pi * np.outer(np.arange(n), np.arange(n)) / n
    return np.cos(ang), np.sin(ang)


def _bf16_table(a):
    return jnp.asarray(a, F32).astype(BF16)


def _channel_dft_matrix():
    c, s = _dft_cos_sin(F_GROUP_DIM)
    scale = F_GROUP_DIM ** -0.5
    eye = np.eye(F_GROUPS)
    return np.concatenate([np.kron(eye, c), np.kron(eye, s)], axis=1) * scale


def _fft_stage1_kernel(f_ref, w1_ref, cs_ref, sc_ref, tc_ref, ts_ref, zr_ref, zi_ref, *, n_slow, nt):
    x = jnp.concatenate([f_ref[0, :, t * F_WIDTH:(t + 1) * F_WIDTH] for t in range(nt)], axis=0)
    ab = _dot(x, w1_ref[...]).astype(BF16)
    for t in range(nt):
        a = ab[t * n_slow:(t + 1) * n_slow, :F_WIDTH]
        b = ab[t * n_slow:(t + 1) * n_slow, F_WIDTH:]
        z = _dot(cs_ref[...], a) + _dot(sc_ref[...], b)
        zr, zi = z[:n_slow], z[n_slow:]
        c, s = tc_ref[t], ts_ref[t]
        zr_ref[0, t] = (zr * c - zi * s).astype(BF16)
        zi_ref[0, t] = (zr * s + zi * c).astype(BF16)


def _fft_stage2_kernel(zr_ref, zi_ref, fc_ref, fs_ref, y_ref):
    y_ref[0] = (_dot(fc_ref[...], zr_ref[0]) + _dot(fs_ref[...], zi_ref[0])).astype(BF16)


def _fourier_positions(f, n_slow, n_fast):
    bsz, n, _ = f.shape
    assert n == n_slow * n_fast
    nt = 8
    w1 = _bf16_table(_channel_dft_matrix())
    c1, s1 = _dft_cos_sin(n_slow)
    sc1 = n_slow ** -0.5
    cs = _bf16_table(np.concatenate([c1, s1], axis=0) * sc1)
    sc = _bf16_table(np.concatenate([-s1, c1], axis=0) * sc1)
    tw = 2.0 * np.pi * np.outer(np.arange(n_fast), np.arange(n_slow)) / n
    tc = jnp.asarray(np.cos(tw)[:, :, None], F32)
    ts = jnp.asarray(np.sin(tw)[:, :, None], F32)
    z_shape = jax.ShapeDtypeStruct((bsz, n_fast, n_slow, F_WIDTH), BF16)
    z_spec = pl.BlockSpec((1, nt, n_slow, F_WIDTH), lambda b, j: (b, j, 0, 0))
    const2 = lambda b, j: (0, 0)
    tw_spec = pl.BlockSpec((nt, n_slow, 1), lambda b, j: (j, 0, 0))
    zr, zi = pl.pallas_call(
        functools.partial(_fft_stage1_kernel, n_slow=n_slow, nt=nt),
        grid=(bsz, n_fast // nt),
        in_specs=[pl.BlockSpec((1, n_slow, nt * F_WIDTH), lambda b, j: (b, 0, j)),
                  pl.BlockSpec(w1.shape, const2), pl.BlockSpec(cs.shape, const2), pl.BlockSpec(sc.shape, const2),
                  tw_spec, tw_spec],
        out_specs=[z_spec, z_spec],
        out_shape=[z_shape, z_shape],
        compiler_params=_cparams(2),
        name="fnet_stage1",
    )(f.reshape(bsz, n_slow, n_fast * F_WIDTH), w1, cs, sc, tc, ts)

    c2, s2 = _dft_cos_sin(n_fast)
    sc2 = n_fast ** -0.5
    fc = _bf16_table(c2 * sc2)
    fs = _bf16_table(-s2 * sc2)
    cols = n_slow * F_WIDTH
    wcol = min(cols, 8192)
    blk = pl.BlockSpec((1, n_fast, wcol), lambda b, j: (b, 0, j))
    y = pl.pallas_call(
        _fft_stage2_kernel,
        grid=(bsz, cols // wcol),
        in_specs=[blk, blk, pl.BlockSpec(fc.shape, const2), pl.BlockSpec(fs.shape, const2)],
        out_specs=blk,
        out_shape=jax.ShapeDtypeStruct((bsz, n_fast, cols), BF16),
        compiler_params=_cparams(2),
        name="fnet_stage2",
    )(zr.reshape(bsz, n_fast, cols), zi.reshape(bsz, n_fast, cols), fc, fs)
    return y.reshape(bsz, n, F_WIDTH)


def _second_largest_sum(a, b, c, d):
    mab, nab = jnp.maximum(a, b), jnp.minimum(a, b)
    mcd, ncd = jnp.maximum(c, d), jnp.minimum(c, d)
    return jnp.maximum(mab, mcd) + jnp.maximum(jnp.minimum(mab, mcd), jnp.maximum(nab, ncd))


def _selected_group(sb_rows):
    epg = EXPERTS_PER_GROUP
    g_score = [_second_largest_sum(*sb_rows[g * epg:(g + 1) * epg]) for g in range(N_GROUPS)]
    best = functools.reduce(jnp.maximum, g_score)
    group = jnp.full_like(best, float(N_GROUPS - 1))
    for g in range(N_GROUPS - 2, -1, -1):
        group = jnp.where(g_score[g] == best, float(g), group)
    return group


def _top2_gates(cand_s, cand_sb):
    n = len(cand_s)
    w = []
    for j in range(n):
        rank = jnp.zeros_like(cand_sb[j])
        for i in range(n):
            if i == j:
                continue
            ahead = (cand_sb[i] > cand_sb[j]) | ((cand_sb[i] == cand_sb[j]) & (i < j))
            rank = rank + jnp.where(ahead, 1.0, 0.0)
        w.append(jnp.where(rank < 2.0, cand_s[j], 0.0))
    total = functools.reduce(jnp.add, w)
    return [wj / total for wj in w]


def _out_kernel(a_ref, y_ref, x_ref, m_ref, wo_ref, bd_ref, g_ref, b_ref, wr_ref, rb_ref,
                x1_ref, h2_ref, grp_ref):
    y2 = _dot(y_ref[0], bd_ref[...]).astype(BF16)
    o = _dot(a_ref[0], wo_ref[:NA_WIDTH, :]) + _dot(y2, wo_ref[NA_WIDTH:, :])
    z = DEEPNORM_ALPHA * x_ref[0] + m_ref[0, 2:3, :] * o
    x1 = _layer_norm(z) * g_ref[...] + b_ref[...]
    x1_ref[0] = x1
    h2 = _layer_norm(x1) * (1.0 + m_ref[0, 4:5, :]) + m_ref[0, 3:4, :]
    _to_slabs(h2_ref, h2)
    sb =jax.nn.sigmoid(_dot_nt(wr_ref[...], h2.astype(BF16))) + rb_ref[...]
    group = _selected_group([sb[e:e + 1] for e in range(N_EXPERTS)])
    grp_ref[0] = jnp.broadcast_to(group, grp_ref.shape[1:])


def _out_projection(attn, yf, x, m, w_out_bf16, w_four_bd, ln_g, ln_b, w_router_t, router_bias, tm):
    bsz, length, _ = x.shape
    tiles_per_batch = length // tm
    row = lambda b, i: (b, i, 0)
    const2 = lambda b, i: (0, 0)
    return pl.pallas_call(
        _out_kernel,
        grid=(bsz, length // tm),
        in_specs=[pl.BlockSpec((1, tm, NA_WIDTH), row), pl.BlockSpec((1, tm, F_WIDTH), row),
                  pl.BlockSpec((1, tm, D_MODEL), row),
                  pl.BlockSpec((1, N_MOD, D_MODEL), lambda b, i: (b, 0, 0)),
                  pl.BlockSpec((NA_WIDTH + F_WIDTH, D_MODEL), const2),
                  pl.BlockSpec((F_WIDTH, F_WIDTH), const2),
                  pl.BlockSpec((1, D_MODEL), const2), pl.BlockSpec((1, D_MODEL), const2),
                  pl.BlockSpec((N_EXPERTS, D_MODEL), const2), pl.BlockSpec((N_EXPERTS, 1), const2)],
        out_specs=[pl.BlockSpec((1, tm, D_MODEL), row),
                   pl.BlockSpec((tm * SLAB_ROWS, LANES), lambda b, i: (b * tiles_per_batch + i, 0)),
                   pl.BlockSpec((1, SUBLANES, tm), lambda b, i: (b, 0, i))],
        out_shape=[jax.ShapeDtypeStruct((bsz, length, D_MODEL), F32),
                   jax.ShapeDtypeStruct((bsz * length * SLAB_ROWS, LANES), F32),
                   jax.ShapeDtypeStruct((bsz, SUBLANES, length), F32)],
        compiler_params=_cparams(2),
        name="out_proj_norm_route",
    )(attn, yf, x, m, w_out_bf16, w_four_bd, ln_g.reshape(1, D_MODEL), ln_b.reshape(1, D_MODEL),
      w_router_t, router_bias.reshape(N_EXPERTS, 1))


def _group_plan(group, tm):
    n = group.shape[0]
    n_slots = n + N_GROUPS * tm
    onehot = (group[:, None] == jnp.arange(N_GROUPS, dtype=jnp.int32)[None, :]).astype(jnp.int32)
    csum = jnp.cumsum(onehot, axis=0)
    rank = jnp.sum(onehot * csum, axis=1) - 1
    count = csum[-1]
    padded = ((count + tm - 1) // tm) * tm
    end = jnp.cumsum(padded)
    start = end - padded
    slot = jnp.sum(onehot * start[None, :], axis=1) + rank
    tile_start = jnp.arange(n_slots // tm, dtype=jnp.int32) * tm
    tile_group = jnp.minimum(jnp.sum((end[None, :] <= tile_start[:, None]).astype(jnp.int32), axis=1),
                             N_GROUPS - 1)
    tile_rows = jnp.clip((start + count)[tile_group] - tile_start, 0, tm)
    return slot, tile_group, tile_rows, start, start + count, end


ROW_DMA_UNROLL = 8


def _slab(ref, token):
    return ref.at[pl.ds(pl.multiple_of(token * SLAB_ROWS, SLAB_ROWS), SLAB_ROWS), :]


def _row_scatter(idx_ref, base, src_ref, dst_hbm, sem, n_rows, wait, inverse_ref=None, src_base=0):
    def one(r, priority):
        if wait:
            pltpu.make_async_copy(_slab(src_ref, 0), _slab(dst_hbm, 0), sem).wait()
        else:
            idx = idx_ref[base + r]
            if inverse_ref is not None:
                inverse_ref[idx] = base + r
            pltpu.make_async_copy(_slab(src_ref, src_base + r), _slab(dst_hbm, idx), sem).start(priority=priority)

    def chunk(j, carry):
        for u in range(ROW_DMA_UNROLL):
            one(j * ROW_DMA_UNROLL + u, u % 2)
        return carry

    def single(r, carry):
        one(r, 0)
        return carry

    n_chunks = n_rows // ROW_DMA_UNROLL
    lax.fori_loop(0, n_chunks, chunk, 0)
    lax.fori_loop(n_chunks * ROW_DMA_UNROLL, n_rows, single, 0)


def _to_slots_kernel(slot_ref, start_ref, rows_end_ref, end_ref, h_hbm, o_hbm, dst_ref, zbuf, sems,
                     *, tm, tm_slots, n_steps):
    step = pl.program_id(0)
    n_slots = dst_ref.shape[0]
    sem = sems.at[0]

    @pl.when(step == 0)
    def _():
        zbuf[...] = jnp.zeros_like(zbuf)

        def fill(slot0):
            rows = tm_slots * SLAB_ROWS
            return pltpu.make_async_copy(zbuf, o_hbm.at[pl.ds(pl.multiple_of(slot0 * SLAB_ROWS, rows), rows), :], sem)

        for g in range(N_GROUPS):
            @pl.when(end_ref[g] > start_ref[g])
            def _():
                fill(end_ref[g] - tm_slots).start()
        for g in range(N_GROUPS):
            @pl.when(end_ref[g] > start_ref[g])
            def _():
                fill(end_ref[g] - tm_slots).wait()

        def fill_unused(t, carry):
            fill(t * tm_slots).start()
            fill(t * tm_slots).wait()
            return carry
        lax.fori_loop(end_ref[N_GROUPS - 1] // tm_slots, n_slots // tm_slots, fill_unused, 0)

        def clear(p, carry):
            dst_ref[p] = 0
            return carry
        for g in range(N_GROUPS):
            lax.fori_loop(rows_end_ref[g], end_ref[g], clear, 0)
        lax.fori_loop(end_ref[N_GROUPS - 1], n_slots, clear, 0)

    def scatter(s, wait):
        _row_scatter(slot_ref, s * tm, h_hbm, o_hbm, sems.at[s % 2], tm, wait,
                     inverse_ref=None if wait else dst_ref, src_base=s * tm)

    scatter(step, wait=False)

    @pl.when(step >= 1)
    def _():
        scatter(step - 1, wait=True)

    @pl.when(step == n_steps - 1)
    def _():
        scatter(step, wait=True)


def _to_slots(h2_slabs, slot, start, rows_end, end, tm, tm_slots):
    n = h2_slabs.shape[0] // SLAB_ROWS
    n_slots = n + N_GROUPS * tm_slots
    grid_spec = pltpu.PrefetchScalarGridSpec(
        num_scalar_prefetch=4,
        grid=(n // tm,),
        in_specs=[pl.BlockSpec(memory_space=pl.ANY)],
        out_specs=[pl.BlockSpec(memory_space=pl.ANY), pl.BlockSpec(memory_space=pltpu.SMEM)],
        scratch_shapes=[pltpu.VMEM((tm_slots * SLAB_ROWS, LANES), F32), pltpu.SemaphoreType.DMA((2,))],
    )
    return pl.pallas_call(
        functools.partial(_to_slots_kernel, tm=tm, tm_slots=tm_slots, n_steps=n // tm),
        grid_spec=grid_spec,
        out_shape=[jax.ShapeDtypeStruct((n_slots * SLAB_ROWS, LANES), F32),
                   jax.ShapeDtypeStruct((n_slots,), jnp.int32)],
        compiler_params=_cparams(1),
        name="moe_rows_to_slots",
    )(slot, start, rows_end, end, h2_slabs)


def _moe_group_kernel(dst_ref, tile_group_ref, tile_rows_ref, n_used_ref, h_ref, wg_ref, wu_ref, wd_ref, wr_ref,
                      rb_ref, y_hbm, ybuf, sems, *, tm, n_tiles):
    del tile_group_ref
    step = pl.program_id(0)
    n_used = n_used_ref[0]
    buf = step % 2

    def scatter(tile, wait):
        _row_scatter(dst_ref, tile * tm, ybuf.at[tile % 2], y_hbm, sems.at[tile % 2], tile_rows_ref[tile], wait)

    @pl.when((step >= 2) & (step < n_used))
    def _():
        scatter(step - 2, wait=True)

    @pl.when(step < n_used)
    def _():
        h = _from_slabs(h_ref, tm).astype(BF16)
        s = jax.nn.sigmoid(_dot(h, wr_ref[0]))
        sb = s + rb_ref[0]
        epg = EXPERTS_PER_GROUP
        gates = _top2_gates([s[:, j:j + 1] for j in range(epg)], [sb[:, j:j + 1] for j in range(epg)])
        acc = None
        for e in range(epg):
            gate = _dot(h, wg_ref[e])
            up = _dot(h, wu_ref[e])
            hid = (gate * jax.nn.sigmoid(gate)) * up * gates[e]
            y = _dot(hid.astype(BF16), wd_ref[e])
            acc = y if acc is None else acc + y
        _to_slabs(ybuf.at[buf], acc)
        scatter(step, wait=False)

    @pl.when(step == n_tiles - 1)
    def _():
        @pl.when(n_used >= 2)
        def _():
            scatter(n_used - 2, wait=True)

        @pl.when(n_used >= 1)
        def _():
            scatter(n_used - 1, wait=True)


def _moe_experts(h_slots, dst, tile_group, tile_rows, n_used, n_tokens, w_gate, w_up, w_down, w_router_grp,
                 router_bias_grp, tm):
    n_slots = h_slots.shape[0] // SLAB_ROWS
    n_tiles = n_slots // tm
    epg = EXPERTS_PER_GROUP
    by_group = lambda i, dst_ref, tg_ref, tr_ref, nu_ref: (tg_ref[i], 0, 0)
    grid_spec = pltpu.PrefetchScalarGridSpec(
        num_scalar_prefetch=4,
        grid=(n_tiles,),
        in_specs=[pl.BlockSpec((tm * SLAB_ROWS, LANES),
                               lambda i, dst_ref, tg_ref, tr_ref, nu_ref: (jnp.minimum(i, nu_ref[0] - 1), 0)),
                  pl.BlockSpec((epg, D_MODEL, D_EXPERT), by_group),
                  pl.BlockSpec((epg, D_MODEL, D_EXPERT), by_group),
                  pl.BlockSpec((epg, D_EXPERT, D_MODEL), by_group),
                  pl.BlockSpec((1, D_MODEL, LANES), by_group),
                  pl.BlockSpec((1, 1, LANES), by_group)],
        out_specs=pl.BlockSpec(memory_space=pl.ANY),
        scratch_shapes=[pltpu.VMEM((2, tm * SLAB_ROWS, LANES), F32), pltpu.SemaphoreType.DMA((2,))],
    )
    return pl.pallas_call(
        functools.partial(_moe_group_kernel, tm=tm, n_tiles=n_tiles),
        grid_spec=grid_spec,
        out_shape=jax.ShapeDtypeStruct((n_tokens * SLAB_ROWS, LANES), F32),
        compiler_params=_cparams(1),
        name="moe_group_experts",
    )(dst, tile_group, tile_rows, n_used, h_slots, w_gate, w_up, w_down, w_router_grp, router_bias_grp)


def _residual_norm_kernel(y_ref, x1_ref, m_ref, g_ref, b_ref, o_ref):
    y = _from_slabs(y_ref, x1_ref.shape[0])
    z = DEEPNORM_ALPHA * x1_ref[...] + m_ref[0, 5:6, :] * y
    o_ref[...] = _layer_norm(z) * g_ref[...] + b_ref[...]


def _residual_norm(y_slabs, x1, m, ln_g, ln_b, tm):
    bsz, length, _ = x1.shape
    n = bsz * length
    tiles_per_batch = length // tm
    const2 = lambda i: (0, 0)
    row = pl.BlockSpec((tm, D_MODEL), lambda i: (i, 0))
    out = pl.pallas_call(
        _residual_norm_kernel,
        grid=(n // tm,),
        in_specs=[pl.BlockSpec((tm * SLAB_ROWS, LANES), lambda i: (i, 0)), row,
                  pl.BlockSpec((1, N_MOD, D_MODEL), lambda i: (i // tiles_per_batch, 0, 0)),
                  pl.BlockSpec((1, D_MODEL), const2), pl.BlockSpec((1, D_MODEL), const2)],
        out_specs=row,
        out_shape=jax.ShapeDtypeStruct((n, D_MODEL), F32),
        compiler_params=_cparams(1),
        name="moe_residual_norm",
    )(y_slabs, x1.reshape(n, D_MODEL), m, ln_g.reshape(1, D_MODEL), ln_b.reshape(1, D_MODEL))
    return out.reshape(bsz, length, D_MODEL)


def _grouped_moe(h2, group_rows, x1, m, w_gate, w_up, w_down, w_router_grp, router_bias_grp, ln_g, ln_b,
                 tm_experts, tm_tokens):
    bsz, length, _ = x1.shape
    n = bsz * length
    group = group_rows[:, 0, :].reshape(n).astype(jnp.int32)
    slot, tile_group, tile_rows, start, rows_end, end = _group_plan(group, tm_experts)
    h_slots, dst = _to_slots(h2, slot, start, rows_end, end, tm_tokens, tm_experts)
    n_used = (end[N_GROUPS - 1:] // tm_experts).astype(jnp.int32)
    y = _moe_experts(h_slots, dst, tile_group, tile_rows, n_used, n, w_gate, w_up, w_down, w_router_grp,
                     router_bias_grp, tm_experts)
    return _residual_norm(y, x1, m, ln_g, ln_b, tm_tokens)


def _ctx_fourier_kernel(f_ref, w1_ref, c_ref, s_ref, y_ref):
    ab = _dot(f_ref[0], w1_ref[...]).astype(BF16)
    y = _dot(c_ref[...], ab[:, :F_WIDTH]) + _dot(s_ref[...], ab[:, F_WIDTH:])
    y_ref[0] = y.astype(BF16)


def _context_fourier(fc):
    bsz, n, _ = fc.shape
    w1 = _bf16_table(_channel_dft_matrix())
    c, s = _dft_cos_sin(n)
    cm = _bf16_table(c * n ** -0.5)
    sm = _bf16_table(-s * n ** -0.5)
    const2 = lambda b: (0, 0)
    blk = pl.BlockSpec((1, n, F_WIDTH), lambda b: (b, 0, 0))
    return pl.pallas_call(
        _ctx_fourier_kernel,
        grid=(bsz,),
        in_specs=[blk, pl.BlockSpec(w1.shape, const2), pl.BlockSpec(cm.shape, const2),
                  pl.BlockSpec(sm.shape, const2)],
        out_specs=blk,
        out_shape=jax.ShapeDtypeStruct((bsz, n, F_WIDTH), BF16),
        compiler_params=_cparams(1),
        name="context_fnet",
    )(fc, w1, cm, sm)


def _block_diag(w):
    g, c, _ = w.shape
    eye = jnp.eye(g, dtype=w.dtype)
    return (eye[:, None, :, None] * w[:, :, None, :]).reshape(g * c, g * c)


def kernel(x, c, ctx, c_ctx, w_mod, b_mod, w_in, rpb, w_four, w_out, ln1_g, ln1_b, ln2_g, ln2_b,
           w_router, router_bias, w_gate, w_up, w_down):
    bsz, length, _ = x.shape
    n_ctx = ctx.shape[1]
    rows = length // GRID_W

    cvec = jnp.concatenate([c, c_ctx[None, :], jnp.zeros((8 - bsz - 1, D_MODEL), F32)], axis=0)
    mods = _modulation(cvec, w_mod, b_mod)
    w_router_t = w_router.T.astype(BF16)
    toeplitz = _rpb_toeplitz(rpb)
    lane_pad = LANES - EXPERTS_PER_GROUP
    w_router_grp = jnp.pad(w_router.reshape(D_MODEL, N_GROUPS, EXPERTS_PER_GROUP).transpose(1, 0, 2),
                           ((0, 0), (0, 0), (0, lane_pad))).astype(BF16)
    router_bias_grp = jnp.pad(router_bias.astype(F32).reshape(N_GROUPS, 1, EXPERTS_PER_GROUP),
                              ((0, 0), (0, 0), (0, lane_pad)))

    xc = ctx
    for i in range(DEPTH):
        last = i == DEPTH - 1
        m = mods[i, :bsz].reshape(bsz, N_MOD, D_MODEL)
        mc = jnp.broadcast_to(mods[i, bsz].reshape(1, N_MOD, D_MODEL), (bsz, N_MOD, D_MODEL))
        w_in_b = w_in[i].astype(BF16)
        w_out_b = w_out[i].astype(BF16)
        w_four_bd = _block_diag(w_four[i]).astype(BF16)
        wg, wu, wd = w_gate[i].astype(BF16), w_up[i].astype(BF16), w_down[i].astype(BF16)

        q, k, v, f = _in_projection(x, m, w_in_b, tm=512)
        qc, kc, vc, fc = _in_projection(xc, mc, w_in_b, tm=n_ctx)

        attn = _neighborhood_attention(q, k, v, kc, vc, toeplitz[i])
        yf = _fourier_positions(f, n_slow=rows, n_fast=GRID_W)
        x1, h2, grp = _out_projection(attn, yf, x, m, w_out_b, w_four_bd, ln1_g[i], ln1_b[i],
                                      w_router_t, router_bias, tm=512)
        x = _grouped_moe(h2, grp, x1, m, wg, wu, wd, w_router_grp, router_bias_grp, ln2_g[i], ln2_b[i],
                         tm_experts=512, tm_tokens=512)
        if last:
            break

        attn_c = _context_attention(qc, kc, vc)
        yc = _context_fourier(fc)
        xc1, h2c, grp_c = _out_projection(attn_c, yc, xc, mc, w_out_b, w_four_bd, ln1_g[i], ln1_b[i],
                                          w_router_t, router_bias, tm=n_ctx)
        xc = _grouped_moe(h2c, grp_c, xc1, mc, wg, wu, wd, w_router_grp, router_bias_grp, ln2_g[i], ln2_b[i],
                          tm_experts=128, tm_tokens=n_ctx)
    return x
```

```python
import functools
import math

import numpy as np
import jax
import jax.numpy as jnp
from jax import lax
from jax.experimental import pallas as pl
from jax.experimental.pallas import tpu as pltpu

D_MODEL = 1024
DEPTH = 2
GRID_W = 64
NA_HEADS = 8
HEAD_DIM = 64
NA_WIDTH = NA_HEADS * HEAD_DIM
WIN_ROWS = 8
WIN_COLS = 16
F_GROUPS = 8
F_GROUP_DIM = 64
F_WIDTH = F_GROUPS * F_GROUP_DIM
IN_WIDTH = 3 * NA_WIDTH + F_WIDTH
N_EXPERTS = 16
N_GROUPS = 4
EXPERTS_PER_GROUP = N_EXPERTS // N_GROUPS
D_EXPERT = 256
N_MOD = 6
DEEPNORM_ALPHA = (2.0 * DEPTH) ** 0.25
LN_EPS = 1e-6

F32 = jnp.float32
BF16 = jnp.bfloat16

V7X_VMEM_BYTES = 64 * 1024 * 1024
VMEM_LIMIT_BYTES = (V7X_VMEM_BYTES * 3) // 4
LANES = 128
SUBLANES = 8
HEADS_PER_STEP = LANES // HEAD_DIM
MASK_VALUE = -1e30

Q_ROWS = 8
K_ROWS = 16
K_CHUNK_ROWS = 4
N_K_CHUNKS = K_ROWS // K_CHUNK_ROWS


def _cparams(n_grid_dims):
    return pltpu.CompilerParams(dimension_semantics=("arbitrary",) * n_grid_dims,
                                vmem_limit_bytes=VMEM_LIMIT_BYTES)


def _layer_norm(x):
    mu = jnp.mean(x, axis=-1, keepdims=True)
    xc = x - mu
    var = jnp.mean(xc * xc, axis=-1, keepdims=True)
    return xc * lax.rsqrt(var + LN_EPS)


SLAB_ROWS = D_MODEL // LANES
assert SLAB_ROWS == SUBLANES


def _to_slabs(ref, x):
    for j in range(SLAB_ROWS):
        ref[pl.ds(j, x.shape[0], stride=SLAB_ROWS), :] = x[:, j * LANES:(j + 1) * LANES]


def _from_slabs(ref, n_tokens):
    return jnp.concatenate([ref[pl.ds(j, n_tokens, stride=SLAB_ROWS), :] for j in range(SLAB_ROWS)], axis=-1)


def _dot(a, b):
    return jnp.dot(a, b, preferred_element_type=F32)


def _dot_nt(a, b):
    return lax.dot_general(a, b, (((1,), (1,)), ((), ())), preferred_element_type=F32)


def _mod_kernel(c_ref, w_ref, b_ref, o_ref):
    c = c_ref[...]
    a = c * jax.nn.sigmoid(c)
    o_ref[0] = jnp.dot(a, w_ref[0], preferred_element_type=F32, precision=lax.Precision.HIGHEST) + b_ref[0]


def _modulation(cvec, w_mod, b_mod):
    n_col_blocks = 4
    wc = (N_MOD * D_MODEL) // n_col_blocks
    rows = cvec.shape[0]
    return pl.pallas_call(
        _mod_kernel,
        grid=(DEPTH, n_col_blocks),
        in_specs=[pl.BlockSpec((rows, D_MODEL), lambda i, j: (0, 0)),
                  pl.BlockSpec((1, D_MODEL, wc), lambda i, j: (i, 0, j)),
                  pl.BlockSpec((1, 1, wc), lambda i, j: (i, 0, j))],
        out_specs=pl.BlockSpec((1, rows, wc), lambda i, j: (i, 0, j)),
        out_shape=jax.ShapeDtypeStruct((DEPTH, rows, N_MOD * D_MODEL), F32),
        compiler_params=_cparams(2),
        name="modulation",
    )(cvec, w_mod, b_mod.reshape(DEPTH, 1, N_MOD * D_MODEL))


def _proj_kernel(x_ref, m_ref, w_ref, q_ref, k_ref, v_ref, f_ref):
    h = _layer_norm(x_ref[0]) * (1.0 + m_ref[0, 1:2, :]) + m_ref[0, 0:1, :]
    p = _dot(h.astype(BF16), w_ref[...])
    q_ref[0] = (p[:, :NA_WIDTH] * (HEAD_DIM ** -0.5)).astype(BF16)
    k_ref[0] = p[:, NA_WIDTH:2 * NA_WIDTH].astype(BF16)
    v_ref[0] = p[:, 2 * NA_WIDTH:3 * NA_WIDTH].astype(BF16)
    f_ref[0] = p[:, 3 * NA_WIDTH:]


def _in_projection(x, m, w_in_bf16, tm):
    bsz, length, _ = x.shape
    out = jax.ShapeDtypeStruct((bsz, length, NA_WIDTH), BF16)
    out_f = jax.ShapeDtypeStruct((bsz, length, F_WIDTH), F32)
    o_spec = pl.BlockSpec((1, tm, NA_WIDTH), lambda b, i: (b, i, 0))
    return pl.pallas_call(
        _proj_kernel,
        grid=(bsz, length // tm),
        in_specs=[pl.BlockSpec((1, tm, D_MODEL), lambda b, i: (b, i, 0)),
                  pl.BlockSpec((1, N_MOD, D_MODEL), lambda b, i: (b, 0, 0)),
                  pl.BlockSpec((D_MODEL, IN_WIDTH), lambda b, i: (0, 0))],
        out_specs=[o_spec, o_spec, o_spec, o_spec],
        out_shape=[out, out, out, out_f],
        compiler_params=_cparams(2),
        name="ln_mod_in_proj",
    )(x, m, w_in_bf16)


def _head_lanes(h):
    lane = lax.broadcasted_iota(jnp.int32, (1, LANES), 1)
    return (lane >= HEAD_DIM * h) & (lane < HEAD_DIM * (h + 1))


def _scores_pass(q, h, tiles, s_ref):
    qh = jnp.where(_head_lanes(h), q, jnp.zeros_like(q))
    m = None
    t = tiles[0][0].shape[0]
    for j, (k, _, bias) in enumerate(tiles):
        s = _dot_nt(qh, k)
        if bias is not None:
            s = s + bias
        s_ref[:, j * t:(j + 1) * t] = s
        mj = jnp.max(s, axis=-1, keepdims=True)
        m = mj if m is None else jnp.maximum(m, mj)
    return m


def _pv_pass(tiles, s_ref, m):
    denom, o = None, None
    t = tiles[0][0].shape[0]
    for j, (_, v, _) in enumerate(tiles):
        p = jnp.exp(s_ref[:, j * t:(j + 1) * t] - m)
        dj = jnp.sum(p, axis=-1, keepdims=True)
        oj = _dot(p.astype(BF16), v)
        denom = dj if denom is None else denom + dj
        o = oj if o is None else o + oj
    return o / denom


def _attention_units(units):
    outs = []
    maxima = [_scores_pass(*units[0])]
    for u in range(len(units)):
        if u + 1 < len(units):
            maxima.append(_scores_pass(*units[u + 1]))
        _, _, tiles, s_ref = units[u]
        outs.append(_pv_pass(tiles, s_ref, maxima[u]))
    return outs


def _merge_heads(outs):
    merged = outs[0]
    for h in range(1, len(outs)):
        merged = jnp.where(_head_lanes(h), outs[h], merged)
    return merged


DR_PAD = 2 * WIN_ROWS
DC_PAD = 2 * WIN_COLS


def _toeplitz_kernel(r_ref, sel_ref, mask_ref, o_ref):
    n = r_ref.shape[0]
    for qc in range(GRID_W):
        block = jnp.dot(r_ref[...], sel_ref[qc], preferred_element_type=F32, precision=lax.Precision.HIGHEST)
        o_ref[pl.ds(qc, n, stride=GRID_W), :] = block + mask_ref[qc]


def _rpb_toeplitz(rpb):
    depth, heads, n_dr, n_dc = rpb.shape
    qc = np.arange(GRID_W)[:, None]
    kc = (np.arange(LANES) % GRID_W)[None, :]
    cs = np.clip(qc - WIN_COLS // 2, 0, GRID_W - WIN_COLS)
    col_valid = (kc >= cs) & (kc < cs + WIN_COLS)
    dc = kc - qc + WIN_COLS - 1
    select = (np.arange(DC_PAD)[None, :, None] == dc[:, None, :]) & col_valid[:, None, :]
    mask = np.where(col_valid, 0.0, MASK_VALUE)[:, None, :]
    r = jnp.pad(rpb.astype(F32), ((0, 0), (0, 0), (0, DR_PAD - n_dr), (0, DC_PAD - n_dc)))
    n = depth * heads * DR_PAD
    const3 = lambda: (0, 0, 0)
    out = pl.pallas_call(
        _toeplitz_kernel,
        grid=(),
        in_specs=[pl.BlockSpec((n, DC_PAD), lambda: (0, 0)),
                  pl.BlockSpec((GRID_W, DC_PAD, LANES), const3), pl.BlockSpec((GRID_W, 1, LANES), const3)],
        out_specs=pl.BlockSpec((n * GRID_W, LANES), lambda: (0, 0)),
        out_shape=jax.ShapeDtypeStruct((n * GRID_W, LANES), F32),
        compiler_params=pltpu.CompilerParams(vmem_limit_bytes=VMEM_LIMIT_BYTES),
        name="rpb_toeplitz",
    )(r.reshape(n, DC_PAD), jnp.asarray(select, F32), jnp.asarray(mask, F32))
    return out.reshape(depth, heads, DR_PAD, GRID_W, LANES)


Q_CHUNK_ROWS = 4
WINDOW_CHUNKS = 3
N_Q_CHUNKS = Q_ROWS // Q_CHUNK_ROWS
assert Q_CHUNK_ROWS == K_CHUNK_ROWS and WINDOW_CHUNKS * K_CHUNK_ROWS >= Q_CHUNK_ROWS + WIN_ROWS - 1
assert N_Q_CHUNKS == 2 and N_K_CHUNKS == 4


def _key_row_start(rb, rows):
    return int(np.clip(rb * Q_ROWS - (K_ROWS - Q_ROWS) // 2, 0, rows - K_ROWS))


def _window_uses_last_chunk(rb, c, n_rb):
    if c == 0:
        return rb == n_rb - 1
    return rb != 0


def _window_key_rows(rb, c, rows):
    var = N_K_CHUNKS - 1 if _window_uses_last_chunk(rb, c, rows // Q_ROWS) else 0
    k0 = _key_row_start(rb, rows)
    return [k0 + K_CHUNK_ROWS * j + i for j in (var, 1, 2) for i in range(K_CHUNK_ROWS)]


def _row_window(qr, rows):
    kh = min(WIN_ROWS, rows)
    rs = int(np.clip(qr - kh // 2, 0, rows - kh))
    return rs, rs + kh


def _check_windows(rows):
    def relative(rb):
        base = rb * Q_ROWS
        return [([kr - base for kr in _window_key_rows(rb, c, rows)],
                 [tuple(r - base for r in _row_window(base + c * Q_CHUNK_ROWS + qi, rows))
                  for qi in range(Q_CHUNK_ROWS)]) for c in range(N_Q_CHUNKS)]

    n_rb = rows // Q_ROWS
    for rb in range(n_rb):
        assert rb in (0, n_rb - 1) or relative(rb) == relative(1), rb
        for c in range(N_Q_CHUNKS):
            have = set(_window_key_rows(rb, c, rows))
            for qi in range(Q_CHUNK_ROWS):
                lo, hi = _row_window(rb * Q_ROWS + c * Q_CHUNK_ROWS + qi, rows)
                assert set(range(lo, hi)) <= have, (rb, c, qi)


def _build_bias_tables(t_ref, bias_scr, rows):
    n_rb = rows // Q_ROWS
    left = lax.broadcasted_iota(jnp.int32, (GRID_W, LANES), 1) < GRID_W
    masked = jnp.full((GRID_W, LANES), MASK_VALUE, F32)
    for variant, rb in enumerate((0, 1, n_rb - 1)):
        for c in range(N_Q_CHUNKS):
            key_rows = _window_key_rows(rb, c, rows)
            for h in range(HEADS_PER_STEP):
                for qi in range(Q_CHUNK_ROWS):
                    qr = rb * Q_ROWS + c * Q_CHUNK_ROWS + qi
                    lo, hi = _row_window(qr, rows)
                    for p in range(len(key_rows) // 2):
                        pair = [t_ref[h, kr - qr + WIN_ROWS - 1] if lo <= kr < hi else None
                                for kr in key_rows[2 * p:2 * p + 2]]
                        if pair[0] is None and pair[1] is None:
                            block = masked
                        else:
                            block = jnp.where(left, masked if pair[0] is None else pair[0],
                                              masked if pair[1] is None else pair[1])
                        bias_scr[variant, c, h, qi * GRID_W:(qi + 1) * GRID_W, p * LANES:(p + 1) * LANES] = block


def _na_kernel(q_ref, k0, k1, k2, k3, v0, v1, v2, v3, kc_ref, vc_ref, t_ref, o_ref, bias_scr, s_scr, *, rows):
    n_rb = rows // Q_ROWS
    b, rb = pl.program_id(1), pl.program_id(2)

    @pl.when((b == 0) & (rb == 0))
    def _():
        _build_bias_tables(t_ref, bias_scr, rows)

    variant = jnp.where(rb == 0, 0, jnp.where(rb == n_rb - 1, 2, 1))
    tq = Q_CHUNK_ROWS * GRID_W
    tk = K_CHUNK_ROWS * GRID_W
    units = []
    for c in range(N_Q_CHUNKS):
        use_last = (rb == n_rb - 1) if c == 0 else (rb != 0)
        window = [(jnp.where(use_last, k3[0], k0[0]), jnp.where(use_last, v3[0], v0[0])),
                  (k1[0], v1[0]), (k2[0], v2[0])]
        q = q_ref[0, c * tq:(c + 1) * tq, :]
        for h in range(HEADS_PER_STEP):
            lat = [(k, v, bias_scr[variant, c, h, :, j * tk:(j + 1) * tk]) for j, (k, v) in enumerate(window)]
            units.append((q, h, lat + [(kc_ref[0], vc_ref[0], None)], s_scr.at[c, h]))
    outs = _attention_units(units)
    for c in range(N_Q_CHUNKS):
        o = _merge_heads(outs[c * HEADS_PER_STEP:(c + 1) * HEADS_PER_STEP])
        o_ref[0, c * tq:(c + 1) * tq, :] = o.astype(BF16)


def _neighborhood_attention(q, k, v, kc, vc, toeplitz):
    bsz, length, _ = q.shape
    rows = length // GRID_W
    n_rb = rows // Q_ROWS
    assert rows % Q_ROWS == 0 and rows >= K_ROWS + Q_ROWS and n_rb >= 3
    _check_windows(rows)
    n_ctx = kc.shape[1]
    tq = Q_ROWS * GRID_W
    tk = K_CHUNK_ROWS * GRID_W
    assert n_ctx == tk
    max_chunk = (rows - K_ROWS) // K_CHUNK_ROWS
    half = (K_ROWS - Q_ROWS) // 2 // K_CHUNK_ROWS

    def kv_spec(j):
        def index(hp, b, rb):
            start = jnp.clip(rb * (Q_ROWS // K_CHUNK_ROWS) - half, 0, max_chunk)
            return (b, start + j, hp)
        return pl.BlockSpec((1, tk, LANES), index)

    ctx_spec = pl.BlockSpec((1, n_ctx, LANES), lambda hp, b, rb: (b, 0, hp))
    q_spec = pl.BlockSpec((1, tq, LANES), lambda hp, b, rb: (b, rb, hp))
    window = WINDOW_CHUNKS * K_CHUNK_ROWS * GRID_W
    return pl.pallas_call(
        functools.partial(_na_kernel, rows=rows),
        grid=(NA_HEADS // HEADS_PER_STEP, bsz, n_rb),
        in_specs=([q_spec] + [kv_spec(j) for j in range(N_K_CHUNKS)] + [kv_spec(j) for j in range(N_K_CHUNKS)]
                  + [ctx_spec, ctx_spec,
                     pl.BlockSpec((HEADS_PER_STEP, DR_PAD, GRID_W, LANES), lambda hp, b, rb: (hp, 0, 0, 0))]),
        out_specs=q_spec,
        out_shape=jax.ShapeDtypeStruct((bsz, length, NA_WIDTH), BF16),
        scratch_shapes=[pltpu.VMEM((3, N_Q_CHUNKS, HEADS_PER_STEP, Q_CHUNK_ROWS * GRID_W, window), F32),
                        pltpu.VMEM((N_Q_CHUNKS, HEADS_PER_STEP, Q_CHUNK_ROWS * GRID_W, window + n_ctx), F32)],
        compiler_params=_cparams(3),
        name="neighborhood_attention",
    )(q, *([k] * N_K_CHUNKS), *([v] * N_K_CHUNKS), kc, vc, toeplitz)


def _ctx_attn_kernel(q_ref, k_ref, v_ref, o_ref, s_scr):
    tiles = [(k_ref[0], v_ref[0], None)]
    outs = _attention_units([(q_ref[0], h, tiles, s_scr.at[h]) for h in range(HEADS_PER_STEP)])
    o_ref[0] = _merge_heads(outs).astype(BF16)


def _context_attention(qc, kc, vc):
    bsz, n_ctx, _ = qc.shape
    spec = pl.BlockSpec((1, n_ctx, LANES), lambda b, hp: (b, 0, hp))
    return pl.pallas_call(
        _ctx_attn_kernel,
        grid=(bsz, NA_HEADS // HEADS_PER_STEP),
        in_specs=[spec, spec, spec],
        out_specs=spec,
        out_shape=jax.ShapeDtypeStruct((bsz, n_ctx, NA_WIDTH), BF16),
        scratch_shapes=[pltpu.VMEM((HEADS_PER_STEP, n_ctx, n_ctx), F32)],
        compiler_params=_cparams(2),
        name="context_attention",
    )(qc, kc, vc)


def _dft_cos_sin(n):
    ang = 2.0 * np.pi * np.outer(np.arange(n), np.arange(n)) / n
    return np.cos(ang), np.sin(ang)


def _bf16_table(a):
    return jnp.asarray(a, F32).astype(BF16)


def _channel_dft_matrix():
    c, s = _dft_cos_sin(F_GROUP_DIM)
    scale = F_GROUP_DIM ** -0.5
    eye = np.eye(F_GROUPS)
    return np.concatenate([np.kron(eye, c), np.kron(eye, s)], axis=1) * scale


def _fft_stage1_kernel(f_ref, w1_ref, cs_ref, sc_ref, tc_ref, ts_ref, zr_ref, zi_ref, *, n_slow, nt):
    x = jnp.concatenate([f_ref[0, :, t, :].astype(BF16) for t in range(nt)], axis=0)
    ab = _dot(x, w1_ref[...]).astype(BF16)
    for t in range(nt):
        a = ab[t * n_slow:(t + 1) * n_slow, :F_WIDTH]
        b = ab[t * n_slow:(t + 1) * n_slow, F_WIDTH:]
        z = _dot(cs_ref[...], a) + _dot(sc_ref[...], b)
        zr, zi = z[:n_slow], z[n_slow:]
        c, s = tc_ref[t], ts_ref[t]
        zr_ref[0, t] = zr * c - zi * s
        zi_ref[0, t] = zr * s + zi * c


def _fft_stage2_kernel(zr_ref, zi_ref, f_ref, y_ref, *, n_fast, kb):
    rhs = jnp.concatenate([zr_ref[0].reshape(n_fast * kb, F_WIDTH).astype(BF16),
                           zi_ref[0].reshape(n_fast * kb, F_WIDTH).astype(BF16)], axis=0)
    y_ref[0] = _dot(f_ref[...], rhs).reshape(n_fast, kb, F_WIDTH)


def _fourier_positions(f, n_slow, n_fast):
    bsz, n, _ = f.shape
    assert n == n_slow * n_fast
    nt = SUBLANES
    kb = SUBLANES
    w1 = _bf16_table(_channel_dft_matrix())
    c1, s1 = _dft_cos_sin(n_slow)
    sc1 = n_slow ** -0.5
    cs = _bf16_table(np.concatenate([c1, s1], axis=0) * sc1)
    sc = _bf16_table(np.concatenate([-s1, c1], axis=0) * sc1)
    tw = 2.0 * np.pi * np.outer(np.arange(n_fast), np.arange(n_slow)) / n
    tc = jnp.asarray(np.cos(tw)[:, :, None], F32)
    ts = jnp.asarray(np.sin(tw)[:, :, None], F32)
    z_shape = jax.ShapeDtypeStruct((bsz, n_fast, n_slow, F_WIDTH), F32)
    z_spec = pl.BlockSpec((1, nt, n_slow, F_WIDTH), lambda b, j: (b, j, 0, 0))
    const2 = lambda b, j: (0, 0)
    tw_spec = pl.BlockSpec((nt, n_slow, 1), lambda b, j: (j, 0, 0))
    zr, zi = pl.pallas_call(
        functools.partial(_fft_stage1_kernel, n_slow=n_slow, nt=nt),
        grid=(bsz, n_fast // nt),
        in_specs=[pl.BlockSpec((1, n_slow, nt, F_WIDTH), lambda b, j: (b, 0, j, 0)),
                  pl.BlockSpec(w1.shape, const2), pl.BlockSpec(cs.shape, const2), pl.BlockSpec(sc.shape, const2),
                  tw_spec, tw_spec],
        out_specs=[z_spec, z_spec],
        out_shape=[z_shape, z_shape],
        compiler_params=_cparams(2),
        name="fnet_stage1",
    )(f.reshape(bsz, n_slow, n_fast, F_WIDTH), w1, cs, sc, tc, ts)

    c2, s2 = _dft_cos_sin(n_fast)
    sc2 = n_fast ** -0.5
    eye = np.eye(kb)
    f2 = _bf16_table(np.concatenate([np.kron(c2, eye), np.kron(-s2, eye)], axis=1) * sc2)
    blk = pl.BlockSpec((1, n_fast, kb, F_WIDTH), lambda b, j: (b, 0, j, 0))
    y = pl.pallas_call(
        functools.partial(_fft_stage2_kernel, n_fast=n_fast, kb=kb),
        grid=(bsz, n_slow // kb),
        in_specs=[blk, blk, pl.BlockSpec(f2.shape, const2)],
        out_specs=blk,
        out_shape=jax.ShapeDtypeStruct((bsz, n_fast, n_slow, F_WIDTH), F32),
        compiler_params=_cparams(2),
        name="fnet_stage2",
    )(zr, zi, f2)
    return y.reshape(bsz, n, F_WIDTH)


def _second_largest_sum(a, b, c, d):
    mab, nab = jnp.maximum(a, b), jnp.minimum(a, b)
    mcd, ncd = jnp.maximum(c, d), jnp.minimum(c, d)
    return jnp.maximum(mab, mcd) + jnp.maximum(jnp.minimum(mab, mcd), jnp.maximum(nab, ncd))


def _selected_group(sb_rows):
    epg = EXPERTS_PER_GROUP
    g_score = [_second_largest_sum(*sb_rows[g * epg:(g + 1) * epg]) for g in range(N_GROUPS)]
    best = functools.reduce(jnp.maximum, g_score)
    group = jnp.full_like(best, float(N_GROUPS - 1))
    for g in range(N_GROUPS - 2, -1, -1):
        group = jnp.where(g_score[g] == best, float(g), group)
    return group


def _top2_gates(cand_s, cand_sb):
    n = len(cand_s)
    w = []
    for j in range(n):
        rank = jnp.zeros_like(cand_sb[j])
        for i in range(n):
            if i == j:
                continue
            ahead = (cand_sb[i] > cand_sb[j]) | ((cand_sb[i] == cand_sb[j]) & (i < j))
            rank = rank + jnp.where(ahead, 1.0, 0.0)
        w.append(jnp.where(rank < 2.0, cand_s[j], 0.0))
    total = functools.reduce(jnp.add, w)
    return [wj / total for wj in w]


def _out_kernel(a_ref, y_ref, x_ref, m_ref, wo_ref, bd_ref, g_ref, b_ref, wr_ref, rb_ref,
                x1_ref, h2_ref, grp_ref):
    y2 = _dot(y_ref[0].astype(BF16), bd_ref[...]).astype(BF16)
    o = _dot(a_ref[0], wo_ref[:NA_WIDTH, :]) + _dot(y2, wo_ref[NA_WIDTH:, :])
    z = DEEPNORM_ALPHA * x_ref[0] + m_ref[0, 2:3, :] * o
    x1 = _layer_norm(z) * g_ref[...] + b_ref[...]
    x1_ref[0] = x1
    h2 = _layer_norm(x1) * (1.0 + m_ref[0, 4:5, :]) + m_ref[0, 3:4, :]
    _to_slabs(h2_ref, h2)
    sb =jax.nn.sigmoid(_dot_nt(wr_ref[...], h2.astype(BF16))) + rb_ref[...]
    group = _selected_group([sb[e:e + 1] for e in range(N_EXPERTS)])
    grp_ref[0] = jnp.broadcast_to(group, grp_ref.shape[1:])


def _out_projection(attn, yf, x, m, w_out_bf16, w_four_bd, ln_g, ln_b, w_router_t, router_bias, tm):
    bsz, length, _ = x.shape
    tiles_per_batch = length // tm
    row = lambda b, i: (b, i, 0)
    const2 = lambda b, i: (0, 0)
    return pl.pallas_call(
        _out_kernel,
        grid=(bsz, length // tm),
        in_specs=[pl.BlockSpec((1, tm, NA_WIDTH), row), pl.BlockSpec((1, tm, F_WIDTH), row),
                  pl.BlockSpec((1, tm, D_MODEL), row),
                  pl.BlockSpec((1, N_MOD, D_MODEL), lambda b, i: (b, 0, 0)),
                  pl.BlockSpec((NA_WIDTH + F_WIDTH, D_MODEL), const2),
                  pl.BlockSpec((F_WIDTH, F_WIDTH), const2),
                  pl.BlockSpec((1, D_MODEL), const2), pl.BlockSpec((1, D_MODEL), const2),
                  pl.BlockSpec((N_EXPERTS, D_MODEL), const2), pl.BlockSpec((N_EXPERTS, 1), const2)],
        out_specs=[pl.BlockSpec((1, tm, D_MODEL), row),
                   pl.BlockSpec((tm * SLAB_ROWS, LANES), lambda b, i: (b * tiles_per_batch + i, 0)),
                   pl.BlockSpec((1, SUBLANES, tm), lambda b, i: (b, 0, i))],
        out_shape=[jax.ShapeDtypeStruct((bsz, length, D_MODEL), F32),
                   jax.ShapeDtypeStruct((bsz * length * SLAB_ROWS, LANES), F32),
                   jax.ShapeDtypeStruct((bsz, SUBLANES, length), F32)],
        compiler_params=_cparams(2),
        name="out_proj_norm_route",
    )(attn, yf, x, m, w_out_bf16, w_four_bd, ln_g.reshape(1, D_MODEL), ln_b.reshape(1, D_MODEL),
      w_router_t, router_bias.reshape(N_EXPERTS, 1))


def _group_plan(group, tm):
    n = group.shape[0]
    n_slots = n + N_GROUPS * tm
    onehot = (group[:, None] == jnp.arange(N_GROUPS, dtype=jnp.int32)[None, :]).astype(jnp.int32)
    csum = jnp.cumsum(onehot, axis=0)
    rank = jnp.sum(onehot * csum, axis=1) - 1
    count = csum[-1]
    padded = ((count + tm - 1) // tm) * tm
    end = jnp.cumsum(padded)
    start = end - padded
    slot = jnp.sum(onehot * start[None, :], axis=1) + rank
    tile_start = jnp.arange(n_slots // tm, dtype=jnp.int32) * tm
    tile_group = jnp.minimum(jnp.sum((end[None, :] <= tile_start[:, None]).astype(jnp.int32), axis=1),
                             N_GROUPS - 1)
    tile_rows = jnp.clip((start + count)[tile_group] - tile_start, 0, tm)
    return slot, tile_group, tile_rows, start, start + count, end


ROW_DMA_UNROLL = 8


def _slab(ref, token):
    return ref.at[pl.ds(pl.multiple_of(token * SLAB_ROWS, SLAB_ROWS), SLAB_ROWS), :]


def _row_scatter(idx_ref, base, src_ref, dst_hbm, sem, n_rows, wait, inverse_ref=None, src_base=0):
    def one(r, priority):
        if wait:
            pltpu.make_async_copy(_slab(src_ref, 0), _slab(dst_hbm, 0), sem).wait()
        else:
            idx = idx_ref[base + r]
            if inverse_ref is not None:
                inverse_ref[idx] = base + r
            pltpu.make_async_copy(_slab(src_ref, src_base + r), _slab(dst_hbm, idx), sem).start(priority=priority)

    def chunk(j, carry):
        for u in range(ROW_DMA_UNROLL):
            one(j * ROW_DMA_UNROLL + u, u % 2)
        return carry

    def single(r, carry):
        one(r, 0)
        return carry

    n_chunks = n_rows // ROW_DMA_UNROLL
    lax.fori_loop(0, n_chunks, chunk, 0)
    lax.fori_loop(n_chunks * ROW_DMA_UNROLL, n_rows, single, 0)


def _to_slots_kernel(slot_ref, start_ref, rows_end_ref, end_ref, h_ref, o_hbm, dst_ref, zbuf, sem,
                     *, tm, tm_slots):
    step = pl.program_id(0)
    n_slots = dst_ref.shape[0]

    @pl.when(step == 0)
    def _():
        zbuf[...] = jnp.zeros_like(zbuf)

        def fill(slot0):
            rows = tm_slots * SLAB_ROWS
            return pltpu.make_async_copy(zbuf, o_hbm.at[pl.ds(pl.multiple_of(slot0 * SLAB_ROWS, rows), rows), :], sem)

        for g in range(N_GROUPS):
            @pl.when(end_ref[g] > start_ref[g])
            def _():
                fill(end_ref[g] - tm_slots).start()
        for g in range(N_GROUPS):
            @pl.when(end_ref[g] > start_ref[g])
            def _():
                fill(end_ref[g] - tm_slots).wait()

        def fill_unused(t, carry):
            fill(t * tm_slots).start()
            fill(t * tm_slots).wait()
            return carry
        lax.fori_loop(end_ref[N_GROUPS - 1] // tm_slots, n_slots // tm_slots, fill_unused, 0)

        def clear(p, carry):
            dst_ref[p] = 0
            return carry
        for g in range(N_GROUPS):
            lax.fori_loop(rows_end_ref[g], end_ref[g], clear, 0)
        lax.fori_loop(end_ref[N_GROUPS - 1], n_slots, clear, 0)

    _row_scatter(slot_ref, step * tm, h_ref, o_hbm, sem, tm, wait=False, inverse_ref=dst_ref)
    _row_scatter(slot_ref, step * tm, h_ref, o_hbm, sem, tm, wait=True)


def _to_slots(h2_slabs, slot, start, rows_end, end, tm, tm_slots):
    n = h2_slabs.shape[0] // SLAB_ROWS
    n_slots = n + N_GROUPS * tm_slots
    grid_spec = pltpu.PrefetchScalarGridSpec(
        num_scalar_prefetch=4,
        grid=(n // tm,),
        in_specs=[pl.BlockSpec((tm * SLAB_ROWS, LANES), lambda i, *_: (i, 0))],
        out_specs=[pl.BlockSpec(memory_space=pl.ANY), pl.BlockSpec(memory_space=pltpu.SMEM)],
        scratch_shapes=[pltpu.VMEM((tm_slots * SLAB_ROWS, LANES), F32), pltpu.SemaphoreType.DMA(())],
    )
    return pl.pallas_call(
        functools.partial(_to_slots_kernel, tm=tm, tm_slots=tm_slots),
        grid_spec=grid_spec,
        out_shape=[jax.ShapeDtypeStruct((n_slots * SLAB_ROWS, LANES), F32),
                   jax.ShapeDtypeStruct((n_slots,), jnp.int32)],
        compiler_params=_cparams(1),
        name="moe_rows_to_slots",
    )(slot, start, rows_end, end, h2_slabs)


def _moe_group_kernel(dst_ref, tile_group_ref, tile_rows_ref, n_used_ref, h_ref, wg_ref, wu_ref, wd_ref, wr_ref,
                      rb_ref, y_hbm, ybuf, sems, *, tm, n_tiles):
    del tile_group_ref
    step = pl.program_id(0)
    n_used = n_used_ref[0]
    buf = step % 2

    def scatter(tile, wait):
        _row_scatter(dst_ref, tile * tm, ybuf.at[tile % 2], y_hbm, sems.at[tile % 2], tile_rows_ref[tile], wait)

    @pl.when((step >= 2) & (step < n_used))
    def _():
        scatter(step - 2, wait=True)

    @pl.when(step < n_used)
    def _():
        h = _from_slabs(h_ref, tm).astype(BF16)
        s = jax.nn.sigmoid(_dot(h, wr_ref[0]))
        sb = s + rb_ref[0]
        epg = EXPERTS_PER_GROUP
        gates = _top2_gates([s[:, j:j + 1] for j in range(epg)], [sb[:, j:j + 1] for j in range(epg)])
        acc = None
        for e in range(epg):
            gate = _dot(h, wg_ref[e])
            up = _dot(h, wu_ref[e])
            hid = (gate * jax.nn.sigmoid(gate)) * up * gates[e]
            y = _dot(hid.astype(BF16), wd_ref[e])
            acc = y if acc is None else acc + y
        _to_slabs(ybuf.at[buf], acc)
        scatter(step, wait=False)

    @pl.when(step == n_tiles - 1)
    def _():
        @pl.when(n_used >= 2)
        def _():
            scatter(n_used - 2, wait=True)

        @pl.when(n_used >= 1)
        def _():
            scatter(n_used - 1, wait=True)


def _moe_experts(h_slots, dst, tile_group, tile_rows, n_used, n_tokens, w_gate, w_up, w_down, w_router_grp,
                 router_bias_grp, tm):
    n_slots = h_slots.shape[0] // SLAB_ROWS
    n_tiles = n_slots // tm
    epg = EXPERTS_PER_GROUP
    by_group = lambda i, dst_ref, tg_ref, tr_ref, nu_ref: (tg_ref[i], 0, 0)
    grid_spec = pltpu.PrefetchScalarGridSpec(
        num_scalar_prefetch=4,
        grid=(n_tiles,),
        in_specs=[pl.BlockSpec((tm * SLAB_ROWS, LANES),
                               lambda i, dst_ref, tg_ref, tr_ref, nu_ref: (jnp.minimum(i, nu_ref[0] - 1), 0)),
                  pl.BlockSpec((epg, D_MODEL, D_EXPERT), by_group),
                  pl.BlockSpec((epg, D_MODEL, D_EXPERT), by_group),
                  pl.BlockSpec((epg, D_EXPERT, D_MODEL), by_group),
                  pl.BlockSpec((1, D_MODEL, LANES), by_group),
                  pl.BlockSpec((1, 1, LANES), by_group)],
        out_specs=pl.BlockSpec(memory_space=pl.ANY),
        scratch_shapes=[pltpu.VMEM((2, tm * SLAB_ROWS, LANES), F32), pltpu.SemaphoreType.DMA((2,))],
    )
    return pl.pallas_call(
        functools.partial(_moe_group_kernel, tm=tm, n_tiles=n_tiles),
        grid_spec=grid_spec,
        out_shape=jax.ShapeDtypeStruct((n_tokens * SLAB_ROWS, LANES), F32),
        compiler_params=_cparams(1),
        name="moe_group_experts",
    )(dst, tile_group, tile_rows, n_used, h_slots, w_gate, w_up, w_down, w_router_grp, router_bias_grp)


def _residual_norm_kernel(y_ref, x1_ref, m_ref, g_ref, b_ref, o_ref):
    y = _from_slabs(y_ref, x1_ref.shape[0])
    z = DEEPNORM_ALPHA * x1_ref[...] + m_ref[0, 5:6, :] * y
    o_ref[...] = _layer_norm(z) * g_ref[...] + b_ref[...]


def _residual_norm(y_slabs, x1, m, ln_g, ln_b, tm):
    bsz, length, _ = x1.shape
    n = bsz * length
    tiles_per_batch = length // tm
    const2 = lambda i: (0, 0)
    row = pl.BlockSpec((tm, D_MODEL), lambda i: (i, 0))
    out = pl.pallas_call(
        _residual_norm_kernel,
        grid=(n // tm,),
        in_specs=[pl.BlockSpec((tm * SLAB_ROWS, LANES), lambda i: (i, 0)), row,
                  pl.BlockSpec((1, N_MOD, D_MODEL), lambda i: (i // tiles_per_batch, 0, 0)),
                  pl.BlockSpec((1, D_MODEL), const2), pl.BlockSpec((1, D_MODEL), const2)],
        out_specs=row,
        out_shape=jax.ShapeDtypeStruct((n, D_MODEL), F32),
        compiler_params=_cparams(1),
        name="moe_residual_norm",
    )(y_slabs, x1.reshape(n, D_MODEL), m, ln_g.reshape(1, D_MODEL), ln_b.reshape(1, D_MODEL))
    return out.reshape(bsz, length, D_MODEL)


def _grouped_moe(h2, group_rows, x1, m, w_gate, w_up, w_down, w_router_grp, router_bias_grp, ln_g, ln_b,
                 tm_experts, tm_tokens):
    bsz, length, _ = x1.shape
    n = bsz * length
    group = group_rows[:, 0, :].reshape(n).astype(jnp.int32)
    slot, tile_group, tile_rows, start, rows_end, end = _group_plan(group, tm_experts)
    h_slots, dst = _to_slots(h2, slot, start, rows_end, end, tm_tokens, tm_experts)
    n_used = (end[N_GROUPS - 1:] // tm_experts).astype(jnp.int32)
    y = _moe_experts(h_slots, dst, tile_group, tile_rows, n_used, n, w_gate, w_up, w_down, w_router_grp,
                     router_bias_grp, tm_experts)
    return _residual_norm(y, x1, m, ln_g, ln_b, tm_tokens)


def _ctx_fourier_kernel(f_ref, w1_ref, c_ref, s_ref, y_ref):
    ab = _dot(f_ref[0].astype(BF16), w1_ref[...]).astype(BF16)
    y = _dot(c_ref[...], ab[:, :F_WIDTH]) + _dot(s_ref[...], ab[:, F_WIDTH:])
    y_ref[0] = y.astype(BF16)


def _context_fourier(fc):
    bsz, n, _ = fc.shape
    w1 = _bf16_table(_channel_dft_matrix())
    c, s = _dft_cos_sin(n)
    cm = _bf16_table(c * n ** -0.5)
    sm = _bf16_table(-s * n ** -0.5)
    const2 = lambda b: (0, 0)
    blk = pl.BlockSpec((1, n, F_WIDTH), lambda b: (b, 0, 0))
    return pl.pallas_call(
        _ctx_fourier_kernel,
        grid=(bsz,),
        in_specs=[blk, pl.BlockSpec(w1.shape, const2), pl.BlockSpec(cm.shape, const2),
                  pl.BlockSpec(sm.shape, const2)],
        out_specs=blk,
        out_shape=jax.ShapeDtypeStruct((bsz, n, F_WIDTH), BF16),
        compiler_params=_cparams(1),
        name="context_fnet",
    )(fc, w1, cm, sm)


def _block_diag(w):
    g, c, _ = w.shape
    eye = jnp.eye(g, dtype=w.dtype)
    return (eye[:, None, :, None] * w[:, :, None, :]).reshape(g * c, g * c)


def kernel(x, c, ctx, c_ctx, w_mod, b_mod, w_in, rpb, w_four, w_out, ln1_g, ln1_b, ln2_g, ln2_b,
           w_router, router_bias, w_gate, w_up, w_down):
    bsz, length, _ = x.shape
    n_ctx = ctx.shape[1]
    rows = length // GRID_W

    cvec = jnp.concatenate([c, c_ctx[None, :], jnp.zeros((8 - bsz - 1, D_MODEL), F32)], axis=0)
    mods = _modulation(cvec, w_mod, b_mod)
    w_router_t = w_router.T.astype(BF16)
    toeplitz = _rpb_toeplitz(rpb)
    lane_pad = LANES - EXPERTS_PER_GROUP
    w_router_grp = jnp.pad(w_router.reshape(D_MODEL, N_GROUPS, EXPERTS_PER_GROUP).transpose(1, 0, 2),
                           ((0, 0), (0, 0), (0, lane_pad))).astype(BF16)
    router_bias_grp = jnp.pad(router_bias.astype(F32).reshape(N_GROUPS, 1, EXPERTS_PER_GROUP),
                              ((0, 0), (0, 0), (0, lane_pad)))

    xc = ctx
    for i in range(DEPTH):
        last = i == DEPTH - 1
        m = mods[i, :bsz].reshape(bsz, N_MOD, D_MODEL)
        mc = jnp.broadcast_to(mods[i, bsz].reshape(1, N_MOD, D_MODEL), (bsz, N_MOD, D_MODEL))
        w_in_b = w_in[i].astype(BF16)
        w_out_b = w_out[i].astype(BF16)
        w_four_bd = _block_diag(w_four[i]).astype(BF16)
        wg, wu, wd = w_gate[i].astype(BF16), w_up[i].astype(BF16), w_down[i].astype(BF16)

        q, k, v, f = _in_projection(x, m, w_in_b, tm=512)
        qc, kc, vc, fc = _in_projection(xc, mc, w_in_b, tm=n_ctx)

        attn = _neighborhood_attention(q, k, v, kc, vc, toeplitz[i])
        yf = _fourier_positions(f, n_slow=rows, n_fast=GRID_W)
        x1, h2, grp = _out_projection(attn, yf, x, m, w_out_b, w_four_bd, ln1_g[i], ln1_b[i],
                                      w_router_t, router_bias, tm=512)
        x = _grouped_moe(h2, grp, x1, m, wg, wu, wd, w_router_grp, router_bias_grp, ln2_g[i], ln2_b[i],
                         tm_experts=512, tm_tokens=512)
        if last:
            break

        attn_c = _context_attention(qc, kc, vc)
        yc = _context_fourier(fc)
        xc1, h2c, grp_c = _out_projection(attn_c, yc, xc, mc, w_out_b, w_four_bd, ln1_g[i], ln1_b[i],
                                          w_router_t, router_bias, tm=n_ctx)
        xc = _grouped_moe(h2c, grp_c, xc1, mc, wg, wu, wd, w_router_grp, router_bias_grp, ln2_g[i], ln2_b[i],
                          tm_experts=128, tm_tokens=n_ctx)
    return x
```

```python
import functools
import math

import numpy as np
import jax
import jax.numpy as jnp
from jax import lax
from jax.experimental import pallas as pl
from jax.experimental.pallas import tpu as pltpu

D_MODEL = 1024
DEPTH = 2
GRID_W = 64
NA_HEADS = 8
HEAD_DIM = 64
NA_WIDTH = NA_HEADS * HEAD_DIM
WIN_ROWS = 8
WIN_COLS = 16
F_GROUPS = 8
F_GROUP_DIM = 64
F_WIDTH = F_GROUPS * F_GROUP_DIM
IN_WIDTH = 3 * NA_WIDTH + F_WIDTH
N_EXPERTS = 16
N_GROUPS = 4
EXPERTS_PER_GROUP = N_EXPERTS // N_GROUPS
D_EXPERT = 256
N_MOD = 6
DEEPNORM_ALPHA = (2.0 * DEPTH) ** 0.25
LN_EPS = 1e-6

F32 = jnp.float32
BF16 = jnp.bfloat16

V7X_VMEM_BYTES = 64 * 1024 * 1024
VMEM_LIMIT_BYTES = (V7X_VMEM_BYTES * 3) // 4
LANES = 128
SUBLANES = 8
HEADS_PER_STEP = LANES // HEAD_DIM
MASK_VALUE = -1e30

Q_ROWS = 8
K_ROWS = 16
K_CHUNK_ROWS = 4
N_K_CHUNKS = K_ROWS // K_CHUNK_ROWS


def _cparams(n_grid_dims):
    return pltpu.CompilerParams(dimension_semantics=("arbitrary",) * n_grid_dims,
                                vmem_limit_bytes=VMEM_LIMIT_BYTES)


def _layer_norm(x):
    mu = jnp.mean(x, axis=-1, keepdims=True)
    xc = x - mu
    var = jnp.mean(xc * xc, axis=-1, keepdims=True)
    return xc * lax.rsqrt(var + LN_EPS)


SLAB_ROWS = D_MODEL // LANES
assert SLAB_ROWS == SUBLANES


def _to_slabs(ref, x):
    for j in range(SLAB_ROWS):
        ref[pl.ds(j, x.shape[0], stride=SLAB_ROWS), :] = x[:, j * LANES:(j + 1) * LANES]


def _from_slabs(ref, n_tokens):
    return jnp.concatenate([ref[pl.ds(j, n_tokens, stride=SLAB_ROWS), :] for j in range(SLAB_ROWS)], axis=-1)


def _dot(a, b):
    return jnp.dot(a, b, preferred_element_type=F32)


def _dot_nt(a, b):
    return lax.dot_general(a, b, (((1,), (1,)), ((), ())), preferred_element_type=F32)


def _mod_kernel(c_ref, w_ref, b_ref, o_ref):
    c = c_ref[...]
    a = c * jax.nn.sigmoid(c)
    o_ref[0] = jnp.dot(a, w_ref[0], preferred_element_type=F32, precision=lax.Precision.HIGHEST) + b_ref[0]


def _modulation(cvec, w_mod, b_mod):
    n_col_blocks = 4
    wc = (N_MOD * D_MODEL) // n_col_blocks
    rows = cvec.shape[0]
    return pl.pallas_call(
        _mod_kernel,
        grid=(DEPTH, n_col_blocks),
        in_specs=[pl.BlockSpec((rows, D_MODEL), lambda i, j: (0, 0)),
                  pl.BlockSpec((1, D_MODEL, wc), lambda i, j: (i, 0, j)),
                  pl.BlockSpec((1, 1, wc), lambda i, j: (i, 0, j))],
        out_specs=pl.BlockSpec((1, rows, wc), lambda i, j: (i, 0, j)),
        out_shape=jax.ShapeDtypeStruct((DEPTH, rows, N_MOD * D_MODEL), F32),
        compiler_params=_cparams(2),
        name="modulation",
    )(cvec, w_mod, b_mod.reshape(DEPTH, 1, N_MOD * D_MODEL))


def _proj_kernel(x_ref, m_ref, w_ref, q_ref, k_ref, v_ref, f_ref):
    h = _layer_norm(x_ref[0]) * (1.0 + m_ref[0, 1:2, :]) + m_ref[0, 0:1, :]
    p = _dot(h.astype(BF16), w_ref[...])
    q_ref[0] = (p[:, :NA_WIDTH] * (HEAD_DIM ** -0.5)).astype(BF16)
    k_ref[0] = p[:, NA_WIDTH:2 * NA_WIDTH].astype(BF16)
    v_ref[0] = p[:, 2 * NA_WIDTH:3 * NA_WIDTH].astype(BF16)
    f_ref[0] = p[:, 3 * NA_WIDTH:]


def _in_projection(x, m, w_in_bf16, tm):
    bsz, length, _ = x.shape
    out = jax.ShapeDtypeStruct((bsz, length, NA_WIDTH), BF16)
    out_f = jax.ShapeDtypeStruct((bsz, length, F_WIDTH), F32)
    o_spec = pl.BlockSpec((1, tm, NA_WIDTH), lambda b, i: (b, i, 0))
    return pl.pallas_call(
        _proj_kernel,
        grid=(bsz, length // tm),
        in_specs=[pl.BlockSpec((1, tm, D_MODEL), lambda b, i: (b, i, 0)),
                  pl.BlockSpec((1, N_MOD, D_MODEL), lambda b, i: (b, 0, 0)),
                  pl.BlockSpec((D_MODEL, IN_WIDTH), lambda b, i: (0, 0))],
        out_specs=[o_spec, o_spec, o_spec, o_spec],
        out_shape=[out, out, out, out_f],
        compiler_params=_cparams(2),
        name="ln_mod_in_proj",
    )(x, m, w_in_bf16)


def _head_lanes(h):
    lane = lax.broadcasted_iota(jnp.int32, (1, LANES), 1)
    return (lane >= HEAD_DIM * h) & (lane < HEAD_DIM * (h + 1))


def _scores_pass(q, h, tiles, s_ref):
    qh = jnp.where(_head_lanes(h), q, jnp.zeros_like(q))
    m = None
    t = tiles[0][0].shape[0]
    for j, (k, _, bias) in enumerate(tiles):
        s = _dot_nt(qh, k)
        if bias is not None:
            s = s + bias
        s_ref[:, j * t:(j + 1) * t] = s
        mj = jnp.max(s, axis=-1, keepdims=True)
        m = mj if m is None else jnp.maximum(m, mj)
    return m


def _pv_pass(tiles, s_ref, m):
    denom, o = None, None
    t = tiles[0][0].shape[0]
    for j, (_, v, _) in enumerate(tiles):
        p = jnp.exp(s_ref[:, j * t:(j + 1) * t] - m)
        dj = jnp.sum(p, axis=-1, keepdims=True)
        oj = _dot(p.astype(BF16), v)
        denom = dj if denom is None else denom + dj
        o = oj if o is None else o + oj
    return o / denom


def _attention_units(units):
    outs = []
    maxima = [_scores_pass(*units[0])]
    for u in range(len(units)):
        if u + 1 < len(units):
            maxima.append(_scores_pass(*units[u + 1]))
        _, _, tiles, s_ref = units[u]
        outs.append(_pv_pass(tiles, s_ref, maxima[u]))
    return outs


def _merge_heads(outs):
    merged = outs[0]
    for h in range(1, len(outs)):
        merged = jnp.where(_head_lanes(h), outs[h], merged)
    return merged


DR_PAD = 2 * WIN_ROWS
DC_PAD = 2 * WIN_COLS


def _toeplitz_kernel(r_ref, sel_ref, mask_ref, o_ref):
    n = r_ref.shape[0]
    for qc in range(GRID_W):
        block = jnp.dot(r_ref[...], sel_ref[qc], preferred_element_type=F32, precision=lax.Precision.HIGHEST)
        o_ref[pl.ds(qc, n, stride=GRID_W), :] = block + mask_ref[qc]


def _rpb_toeplitz(rpb):
    depth, heads, n_dr, n_dc = rpb.shape
    qc = np.arange(GRID_W)[:, None]
    kc = (np.arange(LANES) % GRID_W)[None, :]
    cs = np.clip(qc - WIN_COLS // 2, 0, GRID_W - WIN_COLS)
    col_valid = (kc >= cs) & (kc < cs + WIN_COLS)
    dc = kc - qc + WIN_COLS - 1
    select = (np.arange(DC_PAD)[None, :, None] == dc[:, None, :]) & col_valid[:, None, :]
    mask = np.where(col_valid, 0.0, MASK_VALUE)[:, None, :]
    r = jnp.pad(rpb.astype(F32), ((0, 0), (0, 0), (0, DR_PAD - n_dr), (0, DC_PAD - n_dc)))
    n = depth * heads * DR_PAD
    const3 = lambda: (0, 0, 0)
    out = pl.pallas_call(
        _toeplitz_kernel,
        grid=(),
        in_specs=[pl.BlockSpec((n, DC_PAD), lambda: (0, 0)),
                  pl.BlockSpec((GRID_W, DC_PAD, LANES), const3), pl.BlockSpec((GRID_W, 1, LANES), const3)],
        out_specs=pl.BlockSpec((n * GRID_W, LANES), lambda: (0, 0)),
        out_shape=jax.ShapeDtypeStruct((n * GRID_W, LANES), F32),
        compiler_params=pltpu.CompilerParams(vmem_limit_bytes=VMEM_LIMIT_BYTES),
        name="rpb_toeplitz",
    )(r.reshape(n, DC_PAD), jnp.asarray(select, F32), jnp.asarray(mask, F32))
    return out.reshape(depth * heads, DR_PAD, GRID_W, LANES)


Q_CHUNK_ROWS = 4
WINDOW_CHUNKS = 3
N_Q_CHUNKS = Q_ROWS // Q_CHUNK_ROWS
assert Q_CHUNK_ROWS == K_CHUNK_ROWS and WINDOW_CHUNKS * K_CHUNK_ROWS >= Q_CHUNK_ROWS + WIN_ROWS - 1
assert N_Q_CHUNKS == 2 and N_K_CHUNKS == 4


def _key_row_start(rb, rows):
    return int(np.clip(rb * Q_ROWS - (K_ROWS - Q_ROWS) // 2, 0, rows - K_ROWS))


def _window_uses_last_chunk(rb, c, n_rb):
    if c == 0:
        return rb == n_rb - 1
    return rb != 0


def _window_key_rows(rb, c, rows):
    var = N_K_CHUNKS - 1 if _window_uses_last_chunk(rb, c, rows // Q_ROWS) else 0
    k0 = _key_row_start(rb, rows)
    return [k0 + K_CHUNK_ROWS * j + i for j in (var, 1, 2) for i in range(K_CHUNK_ROWS)]


def _row_window(qr, rows):
    kh = min(WIN_ROWS, rows)
    rs = int(np.clip(qr - kh // 2, 0, rows - kh))
    return rs, rs + kh


def _check_windows(rows):
    def relative(rb):
        base = rb * Q_ROWS
        return [([kr - base for kr in _window_key_rows(rb, c, rows)],
                 [tuple(r - base for r in _row_window(base + c * Q_CHUNK_ROWS + qi, rows))
                  for qi in range(Q_CHUNK_ROWS)]) for c in range(N_Q_CHUNKS)]

    n_rb = rows // Q_ROWS
    for rb in range(n_rb):
        assert rb in (0, n_rb - 1) or relative(rb) == relative(1), rb
        for c in range(N_Q_CHUNKS):
            have = set(_window_key_rows(rb, c, rows))
            for qi in range(Q_CHUNK_ROWS):
                lo, hi = _row_window(rb * Q_ROWS + c * Q_CHUNK_ROWS + qi, rows)
                assert set(range(lo, hi)) <= have, (rb, c, qi)


def _build_bias_tables(t_ref, bias_scr, rows):
    n_rb = rows // Q_ROWS
    left = lax.broadcasted_iota(jnp.int32, (GRID_W, LANES), 1) < GRID_W
    masked = jnp.full((GRID_W, LANES), MASK_VALUE, F32)
    for variant, rb in enumerate((0, 1, n_rb - 1)):
        for c in range(N_Q_CHUNKS):
            key_rows = _window_key_rows(rb, c, rows)
            for h in range(HEADS_PER_STEP):
                for qi in range(Q_CHUNK_ROWS):
                    qr = rb * Q_ROWS + c * Q_CHUNK_ROWS + qi
                    lo, hi = _row_window(qr, rows)
                    for p in range(len(key_rows) // 2):
                        pair = [t_ref[h, kr - qr + WIN_ROWS - 1] if lo <= kr < hi else None
                                for kr in key_rows[2 * p:2 * p + 2]]
                        if pair[0] is None and pair[1] is None:
                            block = masked
                        else:
                            block = jnp.where(left, masked if pair[0] is None else pair[0],
                                              masked if pair[1] is None else pair[1])
                        bias_scr[variant, c, h, qi * GRID_W:(qi + 1) * GRID_W, p * LANES:(p + 1) * LANES] = block


def _na_kernel(q_ref, k0, k1, k2, k3, v0, v1, v2, v3, kc_ref, vc_ref, t_ref, o_ref, bias_scr, s_scr, *, rows):
    n_rb = rows // Q_ROWS
    b, rb = pl.program_id(1), pl.program_id(2)

    @pl.when((b == 0) & (rb == 0))
    def _():
        _build_bias_tables(t_ref, bias_scr, rows)

    variant = jnp.where(rb == 0, 0, jnp.where(rb == n_rb - 1, 2, 1))
    tq = Q_CHUNK_ROWS * GRID_W
    tk = K_CHUNK_ROWS * GRID_W
    units = []
    for c in range(N_Q_CHUNKS):
        use_last = (rb == n_rb - 1) if c == 0 else (rb != 0)
        window = [(jnp.where(use_last, k3[0], k0[0]), jnp.where(use_last, v3[0], v0[0])),
                  (k1[0], v1[0]), (k2[0], v2[0])]
        q = q_ref[0, c * tq:(c + 1) * tq, :]
        for h in range(HEADS_PER_STEP):
            lat = [(k, v, bias_scr[variant, c, h, :, j * tk:(j + 1) * tk]) for j, (k, v) in enumerate(window)]
            units.append((q, h, lat + [(kc_ref[0], vc_ref[0], None)], s_scr.at[c, h]))
    outs = _attention_units(units)
    for c in range(N_Q_CHUNKS):
        o = _merge_heads(outs[c * HEADS_PER_STEP:(c + 1) * HEADS_PER_STEP])
        o_ref[0, c * tq:(c + 1) * tq, :] = o.astype(BF16)


def _neighborhood_attention(q, k, v, kc, vc, toeplitz, layer):
    bsz, length, _ = q.shape
    rows = length // GRID_W
    n_rb = rows // Q_ROWS
    assert rows % Q_ROWS == 0 and rows >= K_ROWS + Q_ROWS and n_rb >= 3
    _check_windows(rows)
    n_ctx = kc.shape[1]
    tq = Q_ROWS * GRID_W
    tk = K_CHUNK_ROWS * GRID_W
    assert n_ctx == tk
    max_chunk = (rows - K_ROWS) // K_CHUNK_ROWS
    half = (K_ROWS - Q_ROWS) // 2 // K_CHUNK_ROWS

    def kv_spec(j):
        def index(hp, b, rb):
            start = jnp.clip(rb * (Q_ROWS // K_CHUNK_ROWS) - half, 0, max_chunk)
            return (b, start + j, hp)
        return pl.BlockSpec((1, tk, LANES), index)

    ctx_spec = pl.BlockSpec((1, n_ctx, LANES), lambda hp, b, rb: (b, 0, hp))
    q_spec = pl.BlockSpec((1, tq, LANES), lambda hp, b, rb: (b, rb, hp))
    window = WINDOW_CHUNKS * K_CHUNK_ROWS * GRID_W
    return pl.pallas_call(
        functools.partial(_na_kernel, rows=rows),
        grid=(NA_HEADS // HEADS_PER_STEP, bsz, n_rb),
        in_specs=([q_spec] + [kv_spec(j) for j in range(N_K_CHUNKS)] + [kv_spec(j) for j in range(N_K_CHUNKS)]
                  + [ctx_spec, ctx_spec,
                     pl.BlockSpec((HEADS_PER_STEP, DR_PAD, GRID_W, LANES),
                                  lambda hp, b, rb: (layer * (NA_HEADS // HEADS_PER_STEP) + hp, 0, 0, 0))]),
        out_specs=q_spec,
        out_shape=jax.ShapeDtypeStruct((bsz, length, NA_WIDTH), BF16),
        scratch_shapes=[pltpu.VMEM((3, N_Q_CHUNKS, HEADS_PER_STEP, Q_CHUNK_ROWS * GRID_W, window), F32),
                        pltpu.VMEM((N_Q_CHUNKS, HEADS_PER_STEP, Q_CHUNK_ROWS * GRID_W, window + n_ctx), F32)],
        compiler_params=_cparams(3),
        name="neighborhood_attention",
    )(q, *([k] * N_K_CHUNKS), *([v] * N_K_CHUNKS), kc, vc, toeplitz)


def _ctx_attn_kernel(q_ref, k_ref, v_ref, o_ref, s_scr):
    tiles = [(k_ref[0], v_ref[0], None)]
    outs = _attention_units([(q_ref[0], h, tiles, s_scr.at[h]) for h in range(HEADS_PER_STEP)])
    o_ref[0] = _merge_heads(outs).astype(BF16)


def _context_attention(qc, kc, vc):
    bsz, n_ctx, _ = qc.shape
    spec = pl.BlockSpec((1, n_ctx, LANES), lambda b, hp: (b, 0, hp))
    return pl.pallas_call(
        _ctx_attn_kernel,
        grid=(bsz, NA_HEADS // HEADS_PER_STEP),
        in_specs=[spec, spec, spec],
        out_specs=spec,
        out_shape=jax.ShapeDtypeStruct((bsz, n_ctx, NA_WIDTH), BF16),
        scratch_shapes=[pltpu.VMEM((HEADS_PER_STEP, n_ctx, n_ctx), F32)],
        compiler_params=_cparams(2),
        name="context_attention",
    )(qc, kc, vc)


def _dft_cos_sin(n):
    ang = 2.0 * np.pi * np.outer(np.arange(n), np.arange(n)) / n
    return np.cos(ang), np.sin(ang)


def _bf16_table(a):
    return jnp.asarray(a, F32).astype(BF16)


def _channel_dft_matrix():
    c, s = _dft_cos_sin(F_GROUP_DIM)
    scale = F_GROUP_DIM ** -0.5
    eye = np.eye(F_GROUPS)
    return np.concatenate([np.kron(eye, c), np.kron(eye, s)], axis=1) * scale


def _fft_stage1_kernel(f_ref, perm_ref, w1_ref, cs_ref, sc_ref, tc_ref, ts_ref, zr_ref, zi_ref, *, n_slow, nt):
    x = f_ref[0].reshape(n_slow * nt, F_WIDTH).astype(BF16)
    x = _dot(perm_ref[...], x).astype(BF16)
    ab = _dot(x, w1_ref[...]).astype(BF16)
    for t in range(nt):
        a = ab[t * n_slow:(t + 1) * n_slow, :F_WIDTH]
        b = ab[t * n_slow:(t + 1) * n_slow, F_WIDTH:]
        z = _dot(cs_ref[...], a) + _dot(sc_ref[...], b)
        zr, zi = z[:n_slow], z[n_slow:]
        c, s = tc_ref[t], ts_ref[t]
        zr_ref[0, t] = zr * c - zi * s
        zi_ref[0, t] = zr * s + zi * c


def _fft_stage2_kernel(zr_ref, zi_ref, f_ref, y_ref, *, n_fast, kb):
    rhs = jnp.concatenate([zr_ref[0].reshape(n_fast * kb, F_WIDTH).astype(BF16),
                           zi_ref[0].reshape(n_fast * kb, F_WIDTH).astype(BF16)], axis=0)
    y_ref[0] = _dot(f_ref[...], rhs).reshape(n_fast, kb, F_WIDTH)


def _fourier_positions(f, n_slow, n_fast):
    bsz, n, _ = f.shape
    assert n == n_slow * n_fast
    nt = SUBLANES
    kb = SUBLANES
    w1 = _bf16_table(_channel_dft_matrix())
    perm = _bf16_table(np.eye(n_slow * nt).reshape(n_slow, nt, n_slow * nt).transpose(1, 0, 2)
                       .reshape(n_slow * nt, n_slow * nt))
    c1, s1 = _dft_cos_sin(n_slow)
    sc1 = n_slow ** -0.5
    cs = _bf16_table(np.concatenate([c1, s1], axis=0) * sc1)
    sc = _bf16_table(np.concatenate([-s1, c1], axis=0) * sc1)
    tw = 2.0 * np.pi * np.outer(np.arange(n_fast), np.arange(n_slow)) / n
    tc = jnp.asarray(np.cos(tw)[:, :, None], F32)
    ts = jnp.asarray(np.sin(tw)[:, :, None], F32)
    z_shape = jax.ShapeDtypeStruct((bsz, n_fast, n_slow, F_WIDTH), F32)
    z_spec = pl.BlockSpec((1, nt, n_slow, F_WIDTH), lambda b, j: (b, j, 0, 0))
    const2 = lambda b, j: (0, 0)
    tw_spec = pl.BlockSpec((nt, n_slow, 1), lambda b, j: (j, 0, 0))
    zr, zi = pl.pallas_call(
        functools.partial(_fft_stage1_kernel, n_slow=n_slow, nt=nt),
        grid=(bsz, n_fast // nt),
        in_specs=[pl.BlockSpec((1, n_slow, nt, F_WIDTH), lambda b, j: (b, 0, j, 0)),
                  pl.BlockSpec(perm.shape, const2),
                  pl.BlockSpec(w1.shape, const2), pl.BlockSpec(cs.shape, const2), pl.BlockSpec(sc.shape, const2),
                  tw_spec, tw_spec],
        out_specs=[z_spec, z_spec],
        out_shape=[z_shape, z_shape],
        compiler_params=_cparams(2),
        name="fnet_stage1",
    )(f.reshape(bsz, n_slow, n_fast, F_WIDTH), perm, w1, cs, sc, tc, ts)

    c2, s2 = _dft_cos_sin(n_fast)
    sc2 = n_fast ** -0.5
    eye = np.eye(kb)
    f2 = _bf16_table(np.concatenate([np.kron(c2, eye), np.kron(-s2, eye)], axis=1) * sc2)
    blk = pl.BlockSpec((1, n_fast, kb, F_WIDTH), lambda b, j: (b, 0, j, 0))
    y = pl.pallas_call(
        functools.partial(_fft_stage2_kernel, n_fast=n_fast, kb=kb),
        grid=(bsz, n_slow // kb),
        in_specs=[blk, blk, pl.BlockSpec(f2.shape, const2)],
        out_specs=blk,
        out_shape=jax.ShapeDtypeStruct((bsz, n_fast, n_slow, F_WIDTH), F32),
        compiler_params=_cparams(2),
        name="fnet_stage2",
    )(zr, zi, f2)
    return y.reshape(bsz, n, F_WIDTH)


def _second_largest_sum(a, b, c, d):
    mab, nab = jnp.maximum(a, b), jnp.minimum(a, b)
    mcd, ncd = jnp.maximum(c, d), jnp.minimum(c, d)
    return jnp.maximum(mab, mcd) + jnp.maximum(jnp.minimum(mab, mcd), jnp.maximum(nab, ncd))


def _selected_group(sb_rows):
    epg = EXPERTS_PER_GROUP
    g_score = [_second_largest_sum(*sb_rows[g * epg:(g + 1) * epg]) for g in range(N_GROUPS)]
    best = functools.reduce(jnp.maximum, g_score)
    group = jnp.full_like(best, float(N_GROUPS - 1))
    for g in range(N_GROUPS - 2, -1, -1):
        group = jnp.where(g_score[g] == best, float(g), group)
    return group


def _top2_gates(cand_s, cand_sb):
    n = len(cand_s)
    w = []
    for j in range(n):
        rank = jnp.zeros_like(cand_sb[j])
        for i in range(n):
            if i == j:
                continue
            ahead = (cand_sb[i] > cand_sb[j]) | ((cand_sb[i] == cand_sb[j]) & (i < j))
            rank = rank + jnp.where(ahead, 1.0, 0.0)
        w.append(jnp.where(rank < 2.0, cand_s[j], 0.0))
    total = functools.reduce(jnp.add, w)
    return [wj / total for wj in w]


def _out_kernel(a_ref, y_ref, x_ref, m_ref, wo_ref, bd_ref, g_ref, b_ref, wr_ref, rb_ref,
                x1_ref, h2_ref, grp_ref):
    y2 = _dot(y_ref[0].astype(BF16), bd_ref[...]).astype(BF16)
    o = _dot(a_ref[0], wo_ref[:NA_WIDTH, :]) + _dot(y2, wo_ref[NA_WIDTH:, :])
    z = DEEPNORM_ALPHA * x_ref[0] + m_ref[0, 2:3, :] * o
    x1 = _layer_norm(z) * g_ref[...] + b_ref[...]
    x1_ref[0] = x1
    h2 = _layer_norm(x1) * (1.0 + m_ref[0, 4:5, :]) + m_ref[0, 3:4, :]
    _to_slabs(h2_ref, h2)
    sb = jax.nn.sigmoid(_dot_nt(wr_ref[...], h2.astype(BF16))) + rb_ref[...]
    group = _selected_group([sb[e:e + 1] for e in range(N_EXPERTS)])
    grp_ref[0] = jnp.broadcast_to(group, grp_ref.shape[1:])


def _out_projection(attn, yf, x, m, w_out_bf16, w_four_bd, ln_g, ln_b, w_router_t, router_bias, tm):
    bsz, length, _ = x.shape
    tiles_per_batch = length // tm
    row = lambda b, i: (b, i, 0)
    const2 = lambda b, i: (0, 0)
    return pl.pallas_call(
        _out_kernel,
        grid=(bsz, length // tm),
        in_specs=[pl.BlockSpec((1, tm, NA_WIDTH), row), pl.BlockSpec((1, tm, F_WIDTH), row),
                  pl.BlockSpec((1, tm, D_MODEL), row),
                  pl.BlockSpec((1, N_MOD, D_MODEL), lambda b, i: (b, 0, 0)),
                  pl.BlockSpec((NA_WIDTH + F_WIDTH, D_MODEL), const2),
                  pl.BlockSpec((F_WIDTH, F_WIDTH), const2),
                  pl.BlockSpec((1, D_MODEL), const2), pl.BlockSpec((1, D_MODEL), const2),
                  pl.BlockSpec((N_EXPERTS, D_MODEL), const2), pl.BlockSpec((N_EXPERTS, 1), const2)],
        out_specs=[pl.BlockSpec((1, tm, D_MODEL), row),
                   pl.BlockSpec((tm * SLAB_ROWS, LANES), lambda b, i: (b * tiles_per_batch + i, 0)),
                   pl.BlockSpec((1, SUBLANES, tm), lambda b, i: (b, 0, i))],
        out_shape=[jax.ShapeDtypeStruct((bsz, length, D_MODEL), F32),
                   jax.ShapeDtypeStruct((bsz * length * SLAB_ROWS, LANES), F32),
                   jax.ShapeDtypeStruct((bsz, SUBLANES, length), F32)],
        compiler_params=_cparams(2),
        name="out_proj_norm_route",
    )(attn, yf, x, m, w_out_bf16, w_four_bd, ln_g.reshape(1, D_MODEL), ln_b.reshape(1, D_MODEL),
      w_router_t, router_bias.reshape(N_EXPERTS, 1))


def _group_plan(group, tm):
    n = group.shape[0]
    n_slots = n + N_GROUPS * tm
    onehot = (group[:, None] == jnp.arange(N_GROUPS, dtype=jnp.int32)[None, :]).astype(jnp.int32)
    csum = jnp.cumsum(onehot, axis=0)
    rank = jnp.sum(onehot * csum, axis=1) - 1
    count = csum[-1]
    padded = ((count + tm - 1) // tm) * tm
    end = jnp.cumsum(padded)
    start = end - padded
    slot = jnp.sum(onehot * start[None, :], axis=1) + rank
    tile_start = jnp.arange(n_slots // tm, dtype=jnp.int32) * tm
    tile_group = jnp.minimum(jnp.sum((end[None, :] <= tile_start[:, None]).astype(jnp.int32), axis=1),
                             N_GROUPS - 1)
    tile_rows = jnp.clip((start + count)[tile_group] - tile_start, 0, tm)
    return slot, tile_group, tile_rows, start, start + count, end


ROW_DMA_UNROLL = 8


def _slab(ref, token):
    return ref.at[pl.ds(pl.multiple_of(token * SLAB_ROWS, SLAB_ROWS), SLAB_ROWS), :]


def _row_scatter(idx_ref, base, src_ref, dst_hbm, sem, n_rows, wait, inverse_ref=None, src_base=0):
    def one(r, priority):
        if wait:
            pltpu.make_async_copy(_slab(src_ref, 0), _slab(dst_hbm, 0), sem).wait()
        else:
            idx = idx_ref[base + r]
            if inverse_ref is not None:
                inverse_ref[idx] = base + r
            pltpu.make_async_copy(_slab(src_ref, src_base + r), _slab(dst_hbm, idx), sem).start(priority=priority)

    def chunk(j, carry):
        for u in range(ROW_DMA_UNROLL):
            one(j * ROW_DMA_UNROLL + u, u % 2)
        return carry

    def single(r, carry):
        one(r, 0)
        return carry

    n_chunks = n_rows // ROW_DMA_UNROLL
    lax.fori_loop(0, n_chunks, chunk, 0)
    lax.fori_loop(n_chunks * ROW_DMA_UNROLL, n_rows, single, 0)


def _to_slots_kernel(slot_ref, start_ref, rows_end_ref, end_ref, h_ref, o_hbm, dst_ref, zbuf, sem,
                     *, tm, tm_slots):
    step = pl.program_id(0)
    n_slots = dst_ref.shape[0]

    @pl.when(step == 0)
    def _():
        zbuf[...] = jnp.zeros_like(zbuf)

        def fill(slot0):
            rows = tm_slots * SLAB_ROWS
            return pltpu.make_async_copy(zbuf, o_hbm.at[pl.ds(pl.multiple_of(slot0 * SLAB_ROWS, rows), rows), :], sem)

        for g in range(N_GROUPS):
            @pl.when(end_ref[g] > start_ref[g])
            def _():
                fill(end_ref[g] - tm_slots).start()
        for g in range(N_GROUPS):
            @pl.when(end_ref[g] > start_ref[g])
            def _():
                fill(end_ref[g] - tm_slots).wait()

        def fill_unused(t, carry):
            fill(t * tm_slots).start()
            fill(t * tm_slots).wait()
            return carry
        lax.fori_loop(end_ref[N_GROUPS - 1] // tm_slots, n_slots // tm_slots, fill_unused, 0)

        def clear(p, carry):
            dst_ref[p] = 0
            return carry
        for g in range(N_GROUPS):
            lax.fori_loop(rows_end_ref[g], end_ref[g], clear, 0)
        lax.fori_loop(end_ref[N_GROUPS - 1], n_slots, clear, 0)

    _row_scatter(slot_ref, step * tm, h_ref, o_hbm, sem, tm, wait=False, inverse_ref=dst_ref)
    _row_scatter(slot_ref, step * tm, h_ref, o_hbm, sem, tm, wait=True)


def _to_slots(h2_slabs, slot, start, rows_end, end, tm, tm_slots):
    n = h2_slabs.shape[0] // SLAB_ROWS
    n_slots = n + N_GROUPS * tm_slots
    grid_spec = pltpu.PrefetchScalarGridSpec(
        num_scalar_prefetch=4,
        grid=(n // tm,),
        in_specs=[pl.BlockSpec((tm * SLAB_ROWS, LANES), lambda i, *_: (i, 0))],
        out_specs=[pl.BlockSpec(memory_space=pl.ANY), pl.BlockSpec(memory_space=pltpu.SMEM)],
        scratch_shapes=[pltpu.VMEM((tm_slots * SLAB_ROWS, LANES), F32), pltpu.SemaphoreType.DMA(())],
    )
    return pl.pallas_call(
        functools.partial(_to_slots_kernel, tm=tm, tm_slots=tm_slots),
        grid_spec=grid_spec,
        out_shape=[jax.ShapeDtypeStruct((n_slots * SLAB_ROWS, LANES), F32),
                   jax.ShapeDtypeStruct((n_slots,), jnp.int32)],
        compiler_params=_cparams(1),
        name="moe_rows_to_slots",
    )(slot, start, rows_end, end, h2_slabs)


def _moe_group_kernel(dst_ref, tile_group_ref, tile_rows_ref, n_used_ref, h_ref, wg_ref, wu_ref, wd_ref, wr_ref,
                      rb_ref, y_hbm, ybuf, sems, *, tm, n_tiles):
    del tile_group_ref
    step = pl.program_id(0)
    n_used = n_used_ref[0]
    buf = step % 2

    def scatter(tile, wait):
        _row_scatter(dst_ref, tile * tm, ybuf.at[tile % 2], y_hbm, sems.at[tile % 2], tile_rows_ref[tile], wait)

    @pl.when((step >= 2) & (step < n_used))
    def _():
        scatter(step - 2, wait=True)

    @pl.when(step < n_used)
    def _():
        h = _from_slabs(h_ref, tm).astype(BF16)
        s = jax.nn.sigmoid(_dot(h, wr_ref[0]))
        sb = s + rb_ref[0]
        epg = EXPERTS_PER_GROUP
        gates = _top2_gates([s[:, j:j + 1] for j in range(epg)], [sb[:, j:j + 1] for j in range(epg)])
        acc = None
        gate_up = [(_dot(h, wg_ref[0]), _dot(h, wu_ref[0]))]
        for e in range(epg):
            if e + 1 < epg:
                gate_up.append((_dot(h, wg_ref[e + 1]), _dot(h, wu_ref[e + 1])))
            gate, up = gate_up[e]
            hid = (gate * jax.nn.sigmoid(gate)) * up * gates[e]
            y = _dot(hid.astype(BF16), wd_ref[e])
            acc = y if acc is None else acc + y
        _to_slabs(ybuf.at[buf], acc)
        scatter(step, wait=False)

    @pl.when(step == n_tiles - 1)
    def _():
        @pl.when(n_used >= 2)
        def _():
            scatter(n_used - 2, wait=True)

        @pl.when(n_used >= 1)
        def _():
            scatter(n_used - 1, wait=True)


def _moe_experts(h_slots, dst, tile_group, tile_rows, n_used, n_tokens, w_gate, w_up, w_down, w_router_grp,
                 router_bias_grp, tm):
    n_slots = h_slots.shape[0] // SLAB_ROWS
    n_tiles = n_slots // tm
    epg = EXPERTS_PER_GROUP
    by_group = lambda i, dst_ref, tg_ref, tr_ref, nu_ref: (tg_ref[i], 0, 0)
    grid_spec = pltpu.PrefetchScalarGridSpec(
        num_scalar_prefetch=4,
        grid=(n_tiles,),
        in_specs=[pl.BlockSpec((tm * SLAB_ROWS, LANES),
                               lambda i, dst_ref, tg_ref, tr_ref, nu_ref: (jnp.minimum(i, nu_ref[0] - 1), 0)),
                  pl.BlockSpec((epg, D_MODEL, D_EXPERT), by_group),
                  pl.BlockSpec((epg, D_MODEL, D_EXPERT), by_group),
                  pl.BlockSpec((epg, D_EXPERT, D_MODEL), by_group),
                  pl.BlockSpec((1, D_MODEL, LANES), by_group),
                  pl.BlockSpec((1, 1, LANES), by_group)],
        out_specs=pl.BlockSpec(memory_space=pl.ANY),
        scratch_shapes=[pltpu.VMEM((2, tm * SLAB_ROWS, LANES), F32), pltpu.SemaphoreType.DMA((2,))],
    )
    return pl.pallas_call(
        functools.partial(_moe_group_kernel, tm=tm, n_tiles=n_tiles),
        grid_spec=grid_spec,
        out_shape=jax.ShapeDtypeStruct((n_tokens * SLAB_ROWS, LANES), F32),
        compiler_params=_cparams(1),
        name="moe_group_experts",
    )(dst, tile_group, tile_rows, n_used, h_slots, w_gate, w_up, w_down, w_router_grp, router_bias_grp)


def _residual_norm_kernel(y_ref, x1_ref, m_ref, g_ref, b_ref, o_ref):
    y = _from_slabs(y_ref, x1_ref.shape[0])
    z = DEEPNORM_ALPHA * x1_ref[...] + m_ref[0, 5:6, :] * y
    o_ref[...] = _layer_norm(z) * g_ref[...] + b_ref[...]


def _residual_norm(y_slabs, x1, m, ln_g, ln_b, tm):
    bsz, length, _ = x1.shape
    n = bsz * length
    tiles_per_batch = length // tm
    const2 = lambda i: (0, 0)
    row = pl.BlockSpec((tm, D_MODEL), lambda i: (i, 0))
    out = pl.pallas_call(
        _residual_norm_kernel,
        grid=(n // tm,),
        in_specs=[pl.BlockSpec((tm * SLAB_ROWS, LANES), lambda i: (i, 0)), row,
                  pl.BlockSpec((1, N_MOD, D_MODEL), lambda i: (i // tiles_per_batch, 0, 0)),
                  pl.BlockSpec((1, D_MODEL), const2), pl.BlockSpec((1, D_MODEL), const2)],
        out_specs=row,
        out_shape=jax.ShapeDtypeStruct((n, D_MODEL), F32),
        compiler_params=_cparams(1),
        name="moe_residual_norm",
    )(y_slabs, x1.reshape(n, D_MODEL), m, ln_g.reshape(1, D_MODEL), ln_b.reshape(1, D_MODEL))
    return out.reshape(bsz, length, D_MODEL)


def _grouped_moe(h2, group_rows, x1, m, w_gate, w_up, w_down, w_router_grp, router_bias_grp, ln_g, ln_b,
                 tm_experts, tm_tokens):
    bsz, length, _ = x1.shape
    n = bsz * length
    group = group_rows[:, 0, :].reshape(n).astype(jnp.int32)
    slot, tile_group, tile_rows, start, rows_end, end = _group_plan(group, tm_experts)
    h_slots, dst = _to_slots(h2, slot, start, rows_end, end, tm_tokens, tm_experts)
    n_used = (end[N_GROUPS - 1:] // tm_experts).astype(jnp.int32)
    y = _moe_experts(h_slots, dst, tile_group, tile_rows, n_used, n, w_gate, w_up, w_down, w_router_grp,
                     router_bias_grp, tm_experts)
    return _residual_norm(y, x1, m, ln_g, ln_b, tm_tokens)


def _ctx_fourier_kernel(f_ref, w1_ref, c_ref, s_ref, y_ref):
    ab = _dot(f_ref[0].astype(BF16), w1_ref[...]).astype(BF16)
    y = _dot(c_ref[...], ab[:, :F_WIDTH]) + _dot(s_ref[...], ab[:, F_WIDTH:])
    y_ref[0] = y.astype(BF16)


def _context_fourier(fc):
    bsz, n, _ = fc.shape
    w1 = _bf16_table(_channel_dft_matrix())
    c, s = _dft_cos_sin(n)
    cm = _bf16_table(c * n ** -0.5)
    sm = _bf16_table(-s * n ** -0.5)
    const2 = lambda b: (0, 0)
    blk = pl.BlockSpec((1, n, F_WIDTH), lambda b: (b, 0, 0))
    return pl.pallas_call(
        _ctx_fourier_kernel,
        grid=(bsz,),
        in_specs=[blk, pl.BlockSpec(w1.shape, const2), pl.BlockSpec(cm.shape, const2),
                  pl.BlockSpec(sm.shape, const2)],
        out_specs=blk,
        out_shape=jax.ShapeDtypeStruct((bsz, n, F_WIDTH), BF16),
        compiler_params=_cparams(1),
        name="context_fnet",
    )(fc, w1, cm, sm)


def _block_diag(w):
    g, c, _ = w.shape
    eye = jnp.eye(g, dtype=w.dtype)
    return (eye[:, None, :, None] * w[:, :, None, :]).reshape(g * c, g * c)


def kernel(x, c, ctx, c_ctx, w_mod, b_mod, w_in, rpb, w_four, w_out, ln1_g, ln1_b, ln2_g, ln2_b,
           w_router, router_bias, w_gate, w_up, w_down):
    bsz, length, _ = x.shape
    n_ctx = ctx.shape[1]
    rows = length // GRID_W

    cvec = jnp.concatenate([c, c_ctx[None, :], jnp.zeros((8 - bsz - 1, D_MODEL), F32)], axis=0)
    mods = _modulation(cvec, w_mod, b_mod)
    w_router_t = w_router.T.astype(BF16)
    toeplitz = _rpb_toeplitz(rpb)
    lane_pad = LANES - EXPERTS_PER_GROUP
    w_router_grp = jnp.pad(w_router.reshape(D_MODEL, N_GROUPS, EXPERTS_PER_GROUP).transpose(1, 0, 2),
                           ((0, 0), (0, 0), (0, lane_pad))).astype(BF16)
    router_bias_grp = jnp.pad(router_bias.astype(F32).reshape(N_GROUPS, 1, EXPERTS_PER_GROUP),
                              ((0, 0), (0, 0), (0, lane_pad)))

    xc = ctx
    for i in range(DEPTH):
        last = i == DEPTH - 1
        m = mods[i, :bsz].reshape(bsz, N_MOD, D_MODEL)
        mc = jnp.broadcast_to(mods[i, bsz].reshape(1, N_MOD, D_MODEL), (bsz, N_MOD, D_MODEL))
        w_in_b = w_in[i].astype(BF16)
        w_out_b = w_out[i].astype(BF16)
        w_four_bd = _block_diag(w_four[i]).astype(BF16)
        wg, wu, wd = w_gate[i].astype(BF16), w_up[i].astype(BF16), w_down[i].astype(BF16)

        q, k, v, f = _in_projection(x, m, w_in_b, tm=512)
        qc, kc, vc, fc = _in_projection(xc, mc, w_in_b, tm=n_ctx)

        attn = _neighborhood_attention(q, k, v, kc, vc, toeplitz, i)
        yf = _fourier_positions(f, n_slow=rows, n_fast=GRID_W)
        x1, h2, grp = _out_projection(attn, yf, x, m, w_out_b, w_four_bd, ln1_g[i], ln1_b[i],
                                      w_router_t, router_bias, tm=512)
        x = _grouped_moe(h2, grp, x1, m, wg, wu, wd, w_router_grp, router_bias_grp, ln2_g[i], ln2_b[i],
                         tm_experts=512, tm_tokens=512)
        if last:
            break

        attn_c = _context_attention(qc, kc, vc)
        yc = _context_fourier(fc)
        xc1, h2c, grp_c = _out_projection(attn_c, yc, xc, mc, w_out_b, w_four_bd, ln1_g[i], ln1_b[i],
                                          w_router_t, router_bias, tm=n_ctx)
        xc = _grouped_moe(h2c, grp_c, xc1, mc, wg, wu, wd, w_router_grp, router_bias_grp, ln2_g[i], ln2_b[i],
                          tm_experts=128, tm_tokens=n_ctx)
    return x
```

```python
import functools
import math

import numpy as np
import jax
import jax.numpy as jnp
from jax import lax
from jax.experimental import pallas as pl
from jax.experimental.pallas import tpu as pltpu

D_MODEL = 1024
DEPTH = 2
GRID_W = 64
NA_HEADS = 8
HEAD_DIM = 64
NA_WIDTH = NA_HEADS * HEAD_DIM
WIN_ROWS = 8
WIN_COLS = 16
F_GROUPS = 8
F_GROUP_DIM = 64
F_WIDTH = F_GROUPS * F_GROUP_DIM
IN_WIDTH = 3 * NA_WIDTH + F_WIDTH
N_EXPERTS = 16
N_GROUPS = 4
EXPERTS_PER_GROUP = N_EXPERTS // N_GROUPS
D_EXPERT = 256
N_MOD = 6
DEEPNORM_ALPHA = (2.0 * DEPTH) ** 0.25
LN_EPS = 1e-6

F32 = jnp.float32
BF16 = jnp.bfloat16

V7X_VMEM_BYTES = 64 * 1024 * 1024
VMEM_LIMIT_BYTES = (V7X_VMEM_BYTES * 3) // 4
LANES = 128
SUBLANES = 8
HEADS_PER_STEP = LANES // HEAD_DIM
assert HEADS_PER_STEP == 2
MASK_VALUE = -1e30

Q_ROWS = 8
K_ROWS = 16
K_CHUNK_ROWS = 4
N_K_CHUNKS = K_ROWS // K_CHUNK_ROWS


def _cparams(n_grid_dims):
    return pltpu.CompilerParams(dimension_semantics=("arbitrary",) * n_grid_dims,
                                vmem_limit_bytes=VMEM_LIMIT_BYTES)


def _layer_norm(x):
    mu = jnp.mean(x, axis=-1, keepdims=True)
    xc = x - mu
    var = jnp.mean(xc * xc, axis=-1, keepdims=True)
    return xc * lax.rsqrt(var + LN_EPS)


SLAB_ROWS = D_MODEL // LANES
assert SLAB_ROWS == SUBLANES


def _to_slabs(ref, x):
    for j in range(SLAB_ROWS):
        ref[pl.ds(j, x.shape[0], stride=SLAB_ROWS), :] = x[:, j * LANES:(j + 1) * LANES]


def _from_slabs(ref, n_tokens):
    return jnp.concatenate([ref[pl.ds(j, n_tokens, stride=SLAB_ROWS), :] for j in range(SLAB_ROWS)], axis=-1)


def _dot(a, b):
    return jnp.dot(a, b, preferred_element_type=F32)


def _dot_nt(a, b):
    return lax.dot_general(a, b, (((1,), (1,)), ((), ())), preferred_element_type=F32)


def _mod_kernel(c_ref, w_ref, b_ref, o_ref):
    c = c_ref[...]
    a = c * jax.nn.sigmoid(c)
    o_ref[0] = jnp.dot(a, w_ref[0], preferred_element_type=F32, precision=lax.Precision.HIGHEST) + b_ref[0]


def _modulation(cvec, w_mod, b_mod):
    n_col_blocks = 4
    wc = (N_MOD * D_MODEL) // n_col_blocks
    rows = cvec.shape[0]
    return pl.pallas_call(
        _mod_kernel,
        grid=(DEPTH, n_col_blocks),
        in_specs=[pl.BlockSpec((rows, D_MODEL), lambda i, j: (0, 0)),
                  pl.BlockSpec((1, D_MODEL, wc), lambda i, j: (i, 0, j)),
                  pl.BlockSpec((1, 1, wc), lambda i, j: (i, 0, j))],
        out_specs=pl.BlockSpec((1, rows, wc), lambda i, j: (i, 0, j)),
        out_shape=jax.ShapeDtypeStruct((DEPTH, rows, N_MOD * D_MODEL), F32),
        compiler_params=_cparams(2),
        name="modulation",
    )(cvec, w_mod, b_mod.reshape(DEPTH, 1, N_MOD * D_MODEL))


def _proj_kernel(x_ref, m_ref, w_ref, q_ref, k_ref, v_ref, f_ref):
    h = _layer_norm(x_ref[0]) * (1.0 + m_ref[0, 1:2, :]) + m_ref[0, 0:1, :]
    p = _dot(h.astype(BF16), w_ref[...])
    q_ref[0] = (p[:, :NA_WIDTH] * (HEAD_DIM ** -0.5)).astype(BF16)
    k_ref[0] = p[:, NA_WIDTH:2 * NA_WIDTH].astype(BF16)
    v_ref[0] = p[:, 2 * NA_WIDTH:3 * NA_WIDTH].astype(BF16)
    f_ref[0] = p[:, 3 * NA_WIDTH:]


def _in_projection(x, m, w_in_bf16, tm):
    bsz, length, _ = x.shape
    out = jax.ShapeDtypeStruct((bsz, length, NA_WIDTH), BF16)
    out_f = jax.ShapeDtypeStruct((bsz, length, F_WIDTH), F32)
    o_spec = pl.BlockSpec((1, tm, NA_WIDTH), lambda b, i: (b, i, 0))
    return pl.pallas_call(
        _proj_kernel,
        grid=(bsz, length // tm),
        in_specs=[pl.BlockSpec((1, tm, D_MODEL), lambda b, i: (b, i, 0)),
                  pl.BlockSpec((1, N_MOD, D_MODEL), lambda b, i: (b, 0, 0)),
                  pl.BlockSpec((D_MODEL, IN_WIDTH), lambda b, i: (0, 0))],
        out_specs=[o_spec, o_spec, o_spec, o_spec],
        out_shape=[out, out, out, out_f],
        compiler_params=_cparams(2),
        name="ln_mod_in_proj",
    )(x, m, w_in_bf16)


def _head_lanes(h):
    lane = lax.broadcasted_iota(jnp.int32, (1, LANES), 1)
    return (lane >= HEAD_DIM * h) & (lane < HEAD_DIM * (h + 1))


def _scores_pass(q, h, tiles, s_ref):
    qh = jnp.where(_head_lanes(h), q, jnp.zeros_like(q))
    m = None
    t = tiles[0][0].shape[0]
    for j, (k, _, bias) in enumerate(tiles):
        s = _dot_nt(qh, k)
        if bias is not None:
            s = s + bias
        s_ref[:, j * t:(j + 1) * t] = s
        mj = jnp.max(s, axis=-1, keepdims=True)
        m = mj if m is None else jnp.maximum(m, mj)
    return m


def _pv_pass(h, tiles, s_ref, m):
    o = None
    t = tiles[0][0].shape[0]
    in_head = _head_lanes(h)
    for j, (_, v, _) in enumerate(tiles):
        p = jnp.exp(s_ref[:, j * t:(j + 1) * t] - m)
        oj = _dot(p.astype(BF16), jnp.where(in_head, v, jnp.ones_like(v)))
        o = oj if o is None else o + oj
    return o / pltpu.roll(o, HEAD_DIM, axis=1)


def _attention_units(units):
    outs = []
    maxima = [_scores_pass(*units[0])]
    for u in range(len(units)):
        if u + 1 < len(units):
            maxima.append(_scores_pass(*units[u + 1]))
        _, h, tiles, s_ref = units[u]
        outs.append(_pv_pass(h, tiles, s_ref, maxima[u]))
    return outs


def _merge_heads(outs):
    merged = outs[0]
    for h in range(1, len(outs)):
        merged = jnp.where(_head_lanes(h), outs[h], merged)
    return merged


DR_PAD = 2 * WIN_ROWS
DC_PAD = 2 * WIN_COLS


def _toeplitz_kernel(r_ref, sel_ref, mask_ref, o_ref):
    n = r_ref.shape[0]
    for qc in range(GRID_W):
        block = jnp.dot(r_ref[...], sel_ref[qc], preferred_element_type=F32, precision=lax.Precision.HIGHEST)
        o_ref[pl.ds(qc, n, stride=GRID_W), :] = block + mask_ref[qc]


def _rpb_toeplitz(rpb):
    depth, heads, n_dr, n_dc = rpb.shape
    qc = np.arange(GRID_W)[:, None]
    kc = (np.arange(LANES) % GRID_W)[None, :]
    cs = np.clip(qc - WIN_COLS // 2, 0, GRID_W - WIN_COLS)
    col_valid = (kc >= cs) & (kc < cs + WIN_COLS)
    dc = kc - qc + WIN_COLS - 1
    select = (np.arange(DC_PAD)[None, :, None] == dc[:, None, :]) & col_valid[:, None, :]
    mask = np.where(col_valid, 0.0, MASK_VALUE)[:, None, :]
    r = jnp.pad(rpb.astype(F32), ((0, 0), (0, 0), (0, DR_PAD - n_dr), (0, DC_PAD - n_dc)))
    n = depth * heads * DR_PAD
    const3 = lambda: (0, 0, 0)
    out = pl.pallas_call(
        _toeplitz_kernel,
        grid=(),
        in_specs=[pl.BlockSpec((n, DC_PAD), lambda: (0, 0)),
                  pl.BlockSpec((GRID_W, DC_PAD, LANES), const3), pl.BlockSpec((GRID_W, 1, LANES), const3)],
        out_specs=pl.BlockSpec((n * GRID_W, LANES), lambda: (0, 0)),
        out_shape=jax.ShapeDtypeStruct((n * GRID_W, LANES), F32),
        compiler_params=pltpu.CompilerParams(vmem_limit_bytes=VMEM_LIMIT_BYTES),
        name="rpb_toeplitz",
    )(r.reshape(n, DC_PAD), jnp.asarray(select, F32), jnp.asarray(mask, F32))
    return out.reshape(depth * heads, DR_PAD, GRID_W, LANES)


Q_CHUNK_ROWS = 4
WINDOW_CHUNKS = 3
N_Q_CHUNKS = Q_ROWS // Q_CHUNK_ROWS
assert Q_CHUNK_ROWS == K_CHUNK_ROWS and WINDOW_CHUNKS * K_CHUNK_ROWS >= Q_CHUNK_ROWS + WIN_ROWS - 1
assert N_Q_CHUNKS == 2 and N_K_CHUNKS == 4


def _key_row_start(rb, rows):
    return int(np.clip(rb * Q_ROWS - (K_ROWS - Q_ROWS) // 2, 0, rows - K_ROWS))


def _window_uses_last_chunk(rb, c, n_rb):
    if c == 0:
        return rb == n_rb - 1
    return rb != 0


def _window_key_rows(rb, c, rows):
    var = N_K_CHUNKS - 1 if _window_uses_last_chunk(rb, c, rows // Q_ROWS) else 0
    k0 = _key_row_start(rb, rows)
    return [k0 + K_CHUNK_ROWS * j + i for j in (var, 1, 2) for i in range(K_CHUNK_ROWS)]


def _row_window(qr, rows):
    kh = min(WIN_ROWS, rows)
    rs = int(np.clip(qr - kh // 2, 0, rows - kh))
    return rs, rs + kh


def _check_windows(rows):
    def relative(rb):
        base = rb * Q_ROWS
        return [([kr - base for kr in _window_key_rows(rb, c, rows)],
                 [tuple(r - base for r in _row_window(base + c * Q_CHUNK_ROWS + qi, rows))
                  for qi in range(Q_CHUNK_ROWS)]) for c in range(N_Q_CHUNKS)]

    n_rb = rows // Q_ROWS
    for rb in range(n_rb):
        assert rb in (0, n_rb - 1) or relative(rb) == relative(1), rb
        for c in range(N_Q_CHUNKS):
            have = set(_window_key_rows(rb, c, rows))
            for qi in range(Q_CHUNK_ROWS):
                lo, hi = _row_window(rb * Q_ROWS + c * Q_CHUNK_ROWS + qi, rows)
                assert set(range(lo, hi)) <= have, (rb, c, qi)


def _build_bias_tables(t_ref, bias_scr, rows):
    n_rb = rows // Q_ROWS
    left = lax.broadcasted_iota(jnp.int32, (GRID_W, LANES), 1) < GRID_W
    masked = jnp.full((GRID_W, LANES), MASK_VALUE, F32)
    for variant, rb in enumerate((0, 1, n_rb - 1)):
        for c in range(N_Q_CHUNKS):
            key_rows = _window_key_rows(rb, c, rows)
            for h in range(HEADS_PER_STEP):
                for qi in range(Q_CHUNK_ROWS):
                    qr = rb * Q_ROWS + c * Q_CHUNK_ROWS + qi
                    lo, hi = _row_window(qr, rows)
                    for p in range(len(key_rows) // 2):
                        pair = [t_ref[h, kr - qr + WIN_ROWS - 1] if lo <= kr < hi else None
                                for kr in key_rows[2 * p:2 * p + 2]]
                        if pair[0] is None and pair[1] is None:
                            block = masked
                        else:
                            block = jnp.where(left, masked if pair[0] is None else pair[0],
                                              masked if pair[1] is None else pair[1])
                        bias_scr[variant, c, h, qi * GRID_W:(qi + 1) * GRID_W, p * LANES:(p + 1) * LANES] = block


def _na_kernel(q_ref, k0, k1, k2, k3, v0, v1, v2, v3, kc_ref, vc_ref, t_ref, o_ref, bias_scr, s_scr, *, rows):
    n_rb = rows // Q_ROWS
    b, rb = pl.program_id(1), pl.program_id(2)

    @pl.when((b == 0) & (rb == 0))
    def _():
        _build_bias_tables(t_ref, bias_scr, rows)

    variant = jnp.where(rb == 0, 0, jnp.where(rb == n_rb - 1, 2, 1))
    tq = Q_CHUNK_ROWS * GRID_W
    tk = K_CHUNK_ROWS * GRID_W
    units = []
    for c in range(N_Q_CHUNKS):
        use_last = (rb == n_rb - 1) if c == 0 else (rb != 0)
        window = [(jnp.where(use_last, k3[0], k0[0]), jnp.where(use_last, v3[0], v0[0])),
                  (k1[0], v1[0]), (k2[0], v2[0])]
        q = q_ref[0, c * tq:(c + 1) * tq, :]
        for h in range(HEADS_PER_STEP):
            lat = [(k, v, bias_scr[variant, c, h, :, j * tk:(j + 1) * tk]) for j, (k, v) in enumerate(window)]
            units.append((q, h, lat + [(kc_ref[0], vc_ref[0], None)], s_scr.at[c, h]))
    outs = _attention_units(units)
    for c in range(N_Q_CHUNKS):
        o = _merge_heads(outs[c * HEADS_PER_STEP:(c + 1) * HEADS_PER_STEP])
        o_ref[0, c * tq:(c + 1) * tq, :] = o.astype(BF16)


def _neighborhood_attention(q, k, v, kc, vc, toeplitz, layer):
    bsz, length, _ = q.shape
    rows = length // GRID_W
    n_rb = rows // Q_ROWS
    assert rows % Q_ROWS == 0 and rows >= K_ROWS + Q_ROWS and n_rb >= 3
    _check_windows(rows)
    n_ctx = kc.shape[1]
    tq = Q_ROWS * GRID_W
    tk = K_CHUNK_ROWS * GRID_W
    assert n_ctx == tk
    max_chunk = (rows - K_ROWS) // K_CHUNK_ROWS
    half = (K_ROWS - Q_ROWS) // 2 // K_CHUNK_ROWS

    def kv_spec(j):
        def index(hp, b, rb):
            start = jnp.clip(rb * (Q_ROWS // K_CHUNK_ROWS) - half, 0, max_chunk)
            return (b, start + j, hp)
        return pl.BlockSpec((1, tk, LANES), index)

    ctx_spec = pl.BlockSpec((1, n_ctx, LANES), lambda hp, b, rb: (b, 0, hp))
    q_spec = pl.BlockSpec((1, tq, LANES), lambda hp, b, rb: (b, rb, hp))
    window = WINDOW_CHUNKS * K_CHUNK_ROWS * GRID_W
    return pl.pallas_call(
        functools.partial(_na_kernel, rows=rows),
        grid=(NA_HEADS // HEADS_PER_STEP, bsz, n_rb),
        in_specs=([q_spec] + [kv_spec(j) for j in range(N_K_CHUNKS)] + [kv_spec(j) for j in range(N_K_CHUNKS)]
                  + [ctx_spec, ctx_spec,
                     pl.BlockSpec((HEADS_PER_STEP, DR_PAD, GRID_W, LANES),
                                  lambda hp, b, rb: (layer * (NA_HEADS // HEADS_PER_STEP) + hp, 0, 0, 0))]),
        out_specs=q_spec,
        out_shape=jax.ShapeDtypeStruct((bsz, length, NA_WIDTH), BF16),
        scratch_shapes=[pltpu.VMEM((3, N_Q_CHUNKS, HEADS_PER_STEP, Q_CHUNK_ROWS * GRID_W, window), F32),
                        pltpu.VMEM((N_Q_CHUNKS, HEADS_PER_STEP, Q_CHUNK_ROWS * GRID_W, window + n_ctx), F32)],
        compiler_params=_cparams(3),
        name="neighborhood_attention",
    )(q, *([k] * N_K_CHUNKS), *([v] * N_K_CHUNKS), kc, vc, toeplitz)


def _ctx_attn_kernel(q_ref, k_ref, v_ref, o_ref, s_scr):
    tiles = [(k_ref[0], v_ref[0], None)]
    outs = _attention_units([(q_ref[0], h, tiles, s_scr.at[h]) for h in range(HEADS_PER_STEP)])
    o_ref[0] = _merge_heads(outs).astype(BF16)


def _context_attention(qc, kc, vc):
    bsz, n_ctx, _ = qc.shape
    spec = pl.BlockSpec((1, n_ctx, LANES), lambda b, hp: (b, 0, hp))
    return pl.pallas_call(
        _ctx_attn_kernel,
        grid=(bsz, NA_HEADS // HEADS_PER_STEP),
        in_specs=[spec, spec, spec],
        out_specs=spec,
        out_shape=jax.ShapeDtypeStruct((bsz, n_ctx, NA_WIDTH), BF16),
        scratch_shapes=[pltpu.VMEM((HEADS_PER_STEP, n_ctx, n_ctx), F32)],
        compiler_params=_cparams(2),
        name="context_attention",
    )(qc, kc, vc)


def _dft_cos_sin(n):
    ang = 2.0 * np.pi * np.outer(np.arange(n), np.arange(n)) / n
    return np.cos(ang), np.sin(ang)


def _bf16_table(a):
    return jnp.asarray(a, F32).astype(BF16)


def _channel_dft_matrix(n_groups=F_GROUPS):
    c, s = _dft_cos_sin(F_GROUP_DIM)
    scale = F_GROUP_DIM ** -0.5
    eye = np.eye(n_groups)
    return np.concatenate([np.kron(eye, c), np.kron(eye, s)], axis=1) * scale


def _fft_stage1_kernel(f_ref, perm_ref, w1_ref, cs_ref, sc_ref, tc_ref, ts_ref, zr_ref, zi_ref, *, n_slow, nt):
    x = f_ref[0].reshape(n_slow * nt, F_WIDTH).astype(BF16)
    x = _dot(perm_ref[...], x).astype(BF16)
    ab = [_dot(x[:, p * LANES:(p + 1) * LANES], w1_ref[...]).astype(BF16) for p in range(F_WIDTH // LANES)]
    a_all = jnp.concatenate([blk[:, :LANES] for blk in ab], axis=-1)
    b_all = jnp.concatenate([blk[:, LANES:] for blk in ab], axis=-1)
    for t in range(nt):
        a = a_all[t * n_slow:(t + 1) * n_slow]
        b = b_all[t * n_slow:(t + 1) * n_slow]
        z = _dot(cs_ref[...], a) + _dot(sc_ref[...], b)
        zr, zi = z[:n_slow], z[n_slow:]
        c, s = tc_ref[t], ts_ref[t]
        zr_ref[0, t] = zr * c - zi * s
        zi_ref[0, t] = zr * s + zi * c


def _fft_stage2_kernel(zr_ref, zi_ref, f_ref, y_ref, *, n_fast, kb):
    rhs = jnp.concatenate([zr_ref[0].reshape(n_fast * kb, F_WIDTH).astype(BF16),
                           zi_ref[0].reshape(n_fast * kb, F_WIDTH).astype(BF16)], axis=0)
    y_ref[0] = _dot(f_ref[...], rhs).reshape(n_fast, kb, F_WIDTH)


def _fourier_positions(f, n_slow, n_fast):
    bsz, n, _ = f.shape
    assert n == n_slow * n_fast
    nt = SUBLANES
    kb = SUBLANES
    w1 = _bf16_table(_channel_dft_matrix(LANES // F_GROUP_DIM))
    perm = _bf16_table(np.eye(n_slow * nt).reshape(n_slow, nt, n_slow * nt).transpose(1, 0, 2)
                       .reshape(n_slow * nt, n_slow * nt))
    c1, s1 = _dft_cos_sin(n_slow)
    sc1 = n_slow ** -0.5
    cs = _bf16_table(np.concatenate([c1, s1], axis=0) * sc1)
    sc = _bf16_table(np.concatenate([-s1, c1], axis=0) * sc1)
    tw = 2.0 * np.pi * np.outer(np.arange(n_fast), np.arange(n_slow)) / n
    tc = jnp.asarray(np.cos(tw)[:, :, None], F32)
    ts = jnp.asarray(np.sin(tw)[:, :, None], F32)
    z_shape = jax.ShapeDtypeStruct((bsz, n_fast, n_slow, F_WIDTH), F32)
    z_spec = pl.BlockSpec((1, nt, n_slow, F_WIDTH), lambda b, j: (b, j, 0, 0))
    const2 = lambda b, j: (0, 0)
    tw_spec = pl.BlockSpec((nt, n_slow, 1), lambda b, j: (j, 0, 0))
    zr, zi = pl.pallas_call(
        functools.partial(_fft_stage1_kernel, n_slow=n_slow, nt=nt),
        grid=(bsz, n_fast // nt),
        in_specs=[pl.BlockSpec((1, n_slow, nt, F_WIDTH), lambda b, j: (b, 0, j, 0)),
                  pl.BlockSpec(perm.shape, const2),
                  pl.BlockSpec(w1.shape, const2), pl.BlockSpec(cs.shape, const2), pl.BlockSpec(sc.shape, const2),
                  tw_spec, tw_spec],
        out_specs=[z_spec, z_spec],
        out_shape=[z_shape, z_shape],
        compiler_params=_cparams(2),
        name="fnet_stage1",
    )(f.reshape(bsz, n_slow, n_fast, F_WIDTH), perm, w1, cs, sc, tc, ts)

    c2, s2 = _dft_cos_sin(n_fast)
    sc2 = n_fast ** -0.5
    eye = np.eye(kb)
    f2 = _bf16_table(np.concatenate([np.kron(c2, eye), np.kron(-s2, eye)], axis=1) * sc2)
    blk = pl.BlockSpec((1, n_fast, kb, F_WIDTH), lambda b, j: (b, 0, j, 0))
    y = pl.pallas_call(
        functools.partial(_fft_stage2_kernel, n_fast=n_fast, kb=kb),
        grid=(bsz, n_slow // kb),
        in_specs=[blk, blk, pl.BlockSpec(f2.shape, const2)],
        out_specs=blk,
        out_shape=jax.ShapeDtypeStruct((bsz, n_fast, n_slow, F_WIDTH), F32),
        compiler_params=_cparams(2),
        name="fnet_stage2",
    )(zr, zi, f2)
    return y.reshape(bsz, n, F_WIDTH)


def _second_largest_sum(a, b, c, d):
    mab, nab = jnp.maximum(a, b), jnp.minimum(a, b)
    mcd, ncd = jnp.maximum(c, d), jnp.minimum(c, d)
    return jnp.maximum(mab, mcd) + jnp.maximum(jnp.minimum(mab, mcd), jnp.maximum(nab, ncd))


def _selected_group(sb_rows):
    epg = EXPERTS_PER_GROUP
    g_score = [_second_largest_sum(*sb_rows[g * epg:(g + 1) * epg]) for g in range(N_GROUPS)]
    best = functools.reduce(jnp.maximum, g_score)
    group = jnp.full_like(best, float(N_GROUPS - 1))
    for g in range(N_GROUPS - 2, -1, -1):
        group = jnp.where(g_score[g] == best, float(g), group)
    return group


def _top2_gates(cand_s, cand_sb):
    n = len(cand_s)
    w = []
    for j in range(n):
        rank = jnp.zeros_like(cand_sb[j])
        for i in range(n):
            if i == j:
                continue
            ahead = (cand_sb[i] > cand_sb[j]) | ((cand_sb[i] == cand_sb[j]) & (i < j))
            rank = rank + jnp.where(ahead, 1.0, 0.0)
        w.append(jnp.where(rank < 2.0, cand_s[j], 0.0))
    total = functools.reduce(jnp.add, w)
    return [wj / total for wj in w]


def _out_kernel(a_ref, y_ref, x_ref, m_ref, wo_ref, bd_ref, g_ref, b_ref, wr_ref, rb_ref,
                x1_ref, h2_ref, grp_ref):
    y2 = _dot(y_ref[0].astype(BF16), bd_ref[...]).astype(BF16)
    o = _dot(a_ref[0], wo_ref[:NA_WIDTH, :]) + _dot(y2, wo_ref[NA_WIDTH:, :])
    z = DEEPNORM_ALPHA * x_ref[0] + m_ref[0, 2:3, :] * o
    x1 = _layer_norm(z) * g_ref[...] + b_ref[...]
    x1_ref[0] = x1
    h2 = _layer_norm(x1) * (1.0 + m_ref[0, 4:5, :]) + m_ref[0, 3:4, :]
    _to_slabs(h2_ref, h2)
    sb = jax.nn.sigmoid(_dot_nt(wr_ref[...], h2.astype(BF16))) + rb_ref[...]
    group = _selected_group([sb[e:e + 1] for e in range(N_EXPERTS)])
    grp_ref[0] = jnp.broadcast_to(group, grp_ref.shape[1:])


def _out_projection(attn, yf, x, m, w_out_bf16, w_four_bd, ln_g, ln_b, w_router_t, router_bias, tm):
    bsz, length, _ = x.shape
    tiles_per_batch = length // tm
    row = lambda b, i: (b, i, 0)
    const2 = lambda b, i: (0, 0)
    return pl.pallas_call(
        _out_kernel,
        grid=(bsz, length // tm),
        in_specs=[pl.BlockSpec((1, tm, NA_WIDTH), row), pl.BlockSpec((1, tm, F_WIDTH), row),
                  pl.BlockSpec((1, tm, D_MODEL), row),
                  pl.BlockSpec((1, N_MOD, D_MODEL), lambda b, i: (b, 0, 0)),
                  pl.BlockSpec((NA_WIDTH + F_WIDTH, D_MODEL), const2),
                  pl.BlockSpec((F_WIDTH, F_WIDTH), const2),
                  pl.BlockSpec((1, D_MODEL), const2), pl.BlockSpec((1, D_MODEL), const2),
                  pl.BlockSpec((N_EXPERTS, D_MODEL), const2), pl.BlockSpec((N_EXPERTS, 1), const2)],
        out_specs=[pl.BlockSpec((1, tm, D_MODEL), row),
                   pl.BlockSpec((tm * SLAB_ROWS, LANES), lambda b, i: (b * tiles_per_batch + i, 0)),
                   pl.BlockSpec((1, SUBLANES, tm), lambda b, i: (b, 0, i))],
        out_shape=[jax.ShapeDtypeStruct((bsz, length, D_MODEL), F32),
                   jax.ShapeDtypeStruct((bsz * length * SLAB_ROWS, LANES), F32),
                   jax.ShapeDtypeStruct((bsz, SUBLANES, length), F32)],
        compiler_params=_cparams(2),
        name="out_proj_norm_route",
    )(attn, yf, x, m, w_out_bf16, w_four_bd, ln_g.reshape(1, D_MODEL), ln_b.reshape(1, D_MODEL),
      w_router_t, router_bias.reshape(N_EXPERTS, 1))


def _group_plan(group, tm):
    n = group.shape[0]
    n_slots = n + N_GROUPS * tm
    onehot = (group[:, None] == jnp.arange(N_GROUPS, dtype=jnp.int32)[None, :]).astype(jnp.int32)
    csum = jnp.cumsum(onehot, axis=0)
    rank = jnp.sum(onehot * csum, axis=1) - 1
    count = csum[-1]
    padded = ((count + tm - 1) // tm) * tm
    end = jnp.cumsum(padded)
    start = end - padded
    slot = jnp.sum(onehot * start[None, :], axis=1) + rank
    tile_start = jnp.arange(n_slots // tm, dtype=jnp.int32) * tm
    tile_group = jnp.minimum(jnp.sum((end[None, :] <= tile_start[:, None]).astype(jnp.int32), axis=1),
                             N_GROUPS - 1)
    tile_rows = jnp.clip((start + count)[tile_group] - tile_start, 0, tm)
    return slot, tile_group, tile_rows, start, start + count, end


ROW_DMA_UNROLL = 8


def _slab(ref, token):
    return ref.at[pl.ds(pl.multiple_of(token * SLAB_ROWS, SLAB_ROWS), SLAB_ROWS), :]


def _row_scatter(idx_ref, base, src_ref, dst_hbm, sem, n_rows, wait, inverse_ref=None, src_base=0):
    def one(r, priority):
        if wait:
            pltpu.make_async_copy(_slab(src_ref, 0), _slab(dst_hbm, 0), sem).wait()
        else:
            idx = idx_ref[base + r]
            if inverse_ref is not None:
                inverse_ref[idx] = base + r
            pltpu.make_async_copy(_slab(src_ref, src_base + r), _slab(dst_hbm, idx), sem).start(priority=priority)

    def chunk(j, carry):
        for u in range(ROW_DMA_UNROLL):
            one(j * ROW_DMA_UNROLL + u, u % 2)
        return carry

    def single(r, carry):
        one(r, 0)
        return carry

    n_chunks = n_rows // ROW_DMA_UNROLL
    lax.fori_loop(0, n_chunks, chunk, 0)
    lax.fori_loop(n_chunks * ROW_DMA_UNROLL, n_rows, single, 0)


def _to_slots_kernel(slot_ref, start_ref, rows_end_ref, end_ref, h_ref, o_hbm, dst_ref, zbuf, sem,
                     *, tm, tm_slots):
    step = pl.program_id(0)
    n_slots = dst_ref.shape[0]

    @pl.when(step == 0)
    def _():
        zbuf[...] = jnp.zeros_like(zbuf)

        def fill(slot0):
            rows = tm_slots * SLAB_ROWS
            return pltpu.make_async_copy(zbuf, o_hbm.at[pl.ds(pl.multiple_of(slot0 * SLAB_ROWS, rows), rows), :], sem)

        for g in range(N_GROUPS):
            @pl.when(end_ref[g] > start_ref[g])
            def _():
                fill(end_ref[g] - tm_slots).start()
        for g in range(N_GROUPS):
            @pl.when(end_ref[g] > start_ref[g])
            def _():
                fill(end_ref[g] - tm_slots).wait()

        def fill_unused(t, carry):
            fill(t * tm_slots).start()
            fill(t * tm_slots).wait()
            return carry
        lax.fori_loop(end_ref[N_GROUPS - 1] // tm_slots, n_slots // tm_slots, fill_unused, 0)

        def clear(p, carry):
            dst_ref[p] = 0
            return carry
        for g in range(N_GROUPS):
            lax.fori_loop(rows_end_ref[g], end_ref[g], clear, 0)
        lax.fori_loop(end_ref[N_GROUPS - 1], n_slots, clear, 0)

    _row_scatter(slot_ref, step * tm, h_ref, o_hbm, sem, tm, wait=False, inverse_ref=dst_ref)
    _row_scatter(slot_ref, step * tm, h_ref, o_hbm, sem, tm, wait=True)


def _to_slots(h2_slabs, slot, start, rows_end, end, tm, tm_slots):
    n = h2_slabs.shape[0] // SLAB_ROWS
    n_slots = n + N_GROUPS * tm_slots
    grid_spec = pltpu.PrefetchScalarGridSpec(
        num_scalar_prefetch=4,
        grid=(n // tm,),
        in_specs=[pl.BlockSpec((tm * SLAB_ROWS, LANES), lambda i, *_: (i, 0))],
        out_specs=[pl.BlockSpec(memory_space=pl.ANY), pl.BlockSpec(memory_space=pltpu.SMEM)],
        scratch_shapes=[pltpu.VMEM((tm_slots * SLAB_ROWS, LANES), F32), pltpu.SemaphoreType.DMA(())],
    )
    return pl.pallas_call(
        functools.partial(_to_slots_kernel, tm=tm, tm_slots=tm_slots),
        grid_spec=grid_spec,
        out_shape=[jax.ShapeDtypeStruct((n_slots * SLAB_ROWS, LANES), F32),
                   jax.ShapeDtypeStruct((n_slots,), jnp.int32)],
        compiler_params=_cparams(1),
        name="moe_rows_to_slots",
    )(slot, start, rows_end, end, h2_slabs)


def _moe_group_kernel(dst_ref, tile_group_ref, tile_rows_ref, n_used_ref, h_ref, wg_ref, wu_ref, wd_ref, wr_ref,
                      rb_ref, y_hbm, ybuf, sems, *, tm, n_tiles):
    del tile_group_ref
    step = pl.program_id(0)
    n_used = n_used_ref[0]
    buf = step % 2

    def scatter(tile, wait):
        _row_scatter(dst_ref, tile * tm, ybuf.at[tile % 2], y_hbm, sems.at[tile % 2], tile_rows_ref[tile], wait)

    @pl.when((step >= 2) & (step < n_used))
    def _():
        scatter(step - 2, wait=True)

    @pl.when(step < n_used)
    def _():
        h = _from_slabs(h_ref, tm).astype(BF16)
        s = jax.nn.sigmoid(_dot(h, wr_ref[0]))
        sb = s + rb_ref[0]
        epg = EXPERTS_PER_GROUP
        gates = _top2_gates([s[:, j:j + 1] for j in range(epg)], [sb[:, j:j + 1] for j in range(epg)])
        acc = None
        gate_up = [(_dot(h, wg_ref[0]), _dot(h, wu_ref[0]))]
        for e in range(epg):
            if e + 1 < epg:
                gate_up.append((_dot(h, wg_ref[e + 1]), _dot(h, wu_ref[e + 1])))
            gate, up = gate_up[e]
            hid = (gate * jax.nn.sigmoid(gate)) * up * gates[e]
            y = _dot(hid.astype(BF16), wd_ref[e])
            acc = y if acc is None else acc + y
        _to_slabs(ybuf.at[buf], acc)
        scatter(step, wait=False)

    @pl.when(step == n_tiles - 1)
    def _():
        @pl.when(n_used >= 2)
        def _():
            scatter(n_used - 2, wait=True)

        @pl.when(n_used >= 1)
        def _():
            scatter(n_used - 1, wait=True)


def _moe_experts(h_slots, dst, tile_group, tile_rows, n_used, n_tokens, layer, w_gate, w_up, w_down,
                 w_router_grp, router_bias_grp, tm):
    n_slots = h_slots.shape[0] // SLAB_ROWS
    n_tiles = n_slots // tm
    epg = EXPERTS_PER_GROUP
    by_group = lambda i, dst_ref, tg_ref, tr_ref, nu_ref: (tg_ref[i], 0, 0)
    by_layer_group = lambda i, dst_ref, tg_ref, tr_ref, nu_ref: (layer * N_GROUPS + tg_ref[i], 0, 0)
    grid_spec = pltpu.PrefetchScalarGridSpec(
        num_scalar_prefetch=4,
        grid=(n_tiles,),
        in_specs=[pl.BlockSpec((tm * SLAB_ROWS, LANES),
                               lambda i, dst_ref, tg_ref, tr_ref, nu_ref: (jnp.minimum(i, nu_ref[0] - 1), 0)),
                  pl.BlockSpec((epg, D_MODEL, D_EXPERT), by_layer_group),
                  pl.BlockSpec((epg, D_MODEL, D_EXPERT), by_layer_group),
                  pl.BlockSpec((epg, D_EXPERT, D_MODEL), by_layer_group),
                  pl.BlockSpec((1, D_MODEL, LANES), by_group),
                  pl.BlockSpec((1, 1, LANES), by_group)],
        out_specs=pl.BlockSpec(memory_space=pl.ANY),
        scratch_shapes=[pltpu.VMEM((2, tm * SLAB_ROWS, LANES), F32), pltpu.SemaphoreType.DMA((2,))],
    )
    return pl.pallas_call(
        functools.partial(_moe_group_kernel, tm=tm, n_tiles=n_tiles),
        grid_spec=grid_spec,
        out_shape=jax.ShapeDtypeStruct((n_tokens * SLAB_ROWS, LANES), F32),
        compiler_params=_cparams(1),
        name="moe_group_experts",
    )(dst, tile_group, tile_rows, n_used, h_slots, w_gate, w_up, w_down, w_router_grp, router_bias_grp)


def _residual_norm_kernel(y_ref, x1_ref, m_ref, g_ref, b_ref, o_ref):
    y = _from_slabs(y_ref, x1_ref.shape[0])
    z = DEEPNORM_ALPHA * x1_ref[...] + m_ref[0, 5:6, :] * y
    o_ref[...] = _layer_norm(z) * g_ref[...] + b_ref[...]


def _residual_norm(y_slabs, x1, m, ln_g, ln_b, tm):
    bsz, length, _ = x1.shape
    n = bsz * length
    tiles_per_batch = length // tm
    const2 = lambda i: (0, 0)
    row = pl.BlockSpec((tm, D_MODEL), lambda i: (i, 0))
    out = pl.pallas_call(
        _residual_norm_kernel,
        grid=(n // tm,),
        in_specs=[pl.BlockSpec((tm * SLAB_ROWS, LANES), lambda i: (i, 0)), row,
                  pl.BlockSpec((1, N_MOD, D_MODEL), lambda i: (i // tiles_per_batch, 0, 0)),
                  pl.BlockSpec((1, D_MODEL), const2), pl.BlockSpec((1, D_MODEL), const2)],
        out_specs=row,
        out_shape=jax.ShapeDtypeStruct((n, D_MODEL), F32),
        compiler_params=_cparams(1),
        name="moe_residual_norm",
    )(y_slabs, x1.reshape(n, D_MODEL), m, ln_g.reshape(1, D_MODEL), ln_b.reshape(1, D_MODEL))
    return out.reshape(bsz, length, D_MODEL)


def _grouped_moe(h2, group_rows, x1, m, layer, w_gate, w_up, w_down, w_router_grp, router_bias_grp, ln_g, ln_b,
                 tm_experts, tm_tokens):
    bsz, length, _ = x1.shape
    n = bsz * length
    group = group_rows[:, 0, :].reshape(n).astype(jnp.int32)
    slot, tile_group, tile_rows, start, rows_end, end = _group_plan(group, tm_experts)
    h_slots, dst = _to_slots(h2, slot, start, rows_end, end, tm_tokens, tm_experts)
    n_used = (end[N_GROUPS - 1:] // tm_experts).astype(jnp.int32)
    y = _moe_experts(h_slots, dst, tile_group, tile_rows, n_used, n, layer, w_gate, w_up, w_down, w_router_grp,
                     router_bias_grp, tm_experts)
    return _residual_norm(y, x1, m, ln_g, ln_b, tm_tokens)


def _ctx_fourier_kernel(f_ref, w1_ref, c_ref, s_ref, y_ref):
    ab = _dot(f_ref[0].astype(BF16), w1_ref[...]).astype(BF16)
    y = _dot(c_ref[...], ab[:, :F_WIDTH]) + _dot(s_ref[...], ab[:, F_WIDTH:])
    y_ref[0] = y.astype(BF16)


def _context_fourier(fc):
    bsz, n, _ = fc.shape
    w1 = _bf16_table(_channel_dft_matrix())
    c, s = _dft_cos_sin(n)
    cm = _bf16_table(c * n ** -0.5)
    sm = _bf16_table(-s * n ** -0.5)
    const2 = lambda b: (0, 0)
    blk = pl.BlockSpec((1, n, F_WIDTH), lambda b: (b, 0, 0))
    return pl.pallas_call(
        _ctx_fourier_kernel,
        grid=(bsz,),
        in_specs=[blk, pl.BlockSpec(w1.shape, const2), pl.BlockSpec(cm.shape, const2),
                  pl.BlockSpec(sm.shape, const2)],
        out_specs=blk,
        out_shape=jax.ShapeDtypeStruct((bsz, n, F_WIDTH), BF16),
        compiler_params=_cparams(1),
        name="context_fnet",
    )(fc, w1, cm, sm)


def _block_diag(w):
    g, c, _ = w.shape
    eye = jnp.eye(g, dtype=w.dtype)
    return (eye[:, None, :, None] * w[:, :, None, :]).reshape(g * c, g * c)


def kernel(x, c, ctx, c_ctx, w_mod, b_mod, w_in, rpb, w_four, w_out, ln1_g, ln1_b, ln2_g, ln2_b,
           w_router, router_bias, w_gate, w_up, w_down):
    bsz, length, _ = x.shape
    n_ctx = ctx.shape[1]
    rows = length // GRID_W

    cvec = jnp.concatenate([c, c_ctx[None, :], jnp.zeros((8 - bsz - 1, D_MODEL), F32)], axis=0)
    mods = _modulation(cvec, w_mod, b_mod)
    w_router_t = w_router.T.astype(BF16)
    toeplitz = _rpb_toeplitz(rpb)
    lane_pad = LANES - EXPERTS_PER_GROUP
    w_router_grp = jnp.pad(w_router.reshape(D_MODEL, N_GROUPS, EXPERTS_PER_GROUP).transpose(1, 0, 2),
                           ((0, 0), (0, 0), (0, lane_pad))).astype(BF16)
    router_bias_grp = jnp.pad(router_bias.astype(F32).reshape(N_GROUPS, 1, EXPERTS_PER_GROUP),
                              ((0, 0), (0, 0), (0, lane_pad)))

    wg = w_gate.astype(BF16).reshape(DEPTH * N_EXPERTS, D_MODEL, D_EXPERT)
    wu = w_up.astype(BF16).reshape(DEPTH * N_EXPERTS, D_MODEL, D_EXPERT)
    wd = w_down.astype(BF16).reshape(DEPTH * N_EXPERTS, D_EXPERT, D_MODEL)

    xc = ctx
    for i in range(DEPTH):
        last = i == DEPTH - 1
        m = mods[i, :bsz].reshape(bsz, N_MOD, D_MODEL)
        mc = jnp.broadcast_to(mods[i, bsz].reshape(1, N_MOD, D_MODEL), (bsz, N_MOD, D_MODEL))
        w_in_b = w_in[i].astype(BF16)
        w_out_b = w_out[i].astype(BF16)
        w_four_bd = _block_diag(w_four[i]).astype(BF16)

        q, k, v, f = _in_projection(x, m, w_in_b, tm=512)
        qc, kc, vc, fc = _in_projection(xc, mc, w_in_b, tm=n_ctx)

        attn = _neighborhood_attention(q, k, v, kc, vc, toeplitz, i)
        yf = _fourier_positions(f, n_slow=rows, n_fast=GRID_W)
        x1, h2, grp = _out_projection(attn, yf, x, m, w_out_b, w_four_bd, ln1_g[i], ln1_b[i],
                                      w_router_t, router_bias, tm=512)
        x = _grouped_moe(h2, grp, x1, m, i, wg, wu, wd, w_router_grp, router_bias_grp, ln2_g[i], ln2_b[i],
                         tm_experts=512, tm_tokens=512)
        if last:
            break

        attn_c = _context_attention(qc, kc, vc)
        yc = _context_fourier(fc)
        xc1, h2c, grp_c = _out_projection(attn_c, yc, xc, mc, w_out_b, w_four_bd, ln1_g[i], ln1_b[i],
                                          w_router_t, router_bias, tm=n_ctx)
        xc = _grouped_moe(h2c, grp_c, xc1, mc, i, wg, wu, wd, w_router_grp, router_bias_grp, ln2_g[i], ln2_b[i],
                          tm_experts=128, tm_tokens=n_ctx)
    return x
```

```python
import functools
import math

import numpy as np
import jax
import jax.numpy as jnp
from jax import lax
from jax.experimental import pallas as pl
from jax.experimental.pallas import tpu as pltpu

D_MODEL = 1024
DEPTH = 2
GRID_W = 64
NA_HEADS = 8
HEAD_DIM = 64
NA_WIDTH = NA_HEADS * HEAD_DIM
WIN_ROWS = 8
WIN_COLS = 16
F_GROUPS = 8
F_GROUP_DIM = 64
F_WIDTH = F_GROUPS * F_GROUP_DIM
IN_WIDTH = 3 * NA_WIDTH + F_WIDTH
N_EXPERTS = 16
N_GROUPS = 4
EXPERTS_PER_GROUP = N_EXPERTS // N_GROUPS
D_EXPERT = 256
N_MOD = 6
DEEPNORM_ALPHA = (2.0 * DEPTH) ** 0.25
LN_EPS = 1e-6

F32 = jnp.float32
BF16 = jnp.bfloat16

V7X_VMEM_BYTES = 64 * 1024 * 1024
VMEM_LIMIT_BYTES = (V7X_VMEM_BYTES * 3) // 4
LANES = 128
SUBLANES = 8
HEADS_PER_STEP = LANES // HEAD_DIM
assert HEADS_PER_STEP == 2
MASK_VALUE = -1e30

Q_ROWS = 8
K_ROWS = 16
K_CHUNK_ROWS = 4
N_K_CHUNKS = K_ROWS // K_CHUNK_ROWS


def _cparams(n_grid_dims):
    return pltpu.CompilerParams(dimension_semantics=("arbitrary",) * n_grid_dims,
                                vmem_limit_bytes=VMEM_LIMIT_BYTES)


def _layer_norm(x):
    mu = jnp.mean(x, axis=-1, keepdims=True)
    xc = x - mu
    var = jnp.mean(xc * xc, axis=-1, keepdims=True)
    return xc * lax.rsqrt(var + LN_EPS)


SLAB_ROWS = D_MODEL // LANES
assert SLAB_ROWS == SUBLANES


def _to_slabs(ref, x):
    for j in range(SLAB_ROWS):
        ref[pl.ds(j, x.shape[0], stride=SLAB_ROWS), :] = x[:, j * LANES:(j + 1) * LANES]


def _from_slabs(ref, n_tokens):
    return jnp.concatenate([ref[pl.ds(j, n_tokens, stride=SLAB_ROWS), :] for j in range(SLAB_ROWS)], axis=-1)


def _dot(a, b):
    return jnp.dot(a, b, preferred_element_type=F32)


def _dot_nt(a, b):
    return lax.dot_general(a, b, (((1,), (1,)), ((), ())), preferred_element_type=F32)


def _mod_kernel(c_ref, w_ref, b_ref, o_ref):
    c = c_ref[...]
    a = c * jax.nn.sigmoid(c)
    o_ref[0] = jnp.dot(a, w_ref[0], preferred_element_type=F32, precision=lax.Precision.HIGHEST) + b_ref[0]


def _modulation(cvec, w_mod, b_mod):
    n_col_blocks = 4
    wc = (N_MOD * D_MODEL) // n_col_blocks
    rows = cvec.shape[0]
    return pl.pallas_call(
        _mod_kernel,
        grid=(DEPTH, n_col_blocks),
        in_specs=[pl.BlockSpec((rows, D_MODEL), lambda i, j: (0, 0)),
                  pl.BlockSpec((1, D_MODEL, wc), lambda i, j: (i, 0, j)),
                  pl.BlockSpec((1, 1, wc), lambda i, j: (i, 0, j))],
        out_specs=pl.BlockSpec((1, rows, wc), lambda i, j: (i, 0, j)),
        out_shape=jax.ShapeDtypeStruct((DEPTH, rows, N_MOD * D_MODEL), F32),
        compiler_params=_cparams(2),
        name="modulation",
    )(cvec, w_mod, b_mod.reshape(DEPTH, 1, N_MOD * D_MODEL))


def _modulate_project(x, m_ref, w_ref, q_ref, k_ref, v_ref, f_ref):
    h = _layer_norm(x) * (1.0 + m_ref[0, 1:2, :]) + m_ref[0, 0:1, :]
    p = _dot(h.astype(BF16), w_ref[...])
    q_ref[0] = (p[:, :NA_WIDTH] * (HEAD_DIM ** -0.5)).astype(BF16)
    k_ref[0] = p[:, NA_WIDTH:2 * NA_WIDTH].astype(BF16)
    v_ref[0] = p[:, 2 * NA_WIDTH:3 * NA_WIDTH].astype(BF16)
    f_ref[0] = p[:, 3 * NA_WIDTH:]


def _proj_kernel(x_ref, m_ref, w_ref, q_ref, k_ref, v_ref, f_ref):
    _modulate_project(x_ref[0], m_ref, w_ref, q_ref, k_ref, v_ref, f_ref)


def _norm_proj_kernel(y_ref, x1_ref, m_ref, g_ref, b_ref, m_next_ref, w_ref, x_ref, q_ref, k_ref, v_ref, f_ref):
    y = _from_slabs(y_ref, x1_ref.shape[1])
    z = DEEPNORM_ALPHA * x1_ref[0] + m_ref[0, 5:6, :] * y
    x = _layer_norm(z) * g_ref[...] + b_ref[...]
    x_ref[0] = x
    _modulate_project(x, m_next_ref, w_ref, q_ref, k_ref, v_ref, f_ref)


def _norm_in_projection(y_slabs, x1, m, ln_g, ln_b, m_next, w_in_bf16, tm):
    bsz, length, _ = x1.shape
    tiles_per_batch = length // tm
    out = jax.ShapeDtypeStruct((bsz, length, NA_WIDTH), BF16)
    out_f = jax.ShapeDtypeStruct((bsz, length, F_WIDTH), F32)
    out_x = jax.ShapeDtypeStruct((bsz, length, D_MODEL), F32)
    row = lambda b, i: (b, i, 0)
    const2 = lambda b, i: (0, 0)
    mod_spec = pl.BlockSpec((1, N_MOD, D_MODEL), lambda b, i: (b, 0, 0))
    o_spec = pl.BlockSpec((1, tm, NA_WIDTH), row)
    x_spec = pl.BlockSpec((1, tm, D_MODEL), row)
    return pl.pallas_call(
        _norm_proj_kernel,
        grid=(bsz, tiles_per_batch),
        in_specs=[pl.BlockSpec((tm * SLAB_ROWS, LANES), lambda b, i: (b * tiles_per_batch + i, 0)),
                  x_spec, mod_spec,
                  pl.BlockSpec((1, D_MODEL), const2), pl.BlockSpec((1, D_MODEL), const2),
                  mod_spec, pl.BlockSpec((D_MODEL, IN_WIDTH), const2)],
        out_specs=[x_spec, o_spec, o_spec, o_spec, o_spec],
        out_shape=[out_x, out, out, out, out_f],
        compiler_params=_cparams(2),
        name="moe_norm_in_proj",
    )(y_slabs, x1, m, ln_g.reshape(1, D_MODEL), ln_b.reshape(1, D_MODEL), m_next, w_in_bf16)


def _in_projection(x, m, w_in_bf16, tm):
    bsz, length, _ = x.shape
    out = jax.ShapeDtypeStruct((bsz, length, NA_WIDTH), BF16)
    out_f = jax.ShapeDtypeStruct((bsz, length, F_WIDTH), F32)
    o_spec = pl.BlockSpec((1, tm, NA_WIDTH), lambda b, i: (b, i, 0))
    return pl.pallas_call(
        _proj_kernel,
        grid=(bsz, length // tm),
        in_specs=[pl.BlockSpec((1, tm, D_MODEL), lambda b, i: (b, i, 0)),
                  pl.BlockSpec((1, N_MOD, D_MODEL), lambda b, i: (b, 0, 0)),
                  pl.BlockSpec((D_MODEL, IN_WIDTH), lambda b, i: (0, 0))],
        out_specs=[o_spec, o_spec, o_spec, o_spec],
        out_shape=[out, out, out, out_f],
        compiler_params=_cparams(2),
        name="ln_mod_in_proj",
    )(x, m, w_in_bf16)


def _head_lanes(h):
    lane = lax.broadcasted_iota(jnp.int32, (1, LANES), 1)
    return (lane >= HEAD_DIM * h) & (lane < HEAD_DIM * (h + 1))


def _scores_pass(q, h, tiles, s_ref):
    qh = jnp.where(_head_lanes(h), q, jnp.zeros_like(q))
    m = None
    t = tiles[0][0].shape[0]
    for j, (k, _, bias) in enumerate(tiles):
        s = _dot_nt(qh, k)
        if bias is not None:
            s = s + bias
        s_ref[:, j * t:(j + 1) * t] = s
        mj = jnp.max(s, axis=-1, keepdims=True)
        m = mj if m is None else jnp.maximum(m, mj)
    return m


def _pv_pass(h, tiles, s_ref, m):
    o = None
    t = tiles[0][0].shape[0]
    in_head = _head_lanes(h)
    for j, (_, v, _) in enumerate(tiles):
        p = jnp.exp(s_ref[:, j * t:(j + 1) * t] - m)
        oj = _dot(p.astype(BF16), jnp.where(in_head, v, jnp.ones_like(v)))
        o = oj if o is None else o + oj
    return o / pltpu.roll(o, HEAD_DIM, axis=1)


def _attention_units(units):
    outs = []
    maxima = [_scores_pass(*units[0])]
    for u in range(len(units)):
        if u + 1 < len(units):
            maxima.append(_scores_pass(*units[u + 1]))
        _, h, tiles, s_ref = units[u]
        outs.append(_pv_pass(h, tiles, s_ref, maxima[u]))
    return outs


def _merge_heads(outs):
    merged = outs[0]
    for h in range(1, len(outs)):
        merged = jnp.where(_head_lanes(h), outs[h], merged)
    return merged


DR_PAD = 2 * WIN_ROWS
DC_PAD = 2 * WIN_COLS


def _toeplitz_kernel(r_ref, sel_ref, mask_ref, o_ref):
    n = r_ref.shape[0]
    for qc in range(GRID_W):
        block = jnp.dot(r_ref[...], sel_ref[qc], preferred_element_type=F32, precision=lax.Precision.HIGHEST)
        o_ref[pl.ds(qc, n, stride=GRID_W), :] = block + mask_ref[qc]


def _rpb_toeplitz(rpb):
    depth, heads, n_dr, n_dc = rpb.shape
    qc = np.arange(GRID_W)[:, None]
    kc = (np.arange(LANES) % GRID_W)[None, :]
    cs = np.clip(qc - WIN_COLS // 2, 0, GRID_W - WIN_COLS)
    col_valid = (kc >= cs) & (kc < cs + WIN_COLS)
    dc = kc - qc + WIN_COLS - 1
    select = (np.arange(DC_PAD)[None, :, None] == dc[:, None, :]) & col_valid[:, None, :]
    mask = np.where(col_valid, 0.0, MASK_VALUE)[:, None, :]
    r = jnp.pad(rpb.astype(F32), ((0, 0), (0, 0), (0, DR_PAD - n_dr), (0, DC_PAD - n_dc)))
    n = depth * heads * DR_PAD
    const3 = lambda: (0, 0, 0)
    out = pl.pallas_call(
        _toeplitz_kernel,
        grid=(),
        in_specs=[pl.BlockSpec((n, DC_PAD), lambda: (0, 0)),
                  pl.BlockSpec((GRID_W, DC_PAD, LANES), const3), pl.BlockSpec((GRID_W, 1, LANES), const3)],
        out_specs=pl.BlockSpec((n * GRID_W, LANES), lambda: (0, 0)),
        out_shape=jax.ShapeDtypeStruct((n * GRID_W, LANES), F32),
        compiler_params=pltpu.CompilerParams(vmem_limit_bytes=VMEM_LIMIT_BYTES),
        name="rpb_toeplitz",
    )(r.reshape(n, DC_PAD), jnp.asarray(select, F32), jnp.asarray(mask, F32))
    return out.reshape(depth * heads, DR_PAD, GRID_W, LANES)


Q_CHUNK_ROWS = 4
WINDOW_CHUNKS = 3
N_Q_CHUNKS = Q_ROWS // Q_CHUNK_ROWS
assert Q_CHUNK_ROWS == K_CHUNK_ROWS and WINDOW_CHUNKS * K_CHUNK_ROWS >= Q_CHUNK_ROWS + WIN_ROWS - 1
assert N_Q_CHUNKS == 2 and N_K_CHUNKS == 4


def _key_row_start(rb, rows):
    return int(np.clip(rb * Q_ROWS - (K_ROWS - Q_ROWS) // 2, 0, rows - K_ROWS))


def _window_uses_last_chunk(rb, c, n_rb):
    if c == 0:
        return rb == n_rb - 1
    return rb != 0


def _window_key_rows(rb, c, rows):
    var = N_K_CHUNKS - 1 if _window_uses_last_chunk(rb, c, rows // Q_ROWS) else 0
    k0 = _key_row_start(rb, rows)
    return [k0 + K_CHUNK_ROWS * j + i for j in (var, 1, 2) for i in range(K_CHUNK_ROWS)]


def _row_window(qr, rows):
    kh = min(WIN_ROWS, rows)
    rs = int(np.clip(qr - kh // 2, 0, rows - kh))
    return rs, rs + kh


def _check_windows(rows):
    def relative(rb):
        base = rb * Q_ROWS
        return [([kr - base for kr in _window_key_rows(rb, c, rows)],
                 [tuple(r - base for r in _row_window(base + c * Q_CHUNK_ROWS + qi, rows))
                  for qi in range(Q_CHUNK_ROWS)]) for c in range(N_Q_CHUNKS)]

    n_rb = rows // Q_ROWS
    for rb in range(n_rb):
        assert rb in (0, n_rb - 1) or relative(rb) == relative(1), rb
        for c in range(N_Q_CHUNKS):
            have = set(_window_key_rows(rb, c, rows))
            for qi in range(Q_CHUNK_ROWS):
                lo, hi = _row_window(rb * Q_ROWS + c * Q_CHUNK_ROWS + qi, rows)
                assert set(range(lo, hi)) <= have, (rb, c, qi)


def _build_bias_tables(t_ref, bias_scr, rows):
    n_rb = rows // Q_ROWS
    left = lax.broadcasted_iota(jnp.int32, (GRID_W, LANES), 1) < GRID_W
    masked = jnp.full((GRID_W, LANES), MASK_VALUE, F32)
    for variant, rb in enumerate((0, 1, n_rb - 1)):
        for c in range(N_Q_CHUNKS):
            key_rows = _window_key_rows(rb, c, rows)
            for h in range(HEADS_PER_STEP):
                for qi in range(Q_CHUNK_ROWS):
                    qr = rb * Q_ROWS + c * Q_CHUNK_ROWS + qi
                    lo, hi = _row_window(qr, rows)
                    for p in range(len(key_rows) // 2):
                        pair = [t_ref[h, kr - qr + WIN_ROWS - 1] if lo <= kr < hi else None
                                for kr in key_rows[2 * p:2 * p + 2]]
                        if pair[0] is None and pair[1] is None:
                            block = masked
                        else:
                            block = jnp.where(left, masked if pair[0] is None else pair[0],
                                              masked if pair[1] is None else pair[1])
                        bias_scr[variant, c, h, qi * GRID_W:(qi + 1) * GRID_W, p * LANES:(p + 1) * LANES] = block


def _na_kernel(q_ref, k0, k1, k2, k3, v0, v1, v2, v3, kc_ref, vc_ref, t_ref, o_ref, bias_scr, s_scr, *, rows):
    n_rb = rows // Q_ROWS
    b, rb = pl.program_id(1), pl.program_id(2)

    @pl.when((b == 0) & (rb == 0))
    def _():
        _build_bias_tables(t_ref, bias_scr, rows)

    variant = jnp.where(rb == 0, 0, jnp.where(rb == n_rb - 1, 2, 1))
    tq = Q_CHUNK_ROWS * GRID_W
    tk = K_CHUNK_ROWS * GRID_W
    units = []
    for c in range(N_Q_CHUNKS):
        use_last = (rb == n_rb - 1) if c == 0 else (rb != 0)
        window = [(jnp.where(use_last, k3[0], k0[0]), jnp.where(use_last, v3[0], v0[0])),
                  (k1[0], v1[0]), (k2[0], v2[0])]
        q = q_ref[0, c * tq:(c + 1) * tq, :]
        for h in range(HEADS_PER_STEP):
            lat = [(k, v, bias_scr[variant, c, h, :, j * tk:(j + 1) * tk]) for j, (k, v) in enumerate(window)]
            units.append((q, h, lat + [(kc_ref[0], vc_ref[0], None)], s_scr.at[c, h]))
    outs = _attention_units(units)
    for c in range(N_Q_CHUNKS):
        o = _merge_heads(outs[c * HEADS_PER_STEP:(c + 1) * HEADS_PER_STEP])
        o_ref[0, c * tq:(c + 1) * tq, :] = o.astype(BF16)


def _neighborhood_attention(q, k, v, kc, vc, toeplitz, layer):
    bsz, length, _ = q.shape
    rows = length // GRID_W
    n_rb = rows // Q_ROWS
    assert rows % Q_ROWS == 0 and rows >= K_ROWS + Q_ROWS and n_rb >= 3
    _check_windows(rows)
    n_ctx = kc.shape[1]
    tq = Q_ROWS * GRID_W
    tk = K_CHUNK_ROWS * GRID_W
    assert n_ctx == tk
    max_chunk = (rows - K_ROWS) // K_CHUNK_ROWS
    half = (K_ROWS - Q_ROWS) // 2 // K_CHUNK_ROWS

    def kv_spec(j):
        def index(hp, b, rb):
            start = jnp.clip(rb * (Q_ROWS // K_CHUNK_ROWS) - half, 0, max_chunk)
            return (b, start + j, hp)
        return pl.BlockSpec((1, tk, LANES), index)

    ctx_spec = pl.BlockSpec((1, n_ctx, LANES), lambda hp, b, rb: (b, 0, hp))
    q_spec = pl.BlockSpec((1, tq, LANES), lambda hp, b, rb: (b, rb, hp))
    window = WINDOW_CHUNKS * K_CHUNK_ROWS * GRID_W
    return pl.pallas_call(
        functools.partial(_na_kernel, rows=rows),
        grid=(NA_HEADS // HEADS_PER_STEP, bsz, n_rb),
        in_specs=([q_spec] + [kv_spec(j) for j in range(N_K_CHUNKS)] + [kv_spec(j) for j in range(N_K_CHUNKS)]
                  + [ctx_spec, ctx_spec,
                     pl.BlockSpec((HEADS_PER_STEP, DR_PAD, GRID_W, LANES),
                                  lambda hp, b, rb: (layer * (NA_HEADS // HEADS_PER_STEP) + hp, 0, 0, 0))]),
        out_specs=q_spec,
        out_shape=jax.ShapeDtypeStruct((bsz, length, NA_WIDTH), BF16),
        scratch_shapes=[pltpu.VMEM((3, N_Q_CHUNKS, HEADS_PER_STEP, Q_CHUNK_ROWS * GRID_W, window), F32),
                        pltpu.VMEM((N_Q_CHUNKS, HEADS_PER_STEP, Q_CHUNK_ROWS * GRID_W, window + n_ctx), F32)],
        compiler_params=_cparams(3),
        name="neighborhood_attention",
    )(q, *([k] * N_K_CHUNKS), *([v] * N_K_CHUNKS), kc, vc, toeplitz)


def _ctx_attn_kernel(q_ref, k_ref, v_ref, o_ref, s_scr):
    tiles = [(k_ref[0], v_ref[0], None)]
    outs = _attention_units([(q_ref[0], h, tiles, s_scr.at[h]) for h in range(HEADS_PER_STEP)])
    o_ref[0] = _merge_heads(outs).astype(BF16)


def _context_attention(qc, kc, vc):
    bsz, n_ctx, _ = qc.shape
    spec = pl.BlockSpec((1, n_ctx, LANES), lambda b, hp: (b, 0, hp))
    return pl.pallas_call(
        _ctx_attn_kernel,
        grid=(bsz, NA_HEADS // HEADS_PER_STEP),
        in_specs=[spec, spec, spec],
        out_specs=spec,
        out_shape=jax.ShapeDtypeStruct((bsz, n_ctx, NA_WIDTH), BF16),
        scratch_shapes=[pltpu.VMEM((HEADS_PER_STEP, n_ctx, n_ctx), F32)],
        compiler_params=_cparams(2),
        name="context_attention",
    )(qc, kc, vc)


def _dft_cos_sin(n):
    ang = 2.0 * np.pi * np.outer(np.arange(n), np.arange(n)) / n
    return np.cos(ang), np.sin(ang)


def _bf16_table(a):
    return jnp.asarray(a, F32).astype(BF16)


def _channel_dft_matrix(n_groups=F_GROUPS):
    c, s = _dft_cos_sin(F_GROUP_DIM)
    scale = F_GROUP_DIM ** -0.5
    eye = np.eye(n_groups)
    return np.concatenate([np.kron(eye, c), np.kron(eye, s)], axis=1) * scale


def _fft_stage1_kernel(f_ref, perm_ref, w1_ref, cs_ref, sc_ref, tc_ref, ts_ref, zr_ref, zi_ref, *, n_slow, nt):
    x = f_ref[0].reshape(n_slow * nt, F_WIDTH).astype(BF16)
    x = _dot(perm_ref[...], x).astype(BF16)
    ab = [_dot(x[:, p * LANES:(p + 1) * LANES], w1_ref[...]).astype(BF16) for p in range(F_WIDTH // LANES)]
    a_all = jnp.concatenate([blk[:, :LANES] for blk in ab], axis=-1)
    b_all = jnp.concatenate([blk[:, LANES:] for blk in ab], axis=-1)
    for t in range(nt):
        a = a_all[t * n_slow:(t + 1) * n_slow]
        b = b_all[t * n_slow:(t + 1) * n_slow]
        z = _dot(cs_ref[...], a) + _dot(sc_ref[...], b)
        zr, zi = z[:n_slow], z[n_slow:]
        c, s = tc_ref[t], ts_ref[t]
        zr_ref[0, t] = zr * c - zi * s
        zi_ref[0, t] = zr * s + zi * c


def _fft_stage2_kernel(zr_ref, zi_ref, f_ref, y_ref, *, n_fast, kb):
    rhs = jnp.concatenate([zr_ref[0].reshape(n_fast * kb, F_WIDTH).astype(BF16),
                           zi_ref[0].reshape(n_fast * kb, F_WIDTH).astype(BF16)], axis=0)
    y_ref[0] = _dot(f_ref[...], rhs).reshape(n_fast, kb, F_WIDTH)


def _fourier_positions(f, n_slow, n_fast):
    bsz, n, _ = f.shape
    assert n == n_slow * n_fast
    nt = SUBLANES
    kb = SUBLANES
    w1 = _bf16_table(_channel_dft_matrix(LANES // F_GROUP_DIM))
    perm = _bf16_table(np.eye(n_slow * nt).reshape(n_slow, nt, n_slow * nt).transpose(1, 0, 2)
                       .reshape(n_slow * nt, n_slow * nt))
    c1, s1 = _dft_cos_sin(n_slow)
    sc1 = n_slow ** -0.5
    cs = _bf16_table(np.concatenate([c1, s1], axis=0) * sc1)
    sc = _bf16_table(np.concatenate([-s1, c1], axis=0) * sc1)
    tw = 2.0 * np.pi * np.outer(np.arange(n_fast), np.arange(n_slow)) / n
    tc = jnp.asarray(np.cos(tw)[:, :, None], F32)
    ts = jnp.asarray(np.sin(tw)[:, :, None], F32)
    z_shape = jax.ShapeDtypeStruct((bsz, n_fast, n_slow, F_WIDTH), F32)
    z_spec = pl.BlockSpec((1, nt, n_slow, F_WIDTH), lambda b, j: (b, j, 0, 0))
    const2 = lambda b, j: (0, 0)
    tw_spec = pl.BlockSpec((nt, n_slow, 1), lambda b, j: (j, 0, 0))
    zr, zi = pl.pallas_call(
        functools.partial(_fft_stage1_kernel, n_slow=n_slow, nt=nt),
        grid=(bsz, n_fast // nt),
        in_specs=[pl.BlockSpec((1, n_slow, nt, F_WIDTH), lambda b, j: (b, 0, j, 0)),
                  pl.BlockSpec(perm.shape, const2),
                  pl.BlockSpec(w1.shape, const2), pl.BlockSpec(cs.shape, const2), pl.BlockSpec(sc.shape, const2),
                  tw_spec, tw_spec],
        out_specs=[z_spec, z_spec],
        out_shape=[z_shape, z_shape],
        compiler_params=_cparams(2),
        name="fnet_stage1",
    )(f.reshape(bsz, n_slow, n_fast, F_WIDTH), perm, w1, cs, sc, tc, ts)

    c2, s2 = _dft_cos_sin(n_fast)
    sc2 = n_fast ** -0.5
    eye = np.eye(kb)
    f2 = _bf16_table(np.concatenate([np.kron(c2, eye), np.kron(-s2, eye)], axis=1) * sc2)
    blk = pl.BlockSpec((1, n_fast, kb, F_WIDTH), lambda b, j: (b, 0, j, 0))
    y = pl.pallas_call(
        functools.partial(_fft_stage2_kernel, n_fast=n_fast, kb=kb),
        grid=(bsz, n_slow // kb),
        in_specs=[blk, blk, pl.BlockSpec(f2.shape, const2)],
        out_specs=blk,
        out_shape=jax.ShapeDtypeStruct((bsz, n_fast, n_slow, F_WIDTH), F32),
        compiler_params=_cparams(2),
        name="fnet_stage2",
    )(zr, zi, f2)
    return y.reshape(bsz, n, F_WIDTH)


def _second_largest_sum(a, b, c, d):
    mab, nab = jnp.maximum(a, b), jnp.minimum(a, b)
    mcd, ncd = jnp.maximum(c, d), jnp.minimum(c, d)
    return jnp.maximum(mab, mcd) + jnp.maximum(jnp.minimum(mab, mcd), jnp.maximum(nab, ncd))


def _selected_group(sb_rows):
    epg = EXPERTS_PER_GROUP
    g_score = [_second_largest_sum(*sb_rows[g * epg:(g + 1) * epg]) for g in range(N_GROUPS)]
    best = functools.reduce(jnp.maximum, g_score)
    group = jnp.full_like(best, float(N_GROUPS - 1))
    for g in range(N_GROUPS - 2, -1, -1):
        group = jnp.where(g_score[g] == best, float(g), group)
    return group


def _top2_gates(cand_s, cand_sb):
    n = len(cand_s)
    w = []
    for j in range(n):
        rank = jnp.zeros_like(cand_sb[j])
        for i in range(n):
            if i == j:
                continue
            ahead = (cand_sb[i] > cand_sb[j]) | ((cand_sb[i] == cand_sb[j]) & (i < j))
            rank = rank + jnp.where(ahead, 1.0, 0.0)
        w.append(jnp.where(rank < 2.0, cand_s[j], 0.0))
    total = functools.reduce(jnp.add, w)
    return [wj / total for wj in w]


def _out_kernel(a_ref, y_ref, x_ref, m_ref, wo_ref, bd_ref, g_ref, b_ref, wr_ref, rb_ref,
                x1_ref, h2_ref, grp_ref):
    y2 = _dot(y_ref[0].astype(BF16), bd_ref[...]).astype(BF16)
    o = _dot(a_ref[0], wo_ref[:NA_WIDTH, :]) + _dot(y2, wo_ref[NA_WIDTH:, :])
    z = DEEPNORM_ALPHA * x_ref[0] + m_ref[0, 2:3, :] * o
    x1 = _layer_norm(z) * g_ref[...] + b_ref[...]
    x1_ref[0] = x1
    h2 = _layer_norm(x1) * (1.0 + m_ref[0, 4:5, :]) + m_ref[0, 3:4, :]
    _to_slabs(h2_ref, h2)
    sb = jax.nn.sigmoid(_dot_nt(wr_ref[...], h2.astype(BF16))) + rb_ref[...]
    group = _selected_group([sb[e:e + 1] for e in range(N_EXPERTS)])
    grp_ref[0] = jnp.broadcast_to(group, grp_ref.shape[1:])


def _out_projection(attn, yf, x, m, w_out_bf16, w_four_bd, ln_g, ln_b, w_router_t, router_bias, tm):
    bsz, length, _ = x.shape
    tiles_per_batch = length // tm
    row = lambda b, i: (b, i, 0)
    const2 = lambda b, i: (0, 0)
    return pl.pallas_call(
        _out_kernel,
        grid=(bsz, length // tm),
        in_specs=[pl.BlockSpec((1, tm, NA_WIDTH), row), pl.BlockSpec((1, tm, F_WIDTH), row),
                  pl.BlockSpec((1, tm, D_MODEL), row),
                  pl.BlockSpec((1, N_MOD, D_MODEL), lambda b, i: (b, 0, 0)),
                  pl.BlockSpec((NA_WIDTH + F_WIDTH, D_MODEL), const2),
                  pl.BlockSpec((F_WIDTH, F_WIDTH), const2),
                  pl.BlockSpec((1, D_MODEL), const2), pl.BlockSpec((1, D_MODEL), const2),
                  pl.BlockSpec((N_EXPERTS, D_MODEL), const2), pl.BlockSpec((N_EXPERTS, 1), const2)],
        out_specs=[pl.BlockSpec((1, tm, D_MODEL), row),
                   pl.BlockSpec((tm * SLAB_ROWS, LANES), lambda b, i: (b * tiles_per_batch + i, 0)),
                   pl.BlockSpec((1, SUBLANES, tm), lambda b, i: (b, 0, i))],
        out_shape=[jax.ShapeDtypeStruct((bsz, length, D_MODEL), F32),
                   jax.ShapeDtypeStruct((bsz * length * SLAB_ROWS, LANES), F32),
                   jax.ShapeDtypeStruct((bsz, SUBLANES, length), F32)],
        compiler_params=_cparams(2),
        name="out_proj_norm_route",
    )(attn, yf, x, m, w_out_bf16, w_four_bd, ln_g.reshape(1, D_MODEL), ln_b.reshape(1, D_MODEL),
      w_router_t, router_bias.reshape(N_EXPERTS, 1))


def _group_plan(group, tm):
    n = group.shape[0]
    n_slots = n + N_GROUPS * tm
    onehot = (group[:, None] == jnp.arange(N_GROUPS, dtype=jnp.int32)[None, :]).astype(jnp.int32)
    csum = jnp.cumsum(onehot, axis=0)
    rank = jnp.sum(onehot * csum, axis=1) - 1
    count = csum[-1]
    padded = ((count + tm - 1) // tm) * tm
    end = jnp.cumsum(padded)
    start = end - padded
    slot = jnp.sum(onehot * start[None, :], axis=1) + rank
    tile_start = jnp.arange(n_slots // tm, dtype=jnp.int32) * tm
    tile_group = jnp.minimum(jnp.sum((end[None, :] <= tile_start[:, None]).astype(jnp.int32), axis=1),
                             N_GROUPS - 1)
    tile_rows = jnp.clip((start + count)[tile_group] - tile_start, 0, tm)
    return slot, tile_group, tile_rows, start, start + count, end


ROW_DMA_UNROLL = 8


def _slab(ref, token):
    return ref.at[pl.ds(pl.multiple_of(token * SLAB_ROWS, SLAB_ROWS), SLAB_ROWS), :]


def _row_scatter(idx_ref, base, src_ref, dst_hbm, sem, n_rows, wait, inverse_ref=None, src_base=0):
    def one(r, priority):
        if wait:
            pltpu.make_async_copy(_slab(src_ref, 0), _slab(dst_hbm, 0), sem).wait()
        else:
            idx = idx_ref[base + r]
            if inverse_ref is not None:
                inverse_ref[idx] = base + r
            pltpu.make_async_copy(_slab(src_ref, src_base + r), _slab(dst_hbm, idx), sem).start(priority=priority)

    def chunk(j, carry):
        for u in range(ROW_DMA_UNROLL):
            one(j * ROW_DMA_UNROLL + u, u % 2)
        return carry

    def single(r, carry):
        one(r, 0)
        return carry

    n_chunks = n_rows // ROW_DMA_UNROLL
    lax.fori_loop(0, n_chunks, chunk, 0)
    lax.fori_loop(n_chunks * ROW_DMA_UNROLL, n_rows, single, 0)


def _to_slots_kernel(slot_ref, start_ref, rows_end_ref, end_ref, h_hbm, o_hbm, dst_ref, zbuf, hbuf, sems,
                     fetch_sems, *, tm, tm_slots, n_steps):
    step = pl.program_id(0)
    n_slots = dst_ref.shape[0]
    sem = sems.at[0]

    @pl.when(step == 0)
    def _():
        zbuf[...] = jnp.zeros_like(zbuf)

        def fill(slot0):
            rows = tm_slots * SLAB_ROWS
            return pltpu.make_async_copy(zbuf, o_hbm.at[pl.ds(pl.multiple_of(slot0 * SLAB_ROWS, rows), rows), :], sem)

        for g in range(N_GROUPS):
            @pl.when(end_ref[g] > start_ref[g])
            def _():
                fill(end_ref[g] - tm_slots).start()
        for g in range(N_GROUPS):
            @pl.when(end_ref[g] > start_ref[g])
            def _():
                fill(end_ref[g] - tm_slots).wait()

        def fill_unused(t, carry):
            fill(t * tm_slots).start()
            fill(t * tm_slots).wait()
            return carry
        lax.fori_loop(end_ref[N_GROUPS - 1] // tm_slots, n_slots // tm_slots, fill_unused, 0)

        def clear(p, carry):
            dst_ref[p] = 0
            return carry
        for g in range(N_GROUPS):
            lax.fori_loop(rows_end_ref[g], end_ref[g], clear, 0)
        lax.fori_loop(end_ref[N_GROUPS - 1], n_slots, clear, 0)

    def fetch(s):
        rows = tm * SLAB_ROWS
        return pltpu.make_async_copy(h_hbm.at[pl.ds(pl.multiple_of(s * rows, rows), rows), :],
                                     hbuf.at[s % 3], fetch_sems.at[s % 3])

    def scatter(s, wait):
        _row_scatter(slot_ref, s * tm, hbuf.at[s % 3], o_hbm, sems.at[s % 2], tm, wait,
                     inverse_ref=None if wait else dst_ref)

    @pl.when(step == 0)
    def _():
        fetch(0).start()

    @pl.when(step + 1 < n_steps)
    def _():
        fetch(step + 1).start()

    fetch(step).wait()
    scatter(step, wait=False)

    @pl.when(step >= 1)
    def _():
        scatter(step - 1, wait=True)

    @pl.when(step == n_steps - 1)
    def _():
        scatter(step, wait=True)


def _to_slots(h2_slabs, slot, start, rows_end, end, tm, tm_slots):
    n = h2_slabs.shape[0] // SLAB_ROWS
    n_slots = n + N_GROUPS * tm_slots
    grid_spec = pltpu.PrefetchScalarGridSpec(
        num_scalar_prefetch=4,
        grid=(n // tm,),
        in_specs=[pl.BlockSpec(memory_space=pl.ANY)],
        out_specs=[pl.BlockSpec(memory_space=pl.ANY), pl.BlockSpec(memory_space=pltpu.SMEM)],
        scratch_shapes=[pltpu.VMEM((tm_slots * SLAB_ROWS, LANES), F32),
                        pltpu.VMEM((3, tm * SLAB_ROWS, LANES), F32),
                        pltpu.SemaphoreType.DMA((2,)), pltpu.SemaphoreType.DMA((3,))],
    )
    return pl.pallas_call(
        functools.partial(_to_slots_kernel, tm=tm, tm_slots=tm_slots, n_steps=n // tm),
        grid_spec=grid_spec,
        out_shape=[jax.ShapeDtypeStruct((n_slots * SLAB_ROWS, LANES), F32),
                   jax.ShapeDtypeStruct((n_slots,), jnp.int32)],
        compiler_params=_cparams(1),
        name="moe_rows_to_slots",
    )(slot, start, rows_end, end, h2_slabs)


def _moe_group_kernel(dst_ref, tile_group_ref, tile_rows_ref, n_used_ref, h_ref, wg_ref, wu_ref, wd_ref, wr_ref,
                      rb_ref, y_hbm, ybuf, sems, *, tm, n_tiles):
    del tile_group_ref
    step = pl.program_id(0)
    n_used = n_used_ref[0]
    buf = step % 2

    def scatter(tile, wait):
        _row_scatter(dst_ref, tile * tm, ybuf.at[tile % 2], y_hbm, sems.at[tile % 2], tile_rows_ref[tile], wait)

    @pl.when((step >= 2) & (step < n_used))
    def _():
        scatter(step - 2, wait=True)

    @pl.when(step < n_used)
    def _():
        h = _from_slabs(h_ref, tm).astype(BF16)
        s = jax.nn.sigmoid(_dot(h, wr_ref[0]))
        sb = s + rb_ref[0]
        epg = EXPERTS_PER_GROUP
        gates = _top2_gates([s[:, j:j + 1] for j in range(epg)], [sb[:, j:j + 1] for j in range(epg)])
        acc = None
        gate_up = [(_dot(h, wg_ref[0]), _dot(h, wu_ref[0]))]
        for e in range(epg):
            if e + 1 < epg:
                gate_up.append((_dot(h, wg_ref[e + 1]), _dot(h, wu_ref[e + 1])))
            gate, up = gate_up[e]
            hid = (gate * jax.nn.sigmoid(gate)) * up * gates[e]
            y = _dot(hid.astype(BF16), wd_ref[e])
            acc = y if acc is None else acc + y
        _to_slabs(ybuf.at[buf], acc)
        scatter(step, wait=False)

    @pl.when(step == n_tiles - 1)
    def _():
        @pl.when(n_used >= 2)
        def _():
            scatter(n_used - 2, wait=True)

        @pl.when(n_used >= 1)
        def _():
            scatter(n_used - 1, wait=True)


def _moe_experts(h_slots, dst, tile_group, tile_rows, n_used, n_tokens, layer, w_gate, w_up, w_down,
                 w_router_grp, router_bias_grp, tm):
    n_slots = h_slots.shape[0] // SLAB_ROWS
    n_tiles = n_slots // tm
    epg = EXPERTS_PER_GROUP
    by_group = lambda i, dst_ref, tg_ref, tr_ref, nu_ref: (tg_ref[i], 0, 0)
    by_layer_group = lambda i, dst_ref, tg_ref, tr_ref, nu_ref: (layer * N_GROUPS + tg_ref[i], 0, 0)
    grid_spec = pltpu.PrefetchScalarGridSpec(
        num_scalar_prefetch=4,
        grid=(n_tiles,),
        in_specs=[pl.BlockSpec((tm * SLAB_ROWS, LANES),
                               lambda i, dst_ref, tg_ref, tr_ref, nu_ref: (jnp.minimum(i, nu_ref[0] - 1), 0)),
                  pl.BlockSpec((epg, D_MODEL, D_EXPERT), by_layer_group),
                  pl.BlockSpec((epg, D_MODEL, D_EXPERT), by_layer_group),
                  pl.BlockSpec((epg, D_EXPERT, D_MODEL), by_layer_group),
                  pl.BlockSpec((1, D_MODEL, LANES), by_group),
                  pl.BlockSpec((1, 1, LANES), by_group)],
        out_specs=pl.BlockSpec(memory_space=pl.ANY),
        scratch_shapes=[pltpu.VMEM((2, tm * SLAB_ROWS, LANES), F32), pltpu.SemaphoreType.DMA((2,))],
    )
    return pl.pallas_call(
        functools.partial(_moe_group_kernel, tm=tm, n_tiles=n_tiles),
        grid_spec=grid_spec,
        out_shape=jax.ShapeDtypeStruct((n_tokens * SLAB_ROWS, LANES), F32),
        compiler_params=_cparams(1),
        name="moe_group_experts",
    )(dst, tile_group, tile_rows, n_used, h_slots, w_gate, w_up, w_down, w_router_grp, router_bias_grp)


def _residual_norm_kernel(y_ref, x1_ref, m_ref, g_ref, b_ref, o_ref):
    y = _from_slabs(y_ref, x1_ref.shape[0])
    z = DEEPNORM_ALPHA * x1_ref[...] + m_ref[0, 5:6, :] * y
    o_ref[...] = _layer_norm(z) * g_ref[...] + b_ref[...]


def _residual_norm(y_slabs, x1, m, ln_g, ln_b, tm):
    bsz, length, _ = x1.shape
    n = bsz * length
    tiles_per_batch = length // tm
    const2 = lambda i: (0, 0)
    row = pl.BlockSpec((tm, D_MODEL), lambda i: (i, 0))
    out = pl.pallas_call(
        _residual_norm_kernel,
        grid=(n // tm,),
        in_specs=[pl.BlockSpec((tm * SLAB_ROWS, LANES), lambda i: (i, 0)), row,
                  pl.BlockSpec((1, N_MOD, D_MODEL), lambda i: (i // tiles_per_batch, 0, 0)),
                  pl.BlockSpec((1, D_MODEL), const2), pl.BlockSpec((1, D_MODEL), const2)],
        out_specs=row,
        out_shape=jax.ShapeDtypeStruct((n, D_MODEL), F32),
        compiler_params=_cparams(1),
        name="moe_residual_norm",
    )(y_slabs, x1.reshape(n, D_MODEL), m, ln_g.reshape(1, D_MODEL), ln_b.reshape(1, D_MODEL))
    return out.reshape(bsz, length, D_MODEL)


def _grouped_moe(h2, group_rows, layer, w_gate, w_up, w_down, w_router_grp, router_bias_grp, tm_experts,
                 tm_tokens):
    bsz, _, length = group_rows.shape
    n = bsz * length
    group = group_rows[:, 0, :].reshape(n).astype(jnp.int32)
    slot, tile_group, tile_rows, start, rows_end, end = _group_plan(group, tm_experts)
    h_slots, dst = _to_slots(h2, slot, start, rows_end, end, tm_tokens, tm_experts)
    n_used = (end[N_GROUPS - 1:] // tm_experts).astype(jnp.int32)
    return _moe_experts(h_slots, dst, tile_group, tile_rows, n_used, n, layer, w_gate, w_up, w_down, w_router_grp,
                        router_bias_grp, tm_experts)


def _ctx_fourier_kernel(f_ref, w1_ref, c_ref, s_ref, y_ref):
    ab = _dot(f_ref[0].astype(BF16), w1_ref[...]).astype(BF16)
    y = _dot(c_ref[...], ab[:, :F_WIDTH]) + _dot(s_ref[...], ab[:, F_WIDTH:])
    y_ref[0] = y.astype(BF16)


def _context_fourier(fc):
    bsz, n, _ = fc.shape
    w1 = _bf16_table(_channel_dft_matrix())
    c, s = _dft_cos_sin(n)
    cm = _bf16_table(c * n ** -0.5)
    sm = _bf16_table(-s * n ** -0.5)
    const2 = lambda b: (0, 0)
    blk = pl.BlockSpec((1, n, F_WIDTH), lambda b: (b, 0, 0))
    return pl.pallas_call(
        _ctx_fourier_kernel,
        grid=(bsz,),
        in_specs=[blk, pl.BlockSpec(w1.shape, const2), pl.BlockSpec(cm.shape, const2),
                  pl.BlockSpec(sm.shape, const2)],
        out_specs=blk,
        out_shape=jax.ShapeDtypeStruct((bsz, n, F_WIDTH), BF16),
        compiler_params=_cparams(1),
        name="context_fnet",
    )(fc, w1, cm, sm)


def _block_diag(w):
    g, c, _ = w.shape
    eye = jnp.eye(g, dtype=w.dtype)
    return (eye[:, None, :, None] * w[:, :, None, :]).reshape(g * c, g * c)


def kernel(x, c, ctx, c_ctx, w_mod, b_mod, w_in, rpb, w_four, w_out, ln1_g, ln1_b, ln2_g, ln2_b,
           w_router, router_bias, w_gate, w_up, w_down):
    bsz, length, _ = x.shape
    n_ctx = ctx.shape[1]
    rows = length // GRID_W

    cvec = jnp.concatenate([c, c_ctx[None, :], jnp.zeros((8 - bsz - 1, D_MODEL), F32)], axis=0)
    mods = _modulation(cvec, w_mod, b_mod)
    w_router_t = w_router.T.astype(BF16)
    toeplitz = _rpb_toeplitz(rpb)
    lane_pad = LANES - EXPERTS_PER_GROUP
    w_router_grp = jnp.pad(w_router.reshape(D_MODEL, N_GROUPS, EXPERTS_PER_GROUP).transpose(1, 0, 2),
                           ((0, 0), (0, 0), (0, lane_pad))).astype(BF16)
    router_bias_grp = jnp.pad(router_bias.astype(F32).reshape(N_GROUPS, 1, EXPERTS_PER_GROUP),
                              ((0, 0), (0, 0), (0, lane_pad)))

    wg = w_gate.astype(BF16).reshape(DEPTH * N_EXPERTS, D_MODEL, D_EXPERT)
    wu = w_up.astype(BF16).reshape(DEPTH * N_EXPERTS, D_MODEL, D_EXPERT)
    wd = w_down.astype(BF16).reshape(DEPTH * N_EXPERTS, D_EXPERT, D_MODEL)

    def latent_mod(i):
        return mods[i, :bsz].reshape(bsz, N_MOD, D_MODEL)

    xc = ctx
    projected = None
    for i in range(DEPTH):
        last = i == DEPTH - 1
        m = latent_mod(i)
        mc = jnp.broadcast_to(mods[i, bsz].reshape(1, N_MOD, D_MODEL), (bsz, N_MOD, D_MODEL))
        w_in_b = w_in[i].astype(BF16)
        w_out_b = w_out[i].astype(BF16)
        w_four_bd = _block_diag(w_four[i]).astype(BF16)

        q, k, v, f = projected if projected is not None else _in_projection(x, m, w_in_b, tm=512)
        qc, kc, vc, fc = _in_projection(xc, mc, w_in_b, tm=n_ctx)

        attn = _neighborhood_attention(q, k, v, kc, vc, toeplitz, i)
        yf = _fourier_positions(f, n_slow=rows, n_fast=GRID_W)
        x1, h2, grp = _out_projection(attn, yf, x, m, w_out_b, w_four_bd, ln1_g[i], ln1_b[i],
                                      w_router_t, router_bias, tm=512)
        y = _grouped_moe(h2, grp, i, wg, wu, wd, w_router_grp, router_bias_grp, tm_experts=512, tm_tokens=512)
        if last:
            return _residual_norm(y, x1, m, ln2_g[i], ln2_b[i], tm=512)
        x, *projected = _norm_in_projection(y, x1, m, ln2_g[i], ln2_b[i], latent_mod(i + 1),
                                            w_in[i + 1].astype(BF16), tm=512)

        attn_c = _context_attention(qc, kc, vc)
        yc = _context_fourier(fc)
        xc1, h2c, grp_c = _out_projection(attn_c, yc, xc, mc, w_out_b, w_four_bd, ln1_g[i], ln1_b[i],
                                          w_router_t, router_bias, tm=n_ctx)
        yc2 = _grouped_moe(h2c, grp_c, i, wg, wu, wd, w_router_grp, router_bias_grp, tm_experts=128,
                           tm_tokens=n_ctx)
        xc = _residual_norm(yc2, xc1, mc, ln2_g[i], ln2_b[i], tm=n_ctx)
    return x
```

```python
import functools
import math

import numpy as np
import jax
import jax.numpy as jnp
from jax import lax
from jax.experimental import pallas as pl
from jax.experimental.pallas import tpu as pltpu

D_MODEL = 1024
DEPTH = 2
GRID_W = 64
NA_HEADS = 8
HEAD_DIM = 64
NA_WIDTH = NA_HEADS * HEAD_DIM
WIN_ROWS = 8
WIN_COLS = 16
F_GROUPS = 8
F_GROUP_DIM = 64
F_WIDTH = F_GROUPS * F_GROUP_DIM
IN_WIDTH = 3 * NA_WIDTH + F_WIDTH
N_EXPERTS = 16
N_GROUPS = 4
EXPERTS_PER_GROUP = N_EXPERTS // N_GROUPS
D_EXPERT = 256
N_MOD = 6
DEEPNORM_ALPHA = (2.0 * DEPTH) ** 0.25
LN_EPS = 1e-6

F32 = jnp.float32
BF16 = jnp.bfloat16

V7X_VMEM_BYTES = 64 * 1024 * 1024
VMEM_LIMIT_BYTES = (V7X_VMEM_BYTES * 3) // 4
LANES = 128
SUBLANES = 8
HEADS_PER_STEP = LANES // HEAD_DIM
assert HEADS_PER_STEP == 2
MASK_VALUE = -1e30

Q_ROWS = 8
K_ROWS = 16
K_CHUNK_ROWS = 4
N_K_CHUNKS = K_ROWS // K_CHUNK_ROWS


def _cparams(n_grid_dims):
    return pltpu.CompilerParams(dimension_semantics=("arbitrary",) * n_grid_dims,
                                vmem_limit_bytes=VMEM_LIMIT_BYTES)


def _layer_norm(x):
    mu = jnp.mean(x, axis=-1, keepdims=True)
    xc = x - mu
    var = jnp.mean(xc * xc, axis=-1, keepdims=True)
    return xc * lax.rsqrt(var + LN_EPS)


SLAB_ROWS = D_MODEL // LANES
assert SLAB_ROWS == SUBLANES


def _to_slabs(ref, x):
    for j in range(SLAB_ROWS):
        ref[pl.ds(j, x.shape[0], stride=SLAB_ROWS), :] = x[:, j * LANES:(j + 1) * LANES]


def _from_slabs(ref, n_tokens):
    return jnp.concatenate([ref[pl.ds(j, n_tokens, stride=SLAB_ROWS), :] for j in range(SLAB_ROWS)], axis=-1)


def _dot(a, b):
    return jnp.dot(a, b, preferred_element_type=F32)


def _dot_nt(a, b):
    return lax.dot_general(a, b, (((1,), (1,)), ((), ())), preferred_element_type=F32)


def _mod_kernel(c_ref, w_ref, b_ref, o_ref):
    c = c_ref[...]
    a = c * jax.nn.sigmoid(c)
    o_ref[0] = jnp.dot(a, w_ref[0], preferred_element_type=F32, precision=lax.Precision.HIGHEST) + b_ref[0]


def _modulation(cvec, w_mod, b_mod):
    n_col_blocks = 4
    wc = (N_MOD * D_MODEL) // n_col_blocks
    rows = cvec.shape[0]
    return pl.pallas_call(
        _mod_kernel,
        grid=(DEPTH, n_col_blocks),
        in_specs=[pl.BlockSpec((rows, D_MODEL), lambda i, j: (0, 0)),
                  pl.BlockSpec((1, D_MODEL, wc), lambda i, j: (i, 0, j)),
                  pl.BlockSpec((1, 1, wc), lambda i, j: (i, 0, j))],
        out_specs=pl.BlockSpec((1, rows, wc), lambda i, j: (i, 0, j)),
        out_shape=jax.ShapeDtypeStruct((DEPTH, rows, N_MOD * D_MODEL), F32),
        compiler_params=_cparams(2),
        name="modulation",
    )(cvec, w_mod, b_mod.reshape(DEPTH, 1, N_MOD * D_MODEL))


def _modulate_project(x, m_ref, w_ref, q_ref, k_ref, v_ref, f_ref):
    h = _layer_norm(x) * (1.0 + m_ref[0, 1:2, :]) + m_ref[0, 0:1, :]
    p = _dot(h.astype(BF16), w_ref[...])
    q_ref[0] = (p[:, :NA_WIDTH] * (HEAD_DIM ** -0.5)).astype(BF16)
    k_ref[0] = p[:, NA_WIDTH:2 * NA_WIDTH].astype(BF16)
    v_ref[0] = p[:, 2 * NA_WIDTH:3 * NA_WIDTH].astype(BF16)
    f_ref[0] = p[:, 3 * NA_WIDTH:]


def _proj_kernel(x_ref, m_ref, w_ref, q_ref, k_ref, v_ref, f_ref):
    _modulate_project(x_ref[0], m_ref, w_ref, q_ref, k_ref, v_ref, f_ref)


def _norm_proj_kernel(back_ref, y_hbm, lslot_ref, x1_ref, m_ref, g_ref, b_ref, m_next_ref, w_ref,
                      x_ref, q_ref, k_ref, v_ref, f_ref, ybuf, sems, *, tiles_per_batch, n_steps, lr):
    step = pl.program_id(0) * tiles_per_batch + pl.program_id(1)
    y = _gather_unsort(back_ref, lslot_ref, y_hbm, ybuf, sems, step, n_steps, lr)
    z = DEEPNORM_ALPHA * x1_ref[0] + m_ref[0, 5:6, :] * y
    x = _layer_norm(z) * g_ref[...] + b_ref[...]
    x_ref[0] = x
    _modulate_project(x, m_next_ref, w_ref, q_ref, k_ref, v_ref, f_ref)


def _norm_in_projection(y_slots, back_chunk, lslot, x1, m, ln_g, ln_b, m_next, w_in_bf16, tm):
    bsz, length, _ = x1.shape
    tiles_per_batch = length // tm
    lr = _local_rows(tm)
    out = jax.ShapeDtypeStruct((bsz, length, NA_WIDTH), BF16)
    out_f = jax.ShapeDtypeStruct((bsz, length, F_WIDTH), F32)
    out_x = jax.ShapeDtypeStruct((bsz, length, D_MODEL), F32)
    row = lambda b, i, back_ref: (b, i, 0)
    const2 = lambda b, i, back_ref: (0, 0)
    mod_spec = pl.BlockSpec((1, N_MOD, D_MODEL), lambda b, i, back_ref: (b, 0, 0))
    o_spec = pl.BlockSpec((1, tm, NA_WIDTH), row)
    x_spec = pl.BlockSpec((1, tm, D_MODEL), row)
    grid_spec = pltpu.PrefetchScalarGridSpec(
        num_scalar_prefetch=1,
        grid=(bsz, tiles_per_batch),
        in_specs=[pl.BlockSpec(memory_space=pl.ANY),
                  pl.BlockSpec((1, SUBLANES, tm), lambda b, i, back_ref: (b, 0, i)),
                  x_spec, mod_spec,
                  pl.BlockSpec((1, D_MODEL), const2), pl.BlockSpec((1, D_MODEL), const2),
                  mod_spec, pl.BlockSpec((D_MODEL, IN_WIDTH), const2)],
        out_specs=[x_spec, o_spec, o_spec, o_spec, o_spec],
        scratch_shapes=[pltpu.VMEM((2, lr, D_MODEL), BF16), pltpu.SemaphoreType.DMA((2,))],
    )
    return pl.pallas_call(
        functools.partial(_norm_proj_kernel, tiles_per_batch=tiles_per_batch, n_steps=bsz * tiles_per_batch,
                          lr=lr),
        grid_spec=grid_spec,
        out_shape=[out_x, out, out, out, out_f],
        compiler_params=_cparams(2),
        name="moe_norm_in_proj",
    )(back_chunk, y_slots, lslot, x1, m, ln_g.reshape(1, D_MODEL), ln_b.reshape(1, D_MODEL), m_next, w_in_bf16)


def _in_projection(x, m, w_in_bf16, tm):
    bsz, length, _ = x.shape
    out = jax.ShapeDtypeStruct((bsz, length, NA_WIDTH), BF16)
    out_f = jax.ShapeDtypeStruct((bsz, length, F_WIDTH), F32)
    o_spec = pl.BlockSpec((1, tm, NA_WIDTH), lambda b, i: (b, i, 0))
    return pl.pallas_call(
        _proj_kernel,
        grid=(bsz, length // tm),
        in_specs=[pl.BlockSpec((1, tm, D_MODEL), lambda b, i: (b, i, 0)),
                  pl.BlockSpec((1, N_MOD, D_MODEL), lambda b, i: (b, 0, 0)),
                  pl.BlockSpec((D_MODEL, IN_WIDTH), lambda b, i: (0, 0))],
        out_specs=[o_spec, o_spec, o_spec, o_spec],
        out_shape=[out, out, out, out_f],
        compiler_params=_cparams(2),
        name="ln_mod_in_proj",
    )(x, m, w_in_bf16)


def _head_lanes(h):
    lane = lax.broadcasted_iota(jnp.int32, (1, LANES), 1)
    return (lane >= HEAD_DIM * h) & (lane < HEAD_DIM * (h + 1))


def _scores_pass(q, h, tiles, s_ref):
    qh = jnp.where(_head_lanes(h), q, jnp.zeros_like(q))
    m = None
    t = tiles[0][0].shape[0]
    for j, (k, _, bias) in enumerate(tiles):
        s = _dot_nt(qh, k)
        if bias is not None:
            s = s + bias
        s_ref[:, j * t:(j + 1) * t] = s
        mj = jnp.max(s, axis=-1, keepdims=True)
        m = mj if m is None else jnp.maximum(m, mj)
    return m


def _pv_pass(h, tiles, s_ref, m):
    o = None
    t = tiles[0][0].shape[0]
    in_head = _head_lanes(h)
    for j, (_, v, _) in enumerate(tiles):
        p = jnp.exp(s_ref[:, j * t:(j + 1) * t] - m)
        oj = _dot(p.astype(BF16), jnp.where(in_head, v, jnp.ones_like(v)))
        o = oj if o is None else o + oj
    return o / pltpu.roll(o, HEAD_DIM, axis=1)


def _attention_units(units):
    outs = []
    maxima = [_scores_pass(*units[0])]
    for u in range(len(units)):
        if u + 1 < len(units):
            maxima.append(_scores_pass(*units[u + 1]))
        _, h, tiles, s_ref = units[u]
        outs.append(_pv_pass(h, tiles, s_ref, maxima[u]))
    return outs


def _merge_heads(outs):
    merged = outs[0]
    for h in range(1, len(outs)):
        merged = jnp.where(_head_lanes(h), outs[h], merged)
    return merged


DR_PAD = 2 * WIN_ROWS
DC_PAD = 2 * WIN_COLS


def _toeplitz_kernel(r_ref, sel_ref, mask_ref, o_ref):
    n = r_ref.shape[0]
    for qc in range(GRID_W):
        block = jnp.dot(r_ref[...], sel_ref[qc], preferred_element_type=F32, precision=lax.Precision.HIGHEST)
        o_ref[pl.ds(qc, n, stride=GRID_W), :] = block + mask_ref[qc]


def _rpb_toeplitz(rpb):
    depth, heads, n_dr, n_dc = rpb.shape
    qc = np.arange(GRID_W)[:, None]
    kc = (np.arange(LANES) % GRID_W)[None, :]
    cs = np.clip(qc - WIN_COLS // 2, 0, GRID_W - WIN_COLS)
    col_valid = (kc >= cs) & (kc < cs + WIN_COLS)
    dc = kc - qc + WIN_COLS - 1
    select = (np.arange(DC_PAD)[None, :, None] == dc[:, None, :]) & col_valid[:, None, :]
    mask = np.where(col_valid, 0.0, MASK_VALUE)[:, None, :]
    r = jnp.pad(rpb.astype(F32), ((0, 0), (0, 0), (0, DR_PAD - n_dr), (0, DC_PAD - n_dc)))
    n = depth * heads * DR_PAD
    const3 = lambda: (0, 0, 0)
    out = pl.pallas_call(
        _toeplitz_kernel,
        grid=(),
        in_specs=[pl.BlockSpec((n, DC_PAD), lambda: (0, 0)),
                  pl.BlockSpec((GRID_W, DC_PAD, LANES), const3), pl.BlockSpec((GRID_W, 1, LANES), const3)],
        out_specs=pl.BlockSpec((n * GRID_W, LANES), lambda: (0, 0)),
        out_shape=jax.ShapeDtypeStruct((n * GRID_W, LANES), F32),
        compiler_params=pltpu.CompilerParams(vmem_limit_bytes=VMEM_LIMIT_BYTES),
        name="rpb_toeplitz",
    )(r.reshape(n, DC_PAD), jnp.asarray(select, F32), jnp.asarray(mask, F32))
    return out.reshape(depth * heads, DR_PAD, GRID_W, LANES)


Q_CHUNK_ROWS = 4
WINDOW_CHUNKS = 3
N_Q_CHUNKS = Q_ROWS // Q_CHUNK_ROWS
assert Q_CHUNK_ROWS == K_CHUNK_ROWS and WINDOW_CHUNKS * K_CHUNK_ROWS >= Q_CHUNK_ROWS + WIN_ROWS - 1
assert N_Q_CHUNKS == 2 and N_K_CHUNKS == 4


def _key_row_start(rb, rows):
    return int(np.clip(rb * Q_ROWS - (K_ROWS - Q_ROWS) // 2, 0, rows - K_ROWS))


def _window_uses_last_chunk(rb, c, n_rb):
    if c == 0:
        return rb == n_rb - 1
    return rb != 0


def _window_key_rows(rb, c, rows):
    var = N_K_CHUNKS - 1 if _window_uses_last_chunk(rb, c, rows // Q_ROWS) else 0
    k0 = _key_row_start(rb, rows)
    return [k0 + K_CHUNK_ROWS * j + i for j in (var, 1, 2) for i in range(K_CHUNK_ROWS)]


def _row_window(qr, rows):
    kh = min(WIN_ROWS, rows)
    rs = int(np.clip(qr - kh // 2, 0, rows - kh))
    return rs, rs + kh


def _check_windows(rows):
    def relative(rb):
        base = rb * Q_ROWS
        return [([kr - base for kr in _window_key_rows(rb, c, rows)],
                 [tuple(r - base for r in _row_window(base + c * Q_CHUNK_ROWS + qi, rows))
                  for qi in range(Q_CHUNK_ROWS)]) for c in range(N_Q_CHUNKS)]

    n_rb = rows // Q_ROWS
    for rb in range(n_rb):
        assert rb in (0, n_rb - 1) or relative(rb) == relative(1), rb
        for c in range(N_Q_CHUNKS):
            have = set(_window_key_rows(rb, c, rows))
            for qi in range(Q_CHUNK_ROWS):
                lo, hi = _row_window(rb * Q_ROWS + c * Q_CHUNK_ROWS + qi, rows)
                assert set(range(lo, hi)) <= have, (rb, c, qi)


def _build_bias_tables(t_ref, bias_scr, rows):
    n_rb = rows // Q_ROWS
    left = lax.broadcasted_iota(jnp.int32, (GRID_W, LANES), 1) < GRID_W
    masked = jnp.full((GRID_W, LANES), MASK_VALUE, F32)
    for variant, rb in enumerate((0, 1, n_rb - 1)):
        for c in range(N_Q_CHUNKS):
            key_rows = _window_key_rows(rb, c, rows)
            for h in range(HEADS_PER_STEP):
                for qi in range(Q_CHUNK_ROWS):
                    qr = rb * Q_ROWS + c * Q_CHUNK_ROWS + qi
                    lo, hi = _row_window(qr, rows)
                    for p in range(len(key_rows) // 2):
                        pair = [t_ref[h, kr - qr + WIN_ROWS - 1] if lo <= kr < hi else None
                                for kr in key_rows[2 * p:2 * p + 2]]
                        if pair[0] is None and pair[1] is None:
                            block = masked
                        else:
                            block = jnp.where(left, masked if pair[0] is None else pair[0],
                                              masked if pair[1] is None else pair[1])
                        bias_scr[variant, c, h, qi * GRID_W:(qi + 1) * GRID_W, p * LANES:(p + 1) * LANES] = block


def _na_kernel(q_ref, k0, k1, k2, k3, v0, v1, v2, v3, kc_ref, vc_ref, t_ref, o_ref, bias_scr, s_scr, *, rows):
    n_rb = rows // Q_ROWS
    b, rb = pl.program_id(1), pl.program_id(2)

    @pl.when((b == 0) & (rb == 0))
    def _():
        _build_bias_tables(t_ref, bias_scr, rows)

    variant = jnp.where(rb == 0, 0, jnp.where(rb == n_rb - 1, 2, 1))
    tq = Q_CHUNK_ROWS * GRID_W
    tk = K_CHUNK_ROWS * GRID_W
    units = []
    for c in range(N_Q_CHUNKS):
        use_last = (rb == n_rb - 1) if c == 0 else (rb != 0)
        window = [(jnp.where(use_last, k3[0], k0[0]), jnp.where(use_last, v3[0], v0[0])),
                  (k1[0], v1[0]), (k2[0], v2[0])]
        q = q_ref[0, c * tq:(c + 1) * tq, :]
        for h in range(HEADS_PER_STEP):
            lat = [(k, v, bias_scr[variant, c, h, :, j * tk:(j + 1) * tk]) for j, (k, v) in enumerate(window)]
            units.append((q, h, lat + [(kc_ref[0], vc_ref[0], None)], s_scr.at[c, h]))
    outs = _attention_units(units)
    for c in range(N_Q_CHUNKS):
        o = _merge_heads(outs[c * HEADS_PER_STEP:(c + 1) * HEADS_PER_STEP])
        o_ref[0, c * tq:(c + 1) * tq, :] = o.astype(BF16)


def _neighborhood_attention(q, k, v, kc, vc, toeplitz, layer):
    bsz, length, _ = q.shape
    rows = length // GRID_W
    n_rb = rows // Q_ROWS
    assert rows % Q_ROWS == 0 and rows >= K_ROWS + Q_ROWS and n_rb >= 3
    _check_windows(rows)
    n_ctx = kc.shape[1]
    tq = Q_ROWS * GRID_W
    tk = K_CHUNK_ROWS * GRID_W
    assert n_ctx == tk
    max_chunk = (rows - K_ROWS) // K_CHUNK_ROWS
    half = (K_ROWS - Q_ROWS) // 2 // K_CHUNK_ROWS

    def kv_spec(j):
        def index(hp, b, rb):
            start = jnp.clip(rb * (Q_ROWS // K_CHUNK_ROWS) - half, 0, max_chunk)
            return (b, start + j, hp)
        return pl.BlockSpec((1, tk, LANES), index)

    ctx_spec = pl.BlockSpec((1, n_ctx, LANES), lambda hp, b, rb: (b, 0, hp))
    q_spec = pl.BlockSpec((1, tq, LANES), lambda hp, b, rb: (b, rb, hp))
    window = WINDOW_CHUNKS * K_CHUNK_ROWS * GRID_W
    return pl.pallas_call(
        functools.partial(_na_kernel, rows=rows),
        grid=(NA_HEADS // HEADS_PER_STEP, bsz, n_rb),
        in_specs=([q_spec] + [kv_spec(j) for j in range(N_K_CHUNKS)] + [kv_spec(j) for j in range(N_K_CHUNKS)]
                  + [ctx_spec, ctx_spec,
                     pl.BlockSpec((HEADS_PER_STEP, DR_PAD, GRID_W, LANES),
                                  lambda hp, b, rb: (layer * (NA_HEADS // HEADS_PER_STEP) + hp, 0, 0, 0))]),
        out_specs=q_spec,
        out_shape=jax.ShapeDtypeStruct((bsz, length, NA_WIDTH), BF16),
        scratch_shapes=[pltpu.VMEM((3, N_Q_CHUNKS, HEADS_PER_STEP, Q_CHUNK_ROWS * GRID_W, window), F32),
                        pltpu.VMEM((N_Q_CHUNKS, HEADS_PER_STEP, Q_CHUNK_ROWS * GRID_W, window + n_ctx), F32)],
        compiler_params=_cparams(3),
        name="neighborhood_attention",
    )(q, *([k] * N_K_CHUNKS), *([v] * N_K_CHUNKS), kc, vc, toeplitz)


def _ctx_attn_kernel(q_ref, k_ref, v_ref, o_ref, s_scr):
    tiles = [(k_ref[0], v_ref[0], None)]
    outs = _attention_units([(q_ref[0], h, tiles, s_scr.at[h]) for h in range(HEADS_PER_STEP)])
    o_ref[0] = _merge_heads(outs).astype(BF16)


def _context_attention(qc, kc, vc):
    bsz, n_ctx, _ = qc.shape
    spec = pl.BlockSpec((1, n_ctx, LANES), lambda b, hp: (b, 0, hp))
    return pl.pallas_call(
        _ctx_attn_kernel,
        grid=(bsz, NA_HEADS // HEADS_PER_STEP),
        in_specs=[spec, spec, spec],
        out_specs=spec,
        out_shape=jax.ShapeDtypeStruct((bsz, n_ctx, NA_WIDTH), BF16),
        scratch_shapes=[pltpu.VMEM((HEADS_PER_STEP, n_ctx, n_ctx), F32)],
        compiler_params=_cparams(2),
        name="context_attention",
    )(qc, kc, vc)


def _dft_cos_sin(n):
    ang = 2.0 * np.pi * np.outer(np.arange(n), np.arange(n)) / n
    return np.cos(ang), np.sin(ang)


def _bf16_table(a):
    return jnp.asarray(a, F32).astype(BF16)


def _channel_dft_matrix(n_groups=F_GROUPS):
    c, s = _dft_cos_sin(F_GROUP_DIM)
    scale = F_GROUP_DIM ** -0.5
    eye = np.eye(n_groups)
    return np.concatenate([np.kron(eye, c), np.kron(eye, s)], axis=1) * scale


def _fft_stage1_kernel(f_ref, perm_ref, w1_ref, cs_ref, sc_ref, tc_ref, ts_ref, zr_ref, zi_ref, *, n_slow, nt):
    x = f_ref[0].reshape(n_slow * nt, F_WIDTH).astype(BF16)
    x = _dot(perm_ref[...], x).astype(BF16)
    ab = [_dot(x[:, p * LANES:(p + 1) * LANES], w1_ref[...]).astype(BF16) for p in range(F_WIDTH // LANES)]
    a_all = jnp.concatenate([blk[:, :LANES] for blk in ab], axis=-1)
    b_all = jnp.concatenate([blk[:, LANES:] for blk in ab], axis=-1)
    for t in range(nt):
        a = a_all[t * n_slow:(t + 1) * n_slow]
        b = b_all[t * n_slow:(t + 1) * n_slow]
        z = _dot(cs_ref[...], a) + _dot(sc_ref[...], b)
        zr, zi = z[:n_slow], z[n_slow:]
        c, s = tc_ref[t], ts_ref[t]
        zr_ref[0, t] = zr * c - zi * s
        zi_ref[0, t] = zr * s + zi * c


def _fft_stage2_kernel(zr_ref, zi_ref, f_ref, y_ref, *, n_fast, kb):
    rhs = jnp.concatenate([zr_ref[0].reshape(n_fast * kb, F_WIDTH).astype(BF16),
                           zi_ref[0].reshape(n_fast * kb, F_WIDTH).astype(BF16)], axis=0)
    y_ref[0] = _dot(f_ref[...], rhs).reshape(n_fast, kb, F_WIDTH)


def _fourier_positions(f, n_slow, n_fast):
    bsz, n, _ = f.shape
    assert n == n_slow * n_fast
    nt = SUBLANES
    kb = SUBLANES
    w1 = _bf16_table(_channel_dft_matrix(LANES // F_GROUP_DIM))
    perm = _bf16_table(np.eye(n_slow * nt).reshape(n_slow, nt, n_slow * nt).transpose(1, 0, 2)
                       .reshape(n_slow * nt, n_slow * nt))
    c1, s1 = _dft_cos_sin(n_slow)
    sc1 = n_slow ** -0.5
    cs = _bf16_table(np.concatenate([c1, s1], axis=0) * sc1)
    sc = _bf16_table(np.concatenate([-s1, c1], axis=0) * sc1)
    tw = 2.0 * np.pi * np.outer(np.arange(n_fast), np.arange(n_slow)) / n
    tc = jnp.asarray(np.cos(tw)[:, :, None], F32)
    ts = jnp.asarray(np.sin(tw)[:, :, None], F32)
    z_shape = jax.ShapeDtypeStruct((bsz, n_fast, n_slow, F_WIDTH), F32)
    z_spec = pl.BlockSpec((1, nt, n_slow, F_WIDTH), lambda b, j: (b, j, 0, 0))
    const2 = lambda b, j: (0, 0)
    tw_spec = pl.BlockSpec((nt, n_slow, 1), lambda b, j: (j, 0, 0))
    zr, zi = pl.pallas_call(
        functools.partial(_fft_stage1_kernel, n_slow=n_slow, nt=nt),
        grid=(bsz, n_fast // nt),
        in_specs=[pl.BlockSpec((1, n_slow, nt, F_WIDTH), lambda b, j: (b, 0, j, 0)),
                  pl.BlockSpec(perm.shape, const2),
                  pl.BlockSpec(w1.shape, const2), pl.BlockSpec(cs.shape, const2), pl.BlockSpec(sc.shape, const2),
                  tw_spec, tw_spec],
        out_specs=[z_spec, z_spec],
        out_shape=[z_shape, z_shape],
        compiler_params=_cparams(2),
        name="fnet_stage1",
    )(f.reshape(bsz, n_slow, n_fast, F_WIDTH), perm, w1, cs, sc, tc, ts)

    c2, s2 = _dft_cos_sin(n_fast)
    sc2 = n_fast ** -0.5
    eye = np.eye(kb)
    f2 = _bf16_table(np.concatenate([np.kron(c2, eye), np.kron(-s2, eye)], axis=1) * sc2)
    blk = pl.BlockSpec((1, n_fast, kb, F_WIDTH), lambda b, j: (b, 0, j, 0))
    y = pl.pallas_call(
        functools.partial(_fft_stage2_kernel, n_fast=n_fast, kb=kb),
        grid=(bsz, n_slow // kb),
        in_specs=[blk, blk, pl.BlockSpec(f2.shape, const2)],
        out_specs=blk,
        out_shape=jax.ShapeDtypeStruct((bsz, n_fast, n_slow, F_WIDTH), F32),
        compiler_params=_cparams(2),
        name="fnet_stage2",
    )(zr, zi, f2)
    return y.reshape(bsz, n, F_WIDTH)


def _second_largest_sum(a, b, c, d):
    mab, nab = jnp.maximum(a, b), jnp.minimum(a, b)
    mcd, ncd = jnp.maximum(c, d), jnp.minimum(c, d)
    return jnp.maximum(mab, mcd) + jnp.maximum(jnp.minimum(mab, mcd), jnp.maximum(nab, ncd))


def _selected_group(sb_rows):
    epg = EXPERTS_PER_GROUP
    g_score = [_second_largest_sum(*sb_rows[g * epg:(g + 1) * epg]) for g in range(N_GROUPS)]
    best = functools.reduce(jnp.maximum, g_score)
    group = jnp.full_like(best, float(N_GROUPS - 1))
    for g in range(N_GROUPS - 2, -1, -1):
        group = jnp.where(g_score[g] == best, float(g), group)
    return group


def _top2_gates(cand_s, cand_sb):
    n = len(cand_s)
    w = []
    for j in range(n):
        rank = jnp.zeros_like(cand_sb[j])
        for i in range(n):
            if i == j:
                continue
            ahead = (cand_sb[i] > cand_sb[j]) | ((cand_sb[i] == cand_sb[j]) & (i < j))
            rank = rank + jnp.where(ahead, 1.0, 0.0)
        w.append(jnp.where(rank < 2.0, cand_s[j], 0.0))
    total = functools.reduce(jnp.add, w)
    return [wj / total for wj in w]


CHUNK_ROWS = 16


def _local_rows(tm):
    need = tm + N_GROUPS * (CHUNK_ROWS - 1) + CHUNK_ROWS
    return -(-need // LANES) * LANES


def _sort_matrix(lslot, lr):
    r_iota = lax.broadcasted_iota(jnp.int32, (lr, 1), 0).astype(F32)
    return jnp.where(r_iota == lslot, 1.0, 0.0).astype(BF16)


def _local_sort(group, h2b, tri_ref, lr):
    t = group.shape[1]
    g_iota = lax.broadcasted_iota(jnp.int32, (SUBLANES, 1), 0).astype(F32)
    onehot = g_iota == group
    prefix = _dot(jnp.where(onehot, 1.0, 0.0).astype(BF16), tri_ref[...])
    count = prefix[:, t - 1:t]
    padded = jnp.floor((count + (CHUNK_ROWS - 1.0)) * (1.0 / CHUNK_ROWS)) * CHUNK_ROWS
    lslot = jnp.zeros((1, t), F32)
    start = jnp.zeros((1, 1), F32)
    for g in range(N_GROUPS):
        lslot = jnp.where(onehot[g:g + 1], start + prefix[g:g + 1] - 1.0, lslot)
        start = start + padded[g:g + 1]
    return lslot, count, _dot(_sort_matrix(lslot, lr), h2b).astype(BF16)


def _out_kernel(a_ref, y_ref, x_ref, m_ref, wo_ref, bd_ref, g_ref, b_ref, wr_ref, rb_ref, tri_ref,
                x1_ref, h2_ref, lslot_ref, count_ref):
    y2 = _dot(y_ref[0].astype(BF16), bd_ref[...]).astype(BF16)
    o = _dot(a_ref[0], wo_ref[:NA_WIDTH, :]) + _dot(y2, wo_ref[NA_WIDTH:, :])
    z = DEEPNORM_ALPHA * x_ref[0] + m_ref[0, 2:3, :] * o
    x1 = _layer_norm(z) * g_ref[...] + b_ref[...]
    x1_ref[0] = x1
    h2 = (_layer_norm(x1) * (1.0 + m_ref[0, 4:5, :]) + m_ref[0, 3:4, :]).astype(BF16)
    sb = jax.nn.sigmoid(_dot_nt(wr_ref[...], h2)) + rb_ref[...]
    group = _selected_group([sb[e:e + 1] for e in range(N_EXPERTS)])
    lslot, count, h2_sorted = _local_sort(group, h2, tri_ref, h2_ref.shape[0])
    h2_ref[...] = h2_sorted
    lslot_ref[0] = jnp.broadcast_to(lslot, lslot_ref.shape[1:])
    count_ref[0] = jnp.broadcast_to(count, count_ref.shape[1:])


def _out_projection(attn, yf, x, m, w_out_bf16, w_four_bd, ln_g, ln_b, w_router_t, router_bias, tm):
    bsz, length, _ = x.shape
    tiles_per_batch = length // tm
    n_tiles = bsz * tiles_per_batch
    lr = _local_rows(tm)
    tri = _bf16_table(np.triu(np.ones((tm, tm))))
    row = lambda b, i: (b, i, 0)
    tile = lambda b, i: (b * tiles_per_batch + i, 0)
    const2 = lambda b, i: (0, 0)
    return pl.pallas_call(
        _out_kernel,
        grid=(bsz, tiles_per_batch),
        in_specs=[pl.BlockSpec((1, tm, NA_WIDTH), row), pl.BlockSpec((1, tm, F_WIDTH), row),
                  pl.BlockSpec((1, tm, D_MODEL), row),
                  pl.BlockSpec((1, N_MOD, D_MODEL), lambda b, i: (b, 0, 0)),
                  pl.BlockSpec((NA_WIDTH + F_WIDTH, D_MODEL), const2),
                  pl.BlockSpec((F_WIDTH, F_WIDTH), const2),
                  pl.BlockSpec((1, D_MODEL), const2), pl.BlockSpec((1, D_MODEL), const2),
                  pl.BlockSpec((N_EXPERTS, D_MODEL), const2), pl.BlockSpec((N_EXPERTS, 1), const2),
                  pl.BlockSpec((tm, tm), const2)],
        out_specs=[pl.BlockSpec((1, tm, D_MODEL), row),
                   pl.BlockSpec((lr, D_MODEL), tile),
                   pl.BlockSpec((1, SUBLANES, tm), lambda b, i: (b, 0, i)),
                   pl.BlockSpec((1, SUBLANES, LANES), lambda b, i: (b * tiles_per_batch + i, 0, 0))],
        out_shape=[jax.ShapeDtypeStruct((bsz, length, D_MODEL), F32),
                   jax.ShapeDtypeStruct((n_tiles * lr, D_MODEL), BF16),
                   jax.ShapeDtypeStruct((bsz, SUBLANES, length), F32),
                   jax.ShapeDtypeStruct((n_tiles, SUBLANES, LANES), F32)],
        compiler_params=_cparams(2),
        name="out_proj_norm_route",
    )(attn, yf, x, m, w_out_bf16, w_four_bd, ln_g.reshape(1, D_MODEL), ln_b.reshape(1, D_MODEL),
      w_router_t, router_bias.reshape(N_EXPERTS, 1), tri)


def _group_plan(group, tm):
    n = group.shape[0]
    n_slots = n + N_GROUPS * tm
    onehot = (group[:, None] == jnp.arange(N_GROUPS, dtype=jnp.int32)[None, :]).astype(jnp.int32)
    csum = jnp.cumsum(onehot, axis=0)
    rank = jnp.sum(onehot * csum, axis=1) - 1
    count = csum[-1]
    padded = ((count + tm - 1) // tm) * tm
    end = jnp.cumsum(padded)
    start = end - padded
    slot = jnp.sum(onehot * start[None, :], axis=1) + rank
    tile_start = jnp.arange(n_slots // tm, dtype=jnp.int32) * tm
    tile_group = jnp.minimum(jnp.sum((end[None, :] <= tile_start[:, None]).astype(jnp.int32), axis=1),
                             N_GROUPS - 1)
    tile_rows = jnp.clip((start + count)[tile_group] - tile_start, 0, tm)
    return slot, tile_group, tile_rows, start, start + count, end


ROW_DMA_UNROLL = 8


def _slab(ref, token):
    return ref.at[pl.ds(pl.multiple_of(token * SLAB_ROWS, SLAB_ROWS), SLAB_ROWS), :]


def _row_scatter(idx_ref, base, src_ref, dst_hbm, sem, n_rows, wait, inverse_ref=None, src_base=0):
    def one(r, priority):
        if wait:
            pltpu.make_async_copy(_slab(src_ref, 0), _slab(dst_hbm, 0), sem).wait()
        else:
            idx = idx_ref[base + r]
            if inverse_ref is not None:
                inverse_ref[idx] = base + r
            pltpu.make_async_copy(_slab(src_ref, src_base + r), _slab(dst_hbm, idx), sem).start(priority=priority)

    def chunk(j, carry):
        for u in range(ROW_DMA_UNROLL):
            one(j * ROW_DMA_UNROLL + u, u % 2)
        return carry

    def single(r, carry):
        one(r, 0)
        return carry

    n_chunks = n_rows // ROW_DMA_UNROLL
    lax.fori_loop(0, n_chunks, chunk, 0)
    lax.fori_loop(n_chunks * ROW_DMA_UNROLL, n_rows, single, 0)


def _to_slots_kernel(slot_ref, start_ref, rows_end_ref, end_ref, h_hbm, o_hbm, dst_ref, zbuf, hbuf, sems,
                     fetch_sems, *, tm, tm_slots, n_steps):
    step = pl.program_id(0)
    n_slots = dst_ref.shape[0]
    sem = sems.at[0]

    @pl.when(step == 0)
    def _():
        zbuf[...] = jnp.zeros_like(zbuf)

        def fill(slot0):
            rows = tm_slots * SLAB_ROWS
            return pltpu.make_async_copy(zbuf, o_hbm.at[pl.ds(pl.multiple_of(slot0 * SLAB_ROWS, rows), rows), :], sem)

        for g in range(N_GROUPS):
            @pl.when(end_ref[g] > start_ref[g])
            def _():
                fill(end_ref[g] - tm_slots).start()
        for g in range(N_GROUPS):
            @pl.when(end_ref[g] > start_ref[g])
            def _():
                fill(end_ref[g] - tm_slots).wait()

        def fill_unused(t, carry):
            fill(t * tm_slots).start()
            fill(t * tm_slots).wait()
            return carry
        lax.fori_loop(end_ref[N_GROUPS - 1] // tm_slots, n_slots // tm_slots, fill_unused, 0)

        def clear(p, carry):
            dst_ref[p] = 0
            return carry
        for g in range(N_GROUPS):
            lax.fori_loop(rows_end_ref[g], end_ref[g], clear, 0)
        lax.fori_loop(end_ref[N_GROUPS - 1], n_slots, clear, 0)

    def fetch(s):
        rows = tm * SLAB_ROWS
        return pltpu.make_async_copy(h_hbm.at[pl.ds(pl.multiple_of(s * rows, rows), rows), :],
                                     hbuf.at[s % 3], fetch_sems.at[s % 3])

    def scatter(s, wait):
        _row_scatter(slot_ref, s * tm, hbuf.at[s % 3], o_hbm, sems.at[s % 2], tm, wait,
                     inverse_ref=None if wait else dst_ref)

    @pl.when(step == 0)
    def _():
        fetch(0).start()

    @pl.when(step + 1 < n_steps)
    def _():
        fetch(step + 1).start()

    fetch(step).wait()
    scatter(step, wait=False)

    @pl.when(step >= 1)
    def _():
        scatter(step - 1, wait=True)

    @pl.when(step == n_steps - 1)
    def _():
        scatter(step, wait=True)


def _to_slots(h2_slabs, slot, start, rows_end, end, tm, tm_slots):
    n = h2_slabs.shape[0] // SLAB_ROWS
    n_slots = n + N_GROUPS * tm_slots
    grid_spec = pltpu.PrefetchScalarGridSpec(
        num_scalar_prefetch=4,
        grid=(n // tm,),
        in_specs=[pl.BlockSpec(memory_space=pl.ANY)],
        out_specs=[pl.BlockSpec(memory_space=pl.ANY), pl.BlockSpec(memory_space=pltpu.SMEM)],
        scratch_shapes=[pltpu.VMEM((tm_slots * SLAB_ROWS, LANES), F32),
                        pltpu.VMEM((3, tm * SLAB_ROWS, LANES), F32),
                        pltpu.SemaphoreType.DMA((2,)), pltpu.SemaphoreType.DMA((3,))],
    )
    return pl.pallas_call(
        functools.partial(_to_slots_kernel, tm=tm, tm_slots=tm_slots, n_steps=n // tm),
        grid_spec=grid_spec,
        out_shape=[jax.ShapeDtypeStruct((n_slots * SLAB_ROWS, LANES), F32),
                   jax.ShapeDtypeStruct((n_slots,), jnp.int32)],
        compiler_params=_cparams(1),
        name="moe_rows_to_slots",
    )(slot, start, rows_end, end, h2_slabs)


def _moe_group_kernel(dst_ref, tile_group_ref, tile_rows_ref, n_used_ref, h_ref, wg_ref, wu_ref, wd_ref, wr_ref,
                      rb_ref, y_hbm, ybuf, sems, *, tm, n_tiles):
    del tile_group_ref
    step = pl.program_id(0)
    n_used = n_used_ref[0]
    buf = step % 2

    def scatter(tile, wait):
        _row_scatter(dst_ref, tile * tm, ybuf.at[tile % 2], y_hbm, sems.at[tile % 2], tile_rows_ref[tile], wait)

    @pl.when((step >= 2) & (step < n_used))
    def _():
        scatter(step - 2, wait=True)

    @pl.when(step < n_used)
    def _():
        h = _from_slabs(h_ref, tm).astype(BF16)
        s = jax.nn.sigmoid(_dot(h, wr_ref[0]))
        sb = s + rb_ref[0]
        epg = EXPERTS_PER_GROUP
        gates = _top2_gates([s[:, j:j + 1] for j in range(epg)], [sb[:, j:j + 1] for j in range(epg)])
        acc = None
        gate_up = [(_dot(h, wg_ref[0]), _dot(h, wu_ref[0]))]
        for e in range(epg):
            if e + 1 < epg:
                gate_up.append((_dot(h, wg_ref[e + 1]), _dot(h, wu_ref[e + 1])))
            gate, up = gate_up[e]
            hid = (gate * jax.nn.sigmoid(gate)) * up * gates[e]
            y = _dot(hid.astype(BF16), wd_ref[e])
            acc = y if acc is None else acc + y
        _to_slabs(ybuf.at[buf], acc)
        scatter(step, wait=False)

    @pl.when(step == n_tiles - 1)
    def _():
        @pl.when(n_used >= 2)
        def _():
            scatter(n_used - 2, wait=True)

        @pl.when(n_used >= 1)
        def _():
            scatter(n_used - 1, wait=True)


def _moe_experts(h_slots, dst, tile_group, tile_rows, n_used, n_tokens, layer, w_gate, w_up, w_down,
                 w_router_grp, router_bias_grp, tm):
    n_slots = h_slots.shape[0] // SLAB_ROWS
    n_tiles = n_slots // tm
    epg = EXPERTS_PER_GROUP
    by_group = lambda i, dst_ref, tg_ref, tr_ref, nu_ref: (tg_ref[i], 0, 0)
    by_layer_group = lambda i, dst_ref, tg_ref, tr_ref, nu_ref: (layer * N_GROUPS + tg_ref[i], 0, 0)
    grid_spec = pltpu.PrefetchScalarGridSpec(
        num_scalar_prefetch=4,
        grid=(n_tiles,),
        in_specs=[pl.BlockSpec((tm * SLAB_ROWS, LANES),
                               lambda i, dst_ref, tg_ref, tr_ref, nu_ref: (jnp.minimum(i, nu_ref[0] - 1), 0)),
                  pl.BlockSpec((epg, D_MODEL, D_EXPERT), by_layer_group),
                  pl.BlockSpec((epg, D_MODEL, D_EXPERT), by_layer_group),
                  pl.BlockSpec((epg, D_EXPERT, D_MODEL), by_layer_group),
                  pl.BlockSpec((1, D_MODEL, LANES), by_group),
                  pl.BlockSpec((1, 1, LANES), by_group)],
        out_specs=pl.BlockSpec(memory_space=pl.ANY),
        scratch_shapes=[pltpu.VMEM((2, tm * SLAB_ROWS, LANES), F32), pltpu.SemaphoreType.DMA((2,))],
    )
    return pl.pallas_call(
        functools.partial(_moe_group_kernel, tm=tm, n_tiles=n_tiles),
        grid_spec=grid_spec,
        out_shape=jax.ShapeDtypeStruct((n_tokens * SLAB_ROWS, LANES), F32),
        compiler_params=_cparams(1),
        name="moe_group_experts",
    )(dst, tile_group, tile_rows, n_used, h_slots, w_gate, w_up, w_down, w_router_grp, router_bias_grp)


def _residual_norm_kernel(y_ref, x1_ref, m_ref, g_ref, b_ref, o_ref):
    y = _from_slabs(y_ref, x1_ref.shape[0])
    z = DEEPNORM_ALPHA * x1_ref[...] + m_ref[0, 5:6, :] * y
    o_ref[...] = _layer_norm(z) * g_ref[...] + b_ref[...]


def _residual_norm(y_slabs, x1, m, ln_g, ln_b, tm):
    bsz, length, _ = x1.shape
    n = bsz * length
    tiles_per_batch = length // tm
    const2 = lambda i: (0, 0)
    row = pl.BlockSpec((tm, D_MODEL), lambda i: (i, 0))
    out = pl.pallas_call(
        _residual_norm_kernel,
        grid=(n // tm,),
        in_specs=[pl.BlockSpec((tm * SLAB_ROWS, LANES), lambda i: (i, 0)), row,
                  pl.BlockSpec((1, N_MOD, D_MODEL), lambda i: (i // tiles_per_batch, 0, 0)),
                  pl.BlockSpec((1, D_MODEL), const2), pl.BlockSpec((1, D_MODEL), const2)],
        out_specs=row,
        out_shape=jax.ShapeDtypeStruct((n, D_MODEL), F32),
        compiler_params=_cparams(1),
        name="moe_residual_norm",
    )(y_slabs, x1.reshape(n, D_MODEL), m, ln_g.reshape(1, D_MODEL), ln_b.reshape(1, D_MODEL))
    return out.reshape(bsz, length, D_MODEL)


def _grouped_moe(h2, group_rows, layer, w_gate, w_up, w_down, w_router_grp, router_bias_grp, tm_experts,
                 tm_tokens):
    bsz, _, length = group_rows.shape
    n = bsz * length
    group = group_rows[:, 0, :].reshape(n).astype(jnp.int32)
    slot, tile_group, tile_rows, start, rows_end, end = _group_plan(group, tm_experts)
    h_slots, dst = _to_slots(h2, slot, start, rows_end, end, tm_tokens, tm_experts)
    n_used = (end[N_GROUPS - 1:] // tm_experts).astype(jnp.int32)
    return _moe_experts(h_slots, dst, tile_group, tile_rows, n_used, n, layer, w_gate, w_up, w_down, w_router_grp,
                        router_bias_grp, tm_experts)


def _run_plan(count, tm, lr, tm_slots):
    n_tiles = count.shape[0]
    ch = CHUNK_ROWS
    n_slots = -(-(n_tiles * (tm + N_GROUPS * (ch - 1)) + N_GROUPS * tm_slots) // tm_slots) * tm_slots
    zero_chunk = lr // ch - 1
    pad = ((count + ch - 1) // ch) * ch
    lstart = jnp.cumsum(pad, axis=1) - pad
    run_off = jnp.cumsum(pad, axis=0) - pad
    seg_len = jnp.sum(pad, axis=0)
    seg_pad = ((seg_len + tm_slots - 1) // tm_slots) * tm_slots
    seg_end = jnp.cumsum(seg_pad)
    seg_start = seg_end - seg_pad

    def group_of(slot):
        return jnp.minimum(jnp.sum((seg_end <= slot[..., None]).astype(jnp.int32), axis=-1), N_GROUPS - 1)

    s = jnp.arange(n_slots // ch, dtype=jnp.int32) * ch
    g = group_of(s)
    o = s - seg_start[g]
    run_end = (run_off + pad).T[g]
    i = jnp.minimum(jnp.sum((run_end <= o[:, None]).astype(jnp.int32), axis=1), n_tiles - 1)
    src_row = i * lr + lstart[i, g] + (o - run_off[i, g])
    src_chunk = jnp.where(o < seg_len[g], src_row // ch, zero_chunk).astype(jnp.int32)

    tile_group = group_of(jnp.arange(n_slots // tm_slots, dtype=jnp.int32) * tm_slots)
    n_used = (seg_end[N_GROUPS - 1:] // tm_slots).astype(jnp.int32)

    r = jnp.arange(lr // ch, dtype=jnp.int32) * ch
    gk = jnp.sum(((lstart + pad)[:, None, :] <= r[None, :, None]).astype(jnp.int32), axis=2)
    gc = jnp.minimum(gk, N_GROUPS - 1)
    slot = seg_start[gc] + jnp.take_along_axis(run_off, gc, axis=1) + (r[None, :] - jnp.take_along_axis(lstart, gc, axis=1))
    back_chunk = jnp.where(gk < N_GROUPS, slot // ch, 0).astype(jnp.int32)
    return src_chunk, tile_group.astype(jnp.int32), n_used, back_chunk.reshape(-1)


def _chunk_gather(idx_ref, tile, n_chunks, src_hbm, buf, sems, wait):
    slot = tile % 2
    rows = n_chunks * CHUNK_ROWS
    if wait:
        pltpu.make_async_copy(src_hbm.at[pl.ds(0, rows), :], buf.at[slot], sems.at[slot]).wait()
        return
    for c in range(n_chunks):
        src = pl.multiple_of(idx_ref[tile * n_chunks + c] * CHUNK_ROWS, CHUNK_ROWS)
        pltpu.make_async_copy(src_hbm.at[pl.ds(src, CHUNK_ROWS), :],
                              buf.at[slot, pl.ds(c * CHUNK_ROWS, CHUNK_ROWS), :], sems.at[slot]).start()


def _moe_runs_kernel(src_ref, tile_group_ref, n_used_ref, h_hbm, wg_ref, wu_ref, wd_ref, wr_ref, rb_ref, y_ref,
                     hbuf, sems, *, tm):
    del tile_group_ref
    step = pl.program_id(0)
    n_used = n_used_ref[0]
    n_chunks = tm // CHUNK_ROWS

    @pl.when((step == 0) & (n_used > 0))
    def _():
        _chunk_gather(src_ref, 0, n_chunks, h_hbm, hbuf, sems, wait=False)

    @pl.when(step + 1 < n_used)
    def _():
        _chunk_gather(src_ref, step + 1, n_chunks, h_hbm, hbuf, sems, wait=False)

    @pl.when(step < n_used)
    def _():
        _chunk_gather(src_ref, step, n_chunks, h_hbm, hbuf, sems, wait=True)
        h = hbuf[step % 2]
        s = jax.nn.sigmoid(_dot(h, wr_ref[0]))
        sb = s + rb_ref[0]
        epg = EXPERTS_PER_GROUP
        gates = _top2_gates([s[:, j:j + 1] for j in range(epg)], [sb[:, j:j + 1] for j in range(epg)])
        acc = None
        gate_up = [(_dot(h, wg_ref[0]), _dot(h, wu_ref[0]))]
        for e in range(epg):
            if e + 1 < epg:
                gate_up.append((_dot(h, wg_ref[e + 1]), _dot(h, wu_ref[e + 1])))
            gate, up = gate_up[e]
            hid = (gate * jax.nn.sigmoid(gate)) * up * gates[e]
            y = _dot(hid.astype(BF16), wd_ref[e])
            acc = y if acc is None else acc + y
        y_ref[...] = acc.astype(BF16)

    @pl.when(step >= n_used)
    def _():
        y_ref[...] = jnp.zeros_like(y_ref)


def _moe_runs(h_sorted, src_chunk, tile_group, n_used, layer, w_gate, w_up, w_down, w_router_grp,
              router_bias_grp, tm):
    n_slots = src_chunk.shape[0] * CHUNK_ROWS
    epg = EXPERTS_PER_GROUP
    by_group = lambda i, src_ref, tg_ref, nu_ref: (tg_ref[i], 0, 0)
    by_layer_group = lambda i, src_ref, tg_ref, nu_ref: (layer * N_GROUPS + tg_ref[i], 0, 0)
    grid_spec = pltpu.PrefetchScalarGridSpec(
        num_scalar_prefetch=3,
        grid=(n_slots // tm,),
        in_specs=[pl.BlockSpec(memory_space=pl.ANY),
                  pl.BlockSpec((epg, D_MODEL, D_EXPERT), by_layer_group),
                  pl.BlockSpec((epg, D_MODEL, D_EXPERT), by_layer_group),
                  pl.BlockSpec((epg, D_EXPERT, D_MODEL), by_layer_group),
                  pl.BlockSpec((1, D_MODEL, LANES), by_group),
                  pl.BlockSpec((1, 1, LANES), by_group)],
        out_specs=pl.BlockSpec((tm, D_MODEL), lambda i, src_ref, tg_ref, nu_ref: (i, 0)),
        scratch_shapes=[pltpu.VMEM((2, tm, D_MODEL), BF16), pltpu.SemaphoreType.DMA((2,))],
    )
    return pl.pallas_call(
        functools.partial(_moe_runs_kernel, tm=tm),
        grid_spec=grid_spec,
        out_shape=jax.ShapeDtypeStruct((n_slots, D_MODEL), BF16),
        compiler_params=_cparams(1),
        name="moe_group_experts",
    )(src_chunk, tile_group, n_used, h_sorted, w_gate, w_up, w_down, w_router_grp, router_bias_grp)


def _gather_unsort(back_ref, lslot_ref, y_hbm, ybuf, sems, step, n_steps, lr):
    n_chunks = lr // CHUNK_ROWS

    @pl.when(step == 0)
    def _():
        _chunk_gather(back_ref, 0, n_chunks, y_hbm, ybuf, sems, wait=False)

    @pl.when(step + 1 < n_steps)
    def _():
        _chunk_gather(back_ref, step + 1, n_chunks, y_hbm, ybuf, sems, wait=False)

    _chunk_gather(back_ref, step, n_chunks, y_hbm, ybuf, sems, wait=True)
    sort = _sort_matrix(lslot_ref[0, 0:1, :], lr)
    return lax.dot_general(sort, ybuf[step % 2], (((0,), (0,)), ((), ())), preferred_element_type=F32)


def _residual_norm_kernel(back_ref, y_hbm, lslot_ref, x1_ref, m_ref, g_ref, b_ref, o_ref, ybuf, sems, *, n_steps, lr):
    y = _gather_unsort(back_ref, lslot_ref, y_hbm, ybuf, sems, pl.program_id(0), n_steps, lr)
    z = DEEPNORM_ALPHA * x1_ref[...] + m_ref[0, 5:6, :] * y
    o_ref[...] = _layer_norm(z) * g_ref[...] + b_ref[...]


def _residual_norm(y_slots, back_chunk, lslot, x1, m, ln_g, ln_b, tm):
    bsz, length, _ = x1.shape
    n = bsz * length
    tiles_per_batch = length // tm
    lr = _local_rows(tm)
    const2 = lambda i, back_ref: (0, 0)
    row = pl.BlockSpec((tm, D_MODEL), lambda i, back_ref: (i, 0))
    grid_spec = pltpu.PrefetchScalarGridSpec(
        num_scalar_prefetch=1,
        grid=(n // tm,),
        in_specs=[pl.BlockSpec(memory_space=pl.ANY),
                  pl.BlockSpec((1, SUBLANES, tm),
                               lambda i, back_ref: (i // tiles_per_batch, 0, i % tiles_per_batch)),
                  row, pl.BlockSpec((1, N_MOD, D_MODEL), lambda i, back_ref: (i // tiles_per_batch, 0, 0)),
                  pl.BlockSpec((1, D_MODEL), const2), pl.BlockSpec((1, D_MODEL), const2)],
        out_specs=row,
        scratch_shapes=[pltpu.VMEM((2, lr, D_MODEL), BF16), pltpu.SemaphoreType.DMA((2,))],
    )
    out = pl.pallas_call(
        functools.partial(_residual_norm_kernel, n_steps=n // tm, lr=lr),
        grid_spec=grid_spec,
        out_shape=jax.ShapeDtypeStruct((n, D_MODEL), F32),
        compiler_params=_cparams(1),
        name="moe_residual_norm",
    )(back_chunk, y_slots, lslot, x1.reshape(n, D_MODEL), m, ln_g.reshape(1, D_MODEL), ln_b.reshape(1, D_MODEL))
    return out.reshape(bsz, length, D_MODEL)


def _grouped_moe_runs(h_sorted, count_rows, layer, w_gate, w_up, w_down, w_router_grp, router_bias_grp,
                      tm_tokens, tm_slots):
    count = count_rows[:, :N_GROUPS, 0].astype(jnp.int32)
    src_chunk, tile_group, n_used, back_chunk = _run_plan(count, tm_tokens, _local_rows(tm_tokens), tm_slots)
    y_slots = _moe_runs(h_sorted, src_chunk, tile_group, n_used, layer, w_gate, w_up, w_down, w_router_grp,
                        router_bias_grp, tm_slots)
    return y_slots, back_chunk


def _ctx_fourier_kernel(f_ref, w1_ref, c_ref, s_ref, y_ref):
    ab = _dot(f_ref[0].astype(BF16), w1_ref[...]).astype(BF16)
    y = _dot(c_ref[...], ab[:, :F_WIDTH]) + _dot(s_ref[...], ab[:, F_WIDTH:])
    y_ref[0] = y.astype(BF16)


def _context_fourier(fc):
    bsz, n, _ = fc.shape
    w1 = _bf16_table(_channel_dft_matrix())
    c, s = _dft_cos_sin(n)
    cm = _bf16_table(c * n ** -0.5)
    sm = _bf16_table(-s * n ** -0.5)
    const2 = lambda b: (0, 0)
    blk = pl.BlockSpec((1, n, F_WIDTH), lambda b: (b, 0, 0))
    return pl.pallas_call(
        _ctx_fourier_kernel,
        grid=(bsz,),
        in_specs=[blk, pl.BlockSpec(w1.shape, const2), pl.BlockSpec(cm.shape, const2),
                  pl.BlockSpec(sm.shape, const2)],
        out_specs=blk,
        out_shape=jax.ShapeDtypeStruct((bsz, n, F_WIDTH), BF16),
        compiler_params=_cparams(1),
        name="context_fnet",
    )(fc, w1, cm, sm)


def _block_diag(w):
    g, c, _ = w.shape
    eye = jnp.eye(g, dtype=w.dtype)
    return (eye[:, None, :, None] * w[:, :, None, :]).reshape(g * c, g * c)


def kernel(x, c, ctx, c_ctx, w_mod, b_mod, w_in, rpb, w_four, w_out, ln1_g, ln1_b, ln2_g, ln2_b,
           w_router, router_bias, w_gate, w_up, w_down):
    bsz, length, _ = x.shape
    n_ctx = ctx.shape[1]
    rows = length // GRID_W

    cvec = jnp.concatenate([c, c_ctx[None, :], jnp.zeros((8 - bsz - 1, D_MODEL), F32)], axis=0)
    mods = _modulation(cvec, w_mod, b_mod)
    w_router_t = w_router.T.astype(BF16)
    toeplitz = _rpb_toeplitz(rpb)
    lane_pad = LANES - EXPERTS_PER_GROUP
    w_router_grp = jnp.pad(w_router.reshape(D_MODEL, N_GROUPS, EXPERTS_PER_GROUP).transpose(1, 0, 2),
                           ((0, 0), (0, 0), (0, lane_pad))).astype(BF16)
    router_bias_grp = jnp.pad(router_bias.astype(F32).reshape(N_GROUPS, 1, EXPERTS_PER_GROUP),
                              ((0, 0), (0, 0), (0, lane_pad)))

    wg = w_gate.astype(BF16).reshape(DEPTH * N_EXPERTS, D_MODEL, D_EXPERT)
    wu = w_up.astype(BF16).reshape(DEPTH * N_EXPERTS, D_MODEL, D_EXPERT)
    wd = w_down.astype(BF16).reshape(DEPTH * N_EXPERTS, D_EXPERT, D_MODEL)

    def latent_mod(i):
        return mods[i, :bsz].reshape(bsz, N_MOD, D_MODEL)

    xc = ctx
    projected = None
    for i in range(DEPTH):
        last = i == DEPTH - 1
        m = latent_mod(i)
        mc = jnp.broadcast_to(mods[i, bsz].reshape(1, N_MOD, D_MODEL), (bsz, N_MOD, D_MODEL))
        w_in_b = w_in[i].astype(BF16)
        w_out_b = w_out[i].astype(BF16)
        w_four_bd = _block_diag(w_four[i]).astype(BF16)

        q, k, v, f = projected if projected is not None else _in_projection(x, m, w_in_b, tm=512)
        qc, kc, vc, fc = _in_projection(xc, mc, w_in_b, tm=n_ctx)

        attn = _neighborhood_attention(q, k, v, kc, vc, toeplitz, i)
        yf = _fourier_positions(f, n_slow=rows, n_fast=GRID_W)
        x1, h2, lslot, cnt = _out_projection(attn, yf, x, m, w_out_b, w_four_bd, ln1_g[i], ln1_b[i],
                                             w_router_t, router_bias, tm=512)
        y, back = _grouped_moe_runs(h2, cnt, i, wg, wu, wd, w_router_grp, router_bias_grp,
                                    tm_tokens=512, tm_slots=512)
        if last:
            return _residual_norm(y, back, lslot, x1, m, ln2_g[i], ln2_b[i], tm=512)
        x, *projected = _norm_in_projection(y, back, lslot, x1, m, ln2_g[i], ln2_b[i], latent_mod(i + 1),
                                            w_in[i + 1].astype(BF16), tm=512)

        attn_c = _context_attention(qc, kc, vc)
        yc = _context_fourier(fc)
        xc1, h2c, lslot_c, cnt_c = _out_projection(attn_c, yc, xc, mc, w_out_b, w_four_bd, ln1_g[i], ln1_b[i],
                                                   w_router_t, router_bias, tm=n_ctx)
        yc2, back_c = _grouped_moe_runs(h2c, cnt_c, i, wg, wu, wd, w_router_grp, router_bias_grp,
                                        tm_tokens=n_ctx, tm_slots=128)
        xc = _residual_norm(yc2, back_c, lslot_c, xc1, mc, ln2_g[i], ln2_b[i], tm=n_ctx)
    return x
```

```python
import functools
import math

import numpy as np
import jax
import jax.numpy as jnp
from jax import lax
from jax.experimental import pallas as pl
from jax.experimental.pallas import tpu as pltpu

D_MODEL = 1024
DEPTH = 2
GRID_W = 64
NA_HEADS = 8
HEAD_DIM = 64
NA_WIDTH = NA_HEADS * HEAD_DIM
WIN_ROWS = 8
WIN_COLS = 16
F_GROUPS = 8
F_GROUP_DIM = 64
F_WIDTH = F_GROUPS * F_GROUP_DIM
IN_WIDTH = 3 * NA_WIDTH + F_WIDTH
N_EXPERTS = 16
N_GROUPS = 4
EXPERTS_PER_GROUP = N_EXPERTS // N_GROUPS
D_EXPERT = 256
N_MOD = 6
DEEPNORM_ALPHA = (2.0 * DEPTH) ** 0.25
LN_EPS = 1e-6

F32 = jnp.float32
BF16 = jnp.bfloat16

V7X_VMEM_BYTES = 64 * 1024 * 1024
VMEM_LIMIT_BYTES = (V7X_VMEM_BYTES * 3) // 4
LANES = 128
SUBLANES = 8
HEADS_PER_STEP = LANES // HEAD_DIM
assert HEADS_PER_STEP == 2
MASK_VALUE = -1e30

Q_ROWS = 8
K_ROWS = 16
K_CHUNK_ROWS = 4
N_K_CHUNKS = K_ROWS // K_CHUNK_ROWS


def _cparams(n_grid_dims):
    return pltpu.CompilerParams(dimension_semantics=("arbitrary",) * n_grid_dims,
                                vmem_limit_bytes=VMEM_LIMIT_BYTES)


def _layer_norm(x):
    mu = jnp.mean(x, axis=-1, keepdims=True)
    xc = x - mu
    var = jnp.mean(xc * xc, axis=-1, keepdims=True)
    return xc * lax.rsqrt(var + LN_EPS)


SLAB_ROWS = D_MODEL // LANES
assert SLAB_ROWS == SUBLANES


def _to_slabs(ref, x):
    for j in range(SLAB_ROWS):
        ref[pl.ds(j, x.shape[0], stride=SLAB_ROWS), :] = x[:, j * LANES:(j + 1) * LANES]


def _from_slabs(ref, n_tokens):
    return jnp.concatenate([ref[pl.ds(j, n_tokens, stride=SLAB_ROWS), :] for j in range(SLAB_ROWS)], axis=-1)


def _dot(a, b):
    return jnp.dot(a, b, preferred_element_type=F32)


def _dot_nt(a, b):
    return lax.dot_general(a, b, (((1,), (1,)), ((), ())), preferred_element_type=F32)


def _mod_kernel(c_ref, w_ref, b_ref, o_ref):
    c = c_ref[...]
    a = c * jax.nn.sigmoid(c)
    o_ref[0] = jnp.dot(a, w_ref[0], preferred_element_type=F32, precision=lax.Precision.HIGHEST) + b_ref[0]


def _modulation(cvec, w_mod, b_mod):
    n_col_blocks = 4
    wc = (N_MOD * D_MODEL) // n_col_blocks
    rows = cvec.shape[0]
    return pl.pallas_call(
        _mod_kernel,
        grid=(DEPTH, n_col_blocks),
        in_specs=[pl.BlockSpec((rows, D_MODEL), lambda i, j: (0, 0)),
                  pl.BlockSpec((1, D_MODEL, wc), lambda i, j: (i, 0, j)),
                  pl.BlockSpec((1, 1, wc), lambda i, j: (i, 0, j))],
        out_specs=pl.BlockSpec((1, rows, wc), lambda i, j: (i, 0, j)),
        out_shape=jax.ShapeDtypeStruct((DEPTH, rows, N_MOD * D_MODEL), F32),
        compiler_params=_cparams(2),
        name="modulation",
    )(cvec, w_mod, b_mod.reshape(DEPTH, 1, N_MOD * D_MODEL))


def _modulate_project(x, m_ref, w_ref, q_ref, k_ref, v_ref, f_ref):
    h = _layer_norm(x) * (1.0 + m_ref[0, 1:2, :]) + m_ref[0, 0:1, :]
    p = _dot(h.astype(BF16), w_ref[...])
    q_ref[0] = (p[:, :NA_WIDTH] * (HEAD_DIM ** -0.5)).astype(BF16)
    k_ref[0] = p[:, NA_WIDTH:2 * NA_WIDTH].astype(BF16)
    v_ref[0] = p[:, 2 * NA_WIDTH:3 * NA_WIDTH].astype(BF16)
    f_ref[0] = p[:, 3 * NA_WIDTH:]


def _proj_kernel(x_ref, m_ref, w_ref, q_ref, k_ref, v_ref, f_ref):
    _modulate_project(x_ref[0], m_ref, w_ref, q_ref, k_ref, v_ref, f_ref)


def _norm_proj_kernel(back_ref, y_hbm, lslot_ref, x1_ref, m_ref, g_ref, b_ref, m_next_ref, w_ref,
                      x_ref, q_ref, k_ref, v_ref, f_ref, ybuf, sems, *, tiles_per_batch, n_steps, lr):
    step = pl.program_id(0) * tiles_per_batch + pl.program_id(1)
    y = _gather_unsort(back_ref, lslot_ref, y_hbm, ybuf, sems, step, n_steps, lr)
    z = DEEPNORM_ALPHA * x1_ref[0] + m_ref[0, 5:6, :] * y
    x = _layer_norm(z) * g_ref[...] + b_ref[...]
    x_ref[0] = x
    _modulate_project(x, m_next_ref, w_ref, q_ref, k_ref, v_ref, f_ref)


def _norm_in_projection(y_slots, back_chunk, lslot, x1, m, ln_g, ln_b, m_next, w_in_bf16, tm):
    bsz, length, _ = x1.shape
    tiles_per_batch = length // tm
    lr = _local_rows(tm)
    out = jax.ShapeDtypeStruct((bsz, length, NA_WIDTH), BF16)
    out_f = jax.ShapeDtypeStruct((bsz, length, F_WIDTH), F32)
    out_x = jax.ShapeDtypeStruct((bsz, length, D_MODEL), F32)
    row = lambda b, i, back_ref: (b, i, 0)
    const2 = lambda b, i, back_ref: (0, 0)
    mod_spec = pl.BlockSpec((1, N_MOD, D_MODEL), lambda b, i, back_ref: (b, 0, 0))
    o_spec = pl.BlockSpec((1, tm, NA_WIDTH), row)
    x_spec = pl.BlockSpec((1, tm, D_MODEL), row)
    grid_spec = pltpu.PrefetchScalarGridSpec(
        num_scalar_prefetch=1,
        grid=(bsz, tiles_per_batch),
        in_specs=[pl.BlockSpec(memory_space=pl.ANY),
                  pl.BlockSpec((1, SUBLANES, tm), lambda b, i, back_ref: (b, 0, i)),
                  x_spec, mod_spec,
                  pl.BlockSpec((1, D_MODEL), const2), pl.BlockSpec((1, D_MODEL), const2),
                  mod_spec, pl.BlockSpec((D_MODEL, IN_WIDTH), const2)],
        out_specs=[x_spec, o_spec, o_spec, o_spec, o_spec],
        scratch_shapes=[pltpu.VMEM((2, lr, D_MODEL), BF16), pltpu.SemaphoreType.DMA((2,))],
    )
    return pl.pallas_call(
        functools.partial(_norm_proj_kernel, tiles_per_batch=tiles_per_batch, n_steps=bsz * tiles_per_batch,
                          lr=lr),
        grid_spec=grid_spec,
        out_shape=[out_x, out, out, out, out_f],
        compiler_params=_cparams(2),
        name="moe_norm_in_proj",
    )(back_chunk, y_slots, lslot, x1, m, ln_g.reshape(1, D_MODEL), ln_b.reshape(1, D_MODEL), m_next, w_in_bf16)


def _in_projection(x, m, w_in_bf16, tm):
    bsz, length, _ = x.shape
    out = jax.ShapeDtypeStruct((bsz, length, NA_WIDTH), BF16)
    out_f = jax.ShapeDtypeStruct((bsz, length, F_WIDTH), F32)
    o_spec = pl.BlockSpec((1, tm, NA_WIDTH), lambda b, i: (b, i, 0))
    return pl.pallas_call(
        _proj_kernel,
        grid=(bsz, length // tm),
        in_specs=[pl.BlockSpec((1, tm, D_MODEL), lambda b, i: (b, i, 0)),
                  pl.BlockSpec((1, N_MOD, D_MODEL), lambda b, i: (b, 0, 0)),
                  pl.BlockSpec((D_MODEL, IN_WIDTH), lambda b, i: (0, 0))],
        out_specs=[o_spec, o_spec, o_spec, o_spec],
        out_shape=[out, out, out, out_f],
        compiler_params=_cparams(2),
        name="ln_mod_in_proj",
    )(x, m, w_in_bf16)


def _head_lanes(h):
    lane = lax.broadcasted_iota(jnp.int32, (1, LANES), 1)
    return (lane >= HEAD_DIM * h) & (lane < HEAD_DIM * (h + 1))


def _scores_pass(q, h, tiles, s_ref):
    qh = jnp.where(_head_lanes(h), q, jnp.zeros_like(q))
    m = None
    t = tiles[0][0].shape[0]
    for j, (k, _, bias) in enumerate(tiles):
        s = _dot_nt(qh, k)
        if bias is not None:
            s = s + bias
        s_ref[:, j * t:(j + 1) * t] = s
        mj = jnp.max(s, axis=-1, keepdims=True)
        m = mj if m is None else jnp.maximum(m, mj)
    return m


def _pv_pass(h, tiles, s_ref, m):
    o = None
    t = tiles[0][0].shape[0]
    in_head = _head_lanes(h)
    for j, (_, v, _) in enumerate(tiles):
        p = jnp.exp(s_ref[:, j * t:(j + 1) * t] - m)
        oj = _dot(p.astype(BF16), jnp.where(in_head, v, jnp.ones_like(v)))
        o = oj if o is None else o + oj
    return o / pltpu.roll(o, HEAD_DIM, axis=1)


def _attention_units(units):
    outs = []
    maxima = [_scores_pass(*units[0])]
    for u in range(len(units)):
        if u + 1 < len(units):
            maxima.append(_scores_pass(*units[u + 1]))
        _, h, tiles, s_ref = units[u]
        outs.append(_pv_pass(h, tiles, s_ref, maxima[u]))
    return outs


def _merge_heads(outs):
    merged = outs[0]
    for h in range(1, len(outs)):
        merged = jnp.where(_head_lanes(h), outs[h], merged)
    return merged


DR_PAD = 2 * WIN_ROWS
DC_PAD = 2 * WIN_COLS


def _toeplitz_kernel(r_ref, sel_ref, mask_ref, o_ref):
    n = r_ref.shape[0]
    for qc in range(GRID_W):
        block = jnp.dot(r_ref[...], sel_ref[qc], preferred_element_type=F32, precision=lax.Precision.HIGHEST)
        o_ref[pl.ds(qc, n, stride=GRID_W), :] = block + mask_ref[qc]


def _rpb_toeplitz(rpb):
    depth, heads, n_dr, n_dc = rpb.shape
    qc = np.arange(GRID_W)[:, None]
    kc = (np.arange(LANES) % GRID_W)[None, :]
    cs = np.clip(qc - WIN_COLS // 2, 0, GRID_W - WIN_COLS)
    col_valid = (kc >= cs) & (kc < cs + WIN_COLS)
    dc = kc - qc + WIN_COLS - 1
    select = (np.arange(DC_PAD)[None, :, None] == dc[:, None, :]) & col_valid[:, None, :]
    mask = np.where(col_valid, 0.0, MASK_VALUE)[:, None, :]
    r = jnp.pad(rpb.astype(F32), ((0, 0), (0, 0), (0, DR_PAD - n_dr), (0, DC_PAD - n_dc)))
    n = depth * heads * DR_PAD
    const3 = lambda: (0, 0, 0)
    out = pl.pallas_call(
        _toeplitz_kernel,
        grid=(),
        in_specs=[pl.BlockSpec((n, DC_PAD), lambda: (0, 0)),
                  pl.BlockSpec((GRID_W, DC_PAD, LANES), const3), pl.BlockSpec((GRID_W, 1, LANES), const3)],
        out_specs=pl.BlockSpec((n * GRID_W, LANES), lambda: (0, 0)),
        out_shape=jax.ShapeDtypeStruct((n * GRID_W, LANES), F32),
        compiler_params=pltpu.CompilerParams(vmem_limit_bytes=VMEM_LIMIT_BYTES),
        name="rpb_toeplitz",
    )(r.reshape(n, DC_PAD), jnp.asarray(select, F32), jnp.asarray(mask, F32))
    return out.reshape(depth * heads, DR_PAD, GRID_W, LANES)


Q_CHUNK_ROWS = 4
WINDOW_CHUNKS = 3
N_Q_CHUNKS = Q_ROWS // Q_CHUNK_ROWS
assert Q_CHUNK_ROWS == K_CHUNK_ROWS and WINDOW_CHUNKS * K_CHUNK_ROWS >= Q_CHUNK_ROWS + WIN_ROWS - 1
assert N_Q_CHUNKS == 2 and N_K_CHUNKS == 4


def _key_row_start(rb, rows):
    return int(np.clip(rb * Q_ROWS - (K_ROWS - Q_ROWS) // 2, 0, rows - K_ROWS))


def _window_uses_last_chunk(rb, c, n_rb):
    if c == 0:
        return rb == n_rb - 1
    return rb != 0


def _window_key_rows(rb, c, rows):
    var = N_K_CHUNKS - 1 if _window_uses_last_chunk(rb, c, rows // Q_ROWS) else 0
    k0 = _key_row_start(rb, rows)
    return [k0 + K_CHUNK_ROWS * j + i for j in (var, 1, 2) for i in range(K_CHUNK_ROWS)]


def _row_window(qr, rows):
    kh = min(WIN_ROWS, rows)
    rs = int(np.clip(qr - kh // 2, 0, rows - kh))
    return rs, rs + kh


def _check_windows(rows):
    def relative(rb):
        base = rb * Q_ROWS
        return [([kr - base for kr in _window_key_rows(rb, c, rows)],
                 [tuple(r - base for r in _row_window(base + c * Q_CHUNK_ROWS + qi, rows))
                  for qi in range(Q_CHUNK_ROWS)]) for c in range(N_Q_CHUNKS)]

    n_rb = rows // Q_ROWS
    for rb in range(n_rb):
        assert rb in (0, n_rb - 1) or relative(rb) == relative(1), rb
        for c in range(N_Q_CHUNKS):
            have = set(_window_key_rows(rb, c, rows))
            for qi in range(Q_CHUNK_ROWS):
                lo, hi = _row_window(rb * Q_ROWS + c * Q_CHUNK_ROWS + qi, rows)
                assert set(range(lo, hi)) <= have, (rb, c, qi)


def _build_bias_tables(t_ref, bias_scr, rows):
    n_rb = rows // Q_ROWS
    left = lax.broadcasted_iota(jnp.int32, (GRID_W, LANES), 1) < GRID_W
    masked = jnp.full((GRID_W, LANES), MASK_VALUE, F32)
    for variant, rb in enumerate((0, 1, n_rb - 1)):
        for c in range(N_Q_CHUNKS):
            key_rows = _window_key_rows(rb, c, rows)
            for h in range(HEADS_PER_STEP):
                for qi in range(Q_CHUNK_ROWS):
                    qr = rb * Q_ROWS + c * Q_CHUNK_ROWS + qi
                    lo, hi = _row_window(qr, rows)
                    for p in range(len(key_rows) // 2):
                        pair = [t_ref[h, kr - qr + WIN_ROWS - 1] if lo <= kr < hi else None
                                for kr in key_rows[2 * p:2 * p + 2]]
                        if pair[0] is None and pair[1] is None:
                            block = masked
                        else:
                            block = jnp.where(left, masked if pair[0] is None else pair[0],
                                              masked if pair[1] is None else pair[1])
                        bias_scr[variant, c, h, qi * GRID_W:(qi + 1) * GRID_W, p * LANES:(p + 1) * LANES] = block


def _na_kernel(q_ref, k0, k1, k2, k3, v0, v1, v2, v3, kc_ref, vc_ref, t_ref, o_ref, bias_scr, s_scr, *, rows):
    n_rb = rows // Q_ROWS
    b, rb = pl.program_id(1), pl.program_id(2)

    @pl.when((b == 0) & (rb == 0))
    def _():
        _build_bias_tables(t_ref, bias_scr, rows)

    variant = jnp.where(rb == 0, 0, jnp.where(rb == n_rb - 1, 2, 1))
    tq = Q_CHUNK_ROWS * GRID_W
    tk = K_CHUNK_ROWS * GRID_W
    units = []
    for c in range(N_Q_CHUNKS):
        use_last = (rb == n_rb - 1) if c == 0 else (rb != 0)
        window = [(jnp.where(use_last, k3[0], k0[0]), jnp.where(use_last, v3[0], v0[0])),
                  (k1[0], v1[0]), (k2[0], v2[0])]
        q = q_ref[0, c * tq:(c + 1) * tq, :]
        for h in range(HEADS_PER_STEP):
            lat = [(k, v, bias_scr[variant, c, h, :, j * tk:(j + 1) * tk]) for j, (k, v) in enumerate(window)]
            units.append((q, h, lat + [(kc_ref[0], vc_ref[0], None)], s_scr.at[c, h]))
    outs = _attention_units(units)
    for c in range(N_Q_CHUNKS):
        o = _merge_heads(outs[c * HEADS_PER_STEP:(c + 1) * HEADS_PER_STEP])
        o_ref[0, c * tq:(c + 1) * tq, :] = o.astype(BF16)


def _neighborhood_attention(q, k, v, kc, vc, toeplitz, layer):
    bsz, length, _ = q.shape
    rows = length // GRID_W
    n_rb = rows // Q_ROWS
    assert rows % Q_ROWS == 0 and rows >= K_ROWS + Q_ROWS and n_rb >= 3
    _check_windows(rows)
    n_ctx = kc.shape[1]
    tq = Q_ROWS * GRID_W
    tk = K_CHUNK_ROWS * GRID_W
    assert n_ctx == tk
    max_chunk = (rows - K_ROWS) // K_CHUNK_ROWS
    half = (K_ROWS - Q_ROWS) // 2 // K_CHUNK_ROWS

    def kv_spec(j):
        def index(hp, b, rb):
            start = jnp.clip(rb * (Q_ROWS // K_CHUNK_ROWS) - half, 0, max_chunk)
            return (b, start + j, hp)
        return pl.BlockSpec((1, tk, LANES), index)

    ctx_spec = pl.BlockSpec((1, n_ctx, LANES), lambda hp, b, rb: (b, 0, hp))
    q_spec = pl.BlockSpec((1, tq, LANES), lambda hp, b, rb: (b, rb, hp))
    window = WINDOW_CHUNKS * K_CHUNK_ROWS * GRID_W
    return pl.pallas_call(
        functools.partial(_na_kernel, rows=rows),
        grid=(NA_HEADS // HEADS_PER_STEP, bsz, n_rb),
        in_specs=([q_spec] + [kv_spec(j) for j in range(N_K_CHUNKS)] + [kv_spec(j) for j in range(N_K_CHUNKS)]
                  + [ctx_spec, ctx_spec,
                     pl.BlockSpec((HEADS_PER_STEP, DR_PAD, GRID_W, LANES),
                                  lambda hp, b, rb: (layer * (NA_HEADS // HEADS_PER_STEP) + hp, 0, 0, 0))]),
        out_specs=q_spec,
        out_shape=jax.ShapeDtypeStruct((bsz, length, NA_WIDTH), BF16),
        scratch_shapes=[pltpu.VMEM((3, N_Q_CHUNKS, HEADS_PER_STEP, Q_CHUNK_ROWS * GRID_W, window), F32),
                        pltpu.VMEM((N_Q_CHUNKS, HEADS_PER_STEP, Q_CHUNK_ROWS * GRID_W, window + n_ctx), F32)],
        compiler_params=_cparams(3),
        name="neighborhood_attention",
    )(q, *([k] * N_K_CHUNKS), *([v] * N_K_CHUNKS), kc, vc, toeplitz)


def _ctx_attn_kernel(q_ref, k_ref, v_ref, o_ref, s_scr):
    tiles = [(k_ref[0], v_ref[0], None)]
    outs = _attention_units([(q_ref[0], h, tiles, s_scr.at[h]) for h in range(HEADS_PER_STEP)])
    o_ref[0] = _merge_heads(outs).astype(BF16)


def _context_attention(qc, kc, vc):
    bsz, n_ctx, _ = qc.shape
    spec = pl.BlockSpec((1, n_ctx, LANES), lambda b, hp: (b, 0, hp))
    return pl.pallas_call(
        _ctx_attn_kernel,
        grid=(bsz, NA_HEADS // HEADS_PER_STEP),
        in_specs=[spec, spec, spec],
        out_specs=spec,
        out_shape=jax.ShapeDtypeStruct((bsz, n_ctx, NA_WIDTH), BF16),
        scratch_shapes=[pltpu.VMEM((HEADS_PER_STEP, n_ctx, n_ctx), F32)],
        compiler_params=_cparams(2),
        name="context_attention",
    )(qc, kc, vc)


def _dft_cos_sin(n):
    ang = 2.0 * np.pi * np.outer(np.arange(n), np.arange(n)) / n
    return np.cos(ang), np.sin(ang)


def _bf16_table(a):
    return jnp.asarray(a, F32).astype(BF16)


def _channel_dft_matrix(n_groups=F_GROUPS):
    c, s = _dft_cos_sin(F_GROUP_DIM)
    scale = F_GROUP_DIM ** -0.5
    eye = np.eye(n_groups)
    return np.concatenate([np.kron(eye, c), np.kron(eye, s)], axis=1) * scale


def _fft_stage1_kernel(f_ref, perm_ref, w1_ref, cs_ref, sc_ref, tc_ref, ts_ref, zr_ref, zi_ref, *, n_slow, nt):
    x = f_ref[0].reshape(n_slow * nt, F_WIDTH).astype(BF16)
    x = _dot(perm_ref[...], x).astype(BF16)
    ab = [_dot(x[:, p * LANES:(p + 1) * LANES], w1_ref[...]).astype(BF16) for p in range(F_WIDTH // LANES)]
    a_all = jnp.concatenate([blk[:, :LANES] for blk in ab], axis=-1)
    b_all = jnp.concatenate([blk[:, LANES:] for blk in ab], axis=-1)
    for t in range(nt):
        a = a_all[t * n_slow:(t + 1) * n_slow]
        b = b_all[t * n_slow:(t + 1) * n_slow]
        z = _dot(cs_ref[...], a) + _dot(sc_ref[...], b)
        zr, zi = z[:n_slow], z[n_slow:]
        c, s = tc_ref[t], ts_ref[t]
        zr_ref[0, t] = zr * c - zi * s
        zi_ref[0, t] = zr * s + zi * c


def _fft_stage2_kernel(zr_ref, zi_ref, f_ref, y_ref, *, n_fast, kb):
    rhs = jnp.concatenate([zr_ref[0].reshape(n_fast * kb, F_WIDTH).astype(BF16),
                           zi_ref[0].reshape(n_fast * kb, F_WIDTH).astype(BF16)], axis=0)
    y_ref[0] = _dot(f_ref[...], rhs).reshape(n_fast, kb, F_WIDTH)


def _fourier_positions(f, n_slow, n_fast):
    bsz, n, _ = f.shape
    assert n == n_slow * n_fast
    nt = SUBLANES
    kb = SUBLANES
    w1 = _bf16_table(_channel_dft_matrix(LANES // F_GROUP_DIM))
    perm = _bf16_table(np.eye(n_slow * nt).reshape(n_slow, nt, n_slow * nt).transpose(1, 0, 2)
                       .reshape(n_slow * nt, n_slow * nt))
    c1, s1 = _dft_cos_sin(n_slow)
    sc1 = n_slow ** -0.5
    cs = _bf16_table(np.concatenate([c1, s1], axis=0) * sc1)
    sc = _bf16_table(np.concatenate([-s1, c1], axis=0) * sc1)
    tw = 2.0 * np.pi * np.outer(np.arange(n_fast), np.arange(n_slow)) / n
    tc = jnp.asarray(np.cos(tw)[:, :, None], F32)
    ts = jnp.asarray(np.sin(tw)[:, :, None], F32)
    z_shape = jax.ShapeDtypeStruct((bsz, n_fast, n_slow, F_WIDTH), F32)
    z_spec = pl.BlockSpec((1, nt, n_slow, F_WIDTH), lambda b, j: (b, j, 0, 0))
    const2 = lambda b, j: (0, 0)
    tw_spec = pl.BlockSpec((nt, n_slow, 1), lambda b, j: (j, 0, 0))
    zr, zi = pl.pallas_call(
        functools.partial(_fft_stage1_kernel, n_slow=n_slow, nt=nt),
        grid=(bsz, n_fast // nt),
        in_specs=[pl.BlockSpec((1, n_slow, nt, F_WIDTH), lambda b, j: (b, 0, j, 0)),
                  pl.BlockSpec(perm.shape, const2),
                  pl.BlockSpec(w1.shape, const2), pl.BlockSpec(cs.shape, const2), pl.BlockSpec(sc.shape, const2),
                  tw_spec, tw_spec],
        out_specs=[z_spec, z_spec],
        out_shape=[z_shape, z_shape],
        compiler_params=_cparams(2),
        name="fnet_stage1",
    )(f.reshape(bsz, n_slow, n_fast, F_WIDTH), perm, w1, cs, sc, tc, ts)

    c2, s2 = _dft_cos_sin(n_fast)
    sc2 = n_fast ** -0.5
    eye = np.eye(kb)
    f2 = _bf16_table(np.concatenate([np.kron(c2, eye), np.kron(-s2, eye)], axis=1) * sc2)
    blk = pl.BlockSpec((1, n_fast, kb, F_WIDTH), lambda b, j: (b, 0, j, 0))
    y = pl.pallas_call(
        functools.partial(_fft_stage2_kernel, n_fast=n_fast, kb=kb),
        grid=(bsz, n_slow // kb),
        in_specs=[blk, blk, pl.BlockSpec(f2.shape, const2)],
        out_specs=blk,
        out_shape=jax.ShapeDtypeStruct((bsz, n_fast, n_slow, F_WIDTH), F32),
        compiler_params=_cparams(2),
        name="fnet_stage2",
    )(zr, zi, f2)
    return y.reshape(bsz, n, F_WIDTH)


def _second_largest_sum(a, b, c, d):
    mab, nab = jnp.maximum(a, b), jnp.minimum(a, b)
    mcd, ncd = jnp.maximum(c, d), jnp.minimum(c, d)
    return jnp.maximum(mab, mcd) + jnp.maximum(jnp.minimum(mab, mcd), jnp.maximum(nab, ncd))


def _selected_group(sb_rows):
    epg = EXPERTS_PER_GROUP
    g_score = [_second_largest_sum(*sb_rows[g * epg:(g + 1) * epg]) for g in range(N_GROUPS)]
    best = functools.reduce(jnp.maximum, g_score)
    group = jnp.full_like(best, float(N_GROUPS - 1))
    for g in range(N_GROUPS - 2, -1, -1):
        group = jnp.where(g_score[g] == best, float(g), group)
    return group


def _top2_gates(cand_s, cand_sb):
    n = len(cand_s)
    w = []
    for j in range(n):
        rank = jnp.zeros_like(cand_sb[j])
        for i in range(n):
            if i == j:
                continue
            ahead = (cand_sb[i] > cand_sb[j]) | ((cand_sb[i] == cand_sb[j]) & (i < j))
            rank = rank + jnp.where(ahead, 1.0, 0.0)
        w.append(jnp.where(rank < 2.0, cand_s[j], 0.0))
    total = functools.reduce(jnp.add, w)
    return [wj / total for wj in w]


CHUNK_ROWS = 16


def _local_rows(tm):
    need = tm + N_GROUPS * (CHUNK_ROWS - 1) + CHUNK_ROWS
    return -(-need // CHUNK_ROWS) * CHUNK_ROWS


def _sort_matrix(lslot, lr):
    r_iota = lax.broadcasted_iota(jnp.int32, (lr, 1), 0).astype(F32)
    return jnp.where(r_iota == lslot, 1.0, 0.0).astype(BF16)


def _local_sort(group, h2b, tri_ref, lr):
    t = group.shape[1]
    g_iota = lax.broadcasted_iota(jnp.int32, (SUBLANES, 1), 0).astype(F32)
    onehot = g_iota == group
    prefix = _dot(jnp.where(onehot, 1.0, 0.0).astype(BF16), tri_ref[...])
    count = prefix[:, t - 1:t]
    padded = jnp.floor((count + (CHUNK_ROWS - 1.0)) * (1.0 / CHUNK_ROWS)) * CHUNK_ROWS
    lslot = jnp.zeros((1, t), F32)
    start = jnp.zeros((1, 1), F32)
    for g in range(N_GROUPS):
        lslot = jnp.where(onehot[g:g + 1], start + prefix[g:g + 1] - 1.0, lslot)
        start = start + padded[g:g + 1]
    return lslot, count, _dot(_sort_matrix(lslot, lr), h2b).astype(BF16)


def _out_kernel(a_ref, y_ref, x_ref, m_ref, wo_ref, bd_ref, g_ref, b_ref, wr_ref, rb_ref, tri_ref,
                x1_ref, h2_ref, lslot_ref, count_ref):
    y2 = _dot(y_ref[0].astype(BF16), bd_ref[...]).astype(BF16)
    o = _dot(a_ref[0], wo_ref[:NA_WIDTH, :]) + _dot(y2, wo_ref[NA_WIDTH:, :])
    z = DEEPNORM_ALPHA * x_ref[0] + m_ref[0, 2:3, :] * o
    x1 = _layer_norm(z) * g_ref[...] + b_ref[...]
    x1_ref[0] = x1
    h2 = (_layer_norm(x1) * (1.0 + m_ref[0, 4:5, :]) + m_ref[0, 3:4, :]).astype(BF16)
    sb = jax.nn.sigmoid(_dot_nt(wr_ref[...], h2)) + rb_ref[...]
    group = _selected_group([sb[e:e + 1] for e in range(N_EXPERTS)])
    lslot, count, h2_sorted = _local_sort(group, h2, tri_ref, h2_ref.shape[0])
    h2_ref[...] = h2_sorted
    lslot_ref[0] = jnp.broadcast_to(lslot, lslot_ref.shape[1:])
    count_ref[0] = jnp.broadcast_to(count, count_ref.shape[1:])


def _out_projection(attn, yf, x, m, w_out_bf16, w_four_bd, ln_g, ln_b, w_router_t, router_bias, tm):
    bsz, length, _ = x.shape
    tiles_per_batch = length // tm
    n_tiles = bsz * tiles_per_batch
    lr = _local_rows(tm)
    tri = _bf16_table(np.triu(np.ones((tm, tm))))
    row = lambda b, i: (b, i, 0)
    tile = lambda b, i: (b * tiles_per_batch + i, 0)
    const2 = lambda b, i: (0, 0)
    return pl.pallas_call(
        _out_kernel,
        grid=(bsz, tiles_per_batch),
        in_specs=[pl.BlockSpec((1, tm, NA_WIDTH), row), pl.BlockSpec((1, tm, F_WIDTH), row),
                  pl.BlockSpec((1, tm, D_MODEL), row),
                  pl.BlockSpec((1, N_MOD, D_MODEL), lambda b, i: (b, 0, 0)),
                  pl.BlockSpec((NA_WIDTH + F_WIDTH, D_MODEL), const2),
                  pl.BlockSpec((F_WIDTH, F_WIDTH), const2),
                  pl.BlockSpec((1, D_MODEL), const2), pl.BlockSpec((1, D_MODEL), const2),
                  pl.BlockSpec((N_EXPERTS, D_MODEL), const2), pl.BlockSpec((N_EXPERTS, 1), const2),
                  pl.BlockSpec((tm, tm), const2)],
        out_specs=[pl.BlockSpec((1, tm, D_MODEL), row),
                   pl.BlockSpec((lr, D_MODEL), tile),
                   pl.BlockSpec((1, SUBLANES, tm), lambda b, i: (b, 0, i)),
                   pl.BlockSpec((1, SUBLANES, LANES), lambda b, i: (b * tiles_per_batch + i, 0, 0))],
        out_shape=[jax.ShapeDtypeStruct((bsz, length, D_MODEL), F32),
                   jax.ShapeDtypeStruct((n_tiles * lr, D_MODEL), BF16),
                   jax.ShapeDtypeStruct((bsz, SUBLANES, length), F32),
                   jax.ShapeDtypeStruct((n_tiles, SUBLANES, LANES), F32)],
        compiler_params=_cparams(2),
        name="out_proj_norm_route",
    )(attn, yf, x, m, w_out_bf16, w_four_bd, ln_g.reshape(1, D_MODEL), ln_b.reshape(1, D_MODEL),
      w_router_t, router_bias.reshape(N_EXPERTS, 1), tri)


def _group_plan(group, tm):
    n = group.shape[0]
    n_slots = n + N_GROUPS * tm
    onehot = (group[:, None] == jnp.arange(N_GROUPS, dtype=jnp.int32)[None, :]).astype(jnp.int32)
    csum = jnp.cumsum(onehot, axis=0)
    rank = jnp.sum(onehot * csum, axis=1) - 1
    count = csum[-1]
    padded = ((count + tm - 1) // tm) * tm
    end = jnp.cumsum(padded)
    start = end - padded
    slot = jnp.sum(onehot * start[None, :], axis=1) + rank
    tile_start = jnp.arange(n_slots // tm, dtype=jnp.int32) * tm
    tile_group = jnp.minimum(jnp.sum((end[None, :] <= tile_start[:, None]).astype(jnp.int32), axis=1),
                             N_GROUPS - 1)
    tile_rows = jnp.clip((start + count)[tile_group] - tile_start, 0, tm)
    return slot, tile_group, tile_rows, start, start + count, end


ROW_DMA_UNROLL = 8


def _slab(ref, token):
    return ref.at[pl.ds(pl.multiple_of(token * SLAB_ROWS, SLAB_ROWS), SLAB_ROWS), :]


def _row_scatter(idx_ref, base, src_ref, dst_hbm, sem, n_rows, wait, inverse_ref=None, src_base=0):
    def one(r, priority):
        if wait:
            pltpu.make_async_copy(_slab(src_ref, 0), _slab(dst_hbm, 0), sem).wait()
        else:
            idx = idx_ref[base + r]
            if inverse_ref is not None:
                inverse_ref[idx] = base + r
            pltpu.make_async_copy(_slab(src_ref, src_base + r), _slab(dst_hbm, idx), sem).start(priority=priority)

    def chunk(j, carry):
        for u in range(ROW_DMA_UNROLL):
            one(j * ROW_DMA_UNROLL + u, u % 2)
        return carry

    def single(r, carry):
        one(r, 0)
        return carry

    n_chunks = n_rows // ROW_DMA_UNROLL
    lax.fori_loop(0, n_chunks, chunk, 0)
    lax.fori_loop(n_chunks * ROW_DMA_UNROLL, n_rows, single, 0)


def _to_slots_kernel(slot_ref, start_ref, rows_end_ref, end_ref, h_hbm, o_hbm, dst_ref, zbuf, hbuf, sems,
                     fetch_sems, *, tm, tm_slots, n_steps):
    step = pl.program_id(0)
    n_slots = dst_ref.shape[0]
    sem = sems.at[0]

    @pl.when(step == 0)
    def _():
        zbuf[...] = jnp.zeros_like(zbuf)

        def fill(slot0):
            rows = tm_slots * SLAB_ROWS
            return pltpu.make_async_copy(zbuf, o_hbm.at[pl.ds(pl.multiple_of(slot0 * SLAB_ROWS, rows), rows), :], sem)

        for g in range(N_GROUPS):
            @pl.when(end_ref[g] > start_ref[g])
            def _():
                fill(end_ref[g] - tm_slots).start()
        for g in range(N_GROUPS):
            @pl.when(end_ref[g] > start_ref[g])
            def _():
                fill(end_ref[g] - tm_slots).wait()

        def fill_unused(t, carry):
            fill(t * tm_slots).start()
            fill(t * tm_slots).wait()
            return carry
        lax.fori_loop(end_ref[N_GROUPS - 1] // tm_slots, n_slots // tm_slots, fill_unused, 0)

        def clear(p, carry):
            dst_ref[p] = 0
            return carry
        for g in range(N_GROUPS):
            lax.fori_loop(rows_end_ref[g], end_ref[g], clear, 0)
        lax.fori_loop(end_ref[N_GROUPS - 1], n_slots, clear, 0)

    def fetch(s):
        rows = tm * SLAB_ROWS
        return pltpu.make_async_copy(h_hbm.at[pl.ds(pl.multiple_of(s * rows, rows), rows), :],
                                     hbuf.at[s % 3], fetch_sems.at[s % 3])

    def scatter(s, wait):
        _row_scatter(slot_ref, s * tm, hbuf.at[s % 3], o_hbm, sems.at[s % 2], tm, wait,
                     inverse_ref=None if wait else dst_ref)

    @pl.when(step == 0)
    def _():
        fetch(0).start()

    @pl.when(step + 1 < n_steps)
    def _():
        fetch(step + 1).start()

    fetch(step).wait()
    scatter(step, wait=False)

    @pl.when(step >= 1)
    def _():
        scatter(step - 1, wait=True)

    @pl.when(step == n_steps - 1)
    def _():
        scatter(step, wait=True)


def _to_slots(h2_slabs, slot, start, rows_end, end, tm, tm_slots):
    n = h2_slabs.shape[0] // SLAB_ROWS
    n_slots = n + N_GROUPS * tm_slots
    grid_spec = pltpu.PrefetchScalarGridSpec(
        num_scalar_prefetch=4,
        grid=(n // tm,),
        in_specs=[pl.BlockSpec(memory_space=pl.ANY)],
        out_specs=[pl.BlockSpec(memory_space=pl.ANY), pl.BlockSpec(memory_space=pltpu.SMEM)],
        scratch_shapes=[pltpu.VMEM((tm_slots * SLAB_ROWS, LANES), F32),
                        pltpu.VMEM((3, tm * SLAB_ROWS, LANES), F32),
                        pltpu.SemaphoreType.DMA((2,)), pltpu.SemaphoreType.DMA((3,))],
    )
    return pl.pallas_call(
        functools.partial(_to_slots_kernel, tm=tm, tm_slots=tm_slots, n_steps=n // tm),
        grid_spec=grid_spec,
        out_shape=[jax.ShapeDtypeStruct((n_slots * SLAB_ROWS, LANES), F32),
                   jax.ShapeDtypeStruct((n_slots,), jnp.int32)],
        compiler_params=_cparams(1),
        name="moe_rows_to_slots",
    )(slot, start, rows_end, end, h2_slabs)


def _moe_group_kernel(dst_ref, tile_group_ref, tile_rows_ref, n_used_ref, h_ref, wg_ref, wu_ref, wd_ref, wr_ref,
                      rb_ref, y_hbm, ybuf, sems, *, tm, n_tiles):
    del tile_group_ref
    step = pl.program_id(0)
    n_used = n_used_ref[0]
    buf = step % 2

    def scatter(tile, wait):
        _row_scatter(dst_ref, tile * tm, ybuf.at[tile % 2], y_hbm, sems.at[tile % 2], tile_rows_ref[tile], wait)

    @pl.when((step >= 2) & (step < n_used))
    def _():
        scatter(step - 2, wait=True)

    @pl.when(step < n_used)
    def _():
        h = _from_slabs(h_ref, tm).astype(BF16)
        s = jax.nn.sigmoid(_dot(h, wr_ref[0]))
        sb = s + rb_ref[0]
        epg = EXPERTS_PER_GROUP
        gates = _top2_gates([s[:, j:j + 1] for j in range(epg)], [sb[:, j:j + 1] for j in range(epg)])
        acc = None
        gate_up = [(_dot(h, wg_ref[0]), _dot(h, wu_ref[0]))]
        for e in range(epg):
            if e + 1 < epg:
                gate_up.append((_dot(h, wg_ref[e + 1]), _dot(h, wu_ref[e + 1])))
            gate, up = gate_up[e]
            hid = (gate * jax.nn.sigmoid(gate)) * up * gates[e]
            y = _dot(hid.astype(BF16), wd_ref[e])
            acc = y if acc is None else acc + y
        _to_slabs(ybuf.at[buf], acc)
        scatter(step, wait=False)

    @pl.when(step == n_tiles - 1)
    def _():
        @pl.when(n_used >= 2)
        def _():
            scatter(n_used - 2, wait=True)

        @pl.when(n_used >= 1)
        def _():
            scatter(n_used - 1, wait=True)


def _moe_experts(h_slots, dst, tile_group, tile_rows, n_used, n_tokens, layer, w_gate, w_up, w_down,
                 w_router_grp, router_bias_grp, tm):
    n_slots = h_slots.shape[0] // SLAB_ROWS
    n_tiles = n_slots // tm
    epg = EXPERTS_PER_GROUP
    by_group = lambda i, dst_ref, tg_ref, tr_ref, nu_ref: (tg_ref[i], 0, 0)
    by_layer_group = lambda i, dst_ref, tg_ref, tr_ref, nu_ref: (layer * N_GROUPS + tg_ref[i], 0, 0)
    grid_spec = pltpu.PrefetchScalarGridSpec(
        num_scalar_prefetch=4,
        grid=(n_tiles,),
        in_specs=[pl.BlockSpec((tm * SLAB_ROWS, LANES),
                               lambda i, dst_ref, tg_ref, tr_ref, nu_ref: (jnp.minimum(i, nu_ref[0] - 1), 0)),
                  pl.BlockSpec((epg, D_MODEL, D_EXPERT), by_layer_group),
                  pl.BlockSpec((epg, D_MODEL, D_EXPERT), by_layer_group),
                  pl.BlockSpec((epg, D_EXPERT, D_MODEL), by_layer_group),
                  pl.BlockSpec((1, D_MODEL, LANES), by_group),
                  pl.BlockSpec((1, 1, LANES), by_group)],
        out_specs=pl.BlockSpec(memory_space=pl.ANY),
        scratch_shapes=[pltpu.VMEM((2, tm * SLAB_ROWS, LANES), F32), pltpu.SemaphoreType.DMA((2,))],
    )
    return pl.pallas_call(
        functools.partial(_moe_group_kernel, tm=tm, n_tiles=n_tiles),
        grid_spec=grid_spec,
        out_shape=jax.ShapeDtypeStruct((n_tokens * SLAB_ROWS, LANES), F32),
        compiler_params=_cparams(1),
        name="moe_group_experts",
    )(dst, tile_group, tile_rows, n_used, h_slots, w_gate, w_up, w_down, w_router_grp, router_bias_grp)


def _residual_norm_kernel(y_ref, x1_ref, m_ref, g_ref, b_ref, o_ref):
    y = _from_slabs(y_ref, x1_ref.shape[0])
    z = DEEPNORM_ALPHA * x1_ref[...] + m_ref[0, 5:6, :] * y
    o_ref[...] = _layer_norm(z) * g_ref[...] + b_ref[...]


def _residual_norm(y_slabs, x1, m, ln_g, ln_b, tm):
    bsz, length, _ = x1.shape
    n = bsz * length
    tiles_per_batch = length // tm
    const2 = lambda i: (0, 0)
    row = pl.BlockSpec((tm, D_MODEL), lambda i: (i, 0))
    out = pl.pallas_call(
        _residual_norm_kernel,
        grid=(n // tm,),
        in_specs=[pl.BlockSpec((tm * SLAB_ROWS, LANES), lambda i: (i, 0)), row,
                  pl.BlockSpec((1, N_MOD, D_MODEL), lambda i: (i // tiles_per_batch, 0, 0)),
                  pl.BlockSpec((1, D_MODEL), const2), pl.BlockSpec((1, D_MODEL), const2)],
        out_specs=row,
        out_shape=jax.ShapeDtypeStruct((n, D_MODEL), F32),
        compiler_params=_cparams(1),
        name="moe_residual_norm",
    )(y_slabs, x1.reshape(n, D_MODEL), m, ln_g.reshape(1, D_MODEL), ln_b.reshape(1, D_MODEL))
    return out.reshape(bsz, length, D_MODEL)


def _grouped_moe(h2, group_rows, layer, w_gate, w_up, w_down, w_router_grp, router_bias_grp, tm_experts,
                 tm_tokens):
    bsz, _, length = group_rows.shape
    n = bsz * length
    group = group_rows[:, 0, :].reshape(n).astype(jnp.int32)
    slot, tile_group, tile_rows, start, rows_end, end = _group_plan(group, tm_experts)
    h_slots, dst = _to_slots(h2, slot, start, rows_end, end, tm_tokens, tm_experts)
    n_used = (end[N_GROUPS - 1:] // tm_experts).astype(jnp.int32)
    return _moe_experts(h_slots, dst, tile_group, tile_rows, n_used, n, layer, w_gate, w_up, w_down, w_router_grp,
                        router_bias_grp, tm_experts)


def _run_plan(count, tm, lr, tm_slots):
    n_tiles = count.shape[0]
    ch = CHUNK_ROWS
    n_slots = -(-(n_tiles * (tm + N_GROUPS * (ch - 1)) + N_GROUPS * tm_slots) // tm_slots) * tm_slots
    zero_chunk = lr // ch - 1
    pad = ((count + ch - 1) // ch) * ch
    lstart = jnp.cumsum(pad, axis=1) - pad
    run_off = jnp.cumsum(pad, axis=0) - pad
    seg_len = jnp.sum(pad, axis=0)
    seg_pad = ((seg_len + tm_slots - 1) // tm_slots) * tm_slots
    seg_end = jnp.cumsum(seg_pad)
    seg_start = seg_end - seg_pad

    s = (jnp.arange(n_slots // ch, dtype=jnp.int32) * ch)[:, None, None]
    in_run = (seg_start + run_off <= s) & (s < seg_start + run_off + pad)
    src_row = jnp.arange(n_tiles, dtype=jnp.int32)[None, :, None] * lr + lstart + (s - seg_start - run_off)
    src_row = jnp.sum(jnp.where(in_run, src_row, 0), axis=(1, 2))
    src_chunk = jnp.where(jnp.any(in_run, axis=(1, 2)), src_row // ch, zero_chunk).astype(jnp.int32)

    tile_start = jnp.arange(n_slots // tm_slots, dtype=jnp.int32) * tm_slots
    tile_group = jnp.minimum(jnp.sum((seg_end[None, :] <= tile_start[:, None]).astype(jnp.int32), axis=1),
                             N_GROUPS - 1)
    n_used = (seg_end[N_GROUPS - 1:] // tm_slots).astype(jnp.int32)

    r = (jnp.arange(lr // ch, dtype=jnp.int32) * ch)[None, :, None]
    in_local = (lstart[:, None, :] <= r) & (r < (lstart + pad)[:, None, :])
    slot = (seg_start + run_off - lstart)[:, None, :] + r
    back_chunk = (jnp.sum(jnp.where(in_local, slot, 0), axis=2) // ch).astype(jnp.int32)
    return src_chunk, tile_group.astype(jnp.int32), n_used, back_chunk.reshape(-1)


def _chunk_gather(idx_ref, tile, n_chunks, src_hbm, buf, sems, wait):
    slot = tile % 2
    rows = n_chunks * CHUNK_ROWS
    if wait:
        pltpu.make_async_copy(src_hbm.at[pl.ds(0, rows), :], buf.at[slot], sems.at[slot]).wait()
        return
    for c in range(n_chunks):
        src = pl.multiple_of(idx_ref[tile * n_chunks + c] * CHUNK_ROWS, CHUNK_ROWS)
        pltpu.make_async_copy(src_hbm.at[pl.ds(src, CHUNK_ROWS), :],
                              buf.at[slot, pl.ds(c * CHUNK_ROWS, CHUNK_ROWS), :], sems.at[slot]).start()


def _moe_runs_kernel(src_ref, tile_group_ref, n_used_ref, h_hbm, wg_ref, wu_ref, wd_ref, wr_ref, rb_ref, y_ref,
                     hbuf, sems, *, tm):
    del tile_group_ref
    step = pl.program_id(0)
    n_used = n_used_ref[0]
    n_chunks = tm // CHUNK_ROWS

    @pl.when((step == 0) & (n_used > 0))
    def _():
        _chunk_gather(src_ref, 0, n_chunks, h_hbm, hbuf, sems, wait=False)

    @pl.when(step + 1 < n_used)
    def _():
        _chunk_gather(src_ref, step + 1, n_chunks, h_hbm, hbuf, sems, wait=False)

    @pl.when(step < n_used)
    def _():
        _chunk_gather(src_ref, step, n_chunks, h_hbm, hbuf, sems, wait=True)
        h = hbuf[step % 2]
        s = jax.nn.sigmoid(_dot(h, wr_ref[0]))
        sb = s + rb_ref[0]
        epg = EXPERTS_PER_GROUP
        gates = _top2_gates([s[:, j:j + 1] for j in range(epg)], [sb[:, j:j + 1] for j in range(epg)])
        acc = None
        gate_up = [(_dot(h, wg_ref[0]), _dot(h, wu_ref[0]))]
        for e in range(epg):
            if e + 1 < epg:
                gate_up.append((_dot(h, wg_ref[e + 1]), _dot(h, wu_ref[e + 1])))
            gate, up = gate_up[e]
            hid = (gate * jax.nn.sigmoid(gate)) * up * gates[e]
            y = _dot(hid.astype(BF16), wd_ref[e])
            acc = y if acc is None else acc + y
        y_ref[...] = acc.astype(BF16)

    @pl.when(step >= n_used)
    def _():
        y_ref[...] = jnp.zeros_like(y_ref)


def _moe_runs(h_sorted, src_chunk, tile_group, n_used, layer, w_gate, w_up, w_down, w_router_grp,
              router_bias_grp, tm):
    n_slots = src_chunk.shape[0] * CHUNK_ROWS
    epg = EXPERTS_PER_GROUP
    by_group = lambda i, src_ref, tg_ref, nu_ref: (tg_ref[i], 0, 0)
    by_layer_group = lambda i, src_ref, tg_ref, nu_ref: (layer * N_GROUPS + tg_ref[i], 0, 0)
    grid_spec = pltpu.PrefetchScalarGridSpec(
        num_scalar_prefetch=3,
        grid=(n_slots // tm,),
        in_specs=[pl.BlockSpec(memory_space=pl.ANY),
                  pl.BlockSpec((epg, D_MODEL, D_EXPERT), by_layer_group),
                  pl.BlockSpec((epg, D_MODEL, D_EXPERT), by_layer_group),
                  pl.BlockSpec((epg, D_EXPERT, D_MODEL), by_layer_group),
                  pl.BlockSpec((1, D_MODEL, LANES), by_group),
                  pl.BlockSpec((1, 1, LANES), by_group)],
        out_specs=pl.BlockSpec((tm, D_MODEL), lambda i, src_ref, tg_ref, nu_ref: (i, 0)),
        scratch_shapes=[pltpu.VMEM((2, tm, D_MODEL), BF16), pltpu.SemaphoreType.DMA((2,))],
    )
    return pl.pallas_call(
        functools.partial(_moe_runs_kernel, tm=tm),
        grid_spec=grid_spec,
        out_shape=jax.ShapeDtypeStruct((n_slots, D_MODEL), BF16),
        compiler_params=_cparams(1),
        name="moe_group_experts",
    )(src_chunk, tile_group, n_used, h_sorted, w_gate, w_up, w_down, w_router_grp, router_bias_grp)


def _gather_unsort(back_ref, lslot_ref, y_hbm, ybuf, sems, step, n_steps, lr):
    n_chunks = lr // CHUNK_ROWS

    @pl.when(step == 0)
    def _():
        _chunk_gather(back_ref, 0, n_chunks, y_hbm, ybuf, sems, wait=False)

    @pl.when(step + 1 < n_steps)
    def _():
        _chunk_gather(back_ref, step + 1, n_chunks, y_hbm, ybuf, sems, wait=False)

    _chunk_gather(back_ref, step, n_chunks, y_hbm, ybuf, sems, wait=True)
    sort = _sort_matrix(lslot_ref[0, 0:1, :], lr)
    return lax.dot_general(sort, ybuf[step % 2], (((0,), (0,)), ((), ())), preferred_element_type=F32)


def _residual_norm_kernel(back_ref, y_hbm, lslot_ref, x1_ref, m_ref, g_ref, b_ref, o_ref, ybuf, sems, *, n_steps, lr):
    y = _gather_unsort(back_ref, lslot_ref, y_hbm, ybuf, sems, pl.program_id(0), n_steps, lr)
    z = DEEPNORM_ALPHA * x1_ref[...] + m_ref[0, 5:6, :] * y
    o_ref[...] = _layer_norm(z) * g_ref[...] + b_ref[...]


def _residual_norm(y_slots, back_chunk, lslot, x1, m, ln_g, ln_b, tm):
    bsz, length, _ = x1.shape
    n = bsz * length
    tiles_per_batch = length // tm
    lr = _local_rows(tm)
    const2 = lambda i, back_ref: (0, 0)
    row = pl.BlockSpec((tm, D_MODEL), lambda i, back_ref: (i, 0))
    grid_spec = pltpu.PrefetchScalarGridSpec(
        num_scalar_prefetch=1,
        grid=(n // tm,),
        in_specs=[pl.BlockSpec(memory_space=pl.ANY),
                  pl.BlockSpec((1, SUBLANES, tm),
                               lambda i, back_ref: (i // tiles_per_batch, 0, i % tiles_per_batch)),
                  row, pl.BlockSpec((1, N_MOD, D_MODEL), lambda i, back_ref: (i // tiles_per_batch, 0, 0)),
                  pl.BlockSpec((1, D_MODEL), const2), pl.BlockSpec((1, D_MODEL), const2)],
        out_specs=row,
        scratch_shapes=[pltpu.VMEM((2, lr, D_MODEL), BF16), pltpu.SemaphoreType.DMA((2,))],
    )
    out = pl.pallas_call(
        functools.partial(_residual_norm_kernel, n_steps=n // tm, lr=lr),
        grid_spec=grid_spec,
        out_shape=jax.ShapeDtypeStruct((n, D_MODEL), F32),
        compiler_params=_cparams(1),
        name="moe_residual_norm",
    )(back_chunk, y_slots, lslot, x1.reshape(n, D_MODEL), m, ln_g.reshape(1, D_MODEL), ln_b.reshape(1, D_MODEL))
    return out.reshape(bsz, length, D_MODEL)


def _grouped_moe_runs(h_sorted, count_rows, layer, w_gate, w_up, w_down, w_router_grp, router_bias_grp,
                      tm_tokens, tm_slots):
    count = count_rows[:, :N_GROUPS, 0].astype(jnp.int32)
    src_chunk, tile_group, n_used, back_chunk = _run_plan(count, tm_tokens, _local_rows(tm_tokens), tm_slots)
    y_slots = _moe_runs(h_sorted, src_chunk, tile_group, n_used, layer, w_gate, w_up, w_down, w_router_grp,
                        router_bias_grp, tm_slots)
    return y_slots, back_chunk


def _ctx_fourier_kernel(f_ref, w1_ref, c_ref, s_ref, y_ref):
    ab = _dot(f_ref[0].astype(BF16), w1_ref[...]).astype(BF16)
    y = _dot(c_ref[...], ab[:, :F_WIDTH]) + _dot(s_ref[...], ab[:, F_WIDTH:])
    y_ref[0] = y.astype(BF16)


def _context_fourier(fc):
    bsz, n, _ = fc.shape
    w1 = _bf16_table(_channel_dft_matrix())
    c, s = _dft_cos_sin(n)
    cm = _bf16_table(c * n ** -0.5)
    sm = _bf16_table(-s * n ** -0.5)
    const2 = lambda b: (0, 0)
    blk = pl.BlockSpec((1, n, F_WIDTH), lambda b: (b, 0, 0))
    return pl.pallas_call(
        _ctx_fourier_kernel,
        grid=(bsz,),
        in_specs=[blk, pl.BlockSpec(w1.shape, const2), pl.BlockSpec(cm.shape, const2),
                  pl.BlockSpec(sm.shape, const2)],
        out_specs=blk,
        out_shape=jax.ShapeDtypeStruct((bsz, n, F_WIDTH), BF16),
        compiler_params=_cparams(1),
        name="context_fnet",
    )(fc, w1, cm, sm)


def _block_diag(w):
    g, c, _ = w.shape
    eye = jnp.eye(g, dtype=w.dtype)
    return (eye[:, None, :, None] * w[:, :, None, :]).reshape(g * c, g * c)


def kernel(x, c, ctx, c_ctx, w_mod, b_mod, w_in, rpb, w_four, w_out, ln1_g, ln1_b, ln2_g, ln2_b,
           w_router, router_bias, w_gate, w_up, w_down):
    bsz, length, _ = x.shape
    n_ctx = ctx.shape[1]
    rows = length // GRID_W

    cvec = jnp.concatenate([c, c_ctx[None, :], jnp.zeros((8 - bsz - 1, D_MODEL), F32)], axis=0)
    mods = _modulation(cvec, w_mod, b_mod)
    w_router_t = w_router.T.astype(BF16)
    toeplitz = _rpb_toeplitz(rpb)
    lane_pad = LANES - EXPERTS_PER_GROUP
    w_router_grp = jnp.pad(w_router.reshape(D_MODEL, N_GROUPS, EXPERTS_PER_GROUP).transpose(1, 0, 2),
                           ((0, 0), (0, 0), (0, lane_pad))).astype(BF16)
    router_bias_grp = jnp.pad(router_bias.astype(F32).reshape(N_GROUPS, 1, EXPERTS_PER_GROUP),
                              ((0, 0), (0, 0), (0, lane_pad)))

    wg = w_gate.astype(BF16).reshape(DEPTH * N_EXPERTS, D_MODEL, D_EXPERT)
    wu = w_up.astype(BF16).reshape(DEPTH * N_EXPERTS, D_MODEL, D_EXPERT)
    wd = w_down.astype(BF16).reshape(DEPTH * N_EXPERTS, D_EXPERT, D_MODEL)

    def latent_mod(i):
        return mods[i, :bsz].reshape(bsz, N_MOD, D_MODEL)

    xc = ctx
    projected = None
    for i in range(DEPTH):
        last = i == DEPTH - 1
        m = latent_mod(i)
        mc = jnp.broadcast_to(mods[i, bsz].reshape(1, N_MOD, D_MODEL), (bsz, N_MOD, D_MODEL))
        w_in_b = w_in[i].astype(BF16)
        w_out_b = w_out[i].astype(BF16)
        w_four_bd = _block_diag(w_four[i]).astype(BF16)

        q, k, v, f = projected if projected is not None else _in_projection(x, m, w_in_b, tm=512)
        qc, kc, vc, fc = _in_projection(xc, mc, w_in_b, tm=n_ctx)

        attn = _neighborhood_attention(q, k, v, kc, vc, toeplitz, i)
        yf = _fourier_positions(f, n_slow=rows, n_fast=GRID_W)
        x1, h2, lslot, cnt = _out_projection(attn, yf, x, m, w_out_b, w_four_bd, ln1_g[i], ln1_b[i],
                                             w_router_t, router_bias, tm=512)
        y, back = _grouped_moe_runs(h2, cnt, i, wg, wu, wd, w_router_grp, router_bias_grp,
                                    tm_tokens=512, tm_slots=512)
        if last:
            return _residual_norm(y, back, lslot, x1, m, ln2_g[i], ln2_b[i], tm=512)
        x, *projected = _norm_in_projection(y, back, lslot, x1, m, ln2_g[i], ln2_b[i], latent_mod(i + 1),
                                            w_in[i + 1].astype(BF16), tm=512)

        attn_c = _context_attention(qc, kc, vc)
        yc = _context_fourier(fc)
        xc1, h2c, lslot_c, cnt_c = _out_projection(attn_c, yc, xc, mc, w_out_b, w_four_bd, ln1_g[i], ln1_b[i],
                                                   w_router_t, router_bias, tm=n_ctx)
        yc2, back_c = _grouped_moe_runs(h2c, cnt_c, i, wg, wu, wd, w_router_grp, router_bias_grp,
                                        tm_tokens=n_ctx, tm_slots=128)
        xc = _residual_norm(yc2, back_c, lslot_c, xc1, mc, ln2_g[i], ln2_b[i], tm=n_ctx)
    return x
```

```python
import functools
import math

import numpy as np
import jax
import jax.numpy as jnp
from jax import lax
from jax.experimental import pallas as pl
from jax.experimental.pallas import tpu as pltpu

D_MODEL = 1024
DEPTH = 2
GRID_W = 64
NA_HEADS = 8
HEAD_DIM = 64
NA_WIDTH = NA_HEADS * HEAD_DIM
WIN_ROWS = 8
WIN_COLS = 16
F_GROUPS = 8
F_GROUP_DIM = 64
F_WIDTH = F_GROUPS * F_GROUP_DIM
IN_WIDTH = 3 * NA_WIDTH + F_WIDTH
N_EXPERTS = 16
N_GROUPS = 4
EXPERTS_PER_GROUP = N_EXPERTS // N_GROUPS
D_EXPERT = 256
N_MOD = 6
DEEPNORM_ALPHA = (2.0 * DEPTH) ** 0.25
LN_EPS = 1e-6

F32 = jnp.float32
BF16 = jnp.bfloat16

V7X_VMEM_BYTES = 64 * 1024 * 1024
VMEM_LIMIT_BYTES = (V7X_VMEM_BYTES * 3) // 4
LANES = 128
SUBLANES = 8
HEADS_PER_STEP = LANES // HEAD_DIM
assert HEADS_PER_STEP == 2
MASK_VALUE = -1e30

Q_ROWS = 8
K_ROWS = 16
K_CHUNK_ROWS = 4
N_K_CHUNKS = K_ROWS // K_CHUNK_ROWS


def _cparams(n_grid_dims):
    return pltpu.CompilerParams(dimension_semantics=("arbitrary",) * n_grid_dims,
                                vmem_limit_bytes=VMEM_LIMIT_BYTES)


def _layer_norm(x):
    mu = jnp.mean(x, axis=-1, keepdims=True)
    xc = x - mu
    var = jnp.mean(xc * xc, axis=-1, keepdims=True)
    return xc * lax.rsqrt(var + LN_EPS)


SLAB_ROWS = D_MODEL // LANES
assert SLAB_ROWS == SUBLANES


def _to_slabs(ref, x):
    for j in range(SLAB_ROWS):
        ref[pl.ds(j, x.shape[0], stride=SLAB_ROWS), :] = x[:, j * LANES:(j + 1) * LANES]


def _from_slabs(ref, n_tokens):
    return jnp.concatenate([ref[pl.ds(j, n_tokens, stride=SLAB_ROWS), :] for j in range(SLAB_ROWS)], axis=-1)


def _dot(a, b):
    return jnp.dot(a, b, preferred_element_type=F32)


def _dot_nt(a, b):
    return lax.dot_general(a, b, (((1,), (1,)), ((), ())), preferred_element_type=F32)


def _mod_kernel(c_ref, w_ref, b_ref, o_ref):
    c = c_ref[...]
    a = c * jax.nn.sigmoid(c)
    o_ref[0] = jnp.dot(a, w_ref[0], preferred_element_type=F32, precision=lax.Precision.HIGHEST) + b_ref[0]


def _modulation(cvec, w_mod, b_mod):
    n_col_blocks = 4
    wc = (N_MOD * D_MODEL) // n_col_blocks
    rows = cvec.shape[0]
    return pl.pallas_call(
        _mod_kernel,
        grid=(DEPTH, n_col_blocks),
        in_specs=[pl.BlockSpec((rows, D_MODEL), lambda i, j: (0, 0)),
                  pl.BlockSpec((1, D_MODEL, wc), lambda i, j: (i, 0, j)),
                  pl.BlockSpec((1, 1, wc), lambda i, j: (i, 0, j))],
        out_specs=pl.BlockSpec((1, rows, wc), lambda i, j: (i, 0, j)),
        out_shape=jax.ShapeDtypeStruct((DEPTH, rows, N_MOD * D_MODEL), F32),
        compiler_params=_cparams(2),
        name="modulation",
    )(cvec, w_mod, b_mod.reshape(DEPTH, 1, N_MOD * D_MODEL))


def _modulate_project(x, m_ref, w_ref, q_ref, k_ref, v_ref, f_ref):
    h = _layer_norm(x) * (1.0 + m_ref[0, 1:2, :]) + m_ref[0, 0:1, :]
    p = _dot(h.astype(BF16), w_ref[...])
    q_ref[0] = (p[:, :NA_WIDTH] * (HEAD_DIM ** -0.5)).astype(BF16)
    k_ref[0] = p[:, NA_WIDTH:2 * NA_WIDTH].astype(BF16)
    v_ref[0] = p[:, 2 * NA_WIDTH:3 * NA_WIDTH].astype(BF16)
    f_ref[0] = p[:, 3 * NA_WIDTH:]


def _proj_kernel(x_ref, m_ref, w_ref, q_ref, k_ref, v_ref, f_ref):
    _modulate_project(x_ref[0], m_ref, w_ref, q_ref, k_ref, v_ref, f_ref)


def _norm_proj_kernel(back_ref, y_hbm, lslot_ref, x1_ref, m_ref, g_ref, b_ref, m_next_ref, w_ref,
                      x_ref, q_ref, k_ref, v_ref, f_ref, ybuf, sems, *, tiles_per_batch, n_steps, lr):
    step = pl.program_id(0) * tiles_per_batch + pl.program_id(1)
    y = _gather_unsort(back_ref, lslot_ref, y_hbm, ybuf, sems, step, n_steps, lr)
    z = DEEPNORM_ALPHA * x1_ref[0] + m_ref[0, 5:6, :] * y
    x = _layer_norm(z) * g_ref[...] + b_ref[...]
    x_ref[0] = x
    _modulate_project(x, m_next_ref, w_ref, q_ref, k_ref, v_ref, f_ref)


def _norm_in_projection(y_slots, back_chunk, lslot, x1, m, ln_g, ln_b, m_next, w_in_bf16, tm):
    bsz, length, _ = x1.shape
    tiles_per_batch = length // tm
    lr = _local_rows(tm)
    out = jax.ShapeDtypeStruct((bsz, length, NA_WIDTH), BF16)
    out_f = jax.ShapeDtypeStruct((bsz, length, F_WIDTH), F32)
    out_x = jax.ShapeDtypeStruct((bsz, length, D_MODEL), F32)
    row = lambda b, i, back_ref: (b, i, 0)
    const2 = lambda b, i, back_ref: (0, 0)
    mod_spec = pl.BlockSpec((1, N_MOD, D_MODEL), lambda b, i, back_ref: (b, 0, 0))
    o_spec = pl.BlockSpec((1, tm, NA_WIDTH), row)
    x_spec = pl.BlockSpec((1, tm, D_MODEL), row)
    grid_spec = pltpu.PrefetchScalarGridSpec(
        num_scalar_prefetch=1,
        grid=(bsz, tiles_per_batch),
        in_specs=[pl.BlockSpec(memory_space=pl.ANY),
                  pl.BlockSpec((1, SUBLANES, tm), lambda b, i, back_ref: (b, 0, i)),
                  x_spec, mod_spec,
                  pl.BlockSpec((1, D_MODEL), const2), pl.BlockSpec((1, D_MODEL), const2),
                  mod_spec, pl.BlockSpec((D_MODEL, IN_WIDTH), const2)],
        out_specs=[x_spec, o_spec, o_spec, o_spec, o_spec],
        scratch_shapes=[pltpu.VMEM((2, lr, D_MODEL), BF16), pltpu.SemaphoreType.DMA((2,))],
    )
    return pl.pallas_call(
        functools.partial(_norm_proj_kernel, tiles_per_batch=tiles_per_batch, n_steps=bsz * tiles_per_batch,
                          lr=lr),
        grid_spec=grid_spec,
        out_shape=[out_x, out, out, out, out_f],
        compiler_params=_cparams(2),
        name="moe_norm_in_proj",
    )(back_chunk, y_slots, lslot, x1, m, ln_g.reshape(1, D_MODEL), ln_b.reshape(1, D_MODEL), m_next, w_in_bf16)


def _in_projection(x, m, w_in_bf16, tm):
    bsz, length, _ = x.shape
    out = jax.ShapeDtypeStruct((bsz, length, NA_WIDTH), BF16)
    out_f = jax.ShapeDtypeStruct((bsz, length, F_WIDTH), F32)
    o_spec = pl.BlockSpec((1, tm, NA_WIDTH), lambda b, i: (b, i, 0))
    return pl.pallas_call(
        _proj_kernel,
        grid=(bsz, length // tm),
        in_specs=[pl.BlockSpec((1, tm, D_MODEL), lambda b, i: (b, i, 0)),
                  pl.BlockSpec((1, N_MOD, D_MODEL), lambda b, i: (b, 0, 0)),
                  pl.BlockSpec((D_MODEL, IN_WIDTH), lambda b, i: (0, 0))],
        out_specs=[o_spec, o_spec, o_spec, o_spec],
        out_shape=[out, out, out, out_f],
        compiler_params=_cparams(2),
        name="ln_mod_in_proj",
    )(x, m, w_in_bf16)


def _head_lanes(h):
    lane = lax.broadcasted_iota(jnp.int32, (1, LANES), 1)
    return (lane >= HEAD_DIM * h) & (lane < HEAD_DIM * (h + 1))


def _scores_pass(q, h, tiles, s_ref):
    qh = jnp.where(_head_lanes(h), q, jnp.zeros_like(q))
    m = None
    t = tiles[0][0].shape[0]
    for j, (k, _, bias) in enumerate(tiles):
        s = _dot_nt(qh, k)
        if bias is not None:
            s = s + bias
        s_ref[:, j * t:(j + 1) * t] = s
        mj = jnp.max(s, axis=-1, keepdims=True)
        m = mj if m is None else jnp.maximum(m, mj)
    return m


def _pv_pass(h, tiles, s_ref, m):
    o = None
    t = tiles[0][0].shape[0]
    in_head = _head_lanes(h)
    for j, (_, v, _) in enumerate(tiles):
        p = jnp.exp(s_ref[:, j * t:(j + 1) * t] - m)
        oj = _dot(p.astype(BF16), jnp.where(in_head, v, jnp.ones_like(v)))
        o = oj if o is None else o + oj
    return o / pltpu.roll(o, HEAD_DIM, axis=1)


def _attention_units(units):
    outs = []
    maxima = [_scores_pass(*units[0])]
    for u in range(len(units)):
        if u + 1 < len(units):
            maxima.append(_scores_pass(*units[u + 1]))
        _, h, tiles, s_ref = units[u]
        outs.append(_pv_pass(h, tiles, s_ref, maxima[u]))
    return outs


def _merge_heads(outs):
    merged = outs[0]
    for h in range(1, len(outs)):
        merged = jnp.where(_head_lanes(h), outs[h], merged)
    return merged


DR_PAD = 2 * WIN_ROWS
DC_PAD = 2 * WIN_COLS


def _toeplitz_kernel(r_ref, sel_ref, mask_ref, o_ref):
    n = r_ref.shape[0]
    for qc in range(GRID_W):
        block = jnp.dot(r_ref[...], sel_ref[qc], preferred_element_type=F32, precision=lax.Precision.HIGHEST)
        o_ref[pl.ds(qc, n, stride=GRID_W), :] = block + mask_ref[qc]


def _rpb_toeplitz(rpb):
    depth, heads, n_dr, n_dc = rpb.shape
    qc = np.arange(GRID_W)[:, None]
    kc = (np.arange(LANES) % GRID_W)[None, :]
    cs = np.clip(qc - WIN_COLS // 2, 0, GRID_W - WIN_COLS)
    col_valid = (kc >= cs) & (kc < cs + WIN_COLS)
    dc = kc - qc + WIN_COLS - 1
    select = (np.arange(DC_PAD)[None, :, None] == dc[:, None, :]) & col_valid[:, None, :]
    mask = np.where(col_valid, 0.0, MASK_VALUE)[:, None, :]
    r = jnp.pad(rpb.astype(F32), ((0, 0), (0, 0), (0, DR_PAD - n_dr), (0, DC_PAD - n_dc)))
    n = depth * heads * DR_PAD
    const3 = lambda: (0, 0, 0)
    out = pl.pallas_call(
        _toeplitz_kernel,
        grid=(),
        in_specs=[pl.BlockSpec((n, DC_PAD), lambda: (0, 0)),
                  pl.BlockSpec((GRID_W, DC_PAD, LANES), const3), pl.BlockSpec((GRID_W, 1, LANES), const3)],
        out_specs=pl.BlockSpec((n * GRID_W, LANES), lambda: (0, 0)),
        out_shape=jax.ShapeDtypeStruct((n * GRID_W, LANES), F32),
        compiler_params=pltpu.CompilerParams(vmem_limit_bytes=VMEM_LIMIT_BYTES),
        name="rpb_toeplitz",
    )(r.reshape(n, DC_PAD), jnp.asarray(select, F32), jnp.asarray(mask, F32))
    return out.reshape(depth * heads, DR_PAD, GRID_W, LANES)


Q_CHUNK_ROWS = 4
WINDOW_CHUNKS = 3
N_Q_CHUNKS = Q_ROWS // Q_CHUNK_ROWS
assert Q_CHUNK_ROWS == K_CHUNK_ROWS and WINDOW_CHUNKS * K_CHUNK_ROWS >= Q_CHUNK_ROWS + WIN_ROWS - 1
assert N_Q_CHUNKS == 2 and N_K_CHUNKS == 4


def _key_row_start(rb, rows):
    return int(np.clip(rb * Q_ROWS - (K_ROWS - Q_ROWS) // 2, 0, rows - K_ROWS))


def _window_uses_last_chunk(rb, c, n_rb):
    if c == 0:
        return rb == n_rb - 1
    return rb != 0


def _window_key_rows(rb, c, rows):
    var = N_K_CHUNKS - 1 if _window_uses_last_chunk(rb, c, rows // Q_ROWS) else 0
    k0 = _key_row_start(rb, rows)
    return [k0 + K_CHUNK_ROWS * j + i for j in (var, 1, 2) for i in range(K_CHUNK_ROWS)]


def _row_window(qr, rows):
    kh = min(WIN_ROWS, rows)
    rs = int(np.clip(qr - kh // 2, 0, rows - kh))
    return rs, rs + kh


def _check_windows(rows):
    def relative(rb):
        base = rb * Q_ROWS
        return [([kr - base for kr in _window_key_rows(rb, c, rows)],
                 [tuple(r - base for r in _row_window(base + c * Q_CHUNK_ROWS + qi, rows))
                  for qi in range(Q_CHUNK_ROWS)]) for c in range(N_Q_CHUNKS)]

    n_rb = rows // Q_ROWS
    for rb in range(n_rb):
        assert rb in (0, n_rb - 1) or relative(rb) == relative(1), rb
        for c in range(N_Q_CHUNKS):
            have = set(_window_key_rows(rb, c, rows))
            for qi in range(Q_CHUNK_ROWS):
                lo, hi = _row_window(rb * Q_ROWS + c * Q_CHUNK_ROWS + qi, rows)
                assert set(range(lo, hi)) <= have, (rb, c, qi)


def _build_bias_tables(t_ref, bias_scr, rows):
    n_rb = rows // Q_ROWS
    left = lax.broadcasted_iota(jnp.int32, (GRID_W, LANES), 1) < GRID_W
    masked = jnp.full((GRID_W, LANES), MASK_VALUE, F32)
    for variant, rb in enumerate((0, 1, n_rb - 1)):
        for c in range(N_Q_CHUNKS):
            key_rows = _window_key_rows(rb, c, rows)
            for h in range(HEADS_PER_STEP):
                for qi in range(Q_CHUNK_ROWS):
                    qr = rb * Q_ROWS + c * Q_CHUNK_ROWS + qi
                    lo, hi = _row_window(qr, rows)
                    for p in range(len(key_rows) // 2):
                        pair = [t_ref[h, kr - qr + WIN_ROWS - 1] if lo <= kr < hi else None
                                for kr in key_rows[2 * p:2 * p + 2]]
                        if pair[0] is None and pair[1] is None:
                            block = masked
                        else:
                            block = jnp.where(left, masked if pair[0] is None else pair[0],
                                              masked if pair[1] is None else pair[1])
                        bias_scr[variant, c, h, qi * GRID_W:(qi + 1) * GRID_W, p * LANES:(p + 1) * LANES] = block


def _na_kernel(q_ref, k0, k1, k2, k3, v0, v1, v2, v3, kc_ref, vc_ref, t_ref, o_ref, bias_scr, s_scr, *, rows):
    n_rb = rows // Q_ROWS
    b, rb = pl.program_id(1), pl.program_id(2)

    @pl.when((b == 0) & (rb == 0))
    def _():
        _build_bias_tables(t_ref, bias_scr, rows)

    variant = jnp.where(rb == 0, 0, jnp.where(rb == n_rb - 1, 2, 1))
    tq = Q_CHUNK_ROWS * GRID_W
    tk = K_CHUNK_ROWS * GRID_W
    units = []
    for c in range(N_Q_CHUNKS):
        use_last = (rb == n_rb - 1) if c == 0 else (rb != 0)
        window = [(jnp.where(use_last, k3[0], k0[0]), jnp.where(use_last, v3[0], v0[0])),
                  (k1[0], v1[0]), (k2[0], v2[0])]
        q = q_ref[0, c * tq:(c + 1) * tq, :]
        for h in range(HEADS_PER_STEP):
            lat = [(k, v, bias_scr[variant, c, h, :, j * tk:(j + 1) * tk]) for j, (k, v) in enumerate(window)]
            units.append((q, h, lat + [(kc_ref[0], vc_ref[0], None)], s_scr.at[c, h]))
    outs = _attention_units(units)
    for c in range(N_Q_CHUNKS):
        o = _merge_heads(outs[c * HEADS_PER_STEP:(c + 1) * HEADS_PER_STEP])
        o_ref[0, c * tq:(c + 1) * tq, :] = o.astype(BF16)


def _neighborhood_attention(q, k, v, kc, vc, toeplitz, layer):
    bsz, length, _ = q.shape
    rows = length // GRID_W
    n_rb = rows // Q_ROWS
    assert rows % Q_ROWS == 0 and rows >= K_ROWS + Q_ROWS and n_rb >= 3
    _check_windows(rows)
    n_ctx = kc.shape[1]
    tq = Q_ROWS * GRID_W
    tk = K_CHUNK_ROWS * GRID_W
    assert n_ctx == tk
    max_chunk = (rows - K_ROWS) // K_CHUNK_ROWS
    half = (K_ROWS - Q_ROWS) // 2 // K_CHUNK_ROWS

    def kv_spec(j):
        def index(hp, b, rb):
            start = jnp.clip(rb * (Q_ROWS // K_CHUNK_ROWS) - half, 0, max_chunk)
            return (b, start + j, hp)
        return pl.BlockSpec((1, tk, LANES), index)

    ctx_spec = pl.BlockSpec((1, n_ctx, LANES), lambda hp, b, rb: (b, 0, hp))
    q_spec = pl.BlockSpec((1, tq, LANES), lambda hp, b, rb: (b, rb, hp))
    window = WINDOW_CHUNKS * K_CHUNK_ROWS * GRID_W
    return pl.pallas_call(
        functools.partial(_na_kernel, rows=rows),
        grid=(NA_HEADS // HEADS_PER_STEP, bsz, n_rb),
        in_specs=([q_spec] + [kv_spec(j) for j in range(N_K_CHUNKS)] + [kv_spec(j) for j in range(N_K_CHUNKS)]
                  + [ctx_spec, ctx_spec,
                     pl.BlockSpec((HEADS_PER_STEP, DR_PAD, GRID_W, LANES),
                                  lambda hp, b, rb: (layer * (NA_HEADS // HEADS_PER_STEP) + hp, 0, 0, 0))]),
        out_specs=q_spec,
        out_shape=jax.ShapeDtypeStruct((bsz, length, NA_WIDTH), BF16),
        scratch_shapes=[pltpu.VMEM((3, N_Q_CHUNKS, HEADS_PER_STEP, Q_CHUNK_ROWS * GRID_W, window), F32),
                        pltpu.VMEM((N_Q_CHUNKS, HEADS_PER_STEP, Q_CHUNK_ROWS * GRID_W, window + n_ctx), F32)],
        compiler_params=_cparams(3),
        name="neighborhood_attention",
    )(q, *([k] * N_K_CHUNKS), *([v] * N_K_CHUNKS), kc, vc, toeplitz)


def _ctx_attn_kernel(q_ref, k_ref, v_ref, o_ref, s_scr):
    tiles = [(k_ref[0], v_ref[0], None)]
    outs = _attention_units([(q_ref[0], h, tiles, s_scr.at[h]) for h in range(HEADS_PER_STEP)])
    o_ref[0] = _merge_heads(outs).astype(BF16)


def _context_attention(qc, kc, vc):
    bsz, n_ctx, _ = qc.shape
    spec = pl.BlockSpec((1, n_ctx, LANES), lambda b, hp: (b, 0, hp))
    return pl.pallas_call(
        _ctx_attn_kernel,
        grid=(bsz, NA_HEADS // HEADS_PER_STEP),
        in_specs=[spec, spec, spec],
        out_specs=spec,
        out_shape=jax.ShapeDtypeStruct((bsz, n_ctx, NA_WIDTH), BF16),
        scratch_shapes=[pltpu.VMEM((HEADS_PER_STEP, n_ctx, n_ctx), F32)],
        compiler_params=_cparams(2),
        name="context_attention",
    )(qc, kc, vc)


def _dft_cos_sin(n):
    ang = 2.0 * np.pi * np.outer(np.arange(n), np.arange(n)) / n
    return np.cos(ang), np.sin(ang)


def _bf16_table(a):
    return jnp.asarray(a, F32).astype(BF16)


def _channel_dft_matrix(n_groups=F_GROUPS):
    c, s = _dft_cos_sin(F_GROUP_DIM)
    scale = F_GROUP_DIM ** -0.5
    eye = np.eye(n_groups)
    return np.concatenate([np.kron(eye, c), np.kron(eye, s)], axis=1) * scale


def _fft_stage1_kernel(f_ref, perm_ref, w1_ref, cs_ref, sc_ref, tc_ref, ts_ref, zr_ref, zi_ref, *, n_slow, nt):
    x = f_ref[0].reshape(n_slow * nt, F_WIDTH).astype(BF16)
    x = _dot(perm_ref[...], x).astype(BF16)
    ab = [_dot(x[:, p * LANES:(p + 1) * LANES], w1_ref[...]).astype(BF16) for p in range(F_WIDTH // LANES)]
    a_all = jnp.concatenate([blk[:, :LANES] for blk in ab], axis=-1)
    b_all = jnp.concatenate([blk[:, LANES:] for blk in ab], axis=-1)
    for t in range(nt):
        a = a_all[t * n_slow:(t + 1) * n_slow]
        b = b_all[t * n_slow:(t + 1) * n_slow]
        z = _dot(cs_ref[...], a) + _dot(sc_ref[...], b)
        zr, zi = z[:n_slow], z[n_slow:]
        c, s = tc_ref[t], ts_ref[t]
        zr_ref[0, t] = (zr * c - zi * s).astype(BF16)
        zi_ref[0, t] = (zr * s + zi * c).astype(BF16)


def _fft_stage2_kernel(zr_ref, zi_ref, f_ref, y_ref, *, n_fast, kb):
    rhs = jnp.concatenate([zr_ref[0].reshape(n_fast * kb, F_WIDTH),
                           zi_ref[0].reshape(n_fast * kb, F_WIDTH)], axis=0)
    y_ref[0] = _dot(f_ref[...], rhs).astype(BF16).reshape(n_fast, kb, F_WIDTH)


def _fourier_positions(f, n_slow, n_fast):
    bsz, n, _ = f.shape
    assert n == n_slow * n_fast
    nt = SUBLANES
    kb = 2 * SUBLANES
    w1 = _bf16_table(_channel_dft_matrix(LANES // F_GROUP_DIM))
    perm = _bf16_table(np.eye(n_slow * nt).reshape(n_slow, nt, n_slow * nt).transpose(1, 0, 2)
                       .reshape(n_slow * nt, n_slow * nt))
    c1, s1 = _dft_cos_sin(n_slow)
    sc1 = n_slow ** -0.5
    cs = _bf16_table(np.concatenate([c1, s1], axis=0) * sc1)
    sc = _bf16_table(np.concatenate([-s1, c1], axis=0) * sc1)
    tw = 2.0 * np.pi * np.outer(np.arange(n_fast), np.arange(n_slow)) / n
    tc = jnp.asarray(np.cos(tw)[:, :, None], F32)
    ts = jnp.asarray(np.sin(tw)[:, :, None], F32)
    z_shape = jax.ShapeDtypeStruct((bsz, n_fast, n_slow, F_WIDTH), BF16)
    z_spec = pl.BlockSpec((1, nt, n_slow, F_WIDTH), lambda b, j: (b, j, 0, 0))
    const2 = lambda b, j: (0, 0)
    tw_spec = pl.BlockSpec((nt, n_slow, 1), lambda b, j: (j, 0, 0))
    zr, zi = pl.pallas_call(
        functools.partial(_fft_stage1_kernel, n_slow=n_slow, nt=nt),
        grid=(bsz, n_fast // nt),
        in_specs=[pl.BlockSpec((1, n_slow, nt, F_WIDTH), lambda b, j: (b, 0, j, 0)),
                  pl.BlockSpec(perm.shape, const2),
                  pl.BlockSpec(w1.shape, const2), pl.BlockSpec(cs.shape, const2), pl.BlockSpec(sc.shape, const2),
                  tw_spec, tw_spec],
        out_specs=[z_spec, z_spec],
        out_shape=[z_shape, z_shape],
        compiler_params=_cparams(2),
        name="fnet_stage1",
    )(f.reshape(bsz, n_slow, n_fast, F_WIDTH), perm, w1, cs, sc, tc, ts)

    c2, s2 = _dft_cos_sin(n_fast)
    sc2 = n_fast ** -0.5
    eye = np.eye(kb)
    f2 = _bf16_table(np.concatenate([np.kron(c2, eye), np.kron(-s2, eye)], axis=1) * sc2)
    blk = pl.BlockSpec((1, n_fast, kb, F_WIDTH), lambda b, j: (b, 0, j, 0))
    y = pl.pallas_call(
        functools.partial(_fft_stage2_kernel, n_fast=n_fast, kb=kb),
        grid=(bsz, n_slow // kb),
        in_specs=[blk, blk, pl.BlockSpec(f2.shape, const2)],
        out_specs=blk,
        out_shape=jax.ShapeDtypeStruct((bsz, n_fast, n_slow, F_WIDTH), BF16),
        compiler_params=_cparams(2),
        name="fnet_stage2",
    )(zr, zi, f2)
    return y.reshape(bsz, n, F_WIDTH)


def _second_largest_sum(a, b, c, d):
    mab, nab = jnp.maximum(a, b), jnp.minimum(a, b)
    mcd, ncd = jnp.maximum(c, d), jnp.minimum(c, d)
    return jnp.maximum(mab, mcd) + jnp.maximum(jnp.minimum(mab, mcd), jnp.maximum(nab, ncd))


def _selected_group(sb_rows):
    epg = EXPERTS_PER_GROUP
    g_score = [_second_largest_sum(*sb_rows[g * epg:(g + 1) * epg]) for g in range(N_GROUPS)]
    best = functools.reduce(jnp.maximum, g_score)
    group = jnp.full_like(best, float(N_GROUPS - 1))
    for g in range(N_GROUPS - 2, -1, -1):
        group = jnp.where(g_score[g] == best, float(g), group)
    return group


def _top2_gates(cand_s, cand_sb):
    n = len(cand_s)
    w = []
    for j in range(n):
        rank = jnp.zeros_like(cand_sb[j])
        for i in range(n):
            if i == j:
                continue
            ahead = (cand_sb[i] > cand_sb[j]) | ((cand_sb[i] == cand_sb[j]) & (i < j))
            rank = rank + jnp.where(ahead, 1.0, 0.0)
        w.append(jnp.where(rank < 2.0, cand_s[j], 0.0))
    total = functools.reduce(jnp.add, w)
    return [wj / total for wj in w]


CHUNK_ROWS = 16


def _local_rows(tm):
    need = tm + N_GROUPS * (CHUNK_ROWS - 1) + CHUNK_ROWS
    return -(-need // CHUNK_ROWS) * CHUNK_ROWS


def _sort_matrix(lslot, lr):
    r_iota = lax.broadcasted_iota(jnp.int32, (lr, 1), 0).astype(F32)
    return jnp.where(r_iota == lslot, 1.0, 0.0).astype(BF16)


def _local_sort(group, h2b, tri_ref, lr):
    t = group.shape[1]
    g_iota = lax.broadcasted_iota(jnp.int32, (SUBLANES, 1), 0).astype(F32)
    onehot = g_iota == group
    prefix = _dot(jnp.where(onehot, 1.0, 0.0).astype(BF16), tri_ref[...])
    count = prefix[:, t - 1:t]
    padded = jnp.floor((count + (CHUNK_ROWS - 1.0)) * (1.0 / CHUNK_ROWS)) * CHUNK_ROWS
    lslot = jnp.zeros((1, t), F32)
    start = jnp.zeros((1, 1), F32)
    for g in range(N_GROUPS):
        lslot = jnp.where(onehot[g:g + 1], start + prefix[g:g + 1] - 1.0, lslot)
        start = start + padded[g:g + 1]
    return lslot, count, _dot(_sort_matrix(lslot, lr), h2b).astype(BF16)


def _out_kernel(a_ref, y_ref, x_ref, m_ref, wo_ref, bd_ref, g_ref, b_ref, wr_ref, rb_ref, tri_ref,
                x1_ref, h2_ref, lslot_ref, count_ref):
    y2 = _dot(y_ref[0].astype(BF16), bd_ref[...]).astype(BF16)
    o = _dot(a_ref[0], wo_ref[:NA_WIDTH, :]) + _dot(y2, wo_ref[NA_WIDTH:, :])
    z = DEEPNORM_ALPHA * x_ref[0] + m_ref[0, 2:3, :] * o
    x1 = _layer_norm(z) * g_ref[...] + b_ref[...]
    x1_ref[0] = x1
    h2 = (_layer_norm(x1) * (1.0 + m_ref[0, 4:5, :]) + m_ref[0, 3:4, :]).astype(BF16)
    sb = jax.nn.sigmoid(_dot_nt(wr_ref[...], h2)) + rb_ref[...]
    group = _selected_group([sb[e:e + 1] for e in range(N_EXPERTS)])
    lslot, count, h2_sorted = _local_sort(group, h2, tri_ref, h2_ref.shape[0])
    h2_ref[...] = h2_sorted
    lslot_ref[0] = jnp.broadcast_to(lslot, lslot_ref.shape[1:])
    count_ref[0] = jnp.broadcast_to(count, count_ref.shape[1:])


def _out_projection(attn, yf, x, m, w_out_bf16, w_four_bd, ln_g, ln_b, w_router_t, router_bias, tm):
    bsz, length, _ = x.shape
    tiles_per_batch = length // tm
    n_tiles = bsz * tiles_per_batch
    lr = _local_rows(tm)
    tri = _bf16_table(np.triu(np.ones((tm, tm))))
    row = lambda b, i: (b, i, 0)
    tile = lambda b, i: (b * tiles_per_batch + i, 0)
    const2 = lambda b, i: (0, 0)
    return pl.pallas_call(
        _out_kernel,
        grid=(bsz, tiles_per_batch),
        in_specs=[pl.BlockSpec((1, tm, NA_WIDTH), row), pl.BlockSpec((1, tm, F_WIDTH), row),
                  pl.BlockSpec((1, tm, D_MODEL), row),
                  pl.BlockSpec((1, N_MOD, D_MODEL), lambda b, i: (b, 0, 0)),
                  pl.BlockSpec((NA_WIDTH + F_WIDTH, D_MODEL), const2),
                  pl.BlockSpec((F_WIDTH, F_WIDTH), const2),
                  pl.BlockSpec((1, D_MODEL), const2), pl.BlockSpec((1, D_MODEL), const2),
                  pl.BlockSpec((N_EXPERTS, D_MODEL), const2), pl.BlockSpec((N_EXPERTS, 1), const2),
                  pl.BlockSpec((tm, tm), const2)],
        out_specs=[pl.BlockSpec((1, tm, D_MODEL), row),
                   pl.BlockSpec((lr, D_MODEL), tile),
                   pl.BlockSpec((1, SUBLANES, tm), lambda b, i: (b, 0, i)),
                   pl.BlockSpec((1, SUBLANES, LANES), lambda b, i: (b * tiles_per_batch + i, 0, 0))],
        out_shape=[jax.ShapeDtypeStruct((bsz, length, D_MODEL), F32),
                   jax.ShapeDtypeStruct((n_tiles * lr, D_MODEL), BF16),
                   jax.ShapeDtypeStruct((bsz, SUBLANES, length), F32),
                   jax.ShapeDtypeStruct((n_tiles, SUBLANES, LANES), F32)],
        compiler_params=_cparams(2),
        name="out_proj_norm_route",
    )(attn, yf, x, m, w_out_bf16, w_four_bd, ln_g.reshape(1, D_MODEL), ln_b.reshape(1, D_MODEL),
      w_router_t, router_bias.reshape(N_EXPERTS, 1), tri)


def _group_plan(group, tm):
    n = group.shape[0]
    n_slots = n + N_GROUPS * tm
    onehot = (group[:, None] == jnp.arange(N_GROUPS, dtype=jnp.int32)[None, :]).astype(jnp.int32)
    csum = jnp.cumsum(onehot, axis=0)
    rank = jnp.sum(onehot * csum, axis=1) - 1
    count = csum[-1]
    padded = ((count + tm - 1) // tm) * tm
    end = jnp.cumsum(padded)
    start = end - padded
    slot = jnp.sum(onehot * start[None, :], axis=1) + rank
    tile_start = jnp.arange(n_slots // tm, dtype=jnp.int32) * tm
    tile_group = jnp.minimum(jnp.sum((end[None, :] <= tile_start[:, None]).astype(jnp.int32), axis=1),
                             N_GROUPS - 1)
    tile_rows = jnp.clip((start + count)[tile_group] - tile_start, 0, tm)
    return slot, tile_group, tile_rows, start, start + count, end


ROW_DMA_UNROLL = 8


def _slab(ref, token):
    return ref.at[pl.ds(pl.multiple_of(token * SLAB_ROWS, SLAB_ROWS), SLAB_ROWS), :]


def _row_scatter(idx_ref, base, src_ref, dst_hbm, sem, n_rows, wait, inverse_ref=None, src_base=0):
    def one(r, priority):
        if wait:
            pltpu.make_async_copy(_slab(src_ref, 0), _slab(dst_hbm, 0), sem).wait()
        else:
            idx = idx_ref[base + r]
            if inverse_ref is not None:
                inverse_ref[idx] = base + r
            pltpu.make_async_copy(_slab(src_ref, src_base + r), _slab(dst_hbm, idx), sem).start(priority=priority)

    def chunk(j, carry):
        for u in range(ROW_DMA_UNROLL):
            one(j * ROW_DMA_UNROLL + u, u % 2)
        return carry

    def single(r, carry):
        one(r, 0)
        return carry

    n_chunks = n_rows // ROW_DMA_UNROLL
    lax.fori_loop(0, n_chunks, chunk, 0)
    lax.fori_loop(n_chunks * ROW_DMA_UNROLL, n_rows, single, 0)


def _to_slots_kernel(slot_ref, start_ref, rows_end_ref, end_ref, h_hbm, o_hbm, dst_ref, zbuf, hbuf, sems,
                     fetch_sems, *, tm, tm_slots, n_steps):
    step = pl.program_id(0)
    n_slots = dst_ref.shape[0]
    sem = sems.at[0]

    @pl.when(step == 0)
    def _():
        zbuf[...] = jnp.zeros_like(zbuf)

        def fill(slot0):
            rows = tm_slots * SLAB_ROWS
            return pltpu.make_async_copy(zbuf, o_hbm.at[pl.ds(pl.multiple_of(slot0 * SLAB_ROWS, rows), rows), :], sem)

        for g in range(N_GROUPS):
            @pl.when(end_ref[g] > start_ref[g])
            def _():
                fill(end_ref[g] - tm_slots).start()
        for g in range(N_GROUPS):
            @pl.when(end_ref[g] > start_ref[g])
            def _():
                fill(end_ref[g] - tm_slots).wait()

        def fill_unused(t, carry):
            fill(t * tm_slots).start()
            fill(t * tm_slots).wait()
            return carry
        lax.fori_loop(end_ref[N_GROUPS - 1] // tm_slots, n_slots // tm_slots, fill_unused, 0)

        def clear(p, carry):
            dst_ref[p] = 0
            return carry
        for g in range(N_GROUPS):
            lax.fori_loop(rows_end_ref[g], end_ref[g], clear, 0)
        lax.fori_loop(end_ref[N_GROUPS - 1], n_slots, clear, 0)

    def fetch(s):
        rows = tm * SLAB_ROWS
        return pltpu.make_async_copy(h_hbm.at[pl.ds(pl.multiple_of(s * rows, rows), rows), :],
                                     hbuf.at[s % 3], fetch_sems.at[s % 3])

    def scatter(s, wait):
        _row_scatter(slot_ref, s * tm, hbuf.at[s % 3], o_hbm, sems.at[s % 2], tm, wait,
                     inverse_ref=None if wait else dst_ref)

    @pl.when(step == 0)
    def _():
        fetch(0).start()

    @pl.when(step + 1 < n_steps)
    def _():
        fetch(step + 1).start()

    fetch(step).wait()
    scatter(step, wait=False)

    @pl.when(step >= 1)
    def _():
        scatter(step - 1, wait=True)

    @pl.when(step == n_steps - 1)
    def _():
        scatter(step, wait=True)


def _to_slots(h2_slabs, slot, start, rows_end, end, tm, tm_slots):
    n = h2_slabs.shape[0] // SLAB_ROWS
    n_slots = n + N_GROUPS * tm_slots
    grid_spec = pltpu.PrefetchScalarGridSpec(
        num_scalar_prefetch=4,
        grid=(n // tm,),
        in_specs=[pl.BlockSpec(memory_space=pl.ANY)],
        out_specs=[pl.BlockSpec(memory_space=pl.ANY), pl.BlockSpec(memory_space=pltpu.SMEM)],
        scratch_shapes=[pltpu.VMEM((tm_slots * SLAB_ROWS, LANES), F32),
                        pltpu.VMEM((3, tm * SLAB_ROWS, LANES), F32),
                        pltpu.SemaphoreType.DMA((2,)), pltpu.SemaphoreType.DMA((3,))],
    )
    return pl.pallas_call(
        functools.partial(_to_slots_kernel, tm=tm, tm_slots=tm_slots, n_steps=n // tm),
        grid_spec=grid_spec,
        out_shape=[jax.ShapeDtypeStruct((n_slots * SLAB_ROWS, LANES), F32),
                   jax.ShapeDtypeStruct((n_slots,), jnp.int32)],
        compiler_params=_cparams(1),
        name="moe_rows_to_slots",
    )(slot, start, rows_end, end, h2_slabs)


def _moe_group_kernel(dst_ref, tile_group_ref, tile_rows_ref, n_used_ref, h_ref, wg_ref, wu_ref, wd_ref, wr_ref,
                      rb_ref, y_hbm, ybuf, sems, *, tm, n_tiles):
    del tile_group_ref
    step = pl.program_id(0)
    n_used = n_used_ref[0]
    buf = step % 2

    def scatter(tile, wait):
        _row_scatter(dst_ref, tile * tm, ybuf.at[tile % 2], y_hbm, sems.at[tile % 2], tile_rows_ref[tile], wait)

    @pl.when((step >= 2) & (step < n_used))
    def _():
        scatter(step - 2, wait=True)

    @pl.when(step < n_used)
    def _():
        h = _from_slabs(h_ref, tm).astype(BF16)
        s = jax.nn.sigmoid(_dot(h, wr_ref[0]))
        sb = s + rb_ref[0]
        epg = EXPERTS_PER_GROUP
        gates = _top2_gates([s[:, j:j + 1] for j in range(epg)], [sb[:, j:j + 1] for j in range(epg)])
        acc = None
        gate_up = [(_dot(h, wg_ref[0]), _dot(h, wu_ref[0]))]
        for e in range(epg):
            if e + 1 < epg:
                gate_up.append((_dot(h, wg_ref[e + 1]), _dot(h, wu_ref[e + 1])))
            gate, up = gate_up[e]
            hid = (gate * jax.nn.sigmoid(gate)) * up * gates[e]
            y = _dot(hid.astype(BF16), wd_ref[e])
            acc = y if acc is None else acc + y
        _to_slabs(ybuf.at[buf], acc)
        scatter(step, wait=False)

    @pl.when(step == n_tiles - 1)
    def _():
        @pl.when(n_used >= 2)
        def _():
            scatter(n_used - 2, wait=True)

        @pl.when(n_used >= 1)
        def _():
            scatter(n_used - 1, wait=True)


def _moe_experts(h_slots, dst, tile_group, tile_rows, n_used, n_tokens, layer, w_gate, w_up, w_down,
                 w_router_grp, router_bias_grp, tm):
    n_slots = h_slots.shape[0] // SLAB_ROWS
    n_tiles = n_slots // tm
    epg = EXPERTS_PER_GROUP
    by_group = lambda i, dst_ref, tg_ref, tr_ref, nu_ref: (tg_ref[i], 0, 0)
    by_layer_group = lambda i, dst_ref, tg_ref, tr_ref, nu_ref: (layer * N_GROUPS + tg_ref[i], 0, 0)
    grid_spec = pltpu.PrefetchScalarGridSpec(
        num_scalar_prefetch=4,
        grid=(n_tiles,),
        in_specs=[pl.BlockSpec((tm * SLAB_ROWS, LANES),
                               lambda i, dst_ref, tg_ref, tr_ref, nu_ref: (jnp.minimum(i, nu_ref[0] - 1), 0)),
                  pl.BlockSpec((epg, D_MODEL, D_EXPERT), by_layer_group),
                  pl.BlockSpec((epg, D_MODEL, D_EXPERT), by_layer_group),
                  pl.BlockSpec((epg, D_EXPERT, D_MODEL), by_layer_group),
                  pl.BlockSpec((1, D_MODEL, LANES), by_group),
                  pl.BlockSpec((1, 1, LANES), by_group)],
        out_specs=pl.BlockSpec(memory_space=pl.ANY),
        scratch_shapes=[pltpu.VMEM((2, tm * SLAB_ROWS, LANES), F32), pltpu.SemaphoreType.DMA((2,))],
    )
    return pl.pallas_call(
        functools.partial(_moe_group_kernel, tm=tm, n_tiles=n_tiles),
        grid_spec=grid_spec,
        out_shape=jax.ShapeDtypeStruct((n_tokens * SLAB_ROWS, LANES), F32),
        compiler_params=_cparams(1),
        name="moe_group_experts",
    )(dst, tile_group, tile_rows, n_used, h_slots, w_gate, w_up, w_down, w_router_grp, router_bias_grp)


def _residual_norm_kernel(y_ref, x1_ref, m_ref, g_ref, b_ref, o_ref):
    y = _from_slabs(y_ref, x1_ref.shape[0])
    z = DEEPNORM_ALPHA * x1_ref[...] + m_ref[0, 5:6, :] * y
    o_ref[...] = _layer_norm(z) * g_ref[...] + b_ref[...]


def _residual_norm(y_slabs, x1, m, ln_g, ln_b, tm):
    bsz, length, _ = x1.shape
    n = bsz * length
    tiles_per_batch = length // tm
    const2 = lambda i: (0, 0)
    row = pl.BlockSpec((tm, D_MODEL), lambda i: (i, 0))
    out = pl.pallas_call(
        _residual_norm_kernel,
        grid=(n // tm,),
        in_specs=[pl.BlockSpec((tm * SLAB_ROWS, LANES), lambda i: (i, 0)), row,
                  pl.BlockSpec((1, N_MOD, D_MODEL), lambda i: (i // tiles_per_batch, 0, 0)),
                  pl.BlockSpec((1, D_MODEL), const2), pl.BlockSpec((1, D_MODEL), const2)],
        out_specs=row,
        out_shape=jax.ShapeDtypeStruct((n, D_MODEL), F32),
        compiler_params=_cparams(1),
        name="moe_residual_norm",
    )(y_slabs, x1.reshape(n, D_MODEL), m, ln_g.reshape(1, D_MODEL), ln_b.reshape(1, D_MODEL))
    return out.reshape(bsz, length, D_MODEL)


def _grouped_moe(h2, group_rows, layer, w_gate, w_up, w_down, w_router_grp, router_bias_grp, tm_experts,
                 tm_tokens):
    bsz, _, length = group_rows.shape
    n = bsz * length
    group = group_rows[:, 0, :].reshape(n).astype(jnp.int32)
    slot, tile_group, tile_rows, start, rows_end, end = _group_plan(group, tm_experts)
    h_slots, dst = _to_slots(h2, slot, start, rows_end, end, tm_tokens, tm_experts)
    n_used = (end[N_GROUPS - 1:] // tm_experts).astype(jnp.int32)
    return _moe_experts(h_slots, dst, tile_group, tile_rows, n_used, n, layer, w_gate, w_up, w_down, w_router_grp,
                        router_bias_grp, tm_experts)


def _run_plan(count, tm, lr, tm_slots):
    n_tiles = count.shape[0]
    ch = CHUNK_ROWS
    n_slots = -(-(n_tiles * (tm + N_GROUPS * (ch - 1)) + N_GROUPS * tm_slots) // tm_slots) * tm_slots
    zero_chunk = lr // ch - 1
    pad = ((count + ch - 1) // ch) * ch
    lstart = jnp.cumsum(pad, axis=1) - pad
    run_off = jnp.cumsum(pad, axis=0) - pad
    seg_len = jnp.sum(pad, axis=0)
    seg_pad = ((seg_len + tm_slots - 1) // tm_slots) * tm_slots
    seg_end = jnp.cumsum(seg_pad)
    seg_start = seg_end - seg_pad

    s = (jnp.arange(n_slots // ch, dtype=jnp.int32) * ch)[:, None, None]
    in_run = (seg_start + run_off <= s) & (s < seg_start + run_off + pad)
    src_row = jnp.arange(n_tiles, dtype=jnp.int32)[None, :, None] * lr + lstart + (s - seg_start - run_off)
    src_row = jnp.sum(jnp.where(in_run, src_row, 0), axis=(1, 2))
    src_chunk = jnp.where(jnp.any(in_run, axis=(1, 2)), src_row // ch, zero_chunk).astype(jnp.int32)

    tile_start = jnp.arange(n_slots // tm_slots, dtype=jnp.int32) * tm_slots
    tile_group = jnp.minimum(jnp.sum((seg_end[None, :] <= tile_start[:, None]).astype(jnp.int32), axis=1),
                             N_GROUPS - 1)
    n_used = (seg_end[N_GROUPS - 1:] // tm_slots).astype(jnp.int32)

    r = (jnp.arange(lr // ch, dtype=jnp.int32) * ch)[None, :, None]
    in_local = (lstart[:, None, :] <= r) & (r < (lstart + pad)[:, None, :])
    slot = (seg_start + run_off - lstart)[:, None, :] + r
    back_chunk = (jnp.sum(jnp.where(in_local, slot, 0), axis=2) // ch).astype(jnp.int32)
    return src_chunk, tile_group.astype(jnp.int32), n_used, back_chunk.reshape(-1)


def _chunk_gather(idx_ref, tile, n_chunks, src_hbm, buf, sems, wait):
    slot = tile % 2
    rows = n_chunks * CHUNK_ROWS
    if wait:
        pltpu.make_async_copy(src_hbm.at[pl.ds(0, rows), :], buf.at[slot], sems.at[slot]).wait()
        return
    for c in range(n_chunks):
        src = pl.multiple_of(idx_ref[tile * n_chunks + c] * CHUNK_ROWS, CHUNK_ROWS)
        pltpu.make_async_copy(src_hbm.at[pl.ds(src, CHUNK_ROWS), :],
                              buf.at[slot, pl.ds(c * CHUNK_ROWS, CHUNK_ROWS), :], sems.at[slot]).start()


def _moe_runs_kernel(src_ref, tile_group_ref, n_used_ref, h_hbm, wg32_ref, wu32_ref, wd32_ref, wr_ref, rb_ref,
                     y_ref, hbuf, wg_ref, wu_ref, wd_ref, sems, *, tm):
    step = pl.program_id(0)
    n_used = n_used_ref[0]
    n_chunks = tm // CHUNK_ROWS
    new_group = (step == 0) | (tile_group_ref[step] != tile_group_ref[jnp.maximum(step - 1, 0)])

    @pl.when(new_group & (step < n_used))
    def _():
        wg_ref[...] = wg32_ref[...].astype(BF16)
        wu_ref[...] = wu32_ref[...].astype(BF16)
        wd_ref[...] = wd32_ref[...].astype(BF16)

    @pl.when((step == 0) & (n_used > 0))
    def _():
        _chunk_gather(src_ref, 0, n_chunks, h_hbm, hbuf, sems, wait=False)

    @pl.when(step + 1 < n_used)
    def _():
        _chunk_gather(src_ref, step + 1, n_chunks, h_hbm, hbuf, sems, wait=False)

    @pl.when(step < n_used)
    def _():
        _chunk_gather(src_ref, step, n_chunks, h_hbm, hbuf, sems, wait=True)
        h = hbuf[step % 2]
        s = jax.nn.sigmoid(_dot(h, wr_ref[0]))
        sb = s + rb_ref[0]
        epg = EXPERTS_PER_GROUP
        gates = _top2_gates([s[:, j:j + 1] for j in range(epg)], [sb[:, j:j + 1] for j in range(epg)])
        acc = None
        gate_up = [(_dot(h, wg_ref[0]), _dot(h, wu_ref[0]))]
        for e in range(epg):
            if e + 1 < epg:
                gate_up.append((_dot(h, wg_ref[e + 1]), _dot(h, wu_ref[e + 1])))
            gate, up = gate_up[e]
            hid = (gate * jax.nn.sigmoid(gate)) * up * gates[e]
            y = _dot(hid.astype(BF16), wd_ref[e])
            acc = y if acc is None else acc + y
        y_ref[...] = acc.astype(BF16)

    @pl.when(step >= n_used)
    def _():
        y_ref[...] = jnp.zeros_like(y_ref)


def _moe_runs(h_sorted, src_chunk, tile_group, n_used, layer, w_gate, w_up, w_down, w_router_grp,
              router_bias_grp, tm):
    n_slots = src_chunk.shape[0] * CHUNK_ROWS
    epg = EXPERTS_PER_GROUP
    by_group = lambda i, src_ref, tg_ref, nu_ref: (tg_ref[i], 0, 0)
    by_layer_group = lambda i, src_ref, tg_ref, nu_ref: (layer * N_GROUPS + tg_ref[i], 0, 0)
    grid_spec = pltpu.PrefetchScalarGridSpec(
        num_scalar_prefetch=3,
        grid=(n_slots // tm,),
        in_specs=[pl.BlockSpec(memory_space=pl.ANY),
                  pl.BlockSpec((epg, D_MODEL, D_EXPERT), by_layer_group),
                  pl.BlockSpec((epg, D_MODEL, D_EXPERT), by_layer_group),
                  pl.BlockSpec((epg, D_EXPERT, D_MODEL), by_layer_group),
                  pl.BlockSpec((1, D_MODEL, LANES), by_group),
                  pl.BlockSpec((1, 1, LANES), by_group)],
        out_specs=pl.BlockSpec((tm, D_MODEL), lambda i, src_ref, tg_ref, nu_ref: (i, 0)),
        scratch_shapes=[pltpu.VMEM((2, tm, D_MODEL), BF16),
                        pltpu.VMEM((epg, D_MODEL, D_EXPERT), BF16), pltpu.VMEM((epg, D_MODEL, D_EXPERT), BF16),
                        pltpu.VMEM((epg, D_EXPERT, D_MODEL), BF16), pltpu.SemaphoreType.DMA((2,))],
    )
    return pl.pallas_call(
        functools.partial(_moe_runs_kernel, tm=tm),
        grid_spec=grid_spec,
        out_shape=jax.ShapeDtypeStruct((n_slots, D_MODEL), BF16),
        compiler_params=_cparams(1),
        name="moe_group_experts",
    )(src_chunk, tile_group, n_used, h_sorted, w_gate, w_up, w_down, w_router_grp, router_bias_grp)


def _gather_unsort(back_ref, lslot_ref, y_hbm, ybuf, sems, step, n_steps, lr):
    n_chunks = lr // CHUNK_ROWS

    @pl.when(step == 0)
    def _():
        _chunk_gather(back_ref, 0, n_chunks, y_hbm, ybuf, sems, wait=False)

    @pl.when(step + 1 < n_steps)
    def _():
        _chunk_gather(back_ref, step + 1, n_chunks, y_hbm, ybuf, sems, wait=False)

    _chunk_gather(back_ref, step, n_chunks, y_hbm, ybuf, sems, wait=True)
    sort = _sort_matrix(lslot_ref[0, 0:1, :], lr)
    return lax.dot_general(sort, ybuf[step % 2], (((0,), (0,)), ((), ())), preferred_element_type=F32)


def _residual_norm_kernel(back_ref, y_hbm, lslot_ref, x1_ref, m_ref, g_ref, b_ref, o_ref, ybuf, sems, *, n_steps, lr):
    y = _gather_unsort(back_ref, lslot_ref, y_hbm, ybuf, sems, pl.program_id(0), n_steps, lr)
    z = DEEPNORM_ALPHA * x1_ref[...] + m_ref[0, 5:6, :] * y
    o_ref[...] = _layer_norm(z) * g_ref[...] + b_ref[...]


def _residual_norm(y_slots, back_chunk, lslot, x1, m, ln_g, ln_b, tm):
    bsz, length, _ = x1.shape
    n = bsz * length
    tiles_per_batch = length // tm
    lr = _local_rows(tm)
    const2 = lambda i, back_ref: (0, 0)
    row = pl.BlockSpec((tm, D_MODEL), lambda i, back_ref: (i, 0))
    grid_spec = pltpu.PrefetchScalarGridSpec(
        num_scalar_prefetch=1,
        grid=(n // tm,),
        in_specs=[pl.BlockSpec(memory_space=pl.ANY),
                  pl.BlockSpec((1, SUBLANES, tm),
                               lambda i, back_ref: (i // tiles_per_batch, 0, i % tiles_per_batch)),
                  row, pl.BlockSpec((1, N_MOD, D_MODEL), lambda i, back_ref: (i // tiles_per_batch, 0, 0)),
                  pl.BlockSpec((1, D_MODEL), const2), pl.BlockSpec((1, D_MODEL), const2)],
        out_specs=row,
        scratch_shapes=[pltpu.VMEM((2, lr, D_MODEL), BF16), pltpu.SemaphoreType.DMA((2,))],
    )
    out = pl.pallas_call(
        functools.partial(_residual_norm_kernel, n_steps=n // tm, lr=lr),
        grid_spec=grid_spec,
        out_shape=jax.ShapeDtypeStruct((n, D_MODEL), F32),
        compiler_params=_cparams(1),
        name="moe_residual_norm",
    )(back_chunk, y_slots, lslot, x1.reshape(n, D_MODEL), m, ln_g.reshape(1, D_MODEL), ln_b.reshape(1, D_MODEL))
    return out.reshape(bsz, length, D_MODEL)


def _grouped_moe_runs(h_sorted, count_rows, layer, w_gate, w_up, w_down, w_router_grp, router_bias_grp,
                      tm_tokens, tm_slots):
    count = count_rows[:, :N_GROUPS, 0].astype(jnp.int32)
    src_chunk, tile_group, n_used, back_chunk = _run_plan(count, tm_tokens, _local_rows(tm_tokens), tm_slots)
    y_slots = _moe_runs(h_sorted, src_chunk, tile_group, n_used, layer, w_gate, w_up, w_down, w_router_grp,
                        router_bias_grp, tm_slots)
    return y_slots, back_chunk


def _ctx_fourier_kernel(f_ref, w1_ref, c_ref, s_ref, y_ref):
    ab = _dot(f_ref[0].astype(BF16), w1_ref[...]).astype(BF16)
    y = _dot(c_ref[...], ab[:, :F_WIDTH]) + _dot(s_ref[...], ab[:, F_WIDTH:])
    y_ref[0] = y.astype(BF16)


def _context_fourier(fc):
    bsz, n, _ = fc.shape
    w1 = _bf16_table(_channel_dft_matrix())
    c, s = _dft_cos_sin(n)
    cm = _bf16_table(c * n ** -0.5)
    sm = _bf16_table(-s * n ** -0.5)
    const2 = lambda b: (0, 0)
    blk = pl.BlockSpec((1, n, F_WIDTH), lambda b: (b, 0, 0))
    return pl.pallas_call(
        _ctx_fourier_kernel,
        grid=(bsz,),
        in_specs=[blk, pl.BlockSpec(w1.shape, const2), pl.BlockSpec(cm.shape, const2),
                  pl.BlockSpec(sm.shape, const2)],
        out_specs=blk,
        out_shape=jax.ShapeDtypeStruct((bsz, n, F_WIDTH), BF16),
        compiler_params=_cparams(1),
        name="context_fnet",
    )(fc, w1, cm, sm)


def _block_diag(w):
    g, c, _ = w.shape
    eye = jnp.eye(g, dtype=w.dtype)
    return (eye[:, None, :, None] * w[:, :, None, :]).reshape(g * c, g * c)


def kernel(x, c, ctx, c_ctx, w_mod, b_mod, w_in, rpb, w_four, w_out, ln1_g, ln1_b, ln2_g, ln2_b,
           w_router, router_bias, w_gate, w_up, w_down):
    bsz, length, _ = x.shape
    n_ctx = ctx.shape[1]
    rows = length // GRID_W

    cvec = jnp.concatenate([c, c_ctx[None, :], jnp.zeros((8 - bsz - 1, D_MODEL), F32)], axis=0)
    mods = _modulation(cvec, w_mod, b_mod)
    w_router_t = w_router.T.astype(BF16)
    toeplitz = _rpb_toeplitz(rpb)
    lane_pad = LANES - EXPERTS_PER_GROUP
    w_router_grp = jnp.pad(w_router.reshape(D_MODEL, N_GROUPS, EXPERTS_PER_GROUP).transpose(1, 0, 2),
                           ((0, 0), (0, 0), (0, lane_pad))).astype(BF16)
    router_bias_grp = jnp.pad(router_bias.astype(F32).reshape(N_GROUPS, 1, EXPERTS_PER_GROUP),
                              ((0, 0), (0, 0), (0, lane_pad)))

    wg = w_gate.reshape(DEPTH * N_EXPERTS, D_MODEL, D_EXPERT)
    wu = w_up.reshape(DEPTH * N_EXPERTS, D_MODEL, D_EXPERT)
    wd = w_down.reshape(DEPTH * N_EXPERTS, D_EXPERT, D_MODEL)

    def latent_mod(i):
        return mods[i, :bsz].reshape(bsz, N_MOD, D_MODEL)

    xc = ctx
    projected = None
    for i in range(DEPTH):
        last = i == DEPTH - 1
        m = latent_mod(i)
        mc = jnp.broadcast_to(mods[i, bsz].reshape(1, N_MOD, D_MODEL), (bsz, N_MOD, D_MODEL))
        w_in_b = w_in[i].astype(BF16)
        w_out_b = w_out[i].astype(BF16)
        w_four_bd = _block_diag(w_four[i]).astype(BF16)

        q, k, v, f = projected if projected is not None else _in_projection(x, m, w_in_b, tm=512)
        qc, kc, vc, fc = _in_projection(xc, mc, w_in_b, tm=n_ctx)

        attn = _neighborhood_attention(q, k, v, kc, vc, toeplitz, i)
        yf = _fourier_positions(f, n_slow=rows, n_fast=GRID_W)
        x1, h2, lslot, cnt = _out_projection(attn, yf, x, m, w_out_b, w_four_bd, ln1_g[i], ln1_b[i],
                                             w_router_t, router_bias, tm=512)
        y, back = _grouped_moe_runs(h2, cnt, i, wg, wu, wd, w_router_grp, router_bias_grp,
                                    tm_tokens=512, tm_slots=512)
        if last:
            return _residual_norm(y, back, lslot, x1, m, ln2_g[i], ln2_b[i], tm=512)
        x, *projected = _norm_in_projection(y, back, lslot, x1, m, ln2_g[i], ln2_b[i], latent_mod(i + 1),
                                            w_in[i + 1].astype(BF16), tm=512)

        attn_c = _context_attention(qc, kc, vc)
        yc = _context_fourier(fc)
        xc1, h2c, lslot_c, cnt_c = _out_projection(attn_c, yc, xc, mc, w_out_b, w_four_bd, ln1_g[i], ln1_b[i],
                                                   w_router_t, router_bias, tm=n_ctx)
        yc2, back_c = _grouped_moe_runs(h2c, cnt_c, i, wg, wu, wd, w_router_grp, router_bias_grp,
                                        tm_tokens=n_ctx, tm_slots=128)
        xc = _residual_norm(yc2, back_c, lslot_c, xc1, mc, ln2_g[i], ln2_b[i], tm=n_ctx)
    return x
```

```python
import functools
import math

import numpy as np
import jax
import jax.numpy as jnp
from jax import lax
from jax.experimental import pallas as pl
from jax.experimental.pallas import tpu as pltpu

D_MODEL = 1024
DEPTH = 2
GRID_W = 64
NA_HEADS = 8
HEAD_DIM = 64
NA_WIDTH = NA_HEADS * HEAD_DIM
WIN_ROWS = 8
WIN_COLS = 16
F_GROUPS = 8
F_GROUP_DIM = 64
F_WIDTH = F_GROUPS * F_GROUP_DIM
IN_WIDTH = 3 * NA_WIDTH + F_WIDTH
N_EXPERTS = 16
N_GROUPS = 4
EXPERTS_PER_GROUP = N_EXPERTS // N_GROUPS
D_EXPERT = 256
N_MOD = 6
DEEPNORM_ALPHA = (2.0 * DEPTH) ** 0.25
LN_EPS = 1e-6

F32 = jnp.float32
BF16 = jnp.bfloat16

V7X_VMEM_BYTES = 64 * 1024 * 1024
VMEM_LIMIT_BYTES = (V7X_VMEM_BYTES * 3) // 4
LANES = 128
SUBLANES = 8
HEADS_PER_STEP = LANES // HEAD_DIM
assert HEADS_PER_STEP == 2
MASK_VALUE = -1e30
LOG2_E = math.log2(math.e)

Q_ROWS = 8
K_ROWS = 16
K_CHUNK_ROWS = 4
N_K_CHUNKS = K_ROWS // K_CHUNK_ROWS


def _cparams(n_grid_dims):
    return pltpu.CompilerParams(dimension_semantics=("arbitrary",) * n_grid_dims,
                                vmem_limit_bytes=VMEM_LIMIT_BYTES)


def _layer_norm(x):
    mu = jnp.mean(x, axis=-1, keepdims=True)
    xc = x - mu
    var = jnp.mean(xc * xc, axis=-1, keepdims=True)
    return xc * lax.rsqrt(var + LN_EPS)


def _dot(a, b):
    return jnp.dot(a, b, preferred_element_type=F32)


def _dot_nt(a, b):
    return lax.dot_general(a, b, (((1,), (1,)), ((), ())), preferred_element_type=F32)


def _mod_kernel(c_ref, w_ref, b_ref, o_ref):
    c = c_ref[...]
    a = c * jax.nn.sigmoid(c)
    o_ref[0] = jnp.dot(a, w_ref[0], preferred_element_type=F32, precision=lax.Precision.HIGHEST) + b_ref[0]


def _modulation(cvec, w_mod, b_mod):
    n_col_blocks = 4
    wc = (N_MOD * D_MODEL) // n_col_blocks
    rows = cvec.shape[0]
    return pl.pallas_call(
        _mod_kernel,
        grid=(DEPTH, n_col_blocks),
        in_specs=[pl.BlockSpec((rows, D_MODEL), lambda i, j: (0, 0)),
                  pl.BlockSpec((1, D_MODEL, wc), lambda i, j: (i, 0, j)),
                  pl.BlockSpec((1, 1, wc), lambda i, j: (i, 0, j))],
        out_specs=pl.BlockSpec((1, rows, wc), lambda i, j: (i, 0, j)),
        out_shape=jax.ShapeDtypeStruct((DEPTH, rows, N_MOD * D_MODEL), F32),
        compiler_params=_cparams(2),
        name="modulation",
    )(cvec, w_mod, b_mod.reshape(DEPTH, 1, N_MOD * D_MODEL))


def _modulate_project(x, m_ref, w_ref, q_ref, k_ref, v_ref, f_ref):
    h = _layer_norm(x) * (1.0 + m_ref[0, 1:2, :]) + m_ref[0, 0:1, :]
    p = _dot(h.astype(BF16), w_ref[...])
    q_ref[0] = (p[:, :NA_WIDTH] * (HEAD_DIM ** -0.5 * LOG2_E)).astype(BF16)
    k_ref[0] = p[:, NA_WIDTH:2 * NA_WIDTH].astype(BF16)
    v_ref[0] = p[:, 2 * NA_WIDTH:3 * NA_WIDTH].astype(BF16)
    f_ref[0] = p[:, 3 * NA_WIDTH:]


def _proj_kernel(x_ref, m_ref, w_ref, q_ref, k_ref, v_ref, f_ref):
    _modulate_project(x_ref[0], m_ref, w_ref, q_ref, k_ref, v_ref, f_ref)


def _norm_proj_kernel(back_ref, y_hbm, lslot_ref, x1_ref, m_ref, g_ref, b_ref, m_next_ref, w_ref,
                      x_ref, q_ref, k_ref, v_ref, f_ref, ybuf, sems, *, tiles_per_batch, n_steps, lr):
    step = pl.program_id(0) * tiles_per_batch + pl.program_id(1)
    y = _gather_unsort(back_ref, lslot_ref, y_hbm, ybuf, sems, step, n_steps, lr)
    z = DEEPNORM_ALPHA * x1_ref[0] + m_ref[0, 5:6, :] * y
    x = _layer_norm(z) * g_ref[...] + b_ref[...]
    x_ref[0] = x
    _modulate_project(x, m_next_ref, w_ref, q_ref, k_ref, v_ref, f_ref)


def _norm_in_projection(y_slots, back_chunk, lslot, x1, m, ln_g, ln_b, m_next, w_in_bf16, tm):
    bsz, length, _ = x1.shape
    tiles_per_batch = length // tm
    lr = _local_rows(tm)
    out = jax.ShapeDtypeStruct((bsz, length, NA_WIDTH), BF16)
    out_f = jax.ShapeDtypeStruct((bsz, length, F_WIDTH), F32)
    out_x = jax.ShapeDtypeStruct((bsz, length, D_MODEL), F32)
    row = lambda b, i, back_ref: (b, i, 0)
    const2 = lambda b, i, back_ref: (0, 0)
    mod_spec = pl.BlockSpec((1, N_MOD, D_MODEL), lambda b, i, back_ref: (b, 0, 0))
    o_spec = pl.BlockSpec((1, tm, NA_WIDTH), row)
    x_spec = pl.BlockSpec((1, tm, D_MODEL), row)
    grid_spec = pltpu.PrefetchScalarGridSpec(
        num_scalar_prefetch=1,
        grid=(bsz, tiles_per_batch),
        in_specs=[pl.BlockSpec(memory_space=pl.ANY),
                  pl.BlockSpec((1, SUBLANES, tm), lambda b, i, back_ref: (b, 0, i)),
                  x_spec, mod_spec,
                  pl.BlockSpec((1, D_MODEL), const2), pl.BlockSpec((1, D_MODEL), const2),
                  mod_spec, pl.BlockSpec((D_MODEL, IN_WIDTH), const2)],
        out_specs=[x_spec, o_spec, o_spec, o_spec, o_spec],
        scratch_shapes=[pltpu.VMEM((2, lr, D_MODEL), BF16), pltpu.SemaphoreType.DMA((2,))],
    )
    return pl.pallas_call(
        functools.partial(_norm_proj_kernel, tiles_per_batch=tiles_per_batch, n_steps=bsz * tiles_per_batch,
                          lr=lr),
        grid_spec=grid_spec,
        out_shape=[out_x, out, out, out, out_f],
        compiler_params=_cparams(2),
        name="moe_norm_in_proj",
    )(back_chunk, y_slots, lslot, x1, m, ln_g.reshape(1, D_MODEL), ln_b.reshape(1, D_MODEL), m_next, w_in_bf16)


def _in_projection(x, m, w_in_bf16, tm):
    bsz, length, _ = x.shape
    out = jax.ShapeDtypeStruct((bsz, length, NA_WIDTH), BF16)
    out_f = jax.ShapeDtypeStruct((bsz, length, F_WIDTH), F32)
    o_spec = pl.BlockSpec((1, tm, NA_WIDTH), lambda b, i: (b, i, 0))
    return pl.pallas_call(
        _proj_kernel,
        grid=(bsz, length // tm),
        in_specs=[pl.BlockSpec((1, tm, D_MODEL), lambda b, i: (b, i, 0)),
                  pl.BlockSpec((1, N_MOD, D_MODEL), lambda b, i: (b, 0, 0)),
                  pl.BlockSpec((D_MODEL, IN_WIDTH), lambda b, i: (0, 0))],
        out_specs=[o_spec, o_spec, o_spec, o_spec],
        out_shape=[out, out, out, out_f],
        compiler_params=_cparams(2),
        name="ln_mod_in_proj",
    )(x, m, w_in_bf16)


def _head_lanes(h):
    lane = lax.broadcasted_iota(jnp.int32, (1, LANES), 1)
    return (lane >= HEAD_DIM * h) & (lane < HEAD_DIM * (h + 1))


def _scores_pass(q, h, tiles, s_ref):
    qh = jnp.where(_head_lanes(h), q, jnp.zeros_like(q))
    m = None
    t = tiles[0][0].shape[0]
    for j, (k, _, bias) in enumerate(tiles):
        s = _dot_nt(qh, k)
        if bias is not None:
            s = s + bias
        s_ref[:, j * t:(j + 1) * t] = s
        mj = jnp.max(s, axis=-1, keepdims=True)
        m = mj if m is None else jnp.maximum(m, mj)
    return m


def _pv_pass(h, tiles, s_ref, m):
    o = None
    t = tiles[0][0].shape[0]
    in_head = _head_lanes(h)
    for j, (_, v, _) in enumerate(tiles):
        p = jnp.exp2(s_ref[:, j * t:(j + 1) * t] - m)
        oj = _dot(p.astype(BF16), jnp.where(in_head, v, jnp.ones_like(v)))
        o = oj if o is None else o + oj
    return o / pltpu.roll(o, HEAD_DIM, axis=1)


def _attention_units(units):
    outs = []
    maxima = [_scores_pass(*units[0])]
    for u in range(len(units)):
        if u + 1 < len(units):
            maxima.append(_scores_pass(*units[u + 1]))
        _, h, tiles, s_ref = units[u]
        outs.append(_pv_pass(h, tiles, s_ref, maxima[u]))
    return outs


def _merge_heads(outs):
    merged = outs[0]
    for h in range(1, len(outs)):
        merged = jnp.where(_head_lanes(h), outs[h], merged)
    return merged


DR_PAD = 2 * WIN_ROWS
DC_PAD = 2 * WIN_COLS


def _toeplitz_kernel(r_ref, sel_ref, mask_ref, o_ref):
    n = r_ref.shape[0]
    for qc in range(GRID_W):
        block = jnp.dot(r_ref[...], sel_ref[qc], preferred_element_type=F32, precision=lax.Precision.HIGHEST)
        o_ref[pl.ds(qc, n, stride=GRID_W), :] = block * LOG2_E + mask_ref[qc]


def _rpb_toeplitz(rpb):
    depth, heads, n_dr, n_dc = rpb.shape
    qc = np.arange(GRID_W)[:, None]
    kc = (np.arange(LANES) % GRID_W)[None, :]
    cs = np.clip(qc - WIN_COLS // 2, 0, GRID_W - WIN_COLS)
    col_valid = (kc >= cs) & (kc < cs + WIN_COLS)
    dc = kc - qc + WIN_COLS - 1
    select = (np.arange(DC_PAD)[None, :, None] == dc[:, None, :]) & col_valid[:, None, :]
    mask = np.where(col_valid, 0.0, MASK_VALUE)[:, None, :]
    r = jnp.pad(rpb.astype(F32), ((0, 0), (0, 0), (0, DR_PAD - n_dr), (0, DC_PAD - n_dc)))
    n = depth * heads * DR_PAD
    const3 = lambda: (0, 0, 0)
    out = pl.pallas_call(
        _toeplitz_kernel,
        grid=(),
        in_specs=[pl.BlockSpec((n, DC_PAD), lambda: (0, 0)),
                  pl.BlockSpec((GRID_W, DC_PAD, LANES), const3), pl.BlockSpec((GRID_W, 1, LANES), const3)],
        out_specs=pl.BlockSpec((n * GRID_W, LANES), lambda: (0, 0)),
        out_shape=jax.ShapeDtypeStruct((n * GRID_W, LANES), F32),
        compiler_params=pltpu.CompilerParams(vmem_limit_bytes=VMEM_LIMIT_BYTES),
        name="rpb_toeplitz",
    )(r.reshape(n, DC_PAD), jnp.asarray(select, F32), jnp.asarray(mask, F32))
    return out.reshape(depth * heads, DR_PAD, GRID_W, LANES)


Q_CHUNK_ROWS = 4
WINDOW_CHUNKS = 3
N_Q_CHUNKS = Q_ROWS // Q_CHUNK_ROWS
assert Q_CHUNK_ROWS == K_CHUNK_ROWS and WINDOW_CHUNKS * K_CHUNK_ROWS >= Q_CHUNK_ROWS + WIN_ROWS - 1
assert N_Q_CHUNKS == 2 and N_K_CHUNKS == 4


def _key_row_start(rb, rows):
    return int(np.clip(rb * Q_ROWS - (K_ROWS - Q_ROWS) // 2, 0, rows - K_ROWS))


def _window_uses_last_chunk(rb, c, n_rb):
    if c == 0:
        return rb == n_rb - 1
    return rb != 0


def _window_key_rows(rb, c, rows):
    var = N_K_CHUNKS - 1 if _window_uses_last_chunk(rb, c, rows // Q_ROWS) else 0
    k0 = _key_row_start(rb, rows)
    return [k0 + K_CHUNK_ROWS * j + i for j in (var, 1, 2) for i in range(K_CHUNK_ROWS)]


def _row_window(qr, rows):
    kh = min(WIN_ROWS, rows)
    rs = int(np.clip(qr - kh // 2, 0, rows - kh))
    return rs, rs + kh


def _check_windows(rows):
    def relative(rb):
        base = rb * Q_ROWS
        return [([kr - base for kr in _window_key_rows(rb, c, rows)],
                 [tuple(r - base for r in _row_window(base + c * Q_CHUNK_ROWS + qi, rows))
                  for qi in range(Q_CHUNK_ROWS)]) for c in range(N_Q_CHUNKS)]

    n_rb = rows // Q_ROWS
    for rb in range(n_rb):
        assert rb in (0, n_rb - 1) or relative(rb) == relative(1), rb
        for c in range(N_Q_CHUNKS):
            have = set(_window_key_rows(rb, c, rows))
            for qi in range(Q_CHUNK_ROWS):
                lo, hi = _row_window(rb * Q_ROWS + c * Q_CHUNK_ROWS + qi, rows)
                assert set(range(lo, hi)) <= have, (rb, c, qi)


def _build_bias_tables(t_ref, bias_scr, rows):
    n_rb = rows // Q_ROWS
    left = lax.broadcasted_iota(jnp.int32, (GRID_W, LANES), 1) < GRID_W
    masked = jnp.full((GRID_W, LANES), MASK_VALUE, F32)
    for variant, rb in enumerate((0, 1, n_rb - 1)):
        for c in range(N_Q_CHUNKS):
            key_rows = _window_key_rows(rb, c, rows)
            for h in range(HEADS_PER_STEP):
                for qi in range(Q_CHUNK_ROWS):
                    qr = rb * Q_ROWS + c * Q_CHUNK_ROWS + qi
                    lo, hi = _row_window(qr, rows)
                    for p in range(len(key_rows) // 2):
                        pair = [t_ref[h, kr - qr + WIN_ROWS - 1] if lo <= kr < hi else None
                                for kr in key_rows[2 * p:2 * p + 2]]
                        if pair[0] is None and pair[1] is None:
                            block = masked
                        else:
                            block = jnp.where(left, masked if pair[0] is None else pair[0],
                                              masked if pair[1] is None else pair[1])
                        bias_scr[variant, c, h, qi * GRID_W:(qi + 1) * GRID_W, p * LANES:(p + 1) * LANES] = block


def _na_kernel(q_ref, k0, k1, k2, k3, v0, v1, v2, v3, kc_ref, vc_ref, t_ref, o_ref, bias_scr, s_scr, *, rows):
    n_rb = rows // Q_ROWS
    b, rb = pl.program_id(1), pl.program_id(2)

    @pl.when((b == 0) & (rb == 0))
    def _():
        _build_bias_tables(t_ref, bias_scr, rows)

    variant = jnp.where(rb == 0, 0, jnp.where(rb == n_rb - 1, 2, 1))
    tq = Q_CHUNK_ROWS * GRID_W
    tk = K_CHUNK_ROWS * GRID_W
    units = []
    for c in range(N_Q_CHUNKS):
        use_last = (rb == n_rb - 1) if c == 0 else (rb != 0)
        window = [(jnp.where(use_last, k3[0], k0[0]), jnp.where(use_last, v3[0], v0[0])),
                  (k1[0], v1[0]), (k2[0], v2[0])]
        q = q_ref[0, c * tq:(c + 1) * tq, :]
        for h in range(HEADS_PER_STEP):
            lat = [(k, v, bias_scr[variant, c, h, :, j * tk:(j + 1) * tk]) for j, (k, v) in enumerate(window)]
            units.append((q, h, lat + [(kc_ref[0], vc_ref[0], None)], s_scr.at[c, h]))
    outs = _attention_units(units)
    for c in range(N_Q_CHUNKS):
        o = _merge_heads(outs[c * HEADS_PER_STEP:(c + 1) * HEADS_PER_STEP])
        o_ref[0, c * tq:(c + 1) * tq, :] = o.astype(BF16)


def _neighborhood_attention(q, k, v, kc, vc, toeplitz, layer):
    bsz, length, _ = q.shape
    rows = length // GRID_W
    n_rb = rows // Q_ROWS
    assert rows % Q_ROWS == 0 and rows >= K_ROWS + Q_ROWS and n_rb >= 3
    _check_windows(rows)
    n_ctx = kc.shape[1]
    tq = Q_ROWS * GRID_W
    tk = K_CHUNK_ROWS * GRID_W
    assert n_ctx == tk
    max_chunk = (rows - K_ROWS) // K_CHUNK_ROWS
    half = (K_ROWS - Q_ROWS) // 2 // K_CHUNK_ROWS

    def kv_spec(j):
        def index(hp, b, rb):
            start = jnp.clip(rb * (Q_ROWS // K_CHUNK_ROWS) - half, 0, max_chunk)
            return (b, start + j, hp)
        return pl.BlockSpec((1, tk, LANES), index)

    ctx_spec = pl.BlockSpec((1, n_ctx, LANES), lambda hp, b, rb: (b, 0, hp))
    q_spec = pl.BlockSpec((1, tq, LANES), lambda hp, b, rb: (b, rb, hp))
    window = WINDOW_CHUNKS * K_CHUNK_ROWS * GRID_W
    return pl.pallas_call(
        functools.partial(_na_kernel, rows=rows),
        grid=(NA_HEADS // HEADS_PER_STEP, bsz, n_rb),
        in_specs=([q_spec] + [kv_spec(j) for j in range(N_K_CHUNKS)] + [kv_spec(j) for j in range(N_K_CHUNKS)]
                  + [ctx_spec, ctx_spec,
                     pl.BlockSpec((HEADS_PER_STEP, DR_PAD, GRID_W, LANES),
                                  lambda hp, b, rb: (layer * (NA_HEADS // HEADS_PER_STEP) + hp, 0, 0, 0))]),
        out_specs=q_spec,
        out_shape=jax.ShapeDtypeStruct((bsz, length, NA_WIDTH), BF16),
        scratch_shapes=[pltpu.VMEM((3, N_Q_CHUNKS, HEADS_PER_STEP, Q_CHUNK_ROWS * GRID_W, window), F32),
                        pltpu.VMEM((N_Q_CHUNKS, HEADS_PER_STEP, Q_CHUNK_ROWS * GRID_W, window + n_ctx), F32)],
        compiler_params=_cparams(3),
        name="neighborhood_attention",
    )(q, *([k] * N_K_CHUNKS), *([v] * N_K_CHUNKS), kc, vc, toeplitz)


def _ctx_attn_kernel(q_ref, k_ref, v_ref, o_ref, s_scr):
    tiles = [(k_ref[0], v_ref[0], None)]
    outs = _attention_units([(q_ref[0], h, tiles, s_scr.at[h]) for h in range(HEADS_PER_STEP)])
    o_ref[0] = _merge_heads(outs).astype(BF16)


def _context_attention(qc, kc, vc):
    bsz, n_ctx, _ = qc.shape
    spec = pl.BlockSpec((1, n_ctx, LANES), lambda b, hp: (b, 0, hp))
    return pl.pallas_call(
        _ctx_attn_kernel,
        grid=(bsz, NA_HEADS // HEADS_PER_STEP),
        in_specs=[spec, spec, spec],
        out_specs=spec,
        out_shape=jax.ShapeDtypeStruct((bsz, n_ctx, NA_WIDTH), BF16),
        scratch_shapes=[pltpu.VMEM((HEADS_PER_STEP, n_ctx, n_ctx), F32)],
        compiler_params=_cparams(2),
        name="context_attention",
    )(qc, kc, vc)


def _dft_cos_sin(n):
    ang = 2.0 * np.pi * np.outer(np.arange(n), np.arange(n)) / n
    return np.cos(ang), np.sin(ang)


def _bf16_table(a):
    return jnp.asarray(a, F32).astype(BF16)


def _channel_dft_matrix(n_groups=F_GROUPS):
    c, s = _dft_cos_sin(F_GROUP_DIM)
    scale = F_GROUP_DIM ** -0.5
    eye = np.eye(n_groups)
    return np.concatenate([np.kron(eye, c), np.kron(eye, s)], axis=1) * scale


def _fft_stage1_kernel(f_ref, perm_ref, w1_ref, cs_ref, sc_ref, tc_ref, ts_ref, zr_ref, zi_ref, *, n_slow, nt):
    x = f_ref[0].reshape(n_slow * nt, F_WIDTH).astype(BF16)
    x = _dot(perm_ref[...], x).astype(BF16)
    ab = [_dot(x[:, p * LANES:(p + 1) * LANES], w1_ref[...]).astype(BF16) for p in range(F_WIDTH // LANES)]
    a_all = jnp.concatenate([blk[:, :LANES] for blk in ab], axis=-1)
    b_all = jnp.concatenate([blk[:, LANES:] for blk in ab], axis=-1)
    for t in range(nt):
        a = a_all[t * n_slow:(t + 1) * n_slow]
        b = b_all[t * n_slow:(t + 1) * n_slow]
        z = _dot(cs_ref[...], a) + _dot(sc_ref[...], b)
        zr, zi = z[:n_slow], z[n_slow:]
        c, s = tc_ref[t], ts_ref[t]
        zr_ref[0, t] = (zr * c - zi * s).astype(BF16)
        zi_ref[0, t] = (zr * s + zi * c).astype(BF16)


def _fft_stage2_kernel(zr_ref, zi_ref, f_ref, y_ref, *, n_fast, kb):
    rhs = jnp.concatenate([zr_ref[0].reshape(n_fast * kb, F_WIDTH),
                           zi_ref[0].reshape(n_fast * kb, F_WIDTH)], axis=0)
    y_ref[0] = _dot(f_ref[...], rhs).astype(BF16).reshape(n_fast, kb, F_WIDTH)


def _fourier_positions(f, n_slow, n_fast):
    bsz, n, _ = f.shape
    assert n == n_slow * n_fast
    nt = SUBLANES
    kb = 2 * SUBLANES
    w1 = _bf16_table(_channel_dft_matrix(LANES // F_GROUP_DIM))
    perm = _bf16_table(np.eye(n_slow * nt).reshape(n_slow, nt, n_slow * nt).transpose(1, 0, 2)
                       .reshape(n_slow * nt, n_slow * nt))
    c1, s1 = _dft_cos_sin(n_slow)
    sc1 = n_slow ** -0.5
    cs = _bf16_table(np.concatenate([c1, s1], axis=0) * sc1)
    sc = _bf16_table(np.concatenate([-s1, c1], axis=0) * sc1)
    tw = 2.0 * np.pi * np.outer(np.arange(n_fast), np.arange(n_slow)) / n
    tc = jnp.asarray(np.cos(tw)[:, :, None], F32)
    ts = jnp.asarray(np.sin(tw)[:, :, None], F32)
    z_shape = jax.ShapeDtypeStruct((bsz, n_fast, n_slow, F_WIDTH), BF16)
    z_spec = pl.BlockSpec((1, nt, n_slow, F_WIDTH), lambda b, j: (b, j, 0, 0))
    const2 = lambda b, j: (0, 0)
    tw_spec = pl.BlockSpec((nt, n_slow, 1), lambda b, j: (j, 0, 0))
    zr, zi = pl.pallas_call(
        functools.partial(_fft_stage1_kernel, n_slow=n_slow, nt=nt),
        grid=(bsz, n_fast // nt),
        in_specs=[pl.BlockSpec((1, n_slow, nt, F_WIDTH), lambda b, j: (b, 0, j, 0)),
                  pl.BlockSpec(perm.shape, const2),
                  pl.BlockSpec(w1.shape, const2), pl.BlockSpec(cs.shape, const2), pl.BlockSpec(sc.shape, const2),
                  tw_spec, tw_spec],
        out_specs=[z_spec, z_spec],
        out_shape=[z_shape, z_shape],
        compiler_params=_cparams(2),
        name="fnet_stage1",
    )(f.reshape(bsz, n_slow, n_fast, F_WIDTH), perm, w1, cs, sc, tc, ts)

    c2, s2 = _dft_cos_sin(n_fast)
    sc2 = n_fast ** -0.5
    eye = np.eye(kb)
    f2 = _bf16_table(np.concatenate([np.kron(c2, eye), np.kron(-s2, eye)], axis=1) * sc2)
    blk = pl.BlockSpec((1, n_fast, kb, F_WIDTH), lambda b, j: (b, 0, j, 0))
    y = pl.pallas_call(
        functools.partial(_fft_stage2_kernel, n_fast=n_fast, kb=kb),
        grid=(bsz, n_slow // kb),
        in_specs=[blk, blk, pl.BlockSpec(f2.shape, const2)],
        out_specs=blk,
        out_shape=jax.ShapeDtypeStruct((bsz, n_fast, n_slow, F_WIDTH), BF16),
        compiler_params=_cparams(2),
        name="fnet_stage2",
    )(zr, zi, f2)
    return y.reshape(bsz, n, F_WIDTH)


def _second_largest_sum(a, b, c, d):
    mab, nab = jnp.maximum(a, b), jnp.minimum(a, b)
    mcd, ncd = jnp.maximum(c, d), jnp.minimum(c, d)
    return jnp.maximum(mab, mcd) + jnp.maximum(jnp.minimum(mab, mcd), jnp.maximum(nab, ncd))


def _selected_group(sb_rows):
    epg = EXPERTS_PER_GROUP
    g_score = [_second_largest_sum(*sb_rows[g * epg:(g + 1) * epg]) for g in range(N_GROUPS)]
    best = functools.reduce(jnp.maximum, g_score)
    group = jnp.full_like(best, float(N_GROUPS - 1))
    for g in range(N_GROUPS - 2, -1, -1):
        group = jnp.where(g_score[g] == best, float(g), group)
    return group


def _top2_gates(cand_s, cand_sb):
    n = len(cand_s)
    w = []
    for j in range(n):
        rank = jnp.zeros_like(cand_sb[j])
        for i in range(n):
            if i == j:
                continue
            ahead = (cand_sb[i] > cand_sb[j]) | ((cand_sb[i] == cand_sb[j]) & (i < j))
            rank = rank + jnp.where(ahead, 1.0, 0.0)
        w.append(jnp.where(rank < 2.0, cand_s[j], 0.0))
    total = functools.reduce(jnp.add, w)
    return [wj / total for wj in w]


CHUNK_ROWS = 16


def _local_rows(tm):
    need = tm + N_GROUPS * (CHUNK_ROWS - 1) + CHUNK_ROWS
    return -(-need // CHUNK_ROWS) * CHUNK_ROWS


def _sort_matrix(lslot, lr):
    r_iota = lax.broadcasted_iota(jnp.int32, (lr, 1), 0).astype(F32)
    return jnp.where(r_iota == lslot, 1.0, 0.0).astype(BF16)


def _local_sort(group, h2b, tri_ref, lr):
    t = group.shape[1]
    g_iota = lax.broadcasted_iota(jnp.int32, (SUBLANES, 1), 0).astype(F32)
    onehot = g_iota == group
    prefix = _dot(jnp.where(onehot, 1.0, 0.0).astype(BF16), tri_ref[...])
    count = prefix[:, t - 1:t]
    padded = jnp.floor((count + (CHUNK_ROWS - 1.0)) * (1.0 / CHUNK_ROWS)) * CHUNK_ROWS
    lslot = jnp.zeros((1, t), F32)
    start = jnp.zeros((1, 1), F32)
    for g in range(N_GROUPS):
        lslot = jnp.where(onehot[g:g + 1], start + prefix[g:g + 1] - 1.0, lslot)
        start = start + padded[g:g + 1]
    return lslot, count, _dot(_sort_matrix(lslot, lr), h2b).astype(BF16)


def _out_kernel(a_ref, y_ref, x_ref, m_ref, wo_ref, bd_ref, g_ref, b_ref, wr_ref, rb_ref, tri_ref,
                x1_ref, h2_ref, lslot_ref, count_ref):
    y2 = _dot(y_ref[0].astype(BF16), bd_ref[...]).astype(BF16)
    o = _dot(a_ref[0], wo_ref[:NA_WIDTH, :]) + _dot(y2, wo_ref[NA_WIDTH:, :])
    z = DEEPNORM_ALPHA * x_ref[0] + m_ref[0, 2:3, :] * o
    x1 = _layer_norm(z) * g_ref[...] + b_ref[...]
    x1_ref[0] = x1
    h2 = (_layer_norm(x1) * (1.0 + m_ref[0, 4:5, :]) + m_ref[0, 3:4, :]).astype(BF16)
    sb = jax.nn.sigmoid(_dot_nt(wr_ref[...], h2)) + rb_ref[...]
    group = _selected_group([sb[e:e + 1] for e in range(N_EXPERTS)])
    lslot, count, h2_sorted = _local_sort(group, h2, tri_ref, h2_ref.shape[0])
    h2_ref[...] = h2_sorted
    lslot_ref[0] = jnp.broadcast_to(lslot, lslot_ref.shape[1:])
    count_ref[0] = jnp.broadcast_to(count, count_ref.shape[1:])


def _out_projection(attn, yf, x, m, w_out_bf16, w_four_bd, ln_g, ln_b, w_router_t, router_bias, tm):
    bsz, length, _ = x.shape
    tiles_per_batch = length // tm
    n_tiles = bsz * tiles_per_batch
    lr = _local_rows(tm)
    tri = _bf16_table(np.triu(np.ones((tm, tm))))
    row = lambda b, i: (b, i, 0)
    tile = lambda b, i: (b * tiles_per_batch + i, 0)
    const2 = lambda b, i: (0, 0)
    return pl.pallas_call(
        _out_kernel,
        grid=(bsz, tiles_per_batch),
        in_specs=[pl.BlockSpec((1, tm, NA_WIDTH), row), pl.BlockSpec((1, tm, F_WIDTH), row),
                  pl.BlockSpec((1, tm, D_MODEL), row),
                  pl.BlockSpec((1, N_MOD, D_MODEL), lambda b, i: (b, 0, 0)),
                  pl.BlockSpec((NA_WIDTH + F_WIDTH, D_MODEL), const2),
                  pl.BlockSpec((F_WIDTH, F_WIDTH), const2),
                  pl.BlockSpec((1, D_MODEL), const2), pl.BlockSpec((1, D_MODEL), const2),
                  pl.BlockSpec((N_EXPERTS, D_MODEL), const2), pl.BlockSpec((N_EXPERTS, 1), const2),
                  pl.BlockSpec((tm, tm), const2)],
        out_specs=[pl.BlockSpec((1, tm, D_MODEL), row),
                   pl.BlockSpec((lr, D_MODEL), tile),
                   pl.BlockSpec((1, SUBLANES, tm), lambda b, i: (b, 0, i)),
                   pl.BlockSpec((1, SUBLANES, LANES), lambda b, i: (b * tiles_per_batch + i, 0, 0))],
        out_shape=[jax.ShapeDtypeStruct((bsz, length, D_MODEL), F32),
                   jax.ShapeDtypeStruct((n_tiles * lr, D_MODEL), BF16),
                   jax.ShapeDtypeStruct((bsz, SUBLANES, length), F32),
                   jax.ShapeDtypeStruct((n_tiles, SUBLANES, LANES), F32)],
        compiler_params=_cparams(2),
        name="out_proj_norm_route",
    )(attn, yf, x, m, w_out_bf16, w_four_bd, ln_g.reshape(1, D_MODEL), ln_b.reshape(1, D_MODEL),
      w_router_t, router_bias.reshape(N_EXPERTS, 1), tri)


def _run_plan(count, tm, lr, tm_slots):
    n_tiles = count.shape[0]
    ch = CHUNK_ROWS
    n_slots = -(-(n_tiles * (tm + N_GROUPS * (ch - 1)) + N_GROUPS * tm_slots) // tm_slots) * tm_slots
    zero_chunk = lr // ch - 1
    pad = ((count + ch - 1) // ch) * ch
    lstart = jnp.cumsum(pad, axis=1) - pad
    run_off = jnp.cumsum(pad, axis=0) - pad
    seg_len = jnp.sum(pad, axis=0)
    seg_pad = ((seg_len + tm_slots - 1) // tm_slots) * tm_slots
    seg_end = jnp.cumsum(seg_pad)
    seg_start = seg_end - seg_pad

    s = (jnp.arange(n_slots // ch, dtype=jnp.int32) * ch)[:, None, None]
    in_run = (seg_start + run_off <= s) & (s < seg_start + run_off + pad)
    src_row = jnp.arange(n_tiles, dtype=jnp.int32)[None, :, None] * lr + lstart + (s - seg_start - run_off)
    src_row = jnp.sum(jnp.where(in_run, src_row, 0), axis=(1, 2))
    src_chunk = jnp.where(jnp.any(in_run, axis=(1, 2)), src_row // ch, zero_chunk).astype(jnp.int32)

    tile_start = jnp.arange(n_slots // tm_slots, dtype=jnp.int32) * tm_slots
    tile_group = jnp.minimum(jnp.sum((seg_end[None, :] <= tile_start[:, None]).astype(jnp.int32), axis=1),
                             N_GROUPS - 1)
    n_used = (seg_end[N_GROUPS - 1:] // tm_slots).astype(jnp.int32)

    r = (jnp.arange(lr // ch, dtype=jnp.int32) * ch)[None, :, None]
    in_local = (lstart[:, None, :] <= r) & (r < (lstart + pad)[:, None, :])
    slot = (seg_start + run_off - lstart)[:, None, :] + r
    back_chunk = (jnp.sum(jnp.where(in_local, slot, 0), axis=2) // ch).astype(jnp.int32)
    return src_chunk, tile_group.astype(jnp.int32), n_used, back_chunk.reshape(-1)


def _chunk_gather(idx_ref, tile, n_chunks, src_hbm, buf, sems, wait):
    slot = tile % 2
    rows = n_chunks * CHUNK_ROWS
    if wait:
        pltpu.make_async_copy(src_hbm.at[pl.ds(0, rows), :], buf.at[slot], sems.at[slot]).wait()
        return
    for c in range(n_chunks):
        src = pl.multiple_of(idx_ref[tile * n_chunks + c] * CHUNK_ROWS, CHUNK_ROWS)
        pltpu.make_async_copy(src_hbm.at[pl.ds(src, CHUNK_ROWS), :],
                              buf.at[slot, pl.ds(c * CHUNK_ROWS, CHUNK_ROWS), :], sems.at[slot]).start()


def _moe_runs_kernel(src_ref, tile_group_ref, n_used_ref, h_hbm, wg32_ref, wu32_ref, wd32_ref, wr_ref, rb_ref,
                     y_ref, hbuf, wg_ref, wu_ref, wd_ref, sems, *, tm):
    step = pl.program_id(0)
    n_used = n_used_ref[0]
    n_chunks = tm // CHUNK_ROWS
    new_group = (step == 0) | (tile_group_ref[step] != tile_group_ref[jnp.maximum(step - 1, 0)])

    @pl.when(new_group & (step < n_used))
    def _():
        wg_ref[...] = wg32_ref[...].astype(BF16)
        wu_ref[...] = wu32_ref[...].astype(BF16)
        wd_ref[...] = wd32_ref[...].astype(BF16)

    @pl.when((step == 0) & (n_used > 0))
    def _():
        _chunk_gather(src_ref, 0, n_chunks, h_hbm, hbuf, sems, wait=False)

    @pl.when(step + 1 < n_used)
    def _():
        _chunk_gather(src_ref, step + 1, n_chunks, h_hbm, hbuf, sems, wait=False)

    @pl.when(step < n_used)
    def _():
        _chunk_gather(src_ref, step, n_chunks, h_hbm, hbuf, sems, wait=True)
        h = hbuf[step % 2]
        s = jax.nn.sigmoid(_dot(h, wr_ref[0]))
        sb = s + rb_ref[0]
        epg = EXPERTS_PER_GROUP
        gates = _top2_gates([s[:, j:j + 1] for j in range(epg)], [sb[:, j:j + 1] for j in range(epg)])
        acc = None
        gate_up = [(_dot(h, wg_ref[0]), _dot(h, wu_ref[0]))]
        for e in range(epg):
            if e + 1 < epg:
                gate_up.append((_dot(h, wg_ref[e + 1]), _dot(h, wu_ref[e + 1])))
            gate, up = gate_up[e]
            hid = (gate * jax.nn.sigmoid(gate)) * up * gates[e]
            y = _dot(hid.astype(BF16), wd_ref[e])
            acc = y if acc is None else acc + y
        y_ref[...] = acc.astype(BF16)

    @pl.when(step >= n_used)
    def _():
        y_ref[...] = jnp.zeros_like(y_ref)


def _moe_runs(h_sorted, src_chunk, tile_group, n_used, layer, w_gate, w_up, w_down, w_router_grp,
              router_bias_grp, tm):
    n_slots = src_chunk.shape[0] * CHUNK_ROWS
    epg = EXPERTS_PER_GROUP
    by_group = lambda i, src_ref, tg_ref, nu_ref: (tg_ref[i], 0, 0)
    by_layer_group = lambda i, src_ref, tg_ref, nu_ref: (layer * N_GROUPS + tg_ref[i], 0, 0)
    grid_spec = pltpu.PrefetchScalarGridSpec(
        num_scalar_prefetch=3,
        grid=(n_slots // tm,),
        in_specs=[pl.BlockSpec(memory_space=pl.ANY),
                  pl.BlockSpec((epg, D_MODEL, D_EXPERT), by_layer_group),
                  pl.BlockSpec((epg, D_MODEL, D_EXPERT), by_layer_group),
                  pl.BlockSpec((epg, D_EXPERT, D_MODEL), by_layer_group),
                  pl.BlockSpec((1, D_MODEL, LANES), by_group),
                  pl.BlockSpec((1, 1, LANES), by_group)],
        out_specs=pl.BlockSpec((tm, D_MODEL), lambda i, src_ref, tg_ref, nu_ref: (i, 0)),
        scratch_shapes=[pltpu.VMEM((2, tm, D_MODEL), BF16),
                        pltpu.VMEM((epg, D_MODEL, D_EXPERT), BF16), pltpu.VMEM((epg, D_MODEL, D_EXPERT), BF16),
                        pltpu.VMEM((epg, D_EXPERT, D_MODEL), BF16), pltpu.SemaphoreType.DMA((2,))],
    )
    return pl.pallas_call(
        functools.partial(_moe_runs_kernel, tm=tm),
        grid_spec=grid_spec,
        out_shape=jax.ShapeDtypeStruct((n_slots, D_MODEL), BF16),
        compiler_params=_cparams(1),
        name="moe_group_experts",
    )(src_chunk, tile_group, n_used, h_sorted, w_gate, w_up, w_down, w_router_grp, router_bias_grp)


def _gather_unsort(back_ref, lslot_ref, y_hbm, ybuf, sems, step, n_steps, lr):
    n_chunks = lr // CHUNK_ROWS

    @pl.when(step == 0)
    def _():
        _chunk_gather(back_ref, 0, n_chunks, y_hbm, ybuf, sems, wait=False)

    @pl.when(step + 1 < n_steps)
    def _():
        _chunk_gather(back_ref, step + 1, n_chunks, y_hbm, ybuf, sems, wait=False)

    _chunk_gather(back_ref, step, n_chunks, y_hbm, ybuf, sems, wait=True)
    sort = _sort_matrix(lslot_ref[0, 0:1, :], lr)
    return lax.dot_general(sort, ybuf[step % 2], (((0,), (0,)), ((), ())), preferred_element_type=F32)


def _residual_norm_kernel(back_ref, y_hbm, lslot_ref, x1_ref, m_ref, g_ref, b_ref, o_ref, ybuf, sems, *, n_steps, lr):
    y = _gather_unsort(back_ref, lslot_ref, y_hbm, ybuf, sems, pl.program_id(0), n_steps, lr)
    z = DEEPNORM_ALPHA * x1_ref[...] + m_ref[0, 5:6, :] * y
    o_ref[...] = _layer_norm(z) * g_ref[...] + b_ref[...]


def _residual_norm(y_slots, back_chunk, lslot, x1, m, ln_g, ln_b, tm):
    bsz, length, _ = x1.shape
    n = bsz * length
    tiles_per_batch = length // tm
    lr = _local_rows(tm)
    const2 = lambda i, back_ref: (0, 0)
    row = pl.BlockSpec((tm, D_MODEL), lambda i, back_ref: (i, 0))
    grid_spec = pltpu.PrefetchScalarGridSpec(
        num_scalar_prefetch=1,
        grid=(n // tm,),
        in_specs=[pl.BlockSpec(memory_space=pl.ANY),
                  pl.BlockSpec((1, SUBLANES, tm),
                               lambda i, back_ref: (i // tiles_per_batch, 0, i % tiles_per_batch)),
                  row, pl.BlockSpec((1, N_MOD, D_MODEL), lambda i, back_ref: (i // tiles_per_batch, 0, 0)),
                  pl.BlockSpec((1, D_MODEL), const2), pl.BlockSpec((1, D_MODEL), const2)],
        out_specs=row,
        scratch_shapes=[pltpu.VMEM((2, lr, D_MODEL), BF16), pltpu.SemaphoreType.DMA((2,))],
    )
    out = pl.pallas_call(
        functools.partial(_residual_norm_kernel, n_steps=n // tm, lr=lr),
        grid_spec=grid_spec,
        out_shape=jax.ShapeDtypeStruct((n, D_MODEL), F32),
        compiler_params=_cparams(1),
        name="moe_residual_norm",
    )(back_chunk, y_slots, lslot, x1.reshape(n, D_MODEL), m, ln_g.reshape(1, D_MODEL), ln_b.reshape(1, D_MODEL))
    return out.reshape(bsz, length, D_MODEL)


def _grouped_moe_runs(h_sorted, count_rows, layer, w_gate, w_up, w_down, w_router_grp, router_bias_grp,
                      tm_tokens, tm_slots):
    count = count_rows[:, :N_GROUPS, 0].astype(jnp.int32)
    src_chunk, tile_group, n_used, back_chunk = _run_plan(count, tm_tokens, _local_rows(tm_tokens), tm_slots)
    y_slots = _moe_runs(h_sorted, src_chunk, tile_group, n_used, layer, w_gate, w_up, w_down, w_router_grp,
                        router_bias_grp, tm_slots)
    return y_slots, back_chunk


def _ctx_fourier_kernel(f_ref, w1_ref, c_ref, s_ref, y_ref):
    ab = _dot(f_ref[0].astype(BF16), w1_ref[...]).astype(BF16)
    y = _dot(c_ref[...], ab[:, :F_WIDTH]) + _dot(s_ref[...], ab[:, F_WIDTH:])
    y_ref[0] = y.astype(BF16)


def _context_fourier(fc):
    bsz, n, _ = fc.shape
    w1 = _bf16_table(_channel_dft_matrix())
    c, s = _dft_cos_sin(n)
    cm = _bf16_table(c * n ** -0.5)
    sm = _bf16_table(-s * n ** -0.5)
    const2 = lambda b: (0, 0)
    blk = pl.BlockSpec((1, n, F_WIDTH), lambda b: (b, 0, 0))
    return pl.pallas_call(
        _ctx_fourier_kernel,
        grid=(bsz,),
        in_specs=[blk, pl.BlockSpec(w1.shape, const2), pl.BlockSpec(cm.shape, const2),
                  pl.BlockSpec(sm.shape, const2)],
        out_specs=blk,
        out_shape=jax.ShapeDtypeStruct((bsz, n, F_WIDTH), BF16),
        compiler_params=_cparams(1),
        name="context_fnet",
    )(fc, w1, cm, sm)


def _block_diag(w):
    g, c, _ = w.shape
    eye = jnp.eye(g, dtype=w.dtype)
    return (eye[:, None, :, None] * w[:, :, None, :]).reshape(g * c, g * c)


def kernel(x, c, ctx, c_ctx, w_mod, b_mod, w_in, rpb, w_four, w_out, ln1_g, ln1_b, ln2_g, ln2_b,
           w_router, router_bias, w_gate, w_up, w_down):
    bsz, length, _ = x.shape
    n_ctx = ctx.shape[1]
    rows = length // GRID_W

    cvec = jnp.concatenate([c, c_ctx[None, :], jnp.zeros((8 - bsz - 1, D_MODEL), F32)], axis=0)
    mods = _modulation(cvec, w_mod, b_mod)
    w_router_t = w_router.T.astype(BF16)
    toeplitz = _rpb_toeplitz(rpb)
    lane_pad = LANES - EXPERTS_PER_GROUP
    w_router_grp = jnp.pad(w_router.reshape(D_MODEL, N_GROUPS, EXPERTS_PER_GROUP).transpose(1, 0, 2),
                           ((0, 0), (0, 0), (0, lane_pad))).astype(BF16)
    router_bias_grp = jnp.pad(router_bias.astype(F32).reshape(N_GROUPS, 1, EXPERTS_PER_GROUP),
                              ((0, 0), (0, 0), (0, lane_pad)))

    wg = w_gate.reshape(DEPTH * N_EXPERTS, D_MODEL, D_EXPERT)
    wu = w_up.reshape(DEPTH * N_EXPERTS, D_MODEL, D_EXPERT)
    wd = w_down.reshape(DEPTH * N_EXPERTS, D_EXPERT, D_MODEL)

    def latent_mod(i):
        return mods[i, :bsz].reshape(bsz, N_MOD, D_MODEL)

    xc = ctx
    projected = None
    for i in range(DEPTH):
        last = i == DEPTH - 1
        m = latent_mod(i)
        mc = jnp.broadcast_to(mods[i, bsz].reshape(1, N_MOD, D_MODEL), (bsz, N_MOD, D_MODEL))
        w_in_b = w_in[i].astype(BF16)
        w_out_b = w_out[i].astype(BF16)
        w_four_bd = _block_diag(w_four[i]).astype(BF16)

        q, k, v, f = projected if projected is not None else _in_projection(x, m, w_in_b, tm=512)
        qc, kc, vc, fc = _in_projection(xc, mc, w_in_b, tm=n_ctx)

        attn = _neighborhood_attention(q, k, v, kc, vc, toeplitz, i)
        yf = _fourier_positions(f, n_slow=rows, n_fast=GRID_W)
        x1, h2, lslot, cnt = _out_projection(attn, yf, x, m, w_out_b, w_four_bd, ln1_g[i], ln1_b[i],
                                             w_router_t, router_bias, tm=512)
        y, back = _grouped_moe_runs(h2, cnt, i, wg, wu, wd, w_router_grp, router_bias_grp,
                                    tm_tokens=512, tm_slots=512)
        if last:
            return _residual_norm(y, back, lslot, x1, m, ln2_g[i], ln2_b[i], tm=512)
        x, *projected = _norm_in_projection(y, back, lslot, x1, m, ln2_g[i], ln2_b[i], latent_mod(i + 1),
                                            w_in[i + 1].astype(BF16), tm=512)

        attn_c = _context_attention(qc, kc, vc)
        yc = _context_fourier(fc)
        xc1, h2c, lslot_c, cnt_c = _out_projection(attn_c, yc, xc, mc, w_out_b, w_four_bd, ln1_g[i], ln1_b[i],
                                                   w_router_t, router_bias, tm=n_ctx)
        yc2, back_c = _grouped_moe_runs(h2c, cnt_c, i, wg, wu, wd, w_router_grp, router_bias_grp,
                                        tm_tokens=n_ctx, tm_slots=128)
        xc = _residual_norm(yc2, back_c, lslot_c, xc1, mc, ln2_g[i], ln2_b[i], tm=n_ctx)
    return x
```

```python
import functools
import math

import numpy as np
import jax
import jax.numpy as jnp
from jax import lax
from jax.experimental import pallas as pl
from jax.experimental.pallas import tpu as pltpu

D_MODEL = 1024
DEPTH = 2
GRID_W = 64
NA_HEADS = 8
HEAD_DIM = 64
NA_WIDTH = NA_HEADS * HEAD_DIM
WIN_ROWS = 8
WIN_COLS = 16
F_GROUPS = 8
F_GROUP_DIM = 64
F_WIDTH = F_GROUPS * F_GROUP_DIM
IN_WIDTH = 3 * NA_WIDTH + F_WIDTH
N_EXPERTS = 16
N_GROUPS = 4
EXPERTS_PER_GROUP = N_EXPERTS // N_GROUPS
D_EXPERT = 256
N_MOD = 6
DEEPNORM_ALPHA = (2.0 * DEPTH) ** 0.25
LN_EPS = 1e-6

F32 = jnp.float32
BF16 = jnp.bfloat16

V7X_VMEM_BYTES = 64 * 1024 * 1024
VMEM_LIMIT_BYTES = (V7X_VMEM_BYTES * 3) // 4
LANES = 128
SUBLANES = 8
HEADS_PER_STEP = LANES // HEAD_DIM
assert HEADS_PER_STEP == 2
MASK_VALUE = -1e30
LOG2_E = math.log2(math.e)

Q_ROWS = 8
K_ROWS = 16
K_CHUNK_ROWS = 4
N_K_CHUNKS = K_ROWS // K_CHUNK_ROWS


def _cparams(n_grid_dims):
    return pltpu.CompilerParams(dimension_semantics=("arbitrary",) * n_grid_dims,
                                vmem_limit_bytes=VMEM_LIMIT_BYTES)


def _layer_norm(x):
    mu = jnp.mean(x, axis=-1, keepdims=True)
    xc = x - mu
    var = jnp.mean(xc * xc, axis=-1, keepdims=True)
    return xc * lax.rsqrt(var + LN_EPS)


def _dot(a, b):
    return jnp.dot(a, b, preferred_element_type=F32)


def _dot_nt(a, b):
    return lax.dot_general(a, b, (((1,), (1,)), ((), ())), preferred_element_type=F32)


def _mod_kernel(c_ref, w_ref, b_ref, o_ref):
    c = c_ref[...]
    a = c * jax.nn.sigmoid(c)
    o_ref[0] = jnp.dot(a, w_ref[0], preferred_element_type=F32, precision=lax.Precision.HIGHEST) + b_ref[0]


def _modulation(cvec, w_mod, b_mod):
    n_col_blocks = 4
    wc = (N_MOD * D_MODEL) // n_col_blocks
    rows = cvec.shape[0]
    return pl.pallas_call(
        _mod_kernel,
        grid=(DEPTH, n_col_blocks),
        in_specs=[pl.BlockSpec((rows, D_MODEL), lambda i, j: (0, 0)),
                  pl.BlockSpec((1, D_MODEL, wc), lambda i, j: (i, 0, j)),
                  pl.BlockSpec((1, 1, wc), lambda i, j: (i, 0, j))],
        out_specs=pl.BlockSpec((1, rows, wc), lambda i, j: (i, 0, j)),
        out_shape=jax.ShapeDtypeStruct((DEPTH, rows, N_MOD * D_MODEL), F32),
        compiler_params=_cparams(2),
        name="modulation",
    )(cvec, w_mod, b_mod.reshape(DEPTH, 1, N_MOD * D_MODEL))


def _modulate(x, m_ref):
    return (_layer_norm(x) * (1.0 + m_ref[0, 1:2, :]) + m_ref[0, 0:1, :]).astype(BF16)


def _project(h, rows, w_ref, q_ref, k_ref, v_ref, f_ref):
    p = _dot(h, w_ref[...])
    q_ref[0, rows, :] = (p[:, :NA_WIDTH] * (HEAD_DIM ** -0.5 * LOG2_E)).astype(BF16)
    k_ref[0, rows, :] = p[:, NA_WIDTH:2 * NA_WIDTH].astype(BF16)
    v_ref[0, rows, :] = p[:, 2 * NA_WIDTH:3 * NA_WIDTH].astype(BF16)
    f_ref[0, rows, :] = p[:, 3 * NA_WIDTH:]


def _modulate_project(x, m_ref, w_ref, q_ref, k_ref, v_ref, f_ref):
    _project(_modulate(x, m_ref), slice(None), w_ref, q_ref, k_ref, v_ref, f_ref)


def _proj_kernel(x_ref, m_ref, w_ref, q_ref, k_ref, v_ref, f_ref):
    tm = x_ref.shape[1]
    halves = [slice(0, tm // 2), slice(tm // 2, tm)] if tm % 512 == 0 else [slice(0, tm)]
    hs = [_modulate(x_ref[0, rows, :], m_ref) for rows in halves]
    for rows, h in zip(halves, hs):
        _project(h, rows, w_ref, q_ref, k_ref, v_ref, f_ref)


def _norm_proj_kernel(back_ref, y_hbm, lslot_ref, x1_ref, m_ref, g_ref, b_ref, m_next_ref, w_ref,
                      x_ref, q_ref, k_ref, v_ref, f_ref, ybuf, sems, *, tiles_per_batch, n_steps, lr):
    step = pl.program_id(0) * tiles_per_batch + pl.program_id(1)
    y = _gather_unsort(back_ref, lslot_ref, y_hbm, ybuf, sems, step, n_steps, lr)
    tm = x1_ref.shape[1]
    halves = [slice(0, tm // 2), slice(tm // 2, tm)]
    hs = []
    for rows in halves:
        z = DEEPNORM_ALPHA * x1_ref[0, rows, :] + m_ref[0, 5:6, :] * y[rows]
        x = _layer_norm(z) * g_ref[...] + b_ref[...]
        x_ref[0, rows, :] = x
        hs.append(_modulate(x, m_next_ref))
    for rows, h in zip(halves, hs):
        _project(h, rows, w_ref, q_ref, k_ref, v_ref, f_ref)


def _norm_in_projection(y_slots, back_chunk, lslot, x1, m, ln_g, ln_b, m_next, w_in_bf16, tm):
    bsz, length, _ = x1.shape
    tiles_per_batch = length // tm
    lr = _local_rows(tm)
    out = jax.ShapeDtypeStruct((bsz, length, NA_WIDTH), BF16)
    out_f = jax.ShapeDtypeStruct((bsz, length, F_WIDTH), F32)
    out_x = jax.ShapeDtypeStruct((bsz, length, D_MODEL), F32)
    row = lambda b, i, back_ref: (b, i, 0)
    const2 = lambda b, i, back_ref: (0, 0)
    mod_spec = pl.BlockSpec((1, N_MOD, D_MODEL), lambda b, i, back_ref: (b, 0, 0))
    o_spec = pl.BlockSpec((1, tm, NA_WIDTH), row)
    x_spec = pl.BlockSpec((1, tm, D_MODEL), row)
    grid_spec = pltpu.PrefetchScalarGridSpec(
        num_scalar_prefetch=1,
        grid=(bsz, tiles_per_batch),
        in_specs=[pl.BlockSpec(memory_space=pl.ANY),
                  pl.BlockSpec((1, SUBLANES, tm), lambda b, i, back_ref: (b, 0, i)),
                  x_spec, mod_spec,
                  pl.BlockSpec((1, D_MODEL), const2), pl.BlockSpec((1, D_MODEL), const2),
                  mod_spec, pl.BlockSpec((D_MODEL, IN_WIDTH), const2)],
        out_specs=[x_spec, o_spec, o_spec, o_spec, o_spec],
        scratch_shapes=[pltpu.VMEM((2, lr, D_MODEL), BF16), pltpu.SemaphoreType.DMA((2,))],
    )
    return pl.pallas_call(
        functools.partial(_norm_proj_kernel, tiles_per_batch=tiles_per_batch, n_steps=bsz * tiles_per_batch,
                          lr=lr),
        grid_spec=grid_spec,
        out_shape=[out_x, out, out, out, out_f],
        compiler_params=_cparams(2),
        name="moe_norm_in_proj",
    )(back_chunk, y_slots, lslot, x1, m, ln_g.reshape(1, D_MODEL), ln_b.reshape(1, D_MODEL), m_next, w_in_bf16)


def _in_projection(x, m, w_in_bf16, tm):
    bsz, length, _ = x.shape
    out = jax.ShapeDtypeStruct((bsz, length, NA_WIDTH), BF16)
    out_f = jax.ShapeDtypeStruct((bsz, length, F_WIDTH), F32)
    o_spec = pl.BlockSpec((1, tm, NA_WIDTH), lambda b, i: (b, i, 0))
    return pl.pallas_call(
        _proj_kernel,
        grid=(bsz, length // tm),
        in_specs=[pl.BlockSpec((1, tm, D_MODEL), lambda b, i: (b, i, 0)),
                  pl.BlockSpec((1, N_MOD, D_MODEL), lambda b, i: (b, 0, 0)),
                  pl.BlockSpec((D_MODEL, IN_WIDTH), lambda b, i: (0, 0))],
        out_specs=[o_spec, o_spec, o_spec, o_spec],
        out_shape=[out, out, out, out_f],
        compiler_params=_cparams(2),
        name="ln_mod_in_proj",
    )(x, m, w_in_bf16)


def _head_lanes(h):
    lane = lax.broadcasted_iota(jnp.int32, (1, LANES), 1)
    return (lane >= HEAD_DIM * h) & (lane < HEAD_DIM * (h + 1))


def _scores_pass(q, h, tiles, s_ref):
    qh = jnp.where(_head_lanes(h), q, jnp.zeros_like(q))
    m = None
    t = tiles[0][0].shape[0]
    for j, (k, _, bias) in enumerate(tiles):
        s = _dot_nt(qh, k)
        if bias is not None:
            s = s + bias
        s_ref[:, j * t:(j + 1) * t] = s
        mj = jnp.max(s, axis=-1, keepdims=True)
        m = mj if m is None else jnp.maximum(m, mj)
    return m


def _pv_pass(h, tiles, s_ref, m):
    o = None
    t = tiles[0][0].shape[0]
    in_head = _head_lanes(h)
    for j, (_, v, _) in enumerate(tiles):
        p = jnp.exp2(s_ref[:, j * t:(j + 1) * t] - m)
        oj = _dot(p.astype(BF16), jnp.where(in_head, v, jnp.ones_like(v)))
        o = oj if o is None else o + oj
    return o / pltpu.roll(o, HEAD_DIM, axis=1)


def _attention_units(units):
    outs = []
    maxima = [_scores_pass(*units[0])]
    for u in range(len(units)):
        if u + 1 < len(units):
            maxima.append(_scores_pass(*units[u + 1]))
        _, h, tiles, s_ref = units[u]
        outs.append(_pv_pass(h, tiles, s_ref, maxima[u]))
    return outs


def _merge_heads(outs):
    merged = outs[0]
    for h in range(1, len(outs)):
        merged = jnp.where(_head_lanes(h), outs[h], merged)
    return merged


DR_PAD = 2 * WIN_ROWS
DC_PAD = 2 * WIN_COLS


def _toeplitz_kernel(r_ref, sel_ref, mask_ref, o_ref):
    n = r_ref.shape[0]
    for qc in range(GRID_W):
        block = jnp.dot(r_ref[...], sel_ref[qc], preferred_element_type=F32, precision=lax.Precision.HIGHEST)
        o_ref[pl.ds(qc, n, stride=GRID_W), :] = block * LOG2_E + mask_ref[qc]


def _rpb_toeplitz(rpb):
    depth, heads, n_dr, n_dc = rpb.shape
    qc = np.arange(GRID_W)[:, None]
    kc = (np.arange(LANES) % GRID_W)[None, :]
    cs = np.clip(qc - WIN_COLS // 2, 0, GRID_W - WIN_COLS)
    col_valid = (kc >= cs) & (kc < cs + WIN_COLS)
    dc = kc - qc + WIN_COLS - 1
    select = (np.arange(DC_PAD)[None, :, None] == dc[:, None, :]) & col_valid[:, None, :]
    mask = np.where(col_valid, 0.0, MASK_VALUE)[:, None, :]
    r = jnp.pad(rpb.astype(F32), ((0, 0), (0, 0), (0, DR_PAD - n_dr), (0, DC_PAD - n_dc)))
    n = depth * heads * DR_PAD
    const3 = lambda: (0, 0, 0)
    out = pl.pallas_call(
        _toeplitz_kernel,
        grid=(),
        in_specs=[pl.BlockSpec((n, DC_PAD), lambda: (0, 0)),
                  pl.BlockSpec((GRID_W, DC_PAD, LANES), const3), pl.BlockSpec((GRID_W, 1, LANES), const3)],
        out_specs=pl.BlockSpec((n * GRID_W, LANES), lambda: (0, 0)),
        out_shape=jax.ShapeDtypeStruct((n * GRID_W, LANES), F32),
        compiler_params=pltpu.CompilerParams(vmem_limit_bytes=VMEM_LIMIT_BYTES),
        name="rpb_toeplitz",
    )(r.reshape(n, DC_PAD), jnp.asarray(select, F32), jnp.asarray(mask, F32))
    return out.reshape(depth * heads, DR_PAD, GRID_W, LANES)


Q_CHUNK_ROWS = 4
WINDOW_CHUNKS = 3
N_Q_CHUNKS = Q_ROWS // Q_CHUNK_ROWS
assert Q_CHUNK_ROWS == K_CHUNK_ROWS and WINDOW_CHUNKS * K_CHUNK_ROWS >= Q_CHUNK_ROWS + WIN_ROWS - 1
assert N_Q_CHUNKS == 2 and N_K_CHUNKS == 4


def _key_row_start(rb, rows):
    return int(np.clip(rb * Q_ROWS - (K_ROWS - Q_ROWS) // 2, 0, rows - K_ROWS))


def _window_uses_last_chunk(rb, c, n_rb):
    if c == 0:
        return rb == n_rb - 1
    return rb != 0


def _window_key_rows(rb, c, rows):
    var = N_K_CHUNKS - 1 if _window_uses_last_chunk(rb, c, rows // Q_ROWS) else 0
    k0 = _key_row_start(rb, rows)
    return [k0 + K_CHUNK_ROWS * j + i for j in (var, 1, 2) for i in range(K_CHUNK_ROWS)]


def _row_window(qr, rows):
    kh = min(WIN_ROWS, rows)
    rs = int(np.clip(qr - kh // 2, 0, rows - kh))
    return rs, rs + kh


def _check_windows(rows):
    def relative(rb):
        base = rb * Q_ROWS
        return [([kr - base for kr in _window_key_rows(rb, c, rows)],
                 [tuple(r - base for r in _row_window(base + c * Q_CHUNK_ROWS + qi, rows))
                  for qi in range(Q_CHUNK_ROWS)]) for c in range(N_Q_CHUNKS)]

    n_rb = rows // Q_ROWS
    for rb in range(n_rb):
        assert rb in (0, n_rb - 1) or relative(rb) == relative(1), rb
        for c in range(N_Q_CHUNKS):
            have = set(_window_key_rows(rb, c, rows))
            for qi in range(Q_CHUNK_ROWS):
                lo, hi = _row_window(rb * Q_ROWS + c * Q_CHUNK_ROWS + qi, rows)
                assert set(range(lo, hi)) <= have, (rb, c, qi)


def _build_bias_tables(t_ref, bias_scr, rows):
    n_rb = rows // Q_ROWS
    left = lax.broadcasted_iota(jnp.int32, (GRID_W, LANES), 1) < GRID_W
    masked = jnp.full((GRID_W, LANES), MASK_VALUE, F32)
    for variant, rb in enumerate((0, 1, n_rb - 1)):
        for c in range(N_Q_CHUNKS):
            key_rows = _window_key_rows(rb, c, rows)
            for h in range(HEADS_PER_STEP):
                for qi in range(Q_CHUNK_ROWS):
                    qr = rb * Q_ROWS + c * Q_CHUNK_ROWS + qi
                    lo, hi = _row_window(qr, rows)
                    for p in range(len(key_rows) // 2):
                        pair = [t_ref[h, kr - qr + WIN_ROWS - 1] if lo <= kr < hi else None
                                for kr in key_rows[2 * p:2 * p + 2]]
                        if pair[0] is None and pair[1] is None:
                            block = masked
                        else:
                            block = jnp.where(left, masked if pair[0] is None else pair[0],
                                              masked if pair[1] is None else pair[1])
                        bias_scr[variant, c, h, qi * GRID_W:(qi + 1) * GRID_W, p * LANES:(p + 1) * LANES] = block


def _na_kernel(q_ref, kvar0, k1, k2, kvar1, vvar0, v1, v2, vvar1, kc_ref, vc_ref, t_ref, o_ref, bias_scr, s_scr,
               *, rows):
    n_rb = rows // Q_ROWS
    b, rb = pl.program_id(1), pl.program_id(2)

    @pl.when((b == 0) & (rb == 0))
    def _():
        _build_bias_tables(t_ref, bias_scr, rows)

    variant = jnp.where(rb == 0, 0, jnp.where(rb == n_rb - 1, 2, 1))
    tq = Q_CHUNK_ROWS * GRID_W
    tk = K_CHUNK_ROWS * GRID_W
    units = []
    for c, (kvar, vvar) in enumerate([(kvar0, vvar0), (kvar1, vvar1)]):
        window = [(kvar[0], vvar[0]), (k1[0], v1[0]), (k2[0], v2[0])]
        q = q_ref[0, c * tq:(c + 1) * tq, :]
        for h in range(HEADS_PER_STEP):
            lat = [(k, v, bias_scr[variant, c, h, :, j * tk:(j + 1) * tk]) for j, (k, v) in enumerate(window)]
            units.append((q, h, lat + [(kc_ref[0], vc_ref[0], None)], s_scr.at[c, h]))
    outs = _attention_units(units)
    for c in range(N_Q_CHUNKS):
        o = _merge_heads(outs[c * HEADS_PER_STEP:(c + 1) * HEADS_PER_STEP])
        o_ref[0, c * tq:(c + 1) * tq, :] = o.astype(BF16)


def _neighborhood_attention(q, k, v, kc, vc, toeplitz, layer):
    bsz, length, _ = q.shape
    rows = length // GRID_W
    n_rb = rows // Q_ROWS
    assert rows % Q_ROWS == 0 and rows >= K_ROWS + Q_ROWS and n_rb >= 3
    _check_windows(rows)
    n_ctx = kc.shape[1]
    tq = Q_ROWS * GRID_W
    tk = K_CHUNK_ROWS * GRID_W
    assert n_ctx == tk
    max_chunk = (rows - K_ROWS) // K_CHUNK_ROWS
    half = (K_ROWS - Q_ROWS) // 2 // K_CHUNK_ROWS

    def kv_spec(j):
        def index(hp, b, rb):
            start = jnp.clip(rb * (Q_ROWS // K_CHUNK_ROWS) - half, 0, max_chunk)
            if j == "var0":
                chunk = jnp.where(rb == n_rb - 1, N_K_CHUNKS - 1, 0)
            elif j == "var1":
                chunk = jnp.where(rb == 0, 0, N_K_CHUNKS - 1)
            else:
                chunk = j
            return (b, start + chunk, hp)
        return pl.BlockSpec((1, tk, LANES), index)

    kv_chunks = ["var0", 1, 2, "var1"]

    ctx_spec = pl.BlockSpec((1, n_ctx, LANES), lambda hp, b, rb: (b, 0, hp))
    q_spec = pl.BlockSpec((1, tq, LANES), lambda hp, b, rb: (b, rb, hp))
    window = WINDOW_CHUNKS * K_CHUNK_ROWS * GRID_W
    return pl.pallas_call(
        functools.partial(_na_kernel, rows=rows),
        grid=(NA_HEADS // HEADS_PER_STEP, bsz, n_rb),
        in_specs=([q_spec] + [kv_spec(j) for j in kv_chunks] + [kv_spec(j) for j in kv_chunks]
                  + [ctx_spec, ctx_spec,
                     pl.BlockSpec((HEADS_PER_STEP, DR_PAD, GRID_W, LANES),
                                  lambda hp, b, rb: (layer * (NA_HEADS // HEADS_PER_STEP) + hp, 0, 0, 0))]),
        out_specs=q_spec,
        out_shape=jax.ShapeDtypeStruct((bsz, length, NA_WIDTH), BF16),
        scratch_shapes=[pltpu.VMEM((3, N_Q_CHUNKS, HEADS_PER_STEP, Q_CHUNK_ROWS * GRID_W, window), F32),
                        pltpu.VMEM((N_Q_CHUNKS, HEADS_PER_STEP, Q_CHUNK_ROWS * GRID_W, window + n_ctx), F32)],
        compiler_params=_cparams(3),
        name="neighborhood_attention",
    )(q, *([k] * N_K_CHUNKS), *([v] * N_K_CHUNKS), kc, vc, toeplitz)


def _ctx_attn_kernel(q_ref, k_ref, v_ref, o_ref, s_scr):
    tiles = [(k_ref[0], v_ref[0], None)]
    outs = _attention_units([(q_ref[0], h, tiles, s_scr.at[h]) for h in range(HEADS_PER_STEP)])
    o_ref[0] = _merge_heads(outs).astype(BF16)


def _context_attention(qc, kc, vc):
    bsz, n_ctx, _ = qc.shape
    spec = pl.BlockSpec((1, n_ctx, LANES), lambda b, hp: (b, 0, hp))
    return pl.pallas_call(
        _ctx_attn_kernel,
        grid=(bsz, NA_HEADS // HEADS_PER_STEP),
        in_specs=[spec, spec, spec],
        out_specs=spec,
        out_shape=jax.ShapeDtypeStruct((bsz, n_ctx, NA_WIDTH), BF16),
        scratch_shapes=[pltpu.VMEM((HEADS_PER_STEP, n_ctx, n_ctx), F32)],
        compiler_params=_cparams(2),
        name="context_attention",
    )(qc, kc, vc)


def _dft_cos_sin(n):
    ang = 2.0 * np.pi * np.outer(np.arange(n), np.arange(n)) / n
    return np.cos(ang), np.sin(ang)


def _bf16_table(a):
    return jnp.asarray(a, F32).astype(BF16)


def _channel_dft_matrix(n_groups=F_GROUPS):
    c, s = _dft_cos_sin(F_GROUP_DIM)
    scale = F_GROUP_DIM ** -0.5
    eye = np.eye(n_groups)
    return np.concatenate([np.kron(eye, c), np.kron(eye, s)], axis=1) * scale


def _fft_stage1_kernel(f_ref, perm_ref, w1_ref, cs_ref, sc_ref, tc_ref, ts_ref, zr_ref, zi_ref, *, n_slow, nt):
    x = f_ref[0].reshape(n_slow * nt, F_WIDTH).astype(BF16)
    x = _dot(perm_ref[...], x).astype(BF16)
    ab = [_dot(x[:, p * LANES:(p + 1) * LANES], w1_ref[...]).astype(BF16) for p in range(F_WIDTH // LANES)]
    a_all = jnp.concatenate([blk[:, :LANES] for blk in ab], axis=-1)
    b_all = jnp.concatenate([blk[:, LANES:] for blk in ab], axis=-1)
    for t in range(nt):
        a = a_all[t * n_slow:(t + 1) * n_slow]
        b = b_all[t * n_slow:(t + 1) * n_slow]
        z = _dot(cs_ref[...], a) + _dot(sc_ref[...], b)
        zr, zi = z[:n_slow], z[n_slow:]
        c, s = tc_ref[t], ts_ref[t]
        zr_ref[0, t] = (zr * c - zi * s).astype(BF16)
        zi_ref[0, t] = (zr * s + zi * c).astype(BF16)


def _fft_stage2_kernel(zr_ref, zi_ref, f_ref, y_ref, *, n_fast, kb):
    rhs = jnp.concatenate([zr_ref[0].reshape(n_fast * kb, F_WIDTH),
                           zi_ref[0].reshape(n_fast * kb, F_WIDTH)], axis=0)
    y_ref[0] = _dot(f_ref[...], rhs).astype(BF16).reshape(n_fast, kb, F_WIDTH)


def _fourier_positions(f, n_slow, n_fast):
    bsz, n, _ = f.shape
    assert n == n_slow * n_fast
    nt = SUBLANES
    kb = 2 * SUBLANES
    w1 = _bf16_table(_channel_dft_matrix(LANES // F_GROUP_DIM))
    perm = _bf16_table(np.eye(n_slow * nt).reshape(n_slow, nt, n_slow * nt).transpose(1, 0, 2)
                       .reshape(n_slow * nt, n_slow * nt))
    c1, s1 = _dft_cos_sin(n_slow)
    sc1 = n_slow ** -0.5
    cs = _bf16_table(np.concatenate([c1, s1], axis=0) * sc1)
    sc = _bf16_table(np.concatenate([-s1, c1], axis=0) * sc1)
    tw = 2.0 * np.pi * np.outer(np.arange(n_fast), np.arange(n_slow)) / n
    tc = jnp.asarray(np.cos(tw)[:, :, None], F32)
    ts = jnp.asarray(np.sin(tw)[:, :, None], F32)
    z_shape = jax.ShapeDtypeStruct((bsz, n_fast, n_slow, F_WIDTH), BF16)
    z_spec = pl.BlockSpec((1, nt, n_slow, F_WIDTH), lambda b, j: (b, j, 0, 0))
    const2 = lambda b, j: (0, 0)
    tw_spec = pl.BlockSpec((nt, n_slow, 1), lambda b, j: (j, 0, 0))
    zr, zi = pl.pallas_call(
        functools.partial(_fft_stage1_kernel, n_slow=n_slow, nt=nt),
        grid=(bsz, n_fast // nt),
        in_specs=[pl.BlockSpec((1, n_slow, nt, F_WIDTH), lambda b, j: (b, 0, j, 0)),
                  pl.BlockSpec(perm.shape, const2),
                  pl.BlockSpec(w1.shape, const2), pl.BlockSpec(cs.shape, const2), pl.BlockSpec(sc.shape, const2),
                  tw_spec, tw_spec],
        out_specs=[z_spec, z_spec],
        out_shape=[z_shape, z_shape],
        compiler_params=_cparams(2),
        name="fnet_stage1",
    )(f.reshape(bsz, n_slow, n_fast, F_WIDTH), perm, w1, cs, sc, tc, ts)

    c2, s2 = _dft_cos_sin(n_fast)
    sc2 = n_fast ** -0.5
    eye = np.eye(kb)
    f2 = _bf16_table(np.concatenate([np.kron(c2, eye), np.kron(-s2, eye)], axis=1) * sc2)
    blk = pl.BlockSpec((1, n_fast, kb, F_WIDTH), lambda b, j: (b, 0, j, 0))
    y = pl.pallas_call(
        functools.partial(_fft_stage2_kernel, n_fast=n_fast, kb=kb),
        grid=(bsz, n_slow // kb),
        in_specs=[blk, blk, pl.BlockSpec(f2.shape, const2)],
        out_specs=blk,
        out_shape=jax.ShapeDtypeStruct((bsz, n_fast, n_slow, F_WIDTH), BF16),
        compiler_params=_cparams(2),
        name="fnet_stage2",
    )(zr, zi, f2)
    return y.reshape(bsz, n, F_WIDTH)


def _second_largest_sum(a, b, c, d):
    mab, nab = jnp.maximum(a, b), jnp.minimum(a, b)
    mcd, ncd = jnp.maximum(c, d), jnp.minimum(c, d)
    return jnp.maximum(mab, mcd) + jnp.maximum(jnp.minimum(mab, mcd), jnp.maximum(nab, ncd))


def _selected_group(sb_rows):
    epg = EXPERTS_PER_GROUP
    g_score = [_second_largest_sum(*sb_rows[g * epg:(g + 1) * epg]) for g in range(N_GROUPS)]
    best = functools.reduce(jnp.maximum, g_score)
    group = jnp.full_like(best, float(N_GROUPS - 1))
    for g in range(N_GROUPS - 2, -1, -1):
        group = jnp.where(g_score[g] == best, float(g), group)
    return group


def _top2_gates(cand_s, cand_sb):
    n = len(cand_s)
    w = []
    for j in range(n):
        rank = jnp.zeros_like(cand_sb[j])
        for i in range(n):
            if i == j:
                continue
            ahead = (cand_sb[i] > cand_sb[j]) | ((cand_sb[i] == cand_sb[j]) & (i < j))
            rank = rank + jnp.where(ahead, 1.0, 0.0)
        w.append(jnp.where(rank < 2.0, cand_s[j], 0.0))
    total = functools.reduce(jnp.add, w)
    return [wj / total for wj in w]


CHUNK_ROWS = 16


def _local_rows(tm):
    need = tm + N_GROUPS * (CHUNK_ROWS - 1) + CHUNK_ROWS
    return -(-need // CHUNK_ROWS) * CHUNK_ROWS


def _sort_matrix(lslot, lr):
    r_iota = lax.broadcasted_iota(jnp.int32, (lr, 1), 0).astype(F32)
    return jnp.where(r_iota == lslot, 1.0, 0.0).astype(BF16)


def _local_sort(group, h2b, tri_ref, lr):
    t = group.shape[1]
    g_iota = lax.broadcasted_iota(jnp.int32, (SUBLANES, 1), 0).astype(F32)
    onehot = g_iota == group
    prefix = _dot(jnp.where(onehot, 1.0, 0.0).astype(BF16), tri_ref[...])
    count = prefix[:, t - 1:t]
    padded = jnp.floor((count + (CHUNK_ROWS - 1.0)) * (1.0 / CHUNK_ROWS)) * CHUNK_ROWS
    lslot = jnp.zeros((1, t), F32)
    start = jnp.zeros((1, 1), F32)
    for g in range(N_GROUPS):
        lslot = jnp.where(onehot[g:g + 1], start + prefix[g:g + 1] - 1.0, lslot)
        start = start + padded[g:g + 1]
    return lslot, count, _dot(_sort_matrix(lslot, lr), h2b).astype(BF16)


def _out_kernel(a_ref, y_ref, x_ref, m_ref, wo_ref, bd_ref, g_ref, b_ref, wr_ref, rb_ref, tri_ref,
                x1_ref, h2_ref, lslot_ref, count_ref):
    y2 = _dot(y_ref[0].astype(BF16), bd_ref[...]).astype(BF16)
    o = _dot(a_ref[0], wo_ref[:NA_WIDTH, :]) + _dot(y2, wo_ref[NA_WIDTH:, :])
    z = DEEPNORM_ALPHA * x_ref[0] + m_ref[0, 2:3, :] * o
    x1 = _layer_norm(z) * g_ref[...] + b_ref[...]
    x1_ref[0] = x1
    h2 = (_layer_norm(x1) * (1.0 + m_ref[0, 4:5, :]) + m_ref[0, 3:4, :]).astype(BF16)
    sb = jax.nn.sigmoid(_dot_nt(wr_ref[...], h2)) + rb_ref[...]
    group = _selected_group([sb[e:e + 1] for e in range(N_EXPERTS)])
    lslot, count, h2_sorted = _local_sort(group, h2, tri_ref, h2_ref.shape[0])
    h2_ref[...] = h2_sorted
    lslot_ref[0] = jnp.broadcast_to(lslot, lslot_ref.shape[1:])
    count_ref[0] = jnp.broadcast_to(count, count_ref.shape[1:])


def _out_projection(attn, yf, x, m, w_out_bf16, w_four_bd, ln_g, ln_b, w_router_t, router_bias, tm):
    bsz, length, _ = x.shape
    tiles_per_batch = length // tm
    n_tiles = bsz * tiles_per_batch
    lr = _local_rows(tm)
    tri = _bf16_table(np.triu(np.ones((tm, tm))))
    row = lambda b, i: (b, i, 0)
    tile = lambda b, i: (b * tiles_per_batch + i, 0)
    const2 = lambda b, i: (0, 0)
    return pl.pallas_call(
        _out_kernel,
        grid=(bsz, tiles_per_batch),
        in_specs=[pl.BlockSpec((1, tm, NA_WIDTH), row), pl.BlockSpec((1, tm, F_WIDTH), row),
                  pl.BlockSpec((1, tm, D_MODEL), row),
                  pl.BlockSpec((1, N_MOD, D_MODEL), lambda b, i: (b, 0, 0)),
                  pl.BlockSpec((NA_WIDTH + F_WIDTH, D_MODEL), const2),
                  pl.BlockSpec((F_WIDTH, F_WIDTH), const2),
                  pl.BlockSpec((1, D_MODEL), const2), pl.BlockSpec((1, D_MODEL), const2),
                  pl.BlockSpec((N_EXPERTS, D_MODEL), const2), pl.BlockSpec((N_EXPERTS, 1), const2),
                  pl.BlockSpec((tm, tm), const2)],
        out_specs=[pl.BlockSpec((1, tm, D_MODEL), row),
                   pl.BlockSpec((lr, D_MODEL), tile),
                   pl.BlockSpec((1, SUBLANES, tm), lambda b, i: (b, 0, i)),
                   pl.BlockSpec((1, SUBLANES, LANES), lambda b, i: (b * tiles_per_batch + i, 0, 0))],
        out_shape=[jax.ShapeDtypeStruct((bsz, length, D_MODEL), F32),
                   jax.ShapeDtypeStruct((n_tiles * lr, D_MODEL), BF16),
                   jax.ShapeDtypeStruct((bsz, SUBLANES, length), F32),
                   jax.ShapeDtypeStruct((n_tiles, SUBLANES, LANES), F32)],
        compiler_params=_cparams(2),
        name="out_proj_norm_route",
    )(attn, yf, x, m, w_out_bf16, w_four_bd, ln_g.reshape(1, D_MODEL), ln_b.reshape(1, D_MODEL),
      w_router_t, router_bias.reshape(N_EXPERTS, 1), tri)


def _run_plan(count, tm, lr, tm_slots):
    n_tiles = count.shape[0]
    ch = CHUNK_ROWS
    n_slots = -(-(n_tiles * (tm + N_GROUPS * (ch - 1)) + N_GROUPS * tm_slots) // tm_slots) * tm_slots
    zero_chunk = lr // ch - 1
    pad = ((count + ch - 1) // ch) * ch
    lstart = jnp.cumsum(pad, axis=1) - pad
    run_off = jnp.cumsum(pad, axis=0) - pad
    seg_len = jnp.sum(pad, axis=0)
    seg_pad = ((seg_len + tm_slots - 1) // tm_slots) * tm_slots
    seg_end = jnp.cumsum(seg_pad)
    seg_start = seg_end - seg_pad

    s = (jnp.arange(n_slots // ch, dtype=jnp.int32) * ch)[:, None, None]
    in_run = (seg_start + run_off <= s) & (s < seg_start + run_off + pad)
    src_row = jnp.arange(n_tiles, dtype=jnp.int32)[None, :, None] * lr + lstart + (s - seg_start - run_off)
    src_row = jnp.sum(jnp.where(in_run, src_row, 0), axis=(1, 2))
    src_chunk = jnp.where(jnp.any(in_run, axis=(1, 2)), src_row // ch, zero_chunk).astype(jnp.int32)

    tile_start = jnp.arange(n_slots // tm_slots, dtype=jnp.int32) * tm_slots
    tile_group = jnp.minimum(jnp.sum((seg_end[None, :] <= tile_start[:, None]).astype(jnp.int32), axis=1),
                             N_GROUPS - 1)
    n_used = (seg_end[N_GROUPS - 1:] // tm_slots).astype(jnp.int32)

    r = (jnp.arange(lr // ch, dtype=jnp.int32) * ch)[None, :, None]
    in_local = (lstart[:, None, :] <= r) & (r < (lstart + pad)[:, None, :])
    slot = (seg_start + run_off - lstart)[:, None, :] + r
    back_chunk = (jnp.sum(jnp.where(in_local, slot, 0), axis=2) // ch).astype(jnp.int32)
    return src_chunk, tile_group.astype(jnp.int32), n_used, back_chunk.reshape(-1)


def _chunk_gather(idx_ref, tile, n_chunks, src_hbm, buf, sems, wait):
    slot = tile % 2
    rows = n_chunks * CHUNK_ROWS
    if wait:
        pltpu.make_async_copy(src_hbm.at[pl.ds(0, rows), :], buf.at[slot], sems.at[slot]).wait()
        return
    for c in range(n_chunks):
        src = pl.multiple_of(idx_ref[tile * n_chunks + c] * CHUNK_ROWS, CHUNK_ROWS)
        pltpu.make_async_copy(src_hbm.at[pl.ds(src, CHUNK_ROWS), :],
                              buf.at[slot, pl.ds(c * CHUNK_ROWS, CHUNK_ROWS), :], sems.at[slot]).start()


def _moe_runs_kernel(src_ref, tile_group_ref, n_used_ref, h_hbm, wg32_ref, wu32_ref, wd32_ref, wr_ref, rb_ref,
                     y_ref, hbuf, wg_ref, wu_ref, wd_ref, sems, *, tm):
    step = pl.program_id(0)
    n_used = n_used_ref[0]
    n_chunks = tm // CHUNK_ROWS
    new_group = (step == 0) | (tile_group_ref[step] != tile_group_ref[jnp.maximum(step - 1, 0)])

    @pl.when(new_group & (step < n_used))
    def _():
        wg_ref[...] = wg32_ref[...].astype(BF16)
        wu_ref[...] = wu32_ref[...].astype(BF16)
        wd_ref[...] = wd32_ref[...].astype(BF16)

    @pl.when((step == 0) & (n_used > 0))
    def _():
        _chunk_gather(src_ref, 0, n_chunks, h_hbm, hbuf, sems, wait=False)

    @pl.when(step + 1 < n_used)
    def _():
        _chunk_gather(src_ref, step + 1, n_chunks, h_hbm, hbuf, sems, wait=False)

    @pl.when(step < n_used)
    def _():
        _chunk_gather(src_ref, step, n_chunks, h_hbm, hbuf, sems, wait=True)
        h = hbuf[step % 2]
        s = jax.nn.sigmoid(_dot(h, wr_ref[0]))
        sb = s + rb_ref[0]
        epg = EXPERTS_PER_GROUP
        gates = _top2_gates([s[:, j:j + 1] for j in range(epg)], [sb[:, j:j + 1] for j in range(epg)])
        acc = None
        gate_up = [(_dot(h, wg_ref[0]), _dot(h, wu_ref[0]))]
        for e in range(epg):
            if e + 1 < epg:
                gate_up.append((_dot(h, wg_ref[e + 1]), _dot(h, wu_ref[e + 1])))
            gate, up = gate_up[e]
            hid = (gate * jax.nn.sigmoid(gate)) * up * gates[e]
            y = _dot(hid.astype(BF16), wd_ref[e])
            acc = y if acc is None else acc + y
        y_ref[...] = acc.astype(BF16)

    @pl.when(step >= n_used)
    def _():
        y_ref[...] = jnp.zeros_like(y_ref)


def _moe_runs(h_sorted, src_chunk, tile_group, n_used, layer, w_gate, w_up, w_down, w_router_grp,
              router_bias_grp, tm):
    n_slots = src_chunk.shape[0] * CHUNK_ROWS
    epg = EXPERTS_PER_GROUP
    by_group = lambda i, src_ref, tg_ref, nu_ref: (tg_ref[i], 0, 0)
    by_layer_group = lambda i, src_ref, tg_ref, nu_ref: (layer * N_GROUPS + tg_ref[i], 0, 0)
    grid_spec = pltpu.PrefetchScalarGridSpec(
        num_scalar_prefetch=3,
        grid=(n_slots // tm,),
        in_specs=[pl.BlockSpec(memory_space=pl.ANY),
                  pl.BlockSpec((epg, D_MODEL, D_EXPERT), by_layer_group),
                  pl.BlockSpec((epg, D_MODEL, D_EXPERT), by_layer_group),
                  pl.BlockSpec((epg, D_EXPERT, D_MODEL), by_layer_group),
                  pl.BlockSpec((1, D_MODEL, LANES), by_group),
                  pl.BlockSpec((1, 1, LANES), by_group)],
        out_specs=pl.BlockSpec((tm, D_MODEL), lambda i, src_ref, tg_ref, nu_ref: (i, 0)),
        scratch_shapes=[pltpu.VMEM((2, tm, D_MODEL), BF16),
                        pltpu.VMEM((epg, D_MODEL, D_EXPERT), BF16), pltpu.VMEM((epg, D_MODEL, D_EXPERT), BF16),
                        pltpu.VMEM((epg, D_EXPERT, D_MODEL), BF16), pltpu.SemaphoreType.DMA((2,))],
    )
    return pl.pallas_call(
        functools.partial(_moe_runs_kernel, tm=tm),
        grid_spec=grid_spec,
        out_shape=jax.ShapeDtypeStruct((n_slots, D_MODEL), BF16),
        compiler_params=_cparams(1),
        name="moe_group_experts",
    )(src_chunk, tile_group, n_used, h_sorted, w_gate, w_up, w_down, w_router_grp, router_bias_grp)


def _gather_unsort(back_ref, lslot_ref, y_hbm, ybuf, sems, step, n_steps, lr):
    n_chunks = lr // CHUNK_ROWS

    @pl.when(step == 0)
    def _():
        _chunk_gather(back_ref, 0, n_chunks, y_hbm, ybuf, sems, wait=False)

    @pl.when(step + 1 < n_steps)
    def _():
        _chunk_gather(back_ref, step + 1, n_chunks, y_hbm, ybuf, sems, wait=False)

    _chunk_gather(back_ref, step, n_chunks, y_hbm, ybuf, sems, wait=True)
    sort = _sort_matrix(lslot_ref[0, 0:1, :], lr)
    return lax.dot_general(sort, ybuf[step % 2], (((0,), (0,)), ((), ())), preferred_element_type=F32)


def _residual_norm_kernel(back_ref, y_hbm, lslot_ref, x1_ref, m_ref, g_ref, b_ref, o_ref, ybuf, sems, *, n_steps, lr):
    y = _gather_unsort(back_ref, lslot_ref, y_hbm, ybuf, sems, pl.program_id(0), n_steps, lr)
    z = DEEPNORM_ALPHA * x1_ref[...] + m_ref[0, 5:6, :] * y
    o_ref[...] = _layer_norm(z) * g_ref[...] + b_ref[...]


def _residual_norm(y_slots, back_chunk, lslot, x1, m, ln_g, ln_b, tm):
    bsz, length, _ = x1.shape
    n = bsz * length
    tiles_per_batch = length // tm
    lr = _local_rows(tm)
    const2 = lambda i, back_ref: (0, 0)
    row = pl.BlockSpec((tm, D_MODEL), lambda i, back_ref: (i, 0))
    grid_spec = pltpu.PrefetchScalarGridSpec(
        num_scalar_prefetch=1,
        grid=(n // tm,),
        in_specs=[pl.BlockSpec(memory_space=pl.ANY),
                  pl.BlockSpec((1, SUBLANES, tm),
                               lambda i, back_ref: (i // tiles_per_batch, 0, i % tiles_per_batch)),
                  row, pl.BlockSpec((1, N_MOD, D_MODEL), lambda i, back_ref: (i // tiles_per_batch, 0, 0)),
                  pl.BlockSpec((1, D_MODEL), const2), pl.BlockSpec((1, D_MODEL), const2)],
        out_specs=row,
        scratch_shapes=[pltpu.VMEM((2, lr, D_MODEL), BF16), pltpu.SemaphoreType.DMA((2,))],
    )
    out = pl.pallas_call(
        functools.partial(_residual_norm_kernel, n_steps=n // tm, lr=lr),
        grid_spec=grid_spec,
        out_shape=jax.ShapeDtypeStruct((n, D_MODEL), F32),
        compiler_params=_cparams(1),
        name="moe_residual_norm",
    )(back_chunk, y_slots, lslot, x1.reshape(n, D_MODEL), m, ln_g.reshape(1, D_MODEL), ln_b.reshape(1, D_MODEL))
    return out.reshape(bsz, length, D_MODEL)


def _grouped_moe_runs(h_sorted, count_rows, layer, w_gate, w_up, w_down, w_router_grp, router_bias_grp,
                      tm_tokens, tm_slots):
    count = count_rows[:, :N_GROUPS, 0].astype(jnp.int32)
    src_chunk, tile_group, n_used, back_chunk = _run_plan(count, tm_tokens, _local_rows(tm_tokens), tm_slots)
    y_slots = _moe_runs(h_sorted, src_chunk, tile_group, n_used, layer, w_gate, w_up, w_down, w_router_grp,
                        router_bias_grp, tm_slots)
    return y_slots, back_chunk


def _ctx_fourier_kernel(f_ref, w1_ref, c_ref, s_ref, y_ref):
    ab = _dot(f_ref[0].astype(BF16), w1_ref[...]).astype(BF16)
    y = _dot(c_ref[...], ab[:, :F_WIDTH]) + _dot(s_ref[...], ab[:, F_WIDTH:])
    y_ref[0] = y.astype(BF16)


def _context_fourier(fc):
    bsz, n, _ = fc.shape
    w1 = _bf16_table(_channel_dft_matrix())
    c, s = _dft_cos_sin(n)
    cm = _bf16_table(c * n ** -0.5)
    sm = _bf16_table(-s * n ** -0.5)
    const2 = lambda b: (0, 0)
    blk = pl.BlockSpec((1, n, F_WIDTH), lambda b: (b, 0, 0))
    return pl.pallas_call(
        _ctx_fourier_kernel,
        grid=(bsz,),
        in_specs=[blk, pl.BlockSpec(w1.shape, const2), pl.BlockSpec(cm.shape, const2),
                  pl.BlockSpec(sm.shape, const2)],
        out_specs=blk,
        out_shape=jax.ShapeDtypeStruct((bsz, n, F_WIDTH), BF16),
        compiler_params=_cparams(1),
        name="context_fnet",
    )(fc, w1, cm, sm)


def _block_diag(w):
    g, c, _ = w.shape
    eye = jnp.eye(g, dtype=w.dtype)
    return (eye[:, None, :, None] * w[:, :, None, :]).reshape(g * c, g * c)


def kernel(x, c, ctx, c_ctx, w_mod, b_mod, w_in, rpb, w_four, w_out, ln1_g, ln1_b, ln2_g, ln2_b,
           w_router, router_bias, w_gate, w_up, w_down):
    bsz, length, _ = x.shape
    n_ctx = ctx.shape[1]
    rows = length // GRID_W

    cvec = jnp.concatenate([c, c_ctx[None, :], jnp.zeros((8 - bsz - 1, D_MODEL), F32)], axis=0)
    mods = _modulation(cvec, w_mod, b_mod)
    w_router_t = w_router.T.astype(BF16)
    toeplitz = _rpb_toeplitz(rpb)
    lane_pad = LANES - EXPERTS_PER_GROUP
    w_router_grp = jnp.pad(w_router.reshape(D_MODEL, N_GROUPS, EXPERTS_PER_GROUP).transpose(1, 0, 2),
                           ((0, 0), (0, 0), (0, lane_pad))).astype(BF16)
    router_bias_grp = jnp.pad(router_bias.astype(F32).reshape(N_GROUPS, 1, EXPERTS_PER_GROUP),
                              ((0, 0), (0, 0), (0, lane_pad)))

    wg = w_gate.reshape(DEPTH * N_EXPERTS, D_MODEL, D_EXPERT)
    wu = w_up.reshape(DEPTH * N_EXPERTS, D_MODEL, D_EXPERT)
    wd = w_down.reshape(DEPTH * N_EXPERTS, D_EXPERT, D_MODEL)

    def latent_mod(i):
        return mods[i, :bsz].reshape(bsz, N_MOD, D_MODEL)

    xc = ctx
    projected = None
    for i in range(DEPTH):
        last = i == DEPTH - 1
        m = latent_mod(i)
        mc = jnp.broadcast_to(mods[i, bsz].reshape(1, N_MOD, D_MODEL), (bsz, N_MOD, D_MODEL))
        w_in_b = w_in[i].astype(BF16)
        w_out_b = w_out[i].astype(BF16)
        w_four_bd = _block_diag(w_four[i]).astype(BF16)

        q, k, v, f = projected if projected is not None else _in_projection(x, m, w_in_b, tm=1024)
        qc, kc, vc, fc = _in_projection(xc, mc, w_in_b, tm=n_ctx)

        attn = _neighborhood_attention(q, k, v, kc, vc, toeplitz, i)
        yf = _fourier_positions(f, n_slow=rows, n_fast=GRID_W)
        x1, h2, lslot, cnt = _out_projection(attn, yf, x, m, w_out_b, w_four_bd, ln1_g[i], ln1_b[i],
                                             w_router_t, router_bias, tm=512)
        y, back = _grouped_moe_runs(h2, cnt, i, wg, wu, wd, w_router_grp, router_bias_grp,
                                    tm_tokens=512, tm_slots=512)
        if last:
            return _residual_norm(y, back, lslot, x1, m, ln2_g[i], ln2_b[i], tm=512)
        x, *projected = _norm_in_projection(y, back, lslot, x1, m, ln2_g[i], ln2_b[i], latent_mod(i + 1),
                                            w_in[i + 1].astype(BF16), tm=512)

        attn_c = _context_attention(qc, kc, vc)
        yc = _context_fourier(fc)
        xc1, h2c, lslot_c, cnt_c = _out_projection(attn_c, yc, xc, mc, w_out_b, w_four_bd, ln1_g[i], ln1_b[i],
                                                   w_router_t, router_bias, tm=n_ctx)
        yc2, back_c = _grouped_moe_runs(h2c, cnt_c, i, wg, wu, wd, w_router_grp, router_bias_grp,
                                        tm_tokens=n_ctx, tm_slots=128)
        xc = _residual_norm(yc2, back_c, lslot_c, xc1, mc, ln2_g[i], ln2_b[i], tm=n_ctx)
    return x
```

```python
import functools
import math

import numpy as np
import jax
import jax.numpy as jnp
from jax import lax
from jax.experimental import pallas as pl
from jax.experimental.pallas import tpu as pltpu

D_MODEL = 1024
DEPTH = 2
GRID_W = 64
NA_HEADS = 8
HEAD_DIM = 64
NA_WIDTH = NA_HEADS * HEAD_DIM
WIN_ROWS = 8
WIN_COLS = 16
F_GROUPS = 8
F_GROUP_DIM = 64
F_WIDTH = F_GROUPS * F_GROUP_DIM
IN_WIDTH = 3 * NA_WIDTH + F_WIDTH
N_EXPERTS = 16
N_GROUPS = 4
EXPERTS_PER_GROUP = N_EXPERTS // N_GROUPS
D_EXPERT = 256
N_MOD = 6
DEEPNORM_ALPHA = (2.0 * DEPTH) ** 0.25
LN_EPS = 1e-6

F32 = jnp.float32
BF16 = jnp.bfloat16

V7X_VMEM_BYTES = 64 * 1024 * 1024
VMEM_LIMIT_BYTES = (V7X_VMEM_BYTES * 3) // 4
LANES = 128
SUBLANES = 8
HEADS_PER_STEP = LANES // HEAD_DIM
assert HEADS_PER_STEP == 2
MASK_VALUE = -1e30
LOG2_E = math.log2(math.e)

Q_ROWS = 8
K_ROWS = 16
K_CHUNK_ROWS = 4
N_K_CHUNKS = K_ROWS // K_CHUNK_ROWS


def _cparams(n_grid_dims):
    return pltpu.CompilerParams(dimension_semantics=("arbitrary",) * n_grid_dims,
                                vmem_limit_bytes=VMEM_LIMIT_BYTES)


def _layer_norm(x):
    mu = jnp.mean(x, axis=-1, keepdims=True)
    xc = x - mu
    var = jnp.mean(xc * xc, axis=-1, keepdims=True)
    return xc * lax.rsqrt(var + LN_EPS)


def _dot(a, b):
    return jnp.dot(a, b, preferred_element_type=F32)


def _dot_nt(a, b):
    return lax.dot_general(a, b, (((1,), (1,)), ((), ())), preferred_element_type=F32)


def _mod_kernel(c_ref, w_ref, b_ref, o_ref):
    c = c_ref[...]
    a = c * jax.nn.sigmoid(c)
    o_ref[0] = jnp.dot(a, w_ref[0], preferred_element_type=F32, precision=lax.Precision.HIGHEST) + b_ref[0]


def _modulation(cvec, w_mod, b_mod):
    n_col_blocks = 4
    wc = (N_MOD * D_MODEL) // n_col_blocks
    rows = cvec.shape[0]
    return pl.pallas_call(
        _mod_kernel,
        grid=(DEPTH, n_col_blocks),
        in_specs=[pl.BlockSpec((rows, D_MODEL), lambda i, j: (0, 0)),
                  pl.BlockSpec((1, D_MODEL, wc), lambda i, j: (i, 0, j)),
                  pl.BlockSpec((1, 1, wc), lambda i, j: (i, 0, j))],
        out_specs=pl.BlockSpec((1, rows, wc), lambda i, j: (i, 0, j)),
        out_shape=jax.ShapeDtypeStruct((DEPTH, rows, N_MOD * D_MODEL), F32),
        compiler_params=_cparams(2),
        name="modulation",
    )(cvec, w_mod, b_mod.reshape(DEPTH, 1, N_MOD * D_MODEL))


def _modulate(x, m_ref):
    return (_layer_norm(x) * (1.0 + m_ref[0, 1:2, :]) + m_ref[0, 0:1, :]).astype(BF16)


def _project(h, rows, w_ref, q_ref, k_ref, v_ref, f_ref):
    p = _dot(h, w_ref[...])
    q_ref[0, rows, :] = (p[:, :NA_WIDTH] * (HEAD_DIM ** -0.5 * LOG2_E)).astype(BF16)
    k_ref[0, rows, :] = p[:, NA_WIDTH:2 * NA_WIDTH].astype(BF16)
    v_ref[0, rows, :] = p[:, 2 * NA_WIDTH:3 * NA_WIDTH].astype(BF16)
    f_ref[0, rows, :] = p[:, 3 * NA_WIDTH:]


def _proj_kernel(x_ref, m_ref, w_ref, q_ref, k_ref, v_ref, f_ref):
    tm = x_ref.shape[1]
    halves = [slice(0, tm // 2), slice(tm // 2, tm)] if tm % 512 == 0 else [slice(0, tm)]
    hs = [_modulate(x_ref[0, rows, :], m_ref) for rows in halves]
    for rows, h in zip(halves, hs):
        _project(h, rows, w_ref, q_ref, k_ref, v_ref, f_ref)


def _norm_proj_kernel(back_ref, y_hbm, lslot_ref, x1_ref, m_ref, g_ref, b_ref, m_next_ref, w_ref,
                      x_ref, q_ref, k_ref, v_ref, f_ref, ybuf, sems, *, tiles_per_batch, n_steps, lr):
    step = pl.program_id(0) * tiles_per_batch + pl.program_id(1)
    y = _gather_unsort(back_ref, lslot_ref, y_hbm, ybuf, sems, step, n_steps, lr)
    z = DEEPNORM_ALPHA * x1_ref[0] + m_ref[0, 5:6, :] * y
    x = _layer_norm(z) * g_ref[...] + b_ref[...]
    x_ref[0] = x
    _project(_modulate(x, m_next_ref), slice(None), w_ref, q_ref, k_ref, v_ref, f_ref)


def _norm_in_projection(y_slots, back_chunk, lslot, x1, m, ln_g, ln_b, m_next, w_in_bf16, tm):
    bsz, length, _ = x1.shape
    tiles_per_batch = length // tm
    lr = _local_rows(tm)
    out = jax.ShapeDtypeStruct((bsz, length, NA_WIDTH), BF16)
    out_f = jax.ShapeDtypeStruct((bsz, length, F_WIDTH), F32)
    out_x = jax.ShapeDtypeStruct((bsz, length, D_MODEL), F32)
    row = lambda b, i, back_ref: (b, i, 0)
    const2 = lambda b, i, back_ref: (0, 0)
    mod_spec = pl.BlockSpec((1, N_MOD, D_MODEL), lambda b, i, back_ref: (b, 0, 0))
    o_spec = pl.BlockSpec((1, tm, NA_WIDTH), row)
    x_spec = pl.BlockSpec((1, tm, D_MODEL), row)
    grid_spec = pltpu.PrefetchScalarGridSpec(
        num_scalar_prefetch=1,
        grid=(bsz, tiles_per_batch),
        in_specs=[pl.BlockSpec(memory_space=pl.ANY),
                  pl.BlockSpec((1, SUBLANES, tm), lambda b, i, back_ref: (b, 0, i)),
                  x_spec, mod_spec,
                  pl.BlockSpec((1, D_MODEL), const2), pl.BlockSpec((1, D_MODEL), const2),
                  mod_spec, pl.BlockSpec((D_MODEL, IN_WIDTH), const2)],
        out_specs=[x_spec, o_spec, o_spec, o_spec, o_spec],
        scratch_shapes=[pltpu.VMEM((2, lr, D_MODEL), BF16), pltpu.SemaphoreType.DMA((2,))],
    )
    return pl.pallas_call(
        functools.partial(_norm_proj_kernel, tiles_per_batch=tiles_per_batch, n_steps=bsz * tiles_per_batch,
                          lr=lr),
        grid_spec=grid_spec,
        out_shape=[out_x, out, out, out, out_f],
        compiler_params=_cparams(2),
        name="moe_norm_in_proj",
    )(back_chunk, y_slots, lslot, x1, m, ln_g.reshape(1, D_MODEL), ln_b.reshape(1, D_MODEL), m_next, w_in_bf16)


def _in_projection(x, m, w_in_bf16, tm):
    bsz, length, _ = x.shape
    out = jax.ShapeDtypeStruct((bsz, length, NA_WIDTH), BF16)
    out_f = jax.ShapeDtypeStruct((bsz, length, F_WIDTH), F32)
    o_spec = pl.BlockSpec((1, tm, NA_WIDTH), lambda b, i: (b, i, 0))
    return pl.pallas_call(
        _proj_kernel,
        grid=(bsz, length // tm),
        in_specs=[pl.BlockSpec((1, tm, D_MODEL), lambda b, i: (b, i, 0)),
                  pl.BlockSpec((1, N_MOD, D_MODEL), lambda b, i: (b, 0, 0)),
                  pl.BlockSpec((D_MODEL, IN_WIDTH), lambda b, i: (0, 0))],
        out_specs=[o_spec, o_spec, o_spec, o_spec],
        out_shape=[out, out, out, out_f],
        compiler_params=_cparams(2),
        name="ln_mod_in_proj",
    )(x, m, w_in_bf16)


def _head_lanes(h):
    lane = lax.broadcasted_iota(jnp.int32, (1, LANES), 1)
    return (lane >= HEAD_DIM * h) & (lane < HEAD_DIM * (h + 1))


def _scores_pass(q, h, tiles, s_ref):
    qh = jnp.where(_head_lanes(h), q, jnp.zeros_like(q))
    m = None
    t = tiles[0][0].shape[0]
    for j, (k, _, bias) in enumerate(tiles):
        s = _dot_nt(qh, k)
        if bias is not None:
            s = s + bias
        s_ref[:, j * t:(j + 1) * t] = s
        mj = jnp.max(s, axis=-1, keepdims=True)
        m = mj if m is None else jnp.maximum(m, mj)
    return m


def _pv_pass(h, tiles, s_ref, m):
    o = None
    t = tiles[0][0].shape[0]
    in_head = _head_lanes(h)
    for j, (_, v, _) in enumerate(tiles):
        p = jnp.exp2(s_ref[:, j * t:(j + 1) * t] - m)
        oj = _dot(p.astype(BF16), jnp.where(in_head, v, jnp.ones_like(v)))
        o = oj if o is None else o + oj
    return o / pltpu.roll(o, HEAD_DIM, axis=1)


def _attention_units(units):
    outs = []
    maxima = [_scores_pass(*units[0])]
    for u in range(len(units)):
        if u + 1 < len(units):
            maxima.append(_scores_pass(*units[u + 1]))
        _, h, tiles, s_ref = units[u]
        outs.append(_pv_pass(h, tiles, s_ref, maxima[u]))
    return outs


def _merge_heads(outs):
    merged = outs[0]
    for h in range(1, len(outs)):
        merged = jnp.where(_head_lanes(h), outs[h], merged)
    return merged


DR_PAD = 2 * WIN_ROWS
DC_PAD = 2 * WIN_COLS


def _toeplitz_kernel(r_ref, sel_ref, mask_ref, o_ref):
    n = r_ref.shape[0]
    for qc in range(GRID_W):
        block = jnp.dot(r_ref[...], sel_ref[qc], preferred_element_type=F32, precision=lax.Precision.HIGHEST)
        o_ref[pl.ds(qc, n, stride=GRID_W), :] = block * LOG2_E + mask_ref[qc]


def _rpb_toeplitz(rpb):
    depth, heads, n_dr, n_dc = rpb.shape
    qc = np.arange(GRID_W)[:, None]
    kc = (np.arange(LANES) % GRID_W)[None, :]
    cs = np.clip(qc - WIN_COLS // 2, 0, GRID_W - WIN_COLS)
    col_valid = (kc >= cs) & (kc < cs + WIN_COLS)
    dc = kc - qc + WIN_COLS - 1
    select = (np.arange(DC_PAD)[None, :, None] == dc[:, None, :]) & col_valid[:, None, :]
    mask = np.where(col_valid, 0.0, MASK_VALUE)[:, None, :]
    r = jnp.pad(rpb.astype(F32), ((0, 0), (0, 0), (0, DR_PAD - n_dr), (0, DC_PAD - n_dc)))
    n = depth * heads * DR_PAD
    const3 = lambda: (0, 0, 0)
    out = pl.pallas_call(
        _toeplitz_kernel,
        grid=(),
        in_specs=[pl.BlockSpec((n, DC_PAD), lambda: (0, 0)),
                  pl.BlockSpec((GRID_W, DC_PAD, LANES), const3), pl.BlockSpec((GRID_W, 1, LANES), const3)],
        out_specs=pl.BlockSpec((n * GRID_W, LANES), lambda: (0, 0)),
        out_shape=jax.ShapeDtypeStruct((n * GRID_W, LANES), F32),
        compiler_params=pltpu.CompilerParams(vmem_limit_bytes=VMEM_LIMIT_BYTES),
        name="rpb_toeplitz",
    )(r.reshape(n, DC_PAD), jnp.asarray(select, F32), jnp.asarray(mask, F32))
    return out.reshape(depth * heads, DR_PAD, GRID_W, LANES)


Q_CHUNK_ROWS = 4
WINDOW_CHUNKS = 3
N_Q_CHUNKS = Q_ROWS // Q_CHUNK_ROWS
assert Q_CHUNK_ROWS == K_CHUNK_ROWS and WINDOW_CHUNKS * K_CHUNK_ROWS >= Q_CHUNK_ROWS + WIN_ROWS - 1
assert N_Q_CHUNKS == 2 and N_K_CHUNKS == 4


def _key_row_start(rb, rows):
    return int(np.clip(rb * Q_ROWS - (K_ROWS - Q_ROWS) // 2, 0, rows - K_ROWS))


def _window_uses_last_chunk(rb, c, n_rb):
    if c == 0:
        return rb == n_rb - 1
    return rb != 0


def _window_key_rows(rb, c, rows):
    var = N_K_CHUNKS - 1 if _window_uses_last_chunk(rb, c, rows // Q_ROWS) else 0
    k0 = _key_row_start(rb, rows)
    return [k0 + K_CHUNK_ROWS * j + i for j in (var, 1, 2) for i in range(K_CHUNK_ROWS)]


def _row_window(qr, rows):
    kh = min(WIN_ROWS, rows)
    rs = int(np.clip(qr - kh // 2, 0, rows - kh))
    return rs, rs + kh


def _check_windows(rows):
    def relative(rb):
        base = rb * Q_ROWS
        return [([kr - base for kr in _window_key_rows(rb, c, rows)],
                 [tuple(r - base for r in _row_window(base + c * Q_CHUNK_ROWS + qi, rows))
                  for qi in range(Q_CHUNK_ROWS)]) for c in range(N_Q_CHUNKS)]

    n_rb = rows // Q_ROWS
    for rb in range(n_rb):
        assert rb in (0, n_rb - 1) or relative(rb) == relative(1), rb
        for c in range(N_Q_CHUNKS):
            have = set(_window_key_rows(rb, c, rows))
            for qi in range(Q_CHUNK_ROWS):
                lo, hi = _row_window(rb * Q_ROWS + c * Q_CHUNK_ROWS + qi, rows)
                assert set(range(lo, hi)) <= have, (rb, c, qi)


def _build_bias_tables(t_ref, bias_scr, rows):
    n_rb = rows // Q_ROWS
    left = lax.broadcasted_iota(jnp.int32, (GRID_W, LANES), 1) < GRID_W
    masked = jnp.full((GRID_W, LANES), MASK_VALUE, F32)
    for variant, rb in enumerate((0, 1, n_rb - 1)):
        for c in range(N_Q_CHUNKS):
            key_rows = _window_key_rows(rb, c, rows)
            for h in range(HEADS_PER_STEP):
                for qi in range(Q_CHUNK_ROWS):
                    qr = rb * Q_ROWS + c * Q_CHUNK_ROWS + qi
                    lo, hi = _row_window(qr, rows)
                    for p in range(len(key_rows) // 2):
                        pair = [t_ref[h, kr - qr + WIN_ROWS - 1] if lo <= kr < hi else None
                                for kr in key_rows[2 * p:2 * p + 2]]
                        if pair[0] is None and pair[1] is None:
                            block = masked
                        else:
                            block = jnp.where(left, masked if pair[0] is None else pair[0],
                                              masked if pair[1] is None else pair[1])
                        bias_scr[variant, c, h, qi * GRID_W:(qi + 1) * GRID_W, p * LANES:(p + 1) * LANES] = block


def _na_kernel(q_ref, kvar0, k1, k2, kvar1, vvar0, v1, v2, vvar1, kc_ref, vc_ref, t_ref, o_ref, bias_scr, s_scr,
               *, rows):
    n_rb = rows // Q_ROWS
    b, rb = pl.program_id(1), pl.program_id(2)

    @pl.when((b == 0) & (rb == 0))
    def _():
        _build_bias_tables(t_ref, bias_scr, rows)

    variant = jnp.where(rb == 0, 0, jnp.where(rb == n_rb - 1, 2, 1))
    tq = Q_CHUNK_ROWS * GRID_W
    tk = K_CHUNK_ROWS * GRID_W
    units = []
    for c, (kvar, vvar) in enumerate([(kvar0, vvar0), (kvar1, vvar1)]):
        window = [(kvar[0], vvar[0]), (k1[0], v1[0]), (k2[0], v2[0])]
        q = q_ref[0, c * tq:(c + 1) * tq, :]
        for h in range(HEADS_PER_STEP):
            lat = [(k, v, bias_scr[variant, c, h, :, j * tk:(j + 1) * tk]) for j, (k, v) in enumerate(window)]
            units.append((q, h, lat + [(kc_ref[0], vc_ref[0], None)], s_scr.at[c, h]))
    outs = _attention_units(units)
    for c in range(N_Q_CHUNKS):
        o = _merge_heads(outs[c * HEADS_PER_STEP:(c + 1) * HEADS_PER_STEP])
        o_ref[0, c * tq:(c + 1) * tq, :] = o.astype(BF16)


def _neighborhood_attention(q, k, v, kc, vc, toeplitz, layer):
    bsz, length, _ = q.shape
    rows = length // GRID_W
    n_rb = rows // Q_ROWS
    assert rows % Q_ROWS == 0 and rows >= K_ROWS + Q_ROWS and n_rb >= 3
    _check_windows(rows)
    n_ctx = kc.shape[1]
    tq = Q_ROWS * GRID_W
    tk = K_CHUNK_ROWS * GRID_W
    assert n_ctx == tk
    max_chunk = (rows - K_ROWS) // K_CHUNK_ROWS
    half = (K_ROWS - Q_ROWS) // 2 // K_CHUNK_ROWS

    def kv_spec(j):
        def index(hp, b, rb):
            start = jnp.clip(rb * (Q_ROWS // K_CHUNK_ROWS) - half, 0, max_chunk)
            if j == "var0":
                chunk = jnp.where(rb == n_rb - 1, N_K_CHUNKS - 1, 0)
            elif j == "var1":
                chunk = jnp.where(rb == 0, 0, N_K_CHUNKS - 1)
            else:
                chunk = j
            return (b, start + chunk, hp)
        return pl.BlockSpec((1, tk, LANES), index)

    kv_chunks = ["var0", 1, 2, "var1"]

    ctx_spec = pl.BlockSpec((1, n_ctx, LANES), lambda hp, b, rb: (b, 0, hp))
    q_spec = pl.BlockSpec((1, tq, LANES), lambda hp, b, rb: (b, rb, hp))
    window = WINDOW_CHUNKS * K_CHUNK_ROWS * GRID_W
    return pl.pallas_call(
        functools.partial(_na_kernel, rows=rows),
        grid=(NA_HEADS // HEADS_PER_STEP, bsz, n_rb),
        in_specs=([q_spec] + [kv_spec(j) for j in kv_chunks] + [kv_spec(j) for j in kv_chunks]
                  + [ctx_spec, ctx_spec,
                     pl.BlockSpec((HEADS_PER_STEP, DR_PAD, GRID_W, LANES),
                                  lambda hp, b, rb: (layer * (NA_HEADS // HEADS_PER_STEP) + hp, 0, 0, 0))]),
        out_specs=q_spec,
        out_shape=jax.ShapeDtypeStruct((bsz, length, NA_WIDTH), BF16),
        scratch_shapes=[pltpu.VMEM((3, N_Q_CHUNKS, HEADS_PER_STEP, Q_CHUNK_ROWS * GRID_W, window), F32),
                        pltpu.VMEM((N_Q_CHUNKS, HEADS_PER_STEP, Q_CHUNK_ROWS * GRID_W, window + n_ctx), F32)],
        compiler_params=_cparams(3),
        name="neighborhood_attention",
    )(q, *([k] * N_K_CHUNKS), *([v] * N_K_CHUNKS), kc, vc, toeplitz)


def _ctx_attn_kernel(q_ref, k_ref, v_ref, o_ref, s_scr):
    tiles = [(k_ref[0], v_ref[0], None)]
    outs = _attention_units([(q_ref[0], h, tiles, s_scr.at[h]) for h in range(HEADS_PER_STEP)])
    o_ref[0] = _merge_heads(outs).astype(BF16)


def _context_attention(qc, kc, vc):
    bsz, n_ctx, _ = qc.shape
    spec = pl.BlockSpec((1, n_ctx, LANES), lambda b, hp: (b, 0, hp))
    return pl.pallas_call(
        _ctx_attn_kernel,
        grid=(bsz, NA_HEADS // HEADS_PER_STEP),
        in_specs=[spec, spec, spec],
        out_specs=spec,
        out_shape=jax.ShapeDtypeStruct((bsz, n_ctx, NA_WIDTH), BF16),
        scratch_shapes=[pltpu.VMEM((HEADS_PER_STEP, n_ctx, n_ctx), F32)],
        compiler_params=_cparams(2),
        name="context_attention",
    )(qc, kc, vc)


def _dft_cos_sin(n):
    ang = 2.0 * np.pi * np.outer(np.arange(n), np.arange(n)) / n
    return np.cos(ang), np.sin(ang)


def _bf16_table(a):
    return jnp.asarray(a, F32).astype(BF16)


def _channel_dft_matrix(n_groups=F_GROUPS):
    c, s = _dft_cos_sin(F_GROUP_DIM)
    scale = F_GROUP_DIM ** -0.5
    eye = np.eye(n_groups)
    return np.concatenate([np.kron(eye, c), np.kron(eye, s)], axis=1) * scale


def _fft_stage1_kernel(f_ref, perm_ref, w1_ref, cs_ref, sc_ref, tc_ref, ts_ref, zr_ref, zi_ref, *, n_slow, nt):
    x = f_ref[0].reshape(n_slow * nt, F_WIDTH).astype(BF16)
    x = _dot(perm_ref[...], x).astype(BF16)
    ab = [_dot(x[:, p * LANES:(p + 1) * LANES], w1_ref[...]).astype(BF16) for p in range(F_WIDTH // LANES)]
    a_all = jnp.concatenate([blk[:, :LANES] for blk in ab], axis=-1)
    b_all = jnp.concatenate([blk[:, LANES:] for blk in ab], axis=-1)
    for t in range(nt):
        a = a_all[t * n_slow:(t + 1) * n_slow]
        b = b_all[t * n_slow:(t + 1) * n_slow]
        z = _dot(cs_ref[...], a) + _dot(sc_ref[...], b)
        zr, zi = z[:n_slow], z[n_slow:]
        c, s = tc_ref[t], ts_ref[t]
        zr_ref[0, t] = (zr * c - zi * s).astype(BF16)
        zi_ref[0, t] = (zr * s + zi * c).astype(BF16)


def _fft_stage2_kernel(zr_ref, zi_ref, f_ref, y_ref, *, n_fast, kb):
    rhs = jnp.concatenate([zr_ref[0].reshape(n_fast * kb, F_WIDTH),
                           zi_ref[0].reshape(n_fast * kb, F_WIDTH)], axis=0)
    y_ref[0] = _dot(f_ref[...], rhs).astype(BF16).reshape(n_fast, kb, F_WIDTH)


def _fourier_positions(f, n_slow, n_fast):
    bsz, n, _ = f.shape
    assert n == n_slow * n_fast
    nt = SUBLANES
    kb = 2 * SUBLANES
    w1 = _bf16_table(_channel_dft_matrix(LANES // F_GROUP_DIM))
    perm = _bf16_table(np.eye(n_slow * nt).reshape(n_slow, nt, n_slow * nt).transpose(1, 0, 2)
                       .reshape(n_slow * nt, n_slow * nt))
    c1, s1 = _dft_cos_sin(n_slow)
    sc1 = n_slow ** -0.5
    cs = _bf16_table(np.concatenate([c1, s1], axis=0) * sc1)
    sc = _bf16_table(np.concatenate([-s1, c1], axis=0) * sc1)
    tw = 2.0 * np.pi * np.outer(np.arange(n_fast), np.arange(n_slow)) / n
    tc = jnp.asarray(np.cos(tw)[:, :, None], F32)
    ts = jnp.asarray(np.sin(tw)[:, :, None], F32)
    z_shape = jax.ShapeDtypeStruct((bsz, n_fast, n_slow, F_WIDTH), BF16)
    z_spec = pl.BlockSpec((1, nt, n_slow, F_WIDTH), lambda b, j: (b, j, 0, 0))
    const2 = lambda b, j: (0, 0)
    tw_spec = pl.BlockSpec((nt, n_slow, 1), lambda b, j: (j, 0, 0))
    zr, zi = pl.pallas_call(
        functools.partial(_fft_stage1_kernel, n_slow=n_slow, nt=nt),
        grid=(bsz, n_fast // nt),
        in_specs=[pl.BlockSpec((1, n_slow, nt, F_WIDTH), lambda b, j: (b, 0, j, 0)),
                  pl.BlockSpec(perm.shape, const2),
                  pl.BlockSpec(w1.shape, const2), pl.BlockSpec(cs.shape, const2), pl.BlockSpec(sc.shape, const2),
                  tw_spec, tw_spec],
        out_specs=[z_spec, z_spec],
        out_shape=[z_shape, z_shape],
        compiler_params=_cparams(2),
        name="fnet_stage1",
    )(f.reshape(bsz, n_slow, n_fast, F_WIDTH), perm, w1, cs, sc, tc, ts)

    c2, s2 = _dft_cos_sin(n_fast)
    sc2 = n_fast ** -0.5
    eye = np.eye(kb)
    f2 = _bf16_table(np.concatenate([np.kron(c2, eye), np.kron(-s2, eye)], axis=1) * sc2)
    blk = pl.BlockSpec((1, n_fast, kb, F_WIDTH), lambda b, j: (b, 0, j, 0))
    y = pl.pallas_call(
        functools.partial(_fft_stage2_kernel, n_fast=n_fast, kb=kb),
        grid=(bsz, n_slow // kb),
        in_specs=[blk, blk, pl.BlockSpec(f2.shape, const2)],
        out_specs=blk,
        out_shape=jax.ShapeDtypeStruct((bsz, n_fast, n_slow, F_WIDTH), BF16),
        compiler_params=_cparams(2),
        name="fnet_stage2",
    )(zr, zi, f2)
    return y.reshape(bsz, n, F_WIDTH)


def _second_largest_sum(a, b, c, d):
    mab, nab = jnp.maximum(a, b), jnp.minimum(a, b)
    mcd, ncd = jnp.maximum(c, d), jnp.minimum(c, d)
    return jnp.maximum(mab, mcd) + jnp.maximum(jnp.minimum(mab, mcd), jnp.maximum(nab, ncd))


def _selected_group(sb_rows):
    epg = EXPERTS_PER_GROUP
    g_score = [_second_largest_sum(*sb_rows[g * epg:(g + 1) * epg]) for g in range(N_GROUPS)]
    best = functools.reduce(jnp.maximum, g_score)
    group = jnp.full_like(best, float(N_GROUPS - 1))
    for g in range(N_GROUPS - 2, -1, -1):
        group = jnp.where(g_score[g] == best, float(g), group)
    return group


def _top2_gates(cand_s, cand_sb):
    n = len(cand_s)
    w = []
    for j in range(n):
        rank = jnp.zeros_like(cand_sb[j])
        for i in range(n):
            if i == j:
                continue
            ahead = (cand_sb[i] > cand_sb[j]) | ((cand_sb[i] == cand_sb[j]) & (i < j))
            rank = rank + jnp.where(ahead, 1.0, 0.0)
        w.append(jnp.where(rank < 2.0, cand_s[j], 0.0))
    total = functools.reduce(jnp.add, w)
    return [wj / total for wj in w]


CHUNK_ROWS = 16


def _local_rows(tm):
    need = tm + N_GROUPS * (CHUNK_ROWS - 1) + CHUNK_ROWS
    return -(-need // CHUNK_ROWS) * CHUNK_ROWS


def _sort_matrix(lslot, lr):
    r_iota = lax.broadcasted_iota(jnp.int32, (lr, 1), 0).astype(F32)
    return jnp.where(r_iota == lslot, 1.0, 0.0).astype(BF16)


def _local_sort(group, h2b, tri_ref, lr):
    t = group.shape[1]
    g_iota = lax.broadcasted_iota(jnp.int32, (SUBLANES, 1), 0).astype(F32)
    onehot = g_iota == group
    prefix = _dot(jnp.where(onehot, 1.0, 0.0).astype(BF16), tri_ref[...])
    count = prefix[:, t - 1:t]
    padded = jnp.floor((count + (CHUNK_ROWS - 1.0)) * (1.0 / CHUNK_ROWS)) * CHUNK_ROWS
    lslot = jnp.zeros((1, t), F32)
    start = jnp.zeros((1, 1), F32)
    for g in range(N_GROUPS):
        lslot = jnp.where(onehot[g:g + 1], start + prefix[g:g + 1] - 1.0, lslot)
        start = start + padded[g:g + 1]
    return lslot, count, _dot(_sort_matrix(lslot, lr), h2b).astype(BF16)


def _out_kernel(a_ref, y_ref, x_ref, m_ref, wo_ref, bd_ref, g_ref, b_ref, wr_ref, rb_ref, tri_ref,
                x1_ref, h2_ref, lslot_ref, count_ref):
    y2 = _dot(y_ref[0].astype(BF16), bd_ref[...]).astype(BF16)
    o = _dot(a_ref[0], wo_ref[:NA_WIDTH, :]) + _dot(y2, wo_ref[NA_WIDTH:, :])
    z = DEEPNORM_ALPHA * x_ref[0] + m_ref[0, 2:3, :] * o
    x1 = _layer_norm(z) * g_ref[...] + b_ref[...]
    x1_ref[0] = x1
    h2 = (_layer_norm(x1) * (1.0 + m_ref[0, 4:5, :]) + m_ref[0, 3:4, :]).astype(BF16)
    sb = jax.nn.sigmoid(_dot_nt(wr_ref[...], h2)) + rb_ref[...]
    group = _selected_group([sb[e:e + 1] for e in range(N_EXPERTS)])
    lslot, count, h2_sorted = _local_sort(group, h2, tri_ref, h2_ref.shape[0])
    h2_ref[...] = h2_sorted
    lslot_ref[0] = jnp.broadcast_to(lslot, lslot_ref.shape[1:])
    count_ref[0] = jnp.broadcast_to(count, count_ref.shape[1:])


def _out_projection(attn, yf, x, m, w_out_bf16, w_four_bd, ln_g, ln_b, w_router_t, router_bias, tm):
    bsz, length, _ = x.shape
    tiles_per_batch = length // tm
    n_tiles = bsz * tiles_per_batch
    lr = _local_rows(tm)
    tri = _bf16_table(np.triu(np.ones((tm, tm))))
    row = lambda b, i: (b, i, 0)
    tile = lambda b, i: (b * tiles_per_batch + i, 0)
    const2 = lambda b, i: (0, 0)
    return pl.pallas_call(
        _out_kernel,
        grid=(bsz, tiles_per_batch),
        in_specs=[pl.BlockSpec((1, tm, NA_WIDTH), row), pl.BlockSpec((1, tm, F_WIDTH), row),
                  pl.BlockSpec((1, tm, D_MODEL), row),
                  pl.BlockSpec((1, N_MOD, D_MODEL), lambda b, i: (b, 0, 0)),
                  pl.BlockSpec((NA_WIDTH + F_WIDTH, D_MODEL), const2),
                  pl.BlockSpec((F_WIDTH, F_WIDTH), const2),
                  pl.BlockSpec((1, D_MODEL), const2), pl.BlockSpec((1, D_MODEL), const2),
                  pl.BlockSpec((N_EXPERTS, D_MODEL), const2), pl.BlockSpec((N_EXPERTS, 1), const2),
                  pl.BlockSpec((tm, tm), const2)],
        out_specs=[pl.BlockSpec((1, tm, D_MODEL), row),
                   pl.BlockSpec((lr, D_MODEL), tile),
                   pl.BlockSpec((1, SUBLANES, tm), lambda b, i: (b, 0, i)),
                   pl.BlockSpec((1, SUBLANES, LANES), lambda b, i: (b * tiles_per_batch + i, 0, 0))],
        out_shape=[jax.ShapeDtypeStruct((bsz, length, D_MODEL), F32),
                   jax.ShapeDtypeStruct((n_tiles * lr, D_MODEL), BF16),
                   jax.ShapeDtypeStruct((bsz, SUBLANES, length), F32),
                   jax.ShapeDtypeStruct((n_tiles, SUBLANES, LANES), F32)],
        compiler_params=_cparams(2),
        name="out_proj_norm_route",
    )(attn, yf, x, m, w_out_bf16, w_four_bd, ln_g.reshape(1, D_MODEL), ln_b.reshape(1, D_MODEL),
      w_router_t, router_bias.reshape(N_EXPERTS, 1), tri)


def _run_plan(count, tm, lr, tm_slots):
    n_tiles = count.shape[0]
    ch = CHUNK_ROWS
    n_slots = -(-(n_tiles * (tm + N_GROUPS * (ch - 1)) + N_GROUPS * tm_slots) // tm_slots) * tm_slots
    zero_chunk = lr // ch - 1
    pad = ((count + ch - 1) // ch) * ch
    lstart = jnp.cumsum(pad, axis=1) - pad
    run_off = jnp.cumsum(pad, axis=0) - pad
    seg_len = jnp.sum(pad, axis=0)
    seg_pad = ((seg_len + tm_slots - 1) // tm_slots) * tm_slots
    seg_end = jnp.cumsum(seg_pad)
    seg_start = seg_end - seg_pad

    s = (jnp.arange(n_slots // ch, dtype=jnp.int32) * ch)[:, None, None]
    in_run = (seg_start + run_off <= s) & (s < seg_start + run_off + pad)
    src_row = jnp.arange(n_tiles, dtype=jnp.int32)[None, :, None] * lr + lstart + (s - seg_start - run_off)
    src_row = jnp.sum(jnp.where(in_run, src_row, 0), axis=(1, 2))
    src_chunk = jnp.where(jnp.any(in_run, axis=(1, 2)), src_row // ch, zero_chunk).astype(jnp.int32)

    tile_start = jnp.arange(n_slots // tm_slots, dtype=jnp.int32) * tm_slots
    tile_group = jnp.minimum(jnp.sum((seg_end[None, :] <= tile_start[:, None]).astype(jnp.int32), axis=1),
                             N_GROUPS - 1)
    n_used = (seg_end[N_GROUPS - 1:] // tm_slots).astype(jnp.int32)

    r = (jnp.arange(lr // ch, dtype=jnp.int32) * ch)[None, :, None]
    in_local = (lstart[:, None, :] <= r) & (r < (lstart + pad)[:, None, :])
    slot = (seg_start + run_off - lstart)[:, None, :] + r
    back_chunk = (jnp.sum(jnp.where(in_local, slot, 0), axis=2) // ch).astype(jnp.int32)
    return src_chunk, tile_group.astype(jnp.int32), n_used, back_chunk.reshape(-1)


def _chunk_gather(idx_ref, tile, n_chunks, src_hbm, buf, sems, wait):
    slot = tile % 2
    rows = n_chunks * CHUNK_ROWS
    if wait:
        pltpu.make_async_copy(src_hbm.at[pl.ds(0, rows), :], buf.at[slot], sems.at[slot]).wait()
        return
    for c in range(n_chunks):
        src = pl.multiple_of(idx_ref[tile * n_chunks + c] * CHUNK_ROWS, CHUNK_ROWS)
        pltpu.make_async_copy(src_hbm.at[pl.ds(src, CHUNK_ROWS), :],
                              buf.at[slot, pl.ds(c * CHUNK_ROWS, CHUNK_ROWS), :], sems.at[slot]).start()


def _moe_runs_kernel(src_ref, tile_group_ref, n_used_ref, h_hbm, wg32_ref, wu32_ref, wd32_ref, wr_ref, rb_ref,
                     y_ref, hbuf, wg_ref, wu_ref, wd_ref, sems, *, tm):
    step = pl.program_id(0)
    n_used = n_used_ref[0]
    n_chunks = tm // CHUNK_ROWS
    new_group = (step == 0) | (tile_group_ref[step] != tile_group_ref[jnp.maximum(step - 1, 0)])

    @pl.when(new_group & (step < n_used))
    def _():
        wg_ref[...] = wg32_ref[...].astype(BF16)
        wu_ref[...] = wu32_ref[...].astype(BF16)
        wd_ref[...] = wd32_ref[...].astype(BF16)

    @pl.when((step == 0) & (n_used > 0))
    def _():
        _chunk_gather(src_ref, 0, n_chunks, h_hbm, hbuf, sems, wait=False)

    @pl.when(step + 1 < n_used)
    def _():
        _chunk_gather(src_ref, step + 1, n_chunks, h_hbm, hbuf, sems, wait=False)

    @pl.when(step < n_used)
    def _():
        _chunk_gather(src_ref, step, n_chunks, h_hbm, hbuf, sems, wait=True)
        h = hbuf[step % 2]
        s = jax.nn.sigmoid(_dot(h, wr_ref[0]))
        sb = s + rb_ref[0]
        epg = EXPERTS_PER_GROUP
        gates = _top2_gates([s[:, j:j + 1] for j in range(epg)], [sb[:, j:j + 1] for j in range(epg)])
        acc = None
        gate_up = [(_dot(h, wg_ref[0]), _dot(h, wu_ref[0]))]
        for e in range(epg):
            if e + 1 < epg:
                gate_up.append((_dot(h, wg_ref[e + 1]), _dot(h, wu_ref[e + 1])))
            gate, up = gate_up[e]
            hid = (gate * jax.nn.sigmoid(gate)) * up * gates[e]
            y = _dot(hid.astype(BF16), wd_ref[e])
            acc = y if acc is None else acc + y
        y_ref[...] = acc.astype(BF16)

    @pl.when(step >= n_used)
    def _():
        y_ref[...] = jnp.zeros_like(y_ref)


def _moe_runs(h_sorted, src_chunk, tile_group, n_used, layer, w_gate, w_up, w_down, w_router_grp,
              router_bias_grp, tm):
    n_slots = src_chunk.shape[0] * CHUNK_ROWS
    epg = EXPERTS_PER_GROUP
    by_group = lambda i, src_ref, tg_ref, nu_ref: (tg_ref[i], 0, 0)
    by_layer_group = lambda i, src_ref, tg_ref, nu_ref: (layer * N_GROUPS + tg_ref[i], 0, 0)
    grid_spec = pltpu.PrefetchScalarGridSpec(
        num_scalar_prefetch=3,
        grid=(n_slots // tm,),
        in_specs=[pl.BlockSpec(memory_space=pl.ANY),
                  pl.BlockSpec((epg, D_MODEL, D_EXPERT), by_layer_group),
                  pl.BlockSpec((epg, D_MODEL, D_EXPERT), by_layer_group),
                  pl.BlockSpec((epg, D_EXPERT, D_MODEL), by_layer_group),
                  pl.BlockSpec((1, D_MODEL, LANES), by_group),
                  pl.BlockSpec((1, 1, LANES), by_group)],
        out_specs=pl.BlockSpec((tm, D_MODEL), lambda i, src_ref, tg_ref, nu_ref: (i, 0)),
        scratch_shapes=[pltpu.VMEM((2, tm, D_MODEL), BF16),
                        pltpu.VMEM((epg, D_MODEL, D_EXPERT), BF16), pltpu.VMEM((epg, D_MODEL, D_EXPERT), BF16),
                        pltpu.VMEM((epg, D_EXPERT, D_MODEL), BF16), pltpu.SemaphoreType.DMA((2,))],
    )
    return pl.pallas_call(
        functools.partial(_moe_runs_kernel, tm=tm),
        grid_spec=grid_spec,
        out_shape=jax.ShapeDtypeStruct((n_slots, D_MODEL), BF16),
        compiler_params=_cparams(1),
        name="moe_group_experts",
    )(src_chunk, tile_group, n_used, h_sorted, w_gate, w_up, w_down, w_router_grp, router_bias_grp)


def _gather_unsort(back_ref, lslot_ref, y_hbm, ybuf, sems, step, n_steps, lr):
    n_chunks = lr // CHUNK_ROWS

    @pl.when(step == 0)
    def _():
        _chunk_gather(back_ref, 0, n_chunks, y_hbm, ybuf, sems, wait=False)

    @pl.when(step + 1 < n_steps)
    def _():
        _chunk_gather(back_ref, step + 1, n_chunks, y_hbm, ybuf, sems, wait=False)

    _chunk_gather(back_ref, step, n_chunks, y_hbm, ybuf, sems, wait=True)
    sort = _sort_matrix(lslot_ref[0, 0:1, :], lr)
    return lax.dot_general(sort, ybuf[step % 2], (((0,), (0,)), ((), ())), preferred_element_type=F32)


def _residual_norm_kernel(back_ref, y_hbm, lslot_ref, x1_ref, m_ref, g_ref, b_ref, o_ref, ybuf, sems, *, n_steps, lr):
    y = _gather_unsort(back_ref, lslot_ref, y_hbm, ybuf, sems, pl.program_id(0), n_steps, lr)
    z = DEEPNORM_ALPHA * x1_ref[...] + m_ref[0, 5:6, :] * y
    o_ref[...] = _layer_norm(z) * g_ref[...] + b_ref[...]


def _residual_norm(y_slots, back_chunk, lslot, x1, m, ln_g, ln_b, tm):
    bsz, length, _ = x1.shape
    n = bsz * length
    tiles_per_batch = length // tm
    lr = _local_rows(tm)
    const2 = lambda i, back_ref: (0, 0)
    row = pl.BlockSpec((tm, D_MODEL), lambda i, back_ref: (i, 0))
    grid_spec = pltpu.PrefetchScalarGridSpec(
        num_scalar_prefetch=1,
        grid=(n // tm,),
        in_specs=[pl.BlockSpec(memory_space=pl.ANY),
                  pl.BlockSpec((1, SUBLANES, tm),
                               lambda i, back_ref: (i // tiles_per_batch, 0, i % tiles_per_batch)),
                  row, pl.BlockSpec((1, N_MOD, D_MODEL), lambda i, back_ref: (i // tiles_per_batch, 0, 0)),
                  pl.BlockSpec((1, D_MODEL), const2), pl.BlockSpec((1, D_MODEL), const2)],
        out_specs=row,
        scratch_shapes=[pltpu.VMEM((2, lr, D_MODEL), BF16), pltpu.SemaphoreType.DMA((2,))],
    )
    out = pl.pallas_call(
        functools.partial(_residual_norm_kernel, n_steps=n // tm, lr=lr),
        grid_spec=grid_spec,
        out_shape=jax.ShapeDtypeStruct((n, D_MODEL), F32),
        compiler_params=_cparams(1),
        name="moe_residual_norm",
    )(back_chunk, y_slots, lslot, x1.reshape(n, D_MODEL), m, ln_g.reshape(1, D_MODEL), ln_b.reshape(1, D_MODEL))
    return out.reshape(bsz, length, D_MODEL)


def _grouped_moe_runs(h_sorted, count_rows, layer, w_gate, w_up, w_down, w_router_grp, router_bias_grp,
                      tm_tokens, tm_slots):
    count = count_rows[:, :N_GROUPS, 0].astype(jnp.int32)
    src_chunk, tile_group, n_used, back_chunk = _run_plan(count, tm_tokens, _local_rows(tm_tokens), tm_slots)
    y_slots = _moe_runs(h_sorted, src_chunk, tile_group, n_used, layer, w_gate, w_up, w_down, w_router_grp,
                        router_bias_grp, tm_slots)
    return y_slots, back_chunk


def _ctx_fourier_kernel(f_ref, w1_ref, c_ref, s_ref, y_ref):
    ab = _dot(f_ref[0].astype(BF16), w1_ref[...]).astype(BF16)
    y = _dot(c_ref[...], ab[:, :F_WIDTH]) + _dot(s_ref[...], ab[:, F_WIDTH:])
    y_ref[0] = y.astype(BF16)


def _context_fourier(fc):
    bsz, n, _ = fc.shape
    w1 = _bf16_table(_channel_dft_matrix())
    c, s = _dft_cos_sin(n)
    cm = _bf16_table(c * n ** -0.5)
    sm = _bf16_table(-s * n ** -0.5)
    const2 = lambda b: (0, 0)
    blk = pl.BlockSpec((1, n, F_WIDTH), lambda b: (b, 0, 0))
    return pl.pallas_call(
        _ctx_fourier_kernel,
        grid=(bsz,),
        in_specs=[blk, pl.BlockSpec(w1.shape, const2), pl.BlockSpec(cm.shape, const2),
                  pl.BlockSpec(sm.shape, const2)],
        out_specs=blk,
        out_shape=jax.ShapeDtypeStruct((bsz, n, F_WIDTH), BF16),
        compiler_params=_cparams(1),
        name="context_fnet",
    )(fc, w1, cm, sm)


def _block_diag(w):
    g, c, _ = w.shape
    eye = jnp.eye(g, dtype=w.dtype)
    return (eye[:, None, :, None] * w[:, :, None, :]).reshape(g * c, g * c)


def kernel(x, c, ctx, c_ctx, w_mod, b_mod, w_in, rpb, w_four, w_out, ln1_g, ln1_b, ln2_g, ln2_b,
           w_router, router_bias, w_gate, w_up, w_down):
    bsz, length, _ = x.shape
    n_ctx = ctx.shape[1]
    rows = length // GRID_W

    cvec = jnp.concatenate([c, c_ctx[None, :], jnp.zeros((8 - bsz - 1, D_MODEL), F32)], axis=0)
    mods = _modulation(cvec, w_mod, b_mod)
    w_router_t = w_router.T.astype(BF16)
    toeplitz = _rpb_toeplitz(rpb)
    lane_pad = LANES - EXPERTS_PER_GROUP
    w_router_grp = jnp.pad(w_router.reshape(D_MODEL, N_GROUPS, EXPERTS_PER_GROUP).transpose(1, 0, 2),
                           ((0, 0), (0, 0), (0, lane_pad))).astype(BF16)
    router_bias_grp = jnp.pad(router_bias.astype(F32).reshape(N_GROUPS, 1, EXPERTS_PER_GROUP),
                              ((0, 0), (0, 0), (0, lane_pad)))

    wg = w_gate.reshape(DEPTH * N_EXPERTS, D_MODEL, D_EXPERT)
    wu = w_up.reshape(DEPTH * N_EXPERTS, D_MODEL, D_EXPERT)
    wd = w_down.reshape(DEPTH * N_EXPERTS, D_EXPERT, D_MODEL)

    def latent_mod(i):
        return mods[i, :bsz].reshape(bsz, N_MOD, D_MODEL)

    xc = ctx
    projected = None
    for i in range(DEPTH):
        last = i == DEPTH - 1
        m = latent_mod(i)
        mc = jnp.broadcast_to(mods[i, bsz].reshape(1, N_MOD, D_MODEL), (bsz, N_MOD, D_MODEL))
        w_in_b = w_in[i].astype(BF16)
        w_out_b = w_out[i].astype(BF16)
        w_four_bd = _block_diag(w_four[i]).astype(BF16)

        q, k, v, f = projected if projected is not None else _in_projection(x, m, w_in_b, tm=1024)
        qc, kc, vc, fc = _in_projection(xc, mc, w_in_b, tm=n_ctx)

        attn = _neighborhood_attention(q, k, v, kc, vc, toeplitz, i)
        yf = _fourier_positions(f, n_slow=rows, n_fast=GRID_W)
        x1, h2, lslot, cnt = _out_projection(attn, yf, x, m, w_out_b, w_four_bd, ln1_g[i], ln1_b[i],
                                             w_router_t, router_bias, tm=512)
        y, back = _grouped_moe_runs(h2, cnt, i, wg, wu, wd, w_router_grp, router_bias_grp,
                                    tm_tokens=512, tm_slots=512)
        if last:
            return _residual_norm(y, back, lslot, x1, m, ln2_g[i], ln2_b[i], tm=512)
        x, *projected = _norm_in_projection(y, back, lslot, x1, m, ln2_g[i], ln2_b[i], latent_mod(i + 1),
                                            w_in[i + 1].astype(BF16), tm=512)

        attn_c = _context_attention(qc, kc, vc)
        yc = _context_fourier(fc)
        xc1, h2c, lslot_c, cnt_c = _out_projection(attn_c, yc, xc, mc, w_out_b, w_four_bd, ln1_g[i], ln1_b[i],
                                                   w_router_t, router_bias, tm=n_ctx)
        yc2, back_c = _grouped_moe_runs(h2c, cnt_c, i, wg, wu, wd, w_router_grp, router_bias_grp,
                                        tm_tokens=n_ctx, tm_slots=128)
        xc = _residual_norm(yc2, back_c, lslot_c, xc1, mc, ln2_g[i], ln2_b[i], tm=n_ctx)
    return x
```

```python
import functools
import math

import numpy as np
import jax
import jax.numpy as jnp
from jax import lax
from jax.experimental import pallas as pl
from jax.experimental.pallas import tpu as pltpu

D_MODEL = 1024
DEPTH = 2
GRID_W = 64
NA_HEADS = 8
HEAD_DIM = 64
NA_WIDTH = NA_HEADS * HEAD_DIM
WIN_ROWS = 8
WIN_COLS = 16
F_GROUPS = 8
F_GROUP_DIM = 64
F_WIDTH = F_GROUPS * F_GROUP_DIM
IN_WIDTH = 3 * NA_WIDTH + F_WIDTH
N_EXPERTS = 16
N_GROUPS = 4
EXPERTS_PER_GROUP = N_EXPERTS // N_GROUPS
D_EXPERT = 256
N_MOD = 6
DEEPNORM_ALPHA = (2.0 * DEPTH) ** 0.25
LN_EPS = 1e-6

F32 = jnp.float32
BF16 = jnp.bfloat16

V7X_VMEM_BYTES = 64 * 1024 * 1024
VMEM_LIMIT_BYTES = (V7X_VMEM_BYTES * 3) // 4
LANES = 128
SUBLANES = 8
HEADS_PER_STEP = LANES // HEAD_DIM
assert HEADS_PER_STEP == 2
MASK_VALUE = -1e30
LOG2_E = math.log2(math.e)

CHUNK_GRID_ROWS = 4
WINDOW_CHUNKS = 3
STEP_CHUNKS = 4


def _cparams(n_grid_dims):
    return pltpu.CompilerParams(dimension_semantics=("arbitrary",) * n_grid_dims,
                                vmem_limit_bytes=VMEM_LIMIT_BYTES)


def _layer_norm(x):
    mu = jnp.mean(x, axis=-1, keepdims=True)
    xc = x - mu
    var = jnp.mean(xc * xc, axis=-1, keepdims=True)
    return xc * lax.rsqrt(var + LN_EPS)


def _dot(a, b):
    return jnp.dot(a, b, preferred_element_type=F32)


def _dot_nt(a, b):
    return lax.dot_general(a, b, (((1,), (1,)), ((), ())), preferred_element_type=F32)


def _mod_kernel(c_ref, w_ref, b_ref, o_ref):
    c = c_ref[...]
    a = c * jax.nn.sigmoid(c)
    o_ref[0] = jnp.dot(a, w_ref[0], preferred_element_type=F32, precision=lax.Precision.HIGHEST) + b_ref[0]


def _modulation(cvec, w_mod, b_mod):
    n_col_blocks = 4
    wc = (N_MOD * D_MODEL) // n_col_blocks
    rows = cvec.shape[0]
    return pl.pallas_call(
        _mod_kernel,
        grid=(DEPTH, n_col_blocks),
        in_specs=[pl.BlockSpec((rows, D_MODEL), lambda i, j: (0, 0)),
                  pl.BlockSpec((1, D_MODEL, wc), lambda i, j: (i, 0, j)),
                  pl.BlockSpec((1, 1, wc), lambda i, j: (i, 0, j))],
        out_specs=pl.BlockSpec((1, rows, wc), lambda i, j: (i, 0, j)),
        out_shape=jax.ShapeDtypeStruct((DEPTH, rows, N_MOD * D_MODEL), F32),
        compiler_params=_cparams(2),
        name="modulation",
    )(cvec, w_mod, b_mod.reshape(DEPTH, 1, N_MOD * D_MODEL))


def _modulate(x, m_ref):
    return (_layer_norm(x) * (1.0 + m_ref[0, 1:2, :]) + m_ref[0, 0:1, :]).astype(BF16)


def _project(h, rows, w_ref, q_ref, k_ref, v_ref, f_ref):
    p = _dot(h, w_ref[...])
    q_ref[0, rows, :] = (p[:, :NA_WIDTH] * (HEAD_DIM ** -0.5 * LOG2_E)).astype(BF16)
    k_ref[0, rows, :] = p[:, NA_WIDTH:2 * NA_WIDTH].astype(BF16)
    v_ref[0, rows, :] = p[:, 2 * NA_WIDTH:3 * NA_WIDTH].astype(BF16)
    f_ref[0, rows, :] = p[:, 3 * NA_WIDTH:]


def _proj_kernel(x_ref, m_ref, w_ref, q_ref, k_ref, v_ref, f_ref):
    tm = x_ref.shape[1]
    halves = [slice(0, tm // 2), slice(tm // 2, tm)] if tm % 512 == 0 else [slice(0, tm)]
    hs = [_modulate(x_ref[0, rows, :], m_ref) for rows in halves]
    for rows, h in zip(halves, hs):
        _project(h, rows, w_ref, q_ref, k_ref, v_ref, f_ref)


def _norm_proj_kernel(back_ref, y_hbm, lslot_ref, x1_ref, m_ref, g_ref, b_ref, m_next_ref, w_ref,
                      x_ref, q_ref, k_ref, v_ref, f_ref, ybuf, sems, *, tiles_per_batch, n_steps, lr):
    step = pl.program_id(0) * tiles_per_batch + pl.program_id(1)
    y = _gather_unsort(back_ref, lslot_ref, y_hbm, ybuf, sems, step, n_steps, lr)
    z = DEEPNORM_ALPHA * x1_ref[0] + m_ref[0, 5:6, :] * y
    x = _layer_norm(z) * g_ref[...] + b_ref[...]
    x_ref[0] = x
    _project(_modulate(x, m_next_ref), slice(None), w_ref, q_ref, k_ref, v_ref, f_ref)


def _norm_in_projection(y_slots, back_chunk, lslot, x1, m, ln_g, ln_b, m_next, w_in_bf16, tm):
    bsz, length, _ = x1.shape
    tiles_per_batch = length // tm
    lr = _local_rows(tm)
    out = jax.ShapeDtypeStruct((bsz, length, NA_WIDTH), BF16)
    out_f = jax.ShapeDtypeStruct((bsz, length, F_WIDTH), F32)
    out_x = jax.ShapeDtypeStruct((bsz, length, D_MODEL), F32)
    row = lambda b, i, back_ref: (b, i, 0)
    const2 = lambda b, i, back_ref: (0, 0)
    mod_spec = pl.BlockSpec((1, N_MOD, D_MODEL), lambda b, i, back_ref: (b, 0, 0))
    o_spec = pl.BlockSpec((1, tm, NA_WIDTH), row)
    x_spec = pl.BlockSpec((1, tm, D_MODEL), row)
    grid_spec = pltpu.PrefetchScalarGridSpec(
        num_scalar_prefetch=1,
        grid=(bsz, tiles_per_batch),
        in_specs=[pl.BlockSpec(memory_space=pl.ANY),
                  pl.BlockSpec((1, SUBLANES, tm), lambda b, i, back_ref: (b, 0, i)),
                  x_spec, mod_spec,
                  pl.BlockSpec((1, D_MODEL), const2), pl.BlockSpec((1, D_MODEL), const2),
                  mod_spec, pl.BlockSpec((D_MODEL, IN_WIDTH), const2)],
        out_specs=[x_spec, o_spec, o_spec, o_spec, o_spec],
        scratch_shapes=[pltpu.VMEM((2, lr, D_MODEL), BF16), pltpu.SemaphoreType.DMA((2,))],
    )
    return pl.pallas_call(
        functools.partial(_norm_proj_kernel, tiles_per_batch=tiles_per_batch, n_steps=bsz * tiles_per_batch,
                          lr=lr),
        grid_spec=grid_spec,
        out_shape=[out_x, out, out, out, out_f],
        compiler_params=_cparams(2),
        name="moe_norm_in_proj",
    )(back_chunk, y_slots, lslot, x1, m, ln_g.reshape(1, D_MODEL), ln_b.reshape(1, D_MODEL), m_next, w_in_bf16)


def _in_projection(x, m, w_in_bf16, tm):
    bsz, length, _ = x.shape
    out = jax.ShapeDtypeStruct((bsz, length, NA_WIDTH), BF16)
    out_f = jax.ShapeDtypeStruct((bsz, length, F_WIDTH), F32)
    o_spec = pl.BlockSpec((1, tm, NA_WIDTH), lambda b, i: (b, i, 0))
    return pl.pallas_call(
        _proj_kernel,
        grid=(bsz, length // tm),
        in_specs=[pl.BlockSpec((1, tm, D_MODEL), lambda b, i: (b, i, 0)),
                  pl.BlockSpec((1, N_MOD, D_MODEL), lambda b, i: (b, 0, 0)),
                  pl.BlockSpec((D_MODEL, IN_WIDTH), lambda b, i: (0, 0))],
        out_specs=[o_spec, o_spec, o_spec, o_spec],
        out_shape=[out, out, out, out_f],
        compiler_params=_cparams(2),
        name="ln_mod_in_proj",
    )(x, m, w_in_bf16)


def _head_lanes(h):
    lane = lax.broadcasted_iota(jnp.int32, (1, LANES), 1)
    return (lane >= HEAD_DIM * h) & (lane < HEAD_DIM * (h + 1))


def _scores_pass(q, h, tiles, s_ref):
    qh = jnp.where(_head_lanes(h), q, jnp.zeros_like(q))
    m = None
    t = tiles[0][0].shape[0]
    for j, (k, _, bias) in enumerate(tiles):
        s = _dot_nt(qh, k)
        if bias is not None:
            s = s + bias
        s_ref[:, j * t:(j + 1) * t] = s
        mj = jnp.max(s, axis=-1, keepdims=True)
        m = mj if m is None else jnp.maximum(m, mj)
    return m


def _pv_pass(h, tiles, s_ref, m):
    o = None
    t = tiles[0][0].shape[0]
    in_head = _head_lanes(h)
    for j, (_, v, _) in enumerate(tiles):
        p = jnp.exp2(s_ref[:, j * t:(j + 1) * t] - m)
        oj = _dot(p.astype(BF16), jnp.where(in_head, v, jnp.ones_like(v)))
        o = oj if o is None else o + oj
    return o / pltpu.roll(o, HEAD_DIM, axis=1)


def _attention_units(units):
    outs = []
    maxima = [_scores_pass(*units[0])]
    for u in range(len(units)):
        if u + 1 < len(units):
            maxima.append(_scores_pass(*units[u + 1]))
        _, h, tiles, s_ref = units[u]
        outs.append(_pv_pass(h, tiles, s_ref, maxima[u]))
    return outs


def _merge_heads(outs):
    merged = outs[0]
    for h in range(1, len(outs)):
        merged = jnp.where(_head_lanes(h), outs[h], merged)
    return merged


DR_PAD = 2 * WIN_ROWS
DC_PAD = 2 * WIN_COLS


def _toeplitz_kernel(r_ref, sel_ref, mask_ref, o_ref):
    n = r_ref.shape[0]
    for qc in range(GRID_W):
        block = jnp.dot(r_ref[...], sel_ref[qc], preferred_element_type=F32, precision=lax.Precision.HIGHEST)
        o_ref[pl.ds(qc, n, stride=GRID_W), :] = block * LOG2_E + mask_ref[qc]


def _rpb_toeplitz(rpb):
    depth, heads, n_dr, n_dc = rpb.shape
    qc = np.arange(GRID_W)[:, None]
    kc = (np.arange(LANES) % GRID_W)[None, :]
    cs = np.clip(qc - WIN_COLS // 2, 0, GRID_W - WIN_COLS)
    col_valid = (kc >= cs) & (kc < cs + WIN_COLS)
    dc = kc - qc + WIN_COLS - 1
    select = (np.arange(DC_PAD)[None, :, None] == dc[:, None, :]) & col_valid[:, None, :]
    mask = np.where(col_valid, 0.0, MASK_VALUE)[:, None, :]
    r = jnp.pad(rpb.astype(F32), ((0, 0), (0, 0), (0, DR_PAD - n_dr), (0, DC_PAD - n_dc)))
    n = depth * heads * DR_PAD
    const3 = lambda: (0, 0, 0)
    out = pl.pallas_call(
        _toeplitz_kernel,
        grid=(),
        in_specs=[pl.BlockSpec((n, DC_PAD), lambda: (0, 0)),
                  pl.BlockSpec((GRID_W, DC_PAD, LANES), const3), pl.BlockSpec((GRID_W, 1, LANES), const3)],
        out_specs=pl.BlockSpec((n * GRID_W, LANES), lambda: (0, 0)),
        out_shape=jax.ShapeDtypeStruct((n * GRID_W, LANES), F32),
        compiler_params=pltpu.CompilerParams(vmem_limit_bytes=VMEM_LIMIT_BYTES),
        name="rpb_toeplitz",
    )(r.reshape(n, DC_PAD), jnp.asarray(select, F32), jnp.asarray(mask, F32))
    return out.reshape(depth * heads, DR_PAD, GRID_W, LANES)


assert WINDOW_CHUNKS * CHUNK_GRID_ROWS >= CHUNK_GRID_ROWS + WIN_ROWS - 1


def _window_start_chunk(cr, n_cr):
    lo = cr - (WINDOW_CHUNKS - 1) // 2
    if isinstance(cr, int):
        return max(0, min(lo, n_cr - WINDOW_CHUNKS))
    return jnp.clip(lo, 0, n_cr - WINDOW_CHUNKS)


def _window_key_rows(cr, n_cr):
    start = _window_start_chunk(cr, n_cr) * CHUNK_GRID_ROWS
    return list(range(start, start + WINDOW_CHUNKS * CHUNK_GRID_ROWS))


def _row_window(qr, rows):
    kh = min(WIN_ROWS, rows)
    rs = int(np.clip(qr - kh // 2, 0, rows - kh))
    return rs, rs + kh


def _window_variant(cr, n_cr):
    return jnp.where(cr == 0, 0, jnp.where(cr == n_cr - 1, 2, 1))


def _check_windows(rows):
    def relative(cr):
        base = cr * CHUNK_GRID_ROWS
        return ([kr - base for kr in _window_key_rows(cr, n_cr)],
                [tuple(r - base for r in _row_window(base + qi, rows)) for qi in range(CHUNK_GRID_ROWS)])

    n_cr = rows // CHUNK_GRID_ROWS
    for cr in range(n_cr):
        assert cr in (0, n_cr - 1) or relative(cr) == relative(1), cr
        have = set(_window_key_rows(cr, n_cr))
        for qi in range(CHUNK_GRID_ROWS):
            lo, hi = _row_window(cr * CHUNK_GRID_ROWS + qi, rows)
            assert set(range(lo, hi)) <= have, (cr, qi)


def _build_bias_tables(t_ref, bias_scr, rows):
    n_cr = rows // CHUNK_GRID_ROWS
    left = lax.broadcasted_iota(jnp.int32, (GRID_W, LANES), 1) < GRID_W
    masked = jnp.full((GRID_W, LANES), MASK_VALUE, F32)
    for variant, cr in enumerate((0, 1, n_cr - 1)):
        key_rows = _window_key_rows(cr, n_cr)
        for h in range(HEADS_PER_STEP):
            for qi in range(CHUNK_GRID_ROWS):
                qr = cr * CHUNK_GRID_ROWS + qi
                lo, hi = _row_window(qr, rows)
                for p in range(len(key_rows) // 2):
                    pair = [t_ref[h, kr - qr + WIN_ROWS - 1] if lo <= kr < hi else None
                            for kr in key_rows[2 * p:2 * p + 2]]
                    if pair[0] is None and pair[1] is None:
                        block = masked
                    else:
                        block = jnp.where(left, masked if pair[0] is None else pair[0],
                                          masked if pair[1] is None else pair[1])
                    bias_scr[variant, h, qi * GRID_W:(qi + 1) * GRID_W, p * LANES:(p + 1) * LANES] = block


def _na_kernel(q_ref, *refs, rows):
    n_kv = STEP_CHUNKS * WINDOW_CHUNKS
    k_refs, v_refs = refs[:n_kv], refs[n_kv:2 * n_kv]
    kc_ref, vc_ref, t_ref, o_ref, bias_scr, s_scr = refs[2 * n_kv:]
    n_cr = rows // CHUNK_GRID_ROWS
    b, rb = pl.program_id(1), pl.program_id(2)

    @pl.when((b == 0) & (rb == 0))
    def _():
        _build_bias_tables(t_ref, bias_scr, rows)

    tq = CHUNK_GRID_ROWS * GRID_W
    units = []
    for c in range(STEP_CHUNKS):
        variant = _window_variant(rb * STEP_CHUNKS + c, n_cr)
        q = q_ref[0, c * tq:(c + 1) * tq, :]
        for h in range(HEADS_PER_STEP):
            lat = [(k_refs[c * WINDOW_CHUNKS + j][0], v_refs[c * WINDOW_CHUNKS + j][0],
                    bias_scr[variant, h, :, j * tq:(j + 1) * tq]) for j in range(WINDOW_CHUNKS)]
            units.append((q, h, lat + [(kc_ref[0], vc_ref[0], None)], s_scr.at[c, h]))
    outs = _attention_units(units)
    for c in range(STEP_CHUNKS):
        o = _merge_heads(outs[c * HEADS_PER_STEP:(c + 1) * HEADS_PER_STEP])
        o_ref[0, c * tq:(c + 1) * tq, :] = o.astype(BF16)


def _neighborhood_attention(q, k, v, kc, vc, toeplitz, layer):
    bsz, length, _ = q.shape
    rows = length // GRID_W
    n_cr = rows // CHUNK_GRID_ROWS
    assert rows % (CHUNK_GRID_ROWS * STEP_CHUNKS) == 0 and n_cr >= WINDOW_CHUNKS + 2
    _check_windows(rows)
    n_ctx = kc.shape[1]
    tq = CHUNK_GRID_ROWS * GRID_W
    assert n_ctx == tq

    def kv_spec(c, j):
        return pl.BlockSpec((1, tq, LANES),
                            lambda hp, b, rb: (b, _window_start_chunk(rb * STEP_CHUNKS + c, n_cr) + j, hp))

    kv_specs = [kv_spec(c, j) for c in range(STEP_CHUNKS) for j in range(WINDOW_CHUNKS)]
    ctx_spec = pl.BlockSpec((1, n_ctx, LANES), lambda hp, b, rb: (b, 0, hp))
    q_spec = pl.BlockSpec((1, STEP_CHUNKS * tq, LANES), lambda hp, b, rb: (b, rb, hp))
    window = WINDOW_CHUNKS * tq
    return pl.pallas_call(
        functools.partial(_na_kernel, rows=rows),
        grid=(NA_HEADS // HEADS_PER_STEP, bsz, n_cr // STEP_CHUNKS),
        in_specs=([q_spec] + kv_specs + kv_specs
                  + [ctx_spec, ctx_spec,
                     pl.BlockSpec((HEADS_PER_STEP, DR_PAD, GRID_W, LANES),
                                  lambda hp, b, rb: (layer * (NA_HEADS // HEADS_PER_STEP) + hp, 0, 0, 0))]),
        out_specs=q_spec,
        out_shape=jax.ShapeDtypeStruct((bsz, length, NA_WIDTH), BF16),
        scratch_shapes=[pltpu.VMEM((3, HEADS_PER_STEP, tq, window), F32),
                        pltpu.VMEM((STEP_CHUNKS, HEADS_PER_STEP, tq, window + n_ctx), F32)],
        compiler_params=_cparams(3),
        name="neighborhood_attention",
    )(q, *([k] * len(kv_specs)), *([v] * len(kv_specs)), kc, vc, toeplitz)


def _ctx_attn_kernel(q_ref, k_ref, v_ref, o_ref, s_scr):
    tiles = [(k_ref[0], v_ref[0], None)]
    outs = _attention_units([(q_ref[0], h, tiles, s_scr.at[h]) for h in range(HEADS_PER_STEP)])
    o_ref[0] = _merge_heads(outs).astype(BF16)


def _context_attention(qc, kc, vc):
    bsz, n_ctx, _ = qc.shape
    spec = pl.BlockSpec((1, n_ctx, LANES), lambda b, hp: (b, 0, hp))
    return pl.pallas_call(
        _ctx_attn_kernel,
        grid=(bsz, NA_HEADS // HEADS_PER_STEP),
        in_specs=[spec, spec, spec],
        out_specs=spec,
        out_shape=jax.ShapeDtypeStruct((bsz, n_ctx, NA_WIDTH), BF16),
        scratch_shapes=[pltpu.VMEM((HEADS_PER_STEP, n_ctx, n_ctx), F32)],
        compiler_params=_cparams(2),
        name="context_attention",
    )(qc, kc, vc)


def _dft_cos_sin(n):
    ang = 2.0 * np.pi * np.outer(np.arange(n), np.arange(n)) / n
    return np.cos(ang), np.sin(ang)


def _bf16_table(a):
    return jnp.asarray(a, F32).astype(BF16)


def _channel_dft_matrix(n_groups=F_GROUPS):
    c, s = _dft_cos_sin(F_GROUP_DIM)
    scale = F_GROUP_DIM ** -0.5
    eye = np.eye(n_groups)
    return np.concatenate([np.kron(eye, c), np.kron(eye, s)], axis=1) * scale


def _fft_stage1_kernel(f_ref, perm_ref, w1_ref, cs_ref, sc_ref, tc_ref, ts_ref, zr_ref, zi_ref, *, n_slow, nt):
    x = f_ref[0].reshape(n_slow * nt, F_WIDTH).astype(BF16)
    x = _dot(perm_ref[...], x).astype(BF16)
    ab = [_dot(x[:, p * LANES:(p + 1) * LANES], w1_ref[...]).astype(BF16) for p in range(F_WIDTH // LANES)]
    a_all = jnp.concatenate([blk[:, :LANES] for blk in ab], axis=-1)
    b_all = jnp.concatenate([blk[:, LANES:] for blk in ab], axis=-1)
    for t in range(nt):
        a = a_all[t * n_slow:(t + 1) * n_slow]
        b = b_all[t * n_slow:(t + 1) * n_slow]
        z = _dot(cs_ref[...], a) + _dot(sc_ref[...], b)
        zr, zi = z[:n_slow], z[n_slow:]
        c, s = tc_ref[t], ts_ref[t]
        zr_ref[0, t] = (zr * c - zi * s).astype(BF16)
        zi_ref[0, t] = (zr * s + zi * c).astype(BF16)


def _fft_stage2_kernel(zr_ref, zi_ref, f_ref, y_ref, *, n_fast, kb):
    rhs = jnp.concatenate([zr_ref[0].reshape(n_fast * kb, F_WIDTH),
                           zi_ref[0].reshape(n_fast * kb, F_WIDTH)], axis=0)
    y_ref[0] = _dot(f_ref[...], rhs).astype(BF16).reshape(n_fast, kb, F_WIDTH)


def _fourier_positions(f, n_slow, n_fast):
    bsz, n, _ = f.shape
    assert n == n_slow * n_fast
    nt = SUBLANES
    kb = 2 * SUBLANES
    w1 = _bf16_table(_channel_dft_matrix(LANES // F_GROUP_DIM))
    perm = _bf16_table(np.eye(n_slow * nt).reshape(n_slow, nt, n_slow * nt).transpose(1, 0, 2)
                       .reshape(n_slow * nt, n_slow * nt))
    c1, s1 = _dft_cos_sin(n_slow)
    sc1 = n_slow ** -0.5
    cs = _bf16_table(np.concatenate([c1, s1], axis=0) * sc1)
    sc = _bf16_table(np.concatenate([-s1, c1], axis=0) * sc1)
    tw = 2.0 * np.pi * np.outer(np.arange(n_fast), np.arange(n_slow)) / n
    tc = jnp.asarray(np.cos(tw)[:, :, None], F32)
    ts = jnp.asarray(np.sin(tw)[:, :, None], F32)
    z_shape = jax.ShapeDtypeStruct((bsz, n_fast, n_slow, F_WIDTH), BF16)
    z_spec = pl.BlockSpec((1, nt, n_slow, F_WIDTH), lambda b, j: (b, j, 0, 0))
    const2 = lambda b, j: (0, 0)
    tw_spec = pl.BlockSpec((nt, n_slow, 1), lambda b, j: (j, 0, 0))
    zr, zi = pl.pallas_call(
        functools.partial(_fft_stage1_kernel, n_slow=n_slow, nt=nt),
        grid=(bsz, n_fast // nt),
        in_specs=[pl.BlockSpec((1, n_slow, nt, F_WIDTH), lambda b, j: (b, 0, j, 0)),
                  pl.BlockSpec(perm.shape, const2),
                  pl.BlockSpec(w1.shape, const2), pl.BlockSpec(cs.shape, const2), pl.BlockSpec(sc.shape, const2),
                  tw_spec, tw_spec],
        out_specs=[z_spec, z_spec],
        out_shape=[z_shape, z_shape],
        compiler_params=_cparams(2),
        name="fnet_stage1",
    )(f.reshape(bsz, n_slow, n_fast, F_WIDTH), perm, w1, cs, sc, tc, ts)

    c2, s2 = _dft_cos_sin(n_fast)
    sc2 = n_fast ** -0.5
    eye = np.eye(kb)
    f2 = _bf16_table(np.concatenate([np.kron(c2, eye), np.kron(-s2, eye)], axis=1) * sc2)
    blk = pl.BlockSpec((1, n_fast, kb, F_WIDTH), lambda b, j: (b, 0, j, 0))
    y = pl.pallas_call(
        functools.partial(_fft_stage2_kernel, n_fast=n_fast, kb=kb),
        grid=(bsz, n_slow // kb),
        in_specs=[blk, blk, pl.BlockSpec(f2.shape, const2)],
        out_specs=blk,
        out_shape=jax.ShapeDtypeStruct((bsz, n_fast, n_slow, F_WIDTH), BF16),
        compiler_params=_cparams(2),
        name="fnet_stage2",
    )(zr, zi, f2)
    return y.reshape(bsz, n, F_WIDTH)


def _second_largest_sum(a, b, c, d):
    mab, nab = jnp.maximum(a, b), jnp.minimum(a, b)
    mcd, ncd = jnp.maximum(c, d), jnp.minimum(c, d)
    return jnp.maximum(mab, mcd) + jnp.maximum(jnp.minimum(mab, mcd), jnp.maximum(nab, ncd))


def _selected_group(sb_rows):
    epg = EXPERTS_PER_GROUP
    g_score = [_second_largest_sum(*sb_rows[g * epg:(g + 1) * epg]) for g in range(N_GROUPS)]
    best = functools.reduce(jnp.maximum, g_score)
    group = jnp.full_like(best, float(N_GROUPS - 1))
    for g in range(N_GROUPS - 2, -1, -1):
        group = jnp.where(g_score[g] == best, float(g), group)
    return group


def _top2_gates(cand_s, cand_sb):
    n = len(cand_s)
    w = []
    for j in range(n):
        rank = jnp.zeros_like(cand_sb[j])
        for i in range(n):
            if i == j:
                continue
            ahead = (cand_sb[i] > cand_sb[j]) | ((cand_sb[i] == cand_sb[j]) & (i < j))
            rank = rank + jnp.where(ahead, 1.0, 0.0)
        w.append(jnp.where(rank < 2.0, cand_s[j], 0.0))
    total = functools.reduce(jnp.add, w)
    return [wj / total for wj in w]


CHUNK_ROWS = 16


def _local_rows(tm):
    need = tm + N_GROUPS * (CHUNK_ROWS - 1) + CHUNK_ROWS
    return -(-need // CHUNK_ROWS) * CHUNK_ROWS


def _sort_matrix(lslot, lr):
    r_iota = lax.broadcasted_iota(jnp.int32, (lr, 1), 0).astype(F32)
    return jnp.where(r_iota == lslot, 1.0, 0.0).astype(BF16)


def _local_sort(group, h2b, tri_ref, lr):
    t = group.shape[1]
    g_iota = lax.broadcasted_iota(jnp.int32, (SUBLANES, 1), 0).astype(F32)
    onehot = g_iota == group
    prefix = _dot(jnp.where(onehot, 1.0, 0.0).astype(BF16), tri_ref[...])
    count = prefix[:, t - 1:t]
    padded = jnp.floor((count + (CHUNK_ROWS - 1.0)) * (1.0 / CHUNK_ROWS)) * CHUNK_ROWS
    lslot = jnp.zeros((1, t), F32)
    start = jnp.zeros((1, 1), F32)
    for g in range(N_GROUPS):
        lslot = jnp.where(onehot[g:g + 1], start + prefix[g:g + 1] - 1.0, lslot)
        start = start + padded[g:g + 1]
    return lslot, count, _dot(_sort_matrix(lslot, lr), h2b).astype(BF16)


def _out_kernel(a_ref, y_ref, x_ref, m_ref, wo_ref, bd_ref, g_ref, b_ref, wr_ref, rb_ref, tri_ref,
                x1_ref, h2_ref, lslot_ref, count_ref):
    y2 = _dot(y_ref[0].astype(BF16), bd_ref[...]).astype(BF16)
    o = _dot(a_ref[0], wo_ref[:NA_WIDTH, :]) + _dot(y2, wo_ref[NA_WIDTH:, :])
    z = DEEPNORM_ALPHA * x_ref[0] + m_ref[0, 2:3, :] * o
    x1 = _layer_norm(z) * g_ref[...] + b_ref[...]
    x1_ref[0] = x1
    h2 = (_layer_norm(x1) * (1.0 + m_ref[0, 4:5, :]) + m_ref[0, 3:4, :]).astype(BF16)
    sb = jax.nn.sigmoid(_dot_nt(wr_ref[...], h2)) + rb_ref[...]
    group = _selected_group([sb[e:e + 1] for e in range(N_EXPERTS)])
    lslot, count, h2_sorted = _local_sort(group, h2, tri_ref, h2_ref.shape[0])
    h2_ref[...] = h2_sorted
    lslot_ref[0] = jnp.broadcast_to(lslot, lslot_ref.shape[1:])
    count_ref[0] = jnp.broadcast_to(count, count_ref.shape[1:])


def _out_projection(attn, yf, x, m, w_out_bf16, w_four_bd, ln_g, ln_b, w_router_t, router_bias, tm):
    bsz, length, _ = x.shape
    tiles_per_batch = length // tm
    n_tiles = bsz * tiles_per_batch
    lr = _local_rows(tm)
    tri = _bf16_table(np.triu(np.ones((tm, tm))))
    row = lambda b, i: (b, i, 0)
    tile = lambda b, i: (b * tiles_per_batch + i, 0)
    const2 = lambda b, i: (0, 0)
    return pl.pallas_call(
        _out_kernel,
        grid=(bsz, tiles_per_batch),
        in_specs=[pl.BlockSpec((1, tm, NA_WIDTH), row), pl.BlockSpec((1, tm, F_WIDTH), row),
                  pl.BlockSpec((1, tm, D_MODEL), row),
                  pl.BlockSpec((1, N_MOD, D_MODEL), lambda b, i: (b, 0, 0)),
                  pl.BlockSpec((NA_WIDTH + F_WIDTH, D_MODEL), const2),
                  pl.BlockSpec((F_WIDTH, F_WIDTH), const2),
                  pl.BlockSpec((1, D_MODEL), const2), pl.BlockSpec((1, D_MODEL), const2),
                  pl.BlockSpec((N_EXPERTS, D_MODEL), const2), pl.BlockSpec((N_EXPERTS, 1), const2),
                  pl.BlockSpec((tm, tm), const2)],
        out_specs=[pl.BlockSpec((1, tm, D_MODEL), row),
                   pl.BlockSpec((lr, D_MODEL), tile),
                   pl.BlockSpec((1, SUBLANES, tm), lambda b, i: (b, 0, i)),
                   pl.BlockSpec((1, SUBLANES, LANES), lambda b, i: (b * tiles_per_batch + i, 0, 0))],
        out_shape=[jax.ShapeDtypeStruct((bsz, length, D_MODEL), F32),
                   jax.ShapeDtypeStruct((n_tiles * lr, D_MODEL), BF16),
                   jax.ShapeDtypeStruct((bsz, SUBLANES, length), F32),
                   jax.ShapeDtypeStruct((n_tiles, SUBLANES, LANES), F32)],
        compiler_params=_cparams(2),
        name="out_proj_norm_route",
    )(attn, yf, x, m, w_out_bf16, w_four_bd, ln_g.reshape(1, D_MODEL), ln_b.reshape(1, D_MODEL),
      w_router_t, router_bias.reshape(N_EXPERTS, 1), tri)


def _run_plan(count, tm, lr, tm_slots):
    n_tiles = count.shape[0]
    ch = CHUNK_ROWS
    n_slots = -(-(n_tiles * (tm + N_GROUPS * (ch - 1)) + N_GROUPS * tm_slots) // tm_slots) * tm_slots
    zero_chunk = lr // ch - 1
    pad = ((count + ch - 1) // ch) * ch
    lstart = jnp.cumsum(pad, axis=1) - pad
    run_off = jnp.cumsum(pad, axis=0) - pad
    seg_len = jnp.sum(pad, axis=0)
    seg_pad = ((seg_len + tm_slots - 1) // tm_slots) * tm_slots
    seg_end = jnp.cumsum(seg_pad)
    seg_start = seg_end - seg_pad

    s = (jnp.arange(n_slots // ch, dtype=jnp.int32) * ch)[:, None, None]
    in_run = (seg_start + run_off <= s) & (s < seg_start + run_off + pad)
    src_row = jnp.arange(n_tiles, dtype=jnp.int32)[None, :, None] * lr + lstart + (s - seg_start - run_off)
    src_row = jnp.sum(jnp.where(in_run, src_row, 0), axis=(1, 2))
    src_chunk = jnp.where(jnp.any(in_run, axis=(1, 2)), src_row // ch, zero_chunk).astype(jnp.int32)

    tile_start = jnp.arange(n_slots // tm_slots, dtype=jnp.int32) * tm_slots
    tile_group = jnp.minimum(jnp.sum((seg_end[None, :] <= tile_start[:, None]).astype(jnp.int32), axis=1),
                             N_GROUPS - 1)
    n_used = (seg_end[N_GROUPS - 1:] // tm_slots).astype(jnp.int32)

    r = (jnp.arange(lr // ch, dtype=jnp.int32) * ch)[None, :, None]
    in_local = (lstart[:, None, :] <= r) & (r < (lstart + pad)[:, None, :])
    slot = (seg_start + run_off - lstart)[:, None, :] + r
    back_chunk = (jnp.sum(jnp.where(in_local, slot, 0), axis=2) // ch).astype(jnp.int32)
    return src_chunk, tile_group.astype(jnp.int32), n_used, back_chunk.reshape(-1)


def _chunk_gather(idx_ref, tile, n_chunks, src_hbm, buf, sems, wait):
    slot = tile % 2
    rows = n_chunks * CHUNK_ROWS
    if wait:
        pltpu.make_async_copy(src_hbm.at[pl.ds(0, rows), :], buf.at[slot], sems.at[slot]).wait()
        return
    for c in range(n_chunks):
        src = pl.multiple_of(idx_ref[tile * n_chunks + c] * CHUNK_ROWS, CHUNK_ROWS)
        pltpu.make_async_copy(src_hbm.at[pl.ds(src, CHUNK_ROWS), :],
                              buf.at[slot, pl.ds(c * CHUNK_ROWS, CHUNK_ROWS), :], sems.at[slot]).start()


def _moe_runs_kernel(src_ref, tile_group_ref, n_used_ref, h_hbm, wg32_ref, wu32_ref, wd32_ref, wr_ref, rb_ref,
                     y_ref, hbuf, wg_ref, wu_ref, wd_ref, sems, *, tm):
    step = pl.program_id(0)
    n_used = n_used_ref[0]
    n_chunks = tm // CHUNK_ROWS
    new_group = (step == 0) | (tile_group_ref[step] != tile_group_ref[jnp.maximum(step - 1, 0)])

    @pl.when(new_group & (step < n_used))
    def _():
        wg_ref[...] = wg32_ref[...].astype(BF16)
        wu_ref[...] = wu32_ref[...].astype(BF16)
        wd_ref[...] = wd32_ref[...].astype(BF16)

    @pl.when((step == 0) & (n_used > 0))
    def _():
        _chunk_gather(src_ref, 0, n_chunks, h_hbm, hbuf, sems, wait=False)

    @pl.when(step + 1 < n_used)
    def _():
        _chunk_gather(src_ref, step + 1, n_chunks, h_hbm, hbuf, sems, wait=False)

    @pl.when(step < n_used)
    def _():
        _chunk_gather(src_ref, step, n_chunks, h_hbm, hbuf, sems, wait=True)
        h = hbuf[step % 2]
        s = jax.nn.sigmoid(_dot(h, wr_ref[0]))
        sb = s + rb_ref[0]
        epg = EXPERTS_PER_GROUP
        gates = _top2_gates([s[:, j:j + 1] for j in range(epg)], [sb[:, j:j + 1] for j in range(epg)])
        acc = None
        gate_up = [(_dot(h, wg_ref[0]), _dot(h, wu_ref[0]))]
        for e in range(epg):
            if e + 1 < epg:
                gate_up.append((_dot(h, wg_ref[e + 1]), _dot(h, wu_ref[e + 1])))
            gate, up = gate_up[e]
            hid = (gate * jax.nn.sigmoid(gate)) * up * gates[e]
            y = _dot(hid.astype(BF16), wd_ref[e])
            acc = y if acc is None else acc + y
        y_ref[...] = acc.astype(BF16)

    @pl.when(step >= n_used)
    def _():
        y_ref[...] = jnp.zeros_like(y_ref)


def _moe_runs(h_sorted, src_chunk, tile_group, n_used, layer, w_gate, w_up, w_down, w_router_grp,
              router_bias_grp, tm):
    n_slots = src_chunk.shape[0] * CHUNK_ROWS
    epg = EXPERTS_PER_GROUP
    by_group = lambda i, src_ref, tg_ref, nu_ref: (tg_ref[i], 0, 0)
    by_layer_group = lambda i, src_ref, tg_ref, nu_ref: (layer * N_GROUPS + tg_ref[i], 0, 0)
    grid_spec = pltpu.PrefetchScalarGridSpec(
        num_scalar_prefetch=3,
        grid=(n_slots // tm,),
        in_specs=[pl.BlockSpec(memory_space=pl.ANY),
                  pl.BlockSpec((epg, D_MODEL, D_EXPERT), by_layer_group),
                  pl.BlockSpec((epg, D_MODEL, D_EXPERT), by_layer_group),
                  pl.BlockSpec((epg, D_EXPERT, D_MODEL), by_layer_group),
                  pl.BlockSpec((1, D_MODEL, LANES), by_group),
                  pl.BlockSpec((1, 1, LANES), by_group)],
        out_specs=pl.BlockSpec((tm, D_MODEL), lambda i, src_ref, tg_ref, nu_ref: (i, 0)),
        scratch_shapes=[pltpu.VMEM((2, tm, D_MODEL), BF16),
                        pltpu.VMEM((epg, D_MODEL, D_EXPERT), BF16), pltpu.VMEM((epg, D_MODEL, D_EXPERT), BF16),
                        pltpu.VMEM((epg, D_EXPERT, D_MODEL), BF16), pltpu.SemaphoreType.DMA((2,))],
    )
    return pl.pallas_call(
        functools.partial(_moe_runs_kernel, tm=tm),
        grid_spec=grid_spec,
        out_shape=jax.ShapeDtypeStruct((n_slots, D_MODEL), BF16),
        compiler_params=_cparams(1),
        name="moe_group_experts",
    )(src_chunk, tile_group, n_used, h_sorted, w_gate, w_up, w_down, w_router_grp, router_bias_grp)


def _gather_unsort(back_ref, lslot_ref, y_hbm, ybuf, sems, step, n_steps, lr):
    n_chunks = lr // CHUNK_ROWS

    @pl.when(step == 0)
    def _():
        _chunk_gather(back_ref, 0, n_chunks, y_hbm, ybuf, sems, wait=False)

    @pl.when(step + 1 < n_steps)
    def _():
        _chunk_gather(back_ref, step + 1, n_chunks, y_hbm, ybuf, sems, wait=False)

    _chunk_gather(back_ref, step, n_chunks, y_hbm, ybuf, sems, wait=True)
    sort = _sort_matrix(lslot_ref[0, 0:1, :], lr)
    return lax.dot_general(sort, ybuf[step % 2], (((0,), (0,)), ((), ())), preferred_element_type=F32)


def _residual_norm_kernel(back_ref, y_hbm, lslot_ref, x1_ref, m_ref, g_ref, b_ref, o_ref, ybuf, sems, *, n_steps, lr):
    y = _gather_unsort(back_ref, lslot_ref, y_hbm, ybuf, sems, pl.program_id(0), n_steps, lr)
    z = DEEPNORM_ALPHA * x1_ref[...] + m_ref[0, 5:6, :] * y
    o_ref[...] = _layer_norm(z) * g_ref[...] + b_ref[...]


def _residual_norm(y_slots, back_chunk, lslot, x1, m, ln_g, ln_b, tm):
    bsz, length, _ = x1.shape
    n = bsz * length
    tiles_per_batch = length // tm
    lr = _local_rows(tm)
    const2 = lambda i, back_ref: (0, 0)
    row = pl.BlockSpec((tm, D_MODEL), lambda i, back_ref: (i, 0))
    grid_spec = pltpu.PrefetchScalarGridSpec(
        num_scalar_prefetch=1,
        grid=(n // tm,),
        in_specs=[pl.BlockSpec(memory_space=pl.ANY),
                  pl.BlockSpec((1, SUBLANES, tm),
                               lambda i, back_ref: (i // tiles_per_batch, 0, i % tiles_per_batch)),
                  row, pl.BlockSpec((1, N_MOD, D_MODEL), lambda i, back_ref: (i // tiles_per_batch, 0, 0)),
                  pl.BlockSpec((1, D_MODEL), const2), pl.BlockSpec((1, D_MODEL), const2)],
        out_specs=row,
        scratch_shapes=[pltpu.VMEM((2, lr, D_MODEL), BF16), pltpu.SemaphoreType.DMA((2,))],
    )
    out = pl.pallas_call(
        functools.partial(_residual_norm_kernel, n_steps=n // tm, lr=lr),
        grid_spec=grid_spec,
        out_shape=jax.ShapeDtypeStruct((n, D_MODEL), F32),
        compiler_params=_cparams(1),
        name="moe_residual_norm",
    )(back_chunk, y_slots, lslot, x1.reshape(n, D_MODEL), m, ln_g.reshape(1, D_MODEL), ln_b.reshape(1, D_MODEL))
    return out.reshape(bsz, length, D_MODEL)


def _grouped_moe_runs(h_sorted, count_rows, layer, w_gate, w_up, w_down, w_router_grp, router_bias_grp,
                      tm_tokens, tm_slots):
    count = count_rows[:, :N_GROUPS, 0].astype(jnp.int32)
    src_chunk, tile_group, n_used, back_chunk = _run_plan(count, tm_tokens, _local_rows(tm_tokens), tm_slots)
    y_slots = _moe_runs(h_sorted, src_chunk, tile_group, n_used, layer, w_gate, w_up, w_down, w_router_grp,
                        router_bias_grp, tm_slots)
    return y_slots, back_chunk


def _ctx_fourier_kernel(f_ref, w1_ref, c_ref, s_ref, y_ref):
    ab = _dot(f_ref[0].astype(BF16), w1_ref[...]).astype(BF16)
    y = _dot(c_ref[...], ab[:, :F_WIDTH]) + _dot(s_ref[...], ab[:, F_WIDTH:])
    y_ref[0] = y.astype(BF16)


def _context_fourier(fc):
    bsz, n, _ = fc.shape
    w1 = _bf16_table(_channel_dft_matrix())
    c, s = _dft_cos_sin(n)
    cm = _bf16_table(c * n ** -0.5)
    sm = _bf16_table(-s * n ** -0.5)
    const2 = lambda b: (0, 0)
    blk = pl.BlockSpec((1, n, F_WIDTH), lambda b: (b, 0, 0))
    return pl.pallas_call(
        _ctx_fourier_kernel,
        grid=(bsz,),
        in_specs=[blk, pl.BlockSpec(w1.shape, const2), pl.BlockSpec(cm.shape, const2),
                  pl.BlockSpec(sm.shape, const2)],
        out_specs=blk,
        out_shape=jax.ShapeDtypeStruct((bsz, n, F_WIDTH), BF16),
        compiler_params=_cparams(1),
        name="context_fnet",
    )(fc, w1, cm, sm)


def _block_diag(w):
    g, c, _ = w.shape
    eye = jnp.eye(g, dtype=w.dtype)
    return (eye[:, None, :, None] * w[:, :, None, :]).reshape(g * c, g * c)


def kernel(x, c, ctx, c_ctx, w_mod, b_mod, w_in, rpb, w_four, w_out, ln1_g, ln1_b, ln2_g, ln2_b,
           w_router, router_bias, w_gate, w_up, w_down):
    bsz, length, _ = x.shape
    n_ctx = ctx.shape[1]
    rows = length // GRID_W

    cvec = jnp.concatenate([c, c_ctx[None, :], jnp.zeros((8 - bsz - 1, D_MODEL), F32)], axis=0)
    mods = _modulation(cvec, w_mod, b_mod)
    w_router_t = w_router.T.astype(BF16)
    toeplitz = _rpb_toeplitz(rpb)
    lane_pad = LANES - EXPERTS_PER_GROUP
    w_router_grp = jnp.pad(w_router.reshape(D_MODEL, N_GROUPS, EXPERTS_PER_GROUP).transpose(1, 0, 2),
                           ((0, 0), (0, 0), (0, lane_pad))).astype(BF16)
    router_bias_grp = jnp.pad(router_bias.astype(F32).reshape(N_GROUPS, 1, EXPERTS_PER_GROUP),
                              ((0, 0), (0, 0), (0, lane_pad)))

    wg = w_gate.reshape(DEPTH * N_EXPERTS, D_MODEL, D_EXPERT)
    wu = w_up.reshape(DEPTH * N_EXPERTS, D_MODEL, D_EXPERT)
    wd = w_down.reshape(DEPTH * N_EXPERTS, D_EXPERT, D_MODEL)

    def latent_mod(i):
        return mods[i, :bsz].reshape(bsz, N_MOD, D_MODEL)

    xc = ctx
    projected = None
    for i in range(DEPTH):
        last = i == DEPTH - 1
        m = latent_mod(i)
        mc = jnp.broadcast_to(mods[i, bsz].reshape(1, N_MOD, D_MODEL), (bsz, N_MOD, D_MODEL))
        w_in_b = w_in[i].astype(BF16)
        w_out_b = w_out[i].astype(BF16)
        w_four_bd = _block_diag(w_four[i]).astype(BF16)

        q, k, v, f = projected if projected is not None else _in_projection(x, m, w_in_b, tm=1024)
        qc, kc, vc, fc = _in_projection(xc, mc, w_in_b, tm=n_ctx)

        attn = _neighborhood_attention(q, k, v, kc, vc, toeplitz, i)
        yf = _fourier_positions(f, n_slow=rows, n_fast=GRID_W)
        x1, h2, lslot, cnt = _out_projection(attn, yf, x, m, w_out_b, w_four_bd, ln1_g[i], ln1_b[i],
                                             w_router_t, router_bias, tm=512)
        y, back = _grouped_moe_runs(h2, cnt, i, wg, wu, wd, w_router_grp, router_bias_grp,
                                    tm_tokens=512, tm_slots=512)
        if last:
            return _residual_norm(y, back, lslot, x1, m, ln2_g[i], ln2_b[i], tm=512)
        x, *projected = _norm_in_projection(y, back, lslot, x1, m, ln2_g[i], ln2_b[i], latent_mod(i + 1),
                                            w_in[i + 1].astype(BF16), tm=512)

        attn_c = _context_attention(qc, kc, vc)
        yc = _context_fourier(fc)
        xc1, h2c, lslot_c, cnt_c = _out_projection(attn_c, yc, xc, mc, w_out_b, w_four_bd, ln1_g[i], ln1_b[i],
                                                   w_router_t, router_bias, tm=n_ctx)
        yc2, back_c = _grouped_moe_runs(h2c, cnt_c, i, wg, wu, wd, w_router_grp, router_bias_grp,
                                        tm_tokens=n_ctx, tm_slots=128)
        xc = _residual_norm(yc2, back_c, lslot_c, xc1, mc, ln2_g[i], ln2_b[i], tm=n_ctx)
    return x
```

```python
import functools
import math

import numpy as np
import jax
import jax.numpy as jnp
from jax import lax
from jax.experimental import pallas as pl
from jax.experimental.pallas import tpu as pltpu

D_MODEL = 1024
DEPTH = 2
GRID_W = 64
NA_HEADS = 8
HEAD_DIM = 64
NA_WIDTH = NA_HEADS * HEAD_DIM
WIN_ROWS = 8
WIN_COLS = 16
F_GROUPS = 8
F_GROUP_DIM = 64
F_WIDTH = F_GROUPS * F_GROUP_DIM
IN_WIDTH = 3 * NA_WIDTH + F_WIDTH
N_EXPERTS = 16
N_GROUPS = 4
EXPERTS_PER_GROUP = N_EXPERTS // N_GROUPS
D_EXPERT = 256
N_MOD = 6
DEEPNORM_ALPHA = (2.0 * DEPTH) ** 0.25
LN_EPS = 1e-6

F32 = jnp.float32
BF16 = jnp.bfloat16

V7X_VMEM_BYTES = 64 * 1024 * 1024
VMEM_LIMIT_BYTES = (V7X_VMEM_BYTES * 3) // 4
LANES = 128
SUBLANES = 8
HEADS_PER_STEP = LANES // HEAD_DIM
assert HEADS_PER_STEP == 2
MASK_VALUE = -1e30
LOG2_E = math.log2(math.e)

CHUNK_GRID_ROWS = 4
WINDOW_CHUNKS = 3
STEP_CHUNKS = 8


def _cparams(n_grid_dims):
    return pltpu.CompilerParams(dimension_semantics=("arbitrary",) * n_grid_dims,
                                vmem_limit_bytes=VMEM_LIMIT_BYTES)


def _layer_norm(x):
    mu = jnp.mean(x, axis=-1, keepdims=True)
    xc = x - mu
    var = jnp.mean(xc * xc, axis=-1, keepdims=True)
    return xc * lax.rsqrt(var + LN_EPS)


def _dot(a, b):
    return jnp.dot(a, b, preferred_element_type=F32)


def _dot_nt(a, b):
    return lax.dot_general(a, b, (((1,), (1,)), ((), ())), preferred_element_type=F32)


def _mod_kernel(c_ref, w_ref, b_ref, o_ref):
    c = c_ref[...]
    a = c * jax.nn.sigmoid(c)
    o_ref[0] = jnp.dot(a, w_ref[0], preferred_element_type=F32, precision=lax.Precision.HIGHEST) + b_ref[0]


def _modulation(cvec, w_mod, b_mod):
    n_col_blocks = 4
    wc = (N_MOD * D_MODEL) // n_col_blocks
    rows = cvec.shape[0]
    return pl.pallas_call(
        _mod_kernel,
        grid=(DEPTH, n_col_blocks),
        in_specs=[pl.BlockSpec((rows, D_MODEL), lambda i, j: (0, 0)),
                  pl.BlockSpec((1, D_MODEL, wc), lambda i, j: (i, 0, j)),
                  pl.BlockSpec((1, 1, wc), lambda i, j: (i, 0, j))],
        out_specs=pl.BlockSpec((1, rows, wc), lambda i, j: (i, 0, j)),
        out_shape=jax.ShapeDtypeStruct((DEPTH, rows, N_MOD * D_MODEL), F32),
        compiler_params=_cparams(2),
        name="modulation",
    )(cvec, w_mod, b_mod.reshape(DEPTH, 1, N_MOD * D_MODEL))


def _modulate(x, m_ref):
    return (_layer_norm(x) * (1.0 + m_ref[0, 1:2, :]) + m_ref[0, 0:1, :]).astype(BF16)


def _project(h, rows, w_ref, q_ref, k_ref, v_ref, f_ref):
    p = _dot(h, w_ref[...])
    q_ref[0, rows, :] = (p[:, :NA_WIDTH] * (HEAD_DIM ** -0.5 * LOG2_E)).astype(BF16)
    k_ref[0, rows, :] = p[:, NA_WIDTH:2 * NA_WIDTH].astype(BF16)
    v_ref[0, rows, :] = p[:, 2 * NA_WIDTH:3 * NA_WIDTH].astype(BF16)
    f_ref[0, rows, :] = p[:, 3 * NA_WIDTH:]


def _proj_kernel(x_ref, m_ref, w_ref, q_ref, k_ref, v_ref, f_ref):
    tm = x_ref.shape[1]
    halves = [slice(0, tm // 2), slice(tm // 2, tm)] if tm % 512 == 0 else [slice(0, tm)]
    hs = [_modulate(x_ref[0, rows, :], m_ref) for rows in halves]
    for rows, h in zip(halves, hs):
        _project(h, rows, w_ref, q_ref, k_ref, v_ref, f_ref)


def _norm_proj_kernel(back_ref, y_hbm, lslot_ref, x1_ref, m_ref, g_ref, b_ref, m_next_ref, w_ref,
                      x_ref, q_ref, k_ref, v_ref, f_ref, ybuf, sems, *, tiles_per_batch, n_steps, lr):
    step = pl.program_id(0) * tiles_per_batch + pl.program_id(1)
    y = _gather_unsort(back_ref, lslot_ref, y_hbm, ybuf, sems, step, n_steps, lr)
    z = DEEPNORM_ALPHA * x1_ref[0] + m_ref[0, 5:6, :] * y
    x = _layer_norm(z) * g_ref[...] + b_ref[...]
    x_ref[0] = x
    _project(_modulate(x, m_next_ref), slice(None), w_ref, q_ref, k_ref, v_ref, f_ref)


def _norm_in_projection(y_slots, back_chunk, lslot, x1, m, ln_g, ln_b, m_next, w_in_bf16, tm):
    bsz, length, _ = x1.shape
    tiles_per_batch = length // tm
    lr = _local_rows(tm)
    out = jax.ShapeDtypeStruct((bsz, length, NA_WIDTH), BF16)
    out_f = jax.ShapeDtypeStruct((bsz, length, F_WIDTH), F32)
    out_x = jax.ShapeDtypeStruct((bsz, length, D_MODEL), F32)
    row = lambda b, i, back_ref: (b, i, 0)
    const2 = lambda b, i, back_ref: (0, 0)
    mod_spec = pl.BlockSpec((1, N_MOD, D_MODEL), lambda b, i, back_ref: (b, 0, 0))
    o_spec = pl.BlockSpec((1, tm, NA_WIDTH), row)
    x_spec = pl.BlockSpec((1, tm, D_MODEL), row)
    grid_spec = pltpu.PrefetchScalarGridSpec(
        num_scalar_prefetch=1,
        grid=(bsz, tiles_per_batch),
        in_specs=[pl.BlockSpec(memory_space=pl.ANY),
                  pl.BlockSpec((1, SUBLANES, tm), lambda b, i, back_ref: (b, 0, i)),
                  x_spec, mod_spec,
                  pl.BlockSpec((1, D_MODEL), const2), pl.BlockSpec((1, D_MODEL), const2),
                  mod_spec, pl.BlockSpec((D_MODEL, IN_WIDTH), const2)],
        out_specs=[x_spec, o_spec, o_spec, o_spec, o_spec],
        scratch_shapes=[pltpu.VMEM((2, lr, D_MODEL), BF16), pltpu.SemaphoreType.DMA((2,))],
    )
    return pl.pallas_call(
        functools.partial(_norm_proj_kernel, tiles_per_batch=tiles_per_batch, n_steps=bsz * tiles_per_batch,
                          lr=lr),
        grid_spec=grid_spec,
        out_shape=[out_x, out, out, out, out_f],
        compiler_params=_cparams(2),
        name="moe_norm_in_proj",
    )(back_chunk, y_slots, lslot, x1, m, ln_g.reshape(1, D_MODEL), ln_b.reshape(1, D_MODEL), m_next, w_in_bf16)


def _in_projection(x, m, w_in_bf16, tm):
    bsz, length, _ = x.shape
    out = jax.ShapeDtypeStruct((bsz, length, NA_WIDTH), BF16)
    out_f = jax.ShapeDtypeStruct((bsz, length, F_WIDTH), F32)
    o_spec = pl.BlockSpec((1, tm, NA_WIDTH), lambda b, i: (b, i, 0))
    return pl.pallas_call(
        _proj_kernel,
        grid=(bsz, length // tm),
        in_specs=[pl.BlockSpec((1, tm, D_MODEL), lambda b, i: (b, i, 0)),
                  pl.BlockSpec((1, N_MOD, D_MODEL), lambda b, i: (b, 0, 0)),
                  pl.BlockSpec((D_MODEL, IN_WIDTH), lambda b, i: (0, 0))],
        out_specs=[o_spec, o_spec, o_spec, o_spec],
        out_shape=[out, out, out, out_f],
        compiler_params=_cparams(2),
        name="ln_mod_in_proj",
    )(x, m, w_in_bf16)


def _head_lanes(h):
    lane = lax.broadcasted_iota(jnp.int32, (1, LANES), 1)
    return (lane >= HEAD_DIM * h) & (lane < HEAD_DIM * (h + 1))


def _scores_pass(q, h, tiles, s_ref):
    qh = jnp.where(_head_lanes(h), q, jnp.zeros_like(q))
    m = None
    t = tiles[0][0].shape[0]
    for j, (k, _, bias) in enumerate(tiles):
        s = _dot_nt(qh, k)
        if bias is not None:
            s = s + bias
        s_ref[:, j * t:(j + 1) * t] = s
        mj = jnp.max(s, axis=-1, keepdims=True)
        m = mj if m is None else jnp.maximum(m, mj)
    return m


def _pv_pass(h, tiles, s_ref, m):
    o = None
    t = tiles[0][0].shape[0]
    in_head = _head_lanes(h)
    for j, (_, v, _) in enumerate(tiles):
        p = jnp.exp2(s_ref[:, j * t:(j + 1) * t] - m)
        oj = _dot(p.astype(BF16), jnp.where(in_head, v, jnp.ones_like(v)))
        o = oj if o is None else o + oj
    return o / pltpu.roll(o, HEAD_DIM, axis=1)


def _attention_units(units):
    outs = []
    maxima = [_scores_pass(*units[0])]
    for u in range(len(units)):
        if u + 1 < len(units):
            maxima.append(_scores_pass(*units[u + 1]))
        _, h, tiles, s_ref = units[u]
        outs.append(_pv_pass(h, tiles, s_ref, maxima[u]))
    return outs


def _merge_heads(outs):
    merged = outs[0]
    for h in range(1, len(outs)):
        merged = jnp.where(_head_lanes(h), outs[h], merged)
    return merged


DR_PAD = 2 * WIN_ROWS
DC_PAD = 2 * WIN_COLS


def _toeplitz_kernel(r_ref, sel_ref, mask_ref, o_ref):
    n = r_ref.shape[0]
    for qc in range(GRID_W):
        block = jnp.dot(r_ref[...], sel_ref[qc], preferred_element_type=F32, precision=lax.Precision.HIGHEST)
        o_ref[pl.ds(qc, n, stride=GRID_W), :] = block * LOG2_E + mask_ref[qc]


def _rpb_toeplitz(rpb):
    depth, heads, n_dr, n_dc = rpb.shape
    qc = np.arange(GRID_W)[:, None]
    kc = (np.arange(LANES) % GRID_W)[None, :]
    cs = np.clip(qc - WIN_COLS // 2, 0, GRID_W - WIN_COLS)
    col_valid = (kc >= cs) & (kc < cs + WIN_COLS)
    dc = kc - qc + WIN_COLS - 1
    select = (np.arange(DC_PAD)[None, :, None] == dc[:, None, :]) & col_valid[:, None, :]
    mask = np.where(col_valid, 0.0, MASK_VALUE)[:, None, :]
    r = jnp.pad(rpb.astype(F32), ((0, 0), (0, 0), (0, DR_PAD - n_dr), (0, DC_PAD - n_dc)))
    n = depth * heads * DR_PAD
    const3 = lambda: (0, 0, 0)
    out = pl.pallas_call(
        _toeplitz_kernel,
        grid=(),
        in_specs=[pl.BlockSpec((n, DC_PAD), lambda: (0, 0)),
                  pl.BlockSpec((GRID_W, DC_PAD, LANES), const3), pl.BlockSpec((GRID_W, 1, LANES), const3)],
        out_specs=pl.BlockSpec((n * GRID_W, LANES), lambda: (0, 0)),
        out_shape=jax.ShapeDtypeStruct((n * GRID_W, LANES), F32),
        compiler_params=pltpu.CompilerParams(vmem_limit_bytes=VMEM_LIMIT_BYTES),
        name="rpb_toeplitz",
    )(r.reshape(n, DC_PAD), jnp.asarray(select, F32), jnp.asarray(mask, F32))
    return out.reshape(depth * heads, DR_PAD, GRID_W, LANES)


assert WINDOW_CHUNKS * CHUNK_GRID_ROWS >= CHUNK_GRID_ROWS + WIN_ROWS - 1


def _window_start_chunk(cr, n_cr):
    lo = cr - (WINDOW_CHUNKS - 1) // 2
    if isinstance(cr, int):
        return max(0, min(lo, n_cr - WINDOW_CHUNKS))
    return jnp.clip(lo, 0, n_cr - WINDOW_CHUNKS)


def _window_key_rows(cr, n_cr):
    start = _window_start_chunk(cr, n_cr) * CHUNK_GRID_ROWS
    return list(range(start, start + WINDOW_CHUNKS * CHUNK_GRID_ROWS))


def _row_window(qr, rows):
    kh = min(WIN_ROWS, rows)
    rs = int(np.clip(qr - kh // 2, 0, rows - kh))
    return rs, rs + kh


def _window_variant(cr, n_cr):
    return jnp.where(cr == 0, 0, jnp.where(cr == n_cr - 1, 2, 1))


def _check_windows(rows):
    def relative(cr):
        base = cr * CHUNK_GRID_ROWS
        return ([kr - base for kr in _window_key_rows(cr, n_cr)],
                [tuple(r - base for r in _row_window(base + qi, rows)) for qi in range(CHUNK_GRID_ROWS)])

    n_cr = rows // CHUNK_GRID_ROWS
    for cr in range(n_cr):
        assert cr in (0, n_cr - 1) or relative(cr) == relative(1), cr
        have = set(_window_key_rows(cr, n_cr))
        for qi in range(CHUNK_GRID_ROWS):
            lo, hi = _row_window(cr * CHUNK_GRID_ROWS + qi, rows)
            assert set(range(lo, hi)) <= have, (cr, qi)


def _build_bias_tables(t_ref, bias_scr, rows):
    n_cr = rows // CHUNK_GRID_ROWS
    left = lax.broadcasted_iota(jnp.int32, (GRID_W, LANES), 1) < GRID_W
    masked = jnp.full((GRID_W, LANES), MASK_VALUE, F32)
    for variant, cr in enumerate((0, 1, n_cr - 1)):
        key_rows = _window_key_rows(cr, n_cr)
        for h in range(HEADS_PER_STEP):
            for qi in range(CHUNK_GRID_ROWS):
                qr = cr * CHUNK_GRID_ROWS + qi
                lo, hi = _row_window(qr, rows)
                for p in range(len(key_rows) // 2):
                    pair = [t_ref[h, kr - qr + WIN_ROWS - 1] if lo <= kr < hi else None
                            for kr in key_rows[2 * p:2 * p + 2]]
                    if pair[0] is None and pair[1] is None:
                        block = masked
                    else:
                        block = jnp.where(left, masked if pair[0] is None else pair[0],
                                          masked if pair[1] is None else pair[1])
                    bias_scr[variant, h, qi * GRID_W:(qi + 1) * GRID_W, p * LANES:(p + 1) * LANES] = block


def _na_kernel(q_ref, *refs, rows):
    n_kv = STEP_CHUNKS * WINDOW_CHUNKS
    k_refs, v_refs = refs[:n_kv], refs[n_kv:2 * n_kv]
    kc_ref, vc_ref, t_ref, o_ref, bias_scr, s_scr = refs[2 * n_kv:]
    n_cr = rows // CHUNK_GRID_ROWS
    b, rb = pl.program_id(1), pl.program_id(2)

    @pl.when((b == 0) & (rb == 0))
    def _():
        _build_bias_tables(t_ref, bias_scr, rows)

    tq = CHUNK_GRID_ROWS * GRID_W
    units = []
    for c in range(STEP_CHUNKS):
        variant = _window_variant(rb * STEP_CHUNKS + c, n_cr)
        q = q_ref[0, c * tq:(c + 1) * tq, :]
        for h in range(HEADS_PER_STEP):
            lat = [(k_refs[c * WINDOW_CHUNKS + j][0], v_refs[c * WINDOW_CHUNKS + j][0],
                    bias_scr[variant, h, :, j * tq:(j + 1) * tq]) for j in range(WINDOW_CHUNKS)]
            units.append((q, h, lat + [(kc_ref[0], vc_ref[0], None)], s_scr.at[c, h]))
    outs = _attention_units(units)
    for c in range(STEP_CHUNKS):
        o = _merge_heads(outs[c * HEADS_PER_STEP:(c + 1) * HEADS_PER_STEP])
        o_ref[0, c * tq:(c + 1) * tq, :] = o.astype(BF16)


def _neighborhood_attention(q, k, v, kc, vc, toeplitz, layer):
    bsz, length, _ = q.shape
    rows = length // GRID_W
    n_cr = rows // CHUNK_GRID_ROWS
    assert rows % (CHUNK_GRID_ROWS * STEP_CHUNKS) == 0 and n_cr >= WINDOW_CHUNKS + 2
    _check_windows(rows)
    n_ctx = kc.shape[1]
    tq = CHUNK_GRID_ROWS * GRID_W
    assert n_ctx == tq

    def kv_spec(c, j):
        return pl.BlockSpec((1, tq, LANES),
                            lambda hp, b, rb: (b, _window_start_chunk(rb * STEP_CHUNKS + c, n_cr) + j, hp))

    kv_specs = [kv_spec(c, j) for c in range(STEP_CHUNKS) for j in range(WINDOW_CHUNKS)]
    ctx_spec = pl.BlockSpec((1, n_ctx, LANES), lambda hp, b, rb: (b, 0, hp))
    q_spec = pl.BlockSpec((1, STEP_CHUNKS * tq, LANES), lambda hp, b, rb: (b, rb, hp))
    window = WINDOW_CHUNKS * tq
    return pl.pallas_call(
        functools.partial(_na_kernel, rows=rows),
        grid=(NA_HEADS // HEADS_PER_STEP, bsz, n_cr // STEP_CHUNKS),
        in_specs=([q_spec] + kv_specs + kv_specs
                  + [ctx_spec, ctx_spec,
                     pl.BlockSpec((HEADS_PER_STEP, DR_PAD, GRID_W, LANES),
                                  lambda hp, b, rb: (layer * (NA_HEADS // HEADS_PER_STEP) + hp, 0, 0, 0))]),
        out_specs=q_spec,
        out_shape=jax.ShapeDtypeStruct((bsz, length, NA_WIDTH), BF16),
        scratch_shapes=[pltpu.VMEM((3, HEADS_PER_STEP, tq, window), F32),
                        pltpu.VMEM((STEP_CHUNKS, HEADS_PER_STEP, tq, window + n_ctx), F32)],
        compiler_params=_cparams(3),
        name="neighborhood_attention",
    )(q, *([k] * len(kv_specs)), *([v] * len(kv_specs)), kc, vc, toeplitz)


def _ctx_attn_kernel(q_ref, k_ref, v_ref, o_ref, s_scr):
    tiles = [(k_ref[0], v_ref[0], None)]
    outs = _attention_units([(q_ref[0], h, tiles, s_scr.at[h]) for h in range(HEADS_PER_STEP)])
    o_ref[0] = _merge_heads(outs).astype(BF16)


def _context_attention(qc, kc, vc):
    bsz, n_ctx, _ = qc.shape
    spec = pl.BlockSpec((1, n_ctx, LANES), lambda b, hp: (b, 0, hp))
    return pl.pallas_call(
        _ctx_attn_kernel,
        grid=(bsz, NA_HEADS // HEADS_PER_STEP),
        in_specs=[spec, spec, spec],
        out_specs=spec,
        out_shape=jax.ShapeDtypeStruct((bsz, n_ctx, NA_WIDTH), BF16),
        scratch_shapes=[pltpu.VMEM((HEADS_PER_STEP, n_ctx, n_ctx), F32)],
        compiler_params=_cparams(2),
        name="context_attention",
    )(qc, kc, vc)


def _dft_cos_sin(n):
    ang = 2.0 * np.pi * np.outer(np.arange(n), np.arange(n)) / n
    return np.cos(ang), np.sin(ang)


def _bf16_table(a):
    return jnp.asarray(a, F32).astype(BF16)


def _channel_dft_matrix(n_groups=F_GROUPS):
    c, s = _dft_cos_sin(F_GROUP_DIM)
    scale = F_GROUP_DIM ** -0.5
    eye = np.eye(n_groups)
    return np.concatenate([np.kron(eye, c), np.kron(eye, s)], axis=1) * scale


def _fft_stage1_kernel(f_ref, perm_ref, w1_ref, cs_ref, sc_ref, tc_ref, ts_ref, zr_ref, zi_ref, *, n_slow, nt):
    x = f_ref[0].reshape(n_slow * nt, F_WIDTH).astype(BF16)
    x = _dot(perm_ref[...], x).astype(BF16)
    ab = [_dot(x[:, p * LANES:(p + 1) * LANES], w1_ref[...]).astype(BF16) for p in range(F_WIDTH // LANES)]
    a_all = jnp.concatenate([blk[:, :LANES] for blk in ab], axis=-1)
    b_all = jnp.concatenate([blk[:, LANES:] for blk in ab], axis=-1)
    for t in range(nt):
        a = a_all[t * n_slow:(t + 1) * n_slow]
        b = b_all[t * n_slow:(t + 1) * n_slow]
        z = _dot(cs_ref[...], a) + _dot(sc_ref[...], b)
        zr, zi = z[:n_slow], z[n_slow:]
        c, s = tc_ref[t], ts_ref[t]
        zr_ref[0, t] = (zr * c - zi * s).astype(BF16)
        zi_ref[0, t] = (zr * s + zi * c).astype(BF16)


def _fft_stage2_kernel(zr_ref, zi_ref, f_ref, y_ref, *, n_fast, kb):
    rhs = jnp.concatenate([zr_ref[0].reshape(n_fast * kb, F_WIDTH),
                           zi_ref[0].reshape(n_fast * kb, F_WIDTH)], axis=0)
    y_ref[0] = _dot(f_ref[...], rhs).astype(BF16).reshape(n_fast, kb, F_WIDTH)


def _fourier_positions(f, n_slow, n_fast):
    bsz, n, _ = f.shape
    assert n == n_slow * n_fast
    nt = SUBLANES
    kb = 2 * SUBLANES
    w1 = _bf16_table(_channel_dft_matrix(LANES // F_GROUP_DIM))
    perm = _bf16_table(np.eye(n_slow * nt).reshape(n_slow, nt, n_slow * nt).transpose(1, 0, 2)
                       .reshape(n_slow * nt, n_slow * nt))
    c1, s1 = _dft_cos_sin(n_slow)
    sc1 = n_slow ** -0.5
    cs = _bf16_table(np.concatenate([c1, s1], axis=0) * sc1)
    sc = _bf16_table(np.concatenate([-s1, c1], axis=0) * sc1)
    tw = 2.0 * np.pi * np.outer(np.arange(n_fast), np.arange(n_slow)) / n
    tc = jnp.asarray(np.cos(tw)[:, :, None], F32)
    ts = jnp.asarray(np.sin(tw)[:, :, None], F32)
    z_shape = jax.ShapeDtypeStruct((bsz, n_fast, n_slow, F_WIDTH), BF16)
    z_spec = pl.BlockSpec((1, nt, n_slow, F_WIDTH), lambda b, j: (b, j, 0, 0))
    const2 = lambda b, j: (0, 0)
    tw_spec = pl.BlockSpec((nt, n_slow, 1), lambda b, j: (j, 0, 0))
    zr, zi = pl.pallas_call(
        functools.partial(_fft_stage1_kernel, n_slow=n_slow, nt=nt),
        grid=(bsz, n_fast // nt),
        in_specs=[pl.BlockSpec((1, n_slow, nt, F_WIDTH), lambda b, j: (b, 0, j, 0)),
                  pl.BlockSpec(perm.shape, const2),
                  pl.BlockSpec(w1.shape, const2), pl.BlockSpec(cs.shape, const2), pl.BlockSpec(sc.shape, const2),
                  tw_spec, tw_spec],
        out_specs=[z_spec, z_spec],
        out_shape=[z_shape, z_shape],
        compiler_params=_cparams(2),
        name="fnet_stage1",
    )(f.reshape(bsz, n_slow, n_fast, F_WIDTH), perm, w1, cs, sc, tc, ts)

    c2, s2 = _dft_cos_sin(n_fast)
    sc2 = n_fast ** -0.5
    eye = np.eye(kb)
    f2 = _bf16_table(np.concatenate([np.kron(c2, eye), np.kron(-s2, eye)], axis=1) * sc2)
    blk = pl.BlockSpec((1, n_fast, kb, F_WIDTH), lambda b, j: (b, 0, j, 0))
    y = pl.pallas_call(
        functools.partial(_fft_stage2_kernel, n_fast=n_fast, kb=kb),
        grid=(bsz, n_slow // kb),
        in_specs=[blk, blk, pl.BlockSpec(f2.shape, const2)],
        out_specs=blk,
        out_shape=jax.ShapeDtypeStruct((bsz, n_fast, n_slow, F_WIDTH), BF16),
        compiler_params=_cparams(2),
        name="fnet_stage2",
    )(zr, zi, f2)
    return y.reshape(bsz, n, F_WIDTH)


def _second_largest_sum(a, b, c, d):
    mab, nab = jnp.maximum(a, b), jnp.minimum(a, b)
    mcd, ncd = jnp.maximum(c, d), jnp.minimum(c, d)
    return jnp.maximum(mab, mcd) + jnp.maximum(jnp.minimum(mab, mcd), jnp.maximum(nab, ncd))


def _selected_group(sb_rows):
    epg = EXPERTS_PER_GROUP
    g_score = [_second_largest_sum(*sb_rows[g * epg:(g + 1) * epg]) for g in range(N_GROUPS)]
    best = functools.reduce(jnp.maximum, g_score)
    group = jnp.full_like(best, float(N_GROUPS - 1))
    for g in range(N_GROUPS - 2, -1, -1):
        group = jnp.where(g_score[g] == best, float(g), group)
    return group


def _top2_gates(cand_s, cand_sb):
    n = len(cand_s)
    w = []
    for j in range(n):
        rank = jnp.zeros_like(cand_sb[j])
        for i in range(n):
            if i == j:
                continue
            ahead = (cand_sb[i] > cand_sb[j]) | ((cand_sb[i] == cand_sb[j]) & (i < j))
            rank = rank + jnp.where(ahead, 1.0, 0.0)
        w.append(jnp.where(rank < 2.0, cand_s[j], 0.0))
    total = functools.reduce(jnp.add, w)
    return [wj / total for wj in w]


CHUNK_ROWS = 16


def _local_rows(tm):
    need = tm + N_GROUPS * (CHUNK_ROWS - 1) + CHUNK_ROWS
    return -(-need // CHUNK_ROWS) * CHUNK_ROWS


def _sort_matrix(lslot, lr):
    r_iota = lax.broadcasted_iota(jnp.int32, (lr, 1), 0).astype(F32)
    return jnp.where(r_iota == lslot, 1.0, 0.0).astype(BF16)


def _local_sort(group, h2b, tri_ref, lr):
    t = group.shape[1]
    g_iota = lax.broadcasted_iota(jnp.int32, (SUBLANES, 1), 0).astype(F32)
    onehot = g_iota == group
    prefix = _dot(jnp.where(onehot, 1.0, 0.0).astype(BF16), tri_ref[...])
    count = prefix[:, t - 1:t]
    padded = jnp.floor((count + (CHUNK_ROWS - 1.0)) * (1.0 / CHUNK_ROWS)) * CHUNK_ROWS
    lslot = jnp.zeros((1, t), F32)
    start = jnp.zeros((1, 1), F32)
    for g in range(N_GROUPS):
        lslot = jnp.where(onehot[g:g + 1], start + prefix[g:g + 1] - 1.0, lslot)
        start = start + padded[g:g + 1]
    return lslot, count, _dot(_sort_matrix(lslot, lr), h2b).astype(BF16)


def _out_kernel(a_ref, y_ref, x_ref, m_ref, wo_ref, bd_ref, g_ref, b_ref, wr_ref, rb_ref, tri_ref,
                x1_ref, h2_ref, lslot_ref, count_ref):
    y2 = _dot(y_ref[0].astype(BF16), bd_ref[...]).astype(BF16)
    o = _dot(a_ref[0], wo_ref[:NA_WIDTH, :]) + _dot(y2, wo_ref[NA_WIDTH:, :])
    z = DEEPNORM_ALPHA * x_ref[0] + m_ref[0, 2:3, :] * o
    x1 = _layer_norm(z) * g_ref[...] + b_ref[...]
    x1_ref[0] = x1
    h2 = (_layer_norm(x1) * (1.0 + m_ref[0, 4:5, :]) + m_ref[0, 3:4, :]).astype(BF16)
    sb = jax.nn.sigmoid(_dot_nt(wr_ref[...], h2)) + rb_ref[...]
    group = _selected_group([sb[e:e + 1] for e in range(N_EXPERTS)])
    lslot, count, h2_sorted = _local_sort(group, h2, tri_ref, h2_ref.shape[0])
    h2_ref[...] = h2_sorted
    lslot_ref[0] = jnp.broadcast_to(lslot, lslot_ref.shape[1:])
    count_ref[0] = jnp.broadcast_to(count, count_ref.shape[1:])


def _out_projection(attn, yf, x, m, w_out_bf16, w_four_bd, ln_g, ln_b, w_router_t, router_bias, tm):
    bsz, length, _ = x.shape
    tiles_per_batch = length // tm
    n_tiles = bsz * tiles_per_batch
    lr = _local_rows(tm)
    tri = _bf16_table(np.triu(np.ones((tm, tm))))
    row = lambda b, i: (b, i, 0)
    tile = lambda b, i: (b * tiles_per_batch + i, 0)
    const2 = lambda b, i: (0, 0)
    return pl.pallas_call(
        _out_kernel,
        grid=(bsz, tiles_per_batch),
        in_specs=[pl.BlockSpec((1, tm, NA_WIDTH), row), pl.BlockSpec((1, tm, F_WIDTH), row),
                  pl.BlockSpec((1, tm, D_MODEL), row),
                  pl.BlockSpec((1, N_MOD, D_MODEL), lambda b, i: (b, 0, 0)),
                  pl.BlockSpec((NA_WIDTH + F_WIDTH, D_MODEL), const2),
                  pl.BlockSpec((F_WIDTH, F_WIDTH), const2),
                  pl.BlockSpec((1, D_MODEL), const2), pl.BlockSpec((1, D_MODEL), const2),
                  pl.BlockSpec((N_EXPERTS, D_MODEL), const2), pl.BlockSpec((N_EXPERTS, 1), const2),
                  pl.BlockSpec((tm, tm), const2)],
        out_specs=[pl.BlockSpec((1, tm, D_MODEL), row),
                   pl.BlockSpec((lr, D_MODEL), tile),
                   pl.BlockSpec((1, SUBLANES, tm), lambda b, i: (b, 0, i)),
                   pl.BlockSpec((1, SUBLANES, LANES), lambda b, i: (b * tiles_per_batch + i, 0, 0))],
        out_shape=[jax.ShapeDtypeStruct((bsz, length, D_MODEL), F32),
                   jax.ShapeDtypeStruct((n_tiles * lr, D_MODEL), BF16),
                   jax.ShapeDtypeStruct((bsz, SUBLANES, length), F32),
                   jax.ShapeDtypeStruct((n_tiles, SUBLANES, LANES), F32)],
        compiler_params=_cparams(2),
        name="out_proj_norm_route",
    )(attn, yf, x, m, w_out_bf16, w_four_bd, ln_g.reshape(1, D_MODEL), ln_b.reshape(1, D_MODEL),
      w_router_t, router_bias.reshape(N_EXPERTS, 1), tri)


def _run_plan(count, tm, lr, tm_slots):
    n_tiles = count.shape[0]
    ch = CHUNK_ROWS
    n_slots = -(-(n_tiles * (tm + N_GROUPS * (ch - 1)) + N_GROUPS * tm_slots) // tm_slots) * tm_slots
    zero_chunk = lr // ch - 1
    pad = ((count + ch - 1) // ch) * ch
    lstart = jnp.cumsum(pad, axis=1) - pad
    run_off = jnp.cumsum(pad, axis=0) - pad
    seg_len = jnp.sum(pad, axis=0)
    seg_pad = ((seg_len + tm_slots - 1) // tm_slots) * tm_slots
    seg_end = jnp.cumsum(seg_pad)
    seg_start = seg_end - seg_pad

    s = (jnp.arange(n_slots // ch, dtype=jnp.int32) * ch)[:, None, None]
    in_run = (seg_start + run_off <= s) & (s < seg_start + run_off + pad)
    src_row = jnp.arange(n_tiles, dtype=jnp.int32)[None, :, None] * lr + lstart + (s - seg_start - run_off)
    src_row = jnp.sum(jnp.where(in_run, src_row, 0), axis=(1, 2))
    src_chunk = jnp.where(jnp.any(in_run, axis=(1, 2)), src_row // ch, zero_chunk).astype(jnp.int32)

    tile_start = jnp.arange(n_slots // tm_slots, dtype=jnp.int32) * tm_slots
    tile_group = jnp.minimum(jnp.sum((seg_end[None, :] <= tile_start[:, None]).astype(jnp.int32), axis=1),
                             N_GROUPS - 1)
    n_used = (seg_end[N_GROUPS - 1:] // tm_slots).astype(jnp.int32)

    r = (jnp.arange(lr // ch, dtype=jnp.int32) * ch)[None, :, None]
    in_local = (lstart[:, None, :] <= r) & (r < (lstart + pad)[:, None, :])
    slot = (seg_start + run_off - lstart)[:, None, :] + r
    back_chunk = (jnp.sum(jnp.where(in_local, slot, 0), axis=2) // ch).astype(jnp.int32)
    return src_chunk, tile_group.astype(jnp.int32), n_used, back_chunk.reshape(-1)


def _chunk_gather(idx_ref, tile, n_chunks, src_hbm, buf, sems, wait):
    slot = tile % 2
    rows = n_chunks * CHUNK_ROWS
    if wait:
        pltpu.make_async_copy(src_hbm.at[pl.ds(0, rows), :], buf.at[slot], sems.at[slot]).wait()
        return
    for c in range(n_chunks):
        src = pl.multiple_of(idx_ref[tile * n_chunks + c] * CHUNK_ROWS, CHUNK_ROWS)
        pltpu.make_async_copy(src_hbm.at[pl.ds(src, CHUNK_ROWS), :],
                              buf.at[slot, pl.ds(c * CHUNK_ROWS, CHUNK_ROWS), :], sems.at[slot]).start()


def _moe_runs_kernel(src_ref, tile_group_ref, n_used_ref, h_hbm, wg32_ref, wu32_ref, wd32_ref, wr_ref, rb_ref,
                     y_ref, hbuf, wg_ref, wu_ref, wd_ref, sems, *, tm):
    step = pl.program_id(0)
    n_used = n_used_ref[0]
    n_chunks = tm // CHUNK_ROWS
    new_group = (step == 0) | (tile_group_ref[step] != tile_group_ref[jnp.maximum(step - 1, 0)])

    @pl.when(new_group & (step < n_used))
    def _():
        wg_ref[...] = wg32_ref[...].astype(BF16)
        wu_ref[...] = wu32_ref[...].astype(BF16)
        wd_ref[...] = wd32_ref[...].astype(BF16)

    @pl.when((step == 0) & (n_used > 0))
    def _():
        _chunk_gather(src_ref, 0, n_chunks, h_hbm, hbuf, sems, wait=False)

    @pl.when(step + 1 < n_used)
    def _():
        _chunk_gather(src_ref, step + 1, n_chunks, h_hbm, hbuf, sems, wait=False)

    @pl.when(step < n_used)
    def _():
        _chunk_gather(src_ref, step, n_chunks, h_hbm, hbuf, sems, wait=True)
        h = hbuf[step % 2]
        s = jax.nn.sigmoid(_dot(h, wr_ref[0]))
        sb = s + rb_ref[0]
        epg = EXPERTS_PER_GROUP
        gates = _top2_gates([s[:, j:j + 1] for j in range(epg)], [sb[:, j:j + 1] for j in range(epg)])
        acc = None
        gate_up = [(_dot(h, wg_ref[0]), _dot(h, wu_ref[0]))]
        for e in range(epg):
            if e + 1 < epg:
                gate_up.append((_dot(h, wg_ref[e + 1]), _dot(h, wu_ref[e + 1])))
            gate, up = gate_up[e]
            hid = (gate * jax.nn.sigmoid(gate)) * up * gates[e]
            y = _dot(hid.astype(BF16), wd_ref[e])
            acc = y if acc is None else acc + y
        y_ref[...] = acc.astype(BF16)

    @pl.when(step >= n_used)
    def _():
        y_ref[...] = jnp.zeros_like(y_ref)


def _moe_runs(h_sorted, src_chunk, tile_group, n_used, layer, w_gate, w_up, w_down, w_router_grp,
              router_bias_grp, tm):
    n_slots = src_chunk.shape[0] * CHUNK_ROWS
    epg = EXPERTS_PER_GROUP
    by_group = lambda i, src_ref, tg_ref, nu_ref: (tg_ref[i], 0, 0)
    by_layer_group = lambda i, src_ref, tg_ref, nu_ref: (layer * N_GROUPS + tg_ref[i], 0, 0)
    grid_spec = pltpu.PrefetchScalarGridSpec(
        num_scalar_prefetch=3,
        grid=(n_slots // tm,),
        in_specs=[pl.BlockSpec(memory_space=pl.ANY),
                  pl.BlockSpec((epg, D_MODEL, D_EXPERT), by_layer_group),
                  pl.BlockSpec((epg, D_MODEL, D_EXPERT), by_layer_group),
                  pl.BlockSpec((epg, D_EXPERT, D_MODEL), by_layer_group),
                  pl.BlockSpec((1, D_MODEL, LANES), by_group),
                  pl.BlockSpec((1, 1, LANES), by_group)],
        out_specs=pl.BlockSpec((tm, D_MODEL), lambda i, src_ref, tg_ref, nu_ref: (i, 0)),
        scratch_shapes=[pltpu.VMEM((2, tm, D_MODEL), BF16),
                        pltpu.VMEM((epg, D_MODEL, D_EXPERT), BF16), pltpu.VMEM((epg, D_MODEL, D_EXPERT), BF16),
                        pltpu.VMEM((epg, D_EXPERT, D_MODEL), BF16), pltpu.SemaphoreType.DMA((2,))],
    )
    return pl.pallas_call(
        functools.partial(_moe_runs_kernel, tm=tm),
        grid_spec=grid_spec,
        out_shape=jax.ShapeDtypeStruct((n_slots, D_MODEL), BF16),
        compiler_params=_cparams(1),
        name="moe_group_experts",
    )(src_chunk, tile_group, n_used, h_sorted, w_gate, w_up, w_down, w_router_grp, router_bias_grp)


def _gather_unsort(back_ref, lslot_ref, y_hbm, ybuf, sems, step, n_steps, lr):
    n_chunks = lr // CHUNK_ROWS

    @pl.when(step == 0)
    def _():
        _chunk_gather(back_ref, 0, n_chunks, y_hbm, ybuf, sems, wait=False)

    @pl.when(step + 1 < n_steps)
    def _():
        _chunk_gather(back_ref, step + 1, n_chunks, y_hbm, ybuf, sems, wait=False)

    _chunk_gather(back_ref, step, n_chunks, y_hbm, ybuf, sems, wait=True)
    sort = _sort_matrix(lslot_ref[0, 0:1, :], lr)
    return lax.dot_general(sort, ybuf[step % 2], (((0,), (0,)), ((), ())), preferred_element_type=F32)


def _residual_norm_kernel(back_ref, y_hbm, lslot_ref, x1_ref, m_ref, g_ref, b_ref, o_ref, ybuf, sems, *, n_steps, lr):
    y = _gather_unsort(back_ref, lslot_ref, y_hbm, ybuf, sems, pl.program_id(0), n_steps, lr)
    z = DEEPNORM_ALPHA * x1_ref[...] + m_ref[0, 5:6, :] * y
    o_ref[...] = _layer_norm(z) * g_ref[...] + b_ref[...]


def _residual_norm(y_slots, back_chunk, lslot, x1, m, ln_g, ln_b, tm):
    bsz, length, _ = x1.shape
    n = bsz * length
    tiles_per_batch = length // tm
    lr = _local_rows(tm)
    const2 = lambda i, back_ref: (0, 0)
    row = pl.BlockSpec((tm, D_MODEL), lambda i, back_ref: (i, 0))
    grid_spec = pltpu.PrefetchScalarGridSpec(
        num_scalar_prefetch=1,
        grid=(n // tm,),
        in_specs=[pl.BlockSpec(memory_space=pl.ANY),
                  pl.BlockSpec((1, SUBLANES, tm),
                               lambda i, back_ref: (i // tiles_per_batch, 0, i % tiles_per_batch)),
                  row, pl.BlockSpec((1, N_MOD, D_MODEL), lambda i, back_ref: (i // tiles_per_batch, 0, 0)),
                  pl.BlockSpec((1, D_MODEL), const2), pl.BlockSpec((1, D_MODEL), const2)],
        out_specs=row,
        scratch_shapes=[pltpu.VMEM((2, lr, D_MODEL), BF16), pltpu.SemaphoreType.DMA((2,))],
    )
    out = pl.pallas_call(
        functools.partial(_residual_norm_kernel, n_steps=n // tm, lr=lr),
        grid_spec=grid_spec,
        out_shape=jax.ShapeDtypeStruct((n, D_MODEL), F32),
        compiler_params=_cparams(1),
        name="moe_residual_norm",
    )(back_chunk, y_slots, lslot, x1.reshape(n, D_MODEL), m, ln_g.reshape(1, D_MODEL), ln_b.reshape(1, D_MODEL))
    return out.reshape(bsz, length, D_MODEL)


def _grouped_moe_runs(h_sorted, count_rows, layer, w_gate, w_up, w_down, w_router_grp, router_bias_grp,
                      tm_tokens, tm_slots):
    count = count_rows[:, :N_GROUPS, 0].astype(jnp.int32)
    src_chunk, tile_group, n_used, back_chunk = _run_plan(count, tm_tokens, _local_rows(tm_tokens), tm_slots)
    y_slots = _moe_runs(h_sorted, src_chunk, tile_group, n_used, layer, w_gate, w_up, w_down, w_router_grp,
                        router_bias_grp, tm_slots)
    return y_slots, back_chunk


def _ctx_fourier_kernel(f_ref, w1_ref, c_ref, s_ref, y_ref):
    ab = _dot(f_ref[0].astype(BF16), w1_ref[...]).astype(BF16)
    y = _dot(c_ref[...], ab[:, :F_WIDTH]) + _dot(s_ref[...], ab[:, F_WIDTH:])
    y_ref[0] = y.astype(BF16)


def _context_fourier(fc):
    bsz, n, _ = fc.shape
    w1 = _bf16_table(_channel_dft_matrix())
    c, s = _dft_cos_sin(n)
    cm = _bf16_table(c * n ** -0.5)
    sm = _bf16_table(-s * n ** -0.5)
    const2 = lambda b: (0, 0)
    blk = pl.BlockSpec((1, n, F_WIDTH), lambda b: (b, 0, 0))
    return pl.pallas_call(
        _ctx_fourier_kernel,
        grid=(bsz,),
        in_specs=[blk, pl.BlockSpec(w1.shape, const2), pl.BlockSpec(cm.shape, const2),
                  pl.BlockSpec(sm.shape, const2)],
        out_specs=blk,
        out_shape=jax.ShapeDtypeStruct((bsz, n, F_WIDTH), BF16),
        compiler_params=_cparams(1),
        name="context_fnet",
    )(fc, w1, cm, sm)


def _block_diag(w):
    g, c, _ = w.shape
    eye = jnp.eye(g, dtype=w.dtype)
    return (eye[:, None, :, None] * w[:, :, None, :]).reshape(g * c, g * c)


def kernel(x, c, ctx, c_ctx, w_mod, b_mod, w_in, rpb, w_four, w_out, ln1_g, ln1_b, ln2_g, ln2_b,
           w_router, router_bias, w_gate, w_up, w_down):
    bsz, length, _ = x.shape
    n_ctx = ctx.shape[1]
    rows = length // GRID_W

    cvec = jnp.concatenate([c, c_ctx[None, :], jnp.zeros((8 - bsz - 1, D_MODEL), F32)], axis=0)
    mods = _modulation(cvec, w_mod, b_mod)
    w_router_t = w_router.T.astype(BF16)
    toeplitz = _rpb_toeplitz(rpb)
    lane_pad = LANES - EXPERTS_PER_GROUP
    w_router_grp = jnp.pad(w_router.reshape(D_MODEL, N_GROUPS, EXPERTS_PER_GROUP).transpose(1, 0, 2),
                           ((0, 0), (0, 0), (0, lane_pad))).astype(BF16)
    router_bias_grp = jnp.pad(router_bias.astype(F32).reshape(N_GROUPS, 1, EXPERTS_PER_GROUP),
                              ((0, 0), (0, 0), (0, lane_pad)))

    wg = w_gate.reshape(DEPTH * N_EXPERTS, D_MODEL, D_EXPERT)
    wu = w_up.reshape(DEPTH * N_EXPERTS, D_MODEL, D_EXPERT)
    wd = w_down.reshape(DEPTH * N_EXPERTS, D_EXPERT, D_MODEL)

    def latent_mod(i):
        return mods[i, :bsz].reshape(bsz, N_MOD, D_MODEL)

    xc = ctx
    projected = None
    for i in range(DEPTH):
        last = i == DEPTH - 1
        m = latent_mod(i)
        mc = jnp.broadcast_to(mods[i, bsz].reshape(1, N_MOD, D_MODEL), (bsz, N_MOD, D_MODEL))
        w_in_b = w_in[i].astype(BF16)
        w_out_b = w_out[i].astype(BF16)
        w_four_bd = _block_diag(w_four[i]).astype(BF16)

        q, k, v, f = projected if projected is not None else _in_projection(x, m, w_in_b, tm=1024)
        qc, kc, vc, fc = _in_projection(xc, mc, w_in_b, tm=n_ctx)

        attn = _neighborhood_attention(q, k, v, kc, vc, toeplitz, i)
        yf = _fourier_positions(f, n_slow=rows, n_fast=GRID_W)
        x1, h2, lslot, cnt = _out_projection(attn, yf, x, m, w_out_b, w_four_bd, ln1_g[i], ln1_b[i],
                                             w_router_t, router_bias, tm=512)
        y, back = _grouped_moe_runs(h2, cnt, i, wg, wu, wd, w_router_grp, router_bias_grp,
                                    tm_tokens=512, tm_slots=512)
        if last:
            return _residual_norm(y, back, lslot, x1, m, ln2_g[i], ln2_b[i], tm=512)
        x, *projected = _norm_in_projection(y, back, lslot, x1, m, ln2_g[i], ln2_b[i], latent_mod(i + 1),
                                            w_in[i + 1].astype(BF16), tm=512)

        attn_c = _context_attention(qc, kc, vc)
        yc = _context_fourier(fc)
        xc1, h2c, lslot_c, cnt_c = _out_projection(attn_c, yc, xc, mc, w_out_b, w_four_bd, ln1_g[i], ln1_b[i],
                                                   w_router_t, router_bias, tm=n_ctx)
        yc2, back_c = _grouped_moe_runs(h2c, cnt_c, i, wg, wu, wd, w_router_grp, router_bias_grp,
                                        tm_tokens=n_ctx, tm_slots=128)
        xc = _residual_norm(yc2, back_c, lslot_c, xc1, mc, ln2_g[i], ln2_b[i], tm=n_ctx)
    return x
```

```python
import functools
import math

import numpy as np
import jax
import jax.numpy as jnp
from jax import lax
from jax.experimental import pallas as pl
from jax.experimental.pallas import tpu as pltpu

D_MODEL = 1024
DEPTH = 2
GRID_W = 64
NA_HEADS = 8
HEAD_DIM = 64
NA_WIDTH = NA_HEADS * HEAD_DIM
WIN_ROWS = 8
WIN_COLS = 16
F_GROUPS = 8
F_GROUP_DIM = 64
F_WIDTH = F_GROUPS * F_GROUP_DIM
IN_WIDTH = 3 * NA_WIDTH + F_WIDTH
N_EXPERTS = 16
N_GROUPS = 4
EXPERTS_PER_GROUP = N_EXPERTS // N_GROUPS
D_EXPERT = 256
N_MOD = 6
DEEPNORM_ALPHA = (2.0 * DEPTH) ** 0.25
LN_EPS = 1e-6

F32 = jnp.float32
BF16 = jnp.bfloat16

V7X_VMEM_BYTES = 64 * 1024 * 1024
VMEM_LIMIT_BYTES = (V7X_VMEM_BYTES * 3) // 4
LANES = 128
SUBLANES = 8
HEADS_PER_STEP = LANES // HEAD_DIM
assert HEADS_PER_STEP == 2
MASK_VALUE = -1e30
LOG2_E = math.log2(math.e)

CHUNK_GRID_ROWS = 4
WINDOW_CHUNKS = 3
STEP_CHUNKS = 8


def _cparams(n_grid_dims):
    return pltpu.CompilerParams(dimension_semantics=("arbitrary",) * n_grid_dims,
                                vmem_limit_bytes=VMEM_LIMIT_BYTES)


def _layer_norm(x):
    mu = jnp.mean(x, axis=-1, keepdims=True)
    xc = x - mu
    var = jnp.mean(xc * xc, axis=-1, keepdims=True)
    return xc * lax.rsqrt(var + LN_EPS)


def _dot(a, b):
    return jnp.dot(a, b, preferred_element_type=F32)


def _dot_nt(a, b):
    return lax.dot_general(a, b, (((1,), (1,)), ((), ())), preferred_element_type=F32)


def _mod_kernel(c_ref, w_ref, b_ref, o_ref):
    c = c_ref[...]
    a = c * jax.nn.sigmoid(c)
    o_ref[0] = jnp.dot(a, w_ref[0], preferred_element_type=F32, precision=lax.Precision.HIGHEST) + b_ref[0]


def _modulation(cvec, w_mod, b_mod):
    n_col_blocks = 4
    wc = (N_MOD * D_MODEL) // n_col_blocks
    rows = cvec.shape[0]
    return pl.pallas_call(
        _mod_kernel,
        grid=(DEPTH, n_col_blocks),
        in_specs=[pl.BlockSpec((rows, D_MODEL), lambda i, j: (0, 0)),
                  pl.BlockSpec((1, D_MODEL, wc), lambda i, j: (i, 0, j)),
                  pl.BlockSpec((1, 1, wc), lambda i, j: (i, 0, j))],
        out_specs=pl.BlockSpec((1, rows, wc), lambda i, j: (i, 0, j)),
        out_shape=jax.ShapeDtypeStruct((DEPTH, rows, N_MOD * D_MODEL), F32),
        compiler_params=_cparams(2),
        name="modulation",
    )(cvec, w_mod, b_mod.reshape(DEPTH, 1, N_MOD * D_MODEL))


def _modulate(x, m_ref):
    return (_layer_norm(x) * (1.0 + m_ref[0, 1:2, :]) + m_ref[0, 0:1, :]).astype(BF16)


def _project(h, rows, w_ref, q_ref, k_ref, v_ref, f_ref):
    p = _dot(h, w_ref[...])
    q_ref[0, rows, :] = (p[:, :NA_WIDTH] * (HEAD_DIM ** -0.5 * LOG2_E)).astype(BF16)
    k_ref[0, rows, :] = p[:, NA_WIDTH:2 * NA_WIDTH].astype(BF16)
    v_ref[0, rows, :] = p[:, 2 * NA_WIDTH:3 * NA_WIDTH].astype(BF16)
    f_ref[0, rows, :] = p[:, 3 * NA_WIDTH:]


def _proj_kernel(x_ref, m_ref, w_ref, q_ref, k_ref, v_ref, f_ref):
    tm = x_ref.shape[1]
    halves = [slice(0, tm // 2), slice(tm // 2, tm)] if tm % 512 == 0 else [slice(0, tm)]
    hs = [_modulate(x_ref[0, rows, :], m_ref) for rows in halves]
    for rows, h in zip(halves, hs):
        _project(h, rows, w_ref, q_ref, k_ref, v_ref, f_ref)


def _norm_proj_kernel(back_ref, y_hbm, lslot_ref, x1_ref, m_ref, g_ref, b_ref, m_next_ref, w_ref,
                      x_ref, q_ref, k_ref, v_ref, f_ref, ybuf, sems, *, steps_per_batch, n_steps, lr):
    step = pl.program_id(0) * steps_per_batch + pl.program_id(1)
    ys = _gather_unsort(back_ref, lslot_ref, y_hbm, ybuf, sems, step, n_steps, lr)
    tm = x1_ref.shape[1] // len(ys)
    hs = []
    for t, y in enumerate(ys):
        rows = slice(t * tm, (t + 1) * tm)
        z = DEEPNORM_ALPHA * x1_ref[0, rows, :] + m_ref[0, 5:6, :] * y
        x = _layer_norm(z) * g_ref[...] + b_ref[...]
        x_ref[0, rows, :] = x
        hs.append(_modulate(x, m_next_ref))
    for t, h in enumerate(hs):
        _project(h, slice(t * tm, (t + 1) * tm), w_ref, q_ref, k_ref, v_ref, f_ref)


def _norm_in_projection(y_slots, back_chunk, lslot, x1, m, ln_g, ln_b, m_next, w_in_bf16, tm):
    bsz, length, _ = x1.shape
    tiles_per_batch = length // tm
    per_step = 2 if tiles_per_batch % 2 == 0 else 1
    steps_per_batch = tiles_per_batch // per_step
    ts = per_step * tm
    lr = _local_rows(tm)
    out = jax.ShapeDtypeStruct((bsz, length, NA_WIDTH), BF16)
    out_f = jax.ShapeDtypeStruct((bsz, length, F_WIDTH), F32)
    out_x = jax.ShapeDtypeStruct((bsz, length, D_MODEL), F32)
    row = lambda b, i, back_ref: (b, i, 0)
    const2 = lambda b, i, back_ref: (0, 0)
    mod_spec = pl.BlockSpec((1, N_MOD, D_MODEL), lambda b, i, back_ref: (b, 0, 0))
    o_spec = pl.BlockSpec((1, ts, NA_WIDTH), row)
    x_spec = pl.BlockSpec((1, ts, D_MODEL), row)
    grid_spec = pltpu.PrefetchScalarGridSpec(
        num_scalar_prefetch=1,
        grid=(bsz, steps_per_batch),
        in_specs=[pl.BlockSpec(memory_space=pl.ANY),
                  pl.BlockSpec((1, SUBLANES, ts), lambda b, i, back_ref: (b, 0, i)),
                  x_spec, mod_spec,
                  pl.BlockSpec((1, D_MODEL), const2), pl.BlockSpec((1, D_MODEL), const2),
                  mod_spec, pl.BlockSpec((D_MODEL, IN_WIDTH), const2)],
        out_specs=[x_spec, o_spec, o_spec, o_spec, o_spec],
        scratch_shapes=[pltpu.VMEM((2, per_step * lr, D_MODEL), BF16), pltpu.SemaphoreType.DMA((2,))],
    )
    return pl.pallas_call(
        functools.partial(_norm_proj_kernel, steps_per_batch=steps_per_batch, n_steps=bsz * steps_per_batch,
                          lr=lr),
        grid_spec=grid_spec,
        out_shape=[out_x, out, out, out, out_f],
        compiler_params=_cparams(2),
        name="moe_norm_in_proj",
    )(back_chunk, y_slots, lslot, x1, m, ln_g.reshape(1, D_MODEL), ln_b.reshape(1, D_MODEL), m_next, w_in_bf16)


def _in_projection(x, m, w_in_bf16, tm):
    bsz, length, _ = x.shape
    out = jax.ShapeDtypeStruct((bsz, length, NA_WIDTH), BF16)
    out_f = jax.ShapeDtypeStruct((bsz, length, F_WIDTH), F32)
    o_spec = pl.BlockSpec((1, tm, NA_WIDTH), lambda b, i: (b, i, 0))
    return pl.pallas_call(
        _proj_kernel,
        grid=(bsz, length // tm),
        in_specs=[pl.BlockSpec((1, tm, D_MODEL), lambda b, i: (b, i, 0)),
                  pl.BlockSpec((1, N_MOD, D_MODEL), lambda b, i: (b, 0, 0)),
                  pl.BlockSpec((D_MODEL, IN_WIDTH), lambda b, i: (0, 0))],
        out_specs=[o_spec, o_spec, o_spec, o_spec],
        out_shape=[out, out, out, out_f],
        compiler_params=_cparams(2),
        name="ln_mod_in_proj",
    )(x, m, w_in_bf16)


def _head_lanes(h):
    lane = lax.broadcasted_iota(jnp.int32, (1, LANES), 1)
    return (lane >= HEAD_DIM * h) & (lane < HEAD_DIM * (h + 1))


def _scores_pass(q, h, tiles, s_ref):
    qh = jnp.where(_head_lanes(h), q, jnp.zeros_like(q))
    m = None
    t = tiles[0][0].shape[0]
    for j, (k, _, bias) in enumerate(tiles):
        s = _dot_nt(qh, k)
        if bias is not None:
            s = s + bias
        s_ref[:, j * t:(j + 1) * t] = s
        mj = jnp.max(s, axis=-1, keepdims=True)
        m = mj if m is None else jnp.maximum(m, mj)
    return m


def _pv_pass(h, tiles, s_ref, m):
    o = None
    t = tiles[0][0].shape[0]
    in_head = _head_lanes(h)
    for j, (_, v, _) in enumerate(tiles):
        p = jnp.exp2(s_ref[:, j * t:(j + 1) * t] - m)
        oj = _dot(p.astype(BF16), jnp.where(in_head, v, jnp.ones_like(v)))
        o = oj if o is None else o + oj
    return o / pltpu.roll(o, HEAD_DIM, axis=1)


def _attention_units(units):
    outs = []
    maxima = [_scores_pass(*units[0])]
    for u in range(len(units)):
        if u + 1 < len(units):
            maxima.append(_scores_pass(*units[u + 1]))
        _, h, tiles, s_ref = units[u]
        outs.append(_pv_pass(h, tiles, s_ref, maxima[u]))
    return outs


def _merge_heads(outs):
    merged = outs[0]
    for h in range(1, len(outs)):
        merged = jnp.where(_head_lanes(h), outs[h], merged)
    return merged


DR_PAD = 2 * WIN_ROWS
DC_PAD = 2 * WIN_COLS


def _toeplitz_kernel(r_ref, sel_ref, mask_ref, o_ref):
    n = r_ref.shape[0]
    for qc in range(GRID_W):
        block = jnp.dot(r_ref[...], sel_ref[qc], preferred_element_type=F32, precision=lax.Precision.HIGHEST)
        o_ref[pl.ds(qc, n, stride=GRID_W), :] = block * LOG2_E + mask_ref[qc]


def _rpb_toeplitz(rpb):
    depth, heads, n_dr, n_dc = rpb.shape
    qc = np.arange(GRID_W)[:, None]
    kc = (np.arange(LANES) % GRID_W)[None, :]
    cs = np.clip(qc - WIN_COLS // 2, 0, GRID_W - WIN_COLS)
    col_valid = (kc >= cs) & (kc < cs + WIN_COLS)
    dc = kc - qc + WIN_COLS - 1
    select = (np.arange(DC_PAD)[None, :, None] == dc[:, None, :]) & col_valid[:, None, :]
    mask = np.where(col_valid, 0.0, MASK_VALUE)[:, None, :]
    r = jnp.pad(rpb.astype(F32), ((0, 0), (0, 0), (0, DR_PAD - n_dr), (0, DC_PAD - n_dc)))
    n = depth * heads * DR_PAD
    const3 = lambda: (0, 0, 0)
    out = pl.pallas_call(
        _toeplitz_kernel,
        grid=(),
        in_specs=[pl.BlockSpec((n, DC_PAD), lambda: (0, 0)),
                  pl.BlockSpec((GRID_W, DC_PAD, LANES), const3), pl.BlockSpec((GRID_W, 1, LANES), const3)],
        out_specs=pl.BlockSpec((n * GRID_W, LANES), lambda: (0, 0)),
        out_shape=jax.ShapeDtypeStruct((n * GRID_W, LANES), F32),
        compiler_params=pltpu.CompilerParams(vmem_limit_bytes=VMEM_LIMIT_BYTES),
        name="rpb_toeplitz",
    )(r.reshape(n, DC_PAD), jnp.asarray(select, F32), jnp.asarray(mask, F32))
    return out.reshape(depth * heads, DR_PAD, GRID_W, LANES)


assert WINDOW_CHUNKS * CHUNK_GRID_ROWS >= CHUNK_GRID_ROWS + WIN_ROWS - 1


def _window_start_chunk(cr, n_cr):
    lo = cr - (WINDOW_CHUNKS - 1) // 2
    if isinstance(cr, int):
        return max(0, min(lo, n_cr - WINDOW_CHUNKS))
    return jnp.clip(lo, 0, n_cr - WINDOW_CHUNKS)


def _window_key_rows(cr, n_cr):
    start = _window_start_chunk(cr, n_cr) * CHUNK_GRID_ROWS
    return list(range(start, start + WINDOW_CHUNKS * CHUNK_GRID_ROWS))


def _row_window(qr, rows):
    kh = min(WIN_ROWS, rows)
    rs = int(np.clip(qr - kh // 2, 0, rows - kh))
    return rs, rs + kh


def _window_variant(cr, n_cr):
    return jnp.where(cr == 0, 0, jnp.where(cr == n_cr - 1, 2, 1))


def _check_windows(rows):
    def relative(cr):
        base = cr * CHUNK_GRID_ROWS
        return ([kr - base for kr in _window_key_rows(cr, n_cr)],
                [tuple(r - base for r in _row_window(base + qi, rows)) for qi in range(CHUNK_GRID_ROWS)])

    n_cr = rows // CHUNK_GRID_ROWS
    for cr in range(n_cr):
        assert cr in (0, n_cr - 1) or relative(cr) == relative(1), cr
        have = set(_window_key_rows(cr, n_cr))
        for qi in range(CHUNK_GRID_ROWS):
            lo, hi = _row_window(cr * CHUNK_GRID_ROWS + qi, rows)
            assert set(range(lo, hi)) <= have, (cr, qi)


def _build_bias_tables(t_ref, bias_scr, rows):
    n_cr = rows // CHUNK_GRID_ROWS
    left = lax.broadcasted_iota(jnp.int32, (GRID_W, LANES), 1) < GRID_W
    masked = jnp.full((GRID_W, LANES), MASK_VALUE, F32)
    for variant, cr in enumerate((0, 1, n_cr - 1)):
        key_rows = _window_key_rows(cr, n_cr)
        for h in range(HEADS_PER_STEP):
            for qi in range(CHUNK_GRID_ROWS):
                qr = cr * CHUNK_GRID_ROWS + qi
                lo, hi = _row_window(qr, rows)
                for p in range(len(key_rows) // 2):
                    pair = [t_ref[h, kr - qr + WIN_ROWS - 1] if lo <= kr < hi else None
                            for kr in key_rows[2 * p:2 * p + 2]]
                    if pair[0] is None and pair[1] is None:
                        block = masked
                    else:
                        block = jnp.where(left, masked if pair[0] is None else pair[0],
                                          masked if pair[1] is None else pair[1])
                    bias_scr[variant, h, qi * GRID_W:(qi + 1) * GRID_W, p * LANES:(p + 1) * LANES] = block


def _na_kernel(q_ref, *refs, rows):
    n_kv = STEP_CHUNKS * WINDOW_CHUNKS
    k_refs, v_refs = refs[:n_kv], refs[n_kv:2 * n_kv]
    kc_ref, vc_ref, t_ref, o_ref, bias_scr, s_scr = refs[2 * n_kv:]
    n_cr = rows // CHUNK_GRID_ROWS
    b, rb = pl.program_id(1), pl.program_id(2)

    @pl.when((b == 0) & (rb == 0))
    def _():
        _build_bias_tables(t_ref, bias_scr, rows)

    tq = CHUNK_GRID_ROWS * GRID_W
    units = []
    for c in range(STEP_CHUNKS):
        variant = _window_variant(rb * STEP_CHUNKS + c, n_cr)
        q = q_ref[0, c * tq:(c + 1) * tq, :]
        for h in range(HEADS_PER_STEP):
            lat = [(k_refs[c * WINDOW_CHUNKS + j][0], v_refs[c * WINDOW_CHUNKS + j][0],
                    bias_scr[variant, h, :, j * tq:(j + 1) * tq]) for j in range(WINDOW_CHUNKS)]
            units.append((q, h, lat + [(kc_ref[0], vc_ref[0], None)], s_scr.at[c, h]))
    outs = _attention_units(units)
    for c in range(STEP_CHUNKS):
        o = _merge_heads(outs[c * HEADS_PER_STEP:(c + 1) * HEADS_PER_STEP])
        o_ref[0, c * tq:(c + 1) * tq, :] = o.astype(BF16)


def _neighborhood_attention(q, k, v, kc, vc, toeplitz, layer):
    bsz, length, _ = q.shape
    rows = length // GRID_W
    n_cr = rows // CHUNK_GRID_ROWS
    assert rows % (CHUNK_GRID_ROWS * STEP_CHUNKS) == 0 and n_cr >= WINDOW_CHUNKS + 2
    _check_windows(rows)
    n_ctx = kc.shape[1]
    tq = CHUNK_GRID_ROWS * GRID_W
    assert n_ctx == tq

    def kv_spec(c, j):
        return pl.BlockSpec((1, tq, LANES),
                            lambda hp, b, rb: (b, _window_start_chunk(rb * STEP_CHUNKS + c, n_cr) + j, hp))

    kv_specs = [kv_spec(c, j) for c in range(STEP_CHUNKS) for j in range(WINDOW_CHUNKS)]
    ctx_spec = pl.BlockSpec((1, n_ctx, LANES), lambda hp, b, rb: (b, 0, hp))
    q_spec = pl.BlockSpec((1, STEP_CHUNKS * tq, LANES), lambda hp, b, rb: (b, rb, hp))
    window = WINDOW_CHUNKS * tq
    return pl.pallas_call(
        functools.partial(_na_kernel, rows=rows),
        grid=(NA_HEADS // HEADS_PER_STEP, bsz, n_cr // STEP_CHUNKS),
        in_specs=([q_spec] + kv_specs + kv_specs
                  + [ctx_spec, ctx_spec,
                     pl.BlockSpec((HEADS_PER_STEP, DR_PAD, GRID_W, LANES),
                                  lambda hp, b, rb: (layer * (NA_HEADS // HEADS_PER_STEP) + hp, 0, 0, 0))]),
        out_specs=q_spec,
        out_shape=jax.ShapeDtypeStruct((bsz, length, NA_WIDTH), BF16),
        scratch_shapes=[pltpu.VMEM((3, HEADS_PER_STEP, tq, window), F32),
                        pltpu.VMEM((STEP_CHUNKS, HEADS_PER_STEP, tq, window + n_ctx), F32)],
        compiler_params=_cparams(3),
        name="neighborhood_attention",
    )(q, *([k] * len(kv_specs)), *([v] * len(kv_specs)), kc, vc, toeplitz)


def _ctx_attn_kernel(q_ref, k_ref, v_ref, o_ref, s_scr):
    tiles = [(k_ref[0], v_ref[0], None)]
    outs = _attention_units([(q_ref[0], h, tiles, s_scr.at[h]) for h in range(HEADS_PER_STEP)])
    o_ref[0] = _merge_heads(outs).astype(BF16)


def _context_attention(qc, kc, vc):
    bsz, n_ctx, _ = qc.shape
    spec = pl.BlockSpec((1, n_ctx, LANES), lambda b, hp: (b, 0, hp))
    return pl.pallas_call(
        _ctx_attn_kernel,
        grid=(bsz, NA_HEADS // HEADS_PER_STEP),
        in_specs=[spec, spec, spec],
        out_specs=spec,
        out_shape=jax.ShapeDtypeStruct((bsz, n_ctx, NA_WIDTH), BF16),
        scratch_shapes=[pltpu.VMEM((HEADS_PER_STEP, n_ctx, n_ctx), F32)],
        compiler_params=_cparams(2),
        name="context_attention",
    )(qc, kc, vc)


def _dft_cos_sin(n):
    ang = 2.0 * np.pi * np.outer(np.arange(n), np.arange(n)) / n
    return np.cos(ang), np.sin(ang)


def _bf16_table(a):
    return jnp.asarray(a, F32).astype(BF16)


def _channel_dft_matrix(n_groups=F_GROUPS):
    c, s = _dft_cos_sin(F_GROUP_DIM)
    scale = F_GROUP_DIM ** -0.5
    eye = np.eye(n_groups)
    return np.concatenate([np.kron(eye, c), np.kron(eye, s)], axis=1) * scale


def _fft_stage1_kernel(f_ref, perm_ref, w1_ref, cs_ref, sc_ref, tc_ref, ts_ref, zr_ref, zi_ref, *, n_slow, nt):
    x = f_ref[0].reshape(n_slow * nt, F_WIDTH).astype(BF16)
    x = _dot(perm_ref[...], x).astype(BF16)
    ab = [_dot(x[:, p * LANES:(p + 1) * LANES], w1_ref[...]).astype(BF16) for p in range(F_WIDTH // LANES)]
    a_all = jnp.concatenate([blk[:, :LANES] for blk in ab], axis=-1)
    b_all = jnp.concatenate([blk[:, LANES:] for blk in ab], axis=-1)
    for t in range(nt):
        a = a_all[t * n_slow:(t + 1) * n_slow]
        b = b_all[t * n_slow:(t + 1) * n_slow]
        z = _dot(cs_ref[...], a) + _dot(sc_ref[...], b)
        zr, zi = z[:n_slow], z[n_slow:]
        c, s = tc_ref[t], ts_ref[t]
        zr_ref[0, t] = (zr * c - zi * s).astype(BF16)
        zi_ref[0, t] = (zr * s + zi * c).astype(BF16)


def _fft_stage2_kernel(zr_ref, zi_ref, f_ref, y_ref, *, n_fast, kb):
    rhs = jnp.concatenate([zr_ref[0].reshape(n_fast * kb, F_WIDTH),
                           zi_ref[0].reshape(n_fast * kb, F_WIDTH)], axis=0)
    y_ref[0] = _dot(f_ref[...], rhs).astype(BF16).reshape(n_fast, kb, F_WIDTH)


def _fourier_positions(f, n_slow, n_fast):
    bsz, n, _ = f.shape
    assert n == n_slow * n_fast
    nt = SUBLANES
    kb = 2 * SUBLANES
    w1 = _bf16_table(_channel_dft_matrix(LANES // F_GROUP_DIM))
    perm = _bf16_table(np.eye(n_slow * nt).reshape(n_slow, nt, n_slow * nt).transpose(1, 0, 2)
                       .reshape(n_slow * nt, n_slow * nt))
    c1, s1 = _dft_cos_sin(n_slow)
    sc1 = n_slow ** -0.5
    cs = _bf16_table(np.concatenate([c1, s1], axis=0) * sc1)
    sc = _bf16_table(np.concatenate([-s1, c1], axis=0) * sc1)
    tw = 2.0 * np.pi * np.outer(np.arange(n_fast), np.arange(n_slow)) / n
    tc = jnp.asarray(np.cos(tw)[:, :, None], F32)
    ts = jnp.asarray(np.sin(tw)[:, :, None], F32)
    z_shape = jax.ShapeDtypeStruct((bsz, n_fast, n_slow, F_WIDTH), BF16)
    z_spec = pl.BlockSpec((1, nt, n_slow, F_WIDTH), lambda b, j: (b, j, 0, 0))
    const2 = lambda b, j: (0, 0)
    tw_spec = pl.BlockSpec((nt, n_slow, 1), lambda b, j: (j, 0, 0))
    zr, zi = pl.pallas_call(
        functools.partial(_fft_stage1_kernel, n_slow=n_slow, nt=nt),
        grid=(bsz, n_fast // nt),
        in_specs=[pl.BlockSpec((1, n_slow, nt, F_WIDTH), lambda b, j: (b, 0, j, 0)),
                  pl.BlockSpec(perm.shape, const2),
                  pl.BlockSpec(w1.shape, const2), pl.BlockSpec(cs.shape, const2), pl.BlockSpec(sc.shape, const2),
                  tw_spec, tw_spec],
        out_specs=[z_spec, z_spec],
        out_shape=[z_shape, z_shape],
        compiler_params=_cparams(2),
        name="fnet_stage1",
    )(f.reshape(bsz, n_slow, n_fast, F_WIDTH), perm, w1, cs, sc, tc, ts)

    c2, s2 = _dft_cos_sin(n_fast)
    sc2 = n_fast ** -0.5
    eye = np.eye(kb)
    f2 = _bf16_table(np.concatenate([np.kron(c2, eye), np.kron(-s2, eye)], axis=1) * sc2)
    blk = pl.BlockSpec((1, n_fast, kb, F_WIDTH), lambda b, j: (b, 0, j, 0))
    y = pl.pallas_call(
        functools.partial(_fft_stage2_kernel, n_fast=n_fast, kb=kb),
        grid=(bsz, n_slow // kb),
        in_specs=[blk, blk, pl.BlockSpec(f2.shape, const2)],
        out_specs=blk,
        out_shape=jax.ShapeDtypeStruct((bsz, n_fast, n_slow, F_WIDTH), BF16),
        compiler_params=_cparams(2),
        name="fnet_stage2",
    )(zr, zi, f2)
    return y.reshape(bsz, n, F_WIDTH)


def _second_largest_sum(a, b, c, d):
    mab, nab = jnp.maximum(a, b), jnp.minimum(a, b)
    mcd, ncd = jnp.maximum(c, d), jnp.minimum(c, d)
    return jnp.maximum(mab, mcd) + jnp.maximum(jnp.minimum(mab, mcd), jnp.maximum(nab, ncd))


def _selected_group(sb_rows):
    epg = EXPERTS_PER_GROUP
    g_score = [_second_largest_sum(*sb_rows[g * epg:(g + 1) * epg]) for g in range(N_GROUPS)]
    best = functools.reduce(jnp.maximum, g_score)
    group = jnp.full_like(best, float(N_GROUPS - 1))
    for g in range(N_GROUPS - 2, -1, -1):
        group = jnp.where(g_score[g] == best, float(g), group)
    return group


def _top2_gates(cand_s, cand_sb):
    n = len(cand_s)
    w = []
    for j in range(n):
        rank = jnp.zeros_like(cand_sb[j])
        for i in range(n):
            if i == j:
                continue
            ahead = (cand_sb[i] > cand_sb[j]) | ((cand_sb[i] == cand_sb[j]) & (i < j))
            rank = rank + jnp.where(ahead, 1.0, 0.0)
        w.append(jnp.where(rank < 2.0, cand_s[j], 0.0))
    total = functools.reduce(jnp.add, w)
    return [wj / total for wj in w]


CHUNK_ROWS = 16


def _local_rows(tm):
    need = tm + N_GROUPS * (CHUNK_ROWS - 1) + CHUNK_ROWS
    return -(-need // CHUNK_ROWS) * CHUNK_ROWS


def _sort_matrix(lslot, lr):
    r_iota = lax.broadcasted_iota(jnp.int32, (lr, 1), 0).astype(F32)
    return jnp.where(r_iota == lslot, 1.0, 0.0).astype(BF16)


def _local_sort(group, h2b, tri_ref, lr):
    t = group.shape[1]
    g_iota = lax.broadcasted_iota(jnp.int32, (SUBLANES, 1), 0).astype(F32)
    onehot = g_iota == group
    prefix = _dot(jnp.where(onehot, 1.0, 0.0).astype(BF16), tri_ref[...])
    count = prefix[:, t - 1:t]
    padded = jnp.floor((count + (CHUNK_ROWS - 1.0)) * (1.0 / CHUNK_ROWS)) * CHUNK_ROWS
    lslot = jnp.zeros((1, t), F32)
    start = jnp.zeros((1, 1), F32)
    for g in range(N_GROUPS):
        lslot = jnp.where(onehot[g:g + 1], start + prefix[g:g + 1] - 1.0, lslot)
        start = start + padded[g:g + 1]
    return lslot, count, _dot(_sort_matrix(lslot, lr), h2b).astype(BF16)


def _out_kernel(a_ref, y_ref, x_ref, m_ref, wo_ref, bd_ref, g_ref, b_ref, wr_ref, rb_ref, tri_ref,
                x1_ref, h2_ref, lslot_ref, count_ref):
    tm = tri_ref.shape[0]
    lr = h2_ref.shape[0] // count_ref.shape[0]

    def project(s):
        rows = slice(s * tm, (s + 1) * tm)
        y2 = _dot(y_ref[0, rows, :].astype(BF16), bd_ref[...]).astype(BF16)
        return _dot(a_ref[0, rows, :], wo_ref[:NA_WIDTH, :]) + _dot(y2, wo_ref[NA_WIDTH:, :])

    def finish(s, o):
        rows = slice(s * tm, (s + 1) * tm)
        z = DEEPNORM_ALPHA * x_ref[0, rows, :] + m_ref[0, 2:3, :] * o
        x1 = _layer_norm(z) * g_ref[...] + b_ref[...]
        x1_ref[0, rows, :] = x1
        h2 = (_layer_norm(x1) * (1.0 + m_ref[0, 4:5, :]) + m_ref[0, 3:4, :]).astype(BF16)
        sb = jax.nn.sigmoid(_dot_nt(wr_ref[...], h2)) + rb_ref[...]
        group = _selected_group([sb[e:e + 1] for e in range(N_EXPERTS)])
        lslot, count, h2_sorted = _local_sort(group, h2, tri_ref, lr)
        h2_ref[s * lr:(s + 1) * lr, :] = h2_sorted
        lslot_ref[0, :, rows] = jnp.broadcast_to(lslot, (lslot_ref.shape[1], tm))
        count_ref[s] = jnp.broadcast_to(count, count_ref.shape[1:])

    n_sub = count_ref.shape[0]
    projected = project(0)
    for s in range(n_sub):
        following = project(s + 1) if s + 1 < n_sub else None
        finish(s, projected)
        projected = following


def _out_projection(attn, yf, x, m, w_out_bf16, w_four_bd, ln_g, ln_b, w_router_t, router_bias, tm):
    bsz, length, _ = x.shape
    tiles_per_batch = length // tm
    n_tiles = bsz * tiles_per_batch
    per_step = 2 if tiles_per_batch % 2 == 0 else 1
    steps_per_batch = tiles_per_batch // per_step
    ts = per_step * tm
    lr = _local_rows(tm)
    tri = _bf16_table(np.triu(np.ones((tm, tm))))
    row = lambda b, i: (b, i, 0)
    const2 = lambda b, i: (0, 0)
    return pl.pallas_call(
        _out_kernel,
        grid=(bsz, steps_per_batch),
        in_specs=[pl.BlockSpec((1, ts, NA_WIDTH), row), pl.BlockSpec((1, ts, F_WIDTH), row),
                  pl.BlockSpec((1, ts, D_MODEL), row),
                  pl.BlockSpec((1, N_MOD, D_MODEL), lambda b, i: (b, 0, 0)),
                  pl.BlockSpec((NA_WIDTH + F_WIDTH, D_MODEL), const2),
                  pl.BlockSpec((F_WIDTH, F_WIDTH), const2),
                  pl.BlockSpec((1, D_MODEL), const2), pl.BlockSpec((1, D_MODEL), const2),
                  pl.BlockSpec((N_EXPERTS, D_MODEL), const2), pl.BlockSpec((N_EXPERTS, 1), const2),
                  pl.BlockSpec((tm, tm), const2)],
        out_specs=[pl.BlockSpec((1, ts, D_MODEL), row),
                   pl.BlockSpec((per_step * lr, D_MODEL), lambda b, i: (b * steps_per_batch + i, 0)),
                   pl.BlockSpec((1, SUBLANES, ts), lambda b, i: (b, 0, i)),
                   pl.BlockSpec((per_step, SUBLANES, LANES), lambda b, i: (b * steps_per_batch + i, 0, 0))],
        out_shape=[jax.ShapeDtypeStruct((bsz, length, D_MODEL), F32),
                   jax.ShapeDtypeStruct((n_tiles * lr, D_MODEL), BF16),
                   jax.ShapeDtypeStruct((bsz, SUBLANES, length), F32),
                   jax.ShapeDtypeStruct((n_tiles, SUBLANES, LANES), F32)],
        compiler_params=_cparams(2),
        name="out_proj_norm_route",
    )(attn, yf, x, m, w_out_bf16, w_four_bd, ln_g.reshape(1, D_MODEL), ln_b.reshape(1, D_MODEL),
      w_router_t, router_bias.reshape(N_EXPERTS, 1), tri)


def _run_plan(count, tm, lr, tm_slots):
    n_tiles = count.shape[0]
    ch = CHUNK_ROWS
    n_slots = -(-(n_tiles * (tm + N_GROUPS * (ch - 1)) + N_GROUPS * tm_slots) // tm_slots) * tm_slots
    zero_chunk = lr // ch - 1
    pad = ((count + ch - 1) // ch) * ch
    lstart = jnp.cumsum(pad, axis=1) - pad
    run_off = jnp.cumsum(pad, axis=0) - pad
    seg_len = jnp.sum(pad, axis=0)
    seg_pad = ((seg_len + tm_slots - 1) // tm_slots) * tm_slots
    seg_end = jnp.cumsum(seg_pad)
    seg_start = seg_end - seg_pad

    s = (jnp.arange(n_slots // ch, dtype=jnp.int32) * ch)[:, None, None]
    in_run = (seg_start + run_off <= s) & (s < seg_start + run_off + pad)
    src_row = jnp.arange(n_tiles, dtype=jnp.int32)[None, :, None] * lr + lstart + (s - seg_start - run_off)
    src_row = jnp.sum(jnp.where(in_run, src_row, 0), axis=(1, 2))
    src_chunk = jnp.where(jnp.any(in_run, axis=(1, 2)), src_row // ch, zero_chunk).astype(jnp.int32)

    tile_start = jnp.arange(n_slots // tm_slots, dtype=jnp.int32) * tm_slots
    tile_group = jnp.minimum(jnp.sum((seg_end[None, :] <= tile_start[:, None]).astype(jnp.int32), axis=1),
                             N_GROUPS - 1)
    n_used = (seg_end[N_GROUPS - 1:] // tm_slots).astype(jnp.int32)

    r = (jnp.arange(lr // ch, dtype=jnp.int32) * ch)[None, :, None]
    in_local = (lstart[:, None, :] <= r) & (r < (lstart + pad)[:, None, :])
    slot = (seg_start + run_off - lstart)[:, None, :] + r
    back_chunk = (jnp.sum(jnp.where(in_local, slot, 0), axis=2) // ch).astype(jnp.int32)
    return src_chunk, tile_group.astype(jnp.int32), n_used, back_chunk.reshape(-1)


def _chunk_gather(idx_ref, tile, n_chunks, src_hbm, buf, sems, wait):
    slot = tile % 2
    rows = n_chunks * CHUNK_ROWS
    if wait:
        pltpu.make_async_copy(src_hbm.at[pl.ds(0, rows), :], buf.at[slot], sems.at[slot]).wait()
        return
    for c in range(n_chunks):
        src = pl.multiple_of(idx_ref[tile * n_chunks + c] * CHUNK_ROWS, CHUNK_ROWS)
        pltpu.make_async_copy(src_hbm.at[pl.ds(src, CHUNK_ROWS), :],
                              buf.at[slot, pl.ds(c * CHUNK_ROWS, CHUNK_ROWS), :], sems.at[slot]).start()


def _moe_runs_kernel(src_ref, tile_group_ref, n_used_ref, h_hbm, wg32_ref, wu32_ref, wd32_ref, wr_ref, rb_ref,
                     y_ref, hbuf, wg_ref, wu_ref, wd_ref, sems, *, tm):
    step = pl.program_id(0)
    n_used = n_used_ref[0]
    n_chunks = tm // CHUNK_ROWS
    new_group = (step == 0) | (tile_group_ref[step] != tile_group_ref[jnp.maximum(step - 1, 0)])

    @pl.when(new_group & (step < n_used))
    def _():
        wg_ref[...] = wg32_ref[...].astype(BF16)
        wu_ref[...] = wu32_ref[...].astype(BF16)
        wd_ref[...] = wd32_ref[...].astype(BF16)

    @pl.when((step == 0) & (n_used > 0))
    def _():
        _chunk_gather(src_ref, 0, n_chunks, h_hbm, hbuf, sems, wait=False)

    @pl.when(step + 1 < n_used)
    def _():
        _chunk_gather(src_ref, step + 1, n_chunks, h_hbm, hbuf, sems, wait=False)

    @pl.when(step < n_used)
    def _():
        _chunk_gather(src_ref, step, n_chunks, h_hbm, hbuf, sems, wait=True)
        h = hbuf[step % 2]
        s = jax.nn.sigmoid(_dot(h, wr_ref[0]))
        sb = s + rb_ref[0]
        epg = EXPERTS_PER_GROUP
        gates = _top2_gates([s[:, j:j + 1] for j in range(epg)], [sb[:, j:j + 1] for j in range(epg)])
        acc = None
        gate_up = [(_dot(h, wg_ref[0]), _dot(h, wu_ref[0]))]
        for e in range(epg):
            if e + 1 < epg:
                gate_up.append((_dot(h, wg_ref[e + 1]), _dot(h, wu_ref[e + 1])))
            gate, up = gate_up[e]
            hid = (gate * jax.nn.sigmoid(gate)) * up * gates[e]
            y = _dot(hid.astype(BF16), wd_ref[e])
            acc = y if acc is None else acc + y
        y_ref[...] = acc.astype(BF16)

    @pl.when(step >= n_used)
    def _():
        y_ref[...] = jnp.zeros_like(y_ref)


def _moe_runs(h_sorted, src_chunk, tile_group, n_used, layer, w_gate, w_up, w_down, w_router_grp,
              router_bias_grp, tm):
    n_slots = src_chunk.shape[0] * CHUNK_ROWS
    epg = EXPERTS_PER_GROUP
    by_group = lambda i, src_ref, tg_ref, nu_ref: (tg_ref[i], 0, 0)
    by_layer_group = lambda i, src_ref, tg_ref, nu_ref: (layer * N_GROUPS + tg_ref[i], 0, 0)
    grid_spec = pltpu.PrefetchScalarGridSpec(
        num_scalar_prefetch=3,
        grid=(n_slots // tm,),
        in_specs=[pl.BlockSpec(memory_space=pl.ANY),
                  pl.BlockSpec((epg, D_MODEL, D_EXPERT), by_layer_group),
                  pl.BlockSpec((epg, D_MODEL, D_EXPERT), by_layer_group),
                  pl.BlockSpec((epg, D_EXPERT, D_MODEL), by_layer_group),
                  pl.BlockSpec((1, D_MODEL, LANES), by_group),
                  pl.BlockSpec((1, 1, LANES), by_group)],
        out_specs=pl.BlockSpec((tm, D_MODEL), lambda i, src_ref, tg_ref, nu_ref: (i, 0)),
        scratch_shapes=[pltpu.VMEM((2, tm, D_MODEL), BF16),
                        pltpu.VMEM((epg, D_MODEL, D_EXPERT), BF16), pltpu.VMEM((epg, D_MODEL, D_EXPERT), BF16),
                        pltpu.VMEM((epg, D_EXPERT, D_MODEL), BF16), pltpu.SemaphoreType.DMA((2,))],
    )
    return pl.pallas_call(
        functools.partial(_moe_runs_kernel, tm=tm),
        grid_spec=grid_spec,
        out_shape=jax.ShapeDtypeStruct((n_slots, D_MODEL), BF16),
        compiler_params=_cparams(1),
        name="moe_group_experts",
    )(src_chunk, tile_group, n_used, h_sorted, w_gate, w_up, w_down, w_router_grp, router_bias_grp)


def _gather_unsort(back_ref, lslot_ref, y_hbm, ybuf, sems, step, n_steps, lr):
    per_step = ybuf.shape[1] // lr
    tm = lslot_ref.shape[2] // per_step
    n_chunks = per_step * lr // CHUNK_ROWS

    @pl.when(step == 0)
    def _():
        _chunk_gather(back_ref, 0, n_chunks, y_hbm, ybuf, sems, wait=False)

    @pl.when(step + 1 < n_steps)
    def _():
        _chunk_gather(back_ref, step + 1, n_chunks, y_hbm, ybuf, sems, wait=False)

    _chunk_gather(back_ref, step, n_chunks, y_hbm, ybuf, sems, wait=True)
    outs = []
    for t in range(per_step):
        sort = _sort_matrix(lslot_ref[0, 0:1, t * tm:(t + 1) * tm], lr)
        outs.append(lax.dot_general(sort, ybuf[step % 2, t * lr:(t + 1) * lr, :], (((0,), (0,)), ((), ())),
                                    preferred_element_type=F32))
    return outs


def _residual_norm_kernel(back_ref, y_hbm, lslot_ref, x1_ref, m_ref, g_ref, b_ref, o_ref, ybuf, sems, *, n_steps, lr):
    y, = _gather_unsort(back_ref, lslot_ref, y_hbm, ybuf, sems, pl.program_id(0), n_steps, lr)
    z = DEEPNORM_ALPHA * x1_ref[...] + m_ref[0, 5:6, :] * y
    o_ref[...] = _layer_norm(z) * g_ref[...] + b_ref[...]


def _residual_norm(y_slots, back_chunk, lslot, x1, m, ln_g, ln_b, tm):
    bsz, length, _ = x1.shape
    n = bsz * length
    tiles_per_batch = length // tm
    lr = _local_rows(tm)
    const2 = lambda i, back_ref: (0, 0)
    row = pl.BlockSpec((tm, D_MODEL), lambda i, back_ref: (i, 0))
    grid_spec = pltpu.PrefetchScalarGridSpec(
        num_scalar_prefetch=1,
        grid=(n // tm,),
        in_specs=[pl.BlockSpec(memory_space=pl.ANY),
                  pl.BlockSpec((1, SUBLANES, tm),
                               lambda i, back_ref: (i // tiles_per_batch, 0, i % tiles_per_batch)),
                  row, pl.BlockSpec((1, N_MOD, D_MODEL), lambda i, back_ref: (i // tiles_per_batch, 0, 0)),
                  pl.BlockSpec((1, D_MODEL), const2), pl.BlockSpec((1, D_MODEL), const2)],
        out_specs=row,
        scratch_shapes=[pltpu.VMEM((2, lr, D_MODEL), BF16), pltpu.SemaphoreType.DMA((2,))],
    )
    out = pl.pallas_call(
        functools.partial(_residual_norm_kernel, n_steps=n // tm, lr=lr),
        grid_spec=grid_spec,
        out_shape=jax.ShapeDtypeStruct((n, D_MODEL), F32),
        compiler_params=_cparams(1),
        name="moe_residual_norm",
    )(back_chunk, y_slots, lslot, x1.reshape(n, D_MODEL), m, ln_g.reshape(1, D_MODEL), ln_b.reshape(1, D_MODEL))
    return out.reshape(bsz, length, D_MODEL)


def _grouped_moe_runs(h_sorted, count_rows, layer, w_gate, w_up, w_down, w_router_grp, router_bias_grp,
                      tm_tokens, tm_slots):
    count = count_rows[:, :N_GROUPS, 0].astype(jnp.int32)
    src_chunk, tile_group, n_used, back_chunk = _run_plan(count, tm_tokens, _local_rows(tm_tokens), tm_slots)
    y_slots = _moe_runs(h_sorted, src_chunk, tile_group, n_used, layer, w_gate, w_up, w_down, w_router_grp,
                        router_bias_grp, tm_slots)
    return y_slots, back_chunk


def _ctx_fourier_kernel(f_ref, w1_ref, c_ref, s_ref, y_ref):
    ab = _dot(f_ref[0].astype(BF16), w1_ref[...]).astype(BF16)
    y = _dot(c_ref[...], ab[:, :F_WIDTH]) + _dot(s_ref[...], ab[:, F_WIDTH:])
    y_ref[0] = y.astype(BF16)


def _context_fourier(fc):
    bsz, n, _ = fc.shape
    w1 = _bf16_table(_channel_dft_matrix())
    c, s = _dft_cos_sin(n)
    cm = _bf16_table(c * n ** -0.5)
    sm = _bf16_table(-s * n ** -0.5)
    const2 = lambda b: (0, 0)
    blk = pl.BlockSpec((1, n, F_WIDTH), lambda b: (b, 0, 0))
    return pl.pallas_call(
        _ctx_fourier_kernel,
        grid=(bsz,),
        in_specs=[blk, pl.BlockSpec(w1.shape, const2), pl.BlockSpec(cm.shape, const2),
                  pl.BlockSpec(sm.shape, const2)],
        out_specs=blk,
        out_shape=jax.ShapeDtypeStruct((bsz, n, F_WIDTH), BF16),
        compiler_params=_cparams(1),
        name="context_fnet",
    )(fc, w1, cm, sm)


def _block_diag(w):
    g, c, _ = w.shape
    eye = jnp.eye(g, dtype=w.dtype)
    return (eye[:, None, :, None] * w[:, :, None, :]).reshape(g * c, g * c)


def kernel(x, c, ctx, c_ctx, w_mod, b_mod, w_in, rpb, w_four, w_out, ln1_g, ln1_b, ln2_g, ln2_b,
           w_router, router_bias, w_gate, w_up, w_down):
    bsz, length, _ = x.shape
    n_ctx = ctx.shape[1]
    rows = length // GRID_W

    cvec = jnp.concatenate([c, c_ctx[None, :], jnp.zeros((8 - bsz - 1, D_MODEL), F32)], axis=0)
    mods = _modulation(cvec, w_mod, b_mod)
    w_router_t = w_router.T.astype(BF16)
    toeplitz = _rpb_toeplitz(rpb)
    lane_pad = LANES - EXPERTS_PER_GROUP
    w_router_grp = jnp.pad(w_router.reshape(D_MODEL, N_GROUPS, EXPERTS_PER_GROUP).transpose(1, 0, 2),
                           ((0, 0), (0, 0), (0, lane_pad))).astype(BF16)
    router_bias_grp = jnp.pad(router_bias.astype(F32).reshape(N_GROUPS, 1, EXPERTS_PER_GROUP),
                              ((0, 0), (0, 0), (0, lane_pad)))

    wg = w_gate.reshape(DEPTH * N_EXPERTS, D_MODEL, D_EXPERT)
    wu = w_up.reshape(DEPTH * N_EXPERTS, D_MODEL, D_EXPERT)
    wd = w_down.reshape(DEPTH * N_EXPERTS, D_EXPERT, D_MODEL)

    def latent_mod(i):
        return mods[i, :bsz].reshape(bsz, N_MOD, D_MODEL)

    xc = ctx
    projected = None
    for i in range(DEPTH):
        last = i == DEPTH - 1
        m = latent_mod(i)
        mc = jnp.broadcast_to(mods[i, bsz].reshape(1, N_MOD, D_MODEL), (bsz, N_MOD, D_MODEL))
        w_in_b = w_in[i].astype(BF16)
        w_out_b = w_out[i].astype(BF16)
        w_four_bd = _block_diag(w_four[i]).astype(BF16)

        q, k, v, f = projected if projected is not None else _in_projection(x, m, w_in_b, tm=1024)
        qc, kc, vc, fc = _in_projection(xc, mc, w_in_b, tm=n_ctx)

        attn = _neighborhood_attention(q, k, v, kc, vc, toeplitz, i)
        yf = _fourier_positions(f, n_slow=rows, n_fast=GRID_W)
        x1, h2, lslot, cnt = _out_projection(attn, yf, x, m, w_out_b, w_four_bd, ln1_g[i], ln1_b[i],
                                             w_router_t, router_bias, tm=512)
        y, back = _grouped_moe_runs(h2, cnt, i, wg, wu, wd, w_router_grp, router_bias_grp,
                                    tm_tokens=512, tm_slots=512)
        if last:
            return _residual_norm(y, back, lslot, x1, m, ln2_g[i], ln2_b[i], tm=512)
        x, *projected = _norm_in_projection(y, back, lslot, x1, m, ln2_g[i], ln2_b[i], latent_mod(i + 1),
                                            w_in[i + 1].astype(BF16), tm=512)

        attn_c = _context_attention(qc, kc, vc)
        yc = _context_fourier(fc)
        xc1, h2c, lslot_c, cnt_c = _out_projection(attn_c, yc, xc, mc, w_out_b, w_four_bd, ln1_g[i], ln1_b[i],
                                                   w_router_t, router_bias, tm=n_ctx)
        yc2, back_c = _grouped_moe_runs(h2c, cnt_c, i, wg, wu, wd, w_router_grp, router_bias_grp,
                                        tm_tokens=n_ctx, tm_slots=128)
        xc = _residual_norm(yc2, back_c, lslot_c, xc1, mc, ln2_g[i], ln2_b[i], tm=n_ctx)
    return x
```

```python
import functools
import math

import numpy as np
import jax
import jax.numpy as jnp
from jax import lax
from jax.experimental import pallas as pl
from jax.experimental.pallas import tpu as pltpu

D_MODEL = 1024
DEPTH = 2
GRID_W = 64
NA_HEADS = 8
HEAD_DIM = 64
NA_WIDTH = NA_HEADS * HEAD_DIM
WIN_ROWS = 8
WIN_COLS = 16
F_GROUPS = 8
F_GROUP_DIM = 64
F_WIDTH = F_GROUPS * F_GROUP_DIM
IN_WIDTH = 3 * NA_WIDTH + F_WIDTH
N_EXPERTS = 16
N_GROUPS = 4
EXPERTS_PER_GROUP = N_EXPERTS // N_GROUPS
D_EXPERT = 256
N_MOD = 6
DEEPNORM_ALPHA = (2.0 * DEPTH) ** 0.25
LN_EPS = 1e-6

F32 = jnp.float32
BF16 = jnp.bfloat16

V7X_VMEM_BYTES = 64 * 1024 * 1024
VMEM_LIMIT_BYTES = (V7X_VMEM_BYTES * 3) // 4
LANES = 128
SUBLANES = 8
HEADS_PER_STEP = LANES // HEAD_DIM
assert HEADS_PER_STEP == 2
MASK_VALUE = -1e30
LOG2_E = math.log2(math.e)

CHUNK_GRID_ROWS = 4
WINDOW_CHUNKS = 3
STEP_CHUNKS = 8


def _cparams(n_grid_dims):
    return pltpu.CompilerParams(dimension_semantics=("arbitrary",) * n_grid_dims,
                                vmem_limit_bytes=VMEM_LIMIT_BYTES)


def _layer_norm(x):
    mu = jnp.mean(x, axis=-1, keepdims=True)
    xc = x - mu
    var = jnp.mean(xc * xc, axis=-1, keepdims=True)
    return xc * lax.rsqrt(var + LN_EPS)


def _dot(a, b):
    return jnp.dot(a, b, preferred_element_type=F32)


def _dot_nt(a, b):
    return lax.dot_general(a, b, (((1,), (1,)), ((), ())), preferred_element_type=F32)


def _mod_kernel(c_ref, w_ref, b_ref, o_ref):
    c = c_ref[...]
    a = c * jax.nn.sigmoid(c)
    o_ref[0] = jnp.dot(a, w_ref[0], preferred_element_type=F32, precision=lax.Precision.HIGHEST) + b_ref[0]


def _modulation(cvec, w_mod, b_mod):
    n_col_blocks = 4
    wc = (N_MOD * D_MODEL) // n_col_blocks
    rows = cvec.shape[0]
    return pl.pallas_call(
        _mod_kernel,
        grid=(DEPTH, n_col_blocks),
        in_specs=[pl.BlockSpec((rows, D_MODEL), lambda i, j: (0, 0)),
                  pl.BlockSpec((1, D_MODEL, wc), lambda i, j: (i, 0, j)),
                  pl.BlockSpec((1, 1, wc), lambda i, j: (i, 0, j))],
        out_specs=pl.BlockSpec((1, rows, wc), lambda i, j: (i, 0, j)),
        out_shape=jax.ShapeDtypeStruct((DEPTH, rows, N_MOD * D_MODEL), F32),
        compiler_params=_cparams(2),
        name="modulation",
    )(cvec, w_mod, b_mod.reshape(DEPTH, 1, N_MOD * D_MODEL))


def _modulate(x, m_ref):
    return (_layer_norm(x) * (1.0 + m_ref[0, 1:2, :]) + m_ref[0, 0:1, :]).astype(BF16)


def _project(h, rows, w_ref, q_ref, k_ref, v_ref, f_ref):
    p = _dot(h, w_ref[...])
    q_ref[0, rows, :] = (p[:, :NA_WIDTH] * (HEAD_DIM ** -0.5 * LOG2_E)).astype(BF16)
    k_ref[0, rows, :] = p[:, NA_WIDTH:2 * NA_WIDTH].astype(BF16)
    v_ref[0, rows, :] = p[:, 2 * NA_WIDTH:3 * NA_WIDTH].astype(BF16)
    f_ref[0, rows, :] = p[:, 3 * NA_WIDTH:]


def _proj_kernel(x_ref, m_ref, w_ref, q_ref, k_ref, v_ref, f_ref):
    tm = x_ref.shape[1]
    halves = [slice(0, tm // 2), slice(tm // 2, tm)] if tm % 512 == 0 else [slice(0, tm)]
    hs = [_modulate(x_ref[0, rows, :], m_ref) for rows in halves]
    for rows, h in zip(halves, hs):
        _project(h, rows, w_ref, q_ref, k_ref, v_ref, f_ref)


def _norm_proj_kernel(back_ref, y_hbm, lslot_ref, x1_ref, m_ref, g_ref, b_ref, m_next_ref, w_ref,
                      x_ref, q_ref, k_ref, v_ref, f_ref, ybuf, sems, *, steps_per_batch, n_steps, lr):
    step = pl.program_id(0) * steps_per_batch + pl.program_id(1)
    ys = _gather_unsort(back_ref, lslot_ref, y_hbm, ybuf, sems, step, n_steps, lr)
    tm = x1_ref.shape[1] // len(ys)
    hs = []
    for t, y in enumerate(ys):
        rows = slice(t * tm, (t + 1) * tm)
        z = DEEPNORM_ALPHA * x1_ref[0, rows, :] + m_ref[0, 5:6, :] * y
        x = _layer_norm(z) * g_ref[...] + b_ref[...]
        x_ref[0, rows, :] = x
        hs.append(_modulate(x, m_next_ref))
    for t, h in enumerate(hs):
        _project(h, slice(t * tm, (t + 1) * tm), w_ref, q_ref, k_ref, v_ref, f_ref)


def _norm_in_projection(y_slots, back_chunk, lslot, x1, m, ln_g, ln_b, m_next, w_in_bf16, tm):
    bsz, length, _ = x1.shape
    tiles_per_batch = length // tm
    per_step = 2 if tiles_per_batch % 2 == 0 else 1
    steps_per_batch = tiles_per_batch // per_step
    ts = per_step * tm
    lr = _local_rows(tm)
    out = jax.ShapeDtypeStruct((bsz, length, NA_WIDTH), BF16)
    out_f = jax.ShapeDtypeStruct((bsz, length, F_WIDTH), F32)
    out_x = jax.ShapeDtypeStruct((bsz, length, D_MODEL), F32)
    row = lambda b, i, back_ref: (b, i, 0)
    const2 = lambda b, i, back_ref: (0, 0)
    mod_spec = pl.BlockSpec((1, N_MOD, D_MODEL), lambda b, i, back_ref: (b, 0, 0))
    o_spec = pl.BlockSpec((1, ts, NA_WIDTH), row)
    x_spec = pl.BlockSpec((1, ts, D_MODEL), row)
    grid_spec = pltpu.PrefetchScalarGridSpec(
        num_scalar_prefetch=1,
        grid=(bsz, steps_per_batch),
        in_specs=[pl.BlockSpec(memory_space=pl.ANY),
                  pl.BlockSpec((1, SUBLANES, ts), lambda b, i, back_ref: (b, 0, i)),
                  x_spec, mod_spec,
                  pl.BlockSpec((1, D_MODEL), const2), pl.BlockSpec((1, D_MODEL), const2),
                  mod_spec, pl.BlockSpec((D_MODEL, IN_WIDTH), const2)],
        out_specs=[x_spec, o_spec, o_spec, o_spec, o_spec],
        scratch_shapes=[pltpu.VMEM((2, per_step * lr, D_MODEL), BF16), pltpu.SemaphoreType.DMA((2,))],
    )
    return pl.pallas_call(
        functools.partial(_norm_proj_kernel, steps_per_batch=steps_per_batch, n_steps=bsz * steps_per_batch,
                          lr=lr),
        grid_spec=grid_spec,
        out_shape=[out_x, out, out, out, out_f],
        compiler_params=_cparams(2),
        name="moe_norm_in_proj",
    )(back_chunk, y_slots, lslot, x1, m, ln_g.reshape(1, D_MODEL), ln_b.reshape(1, D_MODEL), m_next, w_in_bf16)


def _in_projection(x, m, w_in_bf16, tm):
    bsz, length, _ = x.shape
    out = jax.ShapeDtypeStruct((bsz, length, NA_WIDTH), BF16)
    out_f = jax.ShapeDtypeStruct((bsz, length, F_WIDTH), F32)
    o_spec = pl.BlockSpec((1, tm, NA_WIDTH), lambda b, i: (b, i, 0))
    return pl.pallas_call(
        _proj_kernel,
        grid=(bsz, length // tm),
        in_specs=[pl.BlockSpec((1, tm, D_MODEL), lambda b, i: (b, i, 0)),
                  pl.BlockSpec((1, N_MOD, D_MODEL), lambda b, i: (b, 0, 0)),
                  pl.BlockSpec((D_MODEL, IN_WIDTH), lambda b, i: (0, 0))],
        out_specs=[o_spec, o_spec, o_spec, o_spec],
        out_shape=[out, out, out, out_f],
        compiler_params=_cparams(2),
        name="ln_mod_in_proj",
    )(x, m, w_in_bf16)


def _head_lanes(h):
    lane = lax.broadcasted_iota(jnp.int32, (1, LANES), 1)
    return (lane >= HEAD_DIM * h) & (lane < HEAD_DIM * (h + 1))


def _scores_pass(q, h, tiles, s_ref):
    qh = jnp.where(_head_lanes(h), q, jnp.zeros_like(q))
    m = None
    t = tiles[0][0].shape[0]
    for j, (k, _, bias) in enumerate(tiles):
        s = _dot_nt(qh, k)
        if bias is not None:
            s = s + bias
        s_ref[:, j * t:(j + 1) * t] = s
        mj = jnp.max(s, axis=-1, keepdims=True)
        m = mj if m is None else jnp.maximum(m, mj)
    return m


def _pv_pass(h, tiles, s_ref, m):
    o = None
    t = tiles[0][0].shape[0]
    in_head = _head_lanes(h)
    for j, (_, v, _) in enumerate(tiles):
        p = jnp.exp2(s_ref[:, j * t:(j + 1) * t] - m)
        oj = _dot(p.astype(BF16), jnp.where(in_head, v, jnp.ones_like(v)))
        o = oj if o is None else o + oj
    return o / pltpu.roll(o, HEAD_DIM, axis=1)


def _attention_units(units):
    outs = []
    maxima = [_scores_pass(*units[0])]
    for u in range(len(units)):
        if u + 1 < len(units):
            maxima.append(_scores_pass(*units[u + 1]))
        _, h, tiles, s_ref = units[u]
        outs.append(_pv_pass(h, tiles, s_ref, maxima[u]))
    return outs


def _merge_heads(outs):
    merged = outs[0]
    for h in range(1, len(outs)):
        merged = jnp.where(_head_lanes(h), outs[h], merged)
    return merged


DR_PAD = 2 * WIN_ROWS
DC_PAD = 2 * WIN_COLS


def _toeplitz_kernel(r_ref, sel_ref, mask_ref, o_ref):
    n = r_ref.shape[0]
    for qc in range(GRID_W):
        block = jnp.dot(r_ref[...], sel_ref[qc], preferred_element_type=F32, precision=lax.Precision.HIGHEST)
        o_ref[pl.ds(qc, n, stride=GRID_W), :] = block * LOG2_E + mask_ref[qc]


def _rpb_toeplitz(rpb):
    depth, heads, n_dr, n_dc = rpb.shape
    qc = np.arange(GRID_W)[:, None]
    kc = (np.arange(LANES) % GRID_W)[None, :]
    cs = np.clip(qc - WIN_COLS // 2, 0, GRID_W - WIN_COLS)
    col_valid = (kc >= cs) & (kc < cs + WIN_COLS)
    dc = kc - qc + WIN_COLS - 1
    select = (np.arange(DC_PAD)[None, :, None] == dc[:, None, :]) & col_valid[:, None, :]
    mask = np.where(col_valid, 0.0, MASK_VALUE)[:, None, :]
    r = jnp.pad(rpb.astype(F32), ((0, 0), (0, 0), (0, DR_PAD - n_dr), (0, DC_PAD - n_dc)))
    n = depth * heads * DR_PAD
    const3 = lambda: (0, 0, 0)
    out = pl.pallas_call(
        _toeplitz_kernel,
        grid=(),
        in_specs=[pl.BlockSpec((n, DC_PAD), lambda: (0, 0)),
                  pl.BlockSpec((GRID_W, DC_PAD, LANES), const3), pl.BlockSpec((GRID_W, 1, LANES), const3)],
        out_specs=pl.BlockSpec((n * GRID_W, LANES), lambda: (0, 0)),
        out_shape=jax.ShapeDtypeStruct((n * GRID_W, LANES), F32),
        compiler_params=pltpu.CompilerParams(vmem_limit_bytes=VMEM_LIMIT_BYTES),
        name="rpb_toeplitz",
    )(r.reshape(n, DC_PAD), jnp.asarray(select, F32), jnp.asarray(mask, F32))
    return out.reshape(depth * heads, DR_PAD, GRID_W, LANES)


assert WINDOW_CHUNKS * CHUNK_GRID_ROWS >= CHUNK_GRID_ROWS + WIN_ROWS - 1


def _window_start_chunk(cr, n_cr):
    lo = cr - (WINDOW_CHUNKS - 1) // 2
    if isinstance(cr, int):
        return max(0, min(lo, n_cr - WINDOW_CHUNKS))
    return jnp.clip(lo, 0, n_cr - WINDOW_CHUNKS)


def _window_key_rows(cr, n_cr):
    start = _window_start_chunk(cr, n_cr) * CHUNK_GRID_ROWS
    return list(range(start, start + WINDOW_CHUNKS * CHUNK_GRID_ROWS))


def _row_window(qr, rows):
    kh = min(WIN_ROWS, rows)
    rs = int(np.clip(qr - kh // 2, 0, rows - kh))
    return rs, rs + kh


def _window_variant(cr, n_cr):
    return jnp.where(cr == 0, 0, jnp.where(cr == n_cr - 1, 2, 1))


def _check_windows(rows):
    def relative(cr):
        base = cr * CHUNK_GRID_ROWS
        return ([kr - base for kr in _window_key_rows(cr, n_cr)],
                [tuple(r - base for r in _row_window(base + qi, rows)) for qi in range(CHUNK_GRID_ROWS)])

    n_cr = rows // CHUNK_GRID_ROWS
    for cr in range(n_cr):
        assert cr in (0, n_cr - 1) or relative(cr) == relative(1), cr
        have = set(_window_key_rows(cr, n_cr))
        for qi in range(CHUNK_GRID_ROWS):
            lo, hi = _row_window(cr * CHUNK_GRID_ROWS + qi, rows)
            assert set(range(lo, hi)) <= have, (cr, qi)


def _build_bias_tables(t_ref, bias_scr, rows):
    n_cr = rows // CHUNK_GRID_ROWS
    left = lax.broadcasted_iota(jnp.int32, (GRID_W, LANES), 1) < GRID_W
    masked = jnp.full((GRID_W, LANES), MASK_VALUE, F32)
    for variant, cr in enumerate((0, 1, n_cr - 1)):
        key_rows = _window_key_rows(cr, n_cr)
        for h in range(HEADS_PER_STEP):
            for qi in range(CHUNK_GRID_ROWS):
                qr = cr * CHUNK_GRID_ROWS + qi
                lo, hi = _row_window(qr, rows)
                for p in range(len(key_rows) // 2):
                    pair = [t_ref[h, kr - qr + WIN_ROWS - 1] if lo <= kr < hi else None
                            for kr in key_rows[2 * p:2 * p + 2]]
                    if pair[0] is None and pair[1] is None:
                        block = masked
                    else:
                        block = jnp.where(left, masked if pair[0] is None else pair[0],
                                          masked if pair[1] is None else pair[1])
                    bias_scr[variant, h, qi * GRID_W:(qi + 1) * GRID_W, p * LANES:(p + 1) * LANES] = block


def _na_kernel(q_ref, *refs, rows):
    n_kv = STEP_CHUNKS * WINDOW_CHUNKS
    k_refs, v_refs = refs[:n_kv], refs[n_kv:2 * n_kv]
    kc_ref, vc_ref, t_ref, o_ref, bias_scr, s_scr = refs[2 * n_kv:]
    n_cr = rows // CHUNK_GRID_ROWS
    b, rb = pl.program_id(1), pl.program_id(2)

    @pl.when((b == 0) & (rb == 0))
    def _():
        _build_bias_tables(t_ref, bias_scr, rows)

    tq = CHUNK_GRID_ROWS * GRID_W
    units = []
    for c in range(STEP_CHUNKS):
        variant = _window_variant(rb * STEP_CHUNKS + c, n_cr)
        q = q_ref[0, c * tq:(c + 1) * tq, :]
        for h in range(HEADS_PER_STEP):
            lat = [(k_refs[c * WINDOW_CHUNKS + j][0], v_refs[c * WINDOW_CHUNKS + j][0],
                    bias_scr[variant, h, :, j * tq:(j + 1) * tq]) for j in range(WINDOW_CHUNKS)]
            units.append((q, h, lat + [(kc_ref[0], vc_ref[0], None)], s_scr.at[c, h]))
    outs = _attention_units(units)
    for c in range(STEP_CHUNKS):
        o = _merge_heads(outs[c * HEADS_PER_STEP:(c + 1) * HEADS_PER_STEP])
        o_ref[0, c * tq:(c + 1) * tq, :] = o.astype(BF16)


def _neighborhood_attention(q, k, v, kc, vc, toeplitz, layer):
    bsz, length, _ = q.shape
    rows = length // GRID_W
    n_cr = rows // CHUNK_GRID_ROWS
    assert rows % (CHUNK_GRID_ROWS * STEP_CHUNKS) == 0 and n_cr >= WINDOW_CHUNKS + 2
    _check_windows(rows)
    n_ctx = kc.shape[1]
    tq = CHUNK_GRID_ROWS * GRID_W
    assert n_ctx == tq

    def kv_spec(c, j):
        return pl.BlockSpec((1, tq, LANES),
                            lambda hp, b, rb: (b, _window_start_chunk(rb * STEP_CHUNKS + c, n_cr) + j, hp))

    kv_specs = [kv_spec(c, j) for c in range(STEP_CHUNKS) for j in range(WINDOW_CHUNKS)]
    ctx_spec = pl.BlockSpec((1, n_ctx, LANES), lambda hp, b, rb: (b, 0, hp))
    q_spec = pl.BlockSpec((1, STEP_CHUNKS * tq, LANES), lambda hp, b, rb: (b, rb, hp))
    window = WINDOW_CHUNKS * tq
    return pl.pallas_call(
        functools.partial(_na_kernel, rows=rows),
        grid=(NA_HEADS // HEADS_PER_STEP, bsz, n_cr // STEP_CHUNKS),
        in_specs=([q_spec] + kv_specs + kv_specs
                  + [ctx_spec, ctx_spec,
                     pl.BlockSpec((HEADS_PER_STEP, DR_PAD, GRID_W, LANES),
                                  lambda hp, b, rb: (layer * (NA_HEADS // HEADS_PER_STEP) + hp, 0, 0, 0))]),
        out_specs=q_spec,
        out_shape=jax.ShapeDtypeStruct((bsz, length, NA_WIDTH), BF16),
        scratch_shapes=[pltpu.VMEM((3, HEADS_PER_STEP, tq, window), F32),
                        pltpu.VMEM((STEP_CHUNKS, HEADS_PER_STEP, tq, window + n_ctx), F32)],
        compiler_params=_cparams(3),
        name="neighborhood_attention",
    )(q, *([k] * len(kv_specs)), *([v] * len(kv_specs)), kc, vc, toeplitz)


def _ctx_attn_kernel(q_ref, k_ref, v_ref, o_ref, s_scr):
    tiles = [(k_ref[0], v_ref[0], None)]
    outs = _attention_units([(q_ref[0], h, tiles, s_scr.at[h]) for h in range(HEADS_PER_STEP)])
    o_ref[0] = _merge_heads(outs).astype(BF16)


def _context_attention(qc, kc, vc):
    bsz, n_ctx, _ = qc.shape
    spec = pl.BlockSpec((1, n_ctx, LANES), lambda b, hp: (b, 0, hp))
    return pl.pallas_call(
        _ctx_attn_kernel,
        grid=(bsz, NA_HEADS // HEADS_PER_STEP),
        in_specs=[spec, spec, spec],
        out_specs=spec,
        out_shape=jax.ShapeDtypeStruct((bsz, n_ctx, NA_WIDTH), BF16),
        scratch_shapes=[pltpu.VMEM((HEADS_PER_STEP, n_ctx, n_ctx), F32)],
        compiler_params=_cparams(2),
        name="context_attention",
    )(qc, kc, vc)


def _dft_cos_sin(n):
    ang = 2.0 * np.pi * np.outer(np.arange(n), np.arange(n)) / n
    return np.cos(ang), np.sin(ang)


def _bf16_table(a):
    return jnp.asarray(a, F32).astype(BF16)


def _channel_dft_matrix(n_groups=F_GROUPS):
    c, s = _dft_cos_sin(F_GROUP_DIM)
    scale = F_GROUP_DIM ** -0.5
    eye = np.eye(n_groups)
    return np.concatenate([np.kron(eye, c), np.kron(eye, s)], axis=1) * scale


def _fft_stage1_kernel(f_ref, perm_ref, w1_ref, cs_ref, sc_ref, tc_ref, ts_ref, zr_ref, zi_ref, *, n_slow, nt):
    x = f_ref[0].reshape(n_slow * nt, F_WIDTH).astype(BF16)
    x = _dot(perm_ref[...], x).astype(BF16)
    ab = [_dot(x[:, p * LANES:(p + 1) * LANES], w1_ref[...]).astype(BF16) for p in range(F_WIDTH // LANES)]
    a_all = jnp.concatenate([blk[:, :LANES] for blk in ab], axis=-1)
    b_all = jnp.concatenate([blk[:, LANES:] for blk in ab], axis=-1)
    for t in range(nt):
        a = a_all[t * n_slow:(t + 1) * n_slow]
        b = b_all[t * n_slow:(t + 1) * n_slow]
        z = _dot(cs_ref[...], a) + _dot(sc_ref[...], b)
        zr, zi = z[:n_slow], z[n_slow:]
        c, s = tc_ref[t], ts_ref[t]
        zr_ref[0, t] = (zr * c - zi * s).astype(BF16)
        zi_ref[0, t] = (zr * s + zi * c).astype(BF16)


def _fft_stage2_kernel(zr_ref, zi_ref, f_ref, y_ref, *, n_fast, kb):
    sub = f_ref.shape[0] // n_fast
    parts = []
    for j0 in range(0, kb, sub):
        rhs = jnp.concatenate([zr_ref[0, :, j0:j0 + sub, :].reshape(n_fast * sub, F_WIDTH),
                               zi_ref[0, :, j0:j0 + sub, :].reshape(n_fast * sub, F_WIDTH)], axis=0)
        parts.append(_dot(f_ref[...], rhs).astype(BF16).reshape(n_fast, sub, F_WIDTH))
    y_ref[0] = jnp.concatenate(parts, axis=1)


def _fourier_positions(f, n_slow, n_fast):
    bsz, n, _ = f.shape
    assert n == n_slow * n_fast
    nt = SUBLANES
    kb = 2 * SUBLANES
    w1 = _bf16_table(_channel_dft_matrix(LANES // F_GROUP_DIM))
    perm = _bf16_table(np.eye(n_slow * nt).reshape(n_slow, nt, n_slow * nt).transpose(1, 0, 2)
                       .reshape(n_slow * nt, n_slow * nt))
    c1, s1 = _dft_cos_sin(n_slow)
    sc1 = n_slow ** -0.5
    cs = _bf16_table(np.concatenate([c1, s1], axis=0) * sc1)
    sc = _bf16_table(np.concatenate([-s1, c1], axis=0) * sc1)
    tw = 2.0 * np.pi * np.outer(np.arange(n_fast), np.arange(n_slow)) / n
    tc = jnp.asarray(np.cos(tw)[:, :, None], F32)
    ts = jnp.asarray(np.sin(tw)[:, :, None], F32)
    z_shape = jax.ShapeDtypeStruct((bsz, n_fast, n_slow, F_WIDTH), BF16)
    z_spec = pl.BlockSpec((1, nt, n_slow, F_WIDTH), lambda b, j: (b, j, 0, 0))
    const2 = lambda b, j: (0, 0)
    tw_spec = pl.BlockSpec((nt, n_slow, 1), lambda b, j: (j, 0, 0))
    zr, zi = pl.pallas_call(
        functools.partial(_fft_stage1_kernel, n_slow=n_slow, nt=nt),
        grid=(bsz, n_fast // nt),
        in_specs=[pl.BlockSpec((1, n_slow, nt, F_WIDTH), lambda b, j: (b, 0, j, 0)),
                  pl.BlockSpec(perm.shape, const2),
                  pl.BlockSpec(w1.shape, const2), pl.BlockSpec(cs.shape, const2), pl.BlockSpec(sc.shape, const2),
                  tw_spec, tw_spec],
        out_specs=[z_spec, z_spec],
        out_shape=[z_shape, z_shape],
        compiler_params=_cparams(2),
        name="fnet_stage1",
    )(f.reshape(bsz, n_slow, n_fast, F_WIDTH), perm, w1, cs, sc, tc, ts)

    c2, s2 = _dft_cos_sin(n_fast)
    sc2 = n_fast ** -0.5
    eye = np.eye(SUBLANES)
    f2 = _bf16_table(np.concatenate([np.kron(c2, eye), np.kron(-s2, eye)], axis=1) * sc2)
    blk = pl.BlockSpec((1, n_fast, kb, F_WIDTH), lambda b, j: (b, 0, j, 0))
    y = pl.pallas_call(
        functools.partial(_fft_stage2_kernel, n_fast=n_fast, kb=kb),
        grid=(bsz, n_slow // kb),
        in_specs=[blk, blk, pl.BlockSpec(f2.shape, const2)],
        out_specs=blk,
        out_shape=jax.ShapeDtypeStruct((bsz, n_fast, n_slow, F_WIDTH), BF16),
        compiler_params=_cparams(2),
        name="fnet_stage2",
    )(zr, zi, f2)
    return y.reshape(bsz, n, F_WIDTH)


def _second_largest_sum(a, b, c, d):
    mab, nab = jnp.maximum(a, b), jnp.minimum(a, b)
    mcd, ncd = jnp.maximum(c, d), jnp.minimum(c, d)
    return jnp.maximum(mab, mcd) + jnp.maximum(jnp.minimum(mab, mcd), jnp.maximum(nab, ncd))


def _selected_group(sb_rows):
    epg = EXPERTS_PER_GROUP
    g_score = [_second_largest_sum(*sb_rows[g * epg:(g + 1) * epg]) for g in range(N_GROUPS)]
    best = functools.reduce(jnp.maximum, g_score)
    group = jnp.full_like(best, float(N_GROUPS - 1))
    for g in range(N_GROUPS - 2, -1, -1):
        group = jnp.where(g_score[g] == best, float(g), group)
    return group


def _top2_gates(cand_s, cand_sb):
    n = len(cand_s)
    w = []
    for j in range(n):
        rank = jnp.zeros_like(cand_sb[j])
        for i in range(n):
            if i == j:
                continue
            ahead = (cand_sb[i] > cand_sb[j]) | ((cand_sb[i] == cand_sb[j]) & (i < j))
            rank = rank + jnp.where(ahead, 1.0, 0.0)
        w.append(jnp.where(rank < 2.0, cand_s[j], 0.0))
    total = functools.reduce(jnp.add, w)
    return [wj / total for wj in w]


CHUNK_ROWS = 16


def _local_rows(tm):
    need = tm + N_GROUPS * (CHUNK_ROWS - 1) + CHUNK_ROWS
    return -(-need // CHUNK_ROWS) * CHUNK_ROWS


def _sort_matrix(lslot, lr):
    r_iota = lax.broadcasted_iota(jnp.int32, (lr, 1), 0).astype(F32)
    return jnp.where(r_iota == lslot, 1.0, 0.0).astype(BF16)


def _local_sort(group, h2b, tri_ref, lr):
    t = group.shape[1]
    g_iota = lax.broadcasted_iota(jnp.int32, (SUBLANES, 1), 0).astype(F32)
    onehot = g_iota == group
    prefix = _dot(jnp.where(onehot, 1.0, 0.0).astype(BF16), tri_ref[...])
    count = prefix[:, t - 1:t]
    padded = jnp.floor((count + (CHUNK_ROWS - 1.0)) * (1.0 / CHUNK_ROWS)) * CHUNK_ROWS
    lslot = jnp.zeros((1, t), F32)
    start = jnp.zeros((1, 1), F32)
    for g in range(N_GROUPS):
        lslot = jnp.where(onehot[g:g + 1], start + prefix[g:g + 1] - 1.0, lslot)
        start = start + padded[g:g + 1]
    return lslot, count, _dot(_sort_matrix(lslot, lr), h2b).astype(BF16)


def _out_kernel(a_ref, y_ref, x_ref, m_ref, wo_ref, bd_ref, g_ref, b_ref, wr_ref, rb_ref, tri_ref,
                x1_ref, h2_ref, lslot_ref, count_ref):
    tm = tri_ref.shape[0]
    lr = h2_ref.shape[0] // count_ref.shape[0]

    def project(s):
        rows = slice(s * tm, (s + 1) * tm)
        y2 = _dot(y_ref[0, rows, :].astype(BF16), bd_ref[...]).astype(BF16)
        return _dot(a_ref[0, rows, :], wo_ref[:NA_WIDTH, :]) + _dot(y2, wo_ref[NA_WIDTH:, :])

    def finish(s, o):
        rows = slice(s * tm, (s + 1) * tm)
        z = DEEPNORM_ALPHA * x_ref[0, rows, :] + m_ref[0, 2:3, :] * o
        x1 = _layer_norm(z) * g_ref[...] + b_ref[...]
        x1_ref[0, rows, :] = x1
        h2 = (_layer_norm(x1) * (1.0 + m_ref[0, 4:5, :]) + m_ref[0, 3:4, :]).astype(BF16)
        sb = jax.nn.sigmoid(_dot_nt(wr_ref[...], h2)) + rb_ref[...]
        group = _selected_group([sb[e:e + 1] for e in range(N_EXPERTS)])
        lslot, count, h2_sorted = _local_sort(group, h2, tri_ref, lr)
        h2_ref[s * lr:(s + 1) * lr, :] = h2_sorted
        lslot_ref[0, :, rows] = jnp.broadcast_to(lslot, (lslot_ref.shape[1], tm))
        count_ref[s] = jnp.broadcast_to(count, count_ref.shape[1:])

    n_sub = count_ref.shape[0]
    projected = project(0)
    for s in range(n_sub):
        following = project(s + 1) if s + 1 < n_sub else None
        finish(s, projected)
        projected = following


def _out_projection(attn, yf, x, m, w_out_bf16, w_four_bd, ln_g, ln_b, w_router_t, router_bias, tm):
    bsz, length, _ = x.shape
    tiles_per_batch = length // tm
    n_tiles = bsz * tiles_per_batch
    per_step = 2 if tiles_per_batch % 2 == 0 else 1
    steps_per_batch = tiles_per_batch // per_step
    ts = per_step * tm
    lr = _local_rows(tm)
    tri = _bf16_table(np.triu(np.ones((tm, tm))))
    row = lambda b, i: (b, i, 0)
    const2 = lambda b, i: (0, 0)
    return pl.pallas_call(
        _out_kernel,
        grid=(bsz, steps_per_batch),
        in_specs=[pl.BlockSpec((1, ts, NA_WIDTH), row), pl.BlockSpec((1, ts, F_WIDTH), row),
                  pl.BlockSpec((1, ts, D_MODEL), row),
                  pl.BlockSpec((1, N_MOD, D_MODEL), lambda b, i: (b, 0, 0)),
                  pl.BlockSpec((NA_WIDTH + F_WIDTH, D_MODEL), const2),
                  pl.BlockSpec((F_WIDTH, F_WIDTH), const2),
                  pl.BlockSpec((1, D_MODEL), const2), pl.BlockSpec((1, D_MODEL), const2),
                  pl.BlockSpec((N_EXPERTS, D_MODEL), const2), pl.BlockSpec((N_EXPERTS, 1), const2),
                  pl.BlockSpec((tm, tm), const2)],
        out_specs=[pl.BlockSpec((1, ts, D_MODEL), row),
                   pl.BlockSpec((per_step * lr, D_MODEL), lambda b, i: (b * steps_per_batch + i, 0)),
                   pl.BlockSpec((1, SUBLANES, ts), lambda b, i: (b, 0, i)),
                   pl.BlockSpec((per_step, SUBLANES, LANES), lambda b, i: (b * steps_per_batch + i, 0, 0))],
        out_shape=[jax.ShapeDtypeStruct((bsz, length, D_MODEL), F32),
                   jax.ShapeDtypeStruct((n_tiles * lr, D_MODEL), BF16),
                   jax.ShapeDtypeStruct((bsz, SUBLANES, length), F32),
                   jax.ShapeDtypeStruct((n_tiles, SUBLANES, LANES), F32)],
        compiler_params=_cparams(2),
        name="out_proj_norm_route",
    )(attn, yf, x, m, w_out_bf16, w_four_bd, ln_g.reshape(1, D_MODEL), ln_b.reshape(1, D_MODEL),
      w_router_t, router_bias.reshape(N_EXPERTS, 1), tri)


def _run_plan(count, tm, lr, tm_slots):
    n_tiles = count.shape[0]
    ch = CHUNK_ROWS
    n_slots = -(-(n_tiles * (tm + N_GROUPS * (ch - 1)) + N_GROUPS * tm_slots) // tm_slots) * tm_slots
    zero_chunk = lr // ch - 1
    pad = ((count + ch - 1) // ch) * ch
    lstart = jnp.cumsum(pad, axis=1) - pad
    run_off = jnp.cumsum(pad, axis=0) - pad
    seg_len = jnp.sum(pad, axis=0)
    seg_pad = ((seg_len + tm_slots - 1) // tm_slots) * tm_slots
    seg_end = jnp.cumsum(seg_pad)
    seg_start = seg_end - seg_pad

    s = (jnp.arange(n_slots // ch, dtype=jnp.int32) * ch)[:, None, None]
    in_run = (seg_start + run_off <= s) & (s < seg_start + run_off + pad)
    src_row = jnp.arange(n_tiles, dtype=jnp.int32)[None, :, None] * lr + lstart + (s - seg_start - run_off)
    src_row = jnp.sum(jnp.where(in_run, src_row, 0), axis=(1, 2))
    src_chunk = jnp.where(jnp.any(in_run, axis=(1, 2)), src_row // ch, zero_chunk).astype(jnp.int32)

    tile_start = jnp.arange(n_slots // tm_slots, dtype=jnp.int32) * tm_slots
    tile_group = jnp.minimum(jnp.sum((seg_end[None, :] <= tile_start[:, None]).astype(jnp.int32), axis=1),
                             N_GROUPS - 1)
    n_used = (seg_end[N_GROUPS - 1:] // tm_slots).astype(jnp.int32)

    r = (jnp.arange(lr // ch, dtype=jnp.int32) * ch)[None, :, None]
    in_local = (lstart[:, None, :] <= r) & (r < (lstart + pad)[:, None, :])
    slot = (seg_start + run_off - lstart)[:, None, :] + r
    back_chunk = (jnp.sum(jnp.where(in_local, slot, 0), axis=2) // ch).astype(jnp.int32)
    return src_chunk, tile_group.astype(jnp.int32), n_used, back_chunk.reshape(-1)


def _chunk_gather(idx_ref, tile, n_chunks, src_hbm, buf, sems, wait):
    slot = tile % 2
    rows = n_chunks * CHUNK_ROWS
    if wait:
        pltpu.make_async_copy(src_hbm.at[pl.ds(0, rows), :], buf.at[slot], sems.at[slot]).wait()
        return
    for c in range(n_chunks):
        src = pl.multiple_of(idx_ref[tile * n_chunks + c] * CHUNK_ROWS, CHUNK_ROWS)
        pltpu.make_async_copy(src_hbm.at[pl.ds(src, CHUNK_ROWS), :],
                              buf.at[slot, pl.ds(c * CHUNK_ROWS, CHUNK_ROWS), :], sems.at[slot]).start()


def _moe_runs_kernel(src_ref, tile_group_ref, n_used_ref, h_hbm, wg32_ref, wu32_ref, wd32_ref, wr_ref, rb_ref,
                     y_ref, hbuf, wg_ref, wu_ref, wd_ref, sems, *, tm):
    step = pl.program_id(0)
    n_used = n_used_ref[0]
    n_chunks = tm // CHUNK_ROWS
    new_group = (step == 0) | (tile_group_ref[step] != tile_group_ref[jnp.maximum(step - 1, 0)])

    @pl.when(new_group & (step < n_used))
    def _():
        wg_ref[...] = wg32_ref[...].astype(BF16)
        wu_ref[...] = wu32_ref[...].astype(BF16)
        wd_ref[...] = wd32_ref[...].astype(BF16)

    @pl.when((step == 0) & (n_used > 0))
    def _():
        _chunk_gather(src_ref, 0, n_chunks, h_hbm, hbuf, sems, wait=False)

    @pl.when(step + 1 < n_used)
    def _():
        _chunk_gather(src_ref, step + 1, n_chunks, h_hbm, hbuf, sems, wait=False)

    @pl.when(step < n_used)
    def _():
        _chunk_gather(src_ref, step, n_chunks, h_hbm, hbuf, sems, wait=True)
        h = hbuf[step % 2]
        s = jax.nn.sigmoid(_dot(h, wr_ref[0]))
        sb = s + rb_ref[0]
        epg = EXPERTS_PER_GROUP
        gates = _top2_gates([s[:, j:j + 1] for j in range(epg)], [sb[:, j:j + 1] for j in range(epg)])
        acc = None
        gate_up = [(_dot(h, wg_ref[0]), _dot(h, wu_ref[0]))]
        for e in range(epg):
            if e + 1 < epg:
                gate_up.append((_dot(h, wg_ref[e + 1]), _dot(h, wu_ref[e + 1])))
            gate, up = gate_up[e]
            hid = (gate * jax.nn.sigmoid(gate)) * up * gates[e]
            y = _dot(hid.astype(BF16), wd_ref[e])
            acc = y if acc is None else acc + y
        y_ref[...] = acc.astype(BF16)

    @pl.when(step >= n_used)
    def _():
        y_ref[...] = jnp.zeros_like(y_ref)


def _moe_runs(h_sorted, src_chunk, tile_group, n_used, layer, w_gate, w_up, w_down, w_router_grp,
              router_bias_grp, tm):
    n_slots = src_chunk.shape[0] * CHUNK_ROWS
    epg = EXPERTS_PER_GROUP
    by_group = lambda i, src_ref, tg_ref, nu_ref: (tg_ref[i], 0, 0)
    by_layer_group = lambda i, src_ref, tg_ref, nu_ref: (layer * N_GROUPS + tg_ref[i], 0, 0)
    grid_spec = pltpu.PrefetchScalarGridSpec(
        num_scalar_prefetch=3,
        grid=(n_slots // tm,),
        in_specs=[pl.BlockSpec(memory_space=pl.ANY),
                  pl.BlockSpec((epg, D_MODEL, D_EXPERT), by_layer_group),
                  pl.BlockSpec((epg, D_MODEL, D_EXPERT), by_layer_group),
                  pl.BlockSpec((epg, D_EXPERT, D_MODEL), by_layer_group),
                  pl.BlockSpec((1, D_MODEL, LANES), by_group),
                  pl.BlockSpec((1, 1, LANES), by_group)],
        out_specs=pl.BlockSpec((tm, D_MODEL), lambda i, src_ref, tg_ref, nu_ref: (i, 0)),
        scratch_shapes=[pltpu.VMEM((2, tm, D_MODEL), BF16),
                        pltpu.VMEM((epg, D_MODEL, D_EXPERT), BF16), pltpu.VMEM((epg, D_MODEL, D_EXPERT), BF16),
                        pltpu.VMEM((epg, D_EXPERT, D_MODEL), BF16), pltpu.SemaphoreType.DMA((2,))],
    )
    return pl.pallas_call(
        functools.partial(_moe_runs_kernel, tm=tm),
        grid_spec=grid_spec,
        out_shape=jax.ShapeDtypeStruct((n_slots, D_MODEL), BF16),
        compiler_params=_cparams(1),
        name="moe_group_experts",
    )(src_chunk, tile_group, n_used, h_sorted, w_gate, w_up, w_down, w_router_grp, router_bias_grp)


def _gather_unsort(back_ref, lslot_ref, y_hbm, ybuf, sems, step, n_steps, lr):
    per_step = ybuf.shape[1] // lr
    tm = lslot_ref.shape[2] // per_step
    n_chunks = per_step * lr // CHUNK_ROWS

    @pl.when(step == 0)
    def _():
        _chunk_gather(back_ref, 0, n_chunks, y_hbm, ybuf, sems, wait=False)

    @pl.when(step + 1 < n_steps)
    def _():
        _chunk_gather(back_ref, step + 1, n_chunks, y_hbm, ybuf, sems, wait=False)

    _chunk_gather(back_ref, step, n_chunks, y_hbm, ybuf, sems, wait=True)
    outs = []
    for t in range(per_step):
        sort = _sort_matrix(lslot_ref[0, 0:1, t * tm:(t + 1) * tm], lr)
        outs.append(lax.dot_general(sort, ybuf[step % 2, t * lr:(t + 1) * lr, :], (((0,), (0,)), ((), ())),
                                    preferred_element_type=F32))
    return outs


def _residual_norm_kernel(back_ref, y_hbm, lslot_ref, x1_ref, m_ref, g_ref, b_ref, o_ref, ybuf, sems, *, n_steps, lr):
    y, = _gather_unsort(back_ref, lslot_ref, y_hbm, ybuf, sems, pl.program_id(0), n_steps, lr)
    z = DEEPNORM_ALPHA * x1_ref[...] + m_ref[0, 5:6, :] * y
    o_ref[...] = _layer_norm(z) * g_ref[...] + b_ref[...]


def _residual_norm(y_slots, back_chunk, lslot, x1, m, ln_g, ln_b, tm):
    bsz, length, _ = x1.shape
    n = bsz * length
    tiles_per_batch = length // tm
    lr = _local_rows(tm)
    const2 = lambda i, back_ref: (0, 0)
    row = pl.BlockSpec((tm, D_MODEL), lambda i, back_ref: (i, 0))
    grid_spec = pltpu.PrefetchScalarGridSpec(
        num_scalar_prefetch=1,
        grid=(n // tm,),
        in_specs=[pl.BlockSpec(memory_space=pl.ANY),
                  pl.BlockSpec((1, SUBLANES, tm),
                               lambda i, back_ref: (i // tiles_per_batch, 0, i % tiles_per_batch)),
                  row, pl.BlockSpec((1, N_MOD, D_MODEL), lambda i, back_ref: (i // tiles_per_batch, 0, 0)),
                  pl.BlockSpec((1, D_MODEL), const2), pl.BlockSpec((1, D_MODEL), const2)],
        out_specs=row,
        scratch_shapes=[pltpu.VMEM((2, lr, D_MODEL), BF16), pltpu.SemaphoreType.DMA((2,))],
    )
    out = pl.pallas_call(
        functools.partial(_residual_norm_kernel, n_steps=n // tm, lr=lr),
        grid_spec=grid_spec,
        out_shape=jax.ShapeDtypeStruct((n, D_MODEL), F32),
        compiler_params=_cparams(1),
        name="moe_residual_norm",
    )(back_chunk, y_slots, lslot, x1.reshape(n, D_MODEL), m, ln_g.reshape(1, D_MODEL), ln_b.reshape(1, D_MODEL))
    return out.reshape(bsz, length, D_MODEL)


def _grouped_moe_runs(h_sorted, count_rows, layer, w_gate, w_up, w_down, w_router_grp, router_bias_grp,
                      tm_tokens, tm_slots):
    count = count_rows[:, :N_GROUPS, 0].astype(jnp.int32)
    src_chunk, tile_group, n_used, back_chunk = _run_plan(count, tm_tokens, _local_rows(tm_tokens), tm_slots)
    y_slots = _moe_runs(h_sorted, src_chunk, tile_group, n_used, layer, w_gate, w_up, w_down, w_router_grp,
                        router_bias_grp, tm_slots)
    return y_slots, back_chunk


def _ctx_fourier_kernel(f_ref, w1_ref, c_ref, s_ref, y_ref):
    ab = _dot(f_ref[0].astype(BF16), w1_ref[...]).astype(BF16)
    y = _dot(c_ref[...], ab[:, :F_WIDTH]) + _dot(s_ref[...], ab[:, F_WIDTH:])
    y_ref[0] = y.astype(BF16)


def _context_fourier(fc):
    bsz, n, _ = fc.shape
    w1 = _bf16_table(_channel_dft_matrix())
    c, s = _dft_cos_sin(n)
    cm = _bf16_table(c * n ** -0.5)
    sm = _bf16_table(-s * n ** -0.5)
    const2 = lambda b: (0, 0)
    blk = pl.BlockSpec((1, n, F_WIDTH), lambda b: (b, 0, 0))
    return pl.pallas_call(
        _ctx_fourier_kernel,
        grid=(bsz,),
        in_specs=[blk, pl.BlockSpec(w1.shape, const2), pl.BlockSpec(cm.shape, const2),
                  pl.BlockSpec(sm.shape, const2)],
        out_specs=blk,
        out_shape=jax.ShapeDtypeStruct((bsz, n, F_WIDTH), BF16),
        compiler_params=_cparams(1),
        name="context_fnet",
    )(fc, w1, cm, sm)


def _block_diag(w):
    g, c, _ = w.shape
    eye = jnp.eye(g, dtype=w.dtype)
    return (eye[:, None, :, None] * w[:, :, None, :]).reshape(g * c, g * c)


def kernel(x, c, ctx, c_ctx, w_mod, b_mod, w_in, rpb, w_four, w_out, ln1_g, ln1_b, ln2_g, ln2_b,
           w_router, router_bias, w_gate, w_up, w_down):
    bsz, length, _ = x.shape
    n_ctx = ctx.shape[1]
    rows = length // GRID_W

    cvec = jnp.concatenate([c, c_ctx[None, :], jnp.zeros((8 - bsz - 1, D_MODEL), F32)], axis=0)
    mods = _modulation(cvec, w_mod, b_mod)
    w_router_t = w_router.T.astype(BF16)
    toeplitz = _rpb_toeplitz(rpb)
    lane_pad = LANES - EXPERTS_PER_GROUP
    w_router_grp = jnp.pad(w_router.reshape(D_MODEL, N_GROUPS, EXPERTS_PER_GROUP).transpose(1, 0, 2),
                           ((0, 0), (0, 0), (0, lane_pad))).astype(BF16)
    router_bias_grp = jnp.pad(router_bias.astype(F32).reshape(N_GROUPS, 1, EXPERTS_PER_GROUP),
                              ((0, 0), (0, 0), (0, lane_pad)))

    wg = w_gate.reshape(DEPTH * N_EXPERTS, D_MODEL, D_EXPERT)
    wu = w_up.reshape(DEPTH * N_EXPERTS, D_MODEL, D_EXPERT)
    wd = w_down.reshape(DEPTH * N_EXPERTS, D_EXPERT, D_MODEL)

    def latent_mod(i):
        return mods[i, :bsz].reshape(bsz, N_MOD, D_MODEL)

    xc = ctx
    projected = None
    for i in range(DEPTH):
        last = i == DEPTH - 1
        m = latent_mod(i)
        mc = jnp.broadcast_to(mods[i, bsz].reshape(1, N_MOD, D_MODEL), (bsz, N_MOD, D_MODEL))
        w_in_b = w_in[i].astype(BF16)
        w_out_b = w_out[i].astype(BF16)
        w_four_bd = _block_diag(w_four[i]).astype(BF16)

        q, k, v, f = projected if projected is not None else _in_projection(x, m, w_in_b, tm=1024)
        qc, kc, vc, fc = _in_projection(xc, mc, w_in_b, tm=n_ctx)

        attn = _neighborhood_attention(q, k, v, kc, vc, toeplitz, i)
        yf = _fourier_positions(f, n_slow=rows, n_fast=GRID_W)
        x1, h2, lslot, cnt = _out_projection(attn, yf, x, m, w_out_b, w_four_bd, ln1_g[i], ln1_b[i],
                                             w_router_t, router_bias, tm=512)
        y, back = _grouped_moe_runs(h2, cnt, i, wg, wu, wd, w_router_grp, router_bias_grp,
                                    tm_tokens=512, tm_slots=512)
        if last:
            return _residual_norm(y, back, lslot, x1, m, ln2_g[i], ln2_b[i], tm=512)
        x, *projected = _norm_in_projection(y, back, lslot, x1, m, ln2_g[i], ln2_b[i], latent_mod(i + 1),
                                            w_in[i + 1].astype(BF16), tm=512)

        attn_c = _context_attention(qc, kc, vc)
        yc = _context_fourier(fc)
        xc1, h2c, lslot_c, cnt_c = _out_projection(attn_c, yc, xc, mc, w_out_b, w_four_bd, ln1_g[i], ln1_b[i],
                                                   w_router_t, router_bias, tm=n_ctx)
        yc2, back_c = _grouped_moe_runs(h2c, cnt_c, i, wg, wu, wd, w_router_grp, router_bias_grp,
                                        tm_tokens=n_ctx, tm_slots=128)
        xc = _residual_norm(yc2, back_c, lslot_c, xc1, mc, ln2_g[i], ln2_b[i], tm=n_ctx)
    return x
```

```python
import functools
import math

import numpy as np
import jax
import jax.numpy as jnp
from jax import lax
from jax.experimental import pallas as pl
from jax.experimental.pallas import tpu as pltpu

D_MODEL = 1024
DEPTH = 2
GRID_W = 64
NA_HEADS = 8
HEAD_DIM = 64
NA_WIDTH = NA_HEADS * HEAD_DIM
WIN_ROWS = 8
WIN_COLS = 16
F_GROUPS = 8
F_GROUP_DIM = 64
F_WIDTH = F_GROUPS * F_GROUP_DIM
IN_WIDTH = 3 * NA_WIDTH + F_WIDTH
N_EXPERTS = 16
N_GROUPS = 4
EXPERTS_PER_GROUP = N_EXPERTS // N_GROUPS
D_EXPERT = 256
N_MOD = 6
DEEPNORM_ALPHA = (2.0 * DEPTH) ** 0.25
LN_EPS = 1e-6

F32 = jnp.float32
BF16 = jnp.bfloat16

V7X_VMEM_BYTES = 64 * 1024 * 1024
VMEM_LIMIT_BYTES = (V7X_VMEM_BYTES * 3) // 4
LANES = 128
SUBLANES = 8
HEADS_PER_STEP = LANES // HEAD_DIM
assert HEADS_PER_STEP == 2
MASK_VALUE = -1e30
LOG2_E = math.log2(math.e)

CHUNK_GRID_ROWS = 4
WINDOW_CHUNKS = 3
STEP_CHUNKS = 8


def _cparams(n_grid_dims):
    return pltpu.CompilerParams(dimension_semantics=("arbitrary",) * n_grid_dims,
                                vmem_limit_bytes=VMEM_LIMIT_BYTES)


def _layer_norm(x):
    mu = jnp.mean(x, axis=-1, keepdims=True)
    xc = x - mu
    var = jnp.mean(xc * xc, axis=-1, keepdims=True)
    return xc * lax.rsqrt(var + LN_EPS)


def _dot(a, b):
    return jnp.dot(a, b, preferred_element_type=F32)


def _dot_nt(a, b):
    return lax.dot_general(a, b, (((1,), (1,)), ((), ())), preferred_element_type=F32)


def _mod_kernel(c_ref, w_ref, b_ref, o_ref):
    c = c_ref[...]
    a = c * jax.nn.sigmoid(c)
    o_ref[0] = jnp.dot(a, w_ref[0], preferred_element_type=F32, precision=lax.Precision.HIGHEST) + b_ref[0]


def _modulation(cvec, w_mod, b_mod):
    n_col_blocks = 4
    wc = (N_MOD * D_MODEL) // n_col_blocks
    rows = cvec.shape[0]
    return pl.pallas_call(
        _mod_kernel,
        grid=(DEPTH, n_col_blocks),
        in_specs=[pl.BlockSpec((rows, D_MODEL), lambda i, j: (0, 0)),
                  pl.BlockSpec((1, D_MODEL, wc), lambda i, j: (i, 0, j)),
                  pl.BlockSpec((1, 1, wc), lambda i, j: (i, 0, j))],
        out_specs=pl.BlockSpec((1, rows, wc), lambda i, j: (i, 0, j)),
        out_shape=jax.ShapeDtypeStruct((DEPTH, rows, N_MOD * D_MODEL), F32),
        compiler_params=_cparams(2),
        name="modulation",
    )(cvec, w_mod, b_mod.reshape(DEPTH, 1, N_MOD * D_MODEL))


def _modulate(x, m_ref):
    return (_layer_norm(x) * (1.0 + m_ref[0, 1:2, :]) + m_ref[0, 0:1, :]).astype(BF16)


def _project(h, rows, w_ref, q_ref, k_ref, v_ref, f_ref):
    p = _dot(h, w_ref[...])
    q_ref[0, rows, :] = (p[:, :NA_WIDTH] * (HEAD_DIM ** -0.5 * LOG2_E)).astype(BF16)
    k_ref[0, rows, :] = p[:, NA_WIDTH:2 * NA_WIDTH].astype(BF16)
    v_ref[0, rows, :] = p[:, 2 * NA_WIDTH:3 * NA_WIDTH].astype(BF16)
    f_ref[0, rows, :] = p[:, 3 * NA_WIDTH:]


def _proj_kernel(x_ref, m_ref, w_ref, q_ref, k_ref, v_ref, f_ref):
    tm = x_ref.shape[1]
    halves = [slice(0, tm // 2), slice(tm // 2, tm)] if tm % 512 == 0 else [slice(0, tm)]
    hs = [_modulate(x_ref[0, rows, :], m_ref) for rows in halves]
    for rows, h in zip(halves, hs):
        _project(h, rows, w_ref, q_ref, k_ref, v_ref, f_ref)


def _norm_proj_kernel(back_ref, y_hbm, lslot_ref, x1_ref, m_ref, g_ref, b_ref, m_next_ref, w_ref,
                      x_ref, q_ref, k_ref, v_ref, f_ref, ybuf, sems, *, steps_per_batch, n_steps, lr):
    step = pl.program_id(0) * steps_per_batch + pl.program_id(1)
    ys = _gather_unsort(back_ref, lslot_ref, y_hbm, ybuf, sems, step, n_steps, lr)
    tm = x1_ref.shape[1] // len(ys)
    hs = []
    for t, y in enumerate(ys):
        rows = slice(t * tm, (t + 1) * tm)
        z = DEEPNORM_ALPHA * x1_ref[0, rows, :] + m_ref[0, 5:6, :] * y
        x = _layer_norm(z) * g_ref[...] + b_ref[...]
        x_ref[0, rows, :] = x
        hs.append(_modulate(x, m_next_ref))
    for t, h in enumerate(hs):
        _project(h, slice(t * tm, (t + 1) * tm), w_ref, q_ref, k_ref, v_ref, f_ref)


def _norm_in_projection(y_slots, back_chunk, lslot, x1, m, ln_g, ln_b, m_next, w_in_bf16, tm):
    bsz, length, _ = x1.shape
    tiles_per_batch = length // tm
    per_step = 2 if tiles_per_batch % 2 == 0 else 1
    steps_per_batch = tiles_per_batch // per_step
    ts = per_step * tm
    lr = _local_rows(tm)
    out = jax.ShapeDtypeStruct((bsz, length, NA_WIDTH), BF16)
    out_f = jax.ShapeDtypeStruct((bsz, length, F_WIDTH), F32)
    out_x = jax.ShapeDtypeStruct((bsz, length, D_MODEL), F32)
    row = lambda b, i, back_ref: (b, i, 0)
    const2 = lambda b, i, back_ref: (0, 0)
    mod_spec = pl.BlockSpec((1, N_MOD, D_MODEL), lambda b, i, back_ref: (b, 0, 0))
    o_spec = pl.BlockSpec((1, ts, NA_WIDTH), row)
    x_spec = pl.BlockSpec((1, ts, D_MODEL), row)
    grid_spec = pltpu.PrefetchScalarGridSpec(
        num_scalar_prefetch=1,
        grid=(bsz, steps_per_batch),
        in_specs=[pl.BlockSpec(memory_space=pl.ANY),
                  pl.BlockSpec((1, SUBLANES, ts), lambda b, i, back_ref: (b, 0, i)),
                  x_spec, mod_spec,
                  pl.BlockSpec((1, D_MODEL), const2), pl.BlockSpec((1, D_MODEL), const2),
                  mod_spec, pl.BlockSpec((D_MODEL, IN_WIDTH), const2)],
        out_specs=[x_spec, o_spec, o_spec, o_spec, o_spec],
        scratch_shapes=[pltpu.VMEM((2, per_step * lr, D_MODEL), BF16), pltpu.SemaphoreType.DMA((2,))],
    )
    return pl.pallas_call(
        functools.partial(_norm_proj_kernel, steps_per_batch=steps_per_batch, n_steps=bsz * steps_per_batch,
                          lr=lr),
        grid_spec=grid_spec,
        out_shape=[out_x, out, out, out, out_f],
        compiler_params=_cparams(2),
        name="moe_norm_in_proj",
    )(back_chunk, y_slots, lslot, x1, m, ln_g.reshape(1, D_MODEL), ln_b.reshape(1, D_MODEL), m_next, w_in_bf16)


def _in_projection(x, m, w_in_bf16, tm):
    bsz, length, _ = x.shape
    out = jax.ShapeDtypeStruct((bsz, length, NA_WIDTH), BF16)
    out_f = jax.ShapeDtypeStruct((bsz, length, F_WIDTH), F32)
    o_spec = pl.BlockSpec((1, tm, NA_WIDTH), lambda b, i: (b, i, 0))
    return pl.pallas_call(
        _proj_kernel,
        grid=(bsz, length // tm),
        in_specs=[pl.BlockSpec((1, tm, D_MODEL), lambda b, i: (b, i, 0)),
                  pl.BlockSpec((1, N_MOD, D_MODEL), lambda b, i: (b, 0, 0)),
                  pl.BlockSpec((D_MODEL, IN_WIDTH), lambda b, i: (0, 0))],
        out_specs=[o_spec, o_spec, o_spec, o_spec],
        out_shape=[out, out, out, out_f],
        compiler_params=_cparams(2),
        name="ln_mod_in_proj",
    )(x, m, w_in_bf16)


def _head_lanes(h):
    lane = lax.broadcasted_iota(jnp.int32, (1, LANES), 1)
    return (lane >= HEAD_DIM * h) & (lane < HEAD_DIM * (h + 1))


def _scores_pass(q, h, tiles, s_ref):
    qh = jnp.where(_head_lanes(h), q, jnp.zeros_like(q))
    m = None
    t = tiles[0][0].shape[0]
    for j, (k, _, bias) in enumerate(tiles):
        s = _dot_nt(qh, k)
        if bias is not None:
            s = s + bias
        s_ref[:, j * t:(j + 1) * t] = s
        mj = jnp.max(s, axis=-1, keepdims=True)
        m = mj if m is None else jnp.maximum(m, mj)
    return m


def _pv_pass(h, tiles, s_ref, m):
    o = None
    t = tiles[0][0].shape[0]
    in_head = _head_lanes(h)
    for j, (_, v, _) in enumerate(tiles):
        p = jnp.exp2(s_ref[:, j * t:(j + 1) * t] - m)
        oj = _dot(p.astype(BF16), jnp.where(in_head, v, jnp.ones_like(v)))
        o = oj if o is None else o + oj
    return o / pltpu.roll(o, HEAD_DIM, axis=1)


def _attention_units(units):
    outs = []
    maxima = [_scores_pass(*units[0])]
    for u in range(len(units)):
        if u + 1 < len(units):
            maxima.append(_scores_pass(*units[u + 1]))
        _, h, tiles, s_ref = units[u]
        outs.append(_pv_pass(h, tiles, s_ref, maxima[u]))
    return outs


def _merge_heads(outs):
    merged = outs[0]
    for h in range(1, len(outs)):
        merged = jnp.where(_head_lanes(h), outs[h], merged)
    return merged


DR_PAD = 2 * WIN_ROWS
DC_PAD = 2 * WIN_COLS


def _toeplitz_kernel(r_ref, sel_ref, mask_ref, o_ref):
    n = r_ref.shape[0]
    for qc in range(GRID_W):
        block = jnp.dot(r_ref[...], sel_ref[qc], preferred_element_type=F32, precision=lax.Precision.HIGHEST)
        o_ref[pl.ds(qc, n, stride=GRID_W), :] = block * LOG2_E + mask_ref[qc]


def _rpb_toeplitz(rpb):
    depth, heads, n_dr, n_dc = rpb.shape
    qc = np.arange(GRID_W)[:, None]
    kc = (np.arange(LANES) % GRID_W)[None, :]
    cs = np.clip(qc - WIN_COLS // 2, 0, GRID_W - WIN_COLS)
    col_valid = (kc >= cs) & (kc < cs + WIN_COLS)
    dc = kc - qc + WIN_COLS - 1
    select = (np.arange(DC_PAD)[None, :, None] == dc[:, None, :]) & col_valid[:, None, :]
    mask = np.where(col_valid, 0.0, MASK_VALUE)[:, None, :]
    r = jnp.pad(rpb.astype(F32), ((0, 0), (0, 0), (0, DR_PAD - n_dr), (0, DC_PAD - n_dc)))
    n = depth * heads * DR_PAD
    const3 = lambda: (0, 0, 0)
    out = pl.pallas_call(
        _toeplitz_kernel,
        grid=(),
        in_specs=[pl.BlockSpec((n, DC_PAD), lambda: (0, 0)),
                  pl.BlockSpec((GRID_W, DC_PAD, LANES), const3), pl.BlockSpec((GRID_W, 1, LANES), const3)],
        out_specs=pl.BlockSpec((n * GRID_W, LANES), lambda: (0, 0)),
        out_shape=jax.ShapeDtypeStruct((n * GRID_W, LANES), F32),
        compiler_params=pltpu.CompilerParams(vmem_limit_bytes=VMEM_LIMIT_BYTES),
        name="rpb_toeplitz",
    )(r.reshape(n, DC_PAD), jnp.asarray(select, F32), jnp.asarray(mask, F32))
    return out.reshape(depth * heads, DR_PAD, GRID_W, LANES)


assert WINDOW_CHUNKS * CHUNK_GRID_ROWS >= CHUNK_GRID_ROWS + WIN_ROWS - 1


def _window_start_chunk(cr, n_cr):
    lo = cr - (WINDOW_CHUNKS - 1) // 2
    if isinstance(cr, int):
        return max(0, min(lo, n_cr - WINDOW_CHUNKS))
    return jnp.clip(lo, 0, n_cr - WINDOW_CHUNKS)


def _window_key_rows(cr, n_cr):
    start = _window_start_chunk(cr, n_cr) * CHUNK_GRID_ROWS
    return list(range(start, start + WINDOW_CHUNKS * CHUNK_GRID_ROWS))


def _row_window(qr, rows):
    kh = min(WIN_ROWS, rows)
    rs = int(np.clip(qr - kh // 2, 0, rows - kh))
    return rs, rs + kh


def _window_variant(cr, n_cr):
    return jnp.where(cr == 0, 0, jnp.where(cr == n_cr - 1, 2, 1))


def _check_windows(rows):
    def relative(cr):
        base = cr * CHUNK_GRID_ROWS
        return ([kr - base for kr in _window_key_rows(cr, n_cr)],
                [tuple(r - base for r in _row_window(base + qi, rows)) for qi in range(CHUNK_GRID_ROWS)])

    n_cr = rows // CHUNK_GRID_ROWS
    for cr in range(n_cr):
        assert cr in (0, n_cr - 1) or relative(cr) == relative(1), cr
        have = set(_window_key_rows(cr, n_cr))
        for qi in range(CHUNK_GRID_ROWS):
            lo, hi = _row_window(cr * CHUNK_GRID_ROWS + qi, rows)
            assert set(range(lo, hi)) <= have, (cr, qi)


def _build_bias_tables(t_ref, bias_scr, rows):
    n_cr = rows // CHUNK_GRID_ROWS
    left = lax.broadcasted_iota(jnp.int32, (GRID_W, LANES), 1) < GRID_W
    masked = jnp.full((GRID_W, LANES), MASK_VALUE, F32)
    for variant, cr in enumerate((0, 1, n_cr - 1)):
        key_rows = _window_key_rows(cr, n_cr)
        for h in range(HEADS_PER_STEP):
            for qi in range(CHUNK_GRID_ROWS):
                qr = cr * CHUNK_GRID_ROWS + qi
                lo, hi = _row_window(qr, rows)
                for p in range(len(key_rows) // 2):
                    pair = [t_ref[h, kr - qr + WIN_ROWS - 1] if lo <= kr < hi else None
                            for kr in key_rows[2 * p:2 * p + 2]]
                    if pair[0] is None and pair[1] is None:
                        block = masked
                    else:
                        block = jnp.where(left, masked if pair[0] is None else pair[0],
                                          masked if pair[1] is None else pair[1])
                    bias_scr[variant, h, qi * GRID_W:(qi + 1) * GRID_W, p * LANES:(p + 1) * LANES] = block


def _na_kernel(q_ref, *refs, rows):
    n_kv = STEP_CHUNKS * WINDOW_CHUNKS
    k_refs, v_refs = refs[:n_kv], refs[n_kv:2 * n_kv]
    kc_ref, vc_ref, t_ref, o_ref, bias_scr, s_scr = refs[2 * n_kv:]
    n_cr = rows // CHUNK_GRID_ROWS
    b, rb = pl.program_id(1), pl.program_id(2)

    @pl.when((b == 0) & (rb == 0))
    def _():
        _build_bias_tables(t_ref, bias_scr, rows)

    tq = CHUNK_GRID_ROWS * GRID_W
    units = []
    for c in range(STEP_CHUNKS):
        variant = _window_variant(rb * STEP_CHUNKS + c, n_cr)
        q = q_ref[0, c * tq:(c + 1) * tq, :]
        for h in range(HEADS_PER_STEP):
            lat = [(k_refs[c * WINDOW_CHUNKS + j][0], v_refs[c * WINDOW_CHUNKS + j][0],
                    bias_scr[variant, h, :, j * tq:(j + 1) * tq]) for j in range(WINDOW_CHUNKS)]
            units.append((q, h, lat + [(kc_ref[0], vc_ref[0], None)], s_scr.at[c, h]))
    outs = _attention_units(units)
    for c in range(STEP_CHUNKS):
        o = _merge_heads(outs[c * HEADS_PER_STEP:(c + 1) * HEADS_PER_STEP])
        o_ref[0, c * tq:(c + 1) * tq, :] = o.astype(BF16)


def _neighborhood_attention(q, k, v, kc, vc, toeplitz, layer):
    bsz, length, _ = q.shape
    rows = length // GRID_W
    n_cr = rows // CHUNK_GRID_ROWS
    assert rows % (CHUNK_GRID_ROWS * STEP_CHUNKS) == 0 and n_cr >= WINDOW_CHUNKS + 2
    _check_windows(rows)
    n_ctx = kc.shape[1]
    tq = CHUNK_GRID_ROWS * GRID_W
    assert n_ctx == tq

    def kv_spec(c, j):
        return pl.BlockSpec((1, tq, LANES),
                            lambda hp, b, rb: (b, _window_start_chunk(rb * STEP_CHUNKS + c, n_cr) + j, hp))

    kv_specs = [kv_spec(c, j) for c in range(STEP_CHUNKS) for j in range(WINDOW_CHUNKS)]
    ctx_spec = pl.BlockSpec((1, n_ctx, LANES), lambda hp, b, rb: (b, 0, hp))
    q_spec = pl.BlockSpec((1, STEP_CHUNKS * tq, LANES), lambda hp, b, rb: (b, rb, hp))
    window = WINDOW_CHUNKS * tq
    return pl.pallas_call(
        functools.partial(_na_kernel, rows=rows),
        grid=(NA_HEADS // HEADS_PER_STEP, bsz, n_cr // STEP_CHUNKS),
        in_specs=([q_spec] + kv_specs + kv_specs
                  + [ctx_spec, ctx_spec,
                     pl.BlockSpec((HEADS_PER_STEP, DR_PAD, GRID_W, LANES),
                                  lambda hp, b, rb: (layer * (NA_HEADS // HEADS_PER_STEP) + hp, 0, 0, 0))]),
        out_specs=q_spec,
        out_shape=jax.ShapeDtypeStruct((bsz, length, NA_WIDTH), BF16),
        scratch_shapes=[pltpu.VMEM((3, HEADS_PER_STEP, tq, window), F32),
                        pltpu.VMEM((STEP_CHUNKS, HEADS_PER_STEP, tq, window + n_ctx), F32)],
        compiler_params=_cparams(3),
        name="neighborhood_attention",
    )(q, *([k] * len(kv_specs)), *([v] * len(kv_specs)), kc, vc, toeplitz)


def _ctx_attn_kernel(q_ref, k_ref, v_ref, o_ref, s_scr):
    tiles = [(k_ref[0], v_ref[0], None)]
    outs = _attention_units([(q_ref[0], h, tiles, s_scr.at[h]) for h in range(HEADS_PER_STEP)])
    o_ref[0] = _merge_heads(outs).astype(BF16)


def _context_attention(qc, kc, vc):
    bsz, n_ctx, _ = qc.shape
    spec = pl.BlockSpec((1, n_ctx, LANES), lambda b, hp: (b, 0, hp))
    return pl.pallas_call(
        _ctx_attn_kernel,
        grid=(bsz, NA_HEADS // HEADS_PER_STEP),
        in_specs=[spec, spec, spec],
        out_specs=spec,
        out_shape=jax.ShapeDtypeStruct((bsz, n_ctx, NA_WIDTH), BF16),
        scratch_shapes=[pltpu.VMEM((HEADS_PER_STEP, n_ctx, n_ctx), F32)],
        compiler_params=_cparams(2),
        name="context_attention",
    )(qc, kc, vc)


def _dft_cos_sin(n):
    ang = 2.0 * np.pi * np.outer(np.arange(n), np.arange(n)) / n
    return np.cos(ang), np.sin(ang)


def _bf16_table(a):
    return jnp.asarray(a, F32).astype(BF16)


def _channel_dft_matrix(n_groups=F_GROUPS):
    c, s = _dft_cos_sin(F_GROUP_DIM)
    scale = F_GROUP_DIM ** -0.5
    eye = np.eye(n_groups)
    return np.concatenate([np.kron(eye, c), np.kron(eye, s)], axis=1) * scale


def _fft_stage1_kernel(f_ref, perm_ref, w1_ref, cs_ref, sc_ref, tc_ref, ts_ref, zr_ref, zi_ref, *, n_slow, nt):
    x = f_ref[0].reshape(n_slow * nt, F_WIDTH).astype(BF16)
    x = _dot(perm_ref[...], x).astype(BF16)
    ab = [_dot(x[:, p * LANES:(p + 1) * LANES], w1_ref[...]).astype(BF16) for p in range(F_WIDTH // LANES)]
    a_all = jnp.concatenate([blk[:, :LANES] for blk in ab], axis=-1)
    b_all = jnp.concatenate([blk[:, LANES:] for blk in ab], axis=-1)
    for t in range(nt):
        a = a_all[t * n_slow:(t + 1) * n_slow]
        b = b_all[t * n_slow:(t + 1) * n_slow]
        z = _dot(cs_ref[...], a) + _dot(sc_ref[...], b)
        zr, zi = z[:n_slow], z[n_slow:]
        c, s = tc_ref[t], ts_ref[t]
        zr_ref[0, t] = (zr * c - zi * s).astype(BF16)
        zi_ref[0, t] = (zr * s + zi * c).astype(BF16)


def _fft_stage2_kernel(zr_ref, zi_ref, f_ref, y_ref, *, n_fast, kb):
    sub = f_ref.shape[0] // n_fast
    parts = []
    for j0 in range(0, kb, sub):
        rhs = jnp.concatenate([zr_ref[0, :, j0:j0 + sub, :].reshape(n_fast * sub, F_WIDTH),
                               zi_ref[0, :, j0:j0 + sub, :].reshape(n_fast * sub, F_WIDTH)], axis=0)
        parts.append(_dot(f_ref[...], rhs).astype(BF16).reshape(n_fast, sub, F_WIDTH))
    y_ref[0] = jnp.concatenate(parts, axis=1)


def _fourier_positions(f, n_slow, n_fast):
    bsz, n, _ = f.shape
    assert n == n_slow * n_fast
    nt = SUBLANES
    kb = 2 * SUBLANES
    w1 = _bf16_table(_channel_dft_matrix(LANES // F_GROUP_DIM))
    perm = _bf16_table(np.eye(n_slow * nt).reshape(n_slow, nt, n_slow * nt).transpose(1, 0, 2)
                       .reshape(n_slow * nt, n_slow * nt))
    c1, s1 = _dft_cos_sin(n_slow)
    sc1 = n_slow ** -0.5
    cs = _bf16_table(np.concatenate([c1, s1], axis=0) * sc1)
    sc = _bf16_table(np.concatenate([-s1, c1], axis=0) * sc1)
    tw = 2.0 * np.pi * np.outer(np.arange(n_fast), np.arange(n_slow)) / n
    tc = jnp.asarray(np.cos(tw)[:, :, None], F32)
    ts = jnp.asarray(np.sin(tw)[:, :, None], F32)
    z_shape = jax.ShapeDtypeStruct((bsz, n_fast, n_slow, F_WIDTH), BF16)
    z_spec = pl.BlockSpec((1, nt, n_slow, F_WIDTH), lambda b, j: (b, j, 0, 0))
    const2 = lambda b, j: (0, 0)
    tw_spec = pl.BlockSpec((nt, n_slow, 1), lambda b, j: (j, 0, 0))
    zr, zi = pl.pallas_call(
        functools.partial(_fft_stage1_kernel, n_slow=n_slow, nt=nt),
        grid=(bsz, n_fast // nt),
        in_specs=[pl.BlockSpec((1, n_slow, nt, F_WIDTH), lambda b, j: (b, 0, j, 0)),
                  pl.BlockSpec(perm.shape, const2),
                  pl.BlockSpec(w1.shape, const2), pl.BlockSpec(cs.shape, const2), pl.BlockSpec(sc.shape, const2),
                  tw_spec, tw_spec],
        out_specs=[z_spec, z_spec],
        out_shape=[z_shape, z_shape],
        compiler_params=_cparams(2),
        name="fnet_stage1",
    )(f.reshape(bsz, n_slow, n_fast, F_WIDTH), perm, w1, cs, sc, tc, ts)

    c2, s2 = _dft_cos_sin(n_fast)
    sc2 = n_fast ** -0.5
    eye = np.eye(SUBLANES)
    f2 = _bf16_table(np.concatenate([np.kron(c2, eye), np.kron(-s2, eye)], axis=1) * sc2)
    blk = pl.BlockSpec((1, n_fast, kb, F_WIDTH), lambda b, j: (b, 0, j, 0))
    y = pl.pallas_call(
        functools.partial(_fft_stage2_kernel, n_fast=n_fast, kb=kb),
        grid=(bsz, n_slow // kb),
        in_specs=[blk, blk, pl.BlockSpec(f2.shape, const2)],
        out_specs=blk,
        out_shape=jax.ShapeDtypeStruct((bsz, n_fast, n_slow, F_WIDTH), BF16),
        compiler_params=_cparams(2),
        name="fnet_stage2",
    )(zr, zi, f2)
    return y.reshape(bsz, n, F_WIDTH)


def _second_largest_sum(a, b, c, d):
    mab, nab = jnp.maximum(a, b), jnp.minimum(a, b)
    mcd, ncd = jnp.maximum(c, d), jnp.minimum(c, d)
    return jnp.maximum(mab, mcd) + jnp.maximum(jnp.minimum(mab, mcd), jnp.maximum(nab, ncd))


def _selected_group(sb_rows):
    epg = EXPERTS_PER_GROUP
    g_score = [_second_largest_sum(*sb_rows[g * epg:(g + 1) * epg]) for g in range(N_GROUPS)]
    best = functools.reduce(jnp.maximum, g_score)
    group = jnp.full_like(best, float(N_GROUPS - 1))
    for g in range(N_GROUPS - 2, -1, -1):
        group = jnp.where(g_score[g] == best, float(g), group)
    return group


def _top2_gates(cand_s, cand_sb):
    n = len(cand_s)
    w = []
    for j in range(n):
        rank = jnp.zeros_like(cand_sb[j])
        for i in range(n):
            if i == j:
                continue
            ahead = (cand_sb[i] > cand_sb[j]) | ((cand_sb[i] == cand_sb[j]) & (i < j))
            rank = rank + jnp.where(ahead, 1.0, 0.0)
        w.append(jnp.where(rank < 2.0, cand_s[j], 0.0))
    total = functools.reduce(jnp.add, w)
    return [wj / total for wj in w]


CHUNK_ROWS = 16


def _local_rows(tm):
    need = tm + N_GROUPS * (CHUNK_ROWS - 1) + CHUNK_ROWS
    return -(-need // CHUNK_ROWS) * CHUNK_ROWS


def _sort_matrix(lslot, lr):
    r_iota = lax.broadcasted_iota(jnp.int32, (lr, 1), 0).astype(F32)
    return jnp.where(r_iota == lslot, 1.0, 0.0).astype(BF16)


def _local_sort(group, h2b, tri_ref, lr):
    t = group.shape[1]
    g_iota = lax.broadcasted_iota(jnp.int32, (SUBLANES, 1), 0).astype(F32)
    onehot = g_iota == group
    prefix = _dot(jnp.where(onehot, 1.0, 0.0).astype(BF16), tri_ref[...])
    count = prefix[:, t - 1:t]
    padded = jnp.floor((count + (CHUNK_ROWS - 1.0)) * (1.0 / CHUNK_ROWS)) * CHUNK_ROWS
    lslot = jnp.zeros((1, t), F32)
    start = jnp.zeros((1, 1), F32)
    for g in range(N_GROUPS):
        lslot = jnp.where(onehot[g:g + 1], start + prefix[g:g + 1] - 1.0, lslot)
        start = start + padded[g:g + 1]
    return lslot, count, _dot(_sort_matrix(lslot, lr), h2b).astype(BF16)


def _out_kernel(a_ref, y_ref, x_ref, m_ref, wo_ref, bd_ref, g_ref, b_ref, wr_ref, rb_ref, tri_ref,
                x1_ref, h2_ref, lslot_ref, count_ref):
    tm = tri_ref.shape[0]
    lr = h2_ref.shape[0] // count_ref.shape[0]

    def project(s):
        rows = slice(s * tm, (s + 1) * tm)
        y2 = _dot(y_ref[0, rows, :].astype(BF16), bd_ref[...]).astype(BF16)
        return _dot(a_ref[0, rows, :], wo_ref[:NA_WIDTH, :]) + _dot(y2, wo_ref[NA_WIDTH:, :])

    def finish(s, o):
        rows = slice(s * tm, (s + 1) * tm)
        z = DEEPNORM_ALPHA * x_ref[0, rows, :] + m_ref[0, 2:3, :] * o
        x1 = _layer_norm(z) * g_ref[...] + b_ref[...]
        x1_ref[0, rows, :] = x1
        h2 = (_layer_norm(x1) * (1.0 + m_ref[0, 4:5, :]) + m_ref[0, 3:4, :]).astype(BF16)
        sb = jax.nn.sigmoid(_dot_nt(wr_ref[...], h2)) + rb_ref[...]
        group = _selected_group([sb[e:e + 1] for e in range(N_EXPERTS)])
        lslot, count, h2_sorted = _local_sort(group, h2, tri_ref, lr)
        h2_ref[s * lr:(s + 1) * lr, :] = h2_sorted
        lslot_ref[0, :, rows] = jnp.broadcast_to(lslot, (lslot_ref.shape[1], tm))
        count_ref[s] = jnp.broadcast_to(count, count_ref.shape[1:])

    n_sub = count_ref.shape[0]
    projected = project(0)
    for s in range(n_sub):
        following = project(s + 1) if s + 1 < n_sub else None
        finish(s, projected)
        projected = following


def _out_projection(attn, yf, x, m, w_out_bf16, w_four_bd, ln_g, ln_b, w_router_t, router_bias, tm):
    bsz, length, _ = x.shape
    tiles_per_batch = length // tm
    n_tiles = bsz * tiles_per_batch
    per_step = 2 if tiles_per_batch % 2 == 0 else 1
    steps_per_batch = tiles_per_batch // per_step
    ts = per_step * tm
    lr = _local_rows(tm)
    tri = _bf16_table(np.triu(np.ones((tm, tm))))
    row = lambda b, i: (b, i, 0)
    const2 = lambda b, i: (0, 0)
    return pl.pallas_call(
        _out_kernel,
        grid=(bsz, steps_per_batch),
        in_specs=[pl.BlockSpec((1, ts, NA_WIDTH), row), pl.BlockSpec((1, ts, F_WIDTH), row),
                  pl.BlockSpec((1, ts, D_MODEL), row),
                  pl.BlockSpec((1, N_MOD, D_MODEL), lambda b, i: (b, 0, 0)),
                  pl.BlockSpec((NA_WIDTH + F_WIDTH, D_MODEL), const2),
                  pl.BlockSpec((F_WIDTH, F_WIDTH), const2),
                  pl.BlockSpec((1, D_MODEL), const2), pl.BlockSpec((1, D_MODEL), const2),
                  pl.BlockSpec((N_EXPERTS, D_MODEL), const2), pl.BlockSpec((N_EXPERTS, 1), const2),
                  pl.BlockSpec((tm, tm), const2)],
        out_specs=[pl.BlockSpec((1, ts, D_MODEL), row),
                   pl.BlockSpec((per_step * lr, D_MODEL), lambda b, i: (b * steps_per_batch + i, 0)),
                   pl.BlockSpec((1, SUBLANES, ts), lambda b, i: (b, 0, i)),
                   pl.BlockSpec((per_step, SUBLANES, LANES), lambda b, i: (b * steps_per_batch + i, 0, 0))],
        out_shape=[jax.ShapeDtypeStruct((bsz, length, D_MODEL), F32),
                   jax.ShapeDtypeStruct((n_tiles * lr, D_MODEL), BF16),
                   jax.ShapeDtypeStruct((bsz, SUBLANES, length), F32),
                   jax.ShapeDtypeStruct((n_tiles, SUBLANES, LANES), F32)],
        compiler_params=_cparams(2),
        name="out_proj_norm_route",
    )(attn, yf, x, m, w_out_bf16, w_four_bd, ln_g.reshape(1, D_MODEL), ln_b.reshape(1, D_MODEL),
      w_router_t, router_bias.reshape(N_EXPERTS, 1), tri)


def _run_plan(count, tm, lr, tm_slots):
    n_tiles = count.shape[0]
    ch = CHUNK_ROWS
    n_slots = -(-(n_tiles * (tm + N_GROUPS * (ch - 1)) + N_GROUPS * tm_slots) // tm_slots) * tm_slots
    zero_chunk = lr // ch - 1
    pad = ((count + ch - 1) // ch) * ch
    lstart = jnp.cumsum(pad, axis=1) - pad
    run_off = jnp.cumsum(pad, axis=0) - pad
    seg_len = jnp.sum(pad, axis=0)
    seg_pad = ((seg_len + tm_slots - 1) // tm_slots) * tm_slots
    seg_end = jnp.cumsum(seg_pad)
    seg_start = seg_end - seg_pad

    s = (jnp.arange(n_slots // ch, dtype=jnp.int32) * ch)[:, None, None]
    in_run = (seg_start + run_off <= s) & (s < seg_start + run_off + pad)
    src_row = jnp.arange(n_tiles, dtype=jnp.int32)[None, :, None] * lr + lstart + (s - seg_start - run_off)
    src_row = jnp.sum(jnp.where(in_run, src_row, 0), axis=(1, 2))
    src_chunk = jnp.where(jnp.any(in_run, axis=(1, 2)), src_row // ch, zero_chunk).astype(jnp.int32)

    tile_start = jnp.arange(n_slots // tm_slots, dtype=jnp.int32) * tm_slots
    tile_group = jnp.minimum(jnp.sum((seg_end[None, :] <= tile_start[:, None]).astype(jnp.int32), axis=1),
                             N_GROUPS - 1)
    n_used = (seg_end[N_GROUPS - 1:] // tm_slots).astype(jnp.int32)

    r = (jnp.arange(lr // ch, dtype=jnp.int32) * ch)[None, :, None]
    in_local = (lstart[:, None, :] <= r) & (r < (lstart + pad)[:, None, :])
    slot = (seg_start + run_off - lstart)[:, None, :] + r
    back_chunk = (jnp.sum(jnp.where(in_local, slot, 0), axis=2) // ch).astype(jnp.int32)
    return src_chunk, tile_group.astype(jnp.int32), n_used, back_chunk.reshape(-1)


def _chunk_gather(idx_ref, tile, n_chunks, src_hbm, buf, sems, wait):
    slot = tile % 2
    rows = n_chunks * CHUNK_ROWS
    if wait:
        pltpu.make_async_copy(src_hbm.at[pl.ds(0, rows), :], buf.at[slot], sems.at[slot]).wait()
        return
    for c in range(n_chunks):
        src = pl.multiple_of(idx_ref[tile * n_chunks + c] * CHUNK_ROWS, CHUNK_ROWS)
        pltpu.make_async_copy(src_hbm.at[pl.ds(src, CHUNK_ROWS), :],
                              buf.at[slot, pl.ds(c * CHUNK_ROWS, CHUNK_ROWS), :], sems.at[slot]).start()


def _moe_runs_kernel(src_ref, tile_group_ref, n_used_ref, h_hbm, wg32_ref, wu32_ref, wd32_ref, wr_ref, rb_ref,
                     y_ref, hbuf, wg_ref, wu_ref, wd_ref, sems, *, tm):
    step = pl.program_id(0)
    n_used = n_used_ref[0]
    n_chunks = tm // CHUNK_ROWS
    new_group = (step == 0) | (tile_group_ref[step] != tile_group_ref[jnp.maximum(step - 1, 0)])

    @pl.when(new_group & (step < n_used))
    def _():
        wg_ref[...] = wg32_ref[...].astype(BF16)
        wu_ref[...] = wu32_ref[...].astype(BF16)
        wd_ref[...] = wd32_ref[...].astype(BF16)

    @pl.when((step == 0) & (n_used > 0))
    def _():
        _chunk_gather(src_ref, 0, n_chunks, h_hbm, hbuf, sems, wait=False)

    @pl.when(step + 1 < n_used)
    def _():
        _chunk_gather(src_ref, step + 1, n_chunks, h_hbm, hbuf, sems, wait=False)

    @pl.when(step < n_used)
    def _():
        _chunk_gather(src_ref, step, n_chunks, h_hbm, hbuf, sems, wait=True)
        h = hbuf[step % 2]
        s = jax.nn.sigmoid(_dot(h, wr_ref[0]))
        sb = s + rb_ref[0]
        epg = EXPERTS_PER_GROUP
        gates = _top2_gates([s[:, j:j + 1] for j in range(epg)], [sb[:, j:j + 1] for j in range(epg)])
        acc = None
        gate_up = [(_dot(h, wg_ref[0]), _dot(h, wu_ref[0]))]
        for e in range(epg):
            if e + 1 < epg:
                gate_up.append((_dot(h, wg_ref[e + 1]), _dot(h, wu_ref[e + 1])))
            gate, up = gate_up[e]
            hid = (gate * jax.nn.sigmoid(gate)) * up * gates[e]
            y = _dot(hid.astype(BF16), wd_ref[e])
            acc = y if acc is None else acc + y
        y_ref[...] = acc.astype(BF16)

    @pl.when(step >= n_used)
    def _():
        y_ref[...] = jnp.zeros_like(y_ref)


def _moe_runs(h_sorted, src_chunk, tile_group, n_used, layer, w_gate, w_up, w_down, w_router_grp,
              router_bias_grp, tm):
    n_slots = src_chunk.shape[0] * CHUNK_ROWS
    epg = EXPERTS_PER_GROUP
    by_group = lambda i, src_ref, tg_ref, nu_ref: (tg_ref[i], 0, 0)
    by_layer_group = lambda i, src_ref, tg_ref, nu_ref: (layer * N_GROUPS + tg_ref[i], 0, 0)
    grid_spec = pltpu.PrefetchScalarGridSpec(
        num_scalar_prefetch=3,
        grid=(n_slots // tm,),
        in_specs=[pl.BlockSpec(memory_space=pl.ANY),
                  pl.BlockSpec((epg, D_MODEL, D_EXPERT), by_layer_group),
                  pl.BlockSpec((epg, D_MODEL, D_EXPERT), by_layer_group),
                  pl.BlockSpec((epg, D_EXPERT, D_MODEL), by_layer_group),
                  pl.BlockSpec((1, D_MODEL, LANES), by_group),
                  pl.BlockSpec((1, 1, LANES), by_group)],
        out_specs=pl.BlockSpec((tm, D_MODEL), lambda i, src_ref, tg_ref, nu_ref: (i, 0)),
        scratch_shapes=[pltpu.VMEM((2, tm, D_MODEL), BF16),
                        pltpu.VMEM((epg, D_MODEL, D_EXPERT), BF16), pltpu.VMEM((epg, D_MODEL, D_EXPERT), BF16),
                        pltpu.VMEM((epg, D_EXPERT, D_MODEL), BF16), pltpu.SemaphoreType.DMA((2,))],
    )
    return pl.pallas_call(
        functools.partial(_moe_runs_kernel, tm=tm),
        grid_spec=grid_spec,
        out_shape=jax.ShapeDtypeStruct((n_slots, D_MODEL), BF16),
        compiler_params=_cparams(1),
        name="moe_group_experts",
    )(src_chunk, tile_group, n_used, h_sorted, w_gate, w_up, w_down, w_router_grp, router_bias_grp)


def _gather_unsort(back_ref, lslot_ref, y_hbm, ybuf, sems, step, n_steps, lr):
    per_step = ybuf.shape[1] // lr
    tm = lslot_ref.shape[2] // per_step
    n_chunks = per_step * lr // CHUNK_ROWS

    @pl.when(step == 0)
    def _():
        _chunk_gather(back_ref, 0, n_chunks, y_hbm, ybuf, sems, wait=False)

    @pl.when(step + 1 < n_steps)
    def _():
        _chunk_gather(back_ref, step + 1, n_chunks, y_hbm, ybuf, sems, wait=False)

    _chunk_gather(back_ref, step, n_chunks, y_hbm, ybuf, sems, wait=True)
    outs = []
    for t in range(per_step):
        sort = _sort_matrix(lslot_ref[0, 0:1, t * tm:(t + 1) * tm], lr)
        outs.append(lax.dot_general(sort, ybuf[step % 2, t * lr:(t + 1) * lr, :], (((0,), (0,)), ((), ())),
                                    preferred_element_type=F32))
    return outs


def _residual_norm_kernel(back_ref, y_hbm, lslot_ref, x1_ref, m_ref, g_ref, b_ref, o_ref, ybuf, sems, *, n_steps, lr):
    ys = _gather_unsort(back_ref, lslot_ref, y_hbm, ybuf, sems, pl.program_id(0), n_steps, lr)
    tm = x1_ref.shape[0] // len(ys)
    for t, y in enumerate(ys):
        rows = slice(t * tm, (t + 1) * tm)
        z = DEEPNORM_ALPHA * x1_ref[rows, :] + m_ref[0, 5:6, :] * y
        o_ref[rows, :] = _layer_norm(z) * g_ref[...] + b_ref[...]


def _residual_norm(y_slots, back_chunk, lslot, x1, m, ln_g, ln_b, tm):
    bsz, length, _ = x1.shape
    n = bsz * length
    tiles_per_batch = length // tm
    per_step = 2 if tiles_per_batch % 2 == 0 else 1
    steps_per_batch = tiles_per_batch // per_step
    ts = per_step * tm
    lr = _local_rows(tm)
    const2 = lambda i, back_ref: (0, 0)
    row = pl.BlockSpec((ts, D_MODEL), lambda i, back_ref: (i, 0))
    grid_spec = pltpu.PrefetchScalarGridSpec(
        num_scalar_prefetch=1,
        grid=(n // ts,),
        in_specs=[pl.BlockSpec(memory_space=pl.ANY),
                  pl.BlockSpec((1, SUBLANES, ts),
                               lambda i, back_ref: (i // steps_per_batch, 0, i % steps_per_batch)),
                  row, pl.BlockSpec((1, N_MOD, D_MODEL), lambda i, back_ref: (i // steps_per_batch, 0, 0)),
                  pl.BlockSpec((1, D_MODEL), const2), pl.BlockSpec((1, D_MODEL), const2)],
        out_specs=row,
        scratch_shapes=[pltpu.VMEM((2, per_step * lr, D_MODEL), BF16), pltpu.SemaphoreType.DMA((2,))],
    )
    out = pl.pallas_call(
        functools.partial(_residual_norm_kernel, n_steps=n // ts, lr=lr),
        grid_spec=grid_spec,
        out_shape=jax.ShapeDtypeStruct((n, D_MODEL), F32),
        compiler_params=_cparams(1),
        name="moe_residual_norm",
    )(back_chunk, y_slots, lslot, x1.reshape(n, D_MODEL), m, ln_g.reshape(1, D_MODEL), ln_b.reshape(1, D_MODEL))
    return out.reshape(bsz, length, D_MODEL)


def _grouped_moe_runs(h_sorted, count_rows, layer, w_gate, w_up, w_down, w_router_grp, router_bias_grp,
                      tm_tokens, tm_slots):
    count = count_rows[:, :N_GROUPS, 0].astype(jnp.int32)
    src_chunk, tile_group, n_used, back_chunk = _run_plan(count, tm_tokens, _local_rows(tm_tokens), tm_slots)
    y_slots = _moe_runs(h_sorted, src_chunk, tile_group, n_used, layer, w_gate, w_up, w_down, w_router_grp,
                        router_bias_grp, tm_slots)
    return y_slots, back_chunk


def _ctx_fourier_kernel(f_ref, w1_ref, c_ref, s_ref, y_ref):
    ab = _dot(f_ref[0].astype(BF16), w1_ref[...]).astype(BF16)
    y = _dot(c_ref[...], ab[:, :F_WIDTH]) + _dot(s_ref[...], ab[:, F_WIDTH:])
    y_ref[0] = y.astype(BF16)


def _context_fourier(fc):
    bsz, n, _ = fc.shape
    w1 = _bf16_table(_channel_dft_matrix())
    c, s = _dft_cos_sin(n)
    cm = _bf16_table(c * n ** -0.5)
    sm = _bf16_table(-s * n ** -0.5)
    const2 = lambda b: (0, 0)
    blk = pl.BlockSpec((1, n, F_WIDTH), lambda b: (b, 0, 0))
    return pl.pallas_call(
        _ctx_fourier_kernel,
        grid=(bsz,),
        in_specs=[blk, pl.BlockSpec(w1.shape, const2), pl.BlockSpec(cm.shape, const2),
                  pl.BlockSpec(sm.shape, const2)],
        out_specs=blk,
        out_shape=jax.ShapeDtypeStruct((bsz, n, F_WIDTH), BF16),
        compiler_params=_cparams(1),
        name="context_fnet",
    )(fc, w1, cm, sm)


def _block_diag(w):
    g, c, _ = w.shape
    eye = jnp.eye(g, dtype=w.dtype)
    return (eye[:, None, :, None] * w[:, :, None, :]).reshape(g * c, g * c)


def kernel(x, c, ctx, c_ctx, w_mod, b_mod, w_in, rpb, w_four, w_out, ln1_g, ln1_b, ln2_g, ln2_b,
           w_router, router_bias, w_gate, w_up, w_down):
    bsz, length, _ = x.shape
    n_ctx = ctx.shape[1]
    rows = length // GRID_W

    cvec = jnp.concatenate([c, c_ctx[None, :], jnp.zeros((8 - bsz - 1, D_MODEL), F32)], axis=0)
    mods = _modulation(cvec, w_mod, b_mod)
    w_router_t = w_router.T.astype(BF16)
    toeplitz = _rpb_toeplitz(rpb)
    lane_pad = LANES - EXPERTS_PER_GROUP
    w_router_grp = jnp.pad(w_router.reshape(D_MODEL, N_GROUPS, EXPERTS_PER_GROUP).transpose(1, 0, 2),
                           ((0, 0), (0, 0), (0, lane_pad))).astype(BF16)
    router_bias_grp = jnp.pad(router_bias.astype(F32).reshape(N_GROUPS, 1, EXPERTS_PER_GROUP),
                              ((0, 0), (0, 0), (0, lane_pad)))

    wg = w_gate.reshape(DEPTH * N_EXPERTS, D_MODEL, D_EXPERT)
    wu = w_up.reshape(DEPTH * N_EXPERTS, D_MODEL, D_EXPERT)
    wd = w_down.reshape(DEPTH * N_EXPERTS, D_EXPERT, D_MODEL)

    def latent_mod(i):
        return mods[i, :bsz].reshape(bsz, N_MOD, D_MODEL)

    xc = ctx
    projected = None
    for i in range(DEPTH):
        last = i == DEPTH - 1
        m = latent_mod(i)
        mc = jnp.broadcast_to(mods[i, bsz].reshape(1, N_MOD, D_MODEL), (bsz, N_MOD, D_MODEL))
        w_in_b = w_in[i].astype(BF16)
        w_out_b = w_out[i].astype(BF16)
        w_four_bd = _block_diag(w_four[i]).astype(BF16)

        q, k, v, f = projected if projected is not None else _in_projection(x, m, w_in_b, tm=1024)
        qc, kc, vc, fc = _in_projection(xc, mc, w_in_b, tm=n_ctx)

        attn = _neighborhood_attention(q, k, v, kc, vc, toeplitz, i)
        yf = _fourier_positions(f, n_slow=rows, n_fast=GRID_W)
        x1, h2, lslot, cnt = _out_projection(attn, yf, x, m, w_out_b, w_four_bd, ln1_g[i], ln1_b[i],
                                             w_router_t, router_bias, tm=512)
        y, back = _grouped_moe_runs(h2, cnt, i, wg, wu, wd, w_router_grp, router_bias_grp,
                                    tm_tokens=512, tm_slots=512)
        if last:
            return _residual_norm(y, back, lslot, x1, m, ln2_g[i], ln2_b[i], tm=512)
        x, *projected = _norm_in_projection(y, back, lslot, x1, m, ln2_g[i], ln2_b[i], latent_mod(i + 1),
                                            w_in[i + 1].astype(BF16), tm=512)

        attn_c = _context_attention(qc, kc, vc)
        yc = _context_fourier(fc)
        xc1, h2c, lslot_c, cnt_c = _out_projection(attn_c, yc, xc, mc, w_out_b, w_four_bd, ln1_g[i], ln1_b[i],
                                                   w_router_t, router_bias, tm=n_ctx)
        yc2, back_c = _grouped_moe_runs(h2c, cnt_c, i, wg, wu, wd, w_router_grp, router_bias_grp,
                                        tm_tokens=n_ctx, tm_slots=128)
        xc = _residual_norm(yc2, back_c, lslot_c, xc1, mc, ln2_g[i], ln2_b[i], tm=n_ctx)
    return x
```

```python
import functools
import math

import numpy as np
import jax
import jax.numpy as jnp
from jax import lax
from jax.experimental import pallas as pl
from jax.experimental.pallas import tpu as pltpu

D_MODEL = 1024
DEPTH = 2
GRID_W = 64
NA_HEADS = 8
HEAD_DIM = 64
NA_WIDTH = NA_HEADS * HEAD_DIM
WIN_ROWS = 8
WIN_COLS = 16
F_GROUPS = 8
F_GROUP_DIM = 64
F_WIDTH = F_GROUPS * F_GROUP_DIM
IN_WIDTH = 3 * NA_WIDTH + F_WIDTH
N_EXPERTS = 16
N_GROUPS = 4
EXPERTS_PER_GROUP = N_EXPERTS // N_GROUPS
D_EXPERT = 256
N_MOD = 6
DEEPNORM_ALPHA = (2.0 * DEPTH) ** 0.25
LN_EPS = 1e-6

F32 = jnp.float32
BF16 = jnp.bfloat16

V7X_VMEM_BYTES = 64 * 1024 * 1024
VMEM_LIMIT_BYTES = (V7X_VMEM_BYTES * 3) // 4
LANES = 128
SUBLANES = 8
HEADS_PER_STEP = LANES // HEAD_DIM
assert HEADS_PER_STEP == 2
MASK_VALUE = -1e30
LOG2_E = math.log2(math.e)

CHUNK_GRID_ROWS = 4
WINDOW_CHUNKS = 3
STEP_CHUNKS = 8


def _cparams(n_grid_dims):
    return pltpu.CompilerParams(dimension_semantics=("arbitrary",) * n_grid_dims,
                                vmem_limit_bytes=VMEM_LIMIT_BYTES)


def _layer_norm(x):
    mu = jnp.mean(x, axis=-1, keepdims=True)
    xc = x - mu
    var = jnp.mean(xc * xc, axis=-1, keepdims=True)
    return xc * lax.rsqrt(var + LN_EPS)


def _dot(a, b):
    return jnp.dot(a, b, preferred_element_type=F32)


def _dot_nt(a, b):
    return lax.dot_general(a, b, (((1,), (1,)), ((), ())), preferred_element_type=F32)


def _mod_kernel(c_ref, w_ref, b_ref, o_ref):
    c = c_ref[...]
    a = c * jax.nn.sigmoid(c)
    o_ref[0] = jnp.dot(a, w_ref[0], preferred_element_type=F32, precision=lax.Precision.HIGHEST) + b_ref[0]


def _modulation(cvec, w_mod, b_mod):
    n_col_blocks = 4
    wc = (N_MOD * D_MODEL) // n_col_blocks
    rows = cvec.shape[0]
    return pl.pallas_call(
        _mod_kernel,
        grid=(DEPTH, n_col_blocks),
        in_specs=[pl.BlockSpec((rows, D_MODEL), lambda i, j: (0, 0)),
                  pl.BlockSpec((1, D_MODEL, wc), lambda i, j: (i, 0, j)),
                  pl.BlockSpec((1, 1, wc), lambda i, j: (i, 0, j))],
        out_specs=pl.BlockSpec((1, rows, wc), lambda i, j: (i, 0, j)),
        out_shape=jax.ShapeDtypeStruct((DEPTH, rows, N_MOD * D_MODEL), F32),
        compiler_params=_cparams(2),
        name="modulation",
    )(cvec, w_mod, b_mod.reshape(DEPTH, 1, N_MOD * D_MODEL))


def _modulate(x, m_ref):
    return (_layer_norm(x) * (1.0 + m_ref[0, 1:2, :]) + m_ref[0, 0:1, :]).astype(BF16)


def _project(h, rows, w_ref, q_ref, k_ref, v_ref, f_ref):
    p = _dot(h, w_ref[...])
    q_ref[0, rows, :] = (p[:, :NA_WIDTH] * (HEAD_DIM ** -0.5 * LOG2_E)).astype(BF16)
    k_ref[0, rows, :] = p[:, NA_WIDTH:2 * NA_WIDTH].astype(BF16)
    v_ref[0, rows, :] = p[:, 2 * NA_WIDTH:3 * NA_WIDTH].astype(BF16)
    f_ref[0, rows, :] = p[:, 3 * NA_WIDTH:]


def _proj_kernel(x_ref, m_ref, w_ref, q_ref, k_ref, v_ref, f_ref):
    tm = x_ref.shape[1]
    halves = [slice(0, tm // 2), slice(tm // 2, tm)] if tm % 512 == 0 else [slice(0, tm)]
    hs = [_modulate(x_ref[0, rows, :], m_ref) for rows in halves]
    for rows, h in zip(halves, hs):
        _project(h, rows, w_ref, q_ref, k_ref, v_ref, f_ref)


def _norm_proj_kernel(back_ref, y_hbm, lslot_ref, x1_ref, m_ref, g_ref, b_ref, m_next_ref, w_ref,
                      x_ref, q_ref, k_ref, v_ref, f_ref, ybuf, sems, *, steps_per_batch, n_steps, lr):
    step = pl.program_id(0) * steps_per_batch + pl.program_id(1)
    ys = _gather_unsort(back_ref, lslot_ref, y_hbm, ybuf, sems, step, n_steps, lr)
    tm = x1_ref.shape[1] // len(ys)
    hs = []
    for t, y in enumerate(ys):
        rows = slice(t * tm, (t + 1) * tm)
        z = DEEPNORM_ALPHA * x1_ref[0, rows, :] + m_ref[0, 5:6, :] * y
        x = _layer_norm(z) * g_ref[...] + b_ref[...]
        x_ref[0, rows, :] = x
        hs.append(_modulate(x, m_next_ref))
    for t, h in enumerate(hs):
        _project(h, slice(t * tm, (t + 1) * tm), w_ref, q_ref, k_ref, v_ref, f_ref)


def _norm_in_projection(y_slots, back_chunk, lslot, x1, m, ln_g, ln_b, m_next, w_in_bf16, tm):
    bsz, length, _ = x1.shape
    tiles_per_batch = length // tm
    per_step = 2 if tiles_per_batch % 2 == 0 else 1
    steps_per_batch = tiles_per_batch // per_step
    ts = per_step * tm
    lr = _local_rows(tm)
    out = jax.ShapeDtypeStruct((bsz, length, NA_WIDTH), BF16)
    out_f = jax.ShapeDtypeStruct((bsz, length, F_WIDTH), F32)
    out_x = jax.ShapeDtypeStruct((bsz, length, D_MODEL), F32)
    row = lambda b, i, back_ref: (b, i, 0)
    const2 = lambda b, i, back_ref: (0, 0)
    mod_spec = pl.BlockSpec((1, N_MOD, D_MODEL), lambda b, i, back_ref: (b, 0, 0))
    o_spec = pl.BlockSpec((1, ts, NA_WIDTH), row)
    x_spec = pl.BlockSpec((1, ts, D_MODEL), row)
    grid_spec = pltpu.PrefetchScalarGridSpec(
        num_scalar_prefetch=1,
        grid=(bsz, steps_per_batch),
        in_specs=[pl.BlockSpec(memory_space=pl.ANY),
                  pl.BlockSpec((1, SUBLANES, ts), lambda b, i, back_ref: (b, 0, i)),
                  x_spec, mod_spec,
                  pl.BlockSpec((1, D_MODEL), const2), pl.BlockSpec((1, D_MODEL), const2),
                  mod_spec, pl.BlockSpec((D_MODEL, IN_WIDTH), const2)],
        out_specs=[x_spec, o_spec, o_spec, o_spec, o_spec],
        scratch_shapes=[pltpu.VMEM((2, per_step * lr, D_MODEL), BF16), pltpu.SemaphoreType.DMA((2,))],
    )
    return pl.pallas_call(
        functools.partial(_norm_proj_kernel, steps_per_batch=steps_per_batch, n_steps=bsz * steps_per_batch,
                          lr=lr),
        grid_spec=grid_spec,
        out_shape=[out_x, out, out, out, out_f],
        compiler_params=_cparams(2),
        name="moe_norm_in_proj",
    )(back_chunk, y_slots, lslot, x1, m, ln_g.reshape(1, D_MODEL), ln_b.reshape(1, D_MODEL), m_next, w_in_bf16)


def _in_projection(x, m, w_in_bf16, tm):
    bsz, length, _ = x.shape
    out = jax.ShapeDtypeStruct((bsz, length, NA_WIDTH), BF16)
    out_f = jax.ShapeDtypeStruct((bsz, length, F_WIDTH), F32)
    o_spec = pl.BlockSpec((1, tm, NA_WIDTH), lambda b, i: (b, i, 0))
    return pl.pallas_call(
        _proj_kernel,
        grid=(bsz, length // tm),
        in_specs=[pl.BlockSpec((1, tm, D_MODEL), lambda b, i: (b, i, 0)),
                  pl.BlockSpec((1, N_MOD, D_MODEL), lambda b, i: (b, 0, 0)),
                  pl.BlockSpec((D_MODEL, IN_WIDTH), lambda b, i: (0, 0))],
        out_specs=[o_spec, o_spec, o_spec, o_spec],
        out_shape=[out, out, out, out_f],
        compiler_params=_cparams(2),
        name="ln_mod_in_proj",
    )(x, m, w_in_bf16)


def _head_lanes(h):
    lane = lax.broadcasted_iota(jnp.int32, (1, LANES), 1)
    return (lane >= HEAD_DIM * h) & (lane < HEAD_DIM * (h + 1))


def _scores_pass(q, h, tiles, s_ref):
    qh = jnp.where(_head_lanes(h), q, jnp.zeros_like(q))
    m = None
    t = tiles[0][0].shape[0]
    for j, (k, _, bias) in enumerate(tiles):
        s = _dot_nt(qh, k)
        if bias is not None:
            s = s + bias
        s_ref[:, j * t:(j + 1) * t] = s
        mj = jnp.max(s, axis=-1, keepdims=True)
        m = mj if m is None else jnp.maximum(m, mj)
    return m


def _pv_pass(h, tiles, s_ref, m):
    o = None
    t = tiles[0][0].shape[0]
    in_head = _head_lanes(h)
    for j, (_, v, _) in enumerate(tiles):
        p = jnp.exp2(s_ref[:, j * t:(j + 1) * t] - m)
        oj = _dot(p.astype(BF16), jnp.where(in_head, v, jnp.ones_like(v)))
        o = oj if o is None else o + oj
    return o / pltpu.roll(o, HEAD_DIM, axis=1)


def _attention_units(units):
    outs = []
    maxima = [_scores_pass(*units[0])]
    for u in range(len(units)):
        if u + 1 < len(units):
            maxima.append(_scores_pass(*units[u + 1]))
        _, h, tiles, s_ref = units[u]
        outs.append(_pv_pass(h, tiles, s_ref, maxima[u]))
    return outs


def _merge_heads(outs):
    merged = outs[0]
    for h in range(1, len(outs)):
        merged = jnp.where(_head_lanes(h), outs[h], merged)
    return merged


DR_PAD = 2 * WIN_ROWS
DC_PAD = 2 * WIN_COLS


def _toeplitz_kernel(r_ref, sel_ref, mask_ref, o_ref):
    n = r_ref.shape[0]
    for qc in range(GRID_W):
        block = jnp.dot(r_ref[...], sel_ref[qc], preferred_element_type=F32, precision=lax.Precision.HIGHEST)
        o_ref[pl.ds(qc, n, stride=GRID_W), :] = block * LOG2_E + mask_ref[qc]


def _rpb_toeplitz(rpb):
    depth, heads, n_dr, n_dc = rpb.shape
    qc = np.arange(GRID_W)[:, None]
    kc = (np.arange(LANES) % GRID_W)[None, :]
    cs = np.clip(qc - WIN_COLS // 2, 0, GRID_W - WIN_COLS)
    col_valid = (kc >= cs) & (kc < cs + WIN_COLS)
    dc = kc - qc + WIN_COLS - 1
    select = (np.arange(DC_PAD)[None, :, None] == dc[:, None, :]) & col_valid[:, None, :]
    mask = np.where(col_valid, 0.0, MASK_VALUE)[:, None, :]
    r = jnp.pad(rpb.astype(F32), ((0, 0), (0, 0), (0, DR_PAD - n_dr), (0, DC_PAD - n_dc)))
    n = depth * heads * DR_PAD
    const3 = lambda: (0, 0, 0)
    out = pl.pallas_call(
        _toeplitz_kernel,
        grid=(),
        in_specs=[pl.BlockSpec((n, DC_PAD), lambda: (0, 0)),
                  pl.BlockSpec((GRID_W, DC_PAD, LANES), const3), pl.BlockSpec((GRID_W, 1, LANES), const3)],
        out_specs=pl.BlockSpec((n * GRID_W, LANES), lambda: (0, 0)),
        out_shape=jax.ShapeDtypeStruct((n * GRID_W, LANES), F32),
        compiler_params=pltpu.CompilerParams(vmem_limit_bytes=VMEM_LIMIT_BYTES),
        name="rpb_toeplitz",
    )(r.reshape(n, DC_PAD), jnp.asarray(select, F32), jnp.asarray(mask, F32))
    return out.reshape(depth * heads, DR_PAD, GRID_W, LANES)


assert WINDOW_CHUNKS * CHUNK_GRID_ROWS >= CHUNK_GRID_ROWS + WIN_ROWS - 1


def _window_start_chunk(cr, n_cr):
    lo = cr - (WINDOW_CHUNKS - 1) // 2
    if isinstance(cr, int):
        return max(0, min(lo, n_cr - WINDOW_CHUNKS))
    return jnp.clip(lo, 0, n_cr - WINDOW_CHUNKS)


def _window_key_rows(cr, n_cr):
    start = _window_start_chunk(cr, n_cr) * CHUNK_GRID_ROWS
    return list(range(start, start + WINDOW_CHUNKS * CHUNK_GRID_ROWS))


def _row_window(qr, rows):
    kh = min(WIN_ROWS, rows)
    rs = int(np.clip(qr - kh // 2, 0, rows - kh))
    return rs, rs + kh


def _window_variant(cr, n_cr):
    return jnp.where(cr == 0, 0, jnp.where(cr == n_cr - 1, 2, 1))


def _check_windows(rows):
    def relative(cr):
        base = cr * CHUNK_GRID_ROWS
        return ([kr - base for kr in _window_key_rows(cr, n_cr)],
                [tuple(r - base for r in _row_window(base + qi, rows)) for qi in range(CHUNK_GRID_ROWS)])

    n_cr = rows // CHUNK_GRID_ROWS
    for cr in range(n_cr):
        assert cr in (0, n_cr - 1) or relative(cr) == relative(1), cr
        have = set(_window_key_rows(cr, n_cr))
        for qi in range(CHUNK_GRID_ROWS):
            lo, hi = _row_window(cr * CHUNK_GRID_ROWS + qi, rows)
            assert set(range(lo, hi)) <= have, (cr, qi)


def _build_bias_tables(t_ref, bias_scr, rows):
    n_cr = rows // CHUNK_GRID_ROWS
    left = lax.broadcasted_iota(jnp.int32, (GRID_W, LANES), 1) < GRID_W
    masked = jnp.full((GRID_W, LANES), MASK_VALUE, F32)
    for variant, cr in enumerate((0, 1, n_cr - 1)):
        key_rows = _window_key_rows(cr, n_cr)
        for h in range(HEADS_PER_STEP):
            for qi in range(CHUNK_GRID_ROWS):
                qr = cr * CHUNK_GRID_ROWS + qi
                lo, hi = _row_window(qr, rows)
                for p in range(len(key_rows) // 2):
                    pair = [t_ref[h, kr - qr + WIN_ROWS - 1] if lo <= kr < hi else None
                            for kr in key_rows[2 * p:2 * p + 2]]
                    if pair[0] is None and pair[1] is None:
                        block = masked
                    else:
                        block = jnp.where(left, masked if pair[0] is None else pair[0],
                                          masked if pair[1] is None else pair[1])
                    bias_scr[variant, h, qi * GRID_W:(qi + 1) * GRID_W, p * LANES:(p + 1) * LANES] = block


def _na_kernel(q_ref, *refs, rows):
    n_kv = STEP_CHUNKS * WINDOW_CHUNKS
    k_refs, v_refs = refs[:n_kv], refs[n_kv:2 * n_kv]
    kc_ref, vc_ref, t_ref, o_ref, bias_scr, s_scr = refs[2 * n_kv:]
    n_cr = rows // CHUNK_GRID_ROWS
    b, rb = pl.program_id(1), pl.program_id(2)

    @pl.when((b == 0) & (rb == 0))
    def _():
        _build_bias_tables(t_ref, bias_scr, rows)

    tq = CHUNK_GRID_ROWS * GRID_W
    units = []
    for c in range(STEP_CHUNKS):
        variant = _window_variant(rb * STEP_CHUNKS + c, n_cr)
        q = q_ref[0, c * tq:(c + 1) * tq, :]
        for h in range(HEADS_PER_STEP):
            lat = [(k_refs[c * WINDOW_CHUNKS + j][0], v_refs[c * WINDOW_CHUNKS + j][0],
                    bias_scr[variant, h, :, j * tq:(j + 1) * tq]) for j in range(WINDOW_CHUNKS)]
            units.append((q, h, lat + [(kc_ref[0], vc_ref[0], None)], s_scr.at[c, h]))
    outs = _attention_units(units)
    for c in range(STEP_CHUNKS):
        o = _merge_heads(outs[c * HEADS_PER_STEP:(c + 1) * HEADS_PER_STEP])
        o_ref[0, c * tq:(c + 1) * tq, :] = o.astype(BF16)


def _neighborhood_attention(q, k, v, kc, vc, toeplitz, layer):
    bsz, length, _ = q.shape
    rows = length // GRID_W
    n_cr = rows // CHUNK_GRID_ROWS
    assert rows % (CHUNK_GRID_ROWS * STEP_CHUNKS) == 0 and n_cr >= WINDOW_CHUNKS + 2
    _check_windows(rows)
    n_ctx = kc.shape[1]
    tq = CHUNK_GRID_ROWS * GRID_W
    assert n_ctx == tq

    def kv_spec(c, j):
        return pl.BlockSpec((1, tq, LANES),
                            lambda hp, b, rb: (b, _window_start_chunk(rb * STEP_CHUNKS + c, n_cr) + j, hp))

    kv_specs = [kv_spec(c, j) for c in range(STEP_CHUNKS) for j in range(WINDOW_CHUNKS)]
    ctx_spec = pl.BlockSpec((1, n_ctx, LANES), lambda hp, b, rb: (b, 0, hp))
    q_spec = pl.BlockSpec((1, STEP_CHUNKS * tq, LANES), lambda hp, b, rb: (b, rb, hp))
    window = WINDOW_CHUNKS * tq
    return pl.pallas_call(
        functools.partial(_na_kernel, rows=rows),
        grid=(NA_HEADS // HEADS_PER_STEP, bsz, n_cr // STEP_CHUNKS),
        in_specs=([q_spec] + kv_specs + kv_specs
                  + [ctx_spec, ctx_spec,
                     pl.BlockSpec((HEADS_PER_STEP, DR_PAD, GRID_W, LANES),
                                  lambda hp, b, rb: (layer * (NA_HEADS // HEADS_PER_STEP) + hp, 0, 0, 0))]),
        out_specs=q_spec,
        out_shape=jax.ShapeDtypeStruct((bsz, length, NA_WIDTH), BF16),
        scratch_shapes=[pltpu.VMEM((3, HEADS_PER_STEP, tq, window), F32),
                        pltpu.VMEM((STEP_CHUNKS, HEADS_PER_STEP, tq, window + n_ctx), F32)],
        compiler_params=_cparams(3),
        name="neighborhood_attention",
    )(q, *([k] * len(kv_specs)), *([v] * len(kv_specs)), kc, vc, toeplitz)


def _ctx_attn_kernel(q_ref, k_ref, v_ref, o_ref, s_scr):
    tiles = [(k_ref[0], v_ref[0], None)]
    outs = _attention_units([(q_ref[0], h, tiles, s_scr.at[h]) for h in range(HEADS_PER_STEP)])
    o_ref[0] = _merge_heads(outs).astype(BF16)


def _context_attention(qc, kc, vc):
    bsz, n_ctx, _ = qc.shape
    spec = pl.BlockSpec((1, n_ctx, LANES), lambda b, hp: (b, 0, hp))
    return pl.pallas_call(
        _ctx_attn_kernel,
        grid=(bsz, NA_HEADS // HEADS_PER_STEP),
        in_specs=[spec, spec, spec],
        out_specs=spec,
        out_shape=jax.ShapeDtypeStruct((bsz, n_ctx, NA_WIDTH), BF16),
        scratch_shapes=[pltpu.VMEM((HEADS_PER_STEP, n_ctx, n_ctx), F32)],
        compiler_params=_cparams(2),
        name="context_attention",
    )(qc, kc, vc)


def _dft_cos_sin(n):
    ang = 2.0 * np.pi * np.outer(np.arange(n), np.arange(n)) / n
    return np.cos(ang), np.sin(ang)


def _bf16_table(a):
    return jnp.asarray(a, F32).astype(BF16)


def _channel_dft_matrix(n_groups=F_GROUPS):
    c, s = _dft_cos_sin(F_GROUP_DIM)
    scale = F_GROUP_DIM ** -0.5
    eye = np.eye(n_groups)
    return np.concatenate([np.kron(eye, c), np.kron(eye, s)], axis=1) * scale


def _fft_stage1_kernel(f_ref, perm_ref, w1_ref, cs_ref, sc_ref, tc_ref, ts_ref, zr_ref, zi_ref, *, n_slow, nt):
    x = f_ref[0].reshape(n_slow * nt, F_WIDTH).astype(BF16)
    x = _dot(perm_ref[...], x).astype(BF16)
    ab = [_dot(x[:, p * LANES:(p + 1) * LANES], w1_ref[...]).astype(BF16) for p in range(F_WIDTH // LANES)]
    a_all = jnp.concatenate([blk[:, :LANES] for blk in ab], axis=-1)
    b_all = jnp.concatenate([blk[:, LANES:] for blk in ab], axis=-1)
    for t in range(nt):
        a = a_all[t * n_slow:(t + 1) * n_slow]
        b = b_all[t * n_slow:(t + 1) * n_slow]
        z = _dot(cs_ref[...], a) + _dot(sc_ref[...], b)
        zr, zi = z[:n_slow], z[n_slow:]
        c, s = tc_ref[t], ts_ref[t]
        zr_ref[0, t] = (zr * c - zi * s).astype(BF16)
        zi_ref[0, t] = (zr * s + zi * c).astype(BF16)


def _fft_stage2_kernel(zr_ref, zi_ref, f_ref, y_ref, *, n_fast, kb):
    sub = f_ref.shape[0] // n_fast
    parts = []
    for j0 in range(0, kb, sub):
        rhs = jnp.concatenate([zr_ref[0, :, j0:j0 + sub, :].reshape(n_fast * sub, F_WIDTH),
                               zi_ref[0, :, j0:j0 + sub, :].reshape(n_fast * sub, F_WIDTH)], axis=0)
        parts.append(_dot(f_ref[...], rhs).astype(BF16).reshape(n_fast, sub, F_WIDTH))
    y_ref[0] = jnp.concatenate(parts, axis=1)


def _fourier_positions(f, n_slow, n_fast):
    bsz, n, _ = f.shape
    assert n == n_slow * n_fast
    nt = SUBLANES
    kb = 4 * SUBLANES
    w1 = _bf16_table(_channel_dft_matrix(LANES // F_GROUP_DIM))
    perm = _bf16_table(np.eye(n_slow * nt).reshape(n_slow, nt, n_slow * nt).transpose(1, 0, 2)
                       .reshape(n_slow * nt, n_slow * nt))
    c1, s1 = _dft_cos_sin(n_slow)
    sc1 = n_slow ** -0.5
    cs = _bf16_table(np.concatenate([c1, s1], axis=0) * sc1)
    sc = _bf16_table(np.concatenate([-s1, c1], axis=0) * sc1)
    tw = 2.0 * np.pi * np.outer(np.arange(n_fast), np.arange(n_slow)) / n
    tc = jnp.asarray(np.cos(tw)[:, :, None], F32)
    ts = jnp.asarray(np.sin(tw)[:, :, None], F32)
    z_shape = jax.ShapeDtypeStruct((bsz, n_fast, n_slow, F_WIDTH), BF16)
    z_spec = pl.BlockSpec((1, nt, n_slow, F_WIDTH), lambda b, j: (b, j, 0, 0))
    const2 = lambda b, j: (0, 0)
    tw_spec = pl.BlockSpec((nt, n_slow, 1), lambda b, j: (j, 0, 0))
    zr, zi = pl.pallas_call(
        functools.partial(_fft_stage1_kernel, n_slow=n_slow, nt=nt),
        grid=(bsz, n_fast // nt),
        in_specs=[pl.BlockSpec((1, n_slow, nt, F_WIDTH), lambda b, j: (b, 0, j, 0)),
                  pl.BlockSpec(perm.shape, const2),
                  pl.BlockSpec(w1.shape, const2), pl.BlockSpec(cs.shape, const2), pl.BlockSpec(sc.shape, const2),
                  tw_spec, tw_spec],
        out_specs=[z_spec, z_spec],
        out_shape=[z_shape, z_shape],
        compiler_params=_cparams(2),
        name="fnet_stage1",
    )(f.reshape(bsz, n_slow, n_fast, F_WIDTH), perm, w1, cs, sc, tc, ts)

    c2, s2 = _dft_cos_sin(n_fast)
    sc2 = n_fast ** -0.5
    eye = np.eye(SUBLANES)
    f2 = _bf16_table(np.concatenate([np.kron(c2, eye), np.kron(-s2, eye)], axis=1) * sc2)
    blk = pl.BlockSpec((1, n_fast, kb, F_WIDTH), lambda b, j: (b, 0, j, 0))
    y = pl.pallas_call(
        functools.partial(_fft_stage2_kernel, n_fast=n_fast, kb=kb),
        grid=(bsz, n_slow // kb),
        in_specs=[blk, blk, pl.BlockSpec(f2.shape, const2)],
        out_specs=blk,
        out_shape=jax.ShapeDtypeStruct((bsz, n_fast, n_slow, F_WIDTH), BF16),
        compiler_params=_cparams(2),
        name="fnet_stage2",
    )(zr, zi, f2)
    return y.reshape(bsz, n, F_WIDTH)


def _second_largest_sum(a, b, c, d):
    mab, nab = jnp.maximum(a, b), jnp.minimum(a, b)
    mcd, ncd = jnp.maximum(c, d), jnp.minimum(c, d)
    return jnp.maximum(mab, mcd) + jnp.maximum(jnp.minimum(mab, mcd), jnp.maximum(nab, ncd))


def _selected_group(sb_rows):
    epg = EXPERTS_PER_GROUP
    g_score = [_second_largest_sum(*sb_rows[g * epg:(g + 1) * epg]) for g in range(N_GROUPS)]
    best = functools.reduce(jnp.maximum, g_score)
    group = jnp.full_like(best, float(N_GROUPS - 1))
    for g in range(N_GROUPS - 2, -1, -1):
        group = jnp.where(g_score[g] == best, float(g), group)
    return group


def _top2_gates(cand_s, cand_sb):
    n = len(cand_s)
    w = []
    for j in range(n):
        rank = jnp.zeros_like(cand_sb[j])
        for i in range(n):
            if i == j:
                continue
            ahead = (cand_sb[i] > cand_sb[j]) | ((cand_sb[i] == cand_sb[j]) & (i < j))
            rank = rank + jnp.where(ahead, 1.0, 0.0)
        w.append(jnp.where(rank < 2.0, cand_s[j], 0.0))
    total = functools.reduce(jnp.add, w)
    return [wj / total for wj in w]


CHUNK_ROWS = 16


def _local_rows(tm):
    need = tm + N_GROUPS * (CHUNK_ROWS - 1) + CHUNK_ROWS
    return -(-need // CHUNK_ROWS) * CHUNK_ROWS


def _sort_matrix(lslot, lr):
    r_iota = lax.broadcasted_iota(jnp.int32, (lr, 1), 0).astype(F32)
    return jnp.where(r_iota == lslot, 1.0, 0.0).astype(BF16)


def _local_sort(group, h2b, tri_ref, lr):
    t = group.shape[1]
    g_iota = lax.broadcasted_iota(jnp.int32, (SUBLANES, 1), 0).astype(F32)
    onehot = g_iota == group
    prefix = _dot(jnp.where(onehot, 1.0, 0.0).astype(BF16), tri_ref[...])
    count = prefix[:, t - 1:t]
    padded = jnp.floor((count + (CHUNK_ROWS - 1.0)) * (1.0 / CHUNK_ROWS)) * CHUNK_ROWS
    lslot = jnp.zeros((1, t), F32)
    start = jnp.zeros((1, 1), F32)
    for g in range(N_GROUPS):
        lslot = jnp.where(onehot[g:g + 1], start + prefix[g:g + 1] - 1.0, lslot)
        start = start + padded[g:g + 1]
    return lslot, count, _dot(_sort_matrix(lslot, lr), h2b).astype(BF16)


def _out_kernel(a_ref, y_ref, x_ref, m_ref, wo_ref, bd_ref, g_ref, b_ref, wr_ref, rb_ref, tri_ref,
                x1_ref, h2_ref, lslot_ref, count_ref):
    tm = tri_ref.shape[0]
    lr = h2_ref.shape[0] // count_ref.shape[0]

    def project(s):
        rows = slice(s * tm, (s + 1) * tm)
        y2 = _dot(y_ref[0, rows, :].astype(BF16), bd_ref[...]).astype(BF16)
        return _dot(a_ref[0, rows, :], wo_ref[:NA_WIDTH, :]) + _dot(y2, wo_ref[NA_WIDTH:, :])

    def finish(s, o):
        rows = slice(s * tm, (s + 1) * tm)
        z = DEEPNORM_ALPHA * x_ref[0, rows, :] + m_ref[0, 2:3, :] * o
        x1 = _layer_norm(z) * g_ref[...] + b_ref[...]
        x1_ref[0, rows, :] = x1
        h2 = (_layer_norm(x1) * (1.0 + m_ref[0, 4:5, :]) + m_ref[0, 3:4, :]).astype(BF16)
        sb = jax.nn.sigmoid(_dot_nt(wr_ref[...], h2)) + rb_ref[...]
        group = _selected_group([sb[e:e + 1] for e in range(N_EXPERTS)])
        lslot, count, h2_sorted = _local_sort(group, h2, tri_ref, lr)
        h2_ref[s * lr:(s + 1) * lr, :] = h2_sorted
        lslot_ref[0, :, rows] = jnp.broadcast_to(lslot, (lslot_ref.shape[1], tm))
        count_ref[s] = jnp.broadcast_to(count, count_ref.shape[1:])

    n_sub = count_ref.shape[0]
    projected = project(0)
    for s in range(n_sub):
        following = project(s + 1) if s + 1 < n_sub else None
        finish(s, projected)
        projected = following


def _out_projection(attn, yf, x, m, w_out_bf16, w_four_bd, ln_g, ln_b, w_router_t, router_bias, tm):
    bsz, length, _ = x.shape
    tiles_per_batch = length // tm
    n_tiles = bsz * tiles_per_batch
    per_step = 2 if tiles_per_batch % 2 == 0 else 1
    steps_per_batch = tiles_per_batch // per_step
    ts = per_step * tm
    lr = _local_rows(tm)
    tri = _bf16_table(np.triu(np.ones((tm, tm))))
    row = lambda b, i: (b, i, 0)
    const2 = lambda b, i: (0, 0)
    return pl.pallas_call(
        _out_kernel,
        grid=(bsz, steps_per_batch),
        in_specs=[pl.BlockSpec((1, ts, NA_WIDTH), row), pl.BlockSpec((1, ts, F_WIDTH), row),
                  pl.BlockSpec((1, ts, D_MODEL), row),
                  pl.BlockSpec((1, N_MOD, D_MODEL), lambda b, i: (b, 0, 0)),
                  pl.BlockSpec((NA_WIDTH + F_WIDTH, D_MODEL), const2),
                  pl.BlockSpec((F_WIDTH, F_WIDTH), const2),
                  pl.BlockSpec((1, D_MODEL), const2), pl.BlockSpec((1, D_MODEL), const2),
                  pl.BlockSpec((N_EXPERTS, D_MODEL), const2), pl.BlockSpec((N_EXPERTS, 1), const2),
                  pl.BlockSpec((tm, tm), const2)],
        out_specs=[pl.BlockSpec((1, ts, D_MODEL), row),
                   pl.BlockSpec((per_step * lr, D_MODEL), lambda b, i: (b * steps_per_batch + i, 0)),
                   pl.BlockSpec((1, SUBLANES, ts), lambda b, i: (b, 0, i)),
                   pl.BlockSpec((per_step, SUBLANES, LANES), lambda b, i: (b * steps_per_batch + i, 0, 0))],
        out_shape=[jax.ShapeDtypeStruct((bsz, length, D_MODEL), F32),
                   jax.ShapeDtypeStruct((n_tiles * lr, D_MODEL), BF16),
                   jax.ShapeDtypeStruct((bsz, SUBLANES, length), F32),
                   jax.ShapeDtypeStruct((n_tiles, SUBLANES, LANES), F32)],
        compiler_params=_cparams(2),
        name="out_proj_norm_route",
    )(attn, yf, x, m, w_out_bf16, w_four_bd, ln_g.reshape(1, D_MODEL), ln_b.reshape(1, D_MODEL),
      w_router_t, router_bias.reshape(N_EXPERTS, 1), tri)


def _run_plan(count, tm, lr, tm_slots):
    n_tiles = count.shape[0]
    ch = CHUNK_ROWS
    n_slots = -(-(n_tiles * (tm + N_GROUPS * (ch - 1)) + N_GROUPS * tm_slots) // tm_slots) * tm_slots
    zero_chunk = lr // ch - 1
    pad = ((count + ch - 1) // ch) * ch
    lstart = jnp.cumsum(pad, axis=1) - pad
    run_off = jnp.cumsum(pad, axis=0) - pad
    seg_len = jnp.sum(pad, axis=0)
    seg_pad = ((seg_len + tm_slots - 1) // tm_slots) * tm_slots
    seg_end = jnp.cumsum(seg_pad)
    seg_start = seg_end - seg_pad

    s = (jnp.arange(n_slots // ch, dtype=jnp.int32) * ch)[:, None, None]
    in_run = (seg_start + run_off <= s) & (s < seg_start + run_off + pad)
    src_row = jnp.arange(n_tiles, dtype=jnp.int32)[None, :, None] * lr + lstart + (s - seg_start - run_off)
    src_row = jnp.sum(jnp.where(in_run, src_row, 0), axis=(1, 2))
    src_chunk = jnp.where(jnp.any(in_run, axis=(1, 2)), src_row // ch, zero_chunk).astype(jnp.int32)

    tile_start = jnp.arange(n_slots // tm_slots, dtype=jnp.int32) * tm_slots
    tile_group = jnp.minimum(jnp.sum((seg_end[None, :] <= tile_start[:, None]).astype(jnp.int32), axis=1),
                             N_GROUPS - 1)
    n_used = (seg_end[N_GROUPS - 1:] // tm_slots).astype(jnp.int32)

    r = (jnp.arange(lr // ch, dtype=jnp.int32) * ch)[None, :, None]
    in_local = (lstart[:, None, :] <= r) & (r < (lstart + pad)[:, None, :])
    slot = (seg_start + run_off - lstart)[:, None, :] + r
    back_chunk = (jnp.sum(jnp.where(in_local, slot, 0), axis=2) // ch).astype(jnp.int32)
    return src_chunk, tile_group.astype(jnp.int32), n_used, back_chunk.reshape(-1)


def _chunk_gather(idx_ref, tile, n_chunks, src_hbm, buf, sems, wait):
    slot = tile % 2
    rows = n_chunks * CHUNK_ROWS
    if wait:
        pltpu.make_async_copy(src_hbm.at[pl.ds(0, rows), :], buf.at[slot], sems.at[slot]).wait()
        return
    for c in range(n_chunks):
        src = pl.multiple_of(idx_ref[tile * n_chunks + c] * CHUNK_ROWS, CHUNK_ROWS)
        pltpu.make_async_copy(src_hbm.at[pl.ds(src, CHUNK_ROWS), :],
                              buf.at[slot, pl.ds(c * CHUNK_ROWS, CHUNK_ROWS), :], sems.at[slot]).start()


def _moe_runs_kernel(src_ref, tile_group_ref, n_used_ref, h_hbm, wg32_ref, wu32_ref, wd32_ref, wr_ref, rb_ref,
                     y_ref, hbuf, wg_ref, wu_ref, wd_ref, sems, *, tm):
    step = pl.program_id(0)
    n_used = n_used_ref[0]
    n_chunks = tm // CHUNK_ROWS
    new_group = (step == 0) | (tile_group_ref[step] != tile_group_ref[jnp.maximum(step - 1, 0)])

    @pl.when(new_group & (step < n_used))
    def _():
        wg_ref[...] = wg32_ref[...].astype(BF16)
        wu_ref[...] = wu32_ref[...].astype(BF16)
        wd_ref[...] = wd32_ref[...].astype(BF16)

    @pl.when((step == 0) & (n_used > 0))
    def _():
        _chunk_gather(src_ref, 0, n_chunks, h_hbm, hbuf, sems, wait=False)

    @pl.when(step + 1 < n_used)
    def _():
        _chunk_gather(src_ref, step + 1, n_chunks, h_hbm, hbuf, sems, wait=False)

    @pl.when(step < n_used)
    def _():
        _chunk_gather(src_ref, step, n_chunks, h_hbm, hbuf, sems, wait=True)
        h = hbuf[step % 2]
        s = jax.nn.sigmoid(_dot(h, wr_ref[0]))
        sb = s + rb_ref[0]
        epg = EXPERTS_PER_GROUP
        gates = _top2_gates([s[:, j:j + 1] for j in range(epg)], [sb[:, j:j + 1] for j in range(epg)])
        acc = None
        gate_up = [(_dot(h, wg_ref[0]), _dot(h, wu_ref[0]))]
        for e in range(epg):
            if e + 1 < epg:
                gate_up.append((_dot(h, wg_ref[e + 1]), _dot(h, wu_ref[e + 1])))
            gate, up = gate_up[e]
            hid = (gate * jax.nn.sigmoid(gate)) * up * gates[e]
            y = _dot(hid.astype(BF16), wd_ref[e])
            acc = y if acc is None else acc + y
        y_ref[...] = acc.astype(BF16)

    @pl.when(step >= n_used)
    def _():
        y_ref[...] = jnp.zeros_like(y_ref)


def _moe_runs(h_sorted, src_chunk, tile_group, n_used, layer, w_gate, w_up, w_down, w_router_grp,
              router_bias_grp, tm):
    n_slots = src_chunk.shape[0] * CHUNK_ROWS
    epg = EXPERTS_PER_GROUP
    by_group = lambda i, src_ref, tg_ref, nu_ref: (tg_ref[i], 0, 0)
    by_layer_group = lambda i, src_ref, tg_ref, nu_ref: (layer * N_GROUPS + tg_ref[i], 0, 0)
    grid_spec = pltpu.PrefetchScalarGridSpec(
        num_scalar_prefetch=3,
        grid=(n_slots // tm,),
        in_specs=[pl.BlockSpec(memory_space=pl.ANY),
                  pl.BlockSpec((epg, D_MODEL, D_EXPERT), by_layer_group),
                  pl.BlockSpec((epg, D_MODEL, D_EXPERT), by_layer_group),
                  pl.BlockSpec((epg, D_EXPERT, D_MODEL), by_layer_group),
                  pl.BlockSpec((1, D_MODEL, LANES), by_group),
                  pl.BlockSpec((1, 1, LANES), by_group)],
        out_specs=pl.BlockSpec((tm, D_MODEL), lambda i, src_ref, tg_ref, nu_ref: (i, 0)),
        scratch_shapes=[pltpu.VMEM((2, tm, D_MODEL), BF16),
                        pltpu.VMEM((epg, D_MODEL, D_EXPERT), BF16), pltpu.VMEM((epg, D_MODEL, D_EXPERT), BF16),
                        pltpu.VMEM((epg, D_EXPERT, D_MODEL), BF16), pltpu.SemaphoreType.DMA((2,))],
    )
    return pl.pallas_call(
        functools.partial(_moe_runs_kernel, tm=tm),
        grid_spec=grid_spec,
        out_shape=jax.ShapeDtypeStruct((n_slots, D_MODEL), BF16),
        compiler_params=_cparams(1),
        name="moe_group_experts",
    )(src_chunk, tile_group, n_used, h_sorted, w_gate, w_up, w_down, w_router_grp, router_bias_grp)


def _gather_unsort(back_ref, lslot_ref, y_hbm, ybuf, sems, step, n_steps, lr):
    per_step = ybuf.shape[1] // lr
    tm = lslot_ref.shape[2] // per_step
    n_chunks = per_step * lr // CHUNK_ROWS

    @pl.when(step == 0)
    def _():
        _chunk_gather(back_ref, 0, n_chunks, y_hbm, ybuf, sems, wait=False)

    @pl.when(step + 1 < n_steps)
    def _():
        _chunk_gather(back_ref, step + 1, n_chunks, y_hbm, ybuf, sems, wait=False)

    _chunk_gather(back_ref, step, n_chunks, y_hbm, ybuf, sems, wait=True)
    outs = []
    for t in range(per_step):
        sort = _sort_matrix(lslot_ref[0, 0:1, t * tm:(t + 1) * tm], lr)
        outs.append(lax.dot_general(sort, ybuf[step % 2, t * lr:(t + 1) * lr, :], (((0,), (0,)), ((), ())),
                                    preferred_element_type=F32))
    return outs


def _residual_norm_kernel(back_ref, y_hbm, lslot_ref, x1_ref, m_ref, g_ref, b_ref, o_ref, ybuf, sems, *, n_steps, lr):
    ys = _gather_unsort(back_ref, lslot_ref, y_hbm, ybuf, sems, pl.program_id(0), n_steps, lr)
    tm = x1_ref.shape[0] // len(ys)
    for t, y in enumerate(ys):
        rows = slice(t * tm, (t + 1) * tm)
        z = DEEPNORM_ALPHA * x1_ref[rows, :] + m_ref[0, 5:6, :] * y
        o_ref[rows, :] = _layer_norm(z) * g_ref[...] + b_ref[...]


def _residual_norm(y_slots, back_chunk, lslot, x1, m, ln_g, ln_b, tm):
    bsz, length, _ = x1.shape
    n = bsz * length
    tiles_per_batch = length // tm
    per_step = 2 if tiles_per_batch % 2 == 0 else 1
    steps_per_batch = tiles_per_batch // per_step
    ts = per_step * tm
    lr = _local_rows(tm)
    const2 = lambda i, back_ref: (0, 0)
    row = pl.BlockSpec((ts, D_MODEL), lambda i, back_ref: (i, 0))
    grid_spec = pltpu.PrefetchScalarGridSpec(
        num_scalar_prefetch=1,
        grid=(n // ts,),
        in_specs=[pl.BlockSpec(memory_space=pl.ANY),
                  pl.BlockSpec((1, SUBLANES, ts),
                               lambda i, back_ref: (i // steps_per_batch, 0, i % steps_per_batch)),
                  row, pl.BlockSpec((1, N_MOD, D_MODEL), lambda i, back_ref: (i // steps_per_batch, 0, 0)),
                  pl.BlockSpec((1, D_MODEL), const2), pl.BlockSpec((1, D_MODEL), const2)],
        out_specs=row,
        scratch_shapes=[pltpu.VMEM((2, per_step * lr, D_MODEL), BF16), pltpu.SemaphoreType.DMA((2,))],
    )
    out = pl.pallas_call(
        functools.partial(_residual_norm_kernel, n_steps=n // ts, lr=lr),
        grid_spec=grid_spec,
        out_shape=jax.ShapeDtypeStruct((n, D_MODEL), F32),
        compiler_params=_cparams(1),
        name="moe_residual_norm",
    )(back_chunk, y_slots, lslot, x1.reshape(n, D_MODEL), m, ln_g.reshape(1, D_MODEL), ln_b.reshape(1, D_MODEL))
    return out.reshape(bsz, length, D_MODEL)


def _grouped_moe_runs(h_sorted, count_rows, layer, w_gate, w_up, w_down, w_router_grp, router_bias_grp,
                      tm_tokens, tm_slots):
    count = count_rows[:, :N_GROUPS, 0].astype(jnp.int32)
    src_chunk, tile_group, n_used, back_chunk = _run_plan(count, tm_tokens, _local_rows(tm_tokens), tm_slots)
    y_slots = _moe_runs(h_sorted, src_chunk, tile_group, n_used, layer, w_gate, w_up, w_down, w_router_grp,
                        router_bias_grp, tm_slots)
    return y_slots, back_chunk


def _ctx_fourier_kernel(f_ref, w1_ref, c_ref, s_ref, y_ref):
    ab = _dot(f_ref[0].astype(BF16), w1_ref[...]).astype(BF16)
    y = _dot(c_ref[...], ab[:, :F_WIDTH]) + _dot(s_ref[...], ab[:, F_WIDTH:])
    y_ref[0] = y.astype(BF16)


def _context_fourier(fc):
    bsz, n, _ = fc.shape
    w1 = _bf16_table(_channel_dft_matrix())
    c, s = _dft_cos_sin(n)
    cm = _bf16_table(c * n ** -0.5)
    sm = _bf16_table(-s * n ** -0.5)
    const2 = lambda b: (0, 0)
    blk = pl.BlockSpec((1, n, F_WIDTH), lambda b: (b, 0, 0))
    return pl.pallas_call(
        _ctx_fourier_kernel,
        grid=(bsz,),
        in_specs=[blk, pl.BlockSpec(w1.shape, const2), pl.BlockSpec(cm.shape, const2),
                  pl.BlockSpec(sm.shape, const2)],
        out_specs=blk,
        out_shape=jax.ShapeDtypeStruct((bsz, n, F_WIDTH), BF16),
        compiler_params=_cparams(1),
        name="context_fnet",
    )(fc, w1, cm, sm)


def _block_diag(w):
    g, c, _ = w.shape
    eye = jnp.eye(g, dtype=w.dtype)
    return (eye[:, None, :, None] * w[:, :, None, :]).reshape(g * c, g * c)


def kernel(x, c, ctx, c_ctx, w_mod, b_mod, w_in, rpb, w_four, w_out, ln1_g, ln1_b, ln2_g, ln2_b,
           w_router, router_bias, w_gate, w_up, w_down):
    bsz, length, _ = x.shape
    n_ctx = ctx.shape[1]
    rows = length // GRID_W

    cvec = jnp.concatenate([c, c_ctx[None, :], jnp.zeros((8 - bsz - 1, D_MODEL), F32)], axis=0)
    mods = _modulation(cvec, w_mod, b_mod)
    w_router_t = w_router.T.astype(BF16)
    toeplitz = _rpb_toeplitz(rpb)
    lane_pad = LANES - EXPERTS_PER_GROUP
    w_router_grp = jnp.pad(w_router.reshape(D_MODEL, N_GROUPS, EXPERTS_PER_GROUP).transpose(1, 0, 2),
                           ((0, 0), (0, 0), (0, lane_pad))).astype(BF16)
    router_bias_grp = jnp.pad(router_bias.astype(F32).reshape(N_GROUPS, 1, EXPERTS_PER_GROUP),
                              ((0, 0), (0, 0), (0, lane_pad)))

    wg = w_gate.reshape(DEPTH * N_EXPERTS, D_MODEL, D_EXPERT)
    wu = w_up.reshape(DEPTH * N_EXPERTS, D_MODEL, D_EXPERT)
    wd = w_down.reshape(DEPTH * N_EXPERTS, D_EXPERT, D_MODEL)

    def latent_mod(i):
        return mods[i, :bsz].reshape(bsz, N_MOD, D_MODEL)

    xc = ctx
    projected = None
    for i in range(DEPTH):
        last = i == DEPTH - 1
        m = latent_mod(i)
        mc = jnp.broadcast_to(mods[i, bsz].reshape(1, N_MOD, D_MODEL), (bsz, N_MOD, D_MODEL))
        w_in_b = w_in[i].astype(BF16)
        w_out_b = w_out[i].astype(BF16)
        w_four_bd = _block_diag(w_four[i]).astype(BF16)

        q, k, v, f = projected if projected is not None else _in_projection(x, m, w_in_b, tm=1024)
        qc, kc, vc, fc = _in_projection(xc, mc, w_in_b, tm=n_ctx)

        attn = _neighborhood_attention(q, k, v, kc, vc, toeplitz, i)
        yf = _fourier_positions(f, n_slow=rows, n_fast=GRID_W)
        x1, h2, lslot, cnt = _out_projection(attn, yf, x, m, w_out_b, w_four_bd, ln1_g[i], ln1_b[i],
                                             w_router_t, router_bias, tm=512)
        y, back = _grouped_moe_runs(h2, cnt, i, wg, wu, wd, w_router_grp, router_bias_grp,
                                    tm_tokens=512, tm_slots=512)
        if last:
            return _residual_norm(y, back, lslot, x1, m, ln2_g[i], ln2_b[i], tm=512)
        x, *projected = _norm_in_projection(y, back, lslot, x1, m, ln2_g[i], ln2_b[i], latent_mod(i + 1),
                                            w_in[i + 1].astype(BF16), tm=512)

        attn_c = _context_attention(qc, kc, vc)
        yc = _context_fourier(fc)
        xc1, h2c, lslot_c, cnt_c = _out_projection(attn_c, yc, xc, mc, w_out_b, w_four_bd, ln1_g[i], ln1_b[i],
                                                   w_router_t, router_bias, tm=n_ctx)
        yc2, back_c = _grouped_moe_runs(h2c, cnt_c, i, wg, wu, wd, w_router_grp, router_bias_grp,
                                        tm_tokens=n_ctx, tm_slots=128)
        xc = _residual_norm(yc2, back_c, lslot_c, xc1, mc, ln2_g[i], ln2_b[i], tm=n_ctx)
    return x
```

```python
import functools
import math

import numpy as np
import jax
import jax.numpy as jnp
from jax import lax
from jax.experimental import pallas as pl
from jax.experimental.pallas import tpu as pltpu

D_MODEL = 1024
DEPTH = 2
GRID_W = 64
NA_HEADS = 8
HEAD_DIM = 64
NA_WIDTH = NA_HEADS * HEAD_DIM
WIN_ROWS = 8
WIN_COLS = 16
F_GROUPS = 8
F_GROUP_DIM = 64
F_WIDTH = F_GROUPS * F_GROUP_DIM
IN_WIDTH = 3 * NA_WIDTH + F_WIDTH
N_EXPERTS = 16
N_GROUPS = 4
EXPERTS_PER_GROUP = N_EXPERTS // N_GROUPS
D_EXPERT = 256
N_MOD = 6
DEEPNORM_ALPHA = (2.0 * DEPTH) ** 0.25
LN_EPS = 1e-6

F32 = jnp.float32
BF16 = jnp.bfloat16

V7X_VMEM_BYTES = 64 * 1024 * 1024
VMEM_LIMIT_BYTES = (V7X_VMEM_BYTES * 3) // 4
LANES = 128
SUBLANES = 8
HEADS_PER_STEP = LANES // HEAD_DIM
assert HEADS_PER_STEP == 2
MASK_VALUE = -1e30
LOG2_E = math.log2(math.e)

CHUNK_GRID_ROWS = 4
WINDOW_CHUNKS = 3
STEP_CHUNKS = 8


def _cparams(n_grid_dims):
    return pltpu.CompilerParams(dimension_semantics=("arbitrary",) * n_grid_dims,
                                vmem_limit_bytes=VMEM_LIMIT_BYTES)


def _layer_norm(x):
    mu = jnp.mean(x, axis=-1, keepdims=True)
    xc = x - mu
    var = jnp.mean(xc * xc, axis=-1, keepdims=True)
    return xc * lax.rsqrt(var + LN_EPS)


def _dot(a, b):
    return jnp.dot(a, b, preferred_element_type=F32)


def _dot_nt(a, b):
    return lax.dot_general(a, b, (((1,), (1,)), ((), ())), preferred_element_type=F32)


def _mod_kernel(c_ref, w_ref, b_ref, o_ref):
    c = c_ref[...]
    a = c * jax.nn.sigmoid(c)
    o_ref[0] = jnp.dot(a, w_ref[0], preferred_element_type=F32, precision=lax.Precision.HIGHEST) + b_ref[0]


def _modulation(cvec, w_mod, b_mod):
    n_col_blocks = 4
    wc = (N_MOD * D_MODEL) // n_col_blocks
    rows = cvec.shape[0]
    return pl.pallas_call(
        _mod_kernel,
        grid=(DEPTH, n_col_blocks),
        in_specs=[pl.BlockSpec((rows, D_MODEL), lambda i, j: (0, 0)),
                  pl.BlockSpec((1, D_MODEL, wc), lambda i, j: (i, 0, j)),
                  pl.BlockSpec((1, 1, wc), lambda i, j: (i, 0, j))],
        out_specs=pl.BlockSpec((1, rows, wc), lambda i, j: (i, 0, j)),
        out_shape=jax.ShapeDtypeStruct((DEPTH, rows, N_MOD * D_MODEL), F32),
        compiler_params=_cparams(2),
        name="modulation",
    )(cvec, w_mod, b_mod.reshape(DEPTH, 1, N_MOD * D_MODEL))


def _modulate(x, m_ref):
    return (_layer_norm(x) * (1.0 + m_ref[0, 1:2, :]) + m_ref[0, 0:1, :]).astype(BF16)


def _project(h, rows, w_ref, q_ref, k_ref, v_ref, f_ref):
    p = _dot(h, w_ref[...])
    q_ref[0, rows, :] = (p[:, :NA_WIDTH] * (HEAD_DIM ** -0.5 * LOG2_E)).astype(BF16)
    k_ref[0, rows, :] = p[:, NA_WIDTH:2 * NA_WIDTH].astype(BF16)
    v_ref[0, rows, :] = p[:, 2 * NA_WIDTH:3 * NA_WIDTH].astype(BF16)
    f_ref[0, rows, :] = p[:, 3 * NA_WIDTH:]


def _proj_kernel(x_ref, m_ref, w_ref, q_ref, k_ref, v_ref, f_ref):
    tm = x_ref.shape[1]
    halves = [slice(0, tm // 2), slice(tm // 2, tm)] if tm % 512 == 0 else [slice(0, tm)]
    hs = [_modulate(x_ref[0, rows, :], m_ref) for rows in halves]
    for rows, h in zip(halves, hs):
        _project(h, rows, w_ref, q_ref, k_ref, v_ref, f_ref)


def _norm_proj_kernel(back_ref, y_hbm, lslot_ref, x1_ref, m_ref, g_ref, b_ref, m_next_ref, w_ref,
                      x_ref, q_ref, k_ref, v_ref, f_ref, ybuf, sems, *, steps_per_batch, n_steps, lr):
    step = pl.program_id(0) * steps_per_batch + pl.program_id(1)
    ys = _gather_unsort(back_ref, lslot_ref, y_hbm, ybuf, sems, step, n_steps, lr)
    tm = x1_ref.shape[1] // len(ys)
    hs = []
    for t, y in enumerate(ys):
        rows = slice(t * tm, (t + 1) * tm)
        z = DEEPNORM_ALPHA * x1_ref[0, rows, :] + m_ref[0, 5:6, :] * y
        x = _layer_norm(z) * g_ref[...] + b_ref[...]
        x_ref[0, rows, :] = x
        hs.append(_modulate(x, m_next_ref))
    for t, h in enumerate(hs):
        _project(h, slice(t * tm, (t + 1) * tm), w_ref, q_ref, k_ref, v_ref, f_ref)


def _norm_in_projection(y_slots, back_chunk, lslot, x1, m, ln_g, ln_b, m_next, w_in_bf16, tm):
    bsz, length, _ = x1.shape
    tiles_per_batch = length // tm
    per_step = 2 if tiles_per_batch % 2 == 0 else 1
    steps_per_batch = tiles_per_batch // per_step
    ts = per_step * tm
    lr = _local_rows(tm)
    out = jax.ShapeDtypeStruct((bsz, length, NA_WIDTH), BF16)
    out_f = jax.ShapeDtypeStruct((bsz, length, F_WIDTH), F32)
    out_x = jax.ShapeDtypeStruct((bsz, length, D_MODEL), F32)
    row = lambda b, i, back_ref: (b, i, 0)
    const2 = lambda b, i, back_ref: (0, 0)
    mod_spec = pl.BlockSpec((1, N_MOD, D_MODEL), lambda b, i, back_ref: (b, 0, 0))
    o_spec = pl.BlockSpec((1, ts, NA_WIDTH), row)
    x_spec = pl.BlockSpec((1, ts, D_MODEL), row)
    grid_spec = pltpu.PrefetchScalarGridSpec(
        num_scalar_prefetch=1,
        grid=(bsz, steps_per_batch),
        in_specs=[pl.BlockSpec(memory_space=pl.ANY),
                  pl.BlockSpec((1, SUBLANES, ts), lambda b, i, back_ref: (b, 0, i)),
                  x_spec, mod_spec,
                  pl.BlockSpec((1, D_MODEL), const2), pl.BlockSpec((1, D_MODEL), const2),
                  mod_spec, pl.BlockSpec((D_MODEL, IN_WIDTH), const2)],
        out_specs=[x_spec, o_spec, o_spec, o_spec, o_spec],
        scratch_shapes=[pltpu.VMEM((2, per_step * lr, D_MODEL), BF16), pltpu.SemaphoreType.DMA((2,))],
    )
    return pl.pallas_call(
        functools.partial(_norm_proj_kernel, steps_per_batch=steps_per_batch, n_steps=bsz * steps_per_batch,
                          lr=lr),
        grid_spec=grid_spec,
        out_shape=[out_x, out, out, out, out_f],
        compiler_params=_cparams(2),
        name="moe_norm_in_proj",
    )(back_chunk, y_slots, lslot, x1, m, ln_g.reshape(1, D_MODEL), ln_b.reshape(1, D_MODEL), m_next, w_in_bf16)


def _in_projection(x, m, w_in_bf16, tm):
    bsz, length, _ = x.shape
    out = jax.ShapeDtypeStruct((bsz, length, NA_WIDTH), BF16)
    out_f = jax.ShapeDtypeStruct((bsz, length, F_WIDTH), F32)
    o_spec = pl.BlockSpec((1, tm, NA_WIDTH), lambda b, i: (b, i, 0))
    return pl.pallas_call(
        _proj_kernel,
        grid=(bsz, length // tm),
        in_specs=[pl.BlockSpec((1, tm, D_MODEL), lambda b, i: (b, i, 0)),
                  pl.BlockSpec((1, N_MOD, D_MODEL), lambda b, i: (b, 0, 0)),
                  pl.BlockSpec((D_MODEL, IN_WIDTH), lambda b, i: (0, 0))],
        out_specs=[o_spec, o_spec, o_spec, o_spec],
        out_shape=[out, out, out, out_f],
        compiler_params=_cparams(2),
        name="ln_mod_in_proj",
    )(x, m, w_in_bf16)


def _head_lanes(h):
    lane = lax.broadcasted_iota(jnp.int32, (1, LANES), 1)
    return (lane >= HEAD_DIM * h) & (lane < HEAD_DIM * (h + 1))


def _scores_pass(q, h, tiles, s_ref):
    qh = jnp.where(_head_lanes(h), q, jnp.zeros_like(q))
    m = None
    t = tiles[0][0].shape[0]
    for j, (k, _, bias) in enumerate(tiles):
        s = _dot_nt(qh, k)
        if bias is not None:
            s = s + bias
        s_ref[:, j * t:(j + 1) * t] = s
        mj = jnp.max(s, axis=-1, keepdims=True)
        m = mj if m is None else jnp.maximum(m, mj)
    return m


def _pv_pass(h, tiles, s_ref, m):
    o = None
    t = tiles[0][0].shape[0]
    in_head = _head_lanes(h)
    for j, (_, v, _) in enumerate(tiles):
        p = jnp.exp2(s_ref[:, j * t:(j + 1) * t] - m)
        oj = _dot(p.astype(BF16), jnp.where(in_head, v, jnp.ones_like(v)))
        o = oj if o is None else o + oj
    return o / pltpu.roll(o, HEAD_DIM, axis=1)


def _attention_units(units):
    outs = []
    maxima = [_scores_pass(*units[0])]
    for u in range(len(units)):
        if u + 1 < len(units):
            maxima.append(_scores_pass(*units[u + 1]))
        _, h, tiles, s_ref = units[u]
        outs.append(_pv_pass(h, tiles, s_ref, maxima[u]))
    return outs


def _merge_heads(outs):
    merged = outs[0]
    for h in range(1, len(outs)):
        merged = jnp.where(_head_lanes(h), outs[h], merged)
    return merged


DR_PAD = 2 * WIN_ROWS
DC_PAD = 2 * WIN_COLS


def _toeplitz_kernel(r_ref, sel_ref, mask_ref, o_ref):
    n = r_ref.shape[0]
    for qc in range(GRID_W):
        block = jnp.dot(r_ref[...], sel_ref[qc], preferred_element_type=F32, precision=lax.Precision.HIGHEST)
        o_ref[pl.ds(qc, n, stride=GRID_W), :] = block * LOG2_E + mask_ref[qc]


def _rpb_toeplitz(rpb):
    depth, heads, n_dr, n_dc = rpb.shape
    qc = np.arange(GRID_W)[:, None]
    kc = (np.arange(LANES) % GRID_W)[None, :]
    cs = np.clip(qc - WIN_COLS // 2, 0, GRID_W - WIN_COLS)
    col_valid = (kc >= cs) & (kc < cs + WIN_COLS)
    dc = kc - qc + WIN_COLS - 1
    select = (np.arange(DC_PAD)[None, :, None] == dc[:, None, :]) & col_valid[:, None, :]
    mask = np.where(col_valid, 0.0, MASK_VALUE)[:, None, :]
    r = jnp.pad(rpb.astype(F32), ((0, 0), (0, 0), (0, DR_PAD - n_dr), (0, DC_PAD - n_dc)))
    n = depth * heads * DR_PAD
    const3 = lambda: (0, 0, 0)
    out = pl.pallas_call(
        _toeplitz_kernel,
        grid=(),
        in_specs=[pl.BlockSpec((n, DC_PAD), lambda: (0, 0)),
                  pl.BlockSpec((GRID_W, DC_PAD, LANES), const3), pl.BlockSpec((GRID_W, 1, LANES), const3)],
        out_specs=pl.BlockSpec((n * GRID_W, LANES), lambda: (0, 0)),
        out_shape=jax.ShapeDtypeStruct((n * GRID_W, LANES), F32),
        compiler_params=pltpu.CompilerParams(vmem_limit_bytes=VMEM_LIMIT_BYTES),
        name="rpb_toeplitz",
    )(r.reshape(n, DC_PAD), jnp.asarray(select, F32), jnp.asarray(mask, F32))
    return out.reshape(depth * heads, DR_PAD, GRID_W, LANES)


assert WINDOW_CHUNKS * CHUNK_GRID_ROWS >= CHUNK_GRID_ROWS + WIN_ROWS - 1


def _window_start_chunk(cr, n_cr):
    lo = cr - (WINDOW_CHUNKS - 1) // 2
    if isinstance(cr, int):
        return max(0, min(lo, n_cr - WINDOW_CHUNKS))
    return jnp.clip(lo, 0, n_cr - WINDOW_CHUNKS)


def _window_key_rows(cr, n_cr):
    start = _window_start_chunk(cr, n_cr) * CHUNK_GRID_ROWS
    return list(range(start, start + WINDOW_CHUNKS * CHUNK_GRID_ROWS))


def _row_window(qr, rows):
    kh = min(WIN_ROWS, rows)
    rs = int(np.clip(qr - kh // 2, 0, rows - kh))
    return rs, rs + kh


def _window_variant(cr, n_cr):
    return jnp.where(cr == 0, 0, jnp.where(cr == n_cr - 1, 2, 1))


def _check_windows(rows):
    def relative(cr):
        base = cr * CHUNK_GRID_ROWS
        return ([kr - base for kr in _window_key_rows(cr, n_cr)],
                [tuple(r - base for r in _row_window(base + qi, rows)) for qi in range(CHUNK_GRID_ROWS)])

    n_cr = rows // CHUNK_GRID_ROWS
    for cr in range(n_cr):
        assert cr in (0, n_cr - 1) or relative(cr) == relative(1), cr
        have = set(_window_key_rows(cr, n_cr))
        for qi in range(CHUNK_GRID_ROWS):
            lo, hi = _row_window(cr * CHUNK_GRID_ROWS + qi, rows)
            assert set(range(lo, hi)) <= have, (cr, qi)


def _build_bias_tables(t_ref, bias_scr, rows):
    n_cr = rows // CHUNK_GRID_ROWS
    left = lax.broadcasted_iota(jnp.int32, (GRID_W, LANES), 1) < GRID_W
    masked = jnp.full((GRID_W, LANES), MASK_VALUE, F32)
    for variant, cr in enumerate((0, 1, n_cr - 1)):
        key_rows = _window_key_rows(cr, n_cr)
        for h in range(HEADS_PER_STEP):
            for qi in range(CHUNK_GRID_ROWS):
                qr = cr * CHUNK_GRID_ROWS + qi
                lo, hi = _row_window(qr, rows)
                for p in range(len(key_rows) // 2):
                    pair = [t_ref[h, kr - qr + WIN_ROWS - 1] if lo <= kr < hi else None
                            for kr in key_rows[2 * p:2 * p + 2]]
                    if pair[0] is None and pair[1] is None:
                        block = masked
                    else:
                        block = jnp.where(left, masked if pair[0] is None else pair[0],
                                          masked if pair[1] is None else pair[1])
                    bias_scr[variant, h, qi * GRID_W:(qi + 1) * GRID_W, p * LANES:(p + 1) * LANES] = block


def _na_kernel(q_ref, *refs, rows):
    n_kv = STEP_CHUNKS * WINDOW_CHUNKS
    k_refs, v_refs = refs[:n_kv], refs[n_kv:2 * n_kv]
    kc_ref, vc_ref, t_ref, o_ref, bias_scr, s_scr = refs[2 * n_kv:]
    n_cr = rows // CHUNK_GRID_ROWS
    b, rb = pl.program_id(1), pl.program_id(2)

    @pl.when((b == 0) & (rb == 0))
    def _():
        _build_bias_tables(t_ref, bias_scr, rows)

    tq = CHUNK_GRID_ROWS * GRID_W
    units = []
    for c in range(STEP_CHUNKS):
        variant = _window_variant(rb * STEP_CHUNKS + c, n_cr)
        q = q_ref[0, c * tq:(c + 1) * tq, :]
        for h in range(HEADS_PER_STEP):
            lat = [(k_refs[c * WINDOW_CHUNKS + j][0], v_refs[c * WINDOW_CHUNKS + j][0],
                    bias_scr[variant, h, :, j * tq:(j + 1) * tq]) for j in range(WINDOW_CHUNKS)]
            units.append((q, h, lat + [(kc_ref[0], vc_ref[0], None)], s_scr.at[c, h]))
    outs = _attention_units(units)
    for c in range(STEP_CHUNKS):
        o = _merge_heads(outs[c * HEADS_PER_STEP:(c + 1) * HEADS_PER_STEP])
        o_ref[0, c * tq:(c + 1) * tq, :] = o.astype(BF16)


def _neighborhood_attention(q, k, v, kc, vc, toeplitz, layer):
    bsz, length, _ = q.shape
    rows = length // GRID_W
    n_cr = rows // CHUNK_GRID_ROWS
    assert rows % (CHUNK_GRID_ROWS * STEP_CHUNKS) == 0 and n_cr >= WINDOW_CHUNKS + 2
    _check_windows(rows)
    n_ctx = kc.shape[1]
    tq = CHUNK_GRID_ROWS * GRID_W
    assert n_ctx == tq

    def kv_spec(c, j):
        return pl.BlockSpec((1, tq, LANES),
                            lambda hp, b, rb: (b, _window_start_chunk(rb * STEP_CHUNKS + c, n_cr) + j, hp))

    kv_specs = [kv_spec(c, j) for c in range(STEP_CHUNKS) for j in range(WINDOW_CHUNKS)]
    ctx_spec = pl.BlockSpec((1, n_ctx, LANES), lambda hp, b, rb: (b, 0, hp))
    q_spec = pl.BlockSpec((1, STEP_CHUNKS * tq, LANES), lambda hp, b, rb: (b, rb, hp))
    window = WINDOW_CHUNKS * tq
    return pl.pallas_call(
        functools.partial(_na_kernel, rows=rows),
        grid=(NA_HEADS // HEADS_PER_STEP, bsz, n_cr // STEP_CHUNKS),
        in_specs=([q_spec] + kv_specs + kv_specs
                  + [ctx_spec, ctx_spec,
                     pl.BlockSpec((HEADS_PER_STEP, DR_PAD, GRID_W, LANES),
                                  lambda hp, b, rb: (layer * (NA_HEADS // HEADS_PER_STEP) + hp, 0, 0, 0))]),
        out_specs=q_spec,
        out_shape=jax.ShapeDtypeStruct((bsz, length, NA_WIDTH), BF16),
        scratch_shapes=[pltpu.VMEM((3, HEADS_PER_STEP, tq, window), F32),
                        pltpu.VMEM((STEP_CHUNKS, HEADS_PER_STEP, tq, window + n_ctx), F32)],
        compiler_params=_cparams(3),
        name="neighborhood_attention",
    )(q, *([k] * len(kv_specs)), *([v] * len(kv_specs)), kc, vc, toeplitz)


def _ctx_attn_kernel(q_ref, k_ref, v_ref, o_ref, s_scr):
    tiles = [(k_ref[0], v_ref[0], None)]
    outs = _attention_units([(q_ref[0], h, tiles, s_scr.at[h]) for h in range(HEADS_PER_STEP)])
    o_ref[0] = _merge_heads(outs).astype(BF16)


def _context_attention(qc, kc, vc):
    bsz, n_ctx, _ = qc.shape
    spec = pl.BlockSpec((1, n_ctx, LANES), lambda b, hp: (b, 0, hp))
    return pl.pallas_call(
        _ctx_attn_kernel,
        grid=(bsz, NA_HEADS // HEADS_PER_STEP),
        in_specs=[spec, spec, spec],
        out_specs=spec,
        out_shape=jax.ShapeDtypeStruct((bsz, n_ctx, NA_WIDTH), BF16),
        scratch_shapes=[pltpu.VMEM((HEADS_PER_STEP, n_ctx, n_ctx), F32)],
        compiler_params=_cparams(2),
        name="context_attention",
    )(qc, kc, vc)


def _dft_cos_sin(n):
    ang = 2.0 * np.pi * np.outer(np.arange(n), np.arange(n)) / n
    return np.cos(ang), np.sin(ang)


def _bf16_table(a):
    return jnp.asarray(a, F32).astype(BF16)


def _channel_dft_matrix(n_groups=F_GROUPS):
    c, s = _dft_cos_sin(F_GROUP_DIM)
    scale = F_GROUP_DIM ** -0.5
    eye = np.eye(n_groups)
    return np.concatenate([np.kron(eye, c), np.kron(eye, s)], axis=1) * scale


def _fft_stage1_kernel(f_ref, perm_ref, w1_ref, cs_ref, sc_ref, tc_ref, ts_ref, zr_ref, zi_ref, *, n_slow, nt):
    x = f_ref[0].reshape(n_slow * nt, F_WIDTH).astype(BF16)
    x = _dot(perm_ref[...], x).astype(BF16)
    ab = [_dot(x[:, p * LANES:(p + 1) * LANES], w1_ref[...]).astype(BF16) for p in range(F_WIDTH // LANES)]
    a_all = jnp.concatenate([blk[:, :LANES] for blk in ab], axis=-1)
    b_all = jnp.concatenate([blk[:, LANES:] for blk in ab], axis=-1)
    for t in range(nt):
        a = a_all[t * n_slow:(t + 1) * n_slow]
        b = b_all[t * n_slow:(t + 1) * n_slow]
        z = _dot(cs_ref[...], a) + _dot(sc_ref[...], b)
        zr, zi = z[:n_slow], z[n_slow:]
        c, s = tc_ref[t], ts_ref[t]
        zr_ref[0, t] = (zr * c - zi * s).astype(BF16)
        zi_ref[0, t] = (zr * s + zi * c).astype(BF16)


def _fft_stage2_kernel(zr_ref, zi_ref, f_ref, y_ref, *, n_fast, kb):
    sub = f_ref.shape[0] // n_fast
    parts = []
    for j0 in range(0, kb, sub):
        rhs = jnp.concatenate([zr_ref[0, :, j0:j0 + sub, :].reshape(n_fast * sub, F_WIDTH),
                               zi_ref[0, :, j0:j0 + sub, :].reshape(n_fast * sub, F_WIDTH)], axis=0)
        parts.append(_dot(f_ref[...], rhs).astype(BF16).reshape(n_fast, sub, F_WIDTH))
    y_ref[0] = jnp.concatenate(parts, axis=1)


def _fourier_positions(f, n_slow, n_fast):
    bsz, n, _ = f.shape
    assert n == n_slow * n_fast
    nt = SUBLANES
    kb = 4 * SUBLANES
    w1 = _bf16_table(_channel_dft_matrix(LANES // F_GROUP_DIM))
    perm = _bf16_table(np.eye(n_slow * nt).reshape(n_slow, nt, n_slow * nt).transpose(1, 0, 2)
                       .reshape(n_slow * nt, n_slow * nt))
    c1, s1 = _dft_cos_sin(n_slow)
    sc1 = n_slow ** -0.5
    cs = _bf16_table(np.concatenate([c1, s1], axis=0) * sc1)
    sc = _bf16_table(np.concatenate([-s1, c1], axis=0) * sc1)
    tw = 2.0 * np.pi * np.outer(np.arange(n_fast), np.arange(n_slow)) / n
    tc = jnp.asarray(np.cos(tw)[:, :, None], F32)
    ts = jnp.asarray(np.sin(tw)[:, :, None], F32)
    z_shape = jax.ShapeDtypeStruct((bsz, n_fast, n_slow, F_WIDTH), BF16)
    z_spec = pl.BlockSpec((1, nt, n_slow, F_WIDTH), lambda b, j: (b, j, 0, 0))
    const2 = lambda b, j: (0, 0)
    tw_spec = pl.BlockSpec((nt, n_slow, 1), lambda b, j: (j, 0, 0))
    zr, zi = pl.pallas_call(
        functools.partial(_fft_stage1_kernel, n_slow=n_slow, nt=nt),
        grid=(bsz, n_fast // nt),
        in_specs=[pl.BlockSpec((1, n_slow, nt, F_WIDTH), lambda b, j: (b, 0, j, 0)),
                  pl.BlockSpec(perm.shape, const2),
                  pl.BlockSpec(w1.shape, const2), pl.BlockSpec(cs.shape, const2), pl.BlockSpec(sc.shape, const2),
                  tw_spec, tw_spec],
        out_specs=[z_spec, z_spec],
        out_shape=[z_shape, z_shape],
        compiler_params=_cparams(2),
        name="fnet_stage1",
    )(f.reshape(bsz, n_slow, n_fast, F_WIDTH), perm, w1, cs, sc, tc, ts)

    c2, s2 = _dft_cos_sin(n_fast)
    sc2 = n_fast ** -0.5
    eye = np.eye(SUBLANES)
    f2 = _bf16_table(np.concatenate([np.kron(c2, eye), np.kron(-s2, eye)], axis=1) * sc2)
    blk = pl.BlockSpec((1, n_fast, kb, F_WIDTH), lambda b, j: (b, 0, j, 0))
    y = pl.pallas_call(
        functools.partial(_fft_stage2_kernel, n_fast=n_fast, kb=kb),
        grid=(bsz, n_slow // kb),
        in_specs=[blk, blk, pl.BlockSpec(f2.shape, const2)],
        out_specs=blk,
        out_shape=jax.ShapeDtypeStruct((bsz, n_fast, n_slow, F_WIDTH), BF16),
        compiler_params=_cparams(2),
        name="fnet_stage2",
    )(zr, zi, f2)
    return y.reshape(bsz, n, F_WIDTH)


def _second_largest_sum(a, b, c, d):
    mab, nab = jnp.maximum(a, b), jnp.minimum(a, b)
    mcd, ncd = jnp.maximum(c, d), jnp.minimum(c, d)
    return jnp.maximum(mab, mcd) + jnp.maximum(jnp.minimum(mab, mcd), jnp.maximum(nab, ncd))


def _selected_group(sb_rows):
    epg = EXPERTS_PER_GROUP
    g_score = [_second_largest_sum(*sb_rows[g * epg:(g + 1) * epg]) for g in range(N_GROUPS)]
    best = functools.reduce(jnp.maximum, g_score)
    group = jnp.full_like(best, float(N_GROUPS - 1))
    for g in range(N_GROUPS - 2, -1, -1):
        group = jnp.where(g_score[g] == best, float(g), group)
    return group


def _top2_gates(cand_s, cand_sb):
    n = len(cand_s)
    w = []
    for j in range(n):
        rank = jnp.zeros_like(cand_sb[j])
        for i in range(n):
            if i == j:
                continue
            ahead = (cand_sb[i] > cand_sb[j]) | ((cand_sb[i] == cand_sb[j]) & (i < j))
            rank = rank + jnp.where(ahead, 1.0, 0.0)
        w.append(jnp.where(rank < 2.0, cand_s[j], 0.0))
    total = functools.reduce(jnp.add, w)
    return [wj / total for wj in w]


CHUNK_ROWS = 16


def _local_rows(tm):
    need = tm + N_GROUPS * (CHUNK_ROWS - 1) + CHUNK_ROWS
    return -(-need // CHUNK_ROWS) * CHUNK_ROWS


def _sort_matrix(lslot, lr):
    r_iota = lax.broadcasted_iota(jnp.int32, (lr, 1), 0).astype(F32)
    return jnp.where(r_iota == lslot, 1.0, 0.0).astype(BF16)


def _local_sort(group, h2b, tri_ref, lr):
    t = group.shape[1]
    g_iota = lax.broadcasted_iota(jnp.int32, (SUBLANES, 1), 0).astype(F32)
    onehot = g_iota == group
    prefix = _dot(jnp.where(onehot, 1.0, 0.0).astype(BF16), tri_ref[...])
    count = prefix[:, t - 1:t]
    padded = jnp.floor((count + (CHUNK_ROWS - 1.0)) * (1.0 / CHUNK_ROWS)) * CHUNK_ROWS
    lslot = jnp.zeros((1, t), F32)
    start = jnp.zeros((1, 1), F32)
    for g in range(N_GROUPS):
        lslot = jnp.where(onehot[g:g + 1], start + prefix[g:g + 1] - 1.0, lslot)
        start = start + padded[g:g + 1]
    return lslot, count, _dot(_sort_matrix(lslot, lr), h2b).astype(BF16)


def _out_kernel(a_ref, y_ref, x_ref, m_ref, wo_ref, g_ref, b_ref, wr_ref, rb_ref, tri_ref,
                x1_ref, h2_ref, lslot_ref, count_ref):
    tm = tri_ref.shape[0]
    lr = h2_ref.shape[0] // count_ref.shape[0]

    def project(s):
        rows = slice(s * tm, (s + 1) * tm)
        return (_dot(a_ref[0, rows, :], wo_ref[:NA_WIDTH, :])
                + _dot(y_ref[0, rows, :].astype(BF16), wo_ref[NA_WIDTH:, :]))

    def finish(s, o):
        rows = slice(s * tm, (s + 1) * tm)
        z = DEEPNORM_ALPHA * x_ref[0, rows, :] + m_ref[0, 2:3, :] * o
        x1 = _layer_norm(z) * g_ref[...] + b_ref[...]
        x1_ref[0, rows, :] = x1
        h2 = (_layer_norm(x1) * (1.0 + m_ref[0, 4:5, :]) + m_ref[0, 3:4, :]).astype(BF16)
        sb = jax.nn.sigmoid(_dot_nt(wr_ref[...], h2)) + rb_ref[...]
        group = _selected_group([sb[e:e + 1] for e in range(N_EXPERTS)])
        lslot, count, h2_sorted = _local_sort(group, h2, tri_ref, lr)
        h2_ref[s * lr:(s + 1) * lr, :] = h2_sorted
        lslot_ref[0, :, rows] = jnp.broadcast_to(lslot, (lslot_ref.shape[1], tm))
        count_ref[s] = jnp.broadcast_to(count, count_ref.shape[1:])

    n_sub = count_ref.shape[0]
    projected = project(0)
    for s in range(n_sub):
        following = project(s + 1) if s + 1 < n_sub else None
        finish(s, projected)
        projected = following


def _fold_kernel(a_ref, b_ref, o_ref):
    o_ref[...] = jnp.dot(a_ref[...], b_ref[...], preferred_element_type=F32, precision=lax.Precision.HIGHEST)


def _fold_fourier_map(w_four, w_out):
    bd = _block_diag(w_four.astype(F32))
    folded = pl.pallas_call(
        _fold_kernel,
        out_shape=jax.ShapeDtypeStruct((F_WIDTH, D_MODEL), F32),
        compiler_params=pltpu.CompilerParams(vmem_limit_bytes=VMEM_LIMIT_BYTES),
        name="fold_fourier_map",
    )(bd, w_out[NA_WIDTH:].astype(F32))
    return jnp.concatenate([w_out[:NA_WIDTH].astype(F32), folded], axis=0)


def _out_projection(attn, yf, x, m, w_out_bf16, ln_g, ln_b, w_router_t, router_bias, tm):
    bsz, length, _ = x.shape
    tiles_per_batch = length // tm
    n_tiles = bsz * tiles_per_batch
    per_step = 2 if tiles_per_batch % 2 == 0 else 1
    steps_per_batch = tiles_per_batch // per_step
    ts = per_step * tm
    lr = _local_rows(tm)
    tri = _bf16_table(np.triu(np.ones((tm, tm))))
    row = lambda b, i: (b, i, 0)
    const2 = lambda b, i: (0, 0)
    return pl.pallas_call(
        _out_kernel,
        grid=(bsz, steps_per_batch),
        in_specs=[pl.BlockSpec((1, ts, NA_WIDTH), row), pl.BlockSpec((1, ts, F_WIDTH), row),
                  pl.BlockSpec((1, ts, D_MODEL), row),
                  pl.BlockSpec((1, N_MOD, D_MODEL), lambda b, i: (b, 0, 0)),
                  pl.BlockSpec((NA_WIDTH + F_WIDTH, D_MODEL), const2),
                  pl.BlockSpec((1, D_MODEL), const2), pl.BlockSpec((1, D_MODEL), const2),
                  pl.BlockSpec((N_EXPERTS, D_MODEL), const2), pl.BlockSpec((N_EXPERTS, 1), const2),
                  pl.BlockSpec((tm, tm), const2)],
        out_specs=[pl.BlockSpec((1, ts, D_MODEL), row),
                   pl.BlockSpec((per_step * lr, D_MODEL), lambda b, i: (b * steps_per_batch + i, 0)),
                   pl.BlockSpec((1, SUBLANES, ts), lambda b, i: (b, 0, i)),
                   pl.BlockSpec((per_step, SUBLANES, LANES), lambda b, i: (b * steps_per_batch + i, 0, 0))],
        out_shape=[jax.ShapeDtypeStruct((bsz, length, D_MODEL), F32),
                   jax.ShapeDtypeStruct((n_tiles * lr, D_MODEL), BF16),
                   jax.ShapeDtypeStruct((bsz, SUBLANES, length), F32),
                   jax.ShapeDtypeStruct((n_tiles, SUBLANES, LANES), F32)],
        compiler_params=_cparams(2),
        name="out_proj_norm_route",
    )(attn, yf, x, m, w_out_bf16, ln_g.reshape(1, D_MODEL), ln_b.reshape(1, D_MODEL),
      w_router_t, router_bias.reshape(N_EXPERTS, 1), tri)


def _run_plan(count, tm, lr, tm_slots):
    n_tiles = count.shape[0]
    ch = CHUNK_ROWS
    n_slots = -(-(n_tiles * (tm + N_GROUPS * (ch - 1)) + N_GROUPS * tm_slots) // tm_slots) * tm_slots
    zero_chunk = lr // ch - 1
    pad = ((count + ch - 1) // ch) * ch
    lstart = jnp.cumsum(pad, axis=1) - pad
    run_off = jnp.cumsum(pad, axis=0) - pad
    seg_len = jnp.sum(pad, axis=0)
    seg_pad = ((seg_len + tm_slots - 1) // tm_slots) * tm_slots
    seg_end = jnp.cumsum(seg_pad)
    seg_start = seg_end - seg_pad

    s = (jnp.arange(n_slots // ch, dtype=jnp.int32) * ch)[:, None, None]
    in_run = (seg_start + run_off <= s) & (s < seg_start + run_off + pad)
    src_row = jnp.arange(n_tiles, dtype=jnp.int32)[None, :, None] * lr + lstart + (s - seg_start - run_off)
    src_row = jnp.sum(jnp.where(in_run, src_row, 0), axis=(1, 2))
    src_chunk = jnp.where(jnp.any(in_run, axis=(1, 2)), src_row // ch, zero_chunk).astype(jnp.int32)

    tile_start = jnp.arange(n_slots // tm_slots, dtype=jnp.int32) * tm_slots
    tile_group = jnp.minimum(jnp.sum((seg_end[None, :] <= tile_start[:, None]).astype(jnp.int32), axis=1),
                             N_GROUPS - 1)
    n_used = (seg_end[N_GROUPS - 1:] // tm_slots).astype(jnp.int32)

    r = (jnp.arange(lr // ch, dtype=jnp.int32) * ch)[None, :, None]
    in_local = (lstart[:, None, :] <= r) & (r < (lstart + pad)[:, None, :])
    slot = (seg_start + run_off - lstart)[:, None, :] + r
    back_chunk = (jnp.sum(jnp.where(in_local, slot, 0), axis=2) // ch).astype(jnp.int32)
    return src_chunk, tile_group.astype(jnp.int32), n_used, back_chunk.reshape(-1)


def _chunk_gather(idx_ref, tile, n_chunks, src_hbm, buf, sems, wait):
    slot = tile % 2
    rows = n_chunks * CHUNK_ROWS
    if wait:
        pltpu.make_async_copy(src_hbm.at[pl.ds(0, rows), :], buf.at[slot], sems.at[slot]).wait()
        return
    for c in range(n_chunks):
        src = pl.multiple_of(idx_ref[tile * n_chunks + c] * CHUNK_ROWS, CHUNK_ROWS)
        pltpu.make_async_copy(src_hbm.at[pl.ds(src, CHUNK_ROWS), :],
                              buf.at[slot, pl.ds(c * CHUNK_ROWS, CHUNK_ROWS), :], sems.at[slot]).start()


def _moe_runs_kernel(src_ref, tile_group_ref, n_used_ref, h_hbm, wg32_ref, wu32_ref, wd32_ref, wr_ref, rb_ref,
                     y_ref, hbuf, wg_ref, wu_ref, wd_ref, sems, *, tm):
    step = pl.program_id(0)
    n_used = n_used_ref[0]
    n_chunks = tm // CHUNK_ROWS
    new_group = (step == 0) | (tile_group_ref[step] != tile_group_ref[jnp.maximum(step - 1, 0)])

    @pl.when(new_group & (step < n_used))
    def _():
        wg_ref[...] = wg32_ref[...].astype(BF16)
        wu_ref[...] = wu32_ref[...].astype(BF16)
        wd_ref[...] = wd32_ref[...].astype(BF16)

    @pl.when((step == 0) & (n_used > 0))
    def _():
        _chunk_gather(src_ref, 0, n_chunks, h_hbm, hbuf, sems, wait=False)

    @pl.when(step + 1 < n_used)
    def _():
        _chunk_gather(src_ref, step + 1, n_chunks, h_hbm, hbuf, sems, wait=False)

    @pl.when(step < n_used)
    def _():
        _chunk_gather(src_ref, step, n_chunks, h_hbm, hbuf, sems, wait=True)
        h = hbuf[step % 2]
        s = jax.nn.sigmoid(_dot(h, wr_ref[0]))
        sb = s + rb_ref[0]
        epg = EXPERTS_PER_GROUP
        gates = _top2_gates([s[:, j:j + 1] for j in range(epg)], [sb[:, j:j + 1] for j in range(epg)])
        acc = None
        gate_up = [(_dot(h, wg_ref[0]), _dot(h, wu_ref[0]))]
        for e in range(epg):
            if e + 1 < epg:
                gate_up.append((_dot(h, wg_ref[e + 1]), _dot(h, wu_ref[e + 1])))
            gate, up = gate_up[e]
            hid = (gate * jax.nn.sigmoid(gate)) * up * gates[e]
            y = _dot(hid.astype(BF16), wd_ref[e])
            acc = y if acc is None else acc + y
        y_ref[...] = acc.astype(BF16)

    @pl.when(step >= n_used)
    def _():
        y_ref[...] = jnp.zeros_like(y_ref)


def _moe_runs(h_sorted, src_chunk, tile_group, n_used, layer, w_gate, w_up, w_down, w_router_grp,
              router_bias_grp, tm):
    n_slots = src_chunk.shape[0] * CHUNK_ROWS
    epg = EXPERTS_PER_GROUP
    by_group = lambda i, src_ref, tg_ref, nu_ref: (tg_ref[i], 0, 0)
    by_layer_group = lambda i, src_ref, tg_ref, nu_ref: (layer * N_GROUPS + tg_ref[i], 0, 0)
    grid_spec = pltpu.PrefetchScalarGridSpec(
        num_scalar_prefetch=3,
        grid=(n_slots // tm,),
        in_specs=[pl.BlockSpec(memory_space=pl.ANY),
                  pl.BlockSpec((epg, D_MODEL, D_EXPERT), by_layer_group),
                  pl.BlockSpec((epg, D_MODEL, D_EXPERT), by_layer_group),
                  pl.BlockSpec((epg, D_EXPERT, D_MODEL), by_layer_group),
                  pl.BlockSpec((1, D_MODEL, LANES), by_group),
                  pl.BlockSpec((1, 1, LANES), by_group)],
        out_specs=pl.BlockSpec((tm, D_MODEL), lambda i, src_ref, tg_ref, nu_ref: (i, 0)),
        scratch_shapes=[pltpu.VMEM((2, tm, D_MODEL), BF16),
                        pltpu.VMEM((epg, D_MODEL, D_EXPERT), BF16), pltpu.VMEM((epg, D_MODEL, D_EXPERT), BF16),
                        pltpu.VMEM((epg, D_EXPERT, D_MODEL), BF16), pltpu.SemaphoreType.DMA((2,))],
    )
    return pl.pallas_call(
        functools.partial(_moe_runs_kernel, tm=tm),
        grid_spec=grid_spec,
        out_shape=jax.ShapeDtypeStruct((n_slots, D_MODEL), BF16),
        compiler_params=_cparams(1),
        name="moe_group_experts",
    )(src_chunk, tile_group, n_used, h_sorted, w_gate, w_up, w_down, w_router_grp, router_bias_grp)


def _gather_unsort(back_ref, lslot_ref, y_hbm, ybuf, sems, step, n_steps, lr):
    per_step = ybuf.shape[1] // lr
    tm = lslot_ref.shape[2] // per_step
    n_chunks = per_step * lr // CHUNK_ROWS

    @pl.when(step == 0)
    def _():
        _chunk_gather(back_ref, 0, n_chunks, y_hbm, ybuf, sems, wait=False)

    @pl.when(step + 1 < n_steps)
    def _():
        _chunk_gather(back_ref, step + 1, n_chunks, y_hbm, ybuf, sems, wait=False)

    _chunk_gather(back_ref, step, n_chunks, y_hbm, ybuf, sems, wait=True)
    outs = []
    for t in range(per_step):
        sort = _sort_matrix(lslot_ref[0, 0:1, t * tm:(t + 1) * tm], lr)
        outs.append(lax.dot_general(sort, ybuf[step % 2, t * lr:(t + 1) * lr, :], (((0,), (0,)), ((), ())),
                                    preferred_element_type=F32))
    return outs


def _residual_norm_kernel(back_ref, y_hbm, lslot_ref, x1_ref, m_ref, g_ref, b_ref, o_ref, ybuf, sems, *, n_steps, lr):
    ys = _gather_unsort(back_ref, lslot_ref, y_hbm, ybuf, sems, pl.program_id(0), n_steps, lr)
    tm = x1_ref.shape[0] // len(ys)
    for t, y in enumerate(ys):
        rows = slice(t * tm, (t + 1) * tm)
        z = DEEPNORM_ALPHA * x1_ref[rows, :] + m_ref[0, 5:6, :] * y
        o_ref[rows, :] = _layer_norm(z) * g_ref[...] + b_ref[...]


def _residual_norm(y_slots, back_chunk, lslot, x1, m, ln_g, ln_b, tm):
    bsz, length, _ = x1.shape
    n = bsz * length
    tiles_per_batch = length // tm
    per_step = 2 if tiles_per_batch % 2 == 0 else 1
    steps_per_batch = tiles_per_batch // per_step
    ts = per_step * tm
    lr = _local_rows(tm)
    const2 = lambda i, back_ref: (0, 0)
    row = pl.BlockSpec((ts, D_MODEL), lambda i, back_ref: (i, 0))
    grid_spec = pltpu.PrefetchScalarGridSpec(
        num_scalar_prefetch=1,
        grid=(n // ts,),
        in_specs=[pl.BlockSpec(memory_space=pl.ANY),
                  pl.BlockSpec((1, SUBLANES, ts),
                               lambda i, back_ref: (i // steps_per_batch, 0, i % steps_per_batch)),
                  row, pl.BlockSpec((1, N_MOD, D_MODEL), lambda i, back_ref: (i // steps_per_batch, 0, 0)),
                  pl.BlockSpec((1, D_MODEL), const2), pl.BlockSpec((1, D_MODEL), const2)],
        out_specs=row,
        scratch_shapes=[pltpu.VMEM((2, per_step * lr, D_MODEL), BF16), pltpu.SemaphoreType.DMA((2,))],
    )
    out = pl.pallas_call(
        functools.partial(_residual_norm_kernel, n_steps=n // ts, lr=lr),
        grid_spec=grid_spec,
        out_shape=jax.ShapeDtypeStruct((n, D_MODEL), F32),
        compiler_params=_cparams(1),
        name="moe_residual_norm",
    )(back_chunk, y_slots, lslot, x1.reshape(n, D_MODEL), m, ln_g.reshape(1, D_MODEL), ln_b.reshape(1, D_MODEL))
    return out.reshape(bsz, length, D_MODEL)


def _grouped_moe_runs(h_sorted, count_rows, layer, w_gate, w_up, w_down, w_router_grp, router_bias_grp,
                      tm_tokens, tm_slots):
    count = count_rows[:, :N_GROUPS, 0].astype(jnp.int32)
    src_chunk, tile_group, n_used, back_chunk = _run_plan(count, tm_tokens, _local_rows(tm_tokens), tm_slots)
    y_slots = _moe_runs(h_sorted, src_chunk, tile_group, n_used, layer, w_gate, w_up, w_down, w_router_grp,
                        router_bias_grp, tm_slots)
    return y_slots, back_chunk


def _ctx_fourier_kernel(f_ref, w1_ref, c_ref, s_ref, y_ref):
    ab = _dot(f_ref[0].astype(BF16), w1_ref[...]).astype(BF16)
    y = _dot(c_ref[...], ab[:, :F_WIDTH]) + _dot(s_ref[...], ab[:, F_WIDTH:])
    y_ref[0] = y.astype(BF16)


def _context_fourier(fc):
    bsz, n, _ = fc.shape
    w1 = _bf16_table(_channel_dft_matrix())
    c, s = _dft_cos_sin(n)
    cm = _bf16_table(c * n ** -0.5)
    sm = _bf16_table(-s * n ** -0.5)
    const2 = lambda b: (0, 0)
    blk = pl.BlockSpec((1, n, F_WIDTH), lambda b: (b, 0, 0))
    return pl.pallas_call(
        _ctx_fourier_kernel,
        grid=(bsz,),
        in_specs=[blk, pl.BlockSpec(w1.shape, const2), pl.BlockSpec(cm.shape, const2),
                  pl.BlockSpec(sm.shape, const2)],
        out_specs=blk,
        out_shape=jax.ShapeDtypeStruct((bsz, n, F_WIDTH), BF16),
        compiler_params=_cparams(1),
        name="context_fnet",
    )(fc, w1, cm, sm)


def _block_diag(w):
    g, c, _ = w.shape
    eye = jnp.eye(g, dtype=w.dtype)
    return (eye[:, None, :, None] * w[:, :, None, :]).reshape(g * c, g * c)


def kernel(x, c, ctx, c_ctx, w_mod, b_mod, w_in, rpb, w_four, w_out, ln1_g, ln1_b, ln2_g, ln2_b,
           w_router, router_bias, w_gate, w_up, w_down):
    bsz, length, _ = x.shape
    n_ctx = ctx.shape[1]
    rows = length // GRID_W

    cvec = jnp.concatenate([c, c_ctx[None, :], jnp.zeros((8 - bsz - 1, D_MODEL), F32)], axis=0)
    mods = _modulation(cvec, w_mod, b_mod)
    w_router_t = w_router.T.astype(BF16)
    toeplitz = _rpb_toeplitz(rpb)
    lane_pad = LANES - EXPERTS_PER_GROUP
    w_router_grp = jnp.pad(w_router.reshape(D_MODEL, N_GROUPS, EXPERTS_PER_GROUP).transpose(1, 0, 2),
                           ((0, 0), (0, 0), (0, lane_pad))).astype(BF16)
    router_bias_grp = jnp.pad(router_bias.astype(F32).reshape(N_GROUPS, 1, EXPERTS_PER_GROUP),
                              ((0, 0), (0, 0), (0, lane_pad)))

    wg = w_gate.reshape(DEPTH * N_EXPERTS, D_MODEL, D_EXPERT)
    wu = w_up.reshape(DEPTH * N_EXPERTS, D_MODEL, D_EXPERT)
    wd = w_down.reshape(DEPTH * N_EXPERTS, D_EXPERT, D_MODEL)

    def latent_mod(i):
        return mods[i, :bsz].reshape(bsz, N_MOD, D_MODEL)

    xc = ctx
    projected = None
    for i in range(DEPTH):
        last = i == DEPTH - 1
        m = latent_mod(i)
        mc = jnp.broadcast_to(mods[i, bsz].reshape(1, N_MOD, D_MODEL), (bsz, N_MOD, D_MODEL))
        w_in_b = w_in[i].astype(BF16)
        w_out_b = _fold_fourier_map(w_four[i], w_out[i]).astype(BF16)

        q, k, v, f = projected if projected is not None else _in_projection(x, m, w_in_b, tm=1024)
        qc, kc, vc, fc = _in_projection(xc, mc, w_in_b, tm=n_ctx)

        attn = _neighborhood_attention(q, k, v, kc, vc, toeplitz, i)
        yf = _fourier_positions(f, n_slow=rows, n_fast=GRID_W)
        x1, h2, lslot, cnt = _out_projection(attn, yf, x, m, w_out_b, ln1_g[i], ln1_b[i],
                                             w_router_t, router_bias, tm=512)
        y, back = _grouped_moe_runs(h2, cnt, i, wg, wu, wd, w_router_grp, router_bias_grp,
                                    tm_tokens=512, tm_slots=512)
        if last:
            return _residual_norm(y, back, lslot, x1, m, ln2_g[i], ln2_b[i], tm=512)
        x, *projected = _norm_in_projection(y, back, lslot, x1, m, ln2_g[i], ln2_b[i], latent_mod(i + 1),
                                            w_in[i + 1].astype(BF16), tm=512)

        attn_c = _context_attention(qc, kc, vc)
        yc = _context_fourier(fc)
        xc1, h2c, lslot_c, cnt_c = _out_projection(attn_c, yc, xc, mc, w_out_b, ln1_g[i], ln1_b[i],
                                                   w_router_t, router_bias, tm=n_ctx)
        yc2, back_c = _grouped_moe_runs(h2c, cnt_c, i, wg, wu, wd, w_router_grp, router_bias_grp,
                                        tm_tokens=n_ctx, tm_slots=128)
        xc = _residual_norm(yc2, back_c, lslot_c, xc1, mc, ln2_g[i], ln2_b[i], tm=n_ctx)
    return x
```

```python
import functools
import math

import numpy as np
import jax
import jax.numpy as jnp
from jax import lax
from jax.experimental import pallas as pl
from jax.experimental.pallas import tpu as pltpu

D_MODEL = 1024
DEPTH = 2
GRID_W = 64
NA_HEADS = 8
HEAD_DIM = 64
NA_WIDTH = NA_HEADS * HEAD_DIM
WIN_ROWS = 8
WIN_COLS = 16
F_GROUPS = 8
F_GROUP_DIM = 64
F_WIDTH = F_GROUPS * F_GROUP_DIM
IN_WIDTH = 3 * NA_WIDTH + F_WIDTH
N_EXPERTS = 16
N_GROUPS = 4
EXPERTS_PER_GROUP = N_EXPERTS // N_GROUPS
D_EXPERT = 256
N_MOD = 6
DEEPNORM_ALPHA = (2.0 * DEPTH) ** 0.25
LN_EPS = 1e-6

F32 = jnp.float32
BF16 = jnp.bfloat16

V7X_VMEM_BYTES = 64 * 1024 * 1024
VMEM_LIMIT_BYTES = (V7X_VMEM_BYTES * 3) // 4
LANES = 128
SUBLANES = 8
HEADS_PER_STEP = LANES // HEAD_DIM
assert HEADS_PER_STEP == 2
MASK_VALUE = -1e30
LOG2_E = math.log2(math.e)

CHUNK_GRID_ROWS = 4
WINDOW_CHUNKS = 3
STEP_CHUNKS = 8


def _cparams(n_grid_dims):
    return pltpu.CompilerParams(dimension_semantics=("arbitrary",) * n_grid_dims,
                                vmem_limit_bytes=VMEM_LIMIT_BYTES)


def _layer_norm(x):
    mu = jnp.mean(x, axis=-1, keepdims=True)
    xc = x - mu
    var = jnp.mean(xc * xc, axis=-1, keepdims=True)
    return xc * lax.rsqrt(var + LN_EPS)


def _dot(a, b):
    return jnp.dot(a, b, preferred_element_type=F32)


def _dot_nt(a, b):
    return lax.dot_general(a, b, (((1,), (1,)), ((), ())), preferred_element_type=F32)


def _mod_kernel(c_ref, w_ref, b_ref, o_ref):
    c = c_ref[...]
    a = c * jax.nn.sigmoid(c)
    o_ref[0] = jnp.dot(a, w_ref[0], preferred_element_type=F32, precision=lax.Precision.HIGHEST) + b_ref[0]


def _modulation(cvec, w_mod, b_mod):
    n_col_blocks = 4
    wc = (N_MOD * D_MODEL) // n_col_blocks
    rows = cvec.shape[0]
    return pl.pallas_call(
        _mod_kernel,
        grid=(DEPTH, n_col_blocks),
        in_specs=[pl.BlockSpec((rows, D_MODEL), lambda i, j: (0, 0)),
                  pl.BlockSpec((1, D_MODEL, wc), lambda i, j: (i, 0, j)),
                  pl.BlockSpec((1, 1, wc), lambda i, j: (i, 0, j))],
        out_specs=pl.BlockSpec((1, rows, wc), lambda i, j: (i, 0, j)),
        out_shape=jax.ShapeDtypeStruct((DEPTH, rows, N_MOD * D_MODEL), F32),
        compiler_params=_cparams(2),
        name="modulation",
    )(cvec, w_mod, b_mod.reshape(DEPTH, 1, N_MOD * D_MODEL))


def _modulate(x, m_ref):
    return (_layer_norm(x) * (1.0 + m_ref[0, 1:2, :]) + m_ref[0, 0:1, :]).astype(BF16)


def _project(h, rows, w_ref, q_ref, k_ref, v_ref, f_ref):
    p = _dot(h, w_ref[...])
    q_ref[0, rows, :] = (p[:, :NA_WIDTH] * (HEAD_DIM ** -0.5 * LOG2_E)).astype(BF16)
    k_ref[0, rows, :] = p[:, NA_WIDTH:2 * NA_WIDTH].astype(BF16)
    v_ref[0, rows, :] = p[:, 2 * NA_WIDTH:3 * NA_WIDTH].astype(BF16)
    f_ref[0, rows, :] = p[:, 3 * NA_WIDTH:]


def _proj_kernel(x_ref, m_ref, w_ref, q_ref, k_ref, v_ref, f_ref):
    tm = x_ref.shape[1]
    halves = [slice(0, tm // 2), slice(tm // 2, tm)] if tm % 512 == 0 else [slice(0, tm)]
    hs = [_modulate(x_ref[0, rows, :], m_ref) for rows in halves]
    for rows, h in zip(halves, hs):
        _project(h, rows, w_ref, q_ref, k_ref, v_ref, f_ref)


def _norm_proj_kernel(back_ref, y_hbm, lslot_ref, x1_ref, m_ref, g_ref, b_ref, m_next_ref, w_ref,
                      x_ref, q_ref, k_ref, v_ref, f_ref, ybuf, sems, *, steps_per_batch, n_steps, lr):
    step = pl.program_id(0) * steps_per_batch + pl.program_id(1)
    ys = _gather_unsort(back_ref, lslot_ref, y_hbm, ybuf, sems, step, n_steps, lr)
    tm = x1_ref.shape[1] // len(ys)
    hs = []
    for t, y in enumerate(ys):
        rows = slice(t * tm, (t + 1) * tm)
        z = DEEPNORM_ALPHA * x1_ref[0, rows, :] + m_ref[0, 5:6, :] * y
        x = _layer_norm(z) * g_ref[...] + b_ref[...]
        x_ref[0, rows, :] = x
        hs.append(_modulate(x, m_next_ref))
    for t, h in enumerate(hs):
        _project(h, slice(t * tm, (t + 1) * tm), w_ref, q_ref, k_ref, v_ref, f_ref)


def _norm_in_projection(y_slots, back_chunk, lslot, x1, m, ln_g, ln_b, m_next, w_in_bf16, tm):
    bsz, length, _ = x1.shape
    tiles_per_batch = length // tm
    per_step = 2 if tiles_per_batch % 2 == 0 else 1
    steps_per_batch = tiles_per_batch // per_step
    ts = per_step * tm
    lr = _local_rows(tm)
    out = jax.ShapeDtypeStruct((bsz, length, NA_WIDTH), BF16)
    out_f = jax.ShapeDtypeStruct((bsz, length, F_WIDTH), F32)
    out_x = jax.ShapeDtypeStruct((bsz, length, D_MODEL), F32)
    row = lambda b, i, back_ref: (b, i, 0)
    const2 = lambda b, i, back_ref: (0, 0)
    mod_spec = pl.BlockSpec((1, N_MOD, D_MODEL), lambda b, i, back_ref: (b, 0, 0))
    o_spec = pl.BlockSpec((1, ts, NA_WIDTH), row)
    x_spec = pl.BlockSpec((1, ts, D_MODEL), row)
    grid_spec = pltpu.PrefetchScalarGridSpec(
        num_scalar_prefetch=1,
        grid=(bsz, steps_per_batch),
        in_specs=[pl.BlockSpec(memory_space=pl.ANY),
                  pl.BlockSpec((1, SUBLANES, ts), lambda b, i, back_ref: (b, 0, i)),
                  x_spec, mod_spec,
                  pl.BlockSpec((1, D_MODEL), const2), pl.BlockSpec((1, D_MODEL), const2),
                  mod_spec, pl.BlockSpec((D_MODEL, IN_WIDTH), const2)],
        out_specs=[x_spec, o_spec, o_spec, o_spec, o_spec],
        scratch_shapes=[pltpu.VMEM((2, per_step * lr, D_MODEL), BF16), pltpu.SemaphoreType.DMA((2,))],
    )
    return pl.pallas_call(
        functools.partial(_norm_proj_kernel, steps_per_batch=steps_per_batch, n_steps=bsz * steps_per_batch,
                          lr=lr),
        grid_spec=grid_spec,
        out_shape=[out_x, out, out, out, out_f],
        compiler_params=_cparams(2),
        name="moe_norm_in_proj",
    )(back_chunk, y_slots, lslot, x1, m, ln_g.reshape(1, D_MODEL), ln_b.reshape(1, D_MODEL), m_next, w_in_bf16)


def _in_projection(x, m, w_in_bf16, tm):
    bsz, length, _ = x.shape
    out = jax.ShapeDtypeStruct((bsz, length, NA_WIDTH), BF16)
    out_f = jax.ShapeDtypeStruct((bsz, length, F_WIDTH), F32)
    o_spec = pl.BlockSpec((1, tm, NA_WIDTH), lambda b, i: (b, i, 0))
    return pl.pallas_call(
        _proj_kernel,
        grid=(bsz, length // tm),
        in_specs=[pl.BlockSpec((1, tm, D_MODEL), lambda b, i: (b, i, 0)),
                  pl.BlockSpec((1, N_MOD, D_MODEL), lambda b, i: (b, 0, 0)),
                  pl.BlockSpec((D_MODEL, IN_WIDTH), lambda b, i: (0, 0))],
        out_specs=[o_spec, o_spec, o_spec, o_spec],
        out_shape=[out, out, out, out_f],
        compiler_params=_cparams(2),
        name="ln_mod_in_proj",
    )(x, m, w_in_bf16)


def _head_lanes(h):
    lane = lax.broadcasted_iota(jnp.int32, (1, LANES), 1)
    return (lane >= HEAD_DIM * h) & (lane < HEAD_DIM * (h + 1))


def _scores_pass(q, h, tiles, s_ref):
    qh = jnp.where(_head_lanes(h), q, jnp.zeros_like(q))
    m = None
    t = tiles[0][0].shape[0]
    for j, (k, _, bias) in enumerate(tiles):
        s = _dot_nt(qh, k)
        if bias is not None:
            s = s + bias
        s_ref[:, j * t:(j + 1) * t] = s
        mj = jnp.max(s, axis=-1, keepdims=True)
        m = mj if m is None else jnp.maximum(m, mj)
    return m


def _pv_pass(h, tiles, s_ref, m):
    o = None
    t = tiles[0][0].shape[0]
    in_head = _head_lanes(h)
    for j, (_, v, _) in enumerate(tiles):
        p = jnp.exp2(s_ref[:, j * t:(j + 1) * t] - m)
        oj = _dot(p.astype(BF16), jnp.where(in_head, v, jnp.ones_like(v)))
        o = oj if o is None else o + oj
    return o / pltpu.roll(o, HEAD_DIM, axis=1)


def _attention_units(units):
    outs = []
    maxima = [_scores_pass(*units[0])]
    for u in range(len(units)):
        if u + 1 < len(units):
            maxima.append(_scores_pass(*units[u + 1]))
        _, h, tiles, s_ref = units[u]
        outs.append(_pv_pass(h, tiles, s_ref, maxima[u]))
    return outs


def _merge_heads(outs):
    merged = outs[0]
    for h in range(1, len(outs)):
        merged = jnp.where(_head_lanes(h), outs[h], merged)
    return merged


DR_PAD = 2 * WIN_ROWS
DC_PAD = 2 * WIN_COLS


def _toeplitz_kernel(r_ref, sel_ref, mask_ref, o_ref):
    n = r_ref.shape[0]
    for qc in range(GRID_W):
        block = jnp.dot(r_ref[...], sel_ref[qc], preferred_element_type=F32, precision=lax.Precision.HIGHEST)
        o_ref[pl.ds(qc, n, stride=GRID_W), :] = block * LOG2_E + mask_ref[qc]


def _rpb_toeplitz(rpb):
    depth, heads, n_dr, n_dc = rpb.shape
    qc = np.arange(GRID_W)[:, None]
    kc = (np.arange(LANES) % GRID_W)[None, :]
    cs = np.clip(qc - WIN_COLS // 2, 0, GRID_W - WIN_COLS)
    col_valid = (kc >= cs) & (kc < cs + WIN_COLS)
    dc = kc - qc + WIN_COLS - 1
    select = (np.arange(DC_PAD)[None, :, None] == dc[:, None, :]) & col_valid[:, None, :]
    mask = np.where(col_valid, 0.0, MASK_VALUE)[:, None, :]
    r = jnp.pad(rpb.astype(F32), ((0, 0), (0, 0), (0, DR_PAD - n_dr), (0, DC_PAD - n_dc)))
    n = depth * heads * DR_PAD
    const3 = lambda: (0, 0, 0)
    out = pl.pallas_call(
        _toeplitz_kernel,
        grid=(),
        in_specs=[pl.BlockSpec((n, DC_PAD), lambda: (0, 0)),
                  pl.BlockSpec((GRID_W, DC_PAD, LANES), const3), pl.BlockSpec((GRID_W, 1, LANES), const3)],
        out_specs=pl.BlockSpec((n * GRID_W, LANES), lambda: (0, 0)),
        out_shape=jax.ShapeDtypeStruct((n * GRID_W, LANES), F32),
        compiler_params=pltpu.CompilerParams(vmem_limit_bytes=VMEM_LIMIT_BYTES),
        name="rpb_toeplitz",
    )(r.reshape(n, DC_PAD), jnp.asarray(select, F32), jnp.asarray(mask, F32))
    return out.reshape(depth * heads, DR_PAD, GRID_W, LANES)


assert WINDOW_CHUNKS * CHUNK_GRID_ROWS >= CHUNK_GRID_ROWS + WIN_ROWS - 1


def _window_start_chunk(cr, n_cr):
    lo = cr - (WINDOW_CHUNKS - 1) // 2
    if isinstance(cr, int):
        return max(0, min(lo, n_cr - WINDOW_CHUNKS))
    return jnp.clip(lo, 0, n_cr - WINDOW_CHUNKS)


def _window_key_rows(cr, n_cr):
    start = _window_start_chunk(cr, n_cr) * CHUNK_GRID_ROWS
    return list(range(start, start + WINDOW_CHUNKS * CHUNK_GRID_ROWS))


def _row_window(qr, rows):
    kh = min(WIN_ROWS, rows)
    rs = int(np.clip(qr - kh // 2, 0, rows - kh))
    return rs, rs + kh


def _window_variant(cr, n_cr):
    return jnp.where(cr == 0, 0, jnp.where(cr == n_cr - 1, 2, 1))


def _check_windows(rows):
    def relative(cr):
        base = cr * CHUNK_GRID_ROWS
        return ([kr - base for kr in _window_key_rows(cr, n_cr)],
                [tuple(r - base for r in _row_window(base + qi, rows)) for qi in range(CHUNK_GRID_ROWS)])

    n_cr = rows // CHUNK_GRID_ROWS
    for cr in range(n_cr):
        assert cr in (0, n_cr - 1) or relative(cr) == relative(1), cr
        have = set(_window_key_rows(cr, n_cr))
        for qi in range(CHUNK_GRID_ROWS):
            lo, hi = _row_window(cr * CHUNK_GRID_ROWS + qi, rows)
            assert set(range(lo, hi)) <= have, (cr, qi)


def _build_bias_tables(t_ref, bias_scr, rows):
    n_cr = rows // CHUNK_GRID_ROWS
    left = lax.broadcasted_iota(jnp.int32, (GRID_W, LANES), 1) < GRID_W
    masked = jnp.full((GRID_W, LANES), MASK_VALUE, F32)
    for variant, cr in enumerate((0, 1, n_cr - 1)):
        key_rows = _window_key_rows(cr, n_cr)
        for h in range(HEADS_PER_STEP):
            for qi in range(CHUNK_GRID_ROWS):
                qr = cr * CHUNK_GRID_ROWS + qi
                lo, hi = _row_window(qr, rows)
                for p in range(len(key_rows) // 2):
                    pair = [t_ref[h, kr - qr + WIN_ROWS - 1] if lo <= kr < hi else None
                            for kr in key_rows[2 * p:2 * p + 2]]
                    if pair[0] is None and pair[1] is None:
                        block = masked
                    else:
                        block = jnp.where(left, masked if pair[0] is None else pair[0],
                                          masked if pair[1] is None else pair[1])
                    bias_scr[variant, h, qi * GRID_W:(qi + 1) * GRID_W, p * LANES:(p + 1) * LANES] = block


def _na_kernel(q_ref, *refs, rows):
    n_kv = STEP_CHUNKS * WINDOW_CHUNKS
    k_refs, v_refs = refs[:n_kv], refs[n_kv:2 * n_kv]
    kc_ref, vc_ref, t_ref, o_ref, bias_scr, s_scr = refs[2 * n_kv:]
    n_cr = rows // CHUNK_GRID_ROWS
    b, rb = pl.program_id(1), pl.program_id(2)

    @pl.when((b == 0) & (rb == 0))
    def _():
        _build_bias_tables(t_ref, bias_scr, rows)

    tq = CHUNK_GRID_ROWS * GRID_W
    units = []
    for c in range(STEP_CHUNKS):
        variant = _window_variant(rb * STEP_CHUNKS + c, n_cr)
        q = q_ref[0, c * tq:(c + 1) * tq, :]
        for h in range(HEADS_PER_STEP):
            lat = [(k_refs[c * WINDOW_CHUNKS + j][0], v_refs[c * WINDOW_CHUNKS + j][0],
                    bias_scr[variant, h, :, j * tq:(j + 1) * tq]) for j in range(WINDOW_CHUNKS)]
            units.append((q, h, lat + [(kc_ref[0], vc_ref[0], None)], s_scr.at[c, h]))
    outs = _attention_units(units)
    for c in range(STEP_CHUNKS):
        o = _merge_heads(outs[c * HEADS_PER_STEP:(c + 1) * HEADS_PER_STEP])
        o_ref[0, c * tq:(c + 1) * tq, :] = o.astype(BF16)


def _neighborhood_attention(q, k, v, kc, vc, toeplitz, layer):
    bsz, length, _ = q.shape
    rows = length // GRID_W
    n_cr = rows // CHUNK_GRID_ROWS
    assert rows % (CHUNK_GRID_ROWS * STEP_CHUNKS) == 0 and n_cr >= WINDOW_CHUNKS + 2
    _check_windows(rows)
    n_ctx = kc.shape[1]
    tq = CHUNK_GRID_ROWS * GRID_W
    assert n_ctx == tq

    def kv_spec(c, j):
        return pl.BlockSpec((1, tq, LANES),
                            lambda hp, b, rb: (b, _window_start_chunk(rb * STEP_CHUNKS + c, n_cr) + j, hp))

    kv_specs = [kv_spec(c, j) for c in range(STEP_CHUNKS) for j in range(WINDOW_CHUNKS)]
    ctx_spec = pl.BlockSpec((1, n_ctx, LANES), lambda hp, b, rb: (b, 0, hp))
    q_spec = pl.BlockSpec((1, STEP_CHUNKS * tq, LANES), lambda hp, b, rb: (b, rb, hp))
    window = WINDOW_CHUNKS * tq
    return pl.pallas_call(
        functools.partial(_na_kernel, rows=rows),
        grid=(NA_HEADS // HEADS_PER_STEP, bsz, n_cr // STEP_CHUNKS),
        in_specs=([q_spec] + kv_specs + kv_specs
                  + [ctx_spec, ctx_spec,
                     pl.BlockSpec((HEADS_PER_STEP, DR_PAD, GRID_W, LANES),
                                  lambda hp, b, rb: (layer * (NA_HEADS // HEADS_PER_STEP) + hp, 0, 0, 0))]),
        out_specs=q_spec,
        out_shape=jax.ShapeDtypeStruct((bsz, length, NA_WIDTH), BF16),
        scratch_shapes=[pltpu.VMEM((3, HEADS_PER_STEP, tq, window), F32),
                        pltpu.VMEM((STEP_CHUNKS, HEADS_PER_STEP, tq, window + n_ctx), F32)],
        compiler_params=_cparams(3),
        name="neighborhood_attention",
    )(q, *([k] * len(kv_specs)), *([v] * len(kv_specs)), kc, vc, toeplitz)


def _ctx_attn_kernel(q_ref, k_ref, v_ref, o_ref, s_scr):
    tiles = [(k_ref[0], v_ref[0], None)]
    outs = _attention_units([(q_ref[0], h, tiles, s_scr.at[h]) for h in range(HEADS_PER_STEP)])
    o_ref[0] = _merge_heads(outs).astype(BF16)


def _context_attention(qc, kc, vc):
    bsz, n_ctx, _ = qc.shape
    spec = pl.BlockSpec((1, n_ctx, LANES), lambda b, hp: (b, 0, hp))
    return pl.pallas_call(
        _ctx_attn_kernel,
        grid=(bsz, NA_HEADS // HEADS_PER_STEP),
        in_specs=[spec, spec, spec],
        out_specs=spec,
        out_shape=jax.ShapeDtypeStruct((bsz, n_ctx, NA_WIDTH), BF16),
        scratch_shapes=[pltpu.VMEM((HEADS_PER_STEP, n_ctx, n_ctx), F32)],
        compiler_params=_cparams(2),
        name="context_attention",
    )(qc, kc, vc)


def _dft_cos_sin(n):
    ang = 2.0 * np.pi * np.outer(np.arange(n), np.arange(n)) / n
    return np.cos(ang), np.sin(ang)


def _bf16_table(a):
    return jnp.asarray(a, F32).astype(BF16)


def _channel_dft_matrix(n_groups=F_GROUPS):
    c, s = _dft_cos_sin(F_GROUP_DIM)
    scale = F_GROUP_DIM ** -0.5
    eye = np.eye(n_groups)
    return np.concatenate([np.kron(eye, c), np.kron(eye, s)], axis=1) * scale


def _fft_stage1_kernel(f_ref, perm_ref, w1_ref, cs_ref, sc_ref, tc_ref, ts_ref, zr_ref, zi_ref, *, n_slow, nt):
    x = f_ref[0].reshape(n_slow * nt, F_WIDTH).astype(BF16)
    x = _dot(perm_ref[...], x).astype(BF16)
    ab = [_dot(x[:, p * LANES:(p + 1) * LANES], w1_ref[...]).astype(BF16) for p in range(F_WIDTH // LANES)]
    a_all = jnp.concatenate([blk[:, :LANES] for blk in ab], axis=-1)
    b_all = jnp.concatenate([blk[:, LANES:] for blk in ab], axis=-1)
    for t in range(nt):
        a = a_all[t * n_slow:(t + 1) * n_slow]
        b = b_all[t * n_slow:(t + 1) * n_slow]
        z = _dot(cs_ref[...], a) + _dot(sc_ref[...], b)
        zr, zi = z[:n_slow], z[n_slow:]
        c, s = tc_ref[t], ts_ref[t]
        zr_ref[0, t] = (zr * c - zi * s).astype(BF16)
        zi_ref[0, t] = (zr * s + zi * c).astype(BF16)


def _fft_stage2_kernel(zr_ref, zi_ref, f_ref, y_ref, *, n_fast, kb):
    sub = f_ref.shape[0] // n_fast
    parts = []
    for j0 in range(0, kb, sub):
        rhs = jnp.concatenate([zr_ref[0, :, j0:j0 + sub, :].reshape(n_fast * sub, F_WIDTH),
                               zi_ref[0, :, j0:j0 + sub, :].reshape(n_fast * sub, F_WIDTH)], axis=0)
        parts.append(_dot(f_ref[...], rhs).astype(BF16).reshape(n_fast, sub, F_WIDTH))
    y_ref[0] = jnp.concatenate(parts, axis=1)


def _fourier_positions(f, n_slow, n_fast):
    bsz, n, _ = f.shape
    assert n == n_slow * n_fast
    nt = SUBLANES
    kb = 4 * SUBLANES
    w1 = _bf16_table(_channel_dft_matrix(LANES // F_GROUP_DIM))
    perm = _bf16_table(np.eye(n_slow * nt).reshape(n_slow, nt, n_slow * nt).transpose(1, 0, 2)
                       .reshape(n_slow * nt, n_slow * nt))
    c1, s1 = _dft_cos_sin(n_slow)
    sc1 = n_slow ** -0.5
    cs = _bf16_table(np.concatenate([c1, s1], axis=0) * sc1)
    sc = _bf16_table(np.concatenate([-s1, c1], axis=0) * sc1)
    tw = 2.0 * np.pi * np.outer(np.arange(n_fast), np.arange(n_slow)) / n
    tc = jnp.asarray(np.cos(tw)[:, :, None], F32)
    ts = jnp.asarray(np.sin(tw)[:, :, None], F32)
    z_shape = jax.ShapeDtypeStruct((bsz, n_fast, n_slow, F_WIDTH), BF16)
    z_spec = pl.BlockSpec((1, nt, n_slow, F_WIDTH), lambda b, j: (b, j, 0, 0))
    const2 = lambda b, j: (0, 0)
    tw_spec = pl.BlockSpec((nt, n_slow, 1), lambda b, j: (j, 0, 0))
    zr, zi = pl.pallas_call(
        functools.partial(_fft_stage1_kernel, n_slow=n_slow, nt=nt),
        grid=(bsz, n_fast // nt),
        in_specs=[pl.BlockSpec((1, n_slow, nt, F_WIDTH), lambda b, j: (b, 0, j, 0)),
                  pl.BlockSpec(perm.shape, const2),
                  pl.BlockSpec(w1.shape, const2), pl.BlockSpec(cs.shape, const2), pl.BlockSpec(sc.shape, const2),
                  tw_spec, tw_spec],
        out_specs=[z_spec, z_spec],
        out_shape=[z_shape, z_shape],
        compiler_params=_cparams(2),
        name="fnet_stage1",
    )(f.reshape(bsz, n_slow, n_fast, F_WIDTH), perm, w1, cs, sc, tc, ts)

    c2, s2 = _dft_cos_sin(n_fast)
    sc2 = n_fast ** -0.5
    eye = np.eye(SUBLANES)
    f2 = _bf16_table(np.concatenate([np.kron(c2, eye), np.kron(-s2, eye)], axis=1) * sc2)
    blk = pl.BlockSpec((1, n_fast, kb, F_WIDTH), lambda b, j: (b, 0, j, 0))
    y = pl.pallas_call(
        functools.partial(_fft_stage2_kernel, n_fast=n_fast, kb=kb),
        grid=(bsz, n_slow // kb),
        in_specs=[blk, blk, pl.BlockSpec(f2.shape, const2)],
        out_specs=blk,
        out_shape=jax.ShapeDtypeStruct((bsz, n_fast, n_slow, F_WIDTH), BF16),
        compiler_params=_cparams(2),
        name="fnet_stage2",
    )(zr, zi, f2)
    return y.reshape(bsz, n, F_WIDTH)


def _second_largest_sum(a, b, c, d):
    mab, nab = jnp.maximum(a, b), jnp.minimum(a, b)
    mcd, ncd = jnp.maximum(c, d), jnp.minimum(c, d)
    return jnp.maximum(mab, mcd) + jnp.maximum(jnp.minimum(mab, mcd), jnp.maximum(nab, ncd))


def _selected_group(sb_rows):
    epg = EXPERTS_PER_GROUP
    g_score = [_second_largest_sum(*sb_rows[g * epg:(g + 1) * epg]) for g in range(N_GROUPS)]
    best = functools.reduce(jnp.maximum, g_score)
    group = jnp.full_like(best, float(N_GROUPS - 1))
    for g in range(N_GROUPS - 2, -1, -1):
        group = jnp.where(g_score[g] == best, float(g), group)
    return group


def _top2_gates(cand_s, cand_sb):
    n = len(cand_s)
    w = []
    for j in range(n):
        rank = jnp.zeros_like(cand_sb[j])
        for i in range(n):
            if i == j:
                continue
            ahead = (cand_sb[i] > cand_sb[j]) | ((cand_sb[i] == cand_sb[j]) & (i < j))
            rank = rank + jnp.where(ahead, 1.0, 0.0)
        w.append(jnp.where(rank < 2.0, cand_s[j], 0.0))
    total = functools.reduce(jnp.add, w)
    return [wj / total for wj in w]


CHUNK_ROWS = 16


def _local_rows(tm):
    need = tm + N_GROUPS * (CHUNK_ROWS - 1) + CHUNK_ROWS
    return -(-need // CHUNK_ROWS) * CHUNK_ROWS


def _sort_matrix(lslot, lr):
    r_iota = lax.broadcasted_iota(jnp.int32, (lr, 1), 0).astype(F32)
    return jnp.where(r_iota == lslot, 1.0, 0.0).astype(BF16)


def _local_sort(group, h2b, tri_ref, lr):
    t = group.shape[1]
    g_iota = lax.broadcasted_iota(jnp.int32, (SUBLANES, 1), 0).astype(F32)
    onehot = g_iota == group
    prefix = _dot(jnp.where(onehot, 1.0, 0.0).astype(BF16), tri_ref[...])
    count = prefix[:, t - 1:t]
    padded = jnp.floor((count + (CHUNK_ROWS - 1.0)) * (1.0 / CHUNK_ROWS)) * CHUNK_ROWS
    lslot = jnp.zeros((1, t), F32)
    start = jnp.zeros((1, 1), F32)
    for g in range(N_GROUPS):
        lslot = jnp.where(onehot[g:g + 1], start + prefix[g:g + 1] - 1.0, lslot)
        start = start + padded[g:g + 1]
    return lslot, count, _dot(_sort_matrix(lslot, lr), h2b).astype(BF16)


def _out_kernel(a_ref, y_ref, x_ref, m_ref, wo_ref, wf_ref, g_ref, b_ref, wr_ref, rb_ref, tri_ref,
                x1_ref, h2_ref, lslot_ref, count_ref):
    tm = tri_ref.shape[0]
    lr = h2_ref.shape[0] // count_ref.shape[0]

    def project(s):
        rows = slice(s * tm, (s + 1) * tm)
        return _dot(a_ref[0, rows, :], wo_ref[0]) + _dot(y_ref[0, rows, :].astype(BF16), wf_ref[0])

    def finish(s, o):
        rows = slice(s * tm, (s + 1) * tm)
        z = DEEPNORM_ALPHA * x_ref[0, rows, :] + m_ref[0, 2:3, :] * o
        x1 = _layer_norm(z) * g_ref[...] + b_ref[...]
        x1_ref[0, rows, :] = x1
        h2 = (_layer_norm(x1) * (1.0 + m_ref[0, 4:5, :]) + m_ref[0, 3:4, :]).astype(BF16)
        sb = jax.nn.sigmoid(_dot_nt(wr_ref[...], h2)) + rb_ref[...]
        group = _selected_group([sb[e:e + 1] for e in range(N_EXPERTS)])
        lslot, count, h2_sorted = _local_sort(group, h2, tri_ref, lr)
        h2_ref[s * lr:(s + 1) * lr, :] = h2_sorted
        lslot_ref[0, :, rows] = jnp.broadcast_to(lslot, (lslot_ref.shape[1], tm))
        count_ref[s] = jnp.broadcast_to(count, count_ref.shape[1:])

    n_sub = count_ref.shape[0]
    projected = project(0)
    for s in range(n_sub):
        following = project(s + 1) if s + 1 < n_sub else None
        finish(s, projected)
        projected = following


def _fold_kernel(a_ref, b_ref, o_ref):
    o_ref[0] = jnp.dot(a_ref[0], b_ref[0], preferred_element_type=F32,
                       precision=lax.Precision.HIGHEST).astype(BF16)


def _fold_fourier_maps(w_four, w_out):
    depth = w_four.shape[0]
    assert NA_WIDTH == F_WIDTH
    bd = jax.vmap(_block_diag)(w_four.astype(F32))
    return pl.pallas_call(
        _fold_kernel,
        grid=(depth,),
        in_specs=[pl.BlockSpec((1, F_WIDTH, F_WIDTH), lambda l: (l, 0, 0)),
                  pl.BlockSpec((1, F_WIDTH, D_MODEL), lambda l: (l, 1, 0))],
        out_specs=pl.BlockSpec((1, F_WIDTH, D_MODEL), lambda l: (l, 0, 0)),
        out_shape=jax.ShapeDtypeStruct((depth, F_WIDTH, D_MODEL), BF16),
        compiler_params=_cparams(1),
        name="fold_fourier_map",
    )(bd, w_out.astype(F32))


def _out_projection(attn, yf, x, m, layer, w_out_bf16, w_fourier_bf16, ln_g, ln_b, w_router_t, router_bias, tm):
    bsz, length, _ = x.shape
    tiles_per_batch = length // tm
    n_tiles = bsz * tiles_per_batch
    per_step = 2 if tiles_per_batch % 2 == 0 else 1
    steps_per_batch = tiles_per_batch // per_step
    ts = per_step * tm
    lr = _local_rows(tm)
    tri = _bf16_table(np.triu(np.ones((tm, tm))))
    row = lambda b, i: (b, i, 0)
    const2 = lambda b, i: (0, 0)
    return pl.pallas_call(
        _out_kernel,
        grid=(bsz, steps_per_batch),
        in_specs=[pl.BlockSpec((1, ts, NA_WIDTH), row), pl.BlockSpec((1, ts, F_WIDTH), row),
                  pl.BlockSpec((1, ts, D_MODEL), row),
                  pl.BlockSpec((1, N_MOD, D_MODEL), lambda b, i: (b, 0, 0)),
                  pl.BlockSpec((1, NA_WIDTH, D_MODEL), lambda b, i: (layer, 0, 0)),
                  pl.BlockSpec((1, F_WIDTH, D_MODEL), lambda b, i: (layer, 0, 0)),
                  pl.BlockSpec((1, D_MODEL), const2), pl.BlockSpec((1, D_MODEL), const2),
                  pl.BlockSpec((N_EXPERTS, D_MODEL), const2), pl.BlockSpec((N_EXPERTS, 1), const2),
                  pl.BlockSpec((tm, tm), const2)],
        out_specs=[pl.BlockSpec((1, ts, D_MODEL), row),
                   pl.BlockSpec((per_step * lr, D_MODEL), lambda b, i: (b * steps_per_batch + i, 0)),
                   pl.BlockSpec((1, SUBLANES, ts), lambda b, i: (b, 0, i)),
                   pl.BlockSpec((per_step, SUBLANES, LANES), lambda b, i: (b * steps_per_batch + i, 0, 0))],
        out_shape=[jax.ShapeDtypeStruct((bsz, length, D_MODEL), F32),
                   jax.ShapeDtypeStruct((n_tiles * lr, D_MODEL), BF16),
                   jax.ShapeDtypeStruct((bsz, SUBLANES, length), F32),
                   jax.ShapeDtypeStruct((n_tiles, SUBLANES, LANES), F32)],
        compiler_params=_cparams(2),
        name="out_proj_norm_route",
    )(attn, yf, x, m, w_out_bf16, w_fourier_bf16, ln_g.reshape(1, D_MODEL), ln_b.reshape(1, D_MODEL),
      w_router_t, router_bias.reshape(N_EXPERTS, 1), tri)


def _run_plan(count, tm, lr, tm_slots):
    n_tiles = count.shape[0]
    ch = CHUNK_ROWS
    n_slots = -(-(n_tiles * (tm + N_GROUPS * (ch - 1)) + N_GROUPS * tm_slots) // tm_slots) * tm_slots
    zero_chunk = lr // ch - 1
    pad = ((count + ch - 1) // ch) * ch
    lstart = jnp.cumsum(pad, axis=1) - pad
    run_off = jnp.cumsum(pad, axis=0) - pad
    seg_len = jnp.sum(pad, axis=0)
    seg_pad = ((seg_len + tm_slots - 1) // tm_slots) * tm_slots
    seg_end = jnp.cumsum(seg_pad)
    seg_start = seg_end - seg_pad

    s = (jnp.arange(n_slots // ch, dtype=jnp.int32) * ch)[:, None, None]
    in_run = (seg_start + run_off <= s) & (s < seg_start + run_off + pad)
    src_row = jnp.arange(n_tiles, dtype=jnp.int32)[None, :, None] * lr + lstart + (s - seg_start - run_off)
    src_row = jnp.sum(jnp.where(in_run, src_row, 0), axis=(1, 2))
    src_chunk = jnp.where(jnp.any(in_run, axis=(1, 2)), src_row // ch, zero_chunk).astype(jnp.int32)

    tile_start = jnp.arange(n_slots // tm_slots, dtype=jnp.int32) * tm_slots
    tile_group = jnp.minimum(jnp.sum((seg_end[None, :] <= tile_start[:, None]).astype(jnp.int32), axis=1),
                             N_GROUPS - 1)
    n_used = (seg_end[N_GROUPS - 1:] // tm_slots).astype(jnp.int32)

    r = (jnp.arange(lr // ch, dtype=jnp.int32) * ch)[None, :, None]
    in_local = (lstart[:, None, :] <= r) & (r < (lstart + pad)[:, None, :])
    slot = (seg_start + run_off - lstart)[:, None, :] + r
    back_chunk = (jnp.sum(jnp.where(in_local, slot, 0), axis=2) // ch).astype(jnp.int32)
    return src_chunk, tile_group.astype(jnp.int32), n_used, back_chunk.reshape(-1)


def _chunk_gather(idx_ref, tile, n_chunks, src_hbm, buf, sems, wait):
    slot = tile % 2
    rows = n_chunks * CHUNK_ROWS
    if wait:
        pltpu.make_async_copy(src_hbm.at[pl.ds(0, rows), :], buf.at[slot], sems.at[slot]).wait()
        return
    for c in range(n_chunks):
        src = pl.multiple_of(idx_ref[tile * n_chunks + c] * CHUNK_ROWS, CHUNK_ROWS)
        pltpu.make_async_copy(src_hbm.at[pl.ds(src, CHUNK_ROWS), :],
                              buf.at[slot, pl.ds(c * CHUNK_ROWS, CHUNK_ROWS), :], sems.at[slot]).start()


def _moe_runs_kernel(src_ref, tile_group_ref, n_used_ref, h_hbm, wg32_ref, wu32_ref, wd32_ref, wr_ref, rb_ref,
                     y_ref, hbuf, wg_ref, wu_ref, wd_ref, sems, *, tm):
    step = pl.program_id(0)
    n_used = n_used_ref[0]
    n_chunks = tm // CHUNK_ROWS
    new_group = (step == 0) | (tile_group_ref[step] != tile_group_ref[jnp.maximum(step - 1, 0)])

    @pl.when(new_group & (step < n_used))
    def _():
        wg_ref[...] = wg32_ref[...].astype(BF16)
        wu_ref[...] = wu32_ref[...].astype(BF16)
        wd_ref[...] = wd32_ref[...].astype(BF16)

    @pl.when((step == 0) & (n_used > 0))
    def _():
        _chunk_gather(src_ref, 0, n_chunks, h_hbm, hbuf, sems, wait=False)

    @pl.when(step + 1 < n_used)
    def _():
        _chunk_gather(src_ref, step + 1, n_chunks, h_hbm, hbuf, sems, wait=False)

    @pl.when(step < n_used)
    def _():
        _chunk_gather(src_ref, step, n_chunks, h_hbm, hbuf, sems, wait=True)
        h = hbuf[step % 2]
        s = jax.nn.sigmoid(_dot(h, wr_ref[0]))
        sb = s + rb_ref[0]
        epg = EXPERTS_PER_GROUP
        gates = _top2_gates([s[:, j:j + 1] for j in range(epg)], [sb[:, j:j + 1] for j in range(epg)])
        acc = None
        gate_up = [(_dot(h, wg_ref[0]), _dot(h, wu_ref[0]))]
        for e in range(epg):
            if e + 1 < epg:
                gate_up.append((_dot(h, wg_ref[e + 1]), _dot(h, wu_ref[e + 1])))
            gate, up = gate_up[e]
            hid = (gate * jax.nn.sigmoid(gate)) * up * gates[e]
            y = _dot(hid.astype(BF16), wd_ref[e])
            acc = y if acc is None else acc + y
        y_ref[...] = acc.astype(BF16)

    @pl.when(step >= n_used)
    def _():
        y_ref[...] = jnp.zeros_like(y_ref)


def _moe_runs(h_sorted, src_chunk, tile_group, n_used, layer, w_gate, w_up, w_down, w_router_grp,
              router_bias_grp, tm):
    n_slots = src_chunk.shape[0] * CHUNK_ROWS
    epg = EXPERTS_PER_GROUP
    by_group = lambda i, src_ref, tg_ref, nu_ref: (tg_ref[i], 0, 0)
    by_layer_group = lambda i, src_ref, tg_ref, nu_ref: (layer * N_GROUPS + tg_ref[i], 0, 0)
    grid_spec = pltpu.PrefetchScalarGridSpec(
        num_scalar_prefetch=3,
        grid=(n_slots // tm,),
        in_specs=[pl.BlockSpec(memory_space=pl.ANY),
                  pl.BlockSpec((epg, D_MODEL, D_EXPERT), by_layer_group),
                  pl.BlockSpec((epg, D_MODEL, D_EXPERT), by_layer_group),
                  pl.BlockSpec((epg, D_EXPERT, D_MODEL), by_layer_group),
                  pl.BlockSpec((1, D_MODEL, LANES), by_group),
                  pl.BlockSpec((1, 1, LANES), by_group)],
        out_specs=pl.BlockSpec((tm, D_MODEL), lambda i, src_ref, tg_ref, nu_ref: (i, 0)),
        scratch_shapes=[pltpu.VMEM((2, tm, D_MODEL), BF16),
                        pltpu.VMEM((epg, D_MODEL, D_EXPERT), BF16), pltpu.VMEM((epg, D_MODEL, D_EXPERT), BF16),
                        pltpu.VMEM((epg, D_EXPERT, D_MODEL), BF16), pltpu.SemaphoreType.DMA((2,))],
    )
    return pl.pallas_call(
        functools.partial(_moe_runs_kernel, tm=tm),
        grid_spec=grid_spec,
        out_shape=jax.ShapeDtypeStruct((n_slots, D_MODEL), BF16),
        compiler_params=_cparams(1),
        name="moe_group_experts",
    )(src_chunk, tile_group, n_used, h_sorted, w_gate, w_up, w_down, w_router_grp, router_bias_grp)


def _gather_unsort(back_ref, lslot_ref, y_hbm, ybuf, sems, step, n_steps, lr):
    per_step = ybuf.shape[1] // lr
    tm = lslot_ref.shape[2] // per_step
    n_chunks = per_step * lr // CHUNK_ROWS

    @pl.when(step == 0)
    def _():
        _chunk_gather(back_ref, 0, n_chunks, y_hbm, ybuf, sems, wait=False)

    @pl.when(step + 1 < n_steps)
    def _():
        _chunk_gather(back_ref, step + 1, n_chunks, y_hbm, ybuf, sems, wait=False)

    _chunk_gather(back_ref, step, n_chunks, y_hbm, ybuf, sems, wait=True)
    outs = []
    for t in range(per_step):
        sort = _sort_matrix(lslot_ref[0, 0:1, t * tm:(t + 1) * tm], lr)
        outs.append(lax.dot_general(sort, ybuf[step % 2, t * lr:(t + 1) * lr, :], (((0,), (0,)), ((), ())),
                                    preferred_element_type=F32))
    return outs


def _residual_norm_kernel(back_ref, y_hbm, lslot_ref, x1_ref, m_ref, g_ref, b_ref, o_ref, ybuf, sems, *, n_steps, lr):
    ys = _gather_unsort(back_ref, lslot_ref, y_hbm, ybuf, sems, pl.program_id(0), n_steps, lr)
    tm = x1_ref.shape[0] // len(ys)
    for t, y in enumerate(ys):
        rows = slice(t * tm, (t + 1) * tm)
        z = DEEPNORM_ALPHA * x1_ref[rows, :] + m_ref[0, 5:6, :] * y
        o_ref[rows, :] = _layer_norm(z) * g_ref[...] + b_ref[...]


def _residual_norm(y_slots, back_chunk, lslot, x1, m, ln_g, ln_b, tm):
    bsz, length, _ = x1.shape
    n = bsz * length
    tiles_per_batch = length // tm
    per_step = 2 if tiles_per_batch % 2 == 0 else 1
    steps_per_batch = tiles_per_batch // per_step
    ts = per_step * tm
    lr = _local_rows(tm)
    const2 = lambda i, back_ref: (0, 0)
    row = pl.BlockSpec((ts, D_MODEL), lambda i, back_ref: (i, 0))
    grid_spec = pltpu.PrefetchScalarGridSpec(
        num_scalar_prefetch=1,
        grid=(n // ts,),
        in_specs=[pl.BlockSpec(memory_space=pl.ANY),
                  pl.BlockSpec((1, SUBLANES, ts),
                               lambda i, back_ref: (i // steps_per_batch, 0, i % steps_per_batch)),
                  row, pl.BlockSpec((1, N_MOD, D_MODEL), lambda i, back_ref: (i // steps_per_batch, 0, 0)),
                  pl.BlockSpec((1, D_MODEL), const2), pl.BlockSpec((1, D_MODEL), const2)],
        out_specs=row,
        scratch_shapes=[pltpu.VMEM((2, per_step * lr, D_MODEL), BF16), pltpu.SemaphoreType.DMA((2,))],
    )
    out = pl.pallas_call(
        functools.partial(_residual_norm_kernel, n_steps=n // ts, lr=lr),
        grid_spec=grid_spec,
        out_shape=jax.ShapeDtypeStruct((n, D_MODEL), F32),
        compiler_params=_cparams(1),
        name="moe_residual_norm",
    )(back_chunk, y_slots, lslot, x1.reshape(n, D_MODEL), m, ln_g.reshape(1, D_MODEL), ln_b.reshape(1, D_MODEL))
    return out.reshape(bsz, length, D_MODEL)


def _grouped_moe_runs(h_sorted, count_rows, layer, w_gate, w_up, w_down, w_router_grp, router_bias_grp,
                      tm_tokens, tm_slots):
    count = count_rows[:, :N_GROUPS, 0].astype(jnp.int32)
    src_chunk, tile_group, n_used, back_chunk = _run_plan(count, tm_tokens, _local_rows(tm_tokens), tm_slots)
    y_slots = _moe_runs(h_sorted, src_chunk, tile_group, n_used, layer, w_gate, w_up, w_down, w_router_grp,
                        router_bias_grp, tm_slots)
    return y_slots, back_chunk


def _ctx_fourier_kernel(f_ref, w1_ref, c_ref, s_ref, y_ref):
    ab = _dot(f_ref[0].astype(BF16), w1_ref[...]).astype(BF16)
    y = _dot(c_ref[...], ab[:, :F_WIDTH]) + _dot(s_ref[...], ab[:, F_WIDTH:])
    y_ref[0] = y.astype(BF16)


def _context_fourier(fc):
    bsz, n, _ = fc.shape
    w1 = _bf16_table(_channel_dft_matrix())
    c, s = _dft_cos_sin(n)
    cm = _bf16_table(c * n ** -0.5)
    sm = _bf16_table(-s * n ** -0.5)
    const2 = lambda b: (0, 0)
    blk = pl.BlockSpec((1, n, F_WIDTH), lambda b: (b, 0, 0))
    return pl.pallas_call(
        _ctx_fourier_kernel,
        grid=(bsz,),
        in_specs=[blk, pl.BlockSpec(w1.shape, const2), pl.BlockSpec(cm.shape, const2),
                  pl.BlockSpec(sm.shape, const2)],
        out_specs=blk,
        out_shape=jax.ShapeDtypeStruct((bsz, n, F_WIDTH), BF16),
        compiler_params=_cparams(1),
        name="context_fnet",
    )(fc, w1, cm, sm)


def _block_diag(w):
    g, c, _ = w.shape
    eye = jnp.eye(g, dtype=w.dtype)
    return (eye[:, None, :, None] * w[:, :, None, :]).reshape(g * c, g * c)


def kernel(x, c, ctx, c_ctx, w_mod, b_mod, w_in, rpb, w_four, w_out, ln1_g, ln1_b, ln2_g, ln2_b,
           w_router, router_bias, w_gate, w_up, w_down):
    bsz, length, _ = x.shape
    n_ctx = ctx.shape[1]
    rows = length // GRID_W

    cvec = jnp.concatenate([c, c_ctx[None, :], jnp.zeros((8 - bsz - 1, D_MODEL), F32)], axis=0)
    mods = _modulation(cvec, w_mod, b_mod)
    w_router_t = w_router.T.astype(BF16)
    toeplitz = _rpb_toeplitz(rpb)
    lane_pad = LANES - EXPERTS_PER_GROUP
    w_router_grp = jnp.pad(w_router.reshape(D_MODEL, N_GROUPS, EXPERTS_PER_GROUP).transpose(1, 0, 2),
                           ((0, 0), (0, 0), (0, lane_pad))).astype(BF16)
    router_bias_grp = jnp.pad(router_bias.astype(F32).reshape(N_GROUPS, 1, EXPERTS_PER_GROUP),
                              ((0, 0), (0, 0), (0, lane_pad)))

    wg = w_gate.reshape(DEPTH * N_EXPERTS, D_MODEL, D_EXPERT)
    wu = w_up.reshape(DEPTH * N_EXPERTS, D_MODEL, D_EXPERT)
    wd = w_down.reshape(DEPTH * N_EXPERTS, D_EXPERT, D_MODEL)

    def latent_mod(i):
        return mods[i, :bsz].reshape(bsz, N_MOD, D_MODEL)

    w_out_b = w_out.astype(BF16)
    w_fourier_b = _fold_fourier_maps(w_four, w_out)

    xc = ctx
    projected = None
    for i in range(DEPTH):
        last = i == DEPTH - 1
        m = latent_mod(i)
        mc = jnp.broadcast_to(mods[i, bsz].reshape(1, N_MOD, D_MODEL), (bsz, N_MOD, D_MODEL))
        w_in_b = w_in[i].astype(BF16)

        q, k, v, f = projected if projected is not None else _in_projection(x, m, w_in_b, tm=1024)
        qc, kc, vc, fc = _in_projection(xc, mc, w_in_b, tm=n_ctx)

        attn = _neighborhood_attention(q, k, v, kc, vc, toeplitz, i)
        yf = _fourier_positions(f, n_slow=rows, n_fast=GRID_W)
        x1, h2, lslot, cnt = _out_projection(attn, yf, x, m, i, w_out_b, w_fourier_b, ln1_g[i], ln1_b[i],
                                             w_router_t, router_bias, tm=512)
        y, back = _grouped_moe_runs(h2, cnt, i, wg, wu, wd, w_router_grp, router_bias_grp,
                                    tm_tokens=512, tm_slots=512)
        if last:
            return _residual_norm(y, back, lslot, x1, m, ln2_g[i], ln2_b[i], tm=512)
        x, *projected = _norm_in_projection(y, back, lslot, x1, m, ln2_g[i], ln2_b[i], latent_mod(i + 1),
                                            w_in[i + 1].astype(BF16), tm=512)

        attn_c = _context_attention(qc, kc, vc)
        yc = _context_fourier(fc)
        xc1, h2c, lslot_c, cnt_c = _out_projection(attn_c, yc, xc, mc, i, w_out_b, w_fourier_b, ln1_g[i], ln1_b[i],
                                                   w_router_t, router_bias, tm=n_ctx)
        yc2, back_c = _grouped_moe_runs(h2c, cnt_c, i, wg, wu, wd, w_router_grp, router_bias_grp,
                                        tm_tokens=n_ctx, tm_slots=128)
        xc = _residual_norm(yc2, back_c, lslot_c, xc1, mc, ln2_g[i], ln2_b[i], tm=n_ctx)
    return x
```
